```python
import math
import jax, jax.numpy as jnp
from jax import lax
import numpy as np

D_MODEL = 2048
BATCH = 8
SEQ = 2048
DEPTH = 2

N_MIXERS = 2
N_A = (DEPTH + 1) // 2
N_B = DEPTH // 2
S5_WIDTH = D_MODEL
S5_GROUP = 16
S5_GROUPS = S5_WIDTH // S5_GROUP
S5_STATE = 64
S5_DT_MIN = 1e-3
S5_DT_MAX = 1e-1
LRU_WIDTH = ((4 * D_MODEL // 3 + 255) // 256) * 256
LRU_BLOCKS = 16
LRU_BLOCK = LRU_WIDTH // LRU_BLOCKS
LRU_C = 8.0
CONV_WIDTH = 4
FFN_HIDDEN = ((8 * D_MODEL // 3 + 255) // 256) * 256
N_MOD = 6
EPS = 1e-6

kernel_name = "adaln_hybrid_s5_rglru_swiglu"


def rmsnorm(x, gain):
    xf = x.astype(jnp.float32)
    y = xf * lax.rsqrt(jnp.mean(xf * xf, axis=-1, keepdims=True) + EPS) * gain.astype(jnp.float32)
    return y.astype(x.dtype)


def _complex_affine_combine(left, right):
    a1r, a1i, b1r, b1i = left
    a2r, a2i, b2r, b2i = right
    return (a2r * a1r - a2i * a1i,
            a2r * a1i + a2i * a1r,
            a2r * b1r - a2i * b1i + b2r,
            a2r * b1i + a2i * b1r + b2i)


def _real_affine_combine(left, right):
    a1, b1 = left
    a2, b2 = right
    return (a2 * a1, a2 * b1 + b2)


def s5_mixer(h, w_in, lam_re, lam_im, log_dt, b_re, b_im, c_re, c_im, d_skip, w_glu):
    f32 = jnp.float32
    bsz, seq, _ = h.shape
    u = h @ w_in
    uf = u.astype(f32)
    ug = uf.reshape(bsz, seq, S5_GROUPS, S5_GROUP)
    dt = jnp.exp(log_dt.astype(f32))[:, None]
    lr = lam_re.astype(f32)
    li = lam_im.astype(f32)
    mag = jnp.exp(lr * dt)
    ab_re = mag * jnp.cos(li * dt)
    ab_im = mag * jnp.sin(li * dt)
    nr, ni = ab_re - 1.0, ab_im
    den = lr * lr + li * li
    f_re = (nr * lr + ni * li) / den
    f_im = (ni * lr - nr * li) / den
    br, bi = b_re.astype(f32), b_im.astype(f32)
    bb_re = f_re[..., None] * br - f_im[..., None] * bi
    bb_im = f_re[..., None] * bi + f_im[..., None] * br
    bu_re = jnp.einsum('blgc,gpc->blgp', ug, bb_re)
    bu_im = jnp.einsum('blgc,gpc->blgp', ug, bb_im)
    a_re = jnp.broadcast_to(ab_re[None, None], (1, seq, S5_GROUPS, S5_STATE))
    a_im = jnp.broadcast_to(ab_im[None, None], (1, seq, S5_GROUPS, S5_STATE))
    _, _, s_re, s_im = lax.associative_scan(
        _complex_affine_combine, (a_re, a_im, bu_re, bu_im), axis=1)
    y = (jnp.einsum('blgp,gcp->blgc', s_re, c_re.astype(f32))
         - jnp.einsum('blgp,gcp->blgc', s_im, c_im.astype(f32)))
    y = y.reshape(bsz, seq, S5_WIDTH) + d_skip.astype(f32) * uf
    y = jax.nn.gelu(y).astype(h.dtype)
    val, gate = jnp.split(y @ w_glu, 2, axis=-1)
    return val * jax.nn.sigmoid(gate)


def causal_depthwise_conv(x, w, b):
    y = lax.conv_general_dilated(
        x, w.astype(x.dtype), window_strides=(1,), padding=[(CONV_WIDTH - 1, 0)],
        dimension_numbers=('NWC', 'WIO', 'NWC'), feature_group_count=x.shape[-1])
    return y + b.astype(x.dtype)


def rglru_mixer(h, w_in, conv_w, conv_b, w_rg, b_rg, w_ig, b_ig, lam, w_out):
    f32 = jnp.float32
    bsz, seq, _ = h.shape
    gate_branch, xb = jnp.split(h @ w_in, 2, axis=-1)
    xb = causal_depthwise_conv(xb, conv_w, conv_b).astype(f32)
    xblk = xb.reshape(bsz, seq, LRU_BLOCKS, LRU_BLOCK)
    r = jax.nn.sigmoid(jnp.einsum('blhi,hij->blhj', xblk, w_rg.astype(f32)).reshape(bsz, seq, LRU_WIDTH)
                       + b_rg.astype(f32))
    ig = jax.nn.sigmoid(jnp.einsum('blhi,hij->blhj', xblk, w_ig.astype(f32)).reshape(bsz, seq, LRU_WIDTH)
                        + b_ig.astype(f32))
    log_a = -LRU_C * r * jax.nn.softplus(-lam.astype(f32))
    a = jnp.exp(log_a)
    mult = jnp.sqrt(-jnp.expm1(2.0 * log_a))
    _, hs = lax.associative_scan(_real_affine_combine, (a, mult * (ig * xb)), axis=1)
    y = hs * jax.nn.gelu(gate_branch.astype(f32))
    return y.astype(h.dtype) @ w_out


def swiglu(h, w_gu, w_down):
    g, u = jnp.split(h @ w_gu, 2, axis=-1)
    return (jax.nn.silu(g) * u) @ w_down


def _fwd_setup_inputs(seed: int = 0) -> dict:
    key = jax.random.key(seed)
    ks = jax.random.split(key, 32)
    f32 = jnp.float32
    nrm = lambda k, shape, s: jax.random.normal(k, shape, f32) * s
    D = D_MODEL
    x = nrm(ks[0], (BATCH, SEQ, D), 1.0)
    c = nrm(ks[1], (BATCH, D), 1.0)
    norm_g = 1.0 + nrm(ks[2], (DEPTH, 2, D), 0.02)
    w_ada = nrm(ks[3], (DEPTH, D, N_MOD * D), 0.5 * D ** -0.5)
    b_ada = nrm(ks[4], (DEPTH, N_MOD * D), 0.02)
    s5_w_in = nrm(ks[5], (N_A, D, S5_WIDTH), D ** -0.5)
    n = jnp.arange(S5_STATE, dtype=f32)
    s5_lam_re = -0.5 + nrm(ks[6], (N_A, S5_GROUPS, S5_STATE), 0.01)
    s5_lam_im = math.pi * n + nrm(ks[7], (N_A, S5_GROUPS, S5_STATE), 0.01)
    s5_log_dt = jax.random.uniform(ks[8], (N_A, S5_GROUPS), f32,
                                   math.log(S5_DT_MIN), math.log(S5_DT_MAX))
    s5_b_re = nrm(ks[9], (N_A, S5_GROUPS, S5_STATE, S5_GROUP), (2 * S5_GROUP) ** -0.5)
    s5_b_im = nrm(ks[10], (N_A, S5_GROUPS, S5_STATE, S5_GROUP), (2 * S5_GROUP) ** -0.5)
    s5_c_re = nrm(ks[11], (N_A, S5_GROUPS, S5_GROUP, S5_STATE), (2 * S5_STATE) ** -0.5)
    s5_c_im = nrm(ks[12], (N_A, S5_GROUPS, S5_GROUP, S5_STATE), (2 * S5_STATE) ** -0.5)
    s5_d = nrm(ks[13], (N_A, S5_WIDTH), 1.0)
    s5_w_glu = nrm(ks[14], (N_A, S5_WIDTH, 2 * D), S5_WIDTH ** -0.5)
    lru_w_in = nrm(ks[15], (N_B, D, 2 * LRU_WIDTH), D ** -0.5)
    lru_conv_w = nrm(ks[16], (N_B, CONV_WIDTH, 1, LRU_WIDTH), CONV_WIDTH ** -0.5)
    lru_conv_b = nrm(ks[17], (N_B, LRU_WIDTH), 0.02)
    lru_w_rg = nrm(ks[18], (N_B, LRU_BLOCKS, LRU_BLOCK, LRU_BLOCK), LRU_BLOCK ** -0.5)
    lru_b_rg = nrm(ks[19], (N_B, LRU_WIDTH), 0.1)
    lru_w_ig = nrm(ks[20], (N_B, LRU_BLOCKS, LRU_BLOCK, LRU_BLOCK), LRU_BLOCK ** -0.5)
    lru_b_ig = nrm(ks[21], (N_B, LRU_WIDTH), 0.1)
    a_pow = jax.random.uniform(ks[22], (N_B, LRU_WIDTH), f32, 0.9, 0.999)
    a0 = a_pow ** (1.0 / LRU_C)
    lru_lam = jnp.log(a0) - jnp.log1p(-a0)
    lru_w_out = nrm(ks[23], (N_B, LRU_WIDTH, D), LRU_WIDTH ** -0.5)
    ffn_w_gu = nrm(ks[24], (DEPTH, D, 2 * FFN_HIDDEN), D ** -0.5)
    ffn_w_down = nrm(ks[25], (DEPTH, FFN_HIDDEN, D), FFN_HIDDEN ** -0.5)
    final_g = 1.0 + nrm(ks[26], (D,), 0.02)
    return {"x": x, "c": c, "norm_g": norm_g, "w_ada": w_ada, "b_ada": b_ada,
            "s5_w_in": s5_w_in, "s5_lam_re": s5_lam_re, "s5_lam_im": s5_lam_im,
            "s5_log_dt": s5_log_dt, "s5_b_re": s5_b_re, "s5_b_im": s5_b_im,
            "s5_c_re": s5_c_re, "s5_c_im": s5_c_im, "s5_d": s5_d, "s5_w_glu": s5_w_glu,
            "lru_w_in": lru_w_in, "lru_conv_w": lru_conv_w, "lru_conv_b": lru_conv_b,
            "lru_w_rg": lru_w_rg, "lru_b_rg": lru_b_rg, "lru_w_ig": lru_w_ig,
            "lru_b_ig": lru_b_ig, "lru_lam": lru_lam, "lru_w_out": lru_w_out,
            "ffn_w_gu": ffn_w_gu, "ffn_w_down": ffn_w_down, "final_g": final_g}


def _fwd_reference(x, c, norm_g, w_ada, b_ada,
              s5_w_in, s5_lam_re, s5_lam_im, s5_log_dt, s5_b_re, s5_b_im,
              s5_c_re, s5_c_im, s5_d, s5_w_glu,
              lru_w_in, lru_conv_w, lru_conv_b, lru_w_rg, lru_b_rg, lru_w_ig,
              lru_b_ig, lru_lam, lru_w_out,
              ffn_w_gu, ffn_w_down, final_g):
    cond = jax.nn.silu(c)
    for i in range(DEPTH):
        mod = cond @ w_ada[i] + b_ada[i]
        sh1, sc1, g1, sh2, sc2, g2 = [m[:, None, :] for m in jnp.split(mod, N_MOD, axis=-1)]
        h = rmsnorm(x, norm_g[i, 0]) * (1.0 + sc1) + sh1
        j = i // N_MIXERS
        if i % N_MIXERS == 0:
            y = s5_mixer(h, s5_w_in[j], s5_lam_re[j], s5_lam_im[j], s5_log_dt[j],
                         s5_b_re[j], s5_b_im[j], s5_c_re[j], s5_c_im[j], s5_d[j], s5_w_glu[j])
        else:
            y = rglru_mixer(h, lru_w_in[j], lru_conv_w[j], lru_conv_b[j], lru_w_rg[j],
                            lru_b_rg[j], lru_w_ig[j], lru_b_ig[j], lru_lam[j], lru_w_out[j])
        x = x + g1 * y
        h = rmsnorm(x, norm_g[i, 1]) * (1.0 + sc2) + sh2
        x = x + g2 * swiglu(h, ffn_w_gu[i], ffn_w_down[i])
    return rmsnorm(x, final_g)


import jax as _jax
import jax.numpy as _jnp

TWIN_FORMAT = 'train_step'
FWD_PARAMS = ['x', 'c', 'norm_g', 'w_ada', 'b_ada', 's5_w_in', 's5_lam_re', 's5_lam_im', 's5_log_dt', 's5_b_re', 's5_b_im', 's5_c_re', 's5_c_im', 's5_d', 's5_w_glu', 'lru_w_in', 'lru_conv_w', 'lru_conv_b', 'lru_w_rg', 'lru_b_rg', 'lru_w_ig', 'lru_b_ig', 'lru_lam', 'lru_w_out', 'ffn_w_gu', 'ffn_w_down', 'final_g']
TWIN_WEIGHTS = ['norm_g', 'w_ada', 'b_ada', 's5_w_in', 's5_lam_re', 's5_lam_im', 's5_log_dt', 's5_b_re', 's5_b_im', 's5_c_re', 's5_c_im', 's5_d', 's5_w_glu', 'lru_w_in', 'lru_conv_w', 'lru_conv_b', 'lru_w_rg', 'lru_b_rg', 'lru_w_ig', 'lru_b_ig', 'lru_lam', 'lru_w_out', 'ffn_w_gu', 'ffn_w_down', 'final_g']
TWIN_DIFF_INPUT = 'x'
TWIN_INPUTS = ['x', 'c', 'norm_g', 'w_ada', 'b_ada', 's5_w_in', 's5_lam_re', 's5_lam_im', 's5_log_dt', 's5_b_re', 's5_b_im', 's5_c_re', 's5_c_im', 's5_d', 's5_w_glu', 'lru_w_in', 'lru_conv_w', 'lru_conv_b', 'lru_w_rg', 'lru_b_rg', 'lru_w_ig', 'lru_b_ig', 'lru_lam', 'lru_w_out', 'ffn_w_gu', 'ffn_w_down', 'final_g', 'loss_target', 'm_norm_g', 'm_w_ada', 'm_b_ada', 'm_s5_w_in', 'm_s5_lam_re', 'm_s5_lam_im', 'm_s5_log_dt', 'm_s5_b_re', 'm_s5_b_im', 'm_s5_c_re', 'm_s5_c_im', 'm_s5_d', 'm_s5_w_glu', 'm_lru_w_in', 'm_lru_conv_w', 'm_lru_conv_b', 'm_lru_w_rg', 'm_lru_b_rg', 'm_lru_w_ig', 'm_lru_b_ig', 'm_lru_lam', 'm_lru_w_out', 'm_ffn_w_gu', 'm_ffn_w_down', 'm_final_g', 'v_norm_g', 'v_w_ada', 'v_b_ada', 'v_s5_w_in', 'v_s5_lam_re', 'v_s5_lam_im', 'v_s5_log_dt', 'v_s5_b_re', 'v_s5_b_im', 'v_s5_c_re', 'v_s5_c_im', 'v_s5_d', 'v_s5_w_glu', 'v_lru_w_in', 'v_lru_conv_w', 'v_lru_conv_b', 'v_lru_w_rg', 'v_lru_b_rg', 'v_lru_w_ig', 'v_lru_b_ig', 'v_lru_lam', 'v_lru_w_out', 'v_ffn_w_gu', 'v_ffn_w_down', 'v_final_g']
TWIN_OUTPUTS = ['loss', 'grad_x', 'grad_norm_g', 'grad_w_ada', 'grad_b_ada', 'grad_s5_w_in', 'grad_s5_lam_re', 'grad_s5_lam_im', 'grad_s5_log_dt', 'grad_s5_b_re', 'grad_s5_b_im', 'grad_s5_c_re', 'grad_s5_c_im', 'grad_s5_d', 'grad_s5_w_glu', 'grad_lru_w_in', 'grad_lru_conv_w', 'grad_lru_conv_b', 'grad_lru_w_rg', 'grad_lru_b_rg', 'grad_lru_w_ig', 'grad_lru_b_ig', 'grad_lru_lam', 'grad_lru_w_out', 'grad_ffn_w_gu', 'grad_ffn_w_down', 'grad_final_g', 'delta_norm_g', 'delta_w_ada', 'delta_b_ada', 'delta_s5_w_in', 'delta_s5_lam_re', 'delta_s5_lam_im', 'delta_s5_log_dt', 'delta_s5_b_re', 'delta_s5_b_im', 'delta_s5_c_re', 'delta_s5_c_im', 'delta_s5_d', 'delta_s5_w_glu', 'delta_lru_w_in', 'delta_lru_conv_w', 'delta_lru_conv_b', 'delta_lru_w_rg', 'delta_lru_b_rg', 'delta_lru_w_ig', 'delta_lru_b_ig', 'delta_lru_lam', 'delta_lru_w_out', 'delta_ffn_w_gu', 'delta_ffn_w_down', 'delta_final_g', 'new_m_norm_g', 'new_m_w_ada', 'new_m_b_ada', 'new_m_s5_w_in', 'new_m_s5_lam_re', 'new_m_s5_lam_im', 'new_m_s5_log_dt', 'new_m_s5_b_re', 'new_m_s5_b_im', 'new_m_s5_c_re', 'new_m_s5_c_im', 'new_m_s5_d', 'new_m_s5_w_glu', 'new_m_lru_w_in', 'new_m_lru_conv_w', 'new_m_lru_conv_b', 'new_m_lru_w_rg', 'new_m_lru_b_rg', 'new_m_lru_w_ig', 'new_m_lru_b_ig', 'new_m_lru_lam', 'new_m_lru_w_out', 'new_m_ffn_w_gu', 'new_m_ffn_w_down', 'new_m_final_g', 'new_v_norm_g', 'new_v_w_ada', 'new_v_b_ada', 'new_v_s5_w_in', 'new_v_s5_lam_re', 'new_v_s5_lam_im', 'new_v_s5_log_dt', 'new_v_s5_b_re', 'new_v_s5_b_im', 'new_v_s5_c_re', 'new_v_s5_c_im', 'new_v_s5_d', 'new_v_s5_w_glu', 'new_v_lru_w_in', 'new_v_lru_conv_w', 'new_v_lru_conv_b', 'new_v_lru_w_rg', 'new_v_lru_b_rg', 'new_v_lru_w_ig', 'new_v_lru_b_ig', 'new_v_lru_lam', 'new_v_lru_w_out', 'new_v_ffn_w_gu', 'new_v_ffn_w_down', 'new_v_final_g']
TWIN_LEAF_KINDS = {'loss': 'loss', 'grad_x': 'grad_x', 'grad_norm_g': 'grad_w', 'grad_w_ada': 'grad_w', 'grad_b_ada': 'grad_w', 'grad_s5_w_in': 'grad_w', 'grad_s5_lam_re': 'grad_w', 'grad_s5_lam_im': 'grad_w', 'grad_s5_log_dt': 'grad_w', 'grad_s5_b_re': 'grad_w', 'grad_s5_b_im': 'grad_w', 'grad_s5_c_re': 'grad_w', 'grad_s5_c_im': 'grad_w', 'grad_s5_d': 'grad_w', 'grad_s5_w_glu': 'grad_w', 'grad_lru_w_in': 'grad_w', 'grad_lru_conv_w': 'grad_w', 'grad_lru_conv_b': 'grad_w', 'grad_lru_w_rg': 'grad_w', 'grad_lru_b_rg': 'grad_w', 'grad_lru_w_ig': 'grad_w', 'grad_lru_b_ig': 'grad_w', 'grad_lru_lam': 'grad_w', 'grad_lru_w_out': 'grad_w', 'grad_ffn_w_gu': 'grad_w', 'grad_ffn_w_down': 'grad_w', 'grad_final_g': 'grad_w', 'delta_norm_g': 'delta_w', 'delta_w_ada': 'delta_w', 'delta_b_ada': 'delta_w', 'delta_s5_w_in': 'delta_w', 'delta_s5_lam_re': 'delta_w', 'delta_s5_lam_im': 'delta_w', 'delta_s5_log_dt': 'delta_w', 'delta_s5_b_re': 'delta_w', 'delta_s5_b_im': 'delta_w', 'delta_s5_c_re': 'delta_w', 'delta_s5_c_im': 'delta_w', 'delta_s5_d': 'delta_w', 'delta_s5_w_glu': 'delta_w', 'delta_lru_w_in': 'delta_w', 'delta_lru_conv_w': 'delta_w', 'delta_lru_conv_b': 'delta_w', 'delta_lru_w_rg': 'delta_w', 'delta_lru_b_rg': 'delta_w', 'delta_lru_w_ig': 'delta_w', 'delta_lru_b_ig': 'delta_w', 'delta_lru_lam': 'delta_w', 'delta_lru_w_out': 'delta_w', 'delta_ffn_w_gu': 'delta_w', 'delta_ffn_w_down': 'delta_w', 'delta_final_g': 'delta_w', 'new_m_norm_g': 'new_m', 'new_m_w_ada': 'new_m', 'new_m_b_ada': 'new_m', 'new_m_s5_w_in': 'new_m', 'new_m_s5_lam_re': 'new_m', 'new_m_s5_lam_im': 'new_m', 'new_m_s5_log_dt': 'new_m', 'new_m_s5_b_re': 'new_m', 'new_m_s5_b_im': 'new_m', 'new_m_s5_c_re': 'new_m', 'new_m_s5_c_im': 'new_m', 'new_m_s5_d': 'new_m', 'new_m_s5_w_glu': 'new_m', 'new_m_lru_w_in': 'new_m', 'new_m_lru_conv_w': 'new_m', 'new_m_lru_conv_b': 'new_m', 'new_m_lru_w_rg': 'new_m', 'new_m_lru_b_rg': 'new_m', 'new_m_lru_w_ig': 'new_m', 'new_m_lru_b_ig': 'new_m', 'new_m_lru_lam': 'new_m', 'new_m_lru_w_out': 'new_m', 'new_m_ffn_w_gu': 'new_m', 'new_m_ffn_w_down': 'new_m', 'new_m_final_g': 'new_m', 'new_v_norm_g': 'new_v', 'new_v_w_ada': 'new_v', 'new_v_b_ada': 'new_v', 'new_v_s5_w_in': 'new_v', 'new_v_s5_lam_re': 'new_v', 'new_v_s5_lam_im': 'new_v', 'new_v_s5_log_dt': 'new_v', 'new_v_s5_b_re': 'new_v', 'new_v_s5_b_im': 'new_v', 'new_v_s5_c_re': 'new_v', 'new_v_s5_c_im': 'new_v', 'new_v_s5_d': 'new_v', 'new_v_s5_w_glu': 'new_v', 'new_v_lru_w_in': 'new_v', 'new_v_lru_conv_w': 'new_v', 'new_v_lru_conv_b': 'new_v', 'new_v_lru_w_rg': 'new_v', 'new_v_lru_b_rg': 'new_v', 'new_v_lru_w_ig': 'new_v', 'new_v_lru_b_ig': 'new_v', 'new_v_lru_lam': 'new_v', 'new_v_lru_w_out': 'new_v', 'new_v_ffn_w_gu': 'new_v', 'new_v_ffn_w_down': 'new_v', 'new_v_final_g': 'new_v'}


def _forward(args):
    return _fwd_reference(*[args[k] for k in FWD_PARAMS])


def _output_shape():
    out = _jax.eval_shape(lambda: _forward(_fwd_setup_inputs(0)))
    return out.shape, out.dtype

N_MICROBATCH = 1
ADAM_LR = 0.001
ADAM_B1 = 0.9
ADAM_B2 = 0.999
ADAM_EPS = 1e-08
ADAM_WD = 0.01
ADAM_STEP = 10
PER_EXAMPLE_BATCH_AXIS = {'x': 0, 'c': 0, 'loss_target': 0}
SHARED_INPUTS = []
_WEIGHT_DTYPES = {'norm_g': _jnp.float32, 'w_ada': _jnp.float32, 'b_ada': _jnp.float32, 's5_w_in': _jnp.float32, 's5_lam_re': _jnp.float32, 's5_lam_im': _jnp.float32, 's5_log_dt': _jnp.float32, 's5_b_re': _jnp.float32, 's5_b_im': _jnp.float32, 's5_c_re': _jnp.float32, 's5_c_im': _jnp.float32, 's5_d': _jnp.float32, 's5_w_glu': _jnp.float32, 'lru_w_in': _jnp.float32, 'lru_conv_w': _jnp.float32, 'lru_conv_b': _jnp.float32, 'lru_w_rg': _jnp.float32, 'lru_b_rg': _jnp.float32, 'lru_w_ig': _jnp.float32, 'lru_b_ig': _jnp.float32, 'lru_lam': _jnp.float32, 'lru_w_out': _jnp.float32, 'ffn_w_gu': _jnp.float32, 'ffn_w_down': _jnp.float32, 'final_g': _jnp.float32}
MOMENT_SCALE = {'norm_g': 1.844419e-02, 'w_ada': 2.680589e-02, 'b_ada': 4.636102e-02, 's5_w_in': 8.591555e-03, 's5_lam_re': 5.569862e-04, 's5_lam_im': 6.705642e-04, 's5_log_dt': 2.892170e-01, 's5_b_re': 3.809095e-04, 's5_b_im': 3.619911e-04, 's5_c_re': 6.938008e-04, 's5_c_im': 6.873233e-04, 's5_d': 9.280327e-03, 's5_w_glu': 6.642573e-03, 'lru_w_in': 2.368851e-02, 'lru_conv_w': 2.523925e-02, 'lru_conv_b': 7.586587e-02, 'lru_w_rg': 2.444541e-03, 'lru_b_rg': 4.336705e-03, 'lru_w_ig': 4.852606e-03, 'lru_b_ig': 9.556701e-03, 'lru_lam': 1.106487e-02, 'lru_w_out': 2.717695e-02, 'ffn_w_gu': 7.996534e-03, 'ffn_w_down': 1.306826e-02, 'final_g': 8.034976e+00}


def _to_microbatches(a, axis):
    t = _jnp.moveaxis(a, axis, 0)
    t = t.reshape((N_MICROBATCH, t.shape[0] // N_MICROBATCH) + t.shape[1:])
    return _jnp.moveaxis(t, 1, axis + 1)


def setup_inputs(seed: int = 0) -> dict:
    inp = _fwd_setup_inputs(seed)
    key = _jax.random.fold_in(_jax.random.key(seed), 7919)
    shape, _ = _output_shape()
    out = dict(inp)
    out["loss_target"] = _jax.random.normal(_jax.random.fold_in(key, 0), shape, _jnp.float32)
    for i, name in enumerate(TWIN_WEIGHTS):
        w = inp[name].astype(_jnp.float32)
        if MOMENT_SCALE is None:
            s = _jnp.sqrt(_jnp.mean(_jnp.square(w)) + 1e-30)
        else:
            s = MOMENT_SCALE[name]
        km, kv = _jax.random.split(_jax.random.fold_in(key, i + 1))
        out[name] = w
        out["m_" + name] = s * _jax.random.normal(km, w.shape, _jnp.float32)
        out["v_" + name] = (s * s) * _jax.random.uniform(kv, w.shape, _jnp.float32, 0.5, 1.5)
    if N_MICROBATCH > 1:
        for name, axis in PER_EXAMPLE_BATCH_AXIS.items():
            out[name] = _to_microbatches(out[name], axis)
    return {'x': out['x'], 'c': out['c'], 'norm_g': out['norm_g'], 'w_ada': out['w_ada'], 'b_ada': out['b_ada'], 's5_w_in': out['s5_w_in'], 's5_lam_re': out['s5_lam_re'], 's5_lam_im': out['s5_lam_im'], 's5_log_dt': out['s5_log_dt'], 's5_b_re': out['s5_b_re'], 's5_b_im': out['s5_b_im'], 's5_c_re': out['s5_c_re'], 's5_c_im': out['s5_c_im'], 's5_d': out['s5_d'], 's5_w_glu': out['s5_w_glu'], 'lru_w_in': out['lru_w_in'], 'lru_conv_w': out['lru_conv_w'], 'lru_conv_b': out['lru_conv_b'], 'lru_w_rg': out['lru_w_rg'], 'lru_b_rg': out['lru_b_rg'], 'lru_w_ig': out['lru_w_ig'], 'lru_b_ig': out['lru_b_ig'], 'lru_lam': out['lru_lam'], 'lru_w_out': out['lru_w_out'], 'ffn_w_gu': out['ffn_w_gu'], 'ffn_w_down': out['ffn_w_down'], 'final_g': out['final_g'], 'loss_target': out['loss_target'], 'm_norm_g': out['m_norm_g'], 'm_w_ada': out['m_w_ada'], 'm_b_ada': out['m_b_ada'], 'm_s5_w_in': out['m_s5_w_in'], 'm_s5_lam_re': out['m_s5_lam_re'], 'm_s5_lam_im': out['m_s5_lam_im'], 'm_s5_log_dt': out['m_s5_log_dt'], 'm_s5_b_re': out['m_s5_b_re'], 'm_s5_b_im': out['m_s5_b_im'], 'm_s5_c_re': out['m_s5_c_re'], 'm_s5_c_im': out['m_s5_c_im'], 'm_s5_d': out['m_s5_d'], 'm_s5_w_glu': out['m_s5_w_glu'], 'm_lru_w_in': out['m_lru_w_in'], 'm_lru_conv_w': out['m_lru_conv_w'], 'm_lru_conv_b': out['m_lru_conv_b'], 'm_lru_w_rg': out['m_lru_w_rg'], 'm_lru_b_rg': out['m_lru_b_rg'], 'm_lru_w_ig': out['m_lru_w_ig'], 'm_lru_b_ig': out['m_lru_b_ig'], 'm_lru_lam': out['m_lru_lam'], 'm_lru_w_out': out['m_lru_w_out'], 'm_ffn_w_gu': out['m_ffn_w_gu'], 'm_ffn_w_down': out['m_ffn_w_down'], 'm_final_g': out['m_final_g'], 'v_norm_g': out['v_norm_g'], 'v_w_ada': out['v_w_ada'], 'v_b_ada': out['v_b_ada'], 'v_s5_w_in': out['v_s5_w_in'], 'v_s5_lam_re': out['v_s5_lam_re'], 'v_s5_lam_im': out['v_s5_lam_im'], 'v_s5_log_dt': out['v_s5_log_dt'], 'v_s5_b_re': out['v_s5_b_re'], 'v_s5_b_im': out['v_s5_b_im'], 'v_s5_c_re': out['v_s5_c_re'], 'v_s5_c_im': out['v_s5_c_im'], 'v_s5_d': out['v_s5_d'], 'v_s5_w_glu': out['v_s5_w_glu'], 'v_lru_w_in': out['v_lru_w_in'], 'v_lru_conv_w': out['v_lru_conv_w'], 'v_lru_conv_b': out['v_lru_conv_b'], 'v_lru_w_rg': out['v_lru_w_rg'], 'v_lru_b_rg': out['v_lru_b_rg'], 'v_lru_w_ig': out['v_lru_w_ig'], 'v_lru_b_ig': out['v_lru_b_ig'], 'v_lru_lam': out['v_lru_lam'], 'v_lru_w_out': out['v_lru_w_out'], 'v_ffn_w_gu': out['v_ffn_w_gu'], 'v_ffn_w_down': out['v_ffn_w_down'], 'v_final_g': out['v_final_g']}


def _loss(weights, diff, rest, loss_target):
    with _jax.named_scope("forward"):
        args = {**rest, TWIN_DIFF_INPUT: diff, **{k: w.astype(_WEIGHT_DTYPES[k]) for k, w in weights.items()}}
        y = _forward(args)
    with _jax.named_scope("loss_head"):
        err = _jnp.square(y.astype(_jnp.float32) - loss_target)
        return 0.5 * _jnp.sum(_jnp.mean(err, axis=-1)) if err.ndim else 0.5 * err


def _adamw(w, g, m, v):
    m = ADAM_B1 * m + (1.0 - ADAM_B1) * g
    v = ADAM_B2 * v + (1.0 - ADAM_B2) * _jnp.square(g)
    m_hat = m / (1.0 - ADAM_B1 ** ADAM_STEP)
    v_hat = v / (1.0 - ADAM_B2 ** ADAM_STEP)
    delta = -ADAM_LR * (m_hat / (_jnp.sqrt(v_hat) + ADAM_EPS) + ADAM_WD * w)
    return delta, m, v


def reference(x, c, norm_g, w_ada, b_ada, s5_w_in, s5_lam_re, s5_lam_im, s5_log_dt, s5_b_re, s5_b_im, s5_c_re, s5_c_im, s5_d, s5_w_glu, lru_w_in, lru_conv_w, lru_conv_b, lru_w_rg, lru_b_rg, lru_w_ig, lru_b_ig, lru_lam, lru_w_out, ffn_w_gu, ffn_w_down, final_g, loss_target, m_norm_g, m_w_ada, m_b_ada, m_s5_w_in, m_s5_lam_re, m_s5_lam_im, m_s5_log_dt, m_s5_b_re, m_s5_b_im, m_s5_c_re, m_s5_c_im, m_s5_d, m_s5_w_glu, m_lru_w_in, m_lru_conv_w, m_lru_conv_b, m_lru_w_rg, m_lru_b_rg, m_lru_w_ig, m_lru_b_ig, m_lru_lam, m_lru_w_out, m_ffn_w_gu, m_ffn_w_down, m_final_g, v_norm_g, v_w_ada, v_b_ada, v_s5_w_in, v_s5_lam_re, v_s5_lam_im, v_s5_log_dt, v_s5_b_re, v_s5_b_im, v_s5_c_re, v_s5_c_im, v_s5_d, v_s5_w_glu, v_lru_w_in, v_lru_conv_w, v_lru_conv_b, v_lru_w_rg, v_lru_b_rg, v_lru_w_ig, v_lru_b_ig, v_lru_lam, v_lru_w_out, v_ffn_w_gu, v_ffn_w_down, v_final_g):
    given = dict(x=x, c=c, norm_g=norm_g, w_ada=w_ada, b_ada=b_ada, s5_w_in=s5_w_in, s5_lam_re=s5_lam_re, s5_lam_im=s5_lam_im, s5_log_dt=s5_log_dt, s5_b_re=s5_b_re, s5_b_im=s5_b_im, s5_c_re=s5_c_re, s5_c_im=s5_c_im, s5_d=s5_d, s5_w_glu=s5_w_glu, lru_w_in=lru_w_in, lru_conv_w=lru_conv_w, lru_conv_b=lru_conv_b, lru_w_rg=lru_w_rg, lru_b_rg=lru_b_rg, lru_w_ig=lru_w_ig, lru_b_ig=lru_b_ig, lru_lam=lru_lam, lru_w_out=lru_w_out, ffn_w_gu=ffn_w_gu, ffn_w_down=ffn_w_down, final_g=final_g, loss_target=loss_target, m_norm_g=m_norm_g, m_w_ada=m_w_ada, m_b_ada=m_b_ada, m_s5_w_in=m_s5_w_in, m_s5_lam_re=m_s5_lam_re, m_s5_lam_im=m_s5_lam_im, m_s5_log_dt=m_s5_log_dt, m_s5_b_re=m_s5_b_re, m_s5_b_im=m_s5_b_im, m_s5_c_re=m_s5_c_re, m_s5_c_im=m_s5_c_im, m_s5_d=m_s5_d, m_s5_w_glu=m_s5_w_glu, m_lru_w_in=m_lru_w_in, m_lru_conv_w=m_lru_conv_w, m_lru_conv_b=m_lru_conv_b, m_lru_w_rg=m_lru_w_rg, m_lru_b_rg=m_lru_b_rg, m_lru_w_ig=m_lru_w_ig, m_lru_b_ig=m_lru_b_ig, m_lru_lam=m_lru_lam, m_lru_w_out=m_lru_w_out, m_ffn_w_gu=m_ffn_w_gu, m_ffn_w_down=m_ffn_w_down, m_final_g=m_final_g, v_norm_g=v_norm_g, v_w_ada=v_w_ada, v_b_ada=v_b_ada, v_s5_w_in=v_s5_w_in, v_s5_lam_re=v_s5_lam_re, v_s5_lam_im=v_s5_lam_im, v_s5_log_dt=v_s5_log_dt, v_s5_b_re=v_s5_b_re, v_s5_b_im=v_s5_b_im, v_s5_c_re=v_s5_c_re, v_s5_c_im=v_s5_c_im, v_s5_d=v_s5_d, v_s5_w_glu=v_s5_w_glu, v_lru_w_in=v_lru_w_in, v_lru_conv_w=v_lru_conv_w, v_lru_conv_b=v_lru_conv_b, v_lru_w_rg=v_lru_w_rg, v_lru_b_rg=v_lru_b_rg, v_lru_w_ig=v_lru_w_ig, v_lru_b_ig=v_lru_b_ig, v_lru_lam=v_lru_lam, v_lru_w_out=v_lru_w_out, v_ffn_w_gu=v_ffn_w_gu, v_ffn_w_down=v_ffn_w_down, v_final_g=v_final_g)
    weights = {n: given[n] for n in TWIN_WEIGHTS}
    shared = {n: given[n] for n in SHARED_INPUTS}
    per_example = {n: given[n] for n in ['x', 'c']}
    grad_fn = _jax.value_and_grad(_loss, argnums=(0, 1))

    def one_microbatch(ex, loss_target):
        ex = dict(ex)
        diff = ex.pop(TWIN_DIFF_INPUT)
        return grad_fn(weights, diff, {**shared, **ex}, loss_target)

    if N_MICROBATCH == 1:
        loss, (grad_w, grad_x) = one_microbatch(per_example, given["loss_target"])
    else:
        def body(carry, xs):
            loss_sum, grad_sum = carry
            l_k, (gw_k, gx_k) = one_microbatch(xs[0], xs[1])
            with _jax.named_scope("update"):
                return (loss_sum + l_k, _jax.tree.map(_jnp.add, grad_sum, gw_k)), gx_k

        init = (_jnp.zeros((), _jnp.float32), _jax.tree.map(_jnp.zeros_like, weights))
        (loss, grad_w), grad_x = _jax.lax.scan(body, init, (per_example, given["loss_target"]))
    with _jax.named_scope("update"):
        delta_w, new_m, new_v = {}, {}, {}
        for n in TWIN_WEIGHTS:
            delta_w[n], new_m[n], new_v[n] = _adamw(weights[n], grad_w[n], given["m_" + n], given["v_" + n])
    return (loss, grad_x, *[grad_w[n] for n in TWIN_WEIGHTS], *[delta_w[n] for n in TWIN_WEIGHTS],
            *[new_m[n] for n in TWIN_WEIGHTS], *[new_v[n] for n in TWIN_WEIGHTS])
```

```python
import functools
import math

import jax
import jax.numpy as jnp
from jax import lax
from jax.experimental import pallas as pl
from jax.experimental.pallas import tpu as pltpu

F32 = jnp.float32
BF16 = jnp.bfloat16
MESH = pl.DeviceIdType.MESH

EPS = 1e-6
LRU_C = 8.0
N_MOD = 6
ADAM_LR = 0.001
ADAM_B1 = 0.9
ADAM_B2 = 0.999
ADAM_EPS = 1e-08
ADAM_WD = 0.01
ADAM_STEP = 10

N_CHIPS = 4
N_DEV = 8
SUBLANES = 8
LANES = 128
S5_SB_GROUPS = 8
V7X_VMEM_LIMIT = 48 * 1024 * 1024
ROW_TILE_ELEMS = 512 * 1024

_TM = (1024, 512, 256, 128, 64, 32, 16, 8)
_TN = (1024, 1408, 512, 384, 256, 128)
_TK = (512, 1408, 256, 128)
_TR = (256, 128, 64, 32, 16, 8)

_GELU_K0 = math.sqrt(2.0 / math.pi)
_GELU_K1 = 0.044715


def _tile(n, cands):
    for t in cands:
        if n % t == 0:
            return t
    return n


def _cp(sem=None):
    return pltpu.CompilerParams(dimension_semantics=sem, vmem_limit_bytes=V7X_VMEM_LIMIT)


def _sds(shape, dtype):
    return jax.ShapeDtypeStruct(shape, dtype)


def _sig(x):
    return 1.0 / (1.0 + jnp.exp(-x))


def _gelu(x):
    t = jnp.tanh(_GELU_K0 * (x + _GELU_K1 * x * x * x))
    return 0.5 * x * (1.0 + t)


def _gelu_grad(x):
    x2 = x * x
    t = jnp.tanh(_GELU_K0 * (x + _GELU_K1 * x * x2))
    return 0.5 * (1.0 + t) + 0.5 * x * (1.0 - t * t) * _GELU_K0 * (1.0 + 3.0 * _GELU_K1 * x2)


def _softplus(z):
    return jnp.maximum(z, 0.0) + jnp.log(1.0 + jnp.exp(-jnp.abs(z)))


def _neg_expm1(x):
    series = -x * (1.0 + x * (0.5 + x * (1.0 / 6.0 + x * (1.0 / 24.0))))
    return jnp.where(x > -0.05, series, 1.0 - jnp.exp(x))


def _row(x, r):
    rows = lax.broadcasted_iota(jnp.int32, x.shape, 0)
    return jnp.sum(jnp.where(rows == r, x, 0.0), axis=0, keepdims=True)


def _colsum(x):
    return jnp.sum(x, axis=0, keepdims=True)


_NN = (((1,), (0,)), ((), ()))
_NT = (((1,), (1,)), ((), ()))
_TN_DIMS = (((0,), (0,)), ((), ()))


def _mm_call(name, a, b, a_spec, b_spec, o_spec, grid, out_shape, acc_shape, dims):
    nk = grid[-1]
    kaxis = len(grid) - 1

    def body(a_ref, b_ref, o_ref, acc_ref):
        k = pl.program_id(kaxis)

        @pl.when(k == 0)
        def _():
            acc_ref[...] = jnp.zeros_like(acc_ref)

        acc_ref[...] += lax.dot_general(a_ref[...].astype(BF16), b_ref[...].astype(BF16), dims,
                                        preferred_element_type=F32)

        @pl.when(k == nk - 1)
        def _():
            o_ref[...] = acc_ref[...].astype(o_ref.dtype)

    return pl.pallas_call(
        body, name=name, grid=grid, in_specs=[a_spec, b_spec], out_specs=o_spec, out_shape=out_shape,
        scratch_shapes=[pltpu.VMEM(acc_shape, F32)],
        compiler_params=_cp(("parallel", "parallel", "parallel", "arbitrary")),
    )(a, b)


def mm_nn(a, b, *, name, out_dtype=F32, bmode="plain"):
    M = a.shape[0]
    if bmode == "plain":
        G, S = 1, 1
        K, Nc = b.shape
    elif bmode == "cols":
        G = 1
        S, K, Nc = b.shape
    else:
        S = 1
        G, K, Nc = b.shape
    tm, tn, tk = _tile(M, _TM), _tile(Nc, _TN), _tile(K, _TK)
    nkb, nnb = K // tk, Nc // tn
    ncol = S * nnb
    grid = (G, M // tm, ncol, nkb)
    a_spec = pl.BlockSpec((tm, tk), lambda g, i, j, k: (i, g * nkb + k))
    if bmode == "plain":
        b_spec = pl.BlockSpec((tk, tn), lambda g, i, j, k: (k, j))
    elif bmode == "cols":
        b_spec = pl.BlockSpec((None, tk, tn), lambda g, i, j, k: (j // nnb, k, j % nnb))
    else:
        b_spec = pl.BlockSpec((None, tk, tn), lambda g, i, j, k: (g, k, j))
    o_spec = pl.BlockSpec((tm, tn), lambda g, i, j, k: (i, g * ncol + j))
    return _mm_call(name, a, b, a_spec, b_spec, o_spec, grid, _sds((M, G * S * Nc), out_dtype), (tm, tn), _NN)


def mm_nt(a, b, *, name, out_dtype=F32, bmode="plain"):
    M = a.shape[0]
    if bmode == "plain":
        G, S = 1, 1
        Ko, Nc = b.shape
    elif bmode == "cols":
        G = 1
        S, Ko, Nc = b.shape
    else:
        S = 1
        G, Ko, Nc = b.shape
    tm, to, tc = _tile(M, _TM), _tile(Ko, _TN), _tile(Nc, _TK)
    npc = Nc // tc
    nc = S * npc
    nob = Ko // to
    grid = (G, M // tm, nob, nc)
    a_spec = pl.BlockSpec((tm, tc), lambda g, i, j, n: (i, g * nc + n))
    if bmode == "plain":
        b_spec = pl.BlockSpec((to, tc), lambda g, i, j, n: (j, n))
    elif bmode == "cols":
        b_spec = pl.BlockSpec((None, to, tc), lambda g, i, j, n: (n // npc, j, n % npc))
    else:
        b_spec = pl.BlockSpec((None, to, tc), lambda g, i, j, n: (g, j, n))
    o_spec = pl.BlockSpec((tm, to), lambda g, i, j, n: (i, g * nob + j))
    return _mm_call(name, a, b, a_spec, b_spec, o_spec, grid, _sds((M, G * Ko), out_dtype), (tm, to), _NT)


def mm_tn(a, b, *, name, out_dtype=F32, omode="plain", groups=1):
    L = a.shape[0]
    G = groups if omode == "batch" else 1
    S = groups if omode == "cols" else 1
    Mo, N = a.shape[1] // G, b.shape[1] // G
    Nc = N // S
    tm, tn, tl = _tile(Mo, _TM), _tile(Nc, _TN), _tile(L, _TK)
    nmb, nnb = Mo // tm, N // tn
    npj = Nc // tn
    grid = (G, nmb, nnb, L // tl)
    a_spec = pl.BlockSpec((tl, tm), lambda g, i, j, l: (l, g * nmb + i))
    b_spec = pl.BlockSpec((tl, tn), lambda g, i, j, l: (l, g * nnb + j))
    if omode == "plain":
        o_spec = pl.BlockSpec((tm, tn), lambda g, i, j, l: (i, j))
        oshape = (Mo, N)
    elif omode == "cols":
        o_spec = pl.BlockSpec((None, tm, tn), lambda g, i, j, l: (j // npj, i, j % npj))
        oshape = (S, Mo, Nc)
    else:
        o_spec = pl.BlockSpec((None, tm, tn), lambda g, i, j, l: (g, i, j))
        oshape = (G, Mo, N)
    return _mm_call(name, a, b, a_spec, b_spec, o_spec, grid, _sds(oshape, out_dtype), (tm, tn), _TN_DIMS)


def _row_call(name, body, row_ins, vec_ins, row_outs, acc_outs=()):
    L = row_ins[0].shape[0]
    wmax = max([a.shape[1] for a in row_ins] + [w for w, _ in row_outs])
    tr = _tile(L, tuple(t for t in _TR if t * wmax <= ROW_TILE_ELEMS) or (SUBLANES,))
    in_specs = [pl.BlockSpec((tr, a.shape[1]), lambda i: (i, 0)) for a in row_ins]
    in_specs += [pl.BlockSpec(v.shape, lambda i, nd=v.ndim: (0,) * nd) for v in vec_ins]
    out_shape = [_sds((L, w), dt) for w, dt in row_outs] + [_sds(s, dt) for s, dt in acc_outs]
    out_specs = [pl.BlockSpec((tr, w), lambda i: (i, 0)) for w, _ in row_outs]
    out_specs += [pl.BlockSpec(s, lambda i, nd=len(s): (0,) * nd) for s, _ in acc_outs]
    sem = ("arbitrary",) if acc_outs else ("parallel",)
    return pl.pallas_call(body, name=name, grid=(L // tr,), in_specs=in_specs, out_specs=out_specs,
                          out_shape=out_shape, compiler_params=_cp(sem))(*row_ins, *vec_ins)


def silu_rows(x, name):
    def body(x_ref, o_ref):
        v = x_ref[...]
        o_ref[...] = (v * _sig(v)).astype(o_ref.dtype)
    return _row_call(name, body, [x], [], [(x.shape[1], BF16)])[0]


def norm_mod_fwd(x, gain, sc, sh, name):
    def body(x_ref, g_ref, sc_ref, sh_ref, h_ref):
        v = x_ref[...]
        r = lax.rsqrt(jnp.mean(v * v, axis=-1, keepdims=True) + EPS)
        h_ref[...] = (v * r * g_ref[...] * (1.0 + sc_ref[...]) + sh_ref[...]).astype(BF16)
    return _row_call(name, body, [x], [gain, sc, sh], [(x.shape[1], BF16)])[0]


def norm_mod_bwd(dh, x, gain, sc, dres, name):
    D = x.shape[1]

    def body(dh_ref, x_ref, dres_ref, g_ref, sc_ref, dx_ref, dg_ref, dsc_ref, dsh_ref):
        @pl.when(pl.program_id(0) == 0)
        def _():
            dg_ref[...] = jnp.zeros_like(dg_ref)
            dsc_ref[...] = jnp.zeros_like(dsc_ref)
            dsh_ref[...] = jnp.zeros_like(dsh_ref)

        v = x_ref[...]
        dh_v = dh_ref[...]
        g = g_ref[...]
        r = lax.rsqrt(jnp.mean(v * v, axis=-1, keepdims=True) + EPS)
        xhat = v * r
        dn = dh_v * (1.0 + sc_ref[...])
        dsc_ref[...] += _colsum(dh_v * xhat * g)
        dsh_ref[...] += _colsum(dh_v)
        dg_ref[...] += _colsum(dn * xhat)
        t = dn * g
        dx_ref[...] = dres_ref[...] + r * (t - xhat * jnp.mean(t * xhat, axis=-1, keepdims=True))

    acc = [((1, D), F32)] * 3
    return _row_call(name, body, [dh, x, dres], [gain, sc], [(D, F32)], acc)


def final_loss(x, gain, tgt, name):
    D = x.shape[1]

    def body(x_ref, t_ref, g_ref, dx_ref, loss_ref, dg_ref, acc_ref):
        i = pl.program_id(0)

        @pl.when(i == 0)
        def _():
            dg_ref[...] = jnp.zeros_like(dg_ref)
            acc_ref[...] = jnp.zeros_like(acc_ref)

        v = x_ref[...]
        g = g_ref[...]
        r = lax.rsqrt(jnp.mean(v * v, axis=-1, keepdims=True) + EPS)
        xhat = v * r
        err = xhat * g - t_ref[...]
        acc_ref[...] += _colsum(err * err)
        dout = err * (1.0 / D)
        dg_ref[...] += _colsum(dout * xhat)
        t = dout * g
        dx_ref[...] = r * (t - xhat * jnp.mean(t * xhat, axis=-1, keepdims=True))

        @pl.when(i == pl.num_programs(0) - 1)
        def _():
            loss_ref[...] = jnp.zeros_like(loss_ref) + jnp.sum(acc_ref[...]) * (0.5 / D)

    return _row_call(name, body, [x, tgt], [gain], [(D, F32)],
                     [((SUBLANES, LANES), F32), ((1, D), F32), ((1, D), F32)])[:3]


def res_gate_fwd(x, z, g, name):
    def body(x_ref, z_ref, g_ref, o_ref):
        o_ref[...] = x_ref[...] + g_ref[...] * z_ref[...]
    return _row_call(name, body, [x, z], [g], [(x.shape[1], F32)])[0]


def res_gate_bwd(dx, z, g, name):
    D = dx.shape[1]

    def body(dx_ref, z_ref, g_ref, dz_ref, dg_ref):
        @pl.when(pl.program_id(0) == 0)
        def _():
            dg_ref[...] = jnp.zeros_like(dg_ref)
        d = dx_ref[...]
        dz_ref[...] = (g_ref[...] * d).astype(BF16)
        dg_ref[...] += _colsum(d * z_ref[...])
    return _row_call(name, body, [dx, z], [g], [(D, BF16)], [((1, D), F32)])


def glu_res_fwd(x, v, g, name):
    D = x.shape[1]

    def body(x_ref, v_ref, g_ref, o_ref):
        vv = v_ref[...]
        o_ref[...] = x_ref[...] + g_ref[...] * (vv[:, :D] * _sig(vv[:, D:]))
    return _row_call(name, body, [x, v], [g], [(D, F32)])[0]


def glu_res_bwd(dx, v, g, name):
    D = dx.shape[1]

    def body(dx_ref, v_ref, g_ref, dv_ref, dg_ref):
        @pl.when(pl.program_id(0) == 0)
        def _():
            dg_ref[...] = jnp.zeros_like(dg_ref)
        d = dx_ref[...]
        vv = v_ref[...]
        val = vv[:, :D]
        s = _sig(vv[:, D:])
        dg_ref[...] += _colsum(d * val * s)
        dm = g_ref[...] * d
        dv_ref[:, :D] = (dm * s).astype(BF16)
        dv_ref[:, D:] = (dm * val * s * (1.0 - s)).astype(BF16)
    return _row_call(name, body, [dx, v], [g], [(2 * D, BF16)], [((1, D), F32)])


def swiglu_fwd(gu, name):
    F = gu.shape[1] // 2

    def body(gu_ref, o_ref):
        v = gu_ref[...].astype(F32)
        g = v[:, :F]
        o_ref[...] = (g * _sig(g) * v[:, F:]).astype(BF16)
    return _row_call(name, body, [gu], [], [(F, BF16)])[0]


def swiglu_bwd(dact, gu, name):
    F = gu.shape[1] // 2

    def body(da_ref, gu_ref, o_ref):
        v = gu_ref[...].astype(F32)
        g, u = v[:, :F], v[:, F:]
        da = da_ref[...]
        s = _sig(g)
        o_ref[:, :F] = (da * u * s * (1.0 + g * (1.0 - s))).astype(BF16)
        o_ref[:, F:] = (da * g * s).astype(BF16)
    return _row_call(name, body, [dact, gu], [], [(2 * F, BF16)])[0]


def adamw(w, g, m, v, name):
    C = w.shape[1]
    c1 = 1.0 - ADAM_B1 ** ADAM_STEP
    c2 = 1.0 - ADAM_B2 ** ADAM_STEP

    def body(w_ref, g_ref, m_ref, v_ref, d_ref, m2_ref, v2_ref):
        gv = g_ref[...]
        m2 = ADAM_B1 * m_ref[...] + (1.0 - ADAM_B1) * gv
        v2 = ADAM_B2 * v_ref[...] + (1.0 - ADAM_B2) * (gv * gv)
        m2_ref[...] = m2
        v2_ref[...] = v2
        d_ref[...] = -ADAM_LR * ((m2 / c1) / (jnp.sqrt(v2 / c2) + ADAM_EPS) + ADAM_WD * w_ref[...])
    return _row_call(name, body, [w, g, m, v], [], [(C, F32)] * 3)


def sum_devices(parts, name):
    n, R, C = parts.shape
    tr = _tile(R, tuple(t for t in _TR if t * C * n <= 4 * ROW_TILE_ELEMS) or (SUBLANES,))

    def body(p_ref, o_ref):
        acc = p_ref[0]
        for d in range(1, n):
            acc = acc + p_ref[d]
        o_ref[...] = acc
    return pl.pallas_call(body, name=name, grid=(R // tr,),
                          in_specs=[pl.BlockSpec((n, tr, C), lambda i: (0, i, 0))],
                          out_specs=pl.BlockSpec((tr, C), lambda i: (i, 0)), out_shape=_sds((R, C), F32),
                          compiler_params=_cp(("parallel",)))(parts)


def _s5_discretize(lam_re, lam_im, log_dt, b_re, b_im):
    dt = jnp.exp(log_dt)[:, None]
    mag = jnp.exp(lam_re * dt)
    ab_re = mag * jnp.cos(lam_im * dt)
    ab_im = mag * jnp.sin(lam_im * dt)
    nr, ni = ab_re - 1.0, ab_im
    den = lam_re * lam_re + lam_im * lam_im
    f_re = (nr * lam_re + ni * lam_im) / den
    f_im = (ni * lam_re - nr * lam_im) / den
    bb_re = f_re[..., None] * b_re - f_im[..., None] * b_im
    bb_im = f_re[..., None] * b_im + f_im[..., None] * b_re
    return ab_re, ab_im, bb_re, bb_im


def _s5_blockdiag(bb_re, bb_im, c_re, c_im):
    G, P, Cg = bb_re.shape
    nsb = G // S5_SB_GROUPS
    eye = jnp.eye(S5_SB_GROUPS, dtype=F32)

    def bmat(bb):
        return jnp.einsum("jgpc,gh->jgchp", bb.reshape(nsb, S5_SB_GROUPS, P, Cg), eye).reshape(
            nsb, S5_SB_GROUPS * Cg, S5_SB_GROUPS * P)

    def cmat(cc):
        return jnp.einsum("jgcp,gh->jgphc", cc.reshape(nsb, S5_SB_GROUPS, Cg, P), eye).reshape(
            nsb, S5_SB_GROUPS * P, S5_SB_GROUPS * Cg)

    bsb = jnp.concatenate([bmat(bb_re), bmat(bb_im)], axis=-1)
    csb = jnp.concatenate([cmat(c_re), -cmat(c_im)], axis=1)
    return bsb, csb


def _s5_blockdiag_grads(dbsb, dcsb, P, Cg):
    nsb = dbsb.shape[0]
    eye = jnp.eye(S5_SB_GROUPS, dtype=F32)
    db6 = dbsb.reshape(nsb, S5_SB_GROUPS, Cg, 2, S5_SB_GROUPS, P)
    dbb_re = jnp.einsum("jgchp,gh->jgpc", db6[:, :, :, 0], eye).reshape(-1, P, Cg)
    dbb_im = jnp.einsum("jgchp,gh->jgpc", db6[:, :, :, 1], eye).reshape(-1, P, Cg)
    dc6 = dcsb.reshape(nsb, 2, S5_SB_GROUPS, P, S5_SB_GROUPS, Cg)
    dc_re = jnp.einsum("jgphc,gh->jgcp", dc6[:, 0], eye).reshape(-1, Cg, P)
    dc_im = -jnp.einsum("jgphc,gh->jgcp", dc6[:, 1], eye).reshape(-1, Cg, P)
    return dbb_re, dbb_im, dc_re, dc_im


def _s5_scan_consts(ab_re, ab_im):
    G, P = ab_re.shape
    nsb = G // S5_SB_GROUPS
    H = S5_SB_GROUPS * P
    ar, ai = ab_re.reshape(nsb, 1, H), ab_im.reshape(nsb, 1, H)
    pows = [(ar, ai)]
    for _ in range(SUBLANES - 1):
        pr, pi_ = pows[-1]
        pows.append((pr * ar - pi_ * ai, pr * ai + pi_ * ar))
    rows = jnp.arange(SUBLANES).reshape(1, SUBLANES, 1)

    def masked(k, keep):
        pr, pi_ = pows[k - 1]
        return jnp.where(keep, pr, 0.0), jnp.where(keep, pi_, 0.0)

    def per_row(sel):
        pr = jnp.concatenate([pows[sel(r) - 1][0] for r in range(SUBLANES)], axis=1)
        pi_ = jnp.concatenate([pows[sel(r) - 1][1] for r in range(SUBLANES)], axis=1)
        return pr, pi_

    fwd = [masked(1, rows >= 1), masked(2, rows >= 2), masked(4, rows >= 4), per_row(lambda r: r + 1)]
    rev = [masked(1, rows < 7), masked(2, rows < 6), masked(4, rows < 4), per_row(lambda r: SUBLANES - r)]

    def pack(lst, conj):
        sgn = -1.0 if conj else 1.0
        return jnp.stack([jnp.concatenate([jnp.broadcast_to(pr, (nsb, SUBLANES, H)),
                                           sgn * jnp.broadcast_to(pi_, (nsb, SUBLANES, H))], axis=-1)
                          for pr, pi_ in lst], axis=1)

    return pack(fwd, False), pack(rev, True)


def _cmadd(xr, xi, ar, ai, yr, yi):
    return xr + ar * yr - ai * yi, xi + ar * yi + ai * yr


def _s5_scan_fwd_loop(src_ref, dst_ref, sp_ref, cf_ref, cr, ci, nblk, H):
    rows = lax.broadcasted_iota(jnp.int32, (SUBLANES, H), 0)

    def body(k, carry):
        cr, ci = carry
        r0 = pl.multiple_of(k * SUBLANES, SUBLANES)
        xr = src_ref[pl.ds(r0, SUBLANES), pl.ds(0, H)]
        xi = src_ref[pl.ds(r0, SUBLANES), pl.ds(H, H)]
        for idx, d in enumerate((1, 2, 4)):
            xr, xi = _cmadd(xr, xi, cf_ref[idx, :, pl.ds(0, H)], cf_ref[idx, :, pl.ds(H, H)],
                            pltpu.roll(xr, d, 0), pltpu.roll(xi, d, 0))
        xr, xi = _cmadd(xr, xi, cf_ref[3, :, pl.ds(0, H)], cf_ref[3, :, pl.ds(H, H)], cr, ci)
        dst_ref[pl.ds(r0, SUBLANES), pl.ds(0, H)] = xr
        dst_ref[pl.ds(r0, SUBLANES), pl.ds(H, H)] = xi
        if sp_ref is not None:
            sp_ref[pl.ds(r0, SUBLANES), pl.ds(0, H)] = jnp.where(rows == 0, cr, pltpu.roll(xr, 1, 0))
            sp_ref[pl.ds(r0, SUBLANES), pl.ds(H, H)] = jnp.where(rows == 0, ci, pltpu.roll(xi, 1, 0))
        return _row(xr, SUBLANES - 1), _row(xi, SUBLANES - 1)

    return lax.fori_loop(0, nblk, body, (cr, ci))


def s5_scan_fwd(u, d_skip, bsb, csb, cf, name):
    L, W = u.shape
    nsb, GW, H2 = bsb.shape
    H = H2 // 2
    Tc = _tile(L, (512, 256, 128, 64, 32, 16, 8))
    nch = L // Tc

    def body(u_ref, d_ref, b_ref, c_ref, cf_ref, ypre_ref, yg_ref, ss_ref, bu_scr, car_scr):
        @pl.when(pl.program_id(1) == 0)
        def _():
            car_scr[...] = jnp.zeros_like(car_scr)

        ss_ref[...] = car_scr[...]
        ub = u_ref[...]
        bu_scr[...] = jnp.dot(ub.astype(BF16), b_ref[...], preferred_element_type=F32)
        cr, ci = _s5_scan_fwd_loop(bu_scr, bu_scr, None, cf_ref, car_scr[:, pl.ds(0, H)], car_scr[:, pl.ds(H, H)],
                                   Tc // SUBLANES, H)
        car_scr[:, pl.ds(0, H)] = cr
        car_scr[:, pl.ds(H, H)] = ci
        ypre = jnp.dot(bu_scr[...].astype(BF16), c_ref[...], preferred_element_type=F32) + d_ref[...] * ub
        ypre_ref[...] = ypre
        yg_ref[...] = _gelu(ypre).astype(BF16)

    return pl.pallas_call(
        body, name=name, grid=(nsb, nch),
        in_specs=[pl.BlockSpec((Tc, GW), lambda j, i: (i, j)),
                  pl.BlockSpec((1, GW), lambda j, i: (0, j)),
                  pl.BlockSpec((None, GW, H2), lambda j, i: (j, 0, 0)),
                  pl.BlockSpec((None, H2, GW), lambda j, i: (j, 0, 0)),
                  pl.BlockSpec((None, 4, SUBLANES, H2), lambda j, i: (j, 0, 0, 0))],
        out_specs=[pl.BlockSpec((Tc, GW), lambda j, i: (i, j)),
                   pl.BlockSpec((Tc, GW), lambda j, i: (i, j)),
                   pl.BlockSpec((None, None, 1, H2), lambda j, i: (i, j, 0, 0))],
        out_shape=[_sds((L, W), F32), _sds((L, W), BF16), _sds((nch, nsb, 1, H2), F32)],
        scratch_shapes=[pltpu.VMEM((Tc, H2), F32), pltpu.VMEM((1, H2), F32)],
        compiler_params=_cp(("arbitrary", "arbitrary")),
    )(u, d_skip, bsb.astype(BF16), csb.astype(BF16), cf)


def s5_scan_bwd(u, dyg, ypre, d_skip, bsb, csb, cf, crv, ss, name):
    L, W = u.shape
    nsb, GW, H2 = bsb.shape
    H = H2 // 2
    Tc = _tile(L, (512, 256, 128, 64, 32, 16, 8))
    nch = L // Tc
    nblk = Tc // SUBLANES
    bsb_t = jnp.swapaxes(bsb, 1, 2).astype(BF16)
    csb_t = jnp.swapaxes(csb, 1, 2).astype(BF16)

    def body(u_ref, dyg_ref, yp_ref, d_ref, b_ref, bt_ref, ct_ref, cf_ref, crv_ref, ss_ref,
             du_ref, db_ref, dc_ref, da_ref, dd_ref, s_scr, sp_scr, g_scr, gcar_scr):
        @pl.when(pl.program_id(1) == 0)
        def _():
            gcar_scr[...] = jnp.zeros_like(gcar_scr)
            db_ref[...] = jnp.zeros_like(db_ref)
            dc_ref[...] = jnp.zeros_like(dc_ref)
            da_ref[...] = jnp.zeros_like(da_ref)
            dd_ref[...] = jnp.zeros_like(dd_ref)

        ub = u_ref[...]
        ubf = ub.astype(BF16)
        dyp = dyg_ref[...] * _gelu_grad(yp_ref[...])
        dypb = dyp.astype(BF16)
        dd_ref[...] += _colsum(dyp * ub)
        s_scr[...] = jnp.dot(ubf, b_ref[...], preferred_element_type=F32)
        _s5_scan_fwd_loop(s_scr, s_scr, sp_scr, cf_ref, ss_ref[:, pl.ds(0, H)], ss_ref[:, pl.ds(H, H)], nblk, H)
        g_scr[...] = jnp.dot(dypb, ct_ref[...], preferred_element_type=F32)

        def rev(kk, carry):
            gr, gi, acc_r, acc_i = carry
            r0 = pl.multiple_of((nblk - 1 - kk) * SUBLANES, SUBLANES)
            xr = g_scr[pl.ds(r0, SUBLANES), pl.ds(0, H)]
            xi = g_scr[pl.ds(r0, SUBLANES), pl.ds(H, H)]
            for idx, d in enumerate((1, 2, 4)):
                xr, xi = _cmadd(xr, xi, crv_ref[idx, :, pl.ds(0, H)], crv_ref[idx, :, pl.ds(H, H)],
                                pltpu.roll(xr, SUBLANES - d, 0), pltpu.roll(xi, SUBLANES - d, 0))
            xr, xi = _cmadd(xr, xi, crv_ref[3, :, pl.ds(0, H)], crv_ref[3, :, pl.ds(H, H)], gr, gi)
            g_scr[pl.ds(r0, SUBLANES), pl.ds(0, H)] = xr
            g_scr[pl.ds(r0, SUBLANES), pl.ds(H, H)] = xi
            spr = sp_scr[pl.ds(r0, SUBLANES), pl.ds(0, H)]
            spi = sp_scr[pl.ds(r0, SUBLANES), pl.ds(H, H)]
            return (_row(xr, 0), _row(xi, 0), acc_r + xr * spr + xi * spi, acc_i + xi * spr - xr * spi)

        zero = jnp.zeros((SUBLANES, H), F32)
        gr, gi, acc_r, acc_i = lax.fori_loop(
            0, nblk, rev, (gcar_scr[:, pl.ds(0, H)], gcar_scr[:, pl.ds(H, H)], zero, zero))
        gcar_scr[:, pl.ds(0, H)] = gr
        gcar_scr[:, pl.ds(H, H)] = gi
        da_ref[:, pl.ds(0, H)] += _colsum(acc_r)
        da_ref[:, pl.ds(H, H)] += _colsum(acc_i)
        gb = g_scr[...].astype(BF16)
        db_ref[...] += lax.dot_general(ubf, gb, _TN_DIMS, preferred_element_type=F32)
        dc_ref[...] += lax.dot_general(s_scr[...].astype(BF16), dypb, _TN_DIMS, preferred_element_type=F32)
        du_ref[...] = (jnp.dot(gb, bt_ref[...], preferred_element_type=F32) + d_ref[...] * dyp).astype(BF16)

    rmap = lambda j, i: (nch - 1 - i, j)
    return pl.pallas_call(
        body, name=name, grid=(nsb, nch),
        in_specs=[pl.BlockSpec((Tc, GW), rmap), pl.BlockSpec((Tc, GW), rmap), pl.BlockSpec((Tc, GW), rmap),
                  pl.BlockSpec((1, GW), lambda j, i: (0, j)),
                  pl.BlockSpec((None, GW, H2), lambda j, i: (j, 0, 0)),
                  pl.BlockSpec((None, H2, GW), lambda j, i: (j, 0, 0)),
                  pl.BlockSpec((None, GW, H2), lambda j, i: (j, 0, 0)),
                  pl.BlockSpec((None, 4, SUBLANES, H2), lambda j, i: (j, 0, 0, 0)),
                  pl.BlockSpec((None, 4, SUBLANES, H2), lambda j, i: (j, 0, 0, 0)),
                  pl.BlockSpec((None, None, 1, H2), lambda j, i: (nch - 1 - i, j, 0, 0))],
        out_specs=[pl.BlockSpec((Tc, GW), rmap),
                   pl.BlockSpec((None, GW, H2), lambda j, i: (j, 0, 0)),
                   pl.BlockSpec((None, H2, GW), lambda j, i: (j, 0, 0)),
                   pl.BlockSpec((None, 1, H2), lambda j, i: (j, 0, 0)),
                   pl.BlockSpec((1, GW), lambda j, i: (0, j))],
        out_shape=[_sds((L, W), BF16), _sds((nsb, GW, H2), F32), _sds((nsb, H2, GW), F32),
                   _sds((nsb, 1, H2), F32), _sds((1, W), F32)],
        scratch_shapes=[pltpu.VMEM((Tc, H2), F32), pltpu.VMEM((Tc, H2), F32), pltpu.VMEM((Tc, H2), F32),
                        pltpu.VMEM((1, H2), F32)],
        compiler_params=_cp(("arbitrary", "arbitrary")),
    )(u, dyg, ypre, d_skip, bsb.astype(BF16), bsb_t, csb_t, cf, crv, ss)


def _lru_blockdiag(w_rg, w_ig):
    nb, bs, _ = w_rg.shape
    sbw = bs * LANES // math.gcd(bs, LANES)
    bps = sbw // bs
    nsb = nb // bps
    eye = jnp.eye(bps, dtype=F32)

    def bd(w):
        return jnp.einsum("jgik,gh->jgihk", w.reshape(nsb, bps, bs, bs), eye).reshape(nsb, sbw, sbw)

    return bd(w_rg), bd(w_ig)


def _lru_blockdiag_grad(dwsb, nb, bs):
    nsb, sbw, _ = dwsb.shape
    bps = sbw // bs
    eye = jnp.eye(bps, dtype=F32)
    return jnp.einsum("jgihk,gh->jgik", dwsb.reshape(nsb, bps, bs, bps, bs), eye).reshape(nb, bs, bs)


def lru_conv_fwd(p, conv_w, conv_b, name):
    L = p.shape[0]
    E = conv_w.shape[1]
    tc = _tile(E, (256, 128))
    noff = E // tc
    kw = conv_w.shape[0]

    def body(xb_ref, w_ref, b_ref, xc_ref, xcb_ref):
        xb = xb_ref[...]
        rows = lax.broadcasted_iota(jnp.int32, xb.shape, 0)
        acc = w_ref[pl.ds(kw - 1, 1), :] * xb + b_ref[...]
        for k in range(kw - 1):
            sh = kw - 1 - k
            acc = acc + w_ref[pl.ds(k, 1), :] * jnp.where(rows >= sh, pltpu.roll(xb, sh, 0), 0.0)
        xc_ref[...] = acc
        xcb_ref[...] = acc.astype(BF16)

    return pl.pallas_call(
        body, name=name, grid=(noff,),
        in_specs=[pl.BlockSpec((L, tc), lambda t: (0, noff + t)),
                  pl.BlockSpec((kw, tc), lambda t: (0, t)), pl.BlockSpec((1, tc), lambda t: (0, t))],
        out_specs=[pl.BlockSpec((L, tc), lambda t: (0, t))] * 2,
        out_shape=[_sds((L, E), F32), _sds((L, E), BF16)],
        compiler_params=_cp(("parallel",)),
    )(p, conv_w, conv_b)


def lru_conv_bwd(d1, d2, d3, p, conv_w, name):
    L = p.shape[0]
    E = conv_w.shape[1]
    tc = _tile(E, (256, 128))
    noff = E // tc
    kw = conv_w.shape[0]

    def body(d1_ref, d2_ref, d3_ref, xb_ref, w_ref, dxb_ref, dw_ref, db_ref):
        dxc = d1_ref[...] + d2_ref[...] + d3_ref[...]
        xb = xb_ref[...]
        rows = lax.broadcasted_iota(jnp.int32, xb.shape, 0)
        db_ref[...] = _colsum(dxc)
        acc = w_ref[pl.ds(kw - 1, 1), :] * dxc
        dw_ref[pl.ds(kw - 1, 1), :] = _colsum(dxc * xb)
        for k in range(kw - 1):
            sh = kw - 1 - k
            dw_ref[pl.ds(k, 1), :] = _colsum(dxc * jnp.where(rows >= sh, pltpu.roll(xb, sh, 0), 0.0))
            acc = acc + w_ref[pl.ds(k, 1), :] * jnp.where(rows < L - sh, pltpu.roll(dxc, L - sh, 0), 0.0)
        dxb_ref[...] = acc.astype(BF16)

    return pl.pallas_call(
        body, name=name, grid=(noff,),
        in_specs=[pl.BlockSpec((L, tc), lambda t: (0, t))] * 3 +
                 [pl.BlockSpec((L, tc), lambda t: (0, noff + t)), pl.BlockSpec((kw, tc), lambda t: (0, t))],
        out_specs=[pl.BlockSpec((L, tc), lambda t: (0, t)), pl.BlockSpec((kw, tc), lambda t: (0, t)),
                   pl.BlockSpec((1, tc), lambda t: (0, t))],
        out_shape=[_sds((L, E), BF16), _sds((kw, E), F32), _sds((1, E), F32)],
        compiler_params=_cp(("parallel",)),
    )(d1, d2, d3, p, conv_w)


def _lru_gates(pr, pi_, brg, big, sp):
    r = _sig(pr + brg)
    ig = _sig(pi_ + big)
    la = -LRU_C * r * sp
    a = jnp.exp(la)
    mult = jnp.sqrt(_neg_expm1(2.0 * la))
    return r, ig, a, mult


def _lru_specs(L):
    tc = LANES
    col = pl.BlockSpec((L, tc), lambda t: (0, t))
    vec = pl.BlockSpec((1, tc), lambda t: (0, t))
    return tc, col, vec


def lru_scan_fwd(pre_r, pre_i, xc, p, b_rg, b_ig, lam, name):
    L, E = xc.shape
    tc, col, vec = _lru_specs(L)
    nblk = L // SUBLANES

    def body(pr_ref, pi_ref, xc_ref, gb_ref, brg_ref, big_ref, lam_ref, hs_ref, yv_ref):
        sp = _softplus(-lam_ref[...])
        brg, big = brg_ref[...], big_ref[...]
        rows = lax.broadcasted_iota(jnp.int32, (SUBLANES, tc), 0)

        def blk(k, carry):
            r0 = pl.multiple_of(k * SUBLANES, SUBLANES)
            sl = pl.ds(r0, SUBLANES)
            _, ig, a, mult = _lru_gates(pr_ref[sl, :], pi_ref[sl, :], brg, big, sp)
            b = mult * ig * xc_ref[sl, :]
            for d in (1, 2, 4):
                keep = rows >= d
                b = b + a * jnp.where(keep, pltpu.roll(b, d, 0), 0.0)
                a = a * jnp.where(keep, pltpu.roll(a, d, 0), 1.0)
            h = b + a * carry
            hs_ref[sl, :] = h
            return _row(h, SUBLANES - 1)

        lax.fori_loop(0, nblk, blk, jnp.zeros((1, tc), F32))
        yv_ref[...] = (hs_ref[...] * _gelu(gb_ref[...])).astype(BF16)

    return pl.pallas_call(
        body, name=name, grid=(E // tc,),
        in_specs=[col, col, col, col, vec, vec, vec],
        out_specs=[col, col], out_shape=[_sds((L, E), F32), _sds((L, E), BF16)],
        compiler_params=_cp(("parallel",)),
    )(pre_r, pre_i, xc, p, b_rg, b_ig, lam)


def lru_scan_bwd(dyv, hs, pre_r, pre_i, xc, p, b_rg, b_ig, lam, name):
    L, E = xc.shape
    tc, col, vec = _lru_specs(L)
    nblk = L // SUBLANES

    def body(dyv_ref, hs_ref, pr_ref, pi_ref, xc_ref, gb_ref, brg_ref, big_ref, lam_ref,
             dgb_ref, dpr_ref, dpi_ref, dxc_ref, dbrg_ref, dbig_ref, dlam_ref, t_gb, t_pr, t_pi):
        lam_v = lam_ref[...]
        sp = _softplus(-lam_v)
        brg, big = brg_ref[...], big_ref[...]
        rows = lax.broadcasted_iota(jnp.int32, (SUBLANES, tc), 0)

        def blk(kk, carry):
            gcar, a_next, acc_sp, acc_r, acc_i = carry
            k = nblk - 1 - kk
            r0 = pl.multiple_of(k * SUBLANES, SUBLANES)
            sl = pl.ds(r0, SUBLANES)
            r, ig, a, mult = _lru_gates(pr_ref[sl, :], pi_ref[sl, :], brg, big, sp)
            gbv, hsv, dyvv, xcv = gb_ref[sl, :], hs_ref[sl, :], dyv_ref[sl, :], xc_ref[sl, :]
            t_gb[sl, :] = dyvv * hsv * _gelu_grad(gbv)
            x = dyvv * _gelu(gbv)
            al = jnp.where(rows == SUBLANES - 1, a_next, pltpu.roll(a, SUBLANES - 1, 0))
            for d in (1, 2, 4):
                keep = rows < SUBLANES - d
                x = x + al * jnp.where(keep, pltpu.roll(x, SUBLANES - d, 0), 0.0)
                al = al * jnp.where(keep, pltpu.roll(al, SUBLANES - d, 0), 1.0)
            g = x + al * gcar
            rp = pl.multiple_of(jnp.maximum(k - 1, 0) * SUBLANES, SUBLANES)
            hlast = _row(hs_ref[pl.ds(rp, SUBLANES), :], SUBLANES - 1) * (k > 0).astype(F32)
            hprev = jnp.where(rows == 0, hlast, pltpu.roll(hsv, 1, 0))
            da = g * hprev
            dmult = g * ig * xcv
            dig = g * mult * xcv
            dxc_ref[sl, :] = g * mult * ig
            dla = da * a - dmult * (a * a) / mult
            dpr = dla * (-LRU_C * sp) * r * (1.0 - r)
            dpi = dig * ig * (1.0 - ig)
            t_pr[sl, :] = dpr
            t_pi[sl, :] = dpi
            return (_row(g, 0), _row(a, 0), acc_sp + dla * (-LRU_C * r), acc_r + dpr, acc_i + dpi)

        zero = jnp.zeros((SUBLANES, tc), F32)
        z1 = jnp.zeros((1, tc), F32)
        _, _, acc_sp, acc_r, acc_i = lax.fori_loop(0, nblk, blk, (z1, z1, zero, zero, zero))
        dgb_ref[...] = t_gb[...].astype(BF16)
        dpr_ref[...] = t_pr[...].astype(BF16)
        dpi_ref[...] = t_pi[...].astype(BF16)
        dbrg_ref[...] = _colsum(acc_r)
        dbig_ref[...] = _colsum(acc_i)
        dlam_ref[...] = -_colsum(acc_sp) * _sig(-lam_v)

    return pl.pallas_call(
        body, name=name, grid=(E // tc,),
        in_specs=[col, col, col, col, col, col, vec, vec, vec],
        out_specs=[col, col, col, col, vec, vec, vec],
        out_shape=[_sds((L, E), BF16), _sds((L, E), BF16), _sds((L, E), BF16), _sds((L, E), F32),
                   _sds((1, E), F32), _sds((1, E), F32), _sds((1, E), F32)],
        scratch_shapes=[pltpu.VMEM((L, tc), F32)] * 3,
        compiler_params=_cp(("parallel",)),
    )(dyv, hs, pre_r, pre_i, xc, p, b_rg, b_ig, lam)


def _place():
    xi, yi, ci = lax.axis_index("x"), lax.axis_index("y"), lax.axis_index("c")
    chips = [(1 - xi, yi), (xi, 1 - yi), (1 - xi, 1 - yi)]
    return xi, yi, ci, chips


_ANY = pl.BlockSpec(memory_space=pl.ANY)


def all_gather_devices(blk, name):
    R, C = blk.shape

    def body(x_ref, out_ref, send_sems, recv_sems, local_sem):
        xi, yi, ci, chips = _place()
        me, sibling = (xi, yi, ci), (xi, yi, 1 - ci)

        def slab(px, py, pc):
            return out_ref.at[4 * px + 2 * py + pc]

        def copy(k, block, to, src=None):
            return pltpu.make_async_remote_copy(
                src_ref=slab(*block) if src is None else src, dst_ref=slab(*block),
                send_sem=send_sems.at[k], recv_sem=recv_sems.at[k], device_id=to, device_id_type=MESH)

        mine = pltpu.make_async_copy(x_ref, slab(*me), local_sem)
        mine.start()
        first = [copy(0, me, sibling, src=x_ref)]
        first += [copy(1 + j, me, (*chip, ci), src=x_ref) for j, chip in enumerate(chips)]
        for cp in first:
            cp.start()
        passed = [copy(4 + j, (*chip, ci), sibling) for j, chip in enumerate(chips)]
        for j, chip in enumerate(chips):
            copy(1 + j, (*chip, ci), me).wait_recv()
            passed[j].start()
        copy(0, sibling, me).wait_recv()
        for j, chip in enumerate(chips):
            copy(4 + j, (*chip, 1 - ci), me).wait_recv()
        for cp in first + passed:
            cp.wait_send()
        mine.wait()

    return pl.pallas_call(
        body, name=name, in_specs=[_ANY], out_specs=_ANY, out_shape=_sds((N_DEV, R, C), blk.dtype),
        scratch_shapes=[pltpu.SemaphoreType.DMA((7,)), pltpu.SemaphoreType.DMA((7,)), pltpu.SemaphoreType.DMA(())],
    )(blk)


def all_gather_weights(shards, name):
    n = len(shards)

    def body(*refs):
        ins, outs = refs[:n], refs[n:2 * n]
        send_sems, recv_sems, local_sems = refs[2 * n:]
        xi, yi, ci, chips = _place()
        sibling = (xi, yi, 1 - ci)
        mychip = 2 * xi + yi

        def half(a, chip_idx, h):
            hr = shards[a].shape[0] // 2
            return outs[a].at[chip_idx, pl.ds(pl.multiple_of(h * hr, 16), hr), :]

        def over_ici(a, j, src, chip_idx):
            k = a * 6 + j
            return pltpu.make_async_remote_copy(
                src_ref=src, dst_ref=half(a, chip_idx, ci), send_sem=send_sems.at[k], recv_sem=recv_sems.at[k],
                device_id=(*chips[j], ci), device_id_type=MESH)

        def over_d2d(a, j, h):
            k = a * 6 + 3 + j
            chip_idx = 2 * chips[j][0] + chips[j][1]
            return pltpu.make_async_remote_copy(
                src_ref=half(a, chip_idx, h), dst_ref=half(a, chip_idx, h), send_sem=send_sems.at[k],
                recv_sem=recv_sems.at[k], device_id=sibling, device_id_type=MESH)

        local = [pltpu.make_async_copy(ins[a], outs[a].at[mychip], local_sems.at[a]) for a in range(n)]
        sends = []
        for a in range(n):
            local[a].start()
            hr = shards[a].shape[0] // 2
            src = ins[a].at[pl.ds(pl.multiple_of(ci * hr, 16), hr), :]
            for j in range(3):
                sends.append(over_ici(a, j, src, mychip))
                sends[-1].start()
        for a in range(n):
            for j in range(3):
                chip_idx = 2 * chips[j][0] + chips[j][1]
                over_ici(a, j, half(a, chip_idx, ci), chip_idx).wait_recv()
                sends.append(over_d2d(a, j, ci))
                sends[-1].start()
        for a in range(n):
            for j in range(3):
                over_d2d(a, j, 1 - ci).wait_recv()
        for cp in sends:
            cp.wait_send()
        for cp in local:
            cp.wait()

    return pl.pallas_call(
        body, name=name, in_specs=[_ANY] * n, out_specs=[_ANY] * n,
        out_shape=[_sds((N_CHIPS,) + s.shape, s.dtype) for s in shards],
        scratch_shapes=[pltpu.SemaphoreType.DMA((6 * n,)), pltpu.SemaphoreType.DMA((6 * n,)),
                        pltpu.SemaphoreType.DMA((n,))],
    )(*shards)


def exchange_halves(grads, name):
    n = len(grads)

    def body(*refs):
        ins, outs = refs[:n], refs[n:2 * n]
        send_sems, recv_sems = refs[2 * n:]
        xi, yi, ci, _ = _place()
        cps = []
        for a in range(n):
            hr = grads[a].shape[1] // 2
            src = ins[a].at[:, pl.ds(pl.multiple_of((1 - ci) * hr, 16), hr), :]
            cps.append(pltpu.make_async_remote_copy(
                src_ref=src, dst_ref=outs[a], send_sem=send_sems.at[a], recv_sem=recv_sems.at[a],
                device_id=(xi, yi, 1 - ci), device_id_type=MESH))
            cps[-1].start()
        for cp in cps:
            cp.wait()

    return pl.pallas_call(
        body, name=name, in_specs=[_ANY] * n, out_specs=[_ANY] * n,
        out_shape=[_sds((N_CHIPS, g.shape[1] // 2, g.shape[2]), g.dtype) for g in grads],
        scratch_shapes=[pltpu.SemaphoreType.DMA((n,)), pltpu.SemaphoreType.DMA((n,))],
    )(*grads)


def add_half(g, got, ci, name):
    S, hr, C = got.shape
    tr = _tile(hr, tuple(t for t in _TR if t * C <= ROW_TILE_ELEMS) or (16,))
    nb = hr // tr

    def body(c_ref, g_ref, r_ref, o_ref):
        o_ref[...] = (g_ref[...].astype(F32) + r_ref[...].astype(F32)).astype(BF16)

    return pl.pallas_call(
        body, name=name,
        grid_spec=pltpu.PrefetchScalarGridSpec(
            num_scalar_prefetch=1, grid=(S, nb),
            in_specs=[pl.BlockSpec((None, tr, C), lambda s, i, c: (s, c[0] * nb + i, 0)),
                      pl.BlockSpec((None, tr, C), lambda s, i, c: (s, i, 0))],
            out_specs=pl.BlockSpec((None, tr, C), lambda s, i, c: (s, i, 0))),
        out_shape=_sds((S, hr, C), BF16), compiler_params=_cp(("parallel", "parallel")),
    )(ci, g, got)


def exchange_chips(parts, name):
    n = len(parts)

    def body(*refs):
        ins, outs = refs[:n], refs[n:2 * n]
        send_sems, recv_sems = refs[2 * n:]
        xi, yi, ci, chips = _place()
        cps = []
        for a in range(n):
            for j in range(3):
                k = 3 * a + j
                cps.append(pltpu.make_async_remote_copy(
                    src_ref=ins[a].at[2 * chips[j][0] + chips[j][1]], dst_ref=outs[a].at[j],
                    send_sem=send_sems.at[k], recv_sem=recv_sems.at[k],
                    device_id=(*chips[j], ci), device_id_type=MESH))
                cps[-1].start()
        for cp in cps:
            cp.wait()

    return pl.pallas_call(
        body, name=name, in_specs=[_ANY] * n, out_specs=[_ANY] * n,
        out_shape=[_sds((3,) + p.shape[1:], p.dtype) for p in parts],
        scratch_shapes=[pltpu.SemaphoreType.DMA((3 * n,)), pltpu.SemaphoreType.DMA((3 * n,))],
    )(*parts)


def add_chips(part, got, chip, name):
    S, hr, C = part.shape
    tr = _tile(hr, tuple(t for t in _TR if t * C <= ROW_TILE_ELEMS) or (16,))

    def body(c_ref, p_ref, r_ref, o_ref):
        acc = p_ref[...].astype(F32)
        for j in range(3):
            acc = acc + r_ref[j].astype(F32)
        o_ref[...] = acc

    return pl.pallas_call(
        body, name=name,
        grid_spec=pltpu.PrefetchScalarGridSpec(
            num_scalar_prefetch=1, grid=(hr // tr,),
            in_specs=[pl.BlockSpec((None, tr, C), lambda i, c: (c[0], i, 0)),
                      pl.BlockSpec((3, tr, C), lambda i, c: (0, i, 0))],
            out_specs=pl.BlockSpec((tr, C), lambda i, c: (i, 0))),
        out_shape=_sds((hr, C), F32), compiler_params=_cp(("parallel",)),
    )(chip, part, got)


def join_halves(halves, name):
    n = len(halves)

    def body(*refs):
        ins, outs = refs[:n], refs[n:2 * n]
        send_sems, recv_sems, local_sems = refs[2 * n:]
        xi, yi, ci, _ = _place()
        cps, loc = [], []
        for a in range(n):
            hr = halves[a].shape[0]
            rows = outs[a].at[pl.ds(pl.multiple_of(ci * hr, 8), hr), :]
            loc.append(pltpu.make_async_copy(ins[a], rows, local_sems.at[a]))
            loc[-1].start()
            cps.append(pltpu.make_async_remote_copy(
                src_ref=ins[a], dst_ref=rows, send_sem=send_sems.at[a], recv_sem=recv_sems.at[a],
                device_id=(xi, yi, 1 - ci), device_id_type=MESH))
            cps[-1].start()
        for a in range(n):
            hr = halves[a].shape[0]
            other = outs[a].at[pl.ds(pl.multiple_of((1 - ci) * hr, 8), hr), :]
            pltpu.make_async_remote_copy(
                src_ref=ins[a], dst_ref=other, send_sem=send_sems.at[a], recv_sem=recv_sems.at[a],
                device_id=(xi, yi, 1 - ci), device_id_type=MESH).wait_recv()
        for cp in cps:
            cp.wait_send()
        for cp in loc:
            cp.wait()

    return pl.pallas_call(
        body, name=name, in_specs=[_ANY] * n, out_specs=[_ANY] * n,
        out_shape=[_sds((2 * h.shape[0], h.shape[1]), h.dtype) for h in halves],
        scratch_shapes=[pltpu.SemaphoreType.DMA((n,)), pltpu.SemaphoreType.DMA((n,)),
                        pltpu.SemaphoreType.DMA((n,))],
    )(*halves)


def reduce_scatter_grads(grads, ci1, chip1):
    got = exchange_halves(grads, "rs_exchange_halves")
    parts = [add_half(g, r, ci1, f"rs_add_half_{a}") for a, (g, r) in enumerate(zip(grads, got))]
    got2 = exchange_chips(parts, "rs_exchange_chips")
    halves = [add_chips(p, r, chip1, f"rs_add_chips_{a}") for a, (p, r) in enumerate(zip(parts, got2))]
    return join_halves(halves, "rs_join_halves")


def _pack(parts, width):
    flat = jnp.concatenate([p.reshape(-1).astype(F32) for p in parts])
    per = SUBLANES * width
    total = -(-flat.shape[0] // per) * per
    flat = jnp.pad(flat, (0, total - flat.shape[0]))
    return flat.reshape(total // width, width)


def _unpack(flat, shapes):
    out, off = [], 0
    for s in shapes:
        n = math.prod(s)
        out.append(flat[off:off + n].reshape(s))
        off += n
    return out


_W_NAMES = ['norm_g', 'w_ada', 'b_ada', 's5_w_in', 's5_lam_re', 's5_lam_im', 's5_log_dt', 's5_b_re', 's5_b_im',
            's5_c_re', 's5_c_im', 's5_d', 's5_w_glu', 'lru_w_in', 'lru_conv_w', 'lru_conv_b', 'lru_w_rg',
            'lru_b_rg', 'lru_w_ig', 'lru_b_ig', 'lru_lam', 'lru_w_out', 'ffn_w_gu', 'ffn_w_down', 'final_g']
_BIG = ('w_ada', 's5_w_in', 's5_w_glu', 'lru_w_in', 'lru_w_out', 'ffn_w_gu', 'ffn_w_down')


def _ffn_fwd(x, h, w_gu, w_down, gate, tag):
    gu = mm_nn(h, w_gu, name=f"{tag}_gu", out_dtype=BF16, bmode="cols")
    act = swiglu_fwd(gu, f"{tag}_act")
    z = mm_nn(act, w_down, name=f"{tag}_down")
    return res_gate_fwd(x, z, gate, f"{tag}_res"), (gu, act, z)


def _ffn_bwd(dx, h, saved, w_gu, w_down, gate, tag):
    gu, act, z = saved
    dz, dgate = res_gate_bwd(dx, z, gate, f"{tag}_res_bwd")
    dact = mm_nt(dz, w_down, name=f"{tag}_dact")
    dw_down = mm_tn(act, dz, name=f"{tag}_dwdown", out_dtype=BF16)
    dgu = swiglu_bwd(dact, gu, f"{tag}_act_bwd")
    dh = mm_nt(dgu, w_gu, name=f"{tag}_dh", bmode="cols")
    dw_gu = mm_tn(h, dgu, name=f"{tag}_dwgu", out_dtype=BF16, omode="cols", groups=N_CHIPS)
    return dh, dgate, dw_gu, dw_down.reshape((N_CHIPS, -1) + dw_down.shape[1:])


def _step(p):
    xi, yi, ci = lax.axis_index("x"), lax.axis_index("y"), lax.axis_index("c")
    chip = 2 * xi + yi
    me = 2 * chip + ci
    ci1 = jnp.reshape(ci, (1,)).astype(jnp.int32)
    chip1 = jnp.reshape(chip, (1,)).astype(jnp.int32)

    x0 = p['x'][0]
    tgt = p['loss_target'][0]
    L, D = x0.shape
    Dq = D // N_CHIPS
    depth = p['w_ada'].shape[0]
    E = p['lru_lam'].shape[1] * N_CHIPS
    Eq = E // N_CHIPS
    kw = p['lru_conv_w'].shape[1]
    Nq = p['w_ada'].shape[2]
    _, G, P, Cg = p['s5_b_re'].shape
    nb, bs = p['lru_w_rg'].shape[1], p['lru_w_rg'].shape[2]

    pay = _pack([p['c'], p['norm_g'], p['lru_conv_w'], p['lru_conv_b'], p['lru_b_rg'], p['lru_b_ig'],
                 p['lru_lam']], 1024)
    g1 = all_gather_devices(pay, "gather_small_params").reshape(N_DEV, -1)
    c_all = g1[:, :D]
    per_chip = g1[0::2]
    sizes = [(depth, 2, Dq), (kw, Eq), (Eq,), (Eq,), (Eq,), (Eq,)]
    offs = D
    pieces = []
    for s in sizes:
        nel = math.prod(s)
        pieces.append(per_chip[:, offs:offs + nel].reshape((N_CHIPS,) + s))
        offs += nel
    norm_g = jnp.moveaxis(pieces[0], 0, 2).reshape(depth, 2, D)
    conv_w = jnp.moveaxis(pieces[1], 0, 1).reshape(kw, E)
    conv_b, b_rg, b_ig, lam = [q.reshape(1, E) for q in pieces[2:]]

    cond = silu_rows(jnp.pad(c_all, ((0, 16 - N_DEV), (0, 0))), "cond_silu")
    cond_rep = jnp.concatenate([cond] * depth, axis=1)
    mod_part = mm_nn(cond_rep, p['w_ada'], name="mod_proj", bmode="batch")[:N_DEV]
    g2 = all_gather_devices(mod_part, "gather_mod")[0::2]
    mine = lax.dynamic_index_in_dim(g2, me, axis=1, keepdims=False).reshape(N_CHIPS, depth, Nq)
    mod = jnp.moveaxis(mine, 0, 1).reshape(depth, N_CHIPS * Nq) + p['b_ada']
    mods = [[mod[i:i + 1, k * D:(k + 1) * D] for k in range(N_MOD)] for i in range(depth)]

    shards = [p['s5_w_in'][0], p['s5_w_glu'][0], p['lru_w_in'][0], p['lru_w_out'][0],
              p['ffn_w_gu'][0], p['ffn_w_gu'][1], p['ffn_w_down'][0], p['ffn_w_down'][1]]
    gathered = all_gather_weights([s.astype(BF16) for s in shards], "gather_weights")
    s5_w_in = gathered[0].reshape(-1, D)
    s5_w_glu = gathered[1]
    lru_w_in = gathered[2]
    lru_w_out = gathered[3].reshape(-1, D)
    w_gu = [gathered[4], gathered[5]]
    w_down = [gathered[6].reshape(-1, D), gathered[7].reshape(-1, D)]

    s5_small = (p['s5_lam_re'][0], p['s5_lam_im'][0], p['s5_log_dt'][0], p['s5_b_re'][0], p['s5_b_im'][0])
    (ab_re, ab_im, bb_re, bb_im), s5_disc_vjp = jax.vjp(_s5_discretize, *s5_small)
    bsb, csb = _s5_blockdiag(bb_re, bb_im, p['s5_c_re'][0], p['s5_c_im'][0])
    cf, crv = _s5_scan_consts(ab_re, ab_im)
    wsb_rg, wsb_ig = [w.astype(BF16) for w in _lru_blockdiag(p['lru_w_rg'][0], p['lru_w_ig'][0])]
    nsb_lru = wsb_rg.shape[0]

    sh1, sc1, gt1, sh2, sc2, gt2 = mods[0]
    h0 = norm_mod_fwd(x0, norm_g[0, 0:1], sc1, sh1, "l0_norm1")
    u = mm_nn(h0, s5_w_in, name="s5_in")
    ypre, yg, ss = s5_scan_fwd(u, p['s5_d'], bsb, csb, cf, "s5_scan")
    v = mm_nn(yg, s5_w_glu, name="s5_glu", bmode="cols")
    x1 = glu_res_fwd(x0, v, gt1, "s5_res")
    h1 = norm_mod_fwd(x1, norm_g[0, 1:2], sc2, sh2, "l0_norm2")
    x2, ffn0 = _ffn_fwd(x1, h1, w_gu[0], w_down[0], gt2, "ffn0")

    sh1b, sc1b, gt1b, sh2b, sc2b, gt2b = mods[1]
    h2 = norm_mod_fwd(x2, norm_g[1, 0:1], sc1b, sh1b, "l1_norm1")
    pq = mm_nn(h2, lru_w_in, name="lru_in", bmode="cols")
    xc, xcb = lru_conv_fwd(pq, conv_w, conv_b, "lru_conv")
    pre_r = mm_nn(xcb, wsb_rg, name="lru_gate_r", bmode="batch")
    pre_i = mm_nn(xcb, wsb_ig, name="lru_gate_i", bmode="batch")
    hs, yv = lru_scan_fwd(pre_r, pre_i, xc, pq, b_rg, b_ig, lam, "lru_scan")
    mix = mm_nn(yv, lru_w_out, name="lru_out")
    x3 = res_gate_fwd(x2, mix, gt1b, "lru_res")
    h3 = norm_mod_fwd(x3, norm_g[1, 1:2], sc2b, sh2b, "l1_norm2")
    x4, ffn1 = _ffn_fwd(x3, h3, w_gu[1], w_down[1], gt2b, "ffn1")

    fg = p['final_g'].reshape(1, D)
    dx4, loss_blk, dfinal_g = final_loss(x4, fg, tgt, "final_loss")
    loss = lax.psum(loss_blk[0, 0], ("x", "y", "c"))

    dh3, dgt2b, dw_gu1, dw_down1 = _ffn_bwd(dx4, h3, ffn1, w_gu[1], w_down[1], gt2b, "ffn1")
    dx3, dgn11, dsc2b, dsh2b = norm_mod_bwd(dh3, x3, norm_g[1, 1:2], sc2b, dx4, "l1_norm2_bwd")

    dmix, dgt1b = res_gate_bwd(dx3, mix, gt1b, "lru_res_bwd")
    dyv = mm_nt(dmix, lru_w_out, name="lru_dyv")
    dw_out = mm_tn(yv, dmix, name="lru_dwout", out_dtype=BF16)
    dgb, dpre_r, dpre_i, dxc1, db_rg, db_ig, dlam = lru_scan_bwd(dyv, hs, pre_r, pre_i, xc, pq, b_rg, b_ig, lam,
                                                                "lru_scan_bwd")
    dxc2 = mm_nt(dpre_r, wsb_rg, name="lru_dxc_r", bmode="batch")
    dxc3 = mm_nt(dpre_i, wsb_ig, name="lru_dxc_i", bmode="batch")
    dwsb_rg = mm_tn(xcb, dpre_r, name="lru_dwgate_r", omode="batch", groups=nsb_lru)
    dwsb_ig = mm_tn(xcb, dpre_i, name="lru_dwgate_i", omode="batch", groups=nsb_lru)
    dxb, dconv_w, dconv_b = lru_conv_bwd(dxc1, dxc2, dxc3, pq, conv_w, "lru_conv_bwd")
    dpq = jnp.concatenate([dgb, dxb], axis=1)
    dh2 = mm_nt(dpq, lru_w_in, name="lru_dh", bmode="cols")
    dw_lru_in = mm_tn(h2, dpq, name="lru_dwin", out_dtype=BF16, omode="cols", groups=N_CHIPS)
    dx2, dgn10, dsc1b, dsh1b = norm_mod_bwd(dh2, x2, norm_g[1, 0:1], sc1b, dx3, "l1_norm1_bwd")

    dh1, dgt2, dw_gu0, dw_down0 = _ffn_bwd(dx2, h1, ffn0, w_gu[0], w_down[0], gt2, "ffn0")
    dx1, dgn01, dsc2, dsh2 = norm_mod_bwd(dh1, x1, norm_g[0, 1:2], sc2, dx2, "l0_norm2_bwd")

    dv, dgt1 = glu_res_bwd(dx1, v, gt1, "s5_res_bwd")
    dyg = mm_nt(dv, s5_w_glu, name="s5_dyg", bmode="cols")
    dw_glu = mm_tn(yg, dv, name="s5_dwglu", out_dtype=BF16, omode="cols", groups=N_CHIPS)
    du, dbsb, dcsb, da, dd = s5_scan_bwd(u, dyg, ypre, p['s5_d'], bsb, csb, cf, crv, ss, "s5_scan_bwd")
    dh0 = mm_nt(du, s5_w_in, name="s5_dh")
    dw_s5_in = mm_tn(h0, du, name="s5_dwin", out_dtype=BF16)
    grad_x, dgn00, dsc1, dsh1 = norm_mod_bwd(dh0, x0, norm_g[0, 0:1], sc1, dx1, "l0_norm1_bwd")

    def rows4(g):
        return g.reshape((N_CHIPS, -1) + g.shape[1:])

    big_parts = [rows4(dw_s5_in), dw_glu, dw_lru_in, rows4(dw_out), dw_gu0, dw_gu1, dw_down0, dw_down1]
    big = reduce_scatter_grads(big_parts, ci1, chip1)
    g_s5_w_in, g_s5_w_glu, g_lru_w_in, g_lru_w_out = big[0], big[1], big[2], big[3]
    g_ffn_w_gu = jnp.stack([big[4], big[5]])
    g_ffn_w_down = jnp.stack([big[6], big[7]])

    dmod = jnp.concatenate([jnp.concatenate([dsh1, dsc1, dgt1, dsh2, dsc2, dgt2], axis=1),
                            jnp.concatenate([dsh1b, dsc1b, dgt1b, dsh2b, dsc2b, dgt2b], axis=1)], axis=0)
    dnorm_g = jnp.stack([jnp.concatenate([dgn00, dgn01]), jnp.concatenate([dgn10, dgn11])])
    dbb_re, dbb_im, dc_re, dc_im = _s5_blockdiag_grads(dbsb, dcsb, P, Cg)
    H = S5_SB_GROUPS * P
    da_re, da_im = da[:, 0, :H].reshape(G, P), da[:, 0, H:].reshape(G, P)
    dw_rg, dw_ig = _lru_blockdiag_grad(dwsb_rg, nb, bs), _lru_blockdiag_grad(dwsb_ig, nb, bs)
    small = [dmod, dnorm_g, da_re, da_im, dbb_re, dbb_im, dc_re, dc_im, dd, dw_rg, dw_ig, dconv_w, dconv_b,
             db_rg, db_ig, dlam, dfinal_g]
    small_shapes = [s.shape for s in small]
    payload = _pack(small, 1024)
    gathered_small = all_gather_devices(payload, "gather_small_grads")
    total = sum_devices(gathered_small, "sum_small_grads").reshape(-1)
    (s_dmod, s_norm_g, s_da_re, s_da_im, s_dbb_re, s_dbb_im, s_dc_re, s_dc_im, s_dd, s_dw_rg, s_dw_ig,
     s_conv_w, s_conv_b, s_b_rg, s_b_ig, s_lam, s_final_g) = _unpack(total, small_shapes)
    g_lam_re, g_lam_im, g_log_dt, g_b_re, g_b_im = s5_disc_vjp((s_da_re, s_da_im, s_dbb_re, s_dbb_im))

    npay = payload.shape[0] * payload.shape[1]
    dmod_all = gathered_small.reshape(N_DEV, npay)[:, :depth * N_MOD * D].reshape(N_DEV, depth, N_CHIPS, Nq)
    dmod_mine = lax.dynamic_index_in_dim(dmod_all, chip, axis=2, keepdims=False).reshape(N_DEV, depth * Nq)
    dmod_mine = jnp.pad(dmod_mine, ((0, 16 - N_DEV), (0, 0)))
    g_w_ada = mm_tn(cond_rep, dmod_mine, name="w_ada_grad", omode="batch", groups=depth)

    def cols(full, width):
        return lax.dynamic_slice_in_dim(full, chip * width, width, axis=full.ndim - 1)

    grads = {
        'norm_g': cols(s_norm_g, Dq), 'w_ada': g_w_ada, 'b_ada': s_dmod,
        's5_w_in': g_s5_w_in[None], 's5_lam_re': g_lam_re[None], 's5_lam_im': g_lam_im[None],
        's5_log_dt': g_log_dt[None], 's5_b_re': g_b_re[None], 's5_b_im': g_b_im[None],
        's5_c_re': s_dc_re[None], 's5_c_im': s_dc_im[None], 's5_d': s_dd, 's5_w_glu': g_s5_w_glu[None],
        'lru_w_in': g_lru_w_in[None], 'lru_conv_w': cols(s_conv_w, Eq)[None, :, None, :],
        'lru_conv_b': cols(s_conv_b, Eq), 'lru_w_rg': s_dw_rg[None], 'lru_b_rg': cols(s_b_rg, Eq),
        'lru_w_ig': s_dw_ig[None], 'lru_b_ig': cols(s_b_ig, Eq), 'lru_lam': cols(s_lam, Eq),
        'lru_w_out': g_lru_w_out[None], 'ffn_w_gu': g_ffn_w_gu, 'ffn_w_down': g_ffn_w_down,
        'final_g': s_final_g.reshape(-1),
    }
    grads = {k: grads[k].reshape(p[k].shape) for k in _W_NAMES}

    delta, new_m, new_v = {}, {}, {}
    for k in _BIG:
        w2 = p[k].reshape(-1, p[k].shape[-1])
        outs = adamw(w2, grads[k].reshape(w2.shape), p['m_' + k].reshape(w2.shape), p['v_' + k].reshape(w2.shape),
                     f"adamw_{k}")
        delta[k], new_m[k], new_v[k] = [o.reshape(p[k].shape) for o in outs]
    rest = [k for k in _W_NAMES if k not in _BIG]
    shapes = [p[k].shape for k in rest]
    packed = [_pack([src[pre_ + k] if pre_ else src[k] for k in rest], 1024)
              for src, pre_ in ((p, ''), (grads, ''), (p, 'm_'), (p, 'v_'))]
    outs = adamw(*packed, "adamw_small")
    for dst, o in zip((delta, new_m, new_v), outs):
        for k, val in zip(rest, _unpack(o.reshape(-1), shapes)):
            dst[k] = val

    return (loss, grad_x[None], *[grads[k] for k in _W_NAMES], *[delta[k] for k in _W_NAMES],
            *[new_m[k] for k in _W_NAMES], *[new_v[k] for k in _W_NAMES])


_IN_NAMES = (['x', 'c'] + _W_NAMES + ['loss_target'] + ['m_' + k for k in _W_NAMES] + ['v_' + k for k in _W_NAMES])


def kernel(x, c, norm_g, w_ada, b_ada, s5_w_in, s5_lam_re, s5_lam_im, s5_log_dt, s5_b_re, s5_b_im, s5_c_re, s5_c_im, s5_d, s5_w_glu, lru_w_in, lru_conv_w, lru_conv_b, lru_w_rg, lru_b_rg, lru_w_ig, lru_b_ig, lru_lam, lru_w_out, ffn_w_gu, ffn_w_down, final_g, loss_target, m_norm_g, m_w_ada, m_b_ada, m_s5_w_in, m_s5_lam_re, m_s5_lam_im, m_s5_log_dt, m_s5_b_re, m_s5_b_im, m_s5_c_re, m_s5_c_im, m_s5_d, m_s5_w_glu, m_lru_w_in, m_lru_conv_w, m_lru_conv_b, m_lru_w_rg, m_lru_b_rg, m_lru_w_ig, m_lru_b_ig, m_lru_lam, m_lru_w_out, m_ffn_w_gu, m_ffn_w_down, m_final_g, v_norm_g, v_w_ada, v_b_ada, v_s5_w_in, v_s5_lam_re, v_s5_lam_im, v_s5_log_dt, v_s5_b_re, v_s5_b_im, v_s5_c_re, v_s5_c_im, v_s5_d, v_s5_w_glu, v_lru_w_in, v_lru_conv_w, v_lru_conv_b, v_lru_w_rg, v_lru_b_rg, v_lru_w_ig, v_lru_b_ig, v_lru_lam, v_lru_w_out, v_ffn_w_gu, v_ffn_w_down, v_final_g):
    args = (x, c, norm_g, w_ada, b_ada, s5_w_in, s5_lam_re, s5_lam_im, s5_log_dt, s5_b_re, s5_b_im, s5_c_re, s5_c_im, s5_d, s5_w_glu, lru_w_in, lru_conv_w, lru_conv_b, lru_w_rg, lru_b_rg, lru_w_ig, lru_b_ig, lru_lam, lru_w_out, ffn_w_gu, ffn_w_down, final_g, loss_target, m_norm_g, m_w_ada, m_b_ada, m_s5_w_in, m_s5_lam_re, m_s5_lam_im, m_s5_log_dt, m_s5_b_re, m_s5_b_im, m_s5_c_re, m_s5_c_im, m_s5_d, m_s5_w_glu, m_lru_w_in, m_lru_conv_w, m_lru_conv_b, m_lru_w_rg, m_lru_b_rg, m_lru_w_ig, m_lru_b_ig, m_lru_lam, m_lru_w_out, m_ffn_w_gu, m_ffn_w_down, m_final_g, v_norm_g, v_w_ada, v_b_ada, v_s5_w_in, v_s5_lam_re, v_s5_lam_im, v_s5_log_dt, v_s5_b_re, v_s5_b_im, v_s5_c_re, v_s5_c_im, v_s5_d, v_s5_w_glu, v_lru_w_in, v_lru_conv_w, v_lru_conv_b, v_lru_w_rg, v_lru_b_rg, v_lru_w_ig, v_lru_b_ig, v_lru_lam, v_lru_w_out, v_ffn_w_gu, v_ffn_w_down, v_final_g)
    return _step(dict(zip(_IN_NAMES, args)))
```

```python
import functools
import math

import jax
import jax.numpy as jnp
from jax import lax
from jax.experimental import pallas as pl
from jax.experimental.pallas import tpu as pltpu

F32 = jnp.float32
BF16 = jnp.bfloat16
MESH = pl.DeviceIdType.MESH

EPS = 1e-6
LRU_C = 8.0
N_MOD = 6
ADAM_LR = 0.001
ADAM_B1 = 0.9
ADAM_B2 = 0.999
ADAM_EPS = 1e-08
ADAM_WD = 0.01
ADAM_STEP = 10

N_CHIPS = 4
N_DEV = 8
SUBLANES = 8
LANES = 128
S5_SB_GROUPS = 8
V7X_VMEM_LIMIT = 48 * 1024 * 1024
ROW_TILE_ELEMS = 512 * 1024
SCAN_UNROLL = 2
LRU_TILE = 256

_TM = (1024, 512, 256, 128, 64, 32, 16, 8)
_TN = (1024, 1408, 512, 384, 256, 128)
_TK = (512, 1408, 256, 128)
_TR = (256, 128, 64, 32, 16, 8)

_GELU_K0 = math.sqrt(2.0 / math.pi)
_GELU_K1 = 0.044715


def _tile(n, cands):
    for t in cands:
        if n % t == 0:
            return t
    return n


def _cp(sem=None):
    return pltpu.CompilerParams(dimension_semantics=sem, vmem_limit_bytes=V7X_VMEM_LIMIT)


def _sds(shape, dtype):
    return jax.ShapeDtypeStruct(shape, dtype)


def _sig(x):
    return 1.0 / (1.0 + jnp.exp(-x))


def _gelu(x):
    t = jnp.tanh(_GELU_K0 * (x + _GELU_K1 * x * x * x))
    return 0.5 * x * (1.0 + t)


def _gelu_grad(x):
    x2 = x * x
    t = jnp.tanh(_GELU_K0 * (x + _GELU_K1 * x * x2))
    return 0.5 * (1.0 + t) + 0.5 * x * (1.0 - t * t) * _GELU_K0 * (1.0 + 3.0 * _GELU_K1 * x2)


def _softplus(z):
    return jnp.maximum(z, 0.0) + jnp.log(1.0 + jnp.exp(-jnp.abs(z)))


def _neg_expm1(x):
    series = -x * (1.0 + x * (0.5 + x * (1.0 / 6.0 + x * (1.0 / 24.0))))
    return jnp.where(x > -0.05, series, 1.0 - jnp.exp(x))


def _row(x, r):
    return x[r:r + 1, :]


def _colsum(x):
    return jnp.sum(x, axis=0, keepdims=True)


_NN = (((1,), (0,)), ((), ()))
_NT = (((1,), (1,)), ((), ()))
_TN_DIMS = (((0,), (0,)), ((), ()))


def _mm_call(name, a, b, a_spec, b_spec, o_spec, grid, out_shape, acc_shape, dims):
    nk = grid[-1]
    kaxis = len(grid) - 1

    def body(a_ref, b_ref, o_ref, acc_ref):
        k = pl.program_id(kaxis)

        @pl.when(k == 0)
        def _():
            acc_ref[...] = jnp.zeros_like(acc_ref)

        acc_ref[...] += lax.dot_general(a_ref[...].astype(BF16), b_ref[...].astype(BF16), dims,
                                        preferred_element_type=F32)

        @pl.when(k == nk - 1)
        def _():
            o_ref[...] = acc_ref[...].astype(o_ref.dtype)

    return pl.pallas_call(
        body, name=name, grid=grid, in_specs=[a_spec, b_spec], out_specs=o_spec, out_shape=out_shape,
        scratch_shapes=[pltpu.VMEM(acc_shape, F32)],
        compiler_params=_cp(("parallel", "parallel", "parallel", "arbitrary")),
    )(a, b)


def mm_nn(a, b, *, name, out_dtype=F32, bmode="plain"):
    M = a.shape[0]
    if bmode == "plain":
        G, S = 1, 1
        K, Nc = b.shape
    elif bmode == "cols":
        G = 1
        S, K, Nc = b.shape
    else:
        S = 1
        G, K, Nc = b.shape
    tm, tn, tk = _tile(M, _TM), _tile(Nc, _TN), _tile(K, _TK)
    nkb, nnb = K // tk, Nc // tn
    ncol = S * nnb
    grid = (G, M // tm, ncol, nkb)
    a_spec = pl.BlockSpec((tm, tk), lambda g, i, j, k: (i, g * nkb + k))
    if bmode == "plain":
        b_spec = pl.BlockSpec((tk, tn), lambda g, i, j, k: (k, j))
    elif bmode == "cols":
        b_spec = pl.BlockSpec((None, tk, tn), lambda g, i, j, k: (j // nnb, k, j % nnb))
    else:
        b_spec = pl.BlockSpec((None, tk, tn), lambda g, i, j, k: (g, k, j))
    o_spec = pl.BlockSpec((tm, tn), lambda g, i, j, k: (i, g * ncol + j))
    return _mm_call(name, a, b, a_spec, b_spec, o_spec, grid, _sds((M, G * S * Nc), out_dtype), (tm, tn), _NN)


def mm_nt(a, b, *, name, out_dtype=F32, bmode="plain"):
    M = a.shape[0]
    if bmode == "plain":
        G, S = 1, 1
        Ko, Nc = b.shape
    elif bmode == "cols":
        G = 1
        S, Ko, Nc = b.shape
    else:
        S = 1
        G, Ko, Nc = b.shape
    tm, to, tc = _tile(M, _TM), _tile(Ko, _TN), _tile(Nc, _TK)
    npc = Nc // tc
    nc = S * npc
    nob = Ko // to
    grid = (G, M // tm, nob, nc)
    a_spec = pl.BlockSpec((tm, tc), lambda g, i, j, n: (i, g * nc + n))
    if bmode == "plain":
        b_spec = pl.BlockSpec((to, tc), lambda g, i, j, n: (j, n))
    elif bmode == "cols":
        b_spec = pl.BlockSpec((None, to, tc), lambda g, i, j, n: (n // npc, j, n % npc))
    else:
        b_spec = pl.BlockSpec((None, to, tc), lambda g, i, j, n: (g, j, n))
    o_spec = pl.BlockSpec((tm, to), lambda g, i, j, n: (i, g * nob + j))
    return _mm_call(name, a, b, a_spec, b_spec, o_spec, grid, _sds((M, G * Ko), out_dtype), (tm, to), _NT)


def mm_tn(a, b, *, name, out_dtype=F32, omode="plain", groups=1):
    L = a.shape[0]
    G = groups if omode == "batch" else 1
    S = groups if omode == "cols" else 1
    Mo, N = a.shape[1] // G, b.shape[1] // G
    Nc = N // S
    tm, tn, tl = _tile(Mo, _TM), _tile(Nc, _TN), _tile(L, _TK)
    nmb, nnb = Mo // tm, N // tn
    npj = Nc // tn
    grid = (G, nmb, nnb, L // tl)
    a_spec = pl.BlockSpec((tl, tm), lambda g, i, j, l: (l, g * nmb + i))
    b_spec = pl.BlockSpec((tl, tn), lambda g, i, j, l: (l, g * nnb + j))
    if omode == "plain":
        o_spec = pl.BlockSpec((tm, tn), lambda g, i, j, l: (i, j))
        oshape = (Mo, N)
    elif omode == "cols":
        o_spec = pl.BlockSpec((None, tm, tn), lambda g, i, j, l: (j // npj, i, j % npj))
        oshape = (S, Mo, Nc)
    else:
        o_spec = pl.BlockSpec((None, tm, tn), lambda g, i, j, l: (g, i, j))
        oshape = (G, Mo, N)
    return _mm_call(name, a, b, a_spec, b_spec, o_spec, grid, _sds(oshape, out_dtype), (tm, tn), _TN_DIMS)


def _row_call(name, body, row_ins, vec_ins, row_outs, acc_outs=()):
    L = row_ins[0].shape[0]
    wmax = max([a.shape[1] for a in row_ins] + [w for w, _ in row_outs])
    tr = _tile(L, tuple(t for t in _TR if t * wmax <= ROW_TILE_ELEMS) or (SUBLANES,))
    in_specs = [pl.BlockSpec((tr, a.shape[1]), lambda i: (i, 0)) for a in row_ins]
    in_specs += [pl.BlockSpec(v.shape, lambda i, nd=v.ndim: (0,) * nd) for v in vec_ins]
    out_shape = [_sds((L, w), dt) for w, dt in row_outs] + [_sds(s, dt) for s, dt in acc_outs]
    out_specs = [pl.BlockSpec((tr, w), lambda i: (i, 0)) for w, _ in row_outs]
    out_specs += [pl.BlockSpec(s, lambda i, nd=len(s): (0,) * nd) for s, _ in acc_outs]
    sem = ("arbitrary",) if acc_outs else ("parallel",)
    return pl.pallas_call(body, name=name, grid=(L // tr,), in_specs=in_specs, out_specs=out_specs,
                          out_shape=out_shape, compiler_params=_cp(sem))(*row_ins, *vec_ins)


def silu_rows(x, name):
    def body(x_ref, o_ref):
        v = x_ref[...]
        o_ref[...] = (v * _sig(v)).astype(o_ref.dtype)
    return _row_call(name, body, [x], [], [(x.shape[1], BF16)])[0]


def norm_mod_fwd(x, gain, sc, sh, name):
    def body(x_ref, g_ref, sc_ref, sh_ref, h_ref):
        v = x_ref[...]
        r = lax.rsqrt(jnp.mean(v * v, axis=-1, keepdims=True) + EPS)
        h_ref[...] = (v * r * g_ref[...] * (1.0 + sc_ref[...]) + sh_ref[...]).astype(BF16)
    return _row_call(name, body, [x], [gain, sc, sh], [(x.shape[1], BF16)])[0]


def norm_mod_bwd(dh, x, gain, sc, dres, name):
    D = x.shape[1]

    def body(dh_ref, x_ref, dres_ref, g_ref, sc_ref, dx_ref, dg_ref, dsc_ref, dsh_ref):
        @pl.when(pl.program_id(0) == 0)
        def _():
            dg_ref[...] = jnp.zeros_like(dg_ref)
            dsc_ref[...] = jnp.zeros_like(dsc_ref)
            dsh_ref[...] = jnp.zeros_like(dsh_ref)

        v = x_ref[...]
        dh_v = dh_ref[...]
        g = g_ref[...]
        r = lax.rsqrt(jnp.mean(v * v, axis=-1, keepdims=True) + EPS)
        xhat = v * r
        dn = dh_v * (1.0 + sc_ref[...])
        dsc_ref[...] += _colsum(dh_v * xhat * g)
        dsh_ref[...] += _colsum(dh_v)
        dg_ref[...] += _colsum(dn * xhat)
        t = dn * g
        dx_ref[...] = dres_ref[...] + r * (t - xhat * jnp.mean(t * xhat, axis=-1, keepdims=True))

    acc = [((1, D), F32)] * 3
    return _row_call(name, body, [dh, x, dres], [gain, sc], [(D, F32)], acc)


def final_loss(x, gain, tgt, name):
    D = x.shape[1]

    def body(x_ref, t_ref, g_ref, dx_ref, loss_ref, dg_ref, acc_ref):
        i = pl.program_id(0)

        @pl.when(i == 0)
        def _():
            dg_ref[...] = jnp.zeros_like(dg_ref)
            acc_ref[...] = jnp.zeros_like(acc_ref)

        v = x_ref[...]
        g = g_ref[...]
        r = lax.rsqrt(jnp.mean(v * v, axis=-1, keepdims=True) + EPS)
        xhat = v * r
        err = xhat * g - t_ref[...]
        acc_ref[...] += _colsum(err * err)
        dout = err * (1.0 / D)
        dg_ref[...] += _colsum(dout * xhat)
        t = dout * g
        dx_ref[...] = r * (t - xhat * jnp.mean(t * xhat, axis=-1, keepdims=True))

        @pl.when(i == pl.num_programs(0) - 1)
        def _():
            loss_ref[...] = jnp.zeros_like(loss_ref) + jnp.sum(acc_ref[...]) * (0.5 / D)

    return _row_call(name, body, [x, tgt], [gain], [(D, F32)],
                     [((SUBLANES, LANES), F32), ((1, D), F32), ((1, D), F32)])[:3]


def res_gate_fwd(x, z, g, name):
    def body(x_ref, z_ref, g_ref, o_ref):
        o_ref[...] = x_ref[...] + g_ref[...] * z_ref[...]
    return _row_call(name, body, [x, z], [g], [(x.shape[1], F32)])[0]


def res_gate_bwd(dx, z, g, name):
    D = dx.shape[1]

    def body(dx_ref, z_ref, g_ref, dz_ref, dg_ref):
        @pl.when(pl.program_id(0) == 0)
        def _():
            dg_ref[...] = jnp.zeros_like(dg_ref)
        d = dx_ref[...]
        dz_ref[...] = (g_ref[...] * d).astype(BF16)
        dg_ref[...] += _colsum(d * z_ref[...])
    return _row_call(name, body, [dx, z], [g], [(D, BF16)], [((1, D), F32)])


def glu_res_fwd(x, v, g, name):
    D = x.shape[1]

    def body(x_ref, v_ref, g_ref, o_ref):
        vv = v_ref[...]
        o_ref[...] = x_ref[...] + g_ref[...] * (vv[:, :D] * _sig(vv[:, D:]))
    return _row_call(name, body, [x, v], [g], [(D, F32)])[0]


def glu_res_bwd(dx, v, g, name):
    D = dx.shape[1]

    def body(dx_ref, v_ref, g_ref, dv_ref, dg_ref):
        @pl.when(pl.program_id(0) == 0)
        def _():
            dg_ref[...] = jnp.zeros_like(dg_ref)
        d = dx_ref[...]
        vv = v_ref[...]
        val = vv[:, :D]
        s = _sig(vv[:, D:])
        dg_ref[...] += _colsum(d * val * s)
        dm = g_ref[...] * d
        dv_ref[:, :D] = (dm * s).astype(BF16)
        dv_ref[:, D:] = (dm * val * s * (1.0 - s)).astype(BF16)
    return _row_call(name, body, [dx, v], [g], [(2 * D, BF16)], [((1, D), F32)])


def swiglu_fwd(gu, name):
    F = gu.shape[1] // 2

    def body(gu_ref, o_ref):
        v = gu_ref[...].astype(F32)
        g = v[:, :F]
        o_ref[...] = (g * _sig(g) * v[:, F:]).astype(BF16)
    return _row_call(name, body, [gu], [], [(F, BF16)])[0]


def swiglu_bwd(dact, gu, name):
    F = gu.shape[1] // 2

    def body(da_ref, gu_ref, o_ref):
        v = gu_ref[...].astype(F32)
        g, u = v[:, :F], v[:, F:]
        da = da_ref[...]
        s = _sig(g)
        o_ref[:, :F] = (da * u * s * (1.0 + g * (1.0 - s))).astype(BF16)
        o_ref[:, F:] = (da * g * s).astype(BF16)
    return _row_call(name, body, [dact, gu], [], [(2 * F, BF16)])[0]


def adamw(w, g, m, v, name):
    C = w.shape[1]
    c1 = 1.0 - ADAM_B1 ** ADAM_STEP
    c2 = 1.0 - ADAM_B2 ** ADAM_STEP

    def body(w_ref, g_ref, m_ref, v_ref, d_ref, m2_ref, v2_ref):
        gv = g_ref[...]
        m2 = ADAM_B1 * m_ref[...] + (1.0 - ADAM_B1) * gv
        v2 = ADAM_B2 * v_ref[...] + (1.0 - ADAM_B2) * (gv * gv)
        m2_ref[...] = m2
        v2_ref[...] = v2
        d_ref[...] = -ADAM_LR * ((m2 / c1) / (jnp.sqrt(v2 / c2) + ADAM_EPS) + ADAM_WD * w_ref[...])
    return _row_call(name, body, [w, g, m, v], [], [(C, F32)] * 3)


def sum_devices(parts, name):
    n, R, C = parts.shape
    tr = _tile(R, tuple(t for t in _TR if t * C * n <= 4 * ROW_TILE_ELEMS) or (SUBLANES,))

    def body(p_ref, o_ref):
        acc = p_ref[0]
        for d in range(1, n):
            acc = acc + p_ref[d]
        o_ref[...] = acc
    return pl.pallas_call(body, name=name, grid=(R // tr,),
                          in_specs=[pl.BlockSpec((n, tr, C), lambda i: (0, i, 0))],
                          out_specs=pl.BlockSpec((tr, C), lambda i: (i, 0)), out_shape=_sds((R, C), F32),
                          compiler_params=_cp(("parallel",)))(parts)


def _s5_discretize(lam_re, lam_im, log_dt, b_re, b_im):
    dt = jnp.exp(log_dt)[:, None]
    mag = jnp.exp(lam_re * dt)
    ab_re = mag * jnp.cos(lam_im * dt)
    ab_im = mag * jnp.sin(lam_im * dt)
    nr, ni = ab_re - 1.0, ab_im
    den = lam_re * lam_re + lam_im * lam_im
    f_re = (nr * lam_re + ni * lam_im) / den
    f_im = (ni * lam_re - nr * lam_im) / den
    bb_re = f_re[..., None] * b_re - f_im[..., None] * b_im
    bb_im = f_re[..., None] * b_im + f_im[..., None] * b_re
    return ab_re, ab_im, bb_re, bb_im


def _s5_blockdiag(bb_re, bb_im, c_re, c_im):
    G, P, Cg = bb_re.shape
    nsb = G // S5_SB_GROUPS

    def bmat(bb):
        t = jnp.swapaxes(bb.reshape(nsb, S5_SB_GROUPS, P, Cg), 2, 3)
        return _spread_diag(t).reshape(nsb, S5_SB_GROUPS * Cg, S5_SB_GROUPS * P)

    def cmat(cc):
        t = jnp.swapaxes(cc.reshape(nsb, S5_SB_GROUPS, Cg, P), 2, 3)
        return _spread_diag(t).reshape(nsb, S5_SB_GROUPS * P, S5_SB_GROUPS * Cg)

    bsb = jnp.concatenate([bmat(bb_re), bmat(bb_im)], axis=-1)
    csb = jnp.concatenate([cmat(c_re), -cmat(c_im)], axis=1)
    return bsb, csb


def _spread_diag(t):
    ng = t.shape[1]
    eye = jnp.eye(ng, dtype=t.dtype)
    return t[:, :, :, None, :] * eye[None, :, None, :, None]


def _take_diag(t):
    return jnp.stack([t[:, g, :, g, :] for g in range(t.shape[1])], axis=1)


def _s5_blockdiag_grads(dbsb, dcsb, P, Cg):
    nsb = dbsb.shape[0]
    db6 = dbsb.reshape(nsb, S5_SB_GROUPS, Cg, 2, S5_SB_GROUPS, P)
    dbb_re = jnp.swapaxes(_take_diag(db6[:, :, :, 0]), 2, 3).reshape(-1, P, Cg)
    dbb_im = jnp.swapaxes(_take_diag(db6[:, :, :, 1]), 2, 3).reshape(-1, P, Cg)
    dc6 = dcsb.reshape(nsb, 2, S5_SB_GROUPS, P, S5_SB_GROUPS, Cg)
    dc_re = jnp.swapaxes(_take_diag(dc6[:, 0]), 2, 3).reshape(-1, Cg, P)
    dc_im = -jnp.swapaxes(_take_diag(dc6[:, 1]), 2, 3).reshape(-1, Cg, P)
    return dbb_re, dbb_im, dc_re, dc_im


def _s5_scan_consts(ab_re, ab_im):
    G, P = ab_re.shape
    nsb = G // S5_SB_GROUPS
    H = S5_SB_GROUPS * P
    ar, ai = ab_re.reshape(nsb, 1, H), ab_im.reshape(nsb, 1, H)
    pows = [(ar, ai)]
    for _ in range(SUBLANES - 1):
        pr, pi_ = pows[-1]
        pows.append((pr * ar - pi_ * ai, pr * ai + pi_ * ar))
    rows = jnp.arange(SUBLANES).reshape(1, SUBLANES, 1)

    def masked(k, keep):
        pr, pi_ = pows[k - 1]
        return jnp.where(keep, pr, 0.0), jnp.where(keep, pi_, 0.0)

    def per_row(sel):
        pr = jnp.concatenate([pows[sel(r) - 1][0] for r in range(SUBLANES)], axis=1)
        pi_ = jnp.concatenate([pows[sel(r) - 1][1] for r in range(SUBLANES)], axis=1)
        return pr, pi_

    fwd = [masked(1, rows >= 1), masked(2, rows >= 2), masked(4, rows >= 4), per_row(lambda r: r + 1)]
    rev = [masked(1, rows < 7), masked(2, rows < 6), masked(4, rows < 4), per_row(lambda r: SUBLANES - r)]

    def pack(lst, conj):
        sgn = -1.0 if conj else 1.0
        return jnp.stack([jnp.concatenate([jnp.broadcast_to(pr, (nsb, SUBLANES, H)),
                                           sgn * jnp.broadcast_to(pi_, (nsb, SUBLANES, H))], axis=-1)
                          for pr, pi_ in lst], axis=1)

    return pack(fwd, False), pack(rev, True)


def _cmadd(xr, xi, ar, ai, yr, yi):
    return xr + ar * yr - ai * yi, xi + ar * yi + ai * yr


def _s5_scan_fwd_loop(src_ref, dst_ref, sp_ref, cf_ref, cr, ci, nblk, H):
    rows = lax.broadcasted_iota(jnp.int32, (SUBLANES, H), 0)

    def body(k, carry):
        cr, ci = carry
        r0 = pl.multiple_of(k * SUBLANES, SUBLANES)
        xr = src_ref[pl.ds(r0, SUBLANES), pl.ds(0, H)]
        xi = src_ref[pl.ds(r0, SUBLANES), pl.ds(H, H)]
        for idx, d in enumerate((1, 2, 4)):
            xr, xi = _cmadd(xr, xi, cf_ref[idx, :, pl.ds(0, H)], cf_ref[idx, :, pl.ds(H, H)],
                            pltpu.roll(xr, d, 0), pltpu.roll(xi, d, 0))
        xr, xi = _cmadd(xr, xi, cf_ref[3, :, pl.ds(0, H)], cf_ref[3, :, pl.ds(H, H)], cr, ci)
        dst_ref[pl.ds(r0, SUBLANES), pl.ds(0, H)] = xr
        dst_ref[pl.ds(r0, SUBLANES), pl.ds(H, H)] = xi
        if sp_ref is not None:
            sp_ref[pl.ds(r0, SUBLANES), pl.ds(0, H)] = jnp.where(rows == 0, cr, pltpu.roll(xr, 1, 0))
            sp_ref[pl.ds(r0, SUBLANES), pl.ds(H, H)] = jnp.where(rows == 0, ci, pltpu.roll(xi, 1, 0))
        return _row(xr, SUBLANES - 1), _row(xi, SUBLANES - 1)

    return lax.fori_loop(0, nblk, body, (cr, ci))


def s5_scan_fwd(u, d_skip, bsb, csb, cf, name):
    L, W = u.shape
    nsb, GW, H2 = bsb.shape
    H = H2 // 2
    Tc = _tile(L, (512, 256, 128, 64, 32, 16, 8))
    nch = L // Tc

    def body(u_ref, d_ref, b_ref, c_ref, cf_ref, ypre_ref, yg_ref, ss_ref, bu_scr, car_scr):
        @pl.when(pl.program_id(1) == 0)
        def _():
            car_scr[...] = jnp.zeros_like(car_scr)

        ss_ref[...] = car_scr[...]
        ub = u_ref[...]
        bu_scr[...] = jnp.dot(ub.astype(BF16), b_ref[...], preferred_element_type=F32)
        cr, ci = _s5_scan_fwd_loop(bu_scr, bu_scr, None, cf_ref, car_scr[:, pl.ds(0, H)], car_scr[:, pl.ds(H, H)],
                                   Tc // SUBLANES, H)
        car_scr[:, pl.ds(0, H)] = cr
        car_scr[:, pl.ds(H, H)] = ci
        ypre = jnp.dot(bu_scr[...].astype(BF16), c_ref[...], preferred_element_type=F32) + d_ref[...] * ub
        ypre_ref[...] = ypre
        yg_ref[...] = _gelu(ypre).astype(BF16)

    return pl.pallas_call(
        body, name=name, grid=(nsb, nch),
        in_specs=[pl.BlockSpec((Tc, GW), lambda j, i: (i, j)),
                  pl.BlockSpec((1, GW), lambda j, i: (0, j)),
                  pl.BlockSpec((None, GW, H2), lambda j, i: (j, 0, 0)),
                  pl.BlockSpec((None, H2, GW), lambda j, i: (j, 0, 0)),
                  pl.BlockSpec((None, 4, SUBLANES, H2), lambda j, i: (j, 0, 0, 0))],
        out_specs=[pl.BlockSpec((Tc, GW), lambda j, i: (i, j)),
                   pl.BlockSpec((Tc, GW), lambda j, i: (i, j)),
                   pl.BlockSpec((None, None, 1, H2), lambda j, i: (i, j, 0, 0))],
        out_shape=[_sds((L, W), F32), _sds((L, W), BF16), _sds((nch, nsb, 1, H2), F32)],
        scratch_shapes=[pltpu.VMEM((Tc, H2), F32), pltpu.VMEM((1, H2), F32)],
        compiler_params=_cp(("arbitrary", "arbitrary")),
    )(u, d_skip, bsb.astype(BF16), csb.astype(BF16), cf)


def s5_scan_bwd(u, dyg, ypre, d_skip, bsb, csb, cf, crv, ss, name):
    L, W = u.shape
    nsb, GW, H2 = bsb.shape
    H = H2 // 2
    Tc = _tile(L, (512, 256, 128, 64, 32, 16, 8))
    nch = L // Tc
    nblk = Tc // SUBLANES
    bsb_t = jnp.swapaxes(bsb, 1, 2).astype(BF16)
    csb_t = jnp.swapaxes(csb, 1, 2).astype(BF16)

    def body(u_ref, dyg_ref, yp_ref, d_ref, b_ref, bt_ref, ct_ref, cf_ref, crv_ref, ss_ref,
             du_ref, db_ref, dc_ref, da_ref, dd_ref, s_scr, sp_scr, g_scr, gcar_scr):
        @pl.when(pl.program_id(1) == 0)
        def _():
            gcar_scr[...] = jnp.zeros_like(gcar_scr)
            db_ref[...] = jnp.zeros_like(db_ref)
            dc_ref[...] = jnp.zeros_like(dc_ref)
            da_ref[...] = jnp.zeros_like(da_ref)
            dd_ref[...] = jnp.zeros_like(dd_ref)

        ub = u_ref[...]
        ubf = ub.astype(BF16)
        dyp = dyg_ref[...] * _gelu_grad(yp_ref[...])
        dypb = dyp.astype(BF16)
        dd_ref[...] += _colsum(dyp * ub)
        s_scr[...] = jnp.dot(ubf, b_ref[...], preferred_element_type=F32)
        _s5_scan_fwd_loop(s_scr, s_scr, sp_scr, cf_ref, ss_ref[:, pl.ds(0, H)], ss_ref[:, pl.ds(H, H)], nblk, H)
        g_scr[...] = jnp.dot(dypb, ct_ref[...], preferred_element_type=F32)

        def rev(kk, carry):
            gr, gi, acc_r, acc_i = carry
            r0 = pl.multiple_of((nblk - 1 - kk) * SUBLANES, SUBLANES)
            xr = g_scr[pl.ds(r0, SUBLANES), pl.ds(0, H)]
            xi = g_scr[pl.ds(r0, SUBLANES), pl.ds(H, H)]
            for idx, d in enumerate((1, 2, 4)):
                xr, xi = _cmadd(xr, xi, crv_ref[idx, :, pl.ds(0, H)], crv_ref[idx, :, pl.ds(H, H)],
                                pltpu.roll(xr, SUBLANES - d, 0), pltpu.roll(xi, SUBLANES - d, 0))
            xr, xi = _cmadd(xr, xi, crv_ref[3, :, pl.ds(0, H)], crv_ref[3, :, pl.ds(H, H)], gr, gi)
            g_scr[pl.ds(r0, SUBLANES), pl.ds(0, H)] = xr
            g_scr[pl.ds(r0, SUBLANES), pl.ds(H, H)] = xi
            spr = sp_scr[pl.ds(r0, SUBLANES), pl.ds(0, H)]
            spi = sp_scr[pl.ds(r0, SUBLANES), pl.ds(H, H)]
            return (_row(xr, 0), _row(xi, 0), acc_r + xr * spr + xi * spi, acc_i + xi * spr - xr * spi)

        zero = jnp.zeros((SUBLANES, H), F32)
        gr, gi, acc_r, acc_i = lax.fori_loop(
            0, nblk, rev, (gcar_scr[:, pl.ds(0, H)], gcar_scr[:, pl.ds(H, H)], zero, zero))
        gcar_scr[:, pl.ds(0, H)] = gr
        gcar_scr[:, pl.ds(H, H)] = gi
        da_ref[:, pl.ds(0, H)] += _colsum(acc_r)
        da_ref[:, pl.ds(H, H)] += _colsum(acc_i)
        gb = g_scr[...].astype(BF16)
        db_ref[...] += lax.dot_general(ubf, gb, _TN_DIMS, preferred_element_type=F32)
        dc_ref[...] += lax.dot_general(s_scr[...].astype(BF16), dypb, _TN_DIMS, preferred_element_type=F32)
        du_ref[...] = (jnp.dot(gb, bt_ref[...], preferred_element_type=F32) + d_ref[...] * dyp).astype(BF16)

    rmap = lambda j, i: (nch - 1 - i, j)
    return pl.pallas_call(
        body, name=name, grid=(nsb, nch),
        in_specs=[pl.BlockSpec((Tc, GW), rmap), pl.BlockSpec((Tc, GW), rmap), pl.BlockSpec((Tc, GW), rmap),
                  pl.BlockSpec((1, GW), lambda j, i: (0, j)),
                  pl.BlockSpec((None, GW, H2), lambda j, i: (j, 0, 0)),
                  pl.BlockSpec((None, H2, GW), lambda j, i: (j, 0, 0)),
                  pl.BlockSpec((None, GW, H2), lambda j, i: (j, 0, 0)),
                  pl.BlockSpec((None, 4, SUBLANES, H2), lambda j, i: (j, 0, 0, 0)),
                  pl.BlockSpec((None, 4, SUBLANES, H2), lambda j, i: (j, 0, 0, 0)),
                  pl.BlockSpec((None, None, 1, H2), lambda j, i: (nch - 1 - i, j, 0, 0))],
        out_specs=[pl.BlockSpec((Tc, GW), rmap),
                   pl.BlockSpec((None, GW, H2), lambda j, i: (j, 0, 0)),
                   pl.BlockSpec((None, H2, GW), lambda j, i: (j, 0, 0)),
                   pl.BlockSpec((None, 1, H2), lambda j, i: (j, 0, 0)),
                   pl.BlockSpec((1, GW), lambda j, i: (0, j))],
        out_shape=[_sds((L, W), BF16), _sds((nsb, GW, H2), F32), _sds((nsb, H2, GW), F32),
                   _sds((nsb, 1, H2), F32), _sds((1, W), F32)],
        scratch_shapes=[pltpu.VMEM((Tc, H2), F32), pltpu.VMEM((Tc, H2), F32), pltpu.VMEM((Tc, H2), F32),
                        pltpu.VMEM((1, H2), F32)],
        compiler_params=_cp(("arbitrary", "arbitrary")),
    )(u, dyg, ypre, d_skip, bsb.astype(BF16), bsb_t, csb_t, cf, crv, ss)


def _lru_blockdiag(w_rg, w_ig):
    nb, bs, _ = w_rg.shape
    sbw = bs * LANES // math.gcd(bs, LANES)
    bps = sbw // bs
    nsb = nb // bps

    def bd(w):
        return _spread_diag(w.reshape(nsb, bps, bs, bs)).reshape(nsb, sbw, sbw)

    return bd(w_rg), bd(w_ig)


def _lru_blockdiag_grad(dwsb, nb, bs):
    nsb, sbw, _ = dwsb.shape
    bps = sbw // bs
    return _take_diag(dwsb.reshape(nsb, bps, bs, bps, bs)).reshape(nb, bs, bs)


def lru_conv_fwd(p, conv_w, conv_b, name):
    L = p.shape[0]
    E = conv_w.shape[1]
    tc = _tile(E, (256, 128))
    noff = E // tc
    kw = conv_w.shape[0]

    def body(xb_ref, w_ref, b_ref, xc_ref, xcb_ref):
        xb = xb_ref[...]
        rows = lax.broadcasted_iota(jnp.int32, xb.shape, 0)
        acc = w_ref[pl.ds(kw - 1, 1), :] * xb + b_ref[...]
        for k in range(kw - 1):
            sh = kw - 1 - k
            acc = acc + w_ref[pl.ds(k, 1), :] * jnp.where(rows >= sh, pltpu.roll(xb, sh, 0), 0.0)
        xc_ref[...] = acc
        xcb_ref[...] = acc.astype(BF16)

    return pl.pallas_call(
        body, name=name, grid=(noff,),
        in_specs=[pl.BlockSpec((L, tc), lambda t: (0, noff + t)),
                  pl.BlockSpec((kw, tc), lambda t: (0, t)), pl.BlockSpec((1, tc), lambda t: (0, t))],
        out_specs=[pl.BlockSpec((L, tc), lambda t: (0, t))] * 2,
        out_shape=[_sds((L, E), F32), _sds((L, E), BF16)],
        compiler_params=_cp(("parallel",)),
    )(p, conv_w, conv_b)


def lru_conv_bwd(d1, d2, d3, p, conv_w, name):
    L = p.shape[0]
    E = conv_w.shape[1]
    tc = _tile(E, (256, 128))
    noff = E // tc
    kw = conv_w.shape[0]

    def body(d1_ref, d2_ref, d3_ref, xb_ref, w_ref, dxb_ref, dw_ref, db_ref):
        dxc = d1_ref[...] + d2_ref[...] + d3_ref[...]
        xb = xb_ref[...]
        rows = lax.broadcasted_iota(jnp.int32, xb.shape, 0)
        db_ref[...] = _colsum(dxc)
        acc = w_ref[pl.ds(kw - 1, 1), :] * dxc
        dw_ref[pl.ds(kw - 1, 1), :] = _colsum(dxc * xb)
        for k in range(kw - 1):
            sh = kw - 1 - k
            dw_ref[pl.ds(k, 1), :] = _colsum(dxc * jnp.where(rows >= sh, pltpu.roll(xb, sh, 0), 0.0))
            acc = acc + w_ref[pl.ds(k, 1), :] * jnp.where(rows < L - sh, pltpu.roll(dxc, L - sh, 0), 0.0)
        dxb_ref[...] = acc.astype(BF16)

    return pl.pallas_call(
        body, name=name, grid=(noff,),
        in_specs=[pl.BlockSpec((L, tc), lambda t: (0, t))] * 3 +
                 [pl.BlockSpec((L, tc), lambda t: (0, noff + t)), pl.BlockSpec((kw, tc), lambda t: (0, t))],
        out_specs=[pl.BlockSpec((L, tc), lambda t: (0, t)), pl.BlockSpec((kw, tc), lambda t: (0, t)),
                   pl.BlockSpec((1, tc), lambda t: (0, t))],
        out_shape=[_sds((L, E), BF16), _sds((kw, E), F32), _sds((1, E), F32)],
        compiler_params=_cp(("parallel",)),
    )(d1, d2, d3, p, conv_w)


def _lru_gates(pr, pi_, brg, big, sp):
    r = _sig(pr + brg)
    ig = _sig(pi_ + big)
    la = -LRU_C * r * sp
    a = jnp.exp(la)
    mult = jnp.sqrt(_neg_expm1(2.0 * la))
    return r, ig, a, mult


def _lru_specs(L, E):
    tc = _tile(E, (LRU_TILE, LANES))
    col = pl.BlockSpec((L, tc), lambda t: (0, t))
    vec = pl.BlockSpec((1, tc), lambda t: (0, t))
    return tc, col, vec


def lru_scan_fwd(pre_r, pre_i, xc, p, b_rg, b_ig, lam, name):
    L, E = xc.shape
    tc, col, vec = _lru_specs(L, E)
    nblk = L // SUBLANES

    def body(pr_ref, pi_ref, xc_ref, gb_ref, brg_ref, big_ref, lam_ref, hs_ref, yv_ref):
        sp = _softplus(-lam_ref[...])
        brg, big = brg_ref[...], big_ref[...]
        rows = lax.broadcasted_iota(jnp.int32, (SUBLANES, tc), 0)

        def blk(k, carry):
            r0 = pl.multiple_of(k * SUBLANES, SUBLANES)
            sl = pl.ds(r0, SUBLANES)
            _, ig, a, mult = _lru_gates(pr_ref[sl, :], pi_ref[sl, :], brg, big, sp)
            b = mult * ig * xc_ref[sl, :]
            for d in (1, 2, 4):
                keep = rows >= d
                b = b + a * jnp.where(keep, pltpu.roll(b, d, 0), 0.0)
                a = a * jnp.where(keep, pltpu.roll(a, d, 0), 1.0)
            h = b + a * carry
            hs_ref[sl, :] = h
            return _row(h, SUBLANES - 1)

        def trip(kt, carry):
            for q in range(SCAN_UNROLL):
                carry = blk(kt * SCAN_UNROLL + q, carry)
            return carry

        lax.fori_loop(0, nblk // SCAN_UNROLL, trip, jnp.zeros((1, tc), F32))
        yv_ref[...] = (hs_ref[...] * _gelu(gb_ref[...])).astype(BF16)

    return pl.pallas_call(
        body, name=name, grid=(E // tc,),
        in_specs=[col, col, col, col, vec, vec, vec],
        out_specs=[col, col], out_shape=[_sds((L, E), F32), _sds((L, E), BF16)],
        compiler_params=_cp(("parallel",)),
    )(pre_r, pre_i, xc, p, b_rg, b_ig, lam)


def lru_scan_bwd(dyv, hs, pre_r, pre_i, xc, p, b_rg, b_ig, lam, name):
    L, E = xc.shape
    tc, col, vec = _lru_specs(L, E)
    nblk = L // SUBLANES

    def body(dyv_ref, hs_ref, pr_ref, pi_ref, xc_ref, gb_ref, brg_ref, big_ref, lam_ref,
             dgb_ref, dpr_ref, dpi_ref, dxc_ref, dbrg_ref, dbig_ref, dlam_ref, t_gb, t_pr, t_pi):
        lam_v = lam_ref[...]
        sp = _softplus(-lam_v)
        brg, big = brg_ref[...], big_ref[...]
        rows = lax.broadcasted_iota(jnp.int32, (SUBLANES, tc), 0)

        def blk(kk, carry):
            gcar, a_next, acc_sp, acc_r, acc_i = carry
            k = nblk - 1 - kk
            r0 = pl.multiple_of(k * SUBLANES, SUBLANES)
            sl = pl.ds(r0, SUBLANES)
            r, ig, a, mult = _lru_gates(pr_ref[sl, :], pi_ref[sl, :], brg, big, sp)
            gbv, hsv, dyvv, xcv = gb_ref[sl, :], hs_ref[sl, :], dyv_ref[sl, :], xc_ref[sl, :]
            t_gb[sl, :] = dyvv * hsv * _gelu_grad(gbv)
            x = dyvv * _gelu(gbv)
            al = jnp.where(rows == SUBLANES - 1, a_next, pltpu.roll(a, SUBLANES - 1, 0))
            for d in (1, 2, 4):
                keep = rows < SUBLANES - d
                x = x + al * jnp.where(keep, pltpu.roll(x, SUBLANES - d, 0), 0.0)
                al = al * jnp.where(keep, pltpu.roll(al, SUBLANES - d, 0), 1.0)
            g = x + al * gcar
            rp = pl.multiple_of(jnp.maximum(k - 1, 0) * SUBLANES, SUBLANES)
            hlast = _row(hs_ref[pl.ds(rp, SUBLANES), :], SUBLANES - 1) * (k > 0).astype(F32)
            hprev = jnp.where(rows == 0, hlast, pltpu.roll(hsv, 1, 0))
            da = g * hprev
            dmult = g * ig * xcv
            dig = g * mult * xcv
            dxc_ref[sl, :] = g * mult * ig
            dla = da * a - dmult * (a * a) / mult
            dpr = dla * (-LRU_C * sp) * r * (1.0 - r)
            dpi = dig * ig * (1.0 - ig)
            t_pr[sl, :] = dpr
            t_pi[sl, :] = dpi
            return (_row(g, 0), _row(a, 0), acc_sp + dla * (-LRU_C * r), acc_r + dpr, acc_i + dpi)

        zero = jnp.zeros((SUBLANES, tc), F32)
        z1 = jnp.zeros((1, tc), F32)
        def trip(kt, carry):
            for q in range(SCAN_UNROLL):
                carry = blk(kt * SCAN_UNROLL + q, carry)
            return carry

        _, _, acc_sp, acc_r, acc_i = lax.fori_loop(0, nblk // SCAN_UNROLL, trip, (z1, z1, zero, zero, zero))
        dgb_ref[...] = t_gb[...].astype(BF16)
        dpr_ref[...] = t_pr[...].astype(BF16)
        dpi_ref[...] = t_pi[...].astype(BF16)
        dbrg_ref[...] = _colsum(acc_r)
        dbig_ref[...] = _colsum(acc_i)
        dlam_ref[...] = -_colsum(acc_sp) * _sig(-lam_v)

    return pl.pallas_call(
        body, name=name, grid=(E // tc,),
        in_specs=[col, col, col, col, col, col, vec, vec, vec],
        out_specs=[col, col, col, col, vec, vec, vec],
        out_shape=[_sds((L, E), BF16), _sds((L, E), BF16), _sds((L, E), BF16), _sds((L, E), F32),
                   _sds((1, E), F32), _sds((1, E), F32), _sds((1, E), F32)],
        scratch_shapes=[pltpu.VMEM((L, tc), F32)] * 3,
        compiler_params=_cp(("parallel",)),
    )(dyv, hs, pre_r, pre_i, xc, p, b_rg, b_ig, lam)


def _place():
    xi, yi, ci = lax.axis_index("x"), lax.axis_index("y"), lax.axis_index("c")
    chips = [(1 - xi, yi), (xi, 1 - yi), (1 - xi, 1 - yi)]
    return xi, yi, ci, chips


_ANY = pl.BlockSpec(memory_space=pl.ANY)


def all_gather_devices(blks, name):
    n = len(blks)

    def body(*refs):
        ins, outs = refs[:n], refs[n:2 * n]
        send_sems, recv_sems, local_sems = refs[2 * n:]
        xi, yi, ci, chips = _place()
        me, sibling = (xi, yi, ci), (xi, yi, 1 - ci)

        def slab(a, px, py, pc):
            return outs[a].at[4 * px + 2 * py + pc]

        def copy(a, k, block, to, src=None):
            return pltpu.make_async_remote_copy(
                src_ref=slab(a, *block) if src is None else src, dst_ref=slab(a, *block),
                send_sem=send_sems.at[7 * a + k], recv_sem=recv_sems.at[7 * a + k], device_id=to,
                device_id_type=MESH)

        mine = [pltpu.make_async_copy(ins[a], slab(a, *me), local_sems.at[a]) for a in range(n)]
        first, passed = [], []
        for a in range(n):
            mine[a].start()
            first.append(copy(a, 0, me, sibling, src=ins[a]))
            first += [copy(a, 1 + j, me, (*chip, ci), src=ins[a]) for j, chip in enumerate(chips)]
        for cp in first:
            cp.start()
        for a in range(n):
            for j, chip in enumerate(chips):
                copy(a, 1 + j, (*chip, ci), me).wait_recv()
                passed.append(copy(a, 4 + j, (*chip, ci), sibling))
                passed[-1].start()
        for a in range(n):
            copy(a, 0, sibling, me).wait_recv()
            for j, chip in enumerate(chips):
                copy(a, 4 + j, (*chip, 1 - ci), me).wait_recv()
        for cp in first + passed:
            cp.wait_send()
        for cp in mine:
            cp.wait()

    return pl.pallas_call(
        body, name=name, in_specs=[_ANY] * n, out_specs=[_ANY] * n,
        out_shape=[_sds((N_DEV,) + b.shape, b.dtype) for b in blks],
        scratch_shapes=[pltpu.SemaphoreType.DMA((7 * n,)), pltpu.SemaphoreType.DMA((7 * n,)),
                        pltpu.SemaphoreType.DMA((n,))],
    )(*blks)


def all_gather_weights(shards, name):
    n = len(shards)

    def body(*refs):
        ins, outs = refs[:n], refs[n:2 * n]
        send_sems, recv_sems, local_sems = refs[2 * n:]
        xi, yi, ci, chips = _place()
        sibling = (xi, yi, 1 - ci)
        mychip = 2 * xi + yi

        def half(a, chip_idx, h):
            hr = shards[a].shape[0] // 2
            return outs[a].at[chip_idx, pl.ds(pl.multiple_of(h * hr, 16), hr), :]

        def over_ici(a, j, src, chip_idx):
            k = a * 6 + j
            return pltpu.make_async_remote_copy(
                src_ref=src, dst_ref=half(a, chip_idx, ci), send_sem=send_sems.at[k], recv_sem=recv_sems.at[k],
                device_id=(*chips[j], ci), device_id_type=MESH)

        def over_d2d(a, j, h):
            k = a * 6 + 3 + j
            chip_idx = 2 * chips[j][0] + chips[j][1]
            return pltpu.make_async_remote_copy(
                src_ref=half(a, chip_idx, h), dst_ref=half(a, chip_idx, h), send_sem=send_sems.at[k],
                recv_sem=recv_sems.at[k], device_id=sibling, device_id_type=MESH)

        local = [pltpu.make_async_copy(ins[a], outs[a].at[mychip], local_sems.at[a]) for a in range(n)]
        sends = []
        for a in range(n):
            local[a].start()
            hr = shards[a].shape[0] // 2
            src = ins[a].at[pl.ds(pl.multiple_of(ci * hr, 16), hr), :]
            for j in range(3):
                sends.append(over_ici(a, j, src, mychip))
                sends[-1].start()
        for a in range(n):
            for j in range(3):
                chip_idx = 2 * chips[j][0] + chips[j][1]
                over_ici(a, j, half(a, chip_idx, ci), chip_idx).wait_recv()
                sends.append(over_d2d(a, j, ci))
                sends[-1].start()
        for a in range(n):
            for j in range(3):
                over_d2d(a, j, 1 - ci).wait_recv()
        for cp in sends:
            cp.wait_send()
        for cp in local:
            cp.wait()

    return pl.pallas_call(
        body, name=name, in_specs=[_ANY] * n, out_specs=[_ANY] * n,
        out_shape=[_sds((N_CHIPS,) + s.shape, s.dtype) for s in shards],
        scratch_shapes=[pltpu.SemaphoreType.DMA((6 * n,)), pltpu.SemaphoreType.DMA((6 * n,)),
                        pltpu.SemaphoreType.DMA((n,))],
    )(*shards)


def exchange_halves(grads, name):
    n = len(grads)

    def body(*refs):
        ins, outs = refs[:n], refs[n:2 * n]
        send_sems, recv_sems = refs[2 * n:]
        xi, yi, ci, _ = _place()
        cps = []
        for a in range(n):
            hr = grads[a].shape[1] // 2
            src = ins[a].at[:, pl.ds(pl.multiple_of((1 - ci) * hr, 16), hr), :]
            cps.append(pltpu.make_async_remote_copy(
                src_ref=src, dst_ref=outs[a], send_sem=send_sems.at[a], recv_sem=recv_sems.at[a],
                device_id=(xi, yi, 1 - ci), device_id_type=MESH))
            cps[-1].start()
        for cp in cps:
            cp.wait()

    return pl.pallas_call(
        body, name=name, in_specs=[_ANY] * n, out_specs=[_ANY] * n,
        out_shape=[_sds((N_CHIPS, g.shape[1] // 2, g.shape[2]), g.dtype) for g in grads],
        scratch_shapes=[pltpu.SemaphoreType.DMA((n,)), pltpu.SemaphoreType.DMA((n,))],
    )(*grads)


def add_half(g, got, ci, name):
    S, hr, C = got.shape
    tr = _tile(hr, tuple(t for t in _TR if t * C <= ROW_TILE_ELEMS) or (16,))
    nb = hr // tr

    def body(c_ref, g_ref, r_ref, o_ref):
        o_ref[...] = (g_ref[...].astype(F32) + r_ref[...].astype(F32)).astype(BF16)

    return pl.pallas_call(
        body, name=name,
        grid_spec=pltpu.PrefetchScalarGridSpec(
            num_scalar_prefetch=1, grid=(S, nb),
            in_specs=[pl.BlockSpec((None, tr, C), lambda s, i, c: (s, c[0] * nb + i, 0)),
                      pl.BlockSpec((None, tr, C), lambda s, i, c: (s, i, 0))],
            out_specs=pl.BlockSpec((None, tr, C), lambda s, i, c: (s, i, 0))),
        out_shape=_sds((S, hr, C), BF16), compiler_params=_cp(("parallel", "parallel")),
    )(ci, g, got)


def exchange_chips(parts, name):
    n = len(parts)

    def body(*refs):
        ins, outs = refs[:n], refs[n:2 * n]
        send_sems, recv_sems = refs[2 * n:]
        xi, yi, ci, chips = _place()
        cps = []
        for a in range(n):
            for j in range(3):
                k = 3 * a + j
                cps.append(pltpu.make_async_remote_copy(
                    src_ref=ins[a].at[2 * chips[j][0] + chips[j][1]], dst_ref=outs[a].at[j],
                    send_sem=send_sems.at[k], recv_sem=recv_sems.at[k],
                    device_id=(*chips[j], ci), device_id_type=MESH))
                cps[-1].start()
        for cp in cps:
            cp.wait()

    return pl.pallas_call(
        body, name=name, in_specs=[_ANY] * n, out_specs=[_ANY] * n,
        out_shape=[_sds((3,) + p.shape[1:], p.dtype) for p in parts],
        scratch_shapes=[pltpu.SemaphoreType.DMA((3 * n,)), pltpu.SemaphoreType.DMA((3 * n,))],
    )(*parts)


def add_chips(part, got, chip, name):
    S, hr, C = part.shape
    tr = _tile(hr, tuple(t for t in _TR if t * C <= ROW_TILE_ELEMS) or (16,))

    def body(c_ref, p_ref, r_ref, o_ref):
        acc = p_ref[...].astype(F32)
        for j in range(3):
            acc = acc + r_ref[j].astype(F32)
        o_ref[...] = acc

    return pl.pallas_call(
        body, name=name,
        grid_spec=pltpu.PrefetchScalarGridSpec(
            num_scalar_prefetch=1, grid=(hr // tr,),
            in_specs=[pl.BlockSpec((None, tr, C), lambda i, c: (c[0], i, 0)),
                      pl.BlockSpec((3, tr, C), lambda i, c: (0, i, 0))],
            out_specs=pl.BlockSpec((tr, C), lambda i, c: (i, 0))),
        out_shape=_sds((hr, C), F32), compiler_params=_cp(("parallel",)),
    )(chip, part, got)


def join_halves(halves, name):
    n = len(halves)

    def body(*refs):
        ins, outs = refs[:n], refs[n:2 * n]
        send_sems, recv_sems, local_sems = refs[2 * n:]
        xi, yi, ci, _ = _place()
        cps, loc = [], []
        for a in range(n):
            hr = halves[a].shape[0]
            rows = outs[a].at[pl.ds(pl.multiple_of(ci * hr, 8), hr), :]
            loc.append(pltpu.make_async_copy(ins[a], rows, local_sems.at[a]))
            loc[-1].start()
            cps.append(pltpu.make_async_remote_copy(
                src_ref=ins[a], dst_ref=rows, send_sem=send_sems.at[a], recv_sem=recv_sems.at[a],
                device_id=(xi, yi, 1 - ci), device_id_type=MESH))
            cps[-1].start()
        for a in range(n):
            hr = halves[a].shape[0]
            other = outs[a].at[pl.ds(pl.multiple_of((1 - ci) * hr, 8), hr), :]
            pltpu.make_async_remote_copy(
                src_ref=ins[a], dst_ref=other, send_sem=send_sems.at[a], recv_sem=recv_sems.at[a],
                device_id=(xi, yi, 1 - ci), device_id_type=MESH).wait_recv()
        for cp in cps:
            cp.wait_send()
        for cp in loc:
            cp.wait()

    return pl.pallas_call(
        body, name=name, in_specs=[_ANY] * n, out_specs=[_ANY] * n,
        out_shape=[_sds((2 * h.shape[0], h.shape[1]), h.dtype) for h in halves],
        scratch_shapes=[pltpu.SemaphoreType.DMA((n,)), pltpu.SemaphoreType.DMA((n,)),
                        pltpu.SemaphoreType.DMA((n,))],
    )(*halves)


def reduce_scatter_grads(grads, ci1, chip1):
    got = exchange_halves(grads, "rs_exchange_halves")
    parts = [add_half(g, r, ci1, f"rs_add_half_{a}") for a, (g, r) in enumerate(zip(grads, got))]
    got2 = exchange_chips(parts, "rs_exchange_chips")
    halves = [add_chips(p, r, chip1, f"rs_add_chips_{a}") for a, (p, r) in enumerate(zip(parts, got2))]
    return join_halves(halves, "rs_join_halves")


def _pack(parts, width):
    flat = jnp.concatenate([p.reshape(-1).astype(F32) for p in parts])
    per = SUBLANES * width
    total = -(-flat.shape[0] // per) * per
    flat = jnp.pad(flat, (0, total - flat.shape[0]))
    return flat.reshape(total // width, width)


def _unpack(flat, shapes):
    out, off = [], 0
    for s in shapes:
        n = math.prod(s)
        out.append(flat[off:off + n].reshape(s))
        off += n
    return out


_W_NAMES = ['norm_g', 'w_ada', 'b_ada', 's5_w_in', 's5_lam_re', 's5_lam_im', 's5_log_dt', 's5_b_re', 's5_b_im',
            's5_c_re', 's5_c_im', 's5_d', 's5_w_glu', 'lru_w_in', 'lru_conv_w', 'lru_conv_b', 'lru_w_rg',
            'lru_b_rg', 'lru_w_ig', 'lru_b_ig', 'lru_lam', 'lru_w_out', 'ffn_w_gu', 'ffn_w_down', 'final_g']
_BIG = ('w_ada', 's5_w_in', 's5_w_glu', 'lru_w_in', 'lru_w_out', 'ffn_w_gu', 'ffn_w_down')
_MID = ('s5_b_re', 's5_b_im', 's5_c_re', 's5_c_im', 'lru_w_rg', 'lru_w_ig')


def _ffn_fwd(x, h, w_gu, w_down, gate, tag):
    gu = mm_nn(h, w_gu, name=f"{tag}_gu", out_dtype=BF16, bmode="cols")
    act = swiglu_fwd(gu, f"{tag}_act")
    z = mm_nn(act, w_down, name=f"{tag}_down")
    return res_gate_fwd(x, z, gate, f"{tag}_res"), (gu, act, z)


def _ffn_bwd(dx, h, saved, w_gu, w_down, gate, tag):
    gu, act, z = saved
    dz, dgate = res_gate_bwd(dx, z, gate, f"{tag}_res_bwd")
    dact = mm_nt(dz, w_down, name=f"{tag}_dact")
    dw_down = mm_tn(act, dz, name=f"{tag}_dwdown", out_dtype=BF16)
    dgu = swiglu_bwd(dact, gu, f"{tag}_act_bwd")
    dh = mm_nt(dgu, w_gu, name=f"{tag}_dh", bmode="cols")
    dw_gu = mm_tn(h, dgu, name=f"{tag}_dwgu", out_dtype=BF16, omode="cols", groups=N_CHIPS)
    return dh, dgate, dw_gu, dw_down.reshape((N_CHIPS, -1) + dw_down.shape[1:])


def _step(p):
    xi, yi, ci = lax.axis_index("x"), lax.axis_index("y"), lax.axis_index("c")
    chip = 2 * xi + yi
    me = 2 * chip + ci
    ci1 = jnp.reshape(ci, (1,)).astype(jnp.int32)
    chip1 = jnp.reshape(chip, (1,)).astype(jnp.int32)

    x0 = p['x'][0]
    tgt = p['loss_target'][0]
    L, D = x0.shape
    Dq = D // N_CHIPS
    depth = p['w_ada'].shape[0]
    E = p['lru_lam'].shape[1] * N_CHIPS
    Eq = E // N_CHIPS
    kw = p['lru_conv_w'].shape[1]
    Nq = p['w_ada'].shape[2]
    _, G, P, Cg = p['s5_b_re'].shape
    nb, bs = p['lru_w_rg'].shape[1], p['lru_w_rg'].shape[2]

    pay = _pack([p['c'], p['norm_g'], p['lru_conv_w'], p['lru_conv_b'], p['lru_b_rg'], p['lru_b_ig'],
                 p['lru_lam']], 1024)
    g1 = all_gather_devices([pay], "gather_small_params")[0].reshape(N_DEV, -1)
    c_all = g1[:, :D]
    per_chip = g1[0::2]
    sizes = [(depth, 2, Dq), (kw, Eq), (Eq,), (Eq,), (Eq,), (Eq,)]
    offs = D
    pieces = []
    for s in sizes:
        nel = math.prod(s)
        pieces.append(per_chip[:, offs:offs + nel].reshape((N_CHIPS,) + s))
        offs += nel
    norm_g = jnp.moveaxis(pieces[0], 0, 2).reshape(depth, 2, D)
    conv_w = jnp.moveaxis(pieces[1], 0, 1).reshape(kw, E)
    conv_b, b_rg, b_ig, lam = [q.reshape(1, E) for q in pieces[2:]]

    cond = silu_rows(jnp.pad(c_all, ((0, 16 - N_DEV), (0, 0))), "cond_silu")
    cond_rep = jnp.concatenate([cond] * depth, axis=1)
    mod_part = mm_nn(cond_rep, p['w_ada'], name="mod_proj", bmode="batch")[:N_DEV]
    g2 = all_gather_devices([mod_part], "gather_mod")[0][0::2]
    mine = lax.dynamic_index_in_dim(g2, me, axis=1, keepdims=False).reshape(N_CHIPS, depth, Nq)
    mod = jnp.moveaxis(mine, 0, 1).reshape(depth, N_CHIPS * Nq) + p['b_ada']
    mods = [[mod[i:i + 1, k * D:(k + 1) * D] for k in range(N_MOD)] for i in range(depth)]

    shards = [p['s5_w_in'][0], p['s5_w_glu'][0], p['lru_w_in'][0], p['lru_w_out'][0],
              p['ffn_w_gu'][0], p['ffn_w_gu'][1], p['ffn_w_down'][0], p['ffn_w_down'][1]]
    gathered = all_gather_weights([s.astype(BF16) for s in shards], "gather_weights")
    s5_w_in = gathered[0].reshape(-1, D)
    s5_w_glu = gathered[1]
    lru_w_in = gathered[2]
    lru_w_out = gathered[3].reshape(-1, D)
    w_gu = [gathered[4], gathered[5]]
    w_down = [gathered[6].reshape(-1, D), gathered[7].reshape(-1, D)]

    s5_small = (p['s5_lam_re'][0], p['s5_lam_im'][0], p['s5_log_dt'][0], p['s5_b_re'][0], p['s5_b_im'][0])
    (ab_re, ab_im, bb_re, bb_im), s5_disc_vjp = jax.vjp(_s5_discretize, *s5_small)
    bsb, csb = _s5_blockdiag(bb_re, bb_im, p['s5_c_re'][0], p['s5_c_im'][0])
    cf, crv = _s5_scan_consts(ab_re, ab_im)
    wsb_rg, wsb_ig = [w.astype(BF16) for w in _lru_blockdiag(p['lru_w_rg'][0], p['lru_w_ig'][0])]
    nsb_lru = wsb_rg.shape[0]

    sh1, sc1, gt1, sh2, sc2, gt2 = mods[0]
    h0 = norm_mod_fwd(x0, norm_g[0, 0:1], sc1, sh1, "l0_norm1")
    u = mm_nn(h0, s5_w_in, name="s5_in")
    ypre, yg, ss = s5_scan_fwd(u, p['s5_d'], bsb, csb, cf, "s5_scan")
    v = mm_nn(yg, s5_w_glu, name="s5_glu", bmode="cols")
    x1 = glu_res_fwd(x0, v, gt1, "s5_res")
    h1 = norm_mod_fwd(x1, norm_g[0, 1:2], sc2, sh2, "l0_norm2")
    x2, ffn0 = _ffn_fwd(x1, h1, w_gu[0], w_down[0], gt2, "ffn0")

    sh1b, sc1b, gt1b, sh2b, sc2b, gt2b = mods[1]
    h2 = norm_mod_fwd(x2, norm_g[1, 0:1], sc1b, sh1b, "l1_norm1")
    pq = mm_nn(h2, lru_w_in, name="lru_in", bmode="cols")
    xc, xcb = lru_conv_fwd(pq, conv_w, conv_b, "lru_conv")
    pre_r = mm_nn(xcb, wsb_rg, name="lru_gate_r", bmode="batch")
    pre_i = mm_nn(xcb, wsb_ig, name="lru_gate_i", bmode="batch")
    hs, yv = lru_scan_fwd(pre_r, pre_i, xc, pq, b_rg, b_ig, lam, "lru_scan")
    mix = mm_nn(yv, lru_w_out, name="lru_out")
    x3 = res_gate_fwd(x2, mix, gt1b, "lru_res")
    h3 = norm_mod_fwd(x3, norm_g[1, 1:2], sc2b, sh2b, "l1_norm2")
    x4, ffn1 = _ffn_fwd(x3, h3, w_gu[1], w_down[1], gt2b, "ffn1")

    fg = p['final_g'].reshape(1, D)
    dx4, loss_blk, dfinal_g = final_loss(x4, fg, tgt, "final_loss")
    loss = lax.psum(loss_blk[0, 0], ("x", "y", "c"))

    dh3, dgt2b, dw_gu1, dw_down1 = _ffn_bwd(dx4, h3, ffn1, w_gu[1], w_down[1], gt2b, "ffn1")
    dx3, dgn11, dsc2b, dsh2b = norm_mod_bwd(dh3, x3, norm_g[1, 1:2], sc2b, dx4, "l1_norm2_bwd")

    dmix, dgt1b = res_gate_bwd(dx3, mix, gt1b, "lru_res_bwd")
    dyv = mm_nt(dmix, lru_w_out, name="lru_dyv")
    dw_out = mm_tn(yv, dmix, name="lru_dwout", out_dtype=BF16)
    dgb, dpre_r, dpre_i, dxc1, db_rg, db_ig, dlam = lru_scan_bwd(dyv, hs, pre_r, pre_i, xc, pq, b_rg, b_ig, lam,
                                                                "lru_scan_bwd")
    dxc2 = mm_nt(dpre_r, wsb_rg, name="lru_dxc_r", bmode="batch")
    dxc3 = mm_nt(dpre_i, wsb_ig, name="lru_dxc_i", bmode="batch")
    dwsb_rg = mm_tn(xcb, dpre_r, name="lru_dwgate_r", omode="batch", groups=nsb_lru)
    dwsb_ig = mm_tn(xcb, dpre_i, name="lru_dwgate_i", omode="batch", groups=nsb_lru)
    dxb, dconv_w, dconv_b = lru_conv_bwd(dxc1, dxc2, dxc3, pq, conv_w, "lru_conv_bwd")
    dpq = jnp.concatenate([dgb, dxb], axis=1)
    dh2 = mm_nt(dpq, lru_w_in, name="lru_dh", bmode="cols")
    dw_lru_in = mm_tn(h2, dpq, name="lru_dwin", out_dtype=BF16, omode="cols", groups=N_CHIPS)
    dx2, dgn10, dsc1b, dsh1b = norm_mod_bwd(dh2, x2, norm_g[1, 0:1], sc1b, dx3, "l1_norm1_bwd")

    dh1, dgt2, dw_gu0, dw_down0 = _ffn_bwd(dx2, h1, ffn0, w_gu[0], w_down[0], gt2, "ffn0")
    dx1, dgn01, dsc2, dsh2 = norm_mod_bwd(dh1, x1, norm_g[0, 1:2], sc2, dx2, "l0_norm2_bwd")

    dv, dgt1 = glu_res_bwd(dx1, v, gt1, "s5_res_bwd")
    dyg = mm_nt(dv, s5_w_glu, name="s5_dyg", bmode="cols")
    dw_glu = mm_tn(yg, dv, name="s5_dwglu", out_dtype=BF16, omode="cols", groups=N_CHIPS)
    du, dbsb, dcsb, da, dd = s5_scan_bwd(u, dyg, ypre, p['s5_d'], bsb, csb, cf, crv, ss, "s5_scan_bwd")
    dh0 = mm_nt(du, s5_w_in, name="s5_dh")
    dw_s5_in = mm_tn(h0, du, name="s5_dwin", out_dtype=BF16)
    grad_x, dgn00, dsc1, dsh1 = norm_mod_bwd(dh0, x0, norm_g[0, 0:1], sc1, dx1, "l0_norm1_bwd")

    def rows4(g):
        return g.reshape((N_CHIPS, -1) + g.shape[1:])

    big_parts = [rows4(dw_s5_in), dw_glu, dw_lru_in, rows4(dw_out), dw_gu0, dw_gu1, dw_down0, dw_down1]
    big = reduce_scatter_grads(big_parts, ci1, chip1)
    g_s5_w_in, g_s5_w_glu, g_lru_w_in, g_lru_w_out = big[0], big[1], big[2], big[3]
    g_ffn_w_gu = jnp.stack([big[4], big[5]])
    g_ffn_w_down = jnp.stack([big[6], big[7]])

    dmod = jnp.concatenate([jnp.concatenate([dsh1, dsc1, dgt1, dsh2, dsc2, dgt2], axis=1),
                            jnp.concatenate([dsh1b, dsc1b, dgt1b, dsh2b, dsc2b, dgt2b], axis=1)], axis=0)
    dnorm_g = jnp.stack([jnp.concatenate([dgn00, dgn01]), jnp.concatenate([dgn10, dgn11])])
    dbb_re, dbb_im, dc_re, dc_im = _s5_blockdiag_grads(dbsb, dcsb, P, Cg)
    H = S5_SB_GROUPS * P
    da_re, da_im = da[:, 0, :H].reshape(G, P), da[:, 0, H:].reshape(G, P)
    dw_rg, dw_ig = _lru_blockdiag_grad(dwsb_rg, nb, bs), _lru_blockdiag_grad(dwsb_ig, nb, bs)
    small = [dmod, dnorm_g, da_re, da_im, dd, dconv_w, dconv_b, db_rg, db_ig, dlam, dfinal_g]
    small_shapes = [s.shape for s in small]
    payload = _pack(small, 1024)
    mid = [dbb_re, dbb_im, dc_re, dc_im, dw_rg, dw_ig]
    mid_shapes = [s.shape for s in mid]
    gathered = all_gather_devices([payload] + [s.reshape(-1, s.shape[-1]) for s in mid], "gather_small_grads")
    gathered_small = gathered[0]
    total = sum_devices(gathered_small, "sum_small_grads").reshape(-1)
    (s_dmod, s_norm_g, s_da_re, s_da_im, s_dd, s_conv_w, s_conv_b, s_b_rg, s_b_ig, s_lam,
     s_final_g) = _unpack(total, small_shapes)
    s_dbb_re, s_dbb_im, s_dc_re, s_dc_im, s_dw_rg, s_dw_ig = [
        sum_devices(g, f"sum_mid_grads_{i}").reshape(s) for i, (g, s) in enumerate(zip(gathered[1:], mid_shapes))]
    g_lam_re, g_lam_im, g_log_dt, g_b_re, g_b_im = s5_disc_vjp((s_da_re, s_da_im, s_dbb_re, s_dbb_im))

    npay = payload.shape[0] * payload.shape[1]
    dmod_all = gathered_small.reshape(N_DEV, npay)[:, :depth * N_MOD * D].reshape(N_DEV, depth, N_CHIPS, Nq)
    dmod_mine = lax.dynamic_index_in_dim(dmod_all, chip, axis=2, keepdims=False).reshape(N_DEV, depth * Nq)
    dmod_mine = jnp.pad(dmod_mine, ((0, 16 - N_DEV), (0, 0)))
    g_w_ada = mm_tn(cond_rep, dmod_mine, name="w_ada_grad", omode="batch", groups=depth)

    def cols(full, width):
        return lax.dynamic_slice_in_dim(full, chip * width, width, axis=full.ndim - 1)

    grads = {
        'norm_g': cols(s_norm_g, Dq), 'w_ada': g_w_ada, 'b_ada': s_dmod,
        's5_w_in': g_s5_w_in[None], 's5_lam_re': g_lam_re[None], 's5_lam_im': g_lam_im[None],
        's5_log_dt': g_log_dt[None], 's5_b_re': g_b_re[None], 's5_b_im': g_b_im[None],
        's5_c_re': s_dc_re[None], 's5_c_im': s_dc_im[None], 's5_d': s_dd, 's5_w_glu': g_s5_w_glu[None],
        'lru_w_in': g_lru_w_in[None], 'lru_conv_w': cols(s_conv_w, Eq)[None, :, None, :],
        'lru_conv_b': cols(s_conv_b, Eq), 'lru_w_rg': s_dw_rg[None], 'lru_b_rg': cols(s_b_rg, Eq),
        'lru_w_ig': s_dw_ig[None], 'lru_b_ig': cols(s_b_ig, Eq), 'lru_lam': cols(s_lam, Eq),
        'lru_w_out': g_lru_w_out[None], 'ffn_w_gu': g_ffn_w_gu, 'ffn_w_down': g_ffn_w_down,
        'final_g': s_final_g.reshape(-1),
    }
    grads = {k: grads[k].reshape(p[k].shape) for k in _W_NAMES}

    delta, new_m, new_v = {}, {}, {}
    for k in _BIG + _MID:
        w2 = p[k].reshape(-1, p[k].shape[-1])
        outs = adamw(w2, grads[k].reshape(w2.shape), p['m_' + k].reshape(w2.shape), p['v_' + k].reshape(w2.shape),
                     f"adamw_{k}")
        delta[k], new_m[k], new_v[k] = [o.reshape(p[k].shape) for o in outs]
    rest = [k for k in _W_NAMES if k not in _BIG + _MID]
    shapes = [p[k].shape for k in rest]
    packed = [_pack([src[pre_ + k] if pre_ else src[k] for k in rest], 1024)
              for src, pre_ in ((p, ''), (grads, ''), (p, 'm_'), (p, 'v_'))]
    outs = adamw(*packed, "adamw_small")
    for dst, o in zip((delta, new_m, new_v), outs):
        for k, val in zip(rest, _unpack(o.reshape(-1), shapes)):
            dst[k] = val

    return (loss, grad_x[None], *[grads[k] for k in _W_NAMES], *[delta[k] for k in _W_NAMES],
            *[new_m[k] for k in _W_NAMES], *[new_v[k] for k in _W_NAMES])


_IN_NAMES = (['x', 'c'] + _W_NAMES + ['loss_target'] + ['m_' + k for k in _W_NAMES] + ['v_' + k for k in _W_NAMES])


def kernel(x, c, norm_g, w_ada, b_ada, s5_w_in, s5_lam_re, s5_lam_im, s5_log_dt, s5_b_re, s5_b_im, s5_c_re, s5_c_im, s5_d, s5_w_glu, lru_w_in, lru_conv_w, lru_conv_b, lru_w_rg, lru_b_rg, lru_w_ig, lru_b_ig, lru_lam, lru_w_out, ffn_w_gu, ffn_w_down, final_g, loss_target, m_norm_g, m_w_ada, m_b_ada, m_s5_w_in, m_s5_lam_re, m_s5_lam_im, m_s5_log_dt, m_s5_b_re, m_s5_b_im, m_s5_c_re, m_s5_c_im, m_s5_d, m_s5_w_glu, m_lru_w_in, m_lru_conv_w, m_lru_conv_b, m_lru_w_rg, m_lru_b_rg, m_lru_w_ig, m_lru_b_ig, m_lru_lam, m_lru_w_out, m_ffn_w_gu, m_ffn_w_down, m_final_g, v_norm_g, v_w_ada, v_b_ada, v_s5_w_in, v_s5_lam_re, v_s5_lam_im, v_s5_log_dt, v_s5_b_re, v_s5_b_im, v_s5_c_re, v_s5_c_im, v_s5_d, v_s5_w_glu, v_lru_w_in, v_lru_conv_w, v_lru_conv_b, v_lru_w_rg, v_lru_b_rg, v_lru_w_ig, v_lru_b_ig, v_lru_lam, v_lru_w_out, v_ffn_w_gu, v_ffn_w_down, v_final_g):
    args = (x, c, norm_g, w_ada, b_ada, s5_w_in, s5_lam_re, s5_lam_im, s5_log_dt, s5_b_re, s5_b_im, s5_c_re, s5_c_im, s5_d, s5_w_glu, lru_w_in, lru_conv_w, lru_conv_b, lru_w_rg, lru_b_rg, lru_w_ig, lru_b_ig, lru_lam, lru_w_out, ffn_w_gu, ffn_w_down, final_g, loss_target, m_norm_g, m_w_ada, m_b_ada, m_s5_w_in, m_s5_lam_re, m_s5_lam_im, m_s5_log_dt, m_s5_b_re, m_s5_b_im, m_s5_c_re, m_s5_c_im, m_s5_d, m_s5_w_glu, m_lru_w_in, m_lru_conv_w, m_lru_conv_b, m_lru_w_rg, m_lru_b_rg, m_lru_w_ig, m_lru_b_ig, m_lru_lam, m_lru_w_out, m_ffn_w_gu, m_ffn_w_down, m_final_g, v_norm_g, v_w_ada, v_b_ada, v_s5_w_in, v_s5_lam_re, v_s5_lam_im, v_s5_log_dt, v_s5_b_re, v_s5_b_im, v_s5_c_re, v_s5_c_im, v_s5_d, v_s5_w_glu, v_lru_w_in, v_lru_conv_w, v_lru_conv_b, v_lru_w_rg, v_lru_b_rg, v_lru_w_ig, v_lru_b_ig, v_lru_lam, v_lru_w_out, v_ffn_w_gu, v_ffn_w_down, v_final_g)
    return _step(dict(zip(_IN_NAMES, args)))
```

```python
import functools
import math

import jax
import jax.numpy as jnp
from jax import lax
from jax.experimental import pallas as pl
from jax.experimental.pallas import tpu as pltpu

F32 = jnp.float32
BF16 = jnp.bfloat16
MESH = pl.DeviceIdType.MESH

EPS = 1e-6
LRU_C = 8.0
N_MOD = 6
ADAM_LR = 0.001
ADAM_B1 = 0.9
ADAM_B2 = 0.999
ADAM_EPS = 1e-08
ADAM_WD = 0.01
ADAM_STEP = 10

N_CHIPS = 4
N_DEV = 8
SUBLANES = 8
LANES = 128
S5_SB_GROUPS = 8
V7X_VMEM_LIMIT = 48 * 1024 * 1024
ROW_TILE_ELEMS = 512 * 1024
SCAN_UNROLL = 2
LRU_TILE = 256

_TM = (1024, 512, 256, 128, 64, 32, 16, 8)
_TN = (1024, 1408, 512, 384, 256, 128)
_TK = (512, 1408, 256, 128)
_TR = (256, 128, 64, 32, 16, 8)

_GELU_K0 = math.sqrt(2.0 / math.pi)
_GELU_K1 = 0.044715


def _tile(n, cands):
    for t in cands:
        if n % t == 0:
            return t
    return n


def _cp(sem=None):
    return pltpu.CompilerParams(dimension_semantics=sem, vmem_limit_bytes=V7X_VMEM_LIMIT)


def _sds(shape, dtype):
    return jax.ShapeDtypeStruct(shape, dtype)


def _sig(x):
    return 1.0 / (1.0 + jnp.exp(-x))


def _gelu(x):
    t = jnp.tanh(_GELU_K0 * (x + _GELU_K1 * x * x * x))
    return 0.5 * x * (1.0 + t)


def _gelu_grad(x):
    x2 = x * x
    t = jnp.tanh(_GELU_K0 * (x + _GELU_K1 * x * x2))
    return 0.5 * (1.0 + t) + 0.5 * x * (1.0 - t * t) * _GELU_K0 * (1.0 + 3.0 * _GELU_K1 * x2)


def _softplus(z):
    return jnp.maximum(z, 0.0) + jnp.log(1.0 + jnp.exp(-jnp.abs(z)))


def _neg_expm1(x):
    series = -x * (1.0 + x * (0.5 + x * (1.0 / 6.0 + x * (1.0 / 24.0))))
    return jnp.where(x > -0.05, series, 1.0 - jnp.exp(x))


def _row(x, r):
    return x[r:r + 1, :]


def _colsum(x):
    return jnp.sum(x, axis=0, keepdims=True)


_NN = (((1,), (0,)), ((), ()))
_NT = (((1,), (1,)), ((), ()))
_TN_DIMS = (((0,), (0,)), ((), ()))


def _mm_call(name, a, b, a_spec, b_spec, o_spec, grid, out_shape, acc_shape, dims):
    nk = grid[-1]
    kaxis = len(grid) - 1

    def body(a_ref, b_ref, o_ref, acc_ref):
        k = pl.program_id(kaxis)

        @pl.when(k == 0)
        def _():
            acc_ref[...] = jnp.zeros_like(acc_ref)

        acc_ref[...] += lax.dot_general(a_ref[...].astype(BF16), b_ref[...].astype(BF16), dims,
                                        preferred_element_type=F32)

        @pl.when(k == nk - 1)
        def _():
            o_ref[...] = acc_ref[...].astype(o_ref.dtype)

    return pl.pallas_call(
        body, name=name, grid=grid, in_specs=[a_spec, b_spec], out_specs=o_spec, out_shape=out_shape,
        scratch_shapes=[pltpu.VMEM(acc_shape, F32)],
        compiler_params=_cp(("parallel", "parallel", "parallel", "arbitrary")),
    )(a, b)


def mm_nn(a, b, *, name, out_dtype=F32, bmode="plain"):
    M = a.shape[0]
    if bmode == "plain":
        G, S = 1, 1
        K, Nc = b.shape
    elif bmode == "cols":
        G = 1
        S, K, Nc = b.shape
    else:
        S = 1
        G, K, Nc = b.shape
    tm, tn, tk = _tile(M, _TM), _tile(Nc, _TN), _tile(K, _TK)
    nkb, nnb = K // tk, Nc // tn
    ncol = S * nnb
    grid = (G, M // tm, ncol, nkb)
    a_spec = pl.BlockSpec((tm, tk), lambda g, i, j, k: (i, g * nkb + k))
    if bmode == "plain":
        b_spec = pl.BlockSpec((tk, tn), lambda g, i, j, k: (k, j))
    elif bmode == "cols":
        b_spec = pl.BlockSpec((None, tk, tn), lambda g, i, j, k: (j // nnb, k, j % nnb))
    else:
        b_spec = pl.BlockSpec((None, tk, tn), lambda g, i, j, k: (g, k, j))
    o_spec = pl.BlockSpec((tm, tn), lambda g, i, j, k: (i, g * ncol + j))
    return _mm_call(name, a, b, a_spec, b_spec, o_spec, grid, _sds((M, G * S * Nc), out_dtype), (tm, tn), _NN)


def mm_nt(a, b, *, name, out_dtype=F32, bmode="plain"):
    M = a.shape[0]
    if bmode == "plain":
        G, S = 1, 1
        Ko, Nc = b.shape
    elif bmode == "cols":
        G = 1
        S, Ko, Nc = b.shape
    else:
        S = 1
        G, Ko, Nc = b.shape
    tm, to, tc = _tile(M, _TM), _tile(Ko, _TN), _tile(Nc, _TK)
    npc = Nc // tc
    nc = S * npc
    nob = Ko // to
    grid = (G, M // tm, nob, nc)
    a_spec = pl.BlockSpec((tm, tc), lambda g, i, j, n: (i, g * nc + n))
    if bmode == "plain":
        b_spec = pl.BlockSpec((to, tc), lambda g, i, j, n: (j, n))
    elif bmode == "cols":
        b_spec = pl.BlockSpec((None, to, tc), lambda g, i, j, n: (n // npc, j, n % npc))
    else:
        b_spec = pl.BlockSpec((None, to, tc), lambda g, i, j, n: (g, j, n))
    o_spec = pl.BlockSpec((tm, to), lambda g, i, j, n: (i, g * nob + j))
    return _mm_call(name, a, b, a_spec, b_spec, o_spec, grid, _sds((M, G * Ko), out_dtype), (tm, to), _NT)


def mm_tn(a, b, *, name, out_dtype=F32, omode="plain", groups=1):
    L = a.shape[0]
    G = groups if omode == "batch" else 1
    S = groups if omode == "cols" else 1
    Mo, N = a.shape[1] // G, b.shape[1] // G
    Nc = N // S
    tm, tn, tl = _tile(Mo, _TM), _tile(Nc, _TN), _tile(L, _TK)
    nmb, nnb = Mo // tm, N // tn
    npj = Nc // tn
    grid = (G, nmb, nnb, L // tl)
    a_spec = pl.BlockSpec((tl, tm), lambda g, i, j, l: (l, g * nmb + i))
    b_spec = pl.BlockSpec((tl, tn), lambda g, i, j, l: (l, g * nnb + j))
    if omode == "plain":
        o_spec = pl.BlockSpec((tm, tn), lambda g, i, j, l: (i, j))
        oshape = (Mo, N)
    elif omode == "cols":
        o_spec = pl.BlockSpec((None, tm, tn), lambda g, i, j, l: (j // npj, i, j % npj))
        oshape = (S, Mo, Nc)
    else:
        o_spec = pl.BlockSpec((None, tm, tn), lambda g, i, j, l: (g, i, j))
        oshape = (G, Mo, N)
    return _mm_call(name, a, b, a_spec, b_spec, o_spec, grid, _sds(oshape, out_dtype), (tm, tn), _TN_DIMS)


def _row_call(name, body, row_ins, vec_ins, row_outs, acc_outs=(), after=None):
    if after is not None:
        n_in = len(row_ins) + len(vec_ins)
        inner = body

        def body(*refs):
            inner(*refs[:n_in], *refs[n_in + 1:])

        return _row_call_impl(name, body, row_ins, vec_ins, row_outs, acc_outs, [after])
    return _row_call_impl(name, body, row_ins, vec_ins, row_outs, acc_outs, [])


def _row_call_impl(name, body, row_ins, vec_ins, row_outs, acc_outs, extra):
    L = row_ins[0].shape[0]
    wmax = max([a.shape[1] for a in row_ins] + [w for w, _ in row_outs])
    tr = _tile(L, tuple(t for t in _TR if t * wmax <= ROW_TILE_ELEMS) or (SUBLANES,))
    in_specs = [pl.BlockSpec((tr, a.shape[1]), lambda i: (i, 0)) for a in row_ins]
    in_specs += [pl.BlockSpec(v.shape, lambda i, nd=v.ndim: (0,) * nd) for v in vec_ins]
    in_specs += [pl.BlockSpec(memory_space=pl.ANY) for _ in extra]
    out_shape = [_sds((L, w), dt) for w, dt in row_outs] + [_sds(s, dt) for s, dt in acc_outs]
    out_specs = [pl.BlockSpec((tr, w), lambda i: (i, 0)) for w, _ in row_outs]
    out_specs += [pl.BlockSpec(s, lambda i, nd=len(s): (0,) * nd) for s, _ in acc_outs]
    sem = ("arbitrary",) if acc_outs else ("parallel",)
    return pl.pallas_call(body, name=name, grid=(L // tr,), in_specs=in_specs, out_specs=out_specs,
                          out_shape=out_shape, compiler_params=_cp(sem))(*row_ins, *vec_ins, *extra)


def silu_rows(x, name, after=None):
    def body(x_ref, o_ref):
        v = x_ref[...]
        o_ref[...] = (v * _sig(v)).astype(o_ref.dtype)
    return _row_call(name, body, [x], [], [(x.shape[1], BF16)], after=after)[0]


def norm_mod_fwd(x, gain, sc, sh, name):
    def body(x_ref, g_ref, sc_ref, sh_ref, h_ref):
        v = x_ref[...]
        r = lax.rsqrt(jnp.mean(v * v, axis=-1, keepdims=True) + EPS)
        h_ref[...] = (v * r * g_ref[...] * (1.0 + sc_ref[...]) + sh_ref[...]).astype(BF16)
    return _row_call(name, body, [x], [gain, sc, sh], [(x.shape[1], BF16)])[0]


def norm_mod_bwd(dh, x, gain, sc, dres, name, after=None):
    D = x.shape[1]

    def body(dh_ref, x_ref, dres_ref, g_ref, sc_ref, dx_ref, dg_ref, dsc_ref, dsh_ref):
        @pl.when(pl.program_id(0) == 0)
        def _():
            dg_ref[...] = jnp.zeros_like(dg_ref)
            dsc_ref[...] = jnp.zeros_like(dsc_ref)
            dsh_ref[...] = jnp.zeros_like(dsh_ref)

        v = x_ref[...]
        dh_v = dh_ref[...]
        g = g_ref[...]
        r = lax.rsqrt(jnp.mean(v * v, axis=-1, keepdims=True) + EPS)
        xhat = v * r
        dn = dh_v * (1.0 + sc_ref[...])
        dsc_ref[...] += _colsum(dh_v * xhat * g)
        dsh_ref[...] += _colsum(dh_v)
        dg_ref[...] += _colsum(dn * xhat)
        t = dn * g
        dx_ref[...] = dres_ref[...] + r * (t - xhat * jnp.mean(t * xhat, axis=-1, keepdims=True))

    acc = [((1, D), F32)] * 3
    return _row_call(name, body, [dh, x, dres], [gain, sc], [(D, F32)], acc, after=after)


def final_loss(x, gain, tgt, name):
    D = x.shape[1]

    def body(x_ref, t_ref, g_ref, dx_ref, loss_ref, dg_ref, acc_ref):
        i = pl.program_id(0)

        @pl.when(i == 0)
        def _():
            dg_ref[...] = jnp.zeros_like(dg_ref)
            acc_ref[...] = jnp.zeros_like(acc_ref)

        v = x_ref[...]
        g = g_ref[...]
        r = lax.rsqrt(jnp.mean(v * v, axis=-1, keepdims=True) + EPS)
        xhat = v * r
        err = xhat * g - t_ref[...]
        acc_ref[...] += _colsum(err * err)
        dout = err * (1.0 / D)
        dg_ref[...] += _colsum(dout * xhat)
        t = dout * g
        dx_ref[...] = r * (t - xhat * jnp.mean(t * xhat, axis=-1, keepdims=True))

        @pl.when(i == pl.num_programs(0) - 1)
        def _():
            loss_ref[...] = jnp.zeros_like(loss_ref) + jnp.sum(acc_ref[...]) * (0.5 / D)

    return _row_call(name, body, [x, tgt], [gain], [(D, F32)],
                     [((SUBLANES, LANES), F32), ((1, D), F32), ((1, D), F32)])[:3]


def res_gate_fwd(x, z, g, name):
    def body(x_ref, z_ref, g_ref, o_ref):
        o_ref[...] = x_ref[...] + g_ref[...] * z_ref[...]
    return _row_call(name, body, [x, z], [g], [(x.shape[1], F32)])[0]


def res_gate_bwd(dx, z, g, name):
    D = dx.shape[1]

    def body(dx_ref, z_ref, g_ref, dz_ref, dg_ref):
        @pl.when(pl.program_id(0) == 0)
        def _():
            dg_ref[...] = jnp.zeros_like(dg_ref)
        d = dx_ref[...]
        dz_ref[...] = (g_ref[...] * d).astype(BF16)
        dg_ref[...] += _colsum(d * z_ref[...])
    return _row_call(name, body, [dx, z], [g], [(D, BF16)], [((1, D), F32)])


def glu_res_fwd(x, v, g, name):
    D = x.shape[1]

    def body(x_ref, v_ref, g_ref, o_ref):
        vv = v_ref[...]
        o_ref[...] = x_ref[...] + g_ref[...] * (vv[:, :D] * _sig(vv[:, D:]))
    return _row_call(name, body, [x, v], [g], [(D, F32)])[0]


def glu_res_bwd(dx, v, g, name):
    D = dx.shape[1]

    def body(dx_ref, v_ref, g_ref, dv_ref, dg_ref):
        @pl.when(pl.program_id(0) == 0)
        def _():
            dg_ref[...] = jnp.zeros_like(dg_ref)
        d = dx_ref[...]
        vv = v_ref[...]
        val = vv[:, :D]
        s = _sig(vv[:, D:])
        dg_ref[...] += _colsum(d * val * s)
        dm = g_ref[...] * d
        dv_ref[:, :D] = (dm * s).astype(BF16)
        dv_ref[:, D:] = (dm * val * s * (1.0 - s)).astype(BF16)
    return _row_call(name, body, [dx, v], [g], [(2 * D, BF16)], [((1, D), F32)])


def swiglu_fwd(gu, name):
    F = gu.shape[1] // 2

    def body(gu_ref, o_ref):
        v = gu_ref[...].astype(F32)
        g = v[:, :F]
        o_ref[...] = (g * _sig(g) * v[:, F:]).astype(BF16)
    return _row_call(name, body, [gu], [], [(F, BF16)])[0]


def swiglu_bwd(dact, gu, name):
    F = gu.shape[1] // 2

    def body(da_ref, gu_ref, o_ref):
        v = gu_ref[...].astype(F32)
        g, u = v[:, :F], v[:, F:]
        da = da_ref[...]
        s = _sig(g)
        o_ref[:, :F] = (da * u * s * (1.0 + g * (1.0 - s))).astype(BF16)
        o_ref[:, F:] = (da * g * s).astype(BF16)
    return _row_call(name, body, [dact, gu], [], [(2 * F, BF16)])[0]


def adamw(w, g, m, v, name):
    C = w.shape[1]
    c1 = 1.0 - ADAM_B1 ** ADAM_STEP
    c2 = 1.0 - ADAM_B2 ** ADAM_STEP

    def body(w_ref, g_ref, m_ref, v_ref, d_ref, m2_ref, v2_ref):
        gv = g_ref[...]
        m2 = ADAM_B1 * m_ref[...] + (1.0 - ADAM_B1) * gv
        v2 = ADAM_B2 * v_ref[...] + (1.0 - ADAM_B2) * (gv * gv)
        m2_ref[...] = m2
        v2_ref[...] = v2
        d_ref[...] = -ADAM_LR * ((m2 / c1) / (jnp.sqrt(v2 / c2) + ADAM_EPS) + ADAM_WD * w_ref[...])
    return _row_call(name, body, [w, g, m, v], [], [(C, F32)] * 3)


def sum_devices(parts, name):
    n, R, C = parts.shape
    tr = _tile(R, tuple(t for t in _TR if t * C * n <= 4 * ROW_TILE_ELEMS) or (SUBLANES,))

    def body(p_ref, o_ref):
        acc = p_ref[0]
        for d in range(1, n):
            acc = acc + p_ref[d]
        o_ref[...] = acc
    return pl.pallas_call(body, name=name, grid=(R // tr,),
                          in_specs=[pl.BlockSpec((n, tr, C), lambda i: (0, i, 0))],
                          out_specs=pl.BlockSpec((tr, C), lambda i: (i, 0)), out_shape=_sds((R, C), F32),
                          compiler_params=_cp(("parallel",)))(parts)


def _s5_discretize(lam_re, lam_im, log_dt, b_re, b_im):
    dt = jnp.exp(log_dt)[:, None]
    mag = jnp.exp(lam_re * dt)
    ab_re = mag * jnp.cos(lam_im * dt)
    ab_im = mag * jnp.sin(lam_im * dt)
    nr, ni = ab_re - 1.0, ab_im
    den = lam_re * lam_re + lam_im * lam_im
    f_re = (nr * lam_re + ni * lam_im) / den
    f_im = (ni * lam_re - nr * lam_im) / den
    bb_re = f_re[..., None] * b_re - f_im[..., None] * b_im
    bb_im = f_re[..., None] * b_im + f_im[..., None] * b_re
    return ab_re, ab_im, bb_re, bb_im


def _s5_blockdiag(bb_re, bb_im, c_re, c_im):
    G, P, Cg = bb_re.shape
    nsb = G // S5_SB_GROUPS

    def bmat(bb):
        t = jnp.swapaxes(bb.reshape(nsb, S5_SB_GROUPS, P, Cg), 2, 3)
        return _spread_diag(t).reshape(nsb, S5_SB_GROUPS * Cg, S5_SB_GROUPS * P)

    def cmat(cc):
        t = jnp.swapaxes(cc.reshape(nsb, S5_SB_GROUPS, Cg, P), 2, 3)
        return _spread_diag(t).reshape(nsb, S5_SB_GROUPS * P, S5_SB_GROUPS * Cg)

    bsb = jnp.concatenate([bmat(bb_re), bmat(bb_im)], axis=-1)
    csb = jnp.concatenate([cmat(c_re), -cmat(c_im)], axis=1)
    return bsb, csb


def _spread_diag(t):
    ng = t.shape[1]
    eye = jnp.eye(ng, dtype=t.dtype)
    return t[:, :, :, None, :] * eye[None, :, None, :, None]


def _take_diag(t):
    return jnp.stack([t[:, g, :, g, :] for g in range(t.shape[1])], axis=1)


def _s5_blockdiag_grads(dbsb, dcsb, P, Cg):
    nsb = dbsb.shape[0]
    db6 = dbsb.reshape(nsb, S5_SB_GROUPS, Cg, 2, S5_SB_GROUPS, P)
    dbb_re = jnp.swapaxes(_take_diag(db6[:, :, :, 0]), 2, 3).reshape(-1, P, Cg)
    dbb_im = jnp.swapaxes(_take_diag(db6[:, :, :, 1]), 2, 3).reshape(-1, P, Cg)
    dc6 = dcsb.reshape(nsb, 2, S5_SB_GROUPS, P, S5_SB_GROUPS, Cg)
    dc_re = jnp.swapaxes(_take_diag(dc6[:, 0]), 2, 3).reshape(-1, Cg, P)
    dc_im = -jnp.swapaxes(_take_diag(dc6[:, 1]), 2, 3).reshape(-1, Cg, P)
    return dbb_re, dbb_im, dc_re, dc_im


def _s5_scan_consts(ab_re, ab_im):
    G, P = ab_re.shape
    nsb = G // S5_SB_GROUPS
    H = S5_SB_GROUPS * P
    ar, ai = ab_re.reshape(nsb, 1, H), ab_im.reshape(nsb, 1, H)
    pows = [(ar, ai)]
    for _ in range(SUBLANES - 1):
        pr, pi_ = pows[-1]
        pows.append((pr * ar - pi_ * ai, pr * ai + pi_ * ar))
    rows = jnp.arange(SUBLANES).reshape(1, SUBLANES, 1)

    def masked(k, keep):
        pr, pi_ = pows[k - 1]
        return jnp.where(keep, pr, 0.0), jnp.where(keep, pi_, 0.0)

    def per_row(sel):
        pr = jnp.concatenate([pows[sel(r) - 1][0] for r in range(SUBLANES)], axis=1)
        pi_ = jnp.concatenate([pows[sel(r) - 1][1] for r in range(SUBLANES)], axis=1)
        return pr, pi_

    fwd = [masked(1, rows >= 1), masked(2, rows >= 2), masked(4, rows >= 4), per_row(lambda r: r + 1)]
    rev = [masked(1, rows < 7), masked(2, rows < 6), masked(4, rows < 4), per_row(lambda r: SUBLANES - r)]

    def pack(lst, conj):
        sgn = -1.0 if conj else 1.0
        return jnp.stack([jnp.concatenate([jnp.broadcast_to(pr, (nsb, SUBLANES, H)),
                                           sgn * jnp.broadcast_to(pi_, (nsb, SUBLANES, H))], axis=-1)
                          for pr, pi_ in lst], axis=1)

    return pack(fwd, False), pack(rev, True)


def _cmadd(xr, xi, ar, ai, yr, yi):
    return xr + ar * yr - ai * yi, xi + ar * yi + ai * yr


def _s5_scan_fwd_loop(src_ref, dst_ref, sp_ref, cf_ref, cr, ci, nblk, H):
    rows = lax.broadcasted_iota(jnp.int32, (SUBLANES, H), 0)

    def body(k, carry):
        cr, ci = carry
        r0 = pl.multiple_of(k * SUBLANES, SUBLANES)
        xr = src_ref[pl.ds(r0, SUBLANES), pl.ds(0, H)]
        xi = src_ref[pl.ds(r0, SUBLANES), pl.ds(H, H)]
        for idx, d in enumerate((1, 2, 4)):
            xr, xi = _cmadd(xr, xi, cf_ref[idx, :, pl.ds(0, H)], cf_ref[idx, :, pl.ds(H, H)],
                            pltpu.roll(xr, d, 0), pltpu.roll(xi, d, 0))
        xr, xi = _cmadd(xr, xi, cf_ref[3, :, pl.ds(0, H)], cf_ref[3, :, pl.ds(H, H)], cr, ci)
        dst_ref[pl.ds(r0, SUBLANES), pl.ds(0, H)] = xr
        dst_ref[pl.ds(r0, SUBLANES), pl.ds(H, H)] = xi
        if sp_ref is not None:
            sp_ref[pl.ds(r0, SUBLANES), pl.ds(0, H)] = jnp.where(rows == 0, cr, pltpu.roll(xr, 1, 0))
            sp_ref[pl.ds(r0, SUBLANES), pl.ds(H, H)] = jnp.where(rows == 0, ci, pltpu.roll(xi, 1, 0))
        return _row(xr, SUBLANES - 1), _row(xi, SUBLANES - 1)

    return lax.fori_loop(0, nblk, body, (cr, ci))


def s5_scan_fwd(u, d_skip, bsb, csb, cf, name):
    L, W = u.shape
    nsb, GW, H2 = bsb.shape
    H = H2 // 2
    Tc = _tile(L, (512, 256, 128, 64, 32, 16, 8))
    nch = L // Tc

    def body(u_ref, d_ref, b_ref, c_ref, cf_ref, ypre_ref, yg_ref, ss_ref, bu_scr, car_scr):
        @pl.when(pl.program_id(1) == 0)
        def _():
            car_scr[...] = jnp.zeros_like(car_scr)

        ss_ref[...] = car_scr[...]
        ub = u_ref[...]
        bu_scr[...] = jnp.dot(ub.astype(BF16), b_ref[...], preferred_element_type=F32)
        cr, ci = _s5_scan_fwd_loop(bu_scr, bu_scr, None, cf_ref, car_scr[:, pl.ds(0, H)], car_scr[:, pl.ds(H, H)],
                                   Tc // SUBLANES, H)
        car_scr[:, pl.ds(0, H)] = cr
        car_scr[:, pl.ds(H, H)] = ci
        ypre = jnp.dot(bu_scr[...].astype(BF16), c_ref[...], preferred_element_type=F32) + d_ref[...] * ub
        ypre_ref[...] = ypre
        yg_ref[...] = _gelu(ypre).astype(BF16)

    return pl.pallas_call(
        body, name=name, grid=(nsb, nch),
        in_specs=[pl.BlockSpec((Tc, GW), lambda j, i: (i, j)),
                  pl.BlockSpec((1, GW), lambda j, i: (0, j)),
                  pl.BlockSpec((None, GW, H2), lambda j, i: (j, 0, 0)),
                  pl.BlockSpec((None, H2, GW), lambda j, i: (j, 0, 0)),
                  pl.BlockSpec((None, 4, SUBLANES, H2), lambda j, i: (j, 0, 0, 0))],
        out_specs=[pl.BlockSpec((Tc, GW), lambda j, i: (i, j)),
                   pl.BlockSpec((Tc, GW), lambda j, i: (i, j)),
                   pl.BlockSpec((None, None, 1, H2), lambda j, i: (i, j, 0, 0))],
        out_shape=[_sds((L, W), F32), _sds((L, W), BF16), _sds((nch, nsb, 1, H2), F32)],
        scratch_shapes=[pltpu.VMEM((Tc, H2), F32), pltpu.VMEM((1, H2), F32)],
        compiler_params=_cp(("arbitrary", "arbitrary")),
    )(u, d_skip, bsb.astype(BF16), csb.astype(BF16), cf)


def s5_scan_bwd(u, dyg, ypre, d_skip, bsb, csb, cf, crv, ss, name):
    L, W = u.shape
    nsb, GW, H2 = bsb.shape
    H = H2 // 2
    Tc = _tile(L, (512, 256, 128, 64, 32, 16, 8))
    nch = L // Tc
    nblk = Tc // SUBLANES
    bsb_t = jnp.swapaxes(bsb, 1, 2).astype(BF16)
    csb_t = jnp.swapaxes(csb, 1, 2).astype(BF16)

    def body(u_ref, dyg_ref, yp_ref, d_ref, b_ref, bt_ref, ct_ref, cf_ref, crv_ref, ss_ref,
             du_ref, db_ref, dc_ref, da_ref, dd_ref, s_scr, sp_scr, g_scr, gcar_scr):
        @pl.when(pl.program_id(1) == 0)
        def _():
            gcar_scr[...] = jnp.zeros_like(gcar_scr)
            db_ref[...] = jnp.zeros_like(db_ref)
            dc_ref[...] = jnp.zeros_like(dc_ref)
            da_ref[...] = jnp.zeros_like(da_ref)
            dd_ref[...] = jnp.zeros_like(dd_ref)

        ub = u_ref[...]
        ubf = ub.astype(BF16)
        dyp = dyg_ref[...] * _gelu_grad(yp_ref[...])
        dypb = dyp.astype(BF16)
        dd_ref[...] += _colsum(dyp * ub)
        s_scr[...] = jnp.dot(ubf, b_ref[...], preferred_element_type=F32)
        _s5_scan_fwd_loop(s_scr, s_scr, sp_scr, cf_ref, ss_ref[:, pl.ds(0, H)], ss_ref[:, pl.ds(H, H)], nblk, H)
        g_scr[...] = jnp.dot(dypb, ct_ref[...], preferred_element_type=F32)

        def rev(kk, carry):
            gr, gi, acc_r, acc_i = carry
            r0 = pl.multiple_of((nblk - 1 - kk) * SUBLANES, SUBLANES)
            xr = g_scr[pl.ds(r0, SUBLANES), pl.ds(0, H)]
            xi = g_scr[pl.ds(r0, SUBLANES), pl.ds(H, H)]
            for idx, d in enumerate((1, 2, 4)):
                xr, xi = _cmadd(xr, xi, crv_ref[idx, :, pl.ds(0, H)], crv_ref[idx, :, pl.ds(H, H)],
                                pltpu.roll(xr, SUBLANES - d, 0), pltpu.roll(xi, SUBLANES - d, 0))
            xr, xi = _cmadd(xr, xi, crv_ref[3, :, pl.ds(0, H)], crv_ref[3, :, pl.ds(H, H)], gr, gi)
            g_scr[pl.ds(r0, SUBLANES), pl.ds(0, H)] = xr
            g_scr[pl.ds(r0, SUBLANES), pl.ds(H, H)] = xi
            spr = sp_scr[pl.ds(r0, SUBLANES), pl.ds(0, H)]
            spi = sp_scr[pl.ds(r0, SUBLANES), pl.ds(H, H)]
            return (_row(xr, 0), _row(xi, 0), acc_r + xr * spr + xi * spi, acc_i + xi * spr - xr * spi)

        zero = jnp.zeros((SUBLANES, H), F32)
        gr, gi, acc_r, acc_i = lax.fori_loop(
            0, nblk, rev, (gcar_scr[:, pl.ds(0, H)], gcar_scr[:, pl.ds(H, H)], zero, zero))
        gcar_scr[:, pl.ds(0, H)] = gr
        gcar_scr[:, pl.ds(H, H)] = gi
        da_ref[:, pl.ds(0, H)] += _colsum(acc_r)
        da_ref[:, pl.ds(H, H)] += _colsum(acc_i)
        gb = g_scr[...].astype(BF16)
        db_ref[...] += lax.dot_general(ubf, gb, _TN_DIMS, preferred_element_type=F32)
        dc_ref[...] += lax.dot_general(s_scr[...].astype(BF16), dypb, _TN_DIMS, preferred_element_type=F32)
        du_ref[...] = (jnp.dot(gb, bt_ref[...], preferred_element_type=F32) + d_ref[...] * dyp).astype(BF16)

    rmap = lambda j, i: (nch - 1 - i, j)
    return pl.pallas_call(
        body, name=name, grid=(nsb, nch),
        in_specs=[pl.BlockSpec((Tc, GW), rmap), pl.BlockSpec((Tc, GW), rmap), pl.BlockSpec((Tc, GW), rmap),
                  pl.BlockSpec((1, GW), lambda j, i: (0, j)),
                  pl.BlockSpec((None, GW, H2), lambda j, i: (j, 0, 0)),
                  pl.BlockSpec((None, H2, GW), lambda j, i: (j, 0, 0)),
                  pl.BlockSpec((None, GW, H2), lambda j, i: (j, 0, 0)),
                  pl.BlockSpec((None, 4, SUBLANES, H2), lambda j, i: (j, 0, 0, 0)),
                  pl.BlockSpec((None, 4, SUBLANES, H2), lambda j, i: (j, 0, 0, 0)),
                  pl.BlockSpec((None, None, 1, H2), lambda j, i: (nch - 1 - i, j, 0, 0))],
        out_specs=[pl.BlockSpec((Tc, GW), rmap),
                   pl.BlockSpec((None, GW, H2), lambda j, i: (j, 0, 0)),
                   pl.BlockSpec((None, H2, GW), lambda j, i: (j, 0, 0)),
                   pl.BlockSpec((None, 1, H2), lambda j, i: (j, 0, 0)),
                   pl.BlockSpec((1, GW), lambda j, i: (0, j))],
        out_shape=[_sds((L, W), BF16), _sds((nsb, GW, H2), F32), _sds((nsb, H2, GW), F32),
                   _sds((nsb, 1, H2), F32), _sds((1, W), F32)],
        scratch_shapes=[pltpu.VMEM((Tc, H2), F32), pltpu.VMEM((Tc, H2), F32), pltpu.VMEM((Tc, H2), F32),
                        pltpu.VMEM((1, H2), F32)],
        compiler_params=_cp(("arbitrary", "arbitrary")),
    )(u, dyg, ypre, d_skip, bsb.astype(BF16), bsb_t, csb_t, cf, crv, ss)


def _lru_blockdiag(w_rg, w_ig):
    nb, bs, _ = w_rg.shape
    sbw = bs * LANES // math.gcd(bs, LANES)
    bps = sbw // bs
    nsb = nb // bps

    def bd(w):
        return _spread_diag(w.reshape(nsb, bps, bs, bs)).reshape(nsb, sbw, sbw)

    return bd(w_rg), bd(w_ig)


def _lru_blockdiag_grad(dwsb, nb, bs):
    nsb, sbw, _ = dwsb.shape
    bps = sbw // bs
    return _take_diag(dwsb.reshape(nsb, bps, bs, bps, bs)).reshape(nb, bs, bs)


def lru_conv_fwd(p, conv_w, conv_b, name):
    L = p.shape[0]
    E = conv_w.shape[1]
    tc = _tile(E, (256, 128))
    noff = E // tc
    kw = conv_w.shape[0]

    def body(xb_ref, w_ref, b_ref, xc_ref, xcb_ref):
        xb = xb_ref[...]
        rows = lax.broadcasted_iota(jnp.int32, xb.shape, 0)
        acc = w_ref[pl.ds(kw - 1, 1), :] * xb + b_ref[...]
        for k in range(kw - 1):
            sh = kw - 1 - k
            acc = acc + w_ref[pl.ds(k, 1), :] * jnp.where(rows >= sh, pltpu.roll(xb, sh, 0), 0.0)
        xc_ref[...] = acc
        xcb_ref[...] = acc.astype(BF16)

    return pl.pallas_call(
        body, name=name, grid=(noff,),
        in_specs=[pl.BlockSpec((L, tc), lambda t: (0, noff + t)),
                  pl.BlockSpec((kw, tc), lambda t: (0, t)), pl.BlockSpec((1, tc), lambda t: (0, t))],
        out_specs=[pl.BlockSpec((L, tc), lambda t: (0, t))] * 2,
        out_shape=[_sds((L, E), F32), _sds((L, E), BF16)],
        compiler_params=_cp(("parallel",)),
    )(p, conv_w, conv_b)


def lru_conv_bwd(d1, d2, d3, p, conv_w, name):
    L = p.shape[0]
    E = conv_w.shape[1]
    tc = _tile(E, (256, 128))
    noff = E // tc
    kw = conv_w.shape[0]

    def body(d1_ref, d2_ref, d3_ref, xb_ref, w_ref, dxb_ref, dw_ref, db_ref):
        dxc = d1_ref[...] + d2_ref[...] + d3_ref[...]
        xb = xb_ref[...]
        rows = lax.broadcasted_iota(jnp.int32, xb.shape, 0)
        db_ref[...] = _colsum(dxc)
        acc = w_ref[pl.ds(kw - 1, 1), :] * dxc
        dw_ref[pl.ds(kw - 1, 1), :] = _colsum(dxc * xb)
        for k in range(kw - 1):
            sh = kw - 1 - k
            dw_ref[pl.ds(k, 1), :] = _colsum(dxc * jnp.where(rows >= sh, pltpu.roll(xb, sh, 0), 0.0))
            acc = acc + w_ref[pl.ds(k, 1), :] * jnp.where(rows < L - sh, pltpu.roll(dxc, L - sh, 0), 0.0)
        dxb_ref[...] = acc.astype(BF16)

    return pl.pallas_call(
        body, name=name, grid=(noff,),
        in_specs=[pl.BlockSpec((L, tc), lambda t: (0, t))] * 3 +
                 [pl.BlockSpec((L, tc), lambda t: (0, noff + t)), pl.BlockSpec((kw, tc), lambda t: (0, t))],
        out_specs=[pl.BlockSpec((L, tc), lambda t: (0, t)), pl.BlockSpec((kw, tc), lambda t: (0, t)),
                   pl.BlockSpec((1, tc), lambda t: (0, t))],
        out_shape=[_sds((L, E), BF16), _sds((kw, E), F32), _sds((1, E), F32)],
        compiler_params=_cp(("parallel",)),
    )(d1, d2, d3, p, conv_w)


def _lru_gates(pr, pi_, brg, big, sp):
    r = _sig(pr + brg)
    ig = _sig(pi_ + big)
    la = -LRU_C * r * sp
    a = jnp.exp(la)
    mult = jnp.sqrt(_neg_expm1(2.0 * la))
    return r, ig, a, mult


def _lru_specs(L, E):
    tc = _tile(E, (LRU_TILE, LANES))
    col = pl.BlockSpec((L, tc), lambda t: (0, t))
    vec = pl.BlockSpec((1, tc), lambda t: (0, t))
    return tc, col, vec


def lru_scan_fwd(pre_r, pre_i, xc, p, b_rg, b_ig, lam, name):
    L, E = xc.shape
    tc, col, vec = _lru_specs(L, E)
    nblk = L // SUBLANES

    def body(pr_ref, pi_ref, xc_ref, gb_ref, brg_ref, big_ref, lam_ref, hs_ref, yv_ref):
        sp = _softplus(-lam_ref[...])
        brg, big = brg_ref[...], big_ref[...]
        rows = lax.broadcasted_iota(jnp.int32, (SUBLANES, tc), 0)

        def blk(k, carry):
            r0 = pl.multiple_of(k * SUBLANES, SUBLANES)
            sl = pl.ds(r0, SUBLANES)
            _, ig, a, mult = _lru_gates(pr_ref[sl, :], pi_ref[sl, :], brg, big, sp)
            b = mult * ig * xc_ref[sl, :]
            for d in (1, 2, 4):
                keep = rows >= d
                b = b + a * jnp.where(keep, pltpu.roll(b, d, 0), 0.0)
                a = a * jnp.where(keep, pltpu.roll(a, d, 0), 1.0)
            h = b + a * carry
            hs_ref[sl, :] = h
            return _row(h, SUBLANES - 1)

        def trip(kt, carry):
            for q in range(SCAN_UNROLL):
                carry = blk(kt * SCAN_UNROLL + q, carry)
            return carry

        lax.fori_loop(0, nblk // SCAN_UNROLL, trip, jnp.zeros((1, tc), F32))
        yv_ref[...] = (hs_ref[...] * _gelu(gb_ref[...])).astype(BF16)

    return pl.pallas_call(
        body, name=name, grid=(E // tc,),
        in_specs=[col, col, col, col, vec, vec, vec],
        out_specs=[col, col], out_shape=[_sds((L, E), F32), _sds((L, E), BF16)],
        compiler_params=_cp(("parallel",)),
    )(pre_r, pre_i, xc, p, b_rg, b_ig, lam)


def lru_scan_bwd(dyv, hs, pre_r, pre_i, xc, p, b_rg, b_ig, lam, name):
    L, E = xc.shape
    tc, col, vec = _lru_specs(L, E)
    nblk = L // SUBLANES

    def body(dyv_ref, hs_ref, pr_ref, pi_ref, xc_ref, gb_ref, brg_ref, big_ref, lam_ref,
             dgb_ref, dpr_ref, dpi_ref, dxc_ref, dbrg_ref, dbig_ref, dlam_ref, t_gb, t_pr, t_pi):
        lam_v = lam_ref[...]
        sp = _softplus(-lam_v)
        brg, big = brg_ref[...], big_ref[...]
        rows = lax.broadcasted_iota(jnp.int32, (SUBLANES, tc), 0)

        def blk(kk, carry):
            gcar, a_next, acc_sp, acc_r, acc_i = carry
            k = nblk - 1 - kk
            r0 = pl.multiple_of(k * SUBLANES, SUBLANES)
            sl = pl.ds(r0, SUBLANES)
            r, ig, a, mult = _lru_gates(pr_ref[sl, :], pi_ref[sl, :], brg, big, sp)
            gbv, hsv, dyvv, xcv = gb_ref[sl, :], hs_ref[sl, :], dyv_ref[sl, :], xc_ref[sl, :]
            t_gb[sl, :] = dyvv * hsv * _gelu_grad(gbv)
            x = dyvv * _gelu(gbv)
            al = jnp.where(rows == SUBLANES - 1, a_next, pltpu.roll(a, SUBLANES - 1, 0))
            for d in (1, 2, 4):
                keep = rows < SUBLANES - d
                x = x + al * jnp.where(keep, pltpu.roll(x, SUBLANES - d, 0), 0.0)
                al = al * jnp.where(keep, pltpu.roll(al, SUBLANES - d, 0), 1.0)
            g = x + al * gcar
            rp = pl.multiple_of(jnp.maximum(k - 1, 0) * SUBLANES, SUBLANES)
            hlast = _row(hs_ref[pl.ds(rp, SUBLANES), :], SUBLANES - 1) * (k > 0).astype(F32)
            hprev = jnp.where(rows == 0, hlast, pltpu.roll(hsv, 1, 0))
            da = g * hprev
            dmult = g * ig * xcv
            dig = g * mult * xcv
            dxc_ref[sl, :] = g * mult * ig
            dla = da * a - dmult * (a * a) / mult
            dpr = dla * (-LRU_C * sp) * r * (1.0 - r)
            dpi = dig * ig * (1.0 - ig)
            t_pr[sl, :] = dpr
            t_pi[sl, :] = dpi
            return (_row(g, 0), _row(a, 0), acc_sp + dla * (-LRU_C * r), acc_r + dpr, acc_i + dpi)

        zero = jnp.zeros((SUBLANES, tc), F32)
        z1 = jnp.zeros((1, tc), F32)
        def trip(kt, carry):
            for q in range(SCAN_UNROLL):
                carry = blk(kt * SCAN_UNROLL + q, carry)
            return carry

        _, _, acc_sp, acc_r, acc_i = lax.fori_loop(0, nblk // SCAN_UNROLL, trip, (z1, z1, zero, zero, zero))
        dgb_ref[...] = t_gb[...].astype(BF16)
        dpr_ref[...] = t_pr[...].astype(BF16)
        dpi_ref[...] = t_pi[...].astype(BF16)
        dbrg_ref[...] = _colsum(acc_r)
        dbig_ref[...] = _colsum(acc_i)
        dlam_ref[...] = -_colsum(acc_sp) * _sig(-lam_v)

    return pl.pallas_call(
        body, name=name, grid=(E // tc,),
        in_specs=[col, col, col, col, col, col, vec, vec, vec],
        out_specs=[col, col, col, col, vec, vec, vec],
        out_shape=[_sds((L, E), BF16), _sds((L, E), BF16), _sds((L, E), BF16), _sds((L, E), F32),
                   _sds((1, E), F32), _sds((1, E), F32), _sds((1, E), F32)],
        scratch_shapes=[pltpu.VMEM((L, tc), F32)] * 3,
        compiler_params=_cp(("parallel",)),
    )(dyv, hs, pre_r, pre_i, xc, p, b_rg, b_ig, lam)


def _place():
    xi, yi, ci = lax.axis_index("x"), lax.axis_index("y"), lax.axis_index("c")
    chips = [(1 - xi, yi), (xi, 1 - yi), (1 - xi, 1 - yi)]
    return xi, yi, ci, chips


_ANY = pl.BlockSpec(memory_space=pl.ANY)


def all_gather_devices(blks, name):
    n = len(blks)

    def body(*refs):
        ins, outs = refs[:n], refs[n:2 * n]
        send_sems, recv_sems, local_sems = refs[2 * n:]
        xi, yi, ci, chips = _place()
        me, sibling = (xi, yi, ci), (xi, yi, 1 - ci)

        def slab(a, px, py, pc):
            return outs[a].at[4 * px + 2 * py + pc]

        def copy(a, k, block, to, src=None):
            return pltpu.make_async_remote_copy(
                src_ref=slab(a, *block) if src is None else src, dst_ref=slab(a, *block),
                send_sem=send_sems.at[7 * a + k], recv_sem=recv_sems.at[7 * a + k], device_id=to,
                device_id_type=MESH)

        mine = [pltpu.make_async_copy(ins[a], slab(a, *me), local_sems.at[a]) for a in range(n)]
        first, passed = [], []
        for a in range(n):
            mine[a].start()
            first.append(copy(a, 0, me, sibling, src=ins[a]))
            first += [copy(a, 1 + j, me, (*chip, ci), src=ins[a]) for j, chip in enumerate(chips)]
        for cp in first:
            cp.start()
        for a in range(n):
            for j, chip in enumerate(chips):
                copy(a, 1 + j, (*chip, ci), me).wait_recv()
                passed.append(copy(a, 4 + j, (*chip, ci), sibling))
                passed[-1].start()
        for a in range(n):
            copy(a, 0, sibling, me).wait_recv()
            for j, chip in enumerate(chips):
                copy(a, 4 + j, (*chip, 1 - ci), me).wait_recv()
        for cp in first + passed:
            cp.wait_send()
        for cp in mine:
            cp.wait()

    return pl.pallas_call(
        body, name=name, in_specs=[_ANY] * n, out_specs=[_ANY] * n,
        out_shape=[_sds((N_DEV,) + b.shape, b.dtype) for b in blks],
        scratch_shapes=[pltpu.SemaphoreType.DMA((7 * n,)), pltpu.SemaphoreType.DMA((7 * n,)),
                        pltpu.SemaphoreType.DMA((n,))],
    )(*blks)


_HBM = pl.BlockSpec(memory_space=pltpu.HBM)
_SEM = pl.BlockSpec(memory_space=pltpu.SEMAPHORE)
_EFFECT = pltpu.SideEffectType.DATAFLOW_SIDE_EFFECTING


def split_start(name, groups, counts, copies_fn):
    flat = [b for g in groups for b in g]
    n, ng = len(flat), len(groups)

    def body(*refs):
        ins, sems, token = refs[:n], refs[n:n + 2 * ng], refs[-1]
        off = 0
        for gi, g in enumerate(groups):
            for cp in copies_fn(ins[off:off + len(g)], [b.shape for b in g], sems[2 * gi], sems[2 * gi + 1]):
                cp.start()
            off += len(g)
        token[...] = jnp.zeros_like(token)

    out_shape = tuple(pltpu.SemaphoreType.DMA((c,)) for c in counts for _ in range(2))
    out_shape += tuple(pltpu.HBM(b.shape, b.dtype) for b in flat) + (_sds((SUBLANES, LANES), F32),)
    outs = pl.pallas_call(
        body, name=name, in_specs=[_HBM] * n, out_shape=out_shape,
        out_specs=tuple([_SEM] * (2 * ng) + [_HBM] * n + [pl.BlockSpec(memory_space=pltpu.VMEM)]),
        input_output_aliases={i: 2 * ng + i for i in range(n)},
        compiler_params=pltpu.CompilerParams(has_side_effects=_EFFECT),
    )(*[pltpu.with_memory_space_constraint(b, pltpu.HBM) for b in flat])
    sems = [(outs[2 * gi], outs[2 * gi + 1]) for gi in range(ng)]
    thru, off = [], 2 * ng
    for g in groups:
        thru.append(list(outs[off:off + len(g)]))
        off += len(g)
    return sems, thru, outs[-1]


def split_wait(name, bufs, sems, copies_fn, after):
    n = len(bufs)
    shapes = [b.shape for b in bufs]

    def body(*refs):
        for cp in copies_fn(refs[:n], shapes, refs[n], refs[n + 1]):
            cp.wait_send()
            cp.wait_recv()

    return list(pl.pallas_call(
        body, name=name, in_specs=[_HBM] * n + [_SEM, _SEM, _ANY],
        out_shape=tuple(pltpu.HBM(b.shape, b.dtype) for b in bufs), out_specs=tuple([_HBM] * n),
        input_output_aliases={i: i for i in range(n)},
        compiler_params=pltpu.CompilerParams(has_side_effects=_EFFECT),
    )(*bufs, sems[0], sems[1], after))


def _gather_copies(bufs, shapes, send_sems, recv_sems):
    xi, yi, ci, chips = _place()
    cps = []
    for a, ref in enumerate(bufs):
        hr = shapes[a][1] // 2
        rows = ref.at[2 * xi + yi, pl.ds(pl.multiple_of(ci * hr, 16), hr), :]
        for j in range(3):
            cps.append(pltpu.make_async_remote_copy(
                src_ref=rows, dst_ref=rows, send_sem=send_sems.at[3 * a + j], recv_sem=recv_sems.at[3 * a + j],
                device_id=(*chips[j], ci), device_id_type=MESH))
    return cps


def _scatter_copies(bufs, shapes, send_sems, recv_sems):
    xi, yi, ci, chips = _place()
    n = len(bufs) // 2
    cps = []
    for a in range(n):
        for j in range(3):
            cps.append(pltpu.make_async_remote_copy(
                src_ref=bufs[a].at[2 * chips[j][0] + chips[j][1]], dst_ref=bufs[n + a].at[j],
                send_sem=send_sems.at[3 * a + j], recv_sem=recv_sems.at[3 * a + j],
                device_id=(*chips[j], ci), device_id_type=MESH))
    return cps


def cast_place(w, layer, chip1, name):
    _, R, C = w.shape
    tr = _tile(R, tuple(t for t in _TR if t * C <= ROW_TILE_ELEMS) or (16,))

    def body(c_ref, w_ref, o_ref):
        o_ref[...] = w_ref[...].astype(BF16)

    return pl.pallas_call(
        body, name=name,
        grid_spec=pltpu.PrefetchScalarGridSpec(
            num_scalar_prefetch=1, grid=(R // tr,),
            in_specs=[pl.BlockSpec((None, tr, C), lambda i, c: (layer, i, 0))],
            out_specs=pl.BlockSpec((None, tr, C), lambda i, c: (c[0], i, 0))),
        out_shape=_sds((N_CHIPS, R, C), BF16), compiler_params=_cp(("parallel",)),
    )(chip1, w)


def forward_halves(bufs, name):
    n = len(bufs)

    def body(*refs):
        ins, outs = refs[:n], refs[n:2 * n]
        send_sems, recv_sems = refs[2 * n:]
        xi, yi, ci, chips = _place()

        def copy(ref, a, j, h):
            hr = bufs[a].shape[1] // 2
            rows = pl.ds(pl.multiple_of(h * hr, 16), hr)
            slot = 2 * chips[j][0] + chips[j][1]
            return pltpu.make_async_remote_copy(
                src_ref=ref[a].at[slot, rows, :], dst_ref=outs[a].at[slot, rows, :],
                send_sem=send_sems.at[3 * a + j], recv_sem=recv_sems.at[3 * a + j],
                device_id=(xi, yi, 1 - ci), device_id_type=MESH)

        sends = [copy(ins, a, j, ci) for a in range(n) for j in range(3)]
        for cp in sends:
            cp.start()
        for a in range(n):
            for j in range(3):
                copy(outs, a, j, 1 - ci).wait_recv()
        for cp in sends:
            cp.wait_send()

    return pl.pallas_call(
        body, name=name, in_specs=[_ANY] * n, out_specs=[_ANY] * n,
        out_shape=[_sds(b.shape, b.dtype) for b in bufs], input_output_aliases={i: i for i in range(n)},
        scratch_shapes=[pltpu.SemaphoreType.DMA((3 * n,)), pltpu.SemaphoreType.DMA((3 * n,))],
    )(*bufs)


def exchange_halves(grads, name):
    n = len(grads)

    def body(*refs):
        ins, outs = refs[:n], refs[n:2 * n]
        send_sems, recv_sems = refs[2 * n:]
        xi, yi, ci, _ = _place()
        cps = []
        for a in range(n):
            hr = grads[a].shape[1] // 2
            src = ins[a].at[:, pl.ds(pl.multiple_of((1 - ci) * hr, 16), hr), :]
            cps.append(pltpu.make_async_remote_copy(
                src_ref=src, dst_ref=outs[a], send_sem=send_sems.at[a], recv_sem=recv_sems.at[a],
                device_id=(xi, yi, 1 - ci), device_id_type=MESH))
            cps[-1].start()
        for cp in cps:
            cp.wait()

    return pl.pallas_call(
        body, name=name, in_specs=[_ANY] * n, out_specs=[_ANY] * n,
        out_shape=[_sds((N_CHIPS, g.shape[1] // 2, g.shape[2]), g.dtype) for g in grads],
        scratch_shapes=[pltpu.SemaphoreType.DMA((n,)), pltpu.SemaphoreType.DMA((n,))],
    )(*grads)


def add_half(g, got, ci, name):
    S, hr, C = got.shape
    tr = _tile(hr, tuple(t for t in _TR if t * C <= ROW_TILE_ELEMS) or (16,))
    nb = hr // tr

    def body(c_ref, g_ref, r_ref, o_ref):
        o_ref[...] = (g_ref[...].astype(F32) + r_ref[...].astype(F32)).astype(BF16)

    return pl.pallas_call(
        body, name=name,
        grid_spec=pltpu.PrefetchScalarGridSpec(
            num_scalar_prefetch=1, grid=(S, nb),
            in_specs=[pl.BlockSpec((None, tr, C), lambda s, i, c: (s, c[0] * nb + i, 0)),
                      pl.BlockSpec((None, tr, C), lambda s, i, c: (s, i, 0))],
            out_specs=pl.BlockSpec((None, tr, C), lambda s, i, c: (s, i, 0))),
        out_shape=_sds((S, hr, C), BF16), compiler_params=_cp(("parallel", "parallel")),
    )(ci, g, got)


def add_chips(part, got, chip, name):
    S, hr, C = part.shape
    tr = _tile(hr, tuple(t for t in _TR if t * C <= ROW_TILE_ELEMS) or (16,))

    def body(c_ref, p_ref, r_ref, o_ref):
        acc = p_ref[...].astype(F32)
        for j in range(3):
            acc = acc + r_ref[j].astype(F32)
        o_ref[...] = acc

    return pl.pallas_call(
        body, name=name,
        grid_spec=pltpu.PrefetchScalarGridSpec(
            num_scalar_prefetch=1, grid=(hr // tr,),
            in_specs=[pl.BlockSpec((None, tr, C), lambda i, c: (c[0], i, 0)),
                      pl.BlockSpec((3, tr, C), lambda i, c: (0, i, 0))],
            out_specs=pl.BlockSpec((tr, C), lambda i, c: (i, 0))),
        out_shape=_sds((hr, C), F32), compiler_params=_cp(("parallel",)),
    )(chip, part, got)


def join_halves(halves, name):
    n = len(halves)

    def body(*refs):
        ins, outs = refs[:n], refs[n:2 * n]
        send_sems, recv_sems, local_sems = refs[2 * n:]
        xi, yi, ci, _ = _place()
        cps, loc = [], []
        for a in range(n):
            hr = halves[a].shape[0]
            rows = outs[a].at[pl.ds(pl.multiple_of(ci * hr, 8), hr), :]
            loc.append(pltpu.make_async_copy(ins[a], rows, local_sems.at[a]))
            loc[-1].start()
            cps.append(pltpu.make_async_remote_copy(
                src_ref=ins[a], dst_ref=rows, send_sem=send_sems.at[a], recv_sem=recv_sems.at[a],
                device_id=(xi, yi, 1 - ci), device_id_type=MESH))
            cps[-1].start()
        for a in range(n):
            hr = halves[a].shape[0]
            other = outs[a].at[pl.ds(pl.multiple_of((1 - ci) * hr, 8), hr), :]
            pltpu.make_async_remote_copy(
                src_ref=ins[a], dst_ref=other, send_sem=send_sems.at[a], recv_sem=recv_sems.at[a],
                device_id=(xi, yi, 1 - ci), device_id_type=MESH).wait_recv()
        for cp in cps:
            cp.wait_send()
        for cp in loc:
            cp.wait()

    return pl.pallas_call(
        body, name=name, in_specs=[_ANY] * n, out_specs=[_ANY] * n,
        out_shape=[_sds((2 * h.shape[0], h.shape[1]), h.dtype) for h in halves],
        scratch_shapes=[pltpu.SemaphoreType.DMA((n,)), pltpu.SemaphoreType.DMA((n,)),
                        pltpu.SemaphoreType.DMA((n,))],
    )(*halves)


def reduce_scatter_start(grads, ci1, tag):
    got = exchange_halves(grads, f"rs_halves_{tag}")
    parts = [add_half(g, r, ci1, f"rs_add_half_{tag}{a}") for a, (g, r) in enumerate(zip(grads, got))]
    lands = [lax.empty((3,) + q.shape[1:], q.dtype) for q in parts]
    sems, thru, token = split_start(f"rs_ici_start_{tag}", [parts + lands], [3 * len(parts)], _scatter_copies)
    return (sems[0], thru[0]), token


def reduce_scatter_finish(state, chip1, after, tag):
    sems, bufs = state
    bufs = split_wait(f"rs_ici_wait_{tag}", bufs, sems, _scatter_copies, after)
    n = len(bufs) // 2
    return [add_chips(bufs[a], bufs[n + a], chip1, f"rs_add_chips_{tag}{a}") for a in range(n)]


def _pack(parts, width):
    flat = jnp.concatenate([p.reshape(-1).astype(F32) for p in parts])
    per = SUBLANES * width
    total = -(-flat.shape[0] // per) * per
    flat = jnp.pad(flat, (0, total - flat.shape[0]))
    return flat.reshape(total // width, width)


def _unpack(flat, shapes):
    out, off = [], 0
    for s in shapes:
        n = math.prod(s)
        out.append(flat[off:off + n].reshape(s))
        off += n
    return out


_W_NAMES = ['norm_g', 'w_ada', 'b_ada', 's5_w_in', 's5_lam_re', 's5_lam_im', 's5_log_dt', 's5_b_re', 's5_b_im',
            's5_c_re', 's5_c_im', 's5_d', 's5_w_glu', 'lru_w_in', 'lru_conv_w', 'lru_conv_b', 'lru_w_rg',
            'lru_b_rg', 'lru_w_ig', 'lru_b_ig', 'lru_lam', 'lru_w_out', 'ffn_w_gu', 'ffn_w_down', 'final_g']
_BIG = ('w_ada', 's5_w_in', 's5_w_glu', 'lru_w_in', 'lru_w_out', 'ffn_w_gu', 'ffn_w_down')
_MID = ('s5_b_re', 's5_b_im', 's5_c_re', 's5_c_im', 'lru_w_rg', 'lru_w_ig')


def _ffn_fwd(x, h, w_gu, w_down, gate, tag):
    gu = mm_nn(h, w_gu, name=f"{tag}_gu", out_dtype=BF16, bmode="cols")
    act = swiglu_fwd(gu, f"{tag}_act")
    z = mm_nn(act, w_down, name=f"{tag}_down")
    return res_gate_fwd(x, z, gate, f"{tag}_res"), (gu, act, z)


def _ffn_bwd(dx, h, saved, w_gu, w_down, gate, tag):
    gu, act, z = saved
    dz, dgate = res_gate_bwd(dx, z, gate, f"{tag}_res_bwd")
    dact = mm_nt(dz, w_down, name=f"{tag}_dact")
    dw_down = mm_tn(act, dz, name=f"{tag}_dwdown", out_dtype=BF16)
    dgu = swiglu_bwd(dact, gu, f"{tag}_act_bwd")
    dh = mm_nt(dgu, w_gu, name=f"{tag}_dh", bmode="cols")
    dw_gu = mm_tn(h, dgu, name=f"{tag}_dwgu", out_dtype=BF16, omode="cols", groups=N_CHIPS)
    return dh, dgate, dw_gu, dw_down.reshape((N_CHIPS, -1) + dw_down.shape[1:])


def _step(p):
    xi, yi, ci = lax.axis_index("x"), lax.axis_index("y"), lax.axis_index("c")
    chip = 2 * xi + yi
    me = 2 * chip + ci
    ci1 = jnp.reshape(ci, (1,)).astype(jnp.int32)
    chip1 = jnp.reshape(chip, (1,)).astype(jnp.int32)

    x0 = p['x'][0]
    tgt = p['loss_target'][0]
    L, D = x0.shape
    Dq = D // N_CHIPS
    depth = p['w_ada'].shape[0]
    E = p['lru_lam'].shape[1] * N_CHIPS
    Eq = E // N_CHIPS
    kw = p['lru_conv_w'].shape[1]
    Nq = p['w_ada'].shape[2]
    _, G, P, Cg = p['s5_b_re'].shape
    nb, bs = p['lru_w_rg'].shape[1], p['lru_w_rg'].shape[2]

    def place(key, layer, tag):
        return cast_place(p[key], layer, chip1, f"place_{tag}")

    wgroups = [[place('s5_w_in', 0, "s5_in"), place('s5_w_glu', 0, "s5_glu")],
               [place('ffn_w_gu', 0, "gu0"), place('ffn_w_down', 0, "down0")],
               [place('lru_w_in', 0, "lru_in"), place('lru_w_out', 0, "lru_out")],
               [place('ffn_w_gu', 1, "gu1"), place('ffn_w_down', 1, "down1")]]
    wsems, wbufs, wtoken = split_start("gather_ici_start", wgroups, [3 * len(g) for g in wgroups], _gather_copies)

    def weights(gi, after, tag):
        landed = split_wait(f"gather_ici_wait_{tag}", wbufs[gi], wsems[gi], _gather_copies, after)
        return forward_halves(landed, f"gather_forward_{tag}")

    pay = _pack([p['c'], p['norm_g'], p['lru_conv_w'], p['lru_conv_b'], p['lru_b_rg'], p['lru_b_ig'],
                 p['lru_lam']], 1024)
    g1 = all_gather_devices([pay], "gather_small_params")[0].reshape(N_DEV, -1)
    c_all = g1[:, :D]
    per_chip = g1[0::2]
    sizes = [(depth, 2, Dq), (kw, Eq), (Eq,), (Eq,), (Eq,), (Eq,)]
    offs = D
    pieces = []
    for s in sizes:
        nel = math.prod(s)
        pieces.append(per_chip[:, offs:offs + nel].reshape((N_CHIPS,) + s))
        offs += nel
    norm_g = jnp.moveaxis(pieces[0], 0, 2).reshape(depth, 2, D)
    conv_w = jnp.moveaxis(pieces[1], 0, 1).reshape(kw, E)
    conv_b, b_rg, b_ig, lam = [q.reshape(1, E) for q in pieces[2:]]

    cond = silu_rows(jnp.pad(c_all, ((0, 16 - N_DEV), (0, 0))), "cond_silu", after=wtoken)
    cond_rep = jnp.concatenate([cond] * depth, axis=1)
    mod_part = mm_nn(cond_rep, p['w_ada'], name="mod_proj", bmode="batch")[:N_DEV]
    g2 = all_gather_devices([mod_part], "gather_mod")[0][0::2]
    mine = lax.dynamic_index_in_dim(g2, me, axis=1, keepdims=False).reshape(N_CHIPS, depth, Nq)
    mod = jnp.moveaxis(mine, 0, 1).reshape(depth, N_CHIPS * Nq) + p['b_ada']
    mods = [[mod[i:i + 1, k * D:(k + 1) * D] for k in range(N_MOD)] for i in range(depth)]

    s5_small = (p['s5_lam_re'][0], p['s5_lam_im'][0], p['s5_log_dt'][0], p['s5_b_re'][0], p['s5_b_im'][0])
    (ab_re, ab_im, bb_re, bb_im), s5_disc_vjp = jax.vjp(_s5_discretize, *s5_small)
    bsb, csb = _s5_blockdiag(bb_re, bb_im, p['s5_c_re'][0], p['s5_c_im'][0])
    cf, crv = _s5_scan_consts(ab_re, ab_im)
    wsb_rg, wsb_ig = [w.astype(BF16) for w in _lru_blockdiag(p['lru_w_rg'][0], p['lru_w_ig'][0])]
    nsb_lru = wsb_rg.shape[0]

    sh1, sc1, gt1, sh2, sc2, gt2 = mods[0]
    s5_w_in, s5_w_glu = weights(0, mod, "s5")
    s5_w_in = s5_w_in.reshape(-1, D)
    h0 = norm_mod_fwd(x0, norm_g[0, 0:1], sc1, sh1, "l0_norm1")
    u = mm_nn(h0, s5_w_in, name="s5_in")
    ypre, yg, ss = s5_scan_fwd(u, p['s5_d'], bsb, csb, cf, "s5_scan")
    v = mm_nn(yg, s5_w_glu, name="s5_glu", bmode="cols")
    w_gu0, w_down0 = weights(1, v, "ffn0")
    w_down0 = w_down0.reshape(-1, D)
    x1 = glu_res_fwd(x0, v, gt1, "s5_res")
    h1 = norm_mod_fwd(x1, norm_g[0, 1:2], sc2, sh2, "l0_norm2")
    x2, ffn0 = _ffn_fwd(x1, h1, w_gu0, w_down0, gt2, "ffn0")

    sh1b, sc1b, gt1b, sh2b, sc2b, gt2b = mods[1]
    lru_w_in, lru_w_out = weights(2, ffn0[0], "lru")
    lru_w_out = lru_w_out.reshape(-1, D)
    h2 = norm_mod_fwd(x2, norm_g[1, 0:1], sc1b, sh1b, "l1_norm1")
    pq = mm_nn(h2, lru_w_in, name="lru_in", bmode="cols")
    xc, xcb = lru_conv_fwd(pq, conv_w, conv_b, "lru_conv")
    pre_r = mm_nn(xcb, wsb_rg, name="lru_gate_r", bmode="batch")
    pre_i = mm_nn(xcb, wsb_ig, name="lru_gate_i", bmode="batch")
    hs, yv = lru_scan_fwd(pre_r, pre_i, xc, pq, b_rg, b_ig, lam, "lru_scan")
    w_gu1, w_down1 = weights(3, hs, "ffn1")
    w_down1 = w_down1.reshape(-1, D)
    mix = mm_nn(yv, lru_w_out, name="lru_out")
    x3 = res_gate_fwd(x2, mix, gt1b, "lru_res")
    h3 = norm_mod_fwd(x3, norm_g[1, 1:2], sc2b, sh2b, "l1_norm2")
    x4, ffn1 = _ffn_fwd(x3, h3, w_gu1, w_down1, gt2b, "ffn1")

    fg = p['final_g'].reshape(1, D)
    dx4, loss_blk, dfinal_g = final_loss(x4, fg, tgt, "final_loss")
    loss = lax.psum(loss_blk[0, 0], ("x", "y", "c"))

    def rows4(g):
        return g.reshape((N_CHIPS, -1) + g.shape[1:])

    dh3, dgt2b, dw_gu1, dw_down1 = _ffn_bwd(dx4, h3, ffn1, w_gu1, w_down1, gt2b, "ffn1")
    rs_ffn1, tok = reduce_scatter_start([dw_gu1, dw_down1], ci1, "ffn1")
    dx3, dgn11, dsc2b, dsh2b = norm_mod_bwd(dh3, x3, norm_g[1, 1:2], sc2b, dx4, "l1_norm2_bwd", after=tok)

    dmix, dgt1b = res_gate_bwd(dx3, mix, gt1b, "lru_res_bwd")
    dyv = mm_nt(dmix, lru_w_out, name="lru_dyv")
    dw_out = mm_tn(yv, dmix, name="lru_dwout", out_dtype=BF16)
    dgb, dpre_r, dpre_i, dxc1, db_rg, db_ig, dlam = lru_scan_bwd(dyv, hs, pre_r, pre_i, xc, pq, b_rg, b_ig, lam,
                                                                "lru_scan_bwd")
    dxc2 = mm_nt(dpre_r, wsb_rg, name="lru_dxc_r", bmode="batch")
    dxc3 = mm_nt(dpre_i, wsb_ig, name="lru_dxc_i", bmode="batch")
    dwsb_rg = mm_tn(xcb, dpre_r, name="lru_dwgate_r", omode="batch", groups=nsb_lru)
    dwsb_ig = mm_tn(xcb, dpre_i, name="lru_dwgate_i", omode="batch", groups=nsb_lru)
    dxb, dconv_w, dconv_b = lru_conv_bwd(dxc1, dxc2, dxc3, pq, conv_w, "lru_conv_bwd")
    dpq = jnp.concatenate([dgb, dxb], axis=1)
    dh2 = mm_nt(dpq, lru_w_in, name="lru_dh", bmode="cols")
    dw_lru_in = mm_tn(h2, dpq, name="lru_dwin", out_dtype=BF16, omode="cols", groups=N_CHIPS)
    rs_lru, tok = reduce_scatter_start([dw_lru_in, rows4(dw_out)], ci1, "lru")
    dx2, dgn10, dsc1b, dsh1b = norm_mod_bwd(dh2, x2, norm_g[1, 0:1], sc1b, dx3, "l1_norm1_bwd", after=tok)

    dh1, dgt2, dw_gu0, dw_down0 = _ffn_bwd(dx2, h1, ffn0, w_gu0, w_down0, gt2, "ffn0")
    rs_ffn0, tok = reduce_scatter_start([dw_gu0, dw_down0], ci1, "ffn0")
    dx1, dgn01, dsc2, dsh2 = norm_mod_bwd(dh1, x1, norm_g[0, 1:2], sc2, dx2, "l0_norm2_bwd", after=tok)

    dv, dgt1 = glu_res_bwd(dx1, v, gt1, "s5_res_bwd")
    dyg = mm_nt(dv, s5_w_glu, name="s5_dyg", bmode="cols")
    dw_glu = mm_tn(yg, dv, name="s5_dwglu", out_dtype=BF16, omode="cols", groups=N_CHIPS)
    du, dbsb, dcsb, da, dd = s5_scan_bwd(u, dyg, ypre, p['s5_d'], bsb, csb, cf, crv, ss, "s5_scan_bwd")
    dh0 = mm_nt(du, s5_w_in, name="s5_dh")
    dw_s5_in = mm_tn(h0, du, name="s5_dwin", out_dtype=BF16)
    rs_s5, tok = reduce_scatter_start([rows4(dw_s5_in), dw_glu], ci1, "s5")
    grad_x, dgn00, dsc1, dsh1 = norm_mod_bwd(dh0, x0, norm_g[0, 0:1], sc1, dx1, "l0_norm1_bwd", after=tok)

    dmod = jnp.concatenate([jnp.concatenate([dsh1, dsc1, dgt1, dsh2, dsc2, dgt2], axis=1),
                            jnp.concatenate([dsh1b, dsc1b, dgt1b, dsh2b, dsc2b, dgt2b], axis=1)], axis=0)
    dnorm_g = jnp.stack([jnp.concatenate([dgn00, dgn01]), jnp.concatenate([dgn10, dgn11])])
    dbb_re, dbb_im, dc_re, dc_im = _s5_blockdiag_grads(dbsb, dcsb, P, Cg)
    H = S5_SB_GROUPS * P
    da_re, da_im = da[:, 0, :H].reshape(G, P), da[:, 0, H:].reshape(G, P)
    dw_rg, dw_ig = _lru_blockdiag_grad(dwsb_rg, nb, bs), _lru_blockdiag_grad(dwsb_ig, nb, bs)
    small = [dmod, dnorm_g, da_re, da_im, dd, dconv_w, dconv_b, db_rg, db_ig, dlam, dfinal_g]
    small_shapes = [s.shape for s in small]
    payload = _pack(small, 1024)
    mid = [dbb_re, dbb_im, dc_re, dc_im, dw_rg, dw_ig]
    mid_shapes = [s.shape for s in mid]
    gathered = all_gather_devices([payload] + [s.reshape(s.shape[0], -1) for s in mid], "gather_small_grads")
    gathered_small = gathered[0]
    total = sum_devices(gathered_small, "sum_small_grads").reshape(-1)
    (s_dmod, s_norm_g, s_da_re, s_da_im, s_dd, s_conv_w, s_conv_b, s_b_rg, s_b_ig, s_lam,
     s_final_g) = _unpack(total, small_shapes)
    s_dbb_re, s_dbb_im, s_dc_re, s_dc_im, s_dw_rg, s_dw_ig = [
        sum_devices(g, f"sum_mid_grads_{i}").reshape(s) for i, (g, s) in enumerate(zip(gathered[1:], mid_shapes))]
    g_lam_re, g_lam_im, g_log_dt, g_b_re, g_b_im = s5_disc_vjp((s_da_re, s_da_im, s_dbb_re, s_dbb_im))

    npay = payload.shape[0] * payload.shape[1]
    dmod_all = gathered_small.reshape(N_DEV, npay)[:, :depth * N_MOD * D].reshape(N_DEV, depth, N_CHIPS, Nq)
    dmod_mine = lax.dynamic_index_in_dim(dmod_all, chip, axis=2, keepdims=False).reshape(N_DEV, depth * Nq)
    dmod_mine = jnp.pad(dmod_mine, ((0, 16 - N_DEV), (0, 0)))
    g_w_ada = mm_tn(cond_rep, dmod_mine, name="w_ada_grad", omode="batch", groups=depth)

    def cols(full, width):
        return lax.dynamic_slice_in_dim(full, chip * width, width, axis=full.ndim - 1)

    grads = {
        'norm_g': cols(s_norm_g, Dq), 'w_ada': g_w_ada, 'b_ada': s_dmod,
        's5_lam_re': g_lam_re, 's5_lam_im': g_lam_im, 's5_log_dt': g_log_dt, 's5_b_re': g_b_re, 's5_b_im': g_b_im,
        's5_c_re': s_dc_re, 's5_c_im': s_dc_im, 's5_d': s_dd, 'lru_conv_w': cols(s_conv_w, Eq),
        'lru_conv_b': cols(s_conv_b, Eq), 'lru_w_rg': s_dw_rg, 'lru_b_rg': cols(s_b_rg, Eq),
        'lru_w_ig': s_dw_ig, 'lru_b_ig': cols(s_b_ig, Eq), 'lru_lam': cols(s_lam, Eq), 'final_g': s_final_g,
    }
    grads = {k: g.reshape(p[k].shape) for k, g in grads.items()}

    delta, new_m, new_v = {}, {}, {}

    def adamw_2d(k, rows):
        w2 = p[k].reshape(rows, -1)
        outs = adamw(w2, grads[k].reshape(w2.shape), p['m_' + k].reshape(w2.shape), p['v_' + k].reshape(w2.shape),
                     f"adamw_{k}")
        delta[k], new_m[k], new_v[k] = [o.reshape(p[k].shape) for o in outs]

    adamw_2d('w_ada', depth * D)
    for k in _MID:
        adamw_2d(k, p[k].shape[1])
    rest = [k for k in _W_NAMES if k not in _BIG + _MID]
    shapes = [p[k].shape for k in rest]
    packed = [_pack([src[pre_ + k] if pre_ else src[k] for k in rest], 1024)
              for src, pre_ in ((p, ''), (grads, ''), (p, 'm_'), (p, 'v_'))]
    outs = adamw(*packed, "adamw_small")
    for dst, o in zip((delta, new_m, new_v), outs):
        for k, val in zip(rest, _unpack(o.reshape(-1), shapes)):
            dst[k] = val

    done = delta['w_ada']
    halves = []
    for state, tag in ((rs_ffn1, "ffn1"), (rs_lru, "lru"), (rs_ffn0, "ffn0"), (rs_s5, "s5")):
        halves += reduce_scatter_finish(state, chip1, done, tag)
    g_gu1, g_down1, g_lru_in, g_lru_out, g_gu0, g_down0, g_s5_in, g_s5_glu = join_halves(halves, "rs_join_halves")
    grads.update({'s5_w_in': g_s5_in[None], 's5_w_glu': g_s5_glu[None], 'lru_w_in': g_lru_in[None],
                  'lru_w_out': g_lru_out[None], 'ffn_w_gu': jnp.stack([g_gu0, g_gu1]),
                  'ffn_w_down': jnp.stack([g_down0, g_down1])})
    for k in _BIG[1:]:
        adamw_2d(k, math.prod(p[k].shape[:-1]))

    return (loss, grad_x[None], *[grads[k] for k in _W_NAMES], *[delta[k] for k in _W_NAMES],
            *[new_m[k] for k in _W_NAMES], *[new_v[k] for k in _W_NAMES])


_IN_NAMES = (['x', 'c'] + _W_NAMES + ['loss_target'] + ['m_' + k for k in _W_NAMES] + ['v_' + k for k in _W_NAMES])


def kernel(x, c, norm_g, w_ada, b_ada, s5_w_in, s5_lam_re, s5_lam_im, s5_log_dt, s5_b_re, s5_b_im, s5_c_re, s5_c_im, s5_d, s5_w_glu, lru_w_in, lru_conv_w, lru_conv_b, lru_w_rg, lru_b_rg, lru_w_ig, lru_b_ig, lru_lam, lru_w_out, ffn_w_gu, ffn_w_down, final_g, loss_target, m_norm_g, m_w_ada, m_b_ada, m_s5_w_in, m_s5_lam_re, m_s5_lam_im, m_s5_log_dt, m_s5_b_re, m_s5_b_im, m_s5_c_re, m_s5_c_im, m_s5_d, m_s5_w_glu, m_lru_w_in, m_lru_conv_w, m_lru_conv_b, m_lru_w_rg, m_lru_b_rg, m_lru_w_ig, m_lru_b_ig, m_lru_lam, m_lru_w_out, m_ffn_w_gu, m_ffn_w_down, m_final_g, v_norm_g, v_w_ada, v_b_ada, v_s5_w_in, v_s5_lam_re, v_s5_lam_im, v_s5_log_dt, v_s5_b_re, v_s5_b_im, v_s5_c_re, v_s5_c_im, v_s5_d, v_s5_w_glu, v_lru_w_in, v_lru_conv_w, v_lru_conv_b, v_lru_w_rg, v_lru_b_rg, v_lru_w_ig, v_lru_b_ig, v_lru_lam, v_lru_w_out, v_ffn_w_gu, v_ffn_w_down, v_final_g):
    args = (x, c, norm_g, w_ada, b_ada, s5_w_in, s5_lam_re, s5_lam_im, s5_log_dt, s5_b_re, s5_b_im, s5_c_re, s5_c_im, s5_d, s5_w_glu, lru_w_in, lru_conv_w, lru_conv_b, lru_w_rg, lru_b_rg, lru_w_ig, lru_b_ig, lru_lam, lru_w_out, ffn_w_gu, ffn_w_down, final_g, loss_target, m_norm_g, m_w_ada, m_b_ada, m_s5_w_in, m_s5_lam_re, m_s5_lam_im, m_s5_log_dt, m_s5_b_re, m_s5_b_im, m_s5_c_re, m_s5_c_im, m_s5_d, m_s5_w_glu, m_lru_w_in, m_lru_conv_w, m_lru_conv_b, m_lru_w_rg, m_lru_b_rg, m_lru_w_ig, m_lru_b_ig, m_lru_lam, m_lru_w_out, m_ffn_w_gu, m_ffn_w_down, m_final_g, v_norm_g, v_w_ada, v_b_ada, v_s5_w_in, v_s5_lam_re, v_s5_lam_im, v_s5_log_dt, v_s5_b_re, v_s5_b_im, v_s5_c_re, v_s5_c_im, v_s5_d, v_s5_w_glu, v_lru_w_in, v_lru_conv_w, v_lru_conv_b, v_lru_w_rg, v_lru_b_rg, v_lru_w_ig, v_lru_b_ig, v_lru_lam, v_lru_w_out, v_ffn_w_gu, v_ffn_w_down, v_final_g)
    return _step(dict(zip(_IN_NAMES, args)))
```

```python
import functools
import math

import jax
import jax.numpy as jnp
from jax import lax
from jax.experimental import pallas as pl
from jax.experimental.pallas import tpu as pltpu

F32 = jnp.float32
BF16 = jnp.bfloat16
MESH = pl.DeviceIdType.MESH

EPS = 1e-6
LRU_C = 8.0
N_MOD = 6
ADAM_LR = 0.001
ADAM_B1 = 0.9
ADAM_B2 = 0.999
ADAM_EPS = 1e-08
ADAM_WD = 0.01
ADAM_STEP = 10

N_CHIPS = 4
N_DEV = 8
SUBLANES = 8
LANES = 128
S5_SB_GROUPS = 8
V7X_VMEM_LIMIT = 48 * 1024 * 1024
ROW_TILE_ELEMS = 512 * 1024
SCAN_UNROLL = 2
LRU_TILE = 256

_TM = (1024, 512, 256, 128, 64, 32, 16, 8)
_TN = (1024, 1408, 512, 384, 256, 128)
_TK = (512, 1408, 256, 128)
_TR = (256, 128, 64, 32, 16, 8)

_GELU_K0 = math.sqrt(2.0 / math.pi)
_GELU_K1 = 0.044715


def _tile(n, cands):
    for t in cands:
        if n % t == 0:
            return t
    return n


def _cp(sem=None):
    return pltpu.CompilerParams(dimension_semantics=sem, vmem_limit_bytes=V7X_VMEM_LIMIT)


def _sds(shape, dtype):
    return jax.ShapeDtypeStruct(shape, dtype)


def _sig(x):
    return 1.0 / (1.0 + jnp.exp(-x))


def _gelu(x):
    t = jnp.tanh(_GELU_K0 * (x + _GELU_K1 * x * x * x))
    return 0.5 * x * (1.0 + t)


def _gelu_grad(x):
    x2 = x * x
    t = jnp.tanh(_GELU_K0 * (x + _GELU_K1 * x * x2))
    return 0.5 * (1.0 + t) + 0.5 * x * (1.0 - t * t) * _GELU_K0 * (1.0 + 3.0 * _GELU_K1 * x2)


def _softplus(z):
    return jnp.maximum(z, 0.0) + jnp.log(1.0 + jnp.exp(-jnp.abs(z)))


def _neg_expm1(x):
    series = -x * (1.0 + x * (0.5 + x * (1.0 / 6.0 + x * (1.0 / 24.0))))
    return jnp.where(x > -0.05, series, 1.0 - jnp.exp(x))


def _row(x, r):
    return x[r:r + 1, :]


def _colsum(x):
    return jnp.sum(x, axis=0, keepdims=True)


_NN = (((1,), (0,)), ((), ()))
_NT = (((1,), (1,)), ((), ()))
_TN_DIMS = (((0,), (0,)), ((), ()))


def _mm_call(name, a, b, a_spec, b_spec, o_spec, grid, out_shape, acc_shape, dims):
    nk = grid[-1]
    kaxis = len(grid) - 1

    def body(a_ref, b_ref, o_ref, acc_ref):
        k = pl.program_id(kaxis)

        @pl.when(k == 0)
        def _():
            acc_ref[...] = jnp.zeros_like(acc_ref)

        acc_ref[...] += lax.dot_general(a_ref[...].astype(BF16), b_ref[...].astype(BF16), dims,
                                        preferred_element_type=F32)

        @pl.when(k == nk - 1)
        def _():
            o_ref[...] = acc_ref[...].astype(o_ref.dtype)

    return pl.pallas_call(
        body, name=name, grid=grid, in_specs=[a_spec, b_spec], out_specs=o_spec, out_shape=out_shape,
        scratch_shapes=[pltpu.VMEM(acc_shape, F32)],
        compiler_params=_cp(("parallel", "parallel", "parallel", "arbitrary")),
    )(a, b)


def mm_nn(a, b, *, name, out_dtype=F32, bmode="plain"):
    M = a.shape[0]
    if bmode == "plain":
        G, S = 1, 1
        K, Nc = b.shape
    elif bmode == "cols":
        G = 1
        S, K, Nc = b.shape
    else:
        S = 1
        G, K, Nc = b.shape
    tm, tn, tk = _tile(M, _TM), _tile(Nc, _TN), _tile(K, _TK)
    nkb, nnb = K // tk, Nc // tn
    ncol = S * nnb
    grid = (G, M // tm, ncol, nkb)
    a_spec = pl.BlockSpec((tm, tk), lambda g, i, j, k: (i, g * nkb + k))
    if bmode == "plain":
        b_spec = pl.BlockSpec((tk, tn), lambda g, i, j, k: (k, j))
    elif bmode == "cols":
        b_spec = pl.BlockSpec((None, tk, tn), lambda g, i, j, k: (j // nnb, k, j % nnb))
    else:
        b_spec = pl.BlockSpec((None, tk, tn), lambda g, i, j, k: (g, k, j))
    o_spec = pl.BlockSpec((tm, tn), lambda g, i, j, k: (i, g * ncol + j))
    return _mm_call(name, a, b, a_spec, b_spec, o_spec, grid, _sds((M, G * S * Nc), out_dtype), (tm, tn), _NN)


def mm_nt(a, b, *, name, out_dtype=F32, bmode="plain"):
    M = a.shape[0]
    if bmode == "plain":
        G, S = 1, 1
        Ko, Nc = b.shape
    elif bmode == "cols":
        G = 1
        S, Ko, Nc = b.shape
    else:
        S = 1
        G, Ko, Nc = b.shape
    tm, to, tc = _tile(M, _TM), _tile(Ko, _TN), _tile(Nc, _TK)
    npc = Nc // tc
    nc = S * npc
    nob = Ko // to
    grid = (G, M // tm, nob, nc)
    a_spec = pl.BlockSpec((tm, tc), lambda g, i, j, n: (i, g * nc + n))
    if bmode == "plain":
        b_spec = pl.BlockSpec((to, tc), lambda g, i, j, n: (j, n))
    elif bmode == "cols":
        b_spec = pl.BlockSpec((None, to, tc), lambda g, i, j, n: (n // npc, j, n % npc))
    else:
        b_spec = pl.BlockSpec((None, to, tc), lambda g, i, j, n: (g, j, n))
    o_spec = pl.BlockSpec((tm, to), lambda g, i, j, n: (i, g * nob + j))
    return _mm_call(name, a, b, a_spec, b_spec, o_spec, grid, _sds((M, G * Ko), out_dtype), (tm, to), _NT)


def mm_tn(a, b, *, name, out_dtype=F32, omode="plain", groups=1):
    L = a.shape[0]
    G = groups if omode == "batch" else 1
    S = groups if omode == "cols" else 1
    Mo, N = a.shape[1] // G, b.shape[1] // G
    Nc = N // S
    tm, tn, tl = _tile(Mo, _TM), _tile(Nc, _TN), _tile(L, _TK)
    nmb, nnb = Mo // tm, N // tn
    npj = Nc // tn
    grid = (G, nmb, nnb, L // tl)
    a_spec = pl.BlockSpec((tl, tm), lambda g, i, j, l: (l, g * nmb + i))
    b_spec = pl.BlockSpec((tl, tn), lambda g, i, j, l: (l, g * nnb + j))
    if omode == "plain":
        o_spec = pl.BlockSpec((tm, tn), lambda g, i, j, l: (i, j))
        oshape = (Mo, N)
    elif omode == "cols":
        o_spec = pl.BlockSpec((None, tm, tn), lambda g, i, j, l: (j // npj, i, j % npj))
        oshape = (S, Mo, Nc)
    else:
        o_spec = pl.BlockSpec((None, tm, tn), lambda g, i, j, l: (g, i, j))
        oshape = (G, Mo, N)
    return _mm_call(name, a, b, a_spec, b_spec, o_spec, grid, _sds(oshape, out_dtype), (tm, tn), _TN_DIMS)


def _row_call(name, body, row_ins, vec_ins, row_outs, acc_outs=(), after=None):
    if after is not None:
        n_in = len(row_ins) + len(vec_ins)
        inner = body

        def body(*refs):
            inner(*refs[:n_in], *refs[n_in + 1:])

        return _row_call_impl(name, body, row_ins, vec_ins, row_outs, acc_outs, [after])
    return _row_call_impl(name, body, row_ins, vec_ins, row_outs, acc_outs, [])


def _row_call_impl(name, body, row_ins, vec_ins, row_outs, acc_outs, extra):
    L = row_ins[0].shape[0]
    wmax = max([a.shape[1] for a in row_ins] + [w for w, _ in row_outs])
    tr = _tile(L, tuple(t for t in _TR if t * wmax <= ROW_TILE_ELEMS) or (SUBLANES,))
    in_specs = [pl.BlockSpec((tr, a.shape[1]), lambda i: (i, 0)) for a in row_ins]
    in_specs += [pl.BlockSpec(v.shape, lambda i, nd=v.ndim: (0,) * nd) for v in vec_ins]
    in_specs += [pl.BlockSpec(memory_space=pl.ANY) for _ in extra]
    out_shape = [_sds((L, w), dt) for w, dt in row_outs] + [_sds(s, dt) for s, dt in acc_outs]
    out_specs = [pl.BlockSpec((tr, w), lambda i: (i, 0)) for w, _ in row_outs]
    out_specs += [pl.BlockSpec(s, lambda i, nd=len(s): (0,) * nd) for s, _ in acc_outs]
    sem = ("arbitrary",) if acc_outs else ("parallel",)
    return pl.pallas_call(body, name=name, grid=(L // tr,), in_specs=in_specs, out_specs=out_specs,
                          out_shape=out_shape, compiler_params=_cp(sem))(*row_ins, *vec_ins, *extra)


def silu_rows(x, name, after=None):
    def body(x_ref, o_ref):
        v = x_ref[...]
        o_ref[...] = (v * _sig(v)).astype(o_ref.dtype)
    return _row_call(name, body, [x], [], [(x.shape[1], BF16)], after=after)[0]


def norm_mod_fwd(x, gain, sc, sh, name):
    def body(x_ref, g_ref, sc_ref, sh_ref, h_ref):
        v = x_ref[...]
        r = lax.rsqrt(jnp.mean(v * v, axis=-1, keepdims=True) + EPS)
        h_ref[...] = (v * r * g_ref[...] * (1.0 + sc_ref[...]) + sh_ref[...]).astype(BF16)
    return _row_call(name, body, [x], [gain, sc, sh], [(x.shape[1], BF16)])[0]


def norm_mod_bwd(dh, x, gain, sc, dres, name, after=None):
    D = x.shape[1]

    def body(dh_ref, x_ref, dres_ref, g_ref, sc_ref, dx_ref, dg_ref, dsc_ref, dsh_ref):
        @pl.when(pl.program_id(0) == 0)
        def _():
            dg_ref[...] = jnp.zeros_like(dg_ref)
            dsc_ref[...] = jnp.zeros_like(dsc_ref)
            dsh_ref[...] = jnp.zeros_like(dsh_ref)

        v = x_ref[...]
        dh_v = dh_ref[...]
        g = g_ref[...]
        r = lax.rsqrt(jnp.mean(v * v, axis=-1, keepdims=True) + EPS)
        xhat = v * r
        dn = dh_v * (1.0 + sc_ref[...])
        dsc_ref[...] += _colsum(dh_v * xhat * g)
        dsh_ref[...] += _colsum(dh_v)
        dg_ref[...] += _colsum(dn * xhat)
        t = dn * g
        dx_ref[...] = dres_ref[...] + r * (t - xhat * jnp.mean(t * xhat, axis=-1, keepdims=True))

    acc = [((1, D), F32)] * 3
    return _row_call(name, body, [dh, x, dres], [gain, sc], [(D, F32)], acc, after=after)


def final_loss(x, gain, tgt, name):
    D = x.shape[1]

    def body(x_ref, t_ref, g_ref, dx_ref, loss_ref, dg_ref, acc_ref):
        i = pl.program_id(0)

        @pl.when(i == 0)
        def _():
            dg_ref[...] = jnp.zeros_like(dg_ref)
            acc_ref[...] = jnp.zeros_like(acc_ref)

        v = x_ref[...]
        g = g_ref[...]
        r = lax.rsqrt(jnp.mean(v * v, axis=-1, keepdims=True) + EPS)
        xhat = v * r
        err = xhat * g - t_ref[...]
        acc_ref[...] += _colsum(err * err)
        dout = err * (1.0 / D)
        dg_ref[...] += _colsum(dout * xhat)
        t = dout * g
        dx_ref[...] = r * (t - xhat * jnp.mean(t * xhat, axis=-1, keepdims=True))

        @pl.when(i == pl.num_programs(0) - 1)
        def _():
            loss_ref[...] = jnp.zeros_like(loss_ref) + jnp.sum(acc_ref[...]) * (0.5 / D)

    return _row_call(name, body, [x, tgt], [gain], [(D, F32)],
                     [((SUBLANES, LANES), F32), ((1, D), F32), ((1, D), F32)])[:3]


def res_gate_fwd(x, z, g, name):
    def body(x_ref, z_ref, g_ref, o_ref):
        o_ref[...] = x_ref[...] + g_ref[...] * z_ref[...]
    return _row_call(name, body, [x, z], [g], [(x.shape[1], F32)])[0]


def res_gate_bwd(dx, z, g, name):
    D = dx.shape[1]

    def body(dx_ref, z_ref, g_ref, dz_ref, dg_ref):
        @pl.when(pl.program_id(0) == 0)
        def _():
            dg_ref[...] = jnp.zeros_like(dg_ref)
        d = dx_ref[...]
        dz_ref[...] = (g_ref[...] * d).astype(BF16)
        dg_ref[...] += _colsum(d * z_ref[...])
    return _row_call(name, body, [dx, z], [g], [(D, BF16)], [((1, D), F32)])


def glu_res_fwd(x, v, g, name):
    D = x.shape[1]

    def body(x_ref, v_ref, g_ref, o_ref):
        vv = v_ref[...]
        o_ref[...] = x_ref[...] + g_ref[...] * (vv[:, :D] * _sig(vv[:, D:]))
    return _row_call(name, body, [x, v], [g], [(D, F32)])[0]


def glu_res_bwd(dx, v, g, name):
    D = dx.shape[1]

    def body(dx_ref, v_ref, g_ref, dv_ref, dg_ref):
        @pl.when(pl.program_id(0) == 0)
        def _():
            dg_ref[...] = jnp.zeros_like(dg_ref)
        d = dx_ref[...]
        vv = v_ref[...]
        val = vv[:, :D]
        s = _sig(vv[:, D:])
        dg_ref[...] += _colsum(d * val * s)
        dm = g_ref[...] * d
        dv_ref[:, :D] = (dm * s).astype(BF16)
        dv_ref[:, D:] = (dm * val * s * (1.0 - s)).astype(BF16)
    return _row_call(name, body, [dx, v], [g], [(2 * D, BF16)], [((1, D), F32)])


def swiglu_fwd(gu, name):
    F = gu.shape[1] // 2

    def body(gu_ref, o_ref):
        v = gu_ref[...].astype(F32)
        g = v[:, :F]
        o_ref[...] = (g * _sig(g) * v[:, F:]).astype(BF16)
    return _row_call(name, body, [gu], [], [(F, BF16)])[0]


def swiglu_bwd(dact, gu, name):
    F = gu.shape[1] // 2

    def body(da_ref, gu_ref, o_ref):
        v = gu_ref[...].astype(F32)
        g, u = v[:, :F], v[:, F:]
        da = da_ref[...]
        s = _sig(g)
        o_ref[:, :F] = (da * u * s * (1.0 + g * (1.0 - s))).astype(BF16)
        o_ref[:, F:] = (da * g * s).astype(BF16)
    return _row_call(name, body, [dact, gu], [], [(2 * F, BF16)])[0]


def adamw(w, g, m, v, name):
    C = w.shape[1]
    c1 = 1.0 - ADAM_B1 ** ADAM_STEP
    c2 = 1.0 - ADAM_B2 ** ADAM_STEP

    def body(w_ref, g_ref, m_ref, v_ref, d_ref, m2_ref, v2_ref):
        gv = g_ref[...]
        m2 = ADAM_B1 * m_ref[...] + (1.0 - ADAM_B1) * gv
        v2 = ADAM_B2 * v_ref[...] + (1.0 - ADAM_B2) * (gv * gv)
        m2_ref[...] = m2
        v2_ref[...] = v2
        d_ref[...] = -ADAM_LR * ((m2 / c1) / (jnp.sqrt(v2 / c2) + ADAM_EPS) + ADAM_WD * w_ref[...])
    return _row_call(name, body, [w, g, m, v], [], [(C, F32)] * 3)


def sum_devices(parts, name):
    n, R, C = parts.shape
    tr = _tile(R, tuple(t for t in _TR if t * C * n <= 4 * ROW_TILE_ELEMS) or (SUBLANES,))

    def body(p_ref, o_ref):
        acc = p_ref[0]
        for d in range(1, n):
            acc = acc + p_ref[d]
        o_ref[...] = acc
    return pl.pallas_call(body, name=name, grid=(R // tr,),
                          in_specs=[pl.BlockSpec((n, tr, C), lambda i: (0, i, 0))],
                          out_specs=pl.BlockSpec((tr, C), lambda i: (i, 0)), out_shape=_sds((R, C), F32),
                          compiler_params=_cp(("parallel",)))(parts)


def _s5_discretize(lam_re, lam_im, log_dt, b_re, b_im):
    dt = jnp.exp(log_dt)[:, None]
    mag = jnp.exp(lam_re * dt)
    ab_re = mag * jnp.cos(lam_im * dt)
    ab_im = mag * jnp.sin(lam_im * dt)
    nr, ni = ab_re - 1.0, ab_im
    den = lam_re * lam_re + lam_im * lam_im
    f_re = (nr * lam_re + ni * lam_im) / den
    f_im = (ni * lam_re - nr * lam_im) / den
    bb_re = f_re[..., None] * b_re - f_im[..., None] * b_im
    bb_im = f_re[..., None] * b_im + f_im[..., None] * b_re
    return ab_re, ab_im, bb_re, bb_im


def _s5_blockdiag(bb_re, bb_im, c_re, c_im):
    G, P, Cg = bb_re.shape
    nsb = G // S5_SB_GROUPS

    def bmat(bb):
        t = jnp.swapaxes(bb.reshape(nsb, S5_SB_GROUPS, P, Cg), 2, 3)
        return _spread_diag(t).reshape(nsb, S5_SB_GROUPS * Cg, S5_SB_GROUPS * P)

    def cmat(cc):
        t = jnp.swapaxes(cc.reshape(nsb, S5_SB_GROUPS, Cg, P), 2, 3)
        return _spread_diag(t).reshape(nsb, S5_SB_GROUPS * P, S5_SB_GROUPS * Cg)

    bsb = jnp.concatenate([bmat(bb_re), bmat(bb_im)], axis=-1)
    csb = jnp.concatenate([cmat(c_re), -cmat(c_im)], axis=1)
    return bsb, csb


def _spread_diag(t):
    ng = t.shape[1]
    eye = jnp.eye(ng, dtype=t.dtype)
    return t[:, :, :, None, :] * eye[None, :, None, :, None]


def _take_diag(t):
    return jnp.stack([t[:, g, :, g, :] for g in range(t.shape[1])], axis=1)


def _s5_blockdiag_grads(dbsb, dcsb, P, Cg):
    nsb = dbsb.shape[0]
    db6 = dbsb.reshape(nsb, S5_SB_GROUPS, Cg, 2, S5_SB_GROUPS, P)
    dbb_re = jnp.swapaxes(_take_diag(db6[:, :, :, 0]), 2, 3).reshape(-1, P, Cg)
    dbb_im = jnp.swapaxes(_take_diag(db6[:, :, :, 1]), 2, 3).reshape(-1, P, Cg)
    dc6 = dcsb.reshape(nsb, 2, S5_SB_GROUPS, P, S5_SB_GROUPS, Cg)
    dc_re = jnp.swapaxes(_take_diag(dc6[:, 0]), 2, 3).reshape(-1, Cg, P)
    dc_im = -jnp.swapaxes(_take_diag(dc6[:, 1]), 2, 3).reshape(-1, Cg, P)
    return dbb_re, dbb_im, dc_re, dc_im


def _s5_scan_consts(ab_re, ab_im):
    G, P = ab_re.shape
    nsb = G // S5_SB_GROUPS
    H = S5_SB_GROUPS * P
    ar, ai = ab_re.reshape(nsb, 1, H), ab_im.reshape(nsb, 1, H)
    pows = [(ar, ai)]
    for _ in range(SUBLANES - 1):
        pr, pi_ = pows[-1]
        pows.append((pr * ar - pi_ * ai, pr * ai + pi_ * ar))
    rows = jnp.arange(SUBLANES).reshape(1, SUBLANES, 1)

    def masked(k, keep):
        pr, pi_ = pows[k - 1]
        return jnp.where(keep, pr, 0.0), jnp.where(keep, pi_, 0.0)

    def per_row(sel):
        pr = jnp.concatenate([pows[sel(r) - 1][0] for r in range(SUBLANES)], axis=1)
        pi_ = jnp.concatenate([pows[sel(r) - 1][1] for r in range(SUBLANES)], axis=1)
        return pr, pi_

    fwd = [masked(1, rows >= 1), masked(2, rows >= 2), masked(4, rows >= 4), per_row(lambda r: r + 1)]
    rev = [masked(1, rows < 7), masked(2, rows < 6), masked(4, rows < 4), per_row(lambda r: SUBLANES - r)]

    def pack(lst, conj):
        sgn = -1.0 if conj else 1.0
        return jnp.stack([jnp.concatenate([jnp.broadcast_to(pr, (nsb, SUBLANES, H)),
                                           sgn * jnp.broadcast_to(pi_, (nsb, SUBLANES, H))], axis=-1)
                          for pr, pi_ in lst], axis=1)

    return pack(fwd, False), pack(rev, True)


def _cmadd(xr, xi, ar, ai, yr, yi):
    return xr + ar * yr - ai * yi, xi + ar * yi + ai * yr


def _s5_scan_fwd_loop(src_ref, dst_ref, sp_ref, cf_ref, cr, ci, nblk, H):
    rows = lax.broadcasted_iota(jnp.int32, (SUBLANES, H), 0)

    def body(k, carry):
        cr, ci = carry
        r0 = pl.multiple_of(k * SUBLANES, SUBLANES)
        xr = src_ref[pl.ds(r0, SUBLANES), pl.ds(0, H)]
        xi = src_ref[pl.ds(r0, SUBLANES), pl.ds(H, H)]
        for idx, d in enumerate((1, 2, 4)):
            xr, xi = _cmadd(xr, xi, cf_ref[idx, :, pl.ds(0, H)], cf_ref[idx, :, pl.ds(H, H)],
                            pltpu.roll(xr, d, 0), pltpu.roll(xi, d, 0))
        xr, xi = _cmadd(xr, xi, cf_ref[3, :, pl.ds(0, H)], cf_ref[3, :, pl.ds(H, H)], cr, ci)
        dst_ref[pl.ds(r0, SUBLANES), pl.ds(0, H)] = xr
        dst_ref[pl.ds(r0, SUBLANES), pl.ds(H, H)] = xi
        if sp_ref is not None:
            sp_ref[pl.ds(r0, SUBLANES), pl.ds(0, H)] = jnp.where(rows == 0, cr, pltpu.roll(xr, 1, 0))
            sp_ref[pl.ds(r0, SUBLANES), pl.ds(H, H)] = jnp.where(rows == 0, ci, pltpu.roll(xi, 1, 0))
        return _row(xr, SUBLANES - 1), _row(xi, SUBLANES - 1)

    return lax.fori_loop(0, nblk, body, (cr, ci))


def s5_scan_fwd(u, d_skip, bsb, csb, cf, name):
    L, W = u.shape
    nsb, GW, H2 = bsb.shape
    H = H2 // 2
    Tc = _tile(L, (512, 256, 128, 64, 32, 16, 8))
    nch = L // Tc

    def body(u_ref, d_ref, b_ref, c_ref, cf_ref, ypre_ref, yg_ref, ss_ref, bu_scr, car_scr):
        @pl.when(pl.program_id(1) == 0)
        def _():
            car_scr[...] = jnp.zeros_like(car_scr)

        ss_ref[...] = car_scr[...]
        ub = u_ref[...]
        bu_scr[...] = jnp.dot(ub.astype(BF16), b_ref[...], preferred_element_type=F32)
        cr, ci = _s5_scan_fwd_loop(bu_scr, bu_scr, None, cf_ref, car_scr[:, pl.ds(0, H)], car_scr[:, pl.ds(H, H)],
                                   Tc // SUBLANES, H)
        car_scr[:, pl.ds(0, H)] = cr
        car_scr[:, pl.ds(H, H)] = ci
        ypre = jnp.dot(bu_scr[...].astype(BF16), c_ref[...], preferred_element_type=F32) + d_ref[...] * ub
        ypre_ref[...] = ypre
        yg_ref[...] = _gelu(ypre).astype(BF16)

    return pl.pallas_call(
        body, name=name, grid=(nsb, nch),
        in_specs=[pl.BlockSpec((Tc, GW), lambda j, i: (i, j)),
                  pl.BlockSpec((1, GW), lambda j, i: (0, j)),
                  pl.BlockSpec((None, GW, H2), lambda j, i: (j, 0, 0)),
                  pl.BlockSpec((None, H2, GW), lambda j, i: (j, 0, 0)),
                  pl.BlockSpec((None, 4, SUBLANES, H2), lambda j, i: (j, 0, 0, 0))],
        out_specs=[pl.BlockSpec((Tc, GW), lambda j, i: (i, j)),
                   pl.BlockSpec((Tc, GW), lambda j, i: (i, j)),
                   pl.BlockSpec((None, None, 1, H2), lambda j, i: (i, j, 0, 0))],
        out_shape=[_sds((L, W), F32), _sds((L, W), BF16), _sds((nch, nsb, 1, H2), F32)],
        scratch_shapes=[pltpu.VMEM((Tc, H2), F32), pltpu.VMEM((1, H2), F32)],
        compiler_params=_cp(("arbitrary", "arbitrary")),
    )(u, d_skip, bsb.astype(BF16), csb.astype(BF16), cf)


def s5_scan_bwd(u, dyg, ypre, d_skip, bsb, csb, cf, crv, ss, name):
    L, W = u.shape
    nsb, GW, H2 = bsb.shape
    H = H2 // 2
    Tc = _tile(L, (512, 256, 128, 64, 32, 16, 8))
    nch = L // Tc
    nblk = Tc // SUBLANES
    bsb_t = jnp.swapaxes(bsb, 1, 2).astype(BF16)
    csb_t = jnp.swapaxes(csb, 1, 2).astype(BF16)

    def body(u_ref, dyg_ref, yp_ref, d_ref, b_ref, bt_ref, ct_ref, cf_ref, crv_ref, ss_ref,
             du_ref, db_ref, dc_ref, da_ref, dd_ref, s_scr, sp_scr, g_scr, gcar_scr):
        @pl.when(pl.program_id(1) == 0)
        def _():
            gcar_scr[...] = jnp.zeros_like(gcar_scr)
            db_ref[...] = jnp.zeros_like(db_ref)
            dc_ref[...] = jnp.zeros_like(dc_ref)
            da_ref[...] = jnp.zeros_like(da_ref)
            dd_ref[...] = jnp.zeros_like(dd_ref)

        ub = u_ref[...]
        ubf = ub.astype(BF16)
        dyp = dyg_ref[...] * _gelu_grad(yp_ref[...])
        dypb = dyp.astype(BF16)
        dd_ref[...] += _colsum(dyp * ub)
        s_scr[...] = jnp.dot(ubf, b_ref[...], preferred_element_type=F32)
        _s5_scan_fwd_loop(s_scr, s_scr, sp_scr, cf_ref, ss_ref[:, pl.ds(0, H)], ss_ref[:, pl.ds(H, H)], nblk, H)
        g_scr[...] = jnp.dot(dypb, ct_ref[...], preferred_element_type=F32)

        def rev(kk, carry):
            gr, gi, acc_r, acc_i = carry
            r0 = pl.multiple_of((nblk - 1 - kk) * SUBLANES, SUBLANES)
            xr = g_scr[pl.ds(r0, SUBLANES), pl.ds(0, H)]
            xi = g_scr[pl.ds(r0, SUBLANES), pl.ds(H, H)]
            for idx, d in enumerate((1, 2, 4)):
                xr, xi = _cmadd(xr, xi, crv_ref[idx, :, pl.ds(0, H)], crv_ref[idx, :, pl.ds(H, H)],
                                pltpu.roll(xr, SUBLANES - d, 0), pltpu.roll(xi, SUBLANES - d, 0))
            xr, xi = _cmadd(xr, xi, crv_ref[3, :, pl.ds(0, H)], crv_ref[3, :, pl.ds(H, H)], gr, gi)
            g_scr[pl.ds(r0, SUBLANES), pl.ds(0, H)] = xr
            g_scr[pl.ds(r0, SUBLANES), pl.ds(H, H)] = xi
            spr = sp_scr[pl.ds(r0, SUBLANES), pl.ds(0, H)]
            spi = sp_scr[pl.ds(r0, SUBLANES), pl.ds(H, H)]
            return (_row(xr, 0), _row(xi, 0), acc_r + xr * spr + xi * spi, acc_i + xi * spr - xr * spi)

        zero = jnp.zeros((SUBLANES, H), F32)
        gr, gi, acc_r, acc_i = lax.fori_loop(
            0, nblk, rev, (gcar_scr[:, pl.ds(0, H)], gcar_scr[:, pl.ds(H, H)], zero, zero))
        gcar_scr[:, pl.ds(0, H)] = gr
        gcar_scr[:, pl.ds(H, H)] = gi
        da_ref[:, pl.ds(0, H)] += _colsum(acc_r)
        da_ref[:, pl.ds(H, H)] += _colsum(acc_i)
        gb = g_scr[...].astype(BF16)
        db_ref[...] += lax.dot_general(ubf, gb, _TN_DIMS, preferred_element_type=F32)
        dc_ref[...] += lax.dot_general(s_scr[...].astype(BF16), dypb, _TN_DIMS, preferred_element_type=F32)
        du_ref[...] = (jnp.dot(gb, bt_ref[...], preferred_element_type=F32) + d_ref[...] * dyp).astype(BF16)

    rmap = lambda j, i: (nch - 1 - i, j)
    return pl.pallas_call(
        body, name=name, grid=(nsb, nch),
        in_specs=[pl.BlockSpec((Tc, GW), rmap), pl.BlockSpec((Tc, GW), rmap), pl.BlockSpec((Tc, GW), rmap),
                  pl.BlockSpec((1, GW), lambda j, i: (0, j)),
                  pl.BlockSpec((None, GW, H2), lambda j, i: (j, 0, 0)),
                  pl.BlockSpec((None, H2, GW), lambda j, i: (j, 0, 0)),
                  pl.BlockSpec((None, GW, H2), lambda j, i: (j, 0, 0)),
                  pl.BlockSpec((None, 4, SUBLANES, H2), lambda j, i: (j, 0, 0, 0)),
                  pl.BlockSpec((None, 4, SUBLANES, H2), lambda j, i: (j, 0, 0, 0)),
                  pl.BlockSpec((None, None, 1, H2), lambda j, i: (nch - 1 - i, j, 0, 0))],
        out_specs=[pl.BlockSpec((Tc, GW), rmap),
                   pl.BlockSpec((None, GW, H2), lambda j, i: (j, 0, 0)),
                   pl.BlockSpec((None, H2, GW), lambda j, i: (j, 0, 0)),
                   pl.BlockSpec((None, 1, H2), lambda j, i: (j, 0, 0)),
                   pl.BlockSpec((1, GW), lambda j, i: (0, j))],
        out_shape=[_sds((L, W), BF16), _sds((nsb, GW, H2), F32), _sds((nsb, H2, GW), F32),
                   _sds((nsb, 1, H2), F32), _sds((1, W), F32)],
        scratch_shapes=[pltpu.VMEM((Tc, H2), F32), pltpu.VMEM((Tc, H2), F32), pltpu.VMEM((Tc, H2), F32),
                        pltpu.VMEM((1, H2), F32)],
        compiler_params=_cp(("arbitrary", "arbitrary")),
    )(u, dyg, ypre, d_skip, bsb.astype(BF16), bsb_t, csb_t, cf, crv, ss)


def _lru_blockdiag(w_rg, w_ig):
    nb, bs, _ = w_rg.shape
    sbw = bs * LANES // math.gcd(bs, LANES)
    bps = sbw // bs
    nsb = nb // bps

    def bd(w):
        return _spread_diag(w.reshape(nsb, bps, bs, bs)).reshape(nsb, sbw, sbw)

    return bd(w_rg), bd(w_ig)


def _lru_blockdiag_grad(dwsb, nb, bs):
    nsb, sbw, _ = dwsb.shape
    bps = sbw // bs
    return _take_diag(dwsb.reshape(nsb, bps, bs, bps, bs)).reshape(nb, bs, bs)


def lru_conv_fwd(p, conv_w, conv_b, name):
    L = p.shape[0]
    E = conv_w.shape[1]
    tc = _tile(E, (256, 128))
    noff = E // tc
    kw = conv_w.shape[0]

    def body(xb_ref, w_ref, b_ref, xc_ref, xcb_ref):
        xb = xb_ref[...]
        rows = lax.broadcasted_iota(jnp.int32, xb.shape, 0)
        acc = w_ref[pl.ds(kw - 1, 1), :] * xb + b_ref[...]
        for k in range(kw - 1):
            sh = kw - 1 - k
            acc = acc + w_ref[pl.ds(k, 1), :] * jnp.where(rows >= sh, pltpu.roll(xb, sh, 0), 0.0)
        xc_ref[...] = acc
        xcb_ref[...] = acc.astype(BF16)

    return pl.pallas_call(
        body, name=name, grid=(noff,),
        in_specs=[pl.BlockSpec((L, tc), lambda t: (0, noff + t)),
                  pl.BlockSpec((kw, tc), lambda t: (0, t)), pl.BlockSpec((1, tc), lambda t: (0, t))],
        out_specs=[pl.BlockSpec((L, tc), lambda t: (0, t))] * 2,
        out_shape=[_sds((L, E), F32), _sds((L, E), BF16)],
        compiler_params=_cp(("parallel",)),
    )(p, conv_w, conv_b)


def lru_conv_bwd(d1, d2, d3, p, conv_w, name):
    L = p.shape[0]
    E = conv_w.shape[1]
    tc = _tile(E, (256, 128))
    noff = E // tc
    kw = conv_w.shape[0]

    def body(d1_ref, d2_ref, d3_ref, xb_ref, w_ref, dxb_ref, dw_ref, db_ref):
        dxc = d1_ref[...] + d2_ref[...] + d3_ref[...]
        xb = xb_ref[...]
        rows = lax.broadcasted_iota(jnp.int32, xb.shape, 0)
        db_ref[...] = _colsum(dxc)
        acc = w_ref[pl.ds(kw - 1, 1), :] * dxc
        dw_ref[pl.ds(kw - 1, 1), :] = _colsum(dxc * xb)
        for k in range(kw - 1):
            sh = kw - 1 - k
            dw_ref[pl.ds(k, 1), :] = _colsum(dxc * jnp.where(rows >= sh, pltpu.roll(xb, sh, 0), 0.0))
            acc = acc + w_ref[pl.ds(k, 1), :] * jnp.where(rows < L - sh, pltpu.roll(dxc, L - sh, 0), 0.0)
        dxb_ref[...] = acc.astype(BF16)

    return pl.pallas_call(
        body, name=name, grid=(noff,),
        in_specs=[pl.BlockSpec((L, tc), lambda t: (0, t))] * 3 +
                 [pl.BlockSpec((L, tc), lambda t: (0, noff + t)), pl.BlockSpec((kw, tc), lambda t: (0, t))],
        out_specs=[pl.BlockSpec((L, tc), lambda t: (0, t)), pl.BlockSpec((kw, tc), lambda t: (0, t)),
                   pl.BlockSpec((1, tc), lambda t: (0, t))],
        out_shape=[_sds((L, E), BF16), _sds((kw, E), F32), _sds((1, E), F32)],
        compiler_params=_cp(("parallel",)),
    )(d1, d2, d3, p, conv_w)


def _lru_gates(pr, pi_, brg, big, sp):
    r = _sig(pr + brg)
    ig = _sig(pi_ + big)
    la = -LRU_C * r * sp
    a = jnp.exp(la)
    mult = jnp.sqrt(_neg_expm1(2.0 * la))
    return r, ig, a, mult


def _lru_specs(L, E):
    tc = _tile(E, (LRU_TILE, LANES))
    col = pl.BlockSpec((L, tc), lambda t: (0, t))
    vec = pl.BlockSpec((1, tc), lambda t: (0, t))
    return tc, col, vec


def lru_scan_fwd(pre_r, pre_i, xc, p, b_rg, b_ig, lam, name):
    L, E = xc.shape
    tc, col, vec = _lru_specs(L, E)
    nblk = L // SUBLANES

    def body(pr_ref, pi_ref, xc_ref, gb_ref, brg_ref, big_ref, lam_ref, hs_ref, yv_ref):
        sp = _softplus(-lam_ref[...])
        brg, big = brg_ref[...], big_ref[...]
        rows = lax.broadcasted_iota(jnp.int32, (SUBLANES, tc), 0)

        def blk(k, carry):
            r0 = pl.multiple_of(k * SUBLANES, SUBLANES)
            sl = pl.ds(r0, SUBLANES)
            _, ig, a, mult = _lru_gates(pr_ref[sl, :], pi_ref[sl, :], brg, big, sp)
            b = mult * ig * xc_ref[sl, :]
            for d in (1, 2, 4):
                keep = rows >= d
                b = b + a * jnp.where(keep, pltpu.roll(b, d, 0), 0.0)
                a = a * jnp.where(keep, pltpu.roll(a, d, 0), 1.0)
            h = b + a * carry
            hs_ref[sl, :] = h
            return _row(h, SUBLANES - 1)

        def trip(kt, carry):
            for q in range(SCAN_UNROLL):
                carry = blk(kt * SCAN_UNROLL + q, carry)
            return carry

        lax.fori_loop(0, nblk // SCAN_UNROLL, trip, jnp.zeros((1, tc), F32))
        yv_ref[...] = (hs_ref[...] * _gelu(gb_ref[...])).astype(BF16)

    return pl.pallas_call(
        body, name=name, grid=(E // tc,),
        in_specs=[col, col, col, col, vec, vec, vec],
        out_specs=[col, col], out_shape=[_sds((L, E), F32), _sds((L, E), BF16)],
        compiler_params=_cp(("parallel",)),
    )(pre_r, pre_i, xc, p, b_rg, b_ig, lam)


def lru_scan_bwd(dyv, hs, pre_r, pre_i, xc, p, b_rg, b_ig, lam, name):
    L, E = xc.shape
    tc, col, vec = _lru_specs(L, E)
    nblk = L // SUBLANES

    def body(dyv_ref, hs_ref, pr_ref, pi_ref, xc_ref, gb_ref, brg_ref, big_ref, lam_ref,
             dgb_ref, dpr_ref, dpi_ref, dxc_ref, dbrg_ref, dbig_ref, dlam_ref, t_gb, t_pr, t_pi):
        lam_v = lam_ref[...]
        sp = _softplus(-lam_v)
        brg, big = brg_ref[...], big_ref[...]
        rows = lax.broadcasted_iota(jnp.int32, (SUBLANES, tc), 0)

        def blk(kk, carry):
            gcar, a_next, acc_sp, acc_r, acc_i = carry
            k = nblk - 1 - kk
            r0 = pl.multiple_of(k * SUBLANES, SUBLANES)
            sl = pl.ds(r0, SUBLANES)
            r, ig, a, mult = _lru_gates(pr_ref[sl, :], pi_ref[sl, :], brg, big, sp)
            gbv, hsv, dyvv, xcv = gb_ref[sl, :], hs_ref[sl, :], dyv_ref[sl, :], xc_ref[sl, :]
            t_gb[sl, :] = dyvv * hsv * _gelu_grad(gbv)
            x = dyvv * _gelu(gbv)
            al = jnp.where(rows == SUBLANES - 1, a_next, pltpu.roll(a, SUBLANES - 1, 0))
            for d in (1, 2, 4):
                keep = rows < SUBLANES - d
                x = x + al * jnp.where(keep, pltpu.roll(x, SUBLANES - d, 0), 0.0)
                al = al * jnp.where(keep, pltpu.roll(al, SUBLANES - d, 0), 1.0)
            g = x + al * gcar
            rp = pl.multiple_of(jnp.maximum(k - 1, 0) * SUBLANES, SUBLANES)
            hlast = _row(hs_ref[pl.ds(rp, SUBLANES), :], SUBLANES - 1) * (k > 0).astype(F32)
            hprev = jnp.where(rows == 0, hlast, pltpu.roll(hsv, 1, 0))
            da = g * hprev
            dmult = g * ig * xcv
            dig = g * mult * xcv
            dxc_ref[sl, :] = g * mult * ig
            dla = da * a - dmult * (a * a) / mult
            dpr = dla * (-LRU_C * sp) * r * (1.0 - r)
            dpi = dig * ig * (1.0 - ig)
            t_pr[sl, :] = dpr
            t_pi[sl, :] = dpi
            return (_row(g, 0), _row(a, 0), acc_sp + dla * (-LRU_C * r), acc_r + dpr, acc_i + dpi)

        zero = jnp.zeros((SUBLANES, tc), F32)
        z1 = jnp.zeros((1, tc), F32)
        def trip(kt, carry):
            for q in range(SCAN_UNROLL):
                carry = blk(kt * SCAN_UNROLL + q, carry)
            return carry

        _, _, acc_sp, acc_r, acc_i = lax.fori_loop(0, nblk // SCAN_UNROLL, trip, (z1, z1, zero, zero, zero))
        dgb_ref[...] = t_gb[...].astype(BF16)
        dpr_ref[...] = t_pr[...].astype(BF16)
        dpi_ref[...] = t_pi[...].astype(BF16)
        dbrg_ref[...] = _colsum(acc_r)
        dbig_ref[...] = _colsum(acc_i)
        dlam_ref[...] = -_colsum(acc_sp) * _sig(-lam_v)

    return pl.pallas_call(
        body, name=name, grid=(E // tc,),
        in_specs=[col, col, col, col, col, col, vec, vec, vec],
        out_specs=[col, col, col, col, vec, vec, vec],
        out_shape=[_sds((L, E), BF16), _sds((L, E), BF16), _sds((L, E), BF16), _sds((L, E), F32),
                   _sds((1, E), F32), _sds((1, E), F32), _sds((1, E), F32)],
        scratch_shapes=[pltpu.VMEM((L, tc), F32)] * 3,
        compiler_params=_cp(("parallel",)),
    )(dyv, hs, pre_r, pre_i, xc, p, b_rg, b_ig, lam)


def _place():
    xi, yi, ci = lax.axis_index("x"), lax.axis_index("y"), lax.axis_index("c")
    chips = [(1 - xi, yi), (xi, 1 - yi), (1 - xi, 1 - yi)]
    return xi, yi, ci, chips


_ANY = pl.BlockSpec(memory_space=pl.ANY)


def all_gather_devices(blks, name):
    n = len(blks)

    def body(*refs):
        ins, outs = refs[:n], refs[n:2 * n]
        send_sems, recv_sems, local_sems = refs[2 * n:]
        xi, yi, ci, chips = _place()
        me, sibling = (xi, yi, ci), (xi, yi, 1 - ci)

        def slab(a, px, py, pc):
            return outs[a].at[4 * px + 2 * py + pc]

        def copy(a, k, block, to, src=None):
            return pltpu.make_async_remote_copy(
                src_ref=slab(a, *block) if src is None else src, dst_ref=slab(a, *block),
                send_sem=send_sems.at[7 * a + k], recv_sem=recv_sems.at[7 * a + k], device_id=to,
                device_id_type=MESH)

        mine = [pltpu.make_async_copy(ins[a], slab(a, *me), local_sems.at[a]) for a in range(n)]
        first, passed = [], []
        for a in range(n):
            mine[a].start()
            first.append(copy(a, 0, me, sibling, src=ins[a]))
            first += [copy(a, 1 + j, me, (*chip, ci), src=ins[a]) for j, chip in enumerate(chips)]
        for cp in first:
            cp.start()
        for a in range(n):
            for j, chip in enumerate(chips):
                copy(a, 1 + j, (*chip, ci), me).wait_recv()
                passed.append(copy(a, 4 + j, (*chip, ci), sibling))
                passed[-1].start()
        for a in range(n):
            copy(a, 0, sibling, me).wait_recv()
            for j, chip in enumerate(chips):
                copy(a, 4 + j, (*chip, 1 - ci), me).wait_recv()
        for cp in first + passed:
            cp.wait_send()
        for cp in mine:
            cp.wait()

    return pl.pallas_call(
        body, name=name, in_specs=[_ANY] * n, out_specs=[_ANY] * n,
        out_shape=[_sds((N_DEV,) + b.shape, b.dtype) for b in blks],
        scratch_shapes=[pltpu.SemaphoreType.DMA((7 * n,)), pltpu.SemaphoreType.DMA((7 * n,)),
                        pltpu.SemaphoreType.DMA((n,))],
    )(*blks)


_HBM = pl.BlockSpec(memory_space=pltpu.HBM)
_SEM = pl.BlockSpec(memory_space=pltpu.SEMAPHORE)
_EFFECT = pltpu.SideEffectType.DATAFLOW_SIDE_EFFECTING


def split_start(name, groups, counts, copies_fn):
    flat = [b for g in groups for b in g]
    n, ng = len(flat), len(groups)

    def body(*refs):
        ins, sems, token = refs[:n], refs[n:n + 2 * ng], refs[-1]
        off = 0
        for gi, g in enumerate(groups):
            for cp in copies_fn(ins[off:off + len(g)], [b.shape for b in g], sems[2 * gi], sems[2 * gi + 1]):
                cp.start()
            off += len(g)
        token[...] = jnp.zeros_like(token)

    out_shape = tuple(pltpu.SemaphoreType.DMA((c,)) for c in counts for _ in range(2))
    out_shape += tuple(pltpu.HBM(b.shape, b.dtype) for b in flat) + (_sds((SUBLANES, LANES), F32),)
    outs = pl.pallas_call(
        body, name=name, in_specs=[_HBM] * n, out_shape=out_shape,
        out_specs=tuple([_SEM] * (2 * ng) + [_HBM] * n + [pl.BlockSpec(memory_space=pltpu.VMEM)]),
        input_output_aliases={i: 2 * ng + i for i in range(n)},
        compiler_params=pltpu.CompilerParams(has_side_effects=_EFFECT),
    )(*[pltpu.with_memory_space_constraint(b, pltpu.HBM) for b in flat])
    sems = [(outs[2 * gi], outs[2 * gi + 1]) for gi in range(ng)]
    thru, off = [], 2 * ng
    for g in groups:
        thru.append(list(outs[off:off + len(g)]))
        off += len(g)
    return sems, thru, outs[-1]


def split_wait(name, bufs, sems, copies_fn, after):
    n = len(bufs)
    shapes = [b.shape for b in bufs]

    def body(*refs):
        for cp in copies_fn(refs[:n], shapes, refs[n], refs[n + 1]):
            cp.wait_send()
            cp.wait_recv()

    return list(pl.pallas_call(
        body, name=name, in_specs=[_HBM] * n + [_SEM, _SEM, _ANY],
        out_shape=tuple(pltpu.HBM(b.shape, b.dtype) for b in bufs), out_specs=tuple([_HBM] * n),
        input_output_aliases={i: i for i in range(n)},
        compiler_params=pltpu.CompilerParams(has_side_effects=_EFFECT),
    )(*bufs, sems[0], sems[1], after))


def _gather_copies(bufs, shapes, send_sems, recv_sems):
    xi, yi, ci, chips = _place()
    cps = []
    for a, ref in enumerate(bufs):
        hr = shapes[a][1] // 2
        rows = ref.at[2 * xi + yi, pl.ds(pl.multiple_of(ci * hr, 16), hr), :]
        for j in range(3):
            cps.append(pltpu.make_async_remote_copy(
                src_ref=rows, dst_ref=rows, send_sem=send_sems.at[3 * a + j], recv_sem=recv_sems.at[3 * a + j],
                device_id=(*chips[j], ci), device_id_type=MESH))
    return cps


def _scatter_copies(bufs, shapes, send_sems, recv_sems):
    xi, yi, ci, chips = _place()
    n = len(bufs) // 2
    cps = []
    for a in range(n):
        for j in range(3):
            cps.append(pltpu.make_async_remote_copy(
                src_ref=bufs[a].at[2 * chips[j][0] + chips[j][1]], dst_ref=bufs[n + a].at[j],
                send_sem=send_sems.at[3 * a + j], recv_sem=recv_sems.at[3 * a + j],
                device_id=(*chips[j], ci), device_id_type=MESH))
    return cps


def cast_place(w, layer, chip1, name, after=None):
    _, R, C = w.shape
    tr = _tile(R, tuple(t for t in _TR if t * C <= ROW_TILE_ELEMS) or (16,))
    extra = [] if after is None else [after]

    def body(c_ref, w_ref, *rest):
        rest[-1][...] = w_ref[...].astype(BF16)

    return pl.pallas_call(
        body, name=name,
        grid_spec=pltpu.PrefetchScalarGridSpec(
            num_scalar_prefetch=1, grid=(R // tr,),
            in_specs=[pl.BlockSpec((None, tr, C), lambda i, c: (layer, i, 0))] + [_ANY] * len(extra),
            out_specs=pl.BlockSpec((None, tr, C), lambda i, c: (c[0], i, 0))),
        out_shape=_sds((N_CHIPS, R, C), BF16), compiler_params=_cp(("parallel",)),
    )(chip1, w, *extra)


def forward_halves(bufs, name):
    n = len(bufs)

    def body(*refs):
        ins, outs = refs[:n], refs[n:2 * n]
        send_sems, recv_sems = refs[2 * n:]
        xi, yi, ci, chips = _place()

        def copy(ref, a, j, h):
            hr = bufs[a].shape[1] // 2
            rows = pl.ds(pl.multiple_of(h * hr, 16), hr)
            slot = 2 * chips[j][0] + chips[j][1]
            return pltpu.make_async_remote_copy(
                src_ref=ref[a].at[slot, rows, :], dst_ref=outs[a].at[slot, rows, :],
                send_sem=send_sems.at[3 * a + j], recv_sem=recv_sems.at[3 * a + j],
                device_id=(xi, yi, 1 - ci), device_id_type=MESH)

        sends = [copy(ins, a, j, ci) for a in range(n) for j in range(3)]
        for cp in sends:
            cp.start()
        for a in range(n):
            for j in range(3):
                copy(outs, a, j, 1 - ci).wait_recv()
        for cp in sends:
            cp.wait_send()

    return pl.pallas_call(
        body, name=name, in_specs=[_ANY] * n, out_specs=[_ANY] * n,
        out_shape=[_sds(b.shape, b.dtype) for b in bufs], input_output_aliases={i: i for i in range(n)},
        scratch_shapes=[pltpu.SemaphoreType.DMA((3 * n,)), pltpu.SemaphoreType.DMA((3 * n,))],
    )(*bufs)


def exchange_halves(grads, name):
    n = len(grads)

    def body(*refs):
        ins, outs = refs[:n], refs[n:2 * n]
        send_sems, recv_sems = refs[2 * n:]
        xi, yi, ci, _ = _place()
        cps = []
        for a in range(n):
            hr = grads[a].shape[1] // 2
            src = ins[a].at[:, pl.ds(pl.multiple_of((1 - ci) * hr, 16), hr), :]
            cps.append(pltpu.make_async_remote_copy(
                src_ref=src, dst_ref=outs[a], send_sem=send_sems.at[a], recv_sem=recv_sems.at[a],
                device_id=(xi, yi, 1 - ci), device_id_type=MESH))
            cps[-1].start()
        for cp in cps:
            cp.wait()

    return pl.pallas_call(
        body, name=name, in_specs=[_ANY] * n, out_specs=[_ANY] * n,
        out_shape=[_sds((N_CHIPS, g.shape[1] // 2, g.shape[2]), g.dtype) for g in grads],
        scratch_shapes=[pltpu.SemaphoreType.DMA((n,)), pltpu.SemaphoreType.DMA((n,))],
    )(*grads)


def add_half(g, got, ci, name):
    S, hr, C = got.shape
    tr = _tile(hr, tuple(t for t in _TR if t * C <= ROW_TILE_ELEMS) or (16,))
    nb = hr // tr

    def body(c_ref, g_ref, r_ref, o_ref):
        o_ref[...] = (g_ref[...].astype(F32) + r_ref[...].astype(F32)).astype(BF16)

    return pl.pallas_call(
        body, name=name,
        grid_spec=pltpu.PrefetchScalarGridSpec(
            num_scalar_prefetch=1, grid=(S, nb),
            in_specs=[pl.BlockSpec((None, tr, C), lambda s, i, c: (s, c[0] * nb + i, 0)),
                      pl.BlockSpec((None, tr, C), lambda s, i, c: (s, i, 0))],
            out_specs=pl.BlockSpec((None, tr, C), lambda s, i, c: (s, i, 0))),
        out_shape=_sds((S, hr, C), BF16), compiler_params=_cp(("parallel", "parallel")),
    )(ci, g, got)


def add_chips(part, got, place, name):
    S, hr, C = part.shape
    tr = _tile(hr, tuple(t for t in _TR if t * C <= ROW_TILE_ELEMS) or (16,))
    nb = hr // tr

    def body(c_ref, p_ref, r_ref, o_ref):
        acc = p_ref[...].astype(F32)
        for j in range(3):
            acc = acc + r_ref[j].astype(F32)
        o_ref[...] = acc

    return pl.pallas_call(
        body, name=name,
        grid_spec=pltpu.PrefetchScalarGridSpec(
            num_scalar_prefetch=1, grid=(nb,),
            in_specs=[pl.BlockSpec((None, tr, C), lambda i, c: (c[0], i, 0)),
                      pl.BlockSpec((3, tr, C), lambda i, c: (0, i, 0))],
            out_specs=pl.BlockSpec((tr, C), lambda i, c: (c[1] * nb + i, 0))),
        out_shape=_sds((2 * hr, C), F32), compiler_params=_cp(("parallel",)),
    )(place, part, got)


def join_halves(bufs, name):
    n = len(bufs)

    def body(*refs):
        ins, outs = refs[:n], refs[n:2 * n]
        send_sems, recv_sems = refs[2 * n:]
        xi, yi, ci, _ = _place()

        def copy(ref, a, h):
            hr = bufs[a].shape[0] // 2
            rows = pl.ds(pl.multiple_of(h * hr, 8), hr)
            return pltpu.make_async_remote_copy(
                src_ref=ref[a].at[rows, :], dst_ref=outs[a].at[rows, :], send_sem=send_sems.at[a],
                recv_sem=recv_sems.at[a], device_id=(xi, yi, 1 - ci), device_id_type=MESH)

        sends = [copy(ins, a, ci) for a in range(n)]
        for cp in sends:
            cp.start()
        for a in range(n):
            copy(outs, a, 1 - ci).wait_recv()
        for cp in sends:
            cp.wait_send()

    return pl.pallas_call(
        body, name=name, in_specs=[_ANY] * n, out_specs=[_ANY] * n,
        out_shape=[_sds(b.shape, b.dtype) for b in bufs], input_output_aliases={i: i for i in range(n)},
        scratch_shapes=[pltpu.SemaphoreType.DMA((n,)), pltpu.SemaphoreType.DMA((n,))],
    )(*bufs)


def reduce_scatter_start(grads, ci1, tag):
    got = exchange_halves(grads, f"rs_halves_{tag}")
    parts = [add_half(g, r, ci1, f"rs_add_half_{tag}{a}") for a, (g, r) in enumerate(zip(grads, got))]
    lands = [lax.empty((3,) + q.shape[1:], q.dtype) for q in parts]
    sems, thru, token = split_start(f"rs_ici_start_{tag}", [parts + lands], [3 * len(parts)], _scatter_copies)
    return (sems[0], thru[0]), token


def reduce_scatter_finish(state, place, after, tag):
    sems, bufs = state
    bufs = split_wait(f"rs_ici_wait_{tag}", bufs, sems, _scatter_copies, after)
    n = len(bufs) // 2
    return [add_chips(bufs[a], bufs[n + a], place, f"rs_add_chips_{tag}{a}") for a in range(n)]


def _pack(parts, width):
    flat = jnp.concatenate([p.reshape(-1).astype(F32) for p in parts])
    per = SUBLANES * width
    total = -(-flat.shape[0] // per) * per
    flat = jnp.pad(flat, (0, total - flat.shape[0]))
    return flat.reshape(total // width, width)


def _unpack(flat, shapes):
    out, off = [], 0
    for s in shapes:
        n = math.prod(s)
        out.append(flat[off:off + n].reshape(s))
        off += n
    return out


_W_NAMES = ['norm_g', 'w_ada', 'b_ada', 's5_w_in', 's5_lam_re', 's5_lam_im', 's5_log_dt', 's5_b_re', 's5_b_im',
            's5_c_re', 's5_c_im', 's5_d', 's5_w_glu', 'lru_w_in', 'lru_conv_w', 'lru_conv_b', 'lru_w_rg',
            'lru_b_rg', 'lru_w_ig', 'lru_b_ig', 'lru_lam', 'lru_w_out', 'ffn_w_gu', 'ffn_w_down', 'final_g']
_BIG = ('w_ada', 's5_w_in', 's5_w_glu', 'lru_w_in', 'lru_w_out', 'ffn_w_gu', 'ffn_w_down')
_MID = ('s5_b_re', 's5_b_im', 's5_c_re', 's5_c_im', 'lru_w_rg', 'lru_w_ig')


def _ffn_fwd(x, h, w_gu, w_down, gate, tag):
    gu = mm_nn(h, w_gu, name=f"{tag}_gu", out_dtype=BF16, bmode="cols")
    act = swiglu_fwd(gu, f"{tag}_act")
    z = mm_nn(act, w_down, name=f"{tag}_down")
    return res_gate_fwd(x, z, gate, f"{tag}_res"), (gu, act, z)


def _ffn_bwd(dx, h, saved, w_gu, w_down, gate, tag):
    gu, act, z = saved
    dz, dgate = res_gate_bwd(dx, z, gate, f"{tag}_res_bwd")
    dact = mm_nt(dz, w_down, name=f"{tag}_dact")
    dw_down = mm_tn(act, dz, name=f"{tag}_dwdown", out_dtype=BF16)
    dgu = swiglu_bwd(dact, gu, f"{tag}_act_bwd")
    dh = mm_nt(dgu, w_gu, name=f"{tag}_dh", bmode="cols")
    dw_gu = mm_tn(h, dgu, name=f"{tag}_dwgu", out_dtype=BF16, omode="cols", groups=N_CHIPS)
    return dh, dgate, dw_gu, dw_down.reshape((N_CHIPS, -1) + dw_down.shape[1:])


def _step(p):
    xi, yi, ci = lax.axis_index("x"), lax.axis_index("y"), lax.axis_index("c")
    chip = 2 * xi + yi
    me = 2 * chip + ci
    ci1 = jnp.reshape(ci, (1,)).astype(jnp.int32)
    chip1 = jnp.reshape(chip, (1,)).astype(jnp.int32)
    place2 = jnp.stack([chip, ci]).astype(jnp.int32)

    x0 = p['x'][0]
    tgt = p['loss_target'][0]
    L, D = x0.shape
    Dq = D // N_CHIPS
    depth = p['w_ada'].shape[0]
    E = p['lru_lam'].shape[1] * N_CHIPS
    Eq = E // N_CHIPS
    kw = p['lru_conv_w'].shape[1]
    Nq = p['w_ada'].shape[2]
    _, G, P, Cg = p['s5_b_re'].shape
    nb, bs = p['lru_w_rg'].shape[1], p['lru_w_rg'].shape[2]

    pay = _pack([p['c'], p['norm_g'], p['lru_conv_w'], p['lru_conv_b'], p['lru_b_rg'], p['lru_b_ig'],
                 p['lru_lam']], 1024)
    g1 = all_gather_devices([pay], "gather_small_params")[0].reshape(N_DEV, -1)
    c_all = g1[:, :D]
    per_chip = g1[0::2]
    sizes = [(depth, 2, Dq), (kw, Eq), (Eq,), (Eq,), (Eq,), (Eq,)]
    offs = D
    pieces = []
    for s in sizes:
        nel = math.prod(s)
        pieces.append(per_chip[:, offs:offs + nel].reshape((N_CHIPS,) + s))
        offs += nel
    norm_g = jnp.moveaxis(pieces[0], 0, 2).reshape(depth, 2, D)
    conv_w = jnp.moveaxis(pieces[1], 0, 1).reshape(kw, E)
    conv_b, b_rg, b_ig, lam = [q.reshape(1, E) for q in pieces[2:]]

    cond = silu_rows(jnp.pad(c_all, ((0, 16 - N_DEV), (0, 0))), "cond_silu")
    cond_rep = jnp.concatenate([cond] * depth, axis=1)
    mod_part = mm_nn(cond_rep, p['w_ada'], name="mod_proj", bmode="batch")[:N_DEV]
    g2 = all_gather_devices([mod_part], "gather_mod")[0][0::2]
    mine = lax.dynamic_index_in_dim(g2, me, axis=1, keepdims=False).reshape(N_CHIPS, depth, Nq)
    mod = jnp.moveaxis(mine, 0, 1).reshape(depth, N_CHIPS * Nq) + p['b_ada']
    mods = [[mod[i:i + 1, k * D:(k + 1) * D] for k in range(N_MOD)] for i in range(depth)]

    def place(key, layer, tag, after=None):
        return cast_place(p[key], layer, chip1, f"place_{tag}", after=after)

    first = [place('s5_w_in', 0, "s5_in", after=mod), place('s5_w_glu', 0, "s5_glu")]
    sems_a, bufs_a, tok_a = split_start("gather_ici_start_s5", [first], [3 * len(first)], _gather_copies)
    others = [[place('ffn_w_gu', 0, "gu0", after=tok_a), place('ffn_w_down', 0, "down0")],
              [place('lru_w_in', 0, "lru_in"), place('lru_w_out', 0, "lru_out")],
              [place('ffn_w_gu', 1, "gu1"), place('ffn_w_down', 1, "down1")]]
    sems_b, bufs_b, tok_b = split_start("gather_ici_start_rest", others, [3 * len(g) for g in others], _gather_copies)
    wsems, wbufs = sems_a + sems_b, bufs_a + bufs_b

    def weights(gi, after, tag):
        landed = split_wait(f"gather_ici_wait_{tag}", wbufs[gi], wsems[gi], _gather_copies, after)
        return forward_halves(landed, f"gather_forward_{tag}")

    s5_small = (p['s5_lam_re'][0], p['s5_lam_im'][0], p['s5_log_dt'][0], p['s5_b_re'][0], p['s5_b_im'][0])
    (ab_re, ab_im, bb_re, bb_im), s5_disc_vjp = jax.vjp(_s5_discretize, *s5_small)
    bsb, csb = _s5_blockdiag(bb_re, bb_im, p['s5_c_re'][0], p['s5_c_im'][0])
    cf, crv = _s5_scan_consts(ab_re, ab_im)
    wsb_rg, wsb_ig = [w.astype(BF16) for w in _lru_blockdiag(p['lru_w_rg'][0], p['lru_w_ig'][0])]
    nsb_lru = wsb_rg.shape[0]

    sh1, sc1, gt1, sh2, sc2, gt2 = mods[0]
    s5_w_in, s5_w_glu = weights(0, tok_b, "s5")
    s5_w_in = s5_w_in.reshape(-1, D)
    h0 = norm_mod_fwd(x0, norm_g[0, 0:1], sc1, sh1, "l0_norm1")
    u = mm_nn(h0, s5_w_in, name="s5_in")
    ypre, yg, ss = s5_scan_fwd(u, p['s5_d'], bsb, csb, cf, "s5_scan")
    v = mm_nn(yg, s5_w_glu, name="s5_glu", bmode="cols")
    w_gu0, w_down0 = weights(1, v, "ffn0")
    w_down0 = w_down0.reshape(-1, D)
    x1 = glu_res_fwd(x0, v, gt1, "s5_res")
    h1 = norm_mod_fwd(x1, norm_g[0, 1:2], sc2, sh2, "l0_norm2")
    x2, ffn0 = _ffn_fwd(x1, h1, w_gu0, w_down0, gt2, "ffn0")

    sh1b, sc1b, gt1b, sh2b, sc2b, gt2b = mods[1]
    lru_w_in, lru_w_out = weights(2, ffn0[0], "lru")
    lru_w_out = lru_w_out.reshape(-1, D)
    h2 = norm_mod_fwd(x2, norm_g[1, 0:1], sc1b, sh1b, "l1_norm1")
    pq = mm_nn(h2, lru_w_in, name="lru_in", bmode="cols")
    xc, xcb = lru_conv_fwd(pq, conv_w, conv_b, "lru_conv")
    pre_r = mm_nn(xcb, wsb_rg, name="lru_gate_r", bmode="batch")
    pre_i = mm_nn(xcb, wsb_ig, name="lru_gate_i", bmode="batch")
    hs, yv = lru_scan_fwd(pre_r, pre_i, xc, pq, b_rg, b_ig, lam, "lru_scan")
    w_gu1, w_down1 = weights(3, hs, "ffn1")
    w_down1 = w_down1.reshape(-1, D)
    mix = mm_nn(yv, lru_w_out, name="lru_out")
    x3 = res_gate_fwd(x2, mix, gt1b, "lru_res")
    h3 = norm_mod_fwd(x3, norm_g[1, 1:2], sc2b, sh2b, "l1_norm2")
    x4, ffn1 = _ffn_fwd(x3, h3, w_gu1, w_down1, gt2b, "ffn1")

    fg = p['final_g'].reshape(1, D)
    dx4, loss_blk, dfinal_g = final_loss(x4, fg, tgt, "final_loss")
    loss = lax.psum(loss_blk[0, 0], ("x", "y", "c"))

    def rows4(g):
        return g.reshape((N_CHIPS, -1) + g.shape[1:])

    dh3, dgt2b, dw_gu1, dw_down1 = _ffn_bwd(dx4, h3, ffn1, w_gu1, w_down1, gt2b, "ffn1")
    rs_ffn1, tok = reduce_scatter_start([dw_gu1, dw_down1], ci1, "ffn1")
    dx3, dgn11, dsc2b, dsh2b = norm_mod_bwd(dh3, x3, norm_g[1, 1:2], sc2b, dx4, "l1_norm2_bwd", after=tok)

    dmix, dgt1b = res_gate_bwd(dx3, mix, gt1b, "lru_res_bwd")
    dyv = mm_nt(dmix, lru_w_out, name="lru_dyv")
    dw_out = mm_tn(yv, dmix, name="lru_dwout", out_dtype=BF16)
    dgb, dpre_r, dpre_i, dxc1, db_rg, db_ig, dlam = lru_scan_bwd(dyv, hs, pre_r, pre_i, xc, pq, b_rg, b_ig, lam,
                                                                "lru_scan_bwd")
    dxc2 = mm_nt(dpre_r, wsb_rg, name="lru_dxc_r", bmode="batch")
    dxc3 = mm_nt(dpre_i, wsb_ig, name="lru_dxc_i", bmode="batch")
    dwsb_rg = mm_tn(xcb, dpre_r, name="lru_dwgate_r", omode="batch", groups=nsb_lru)
    dwsb_ig = mm_tn(xcb, dpre_i, name="lru_dwgate_i", omode="batch", groups=nsb_lru)
    dxb, dconv_w, dconv_b = lru_conv_bwd(dxc1, dxc2, dxc3, pq, conv_w, "lru_conv_bwd")
    dpq = jnp.concatenate([dgb, dxb], axis=1)
    dh2 = mm_nt(dpq, lru_w_in, name="lru_dh", bmode="cols")
    dw_lru_in = mm_tn(h2, dpq, name="lru_dwin", out_dtype=BF16, omode="cols", groups=N_CHIPS)
    rs_lru, tok = reduce_scatter_start([dw_lru_in, rows4(dw_out)], ci1, "lru")
    dx2, dgn10, dsc1b, dsh1b = norm_mod_bwd(dh2, x2, norm_g[1, 0:1], sc1b, dx3, "l1_norm1_bwd", after=tok)

    dh1, dgt2, dw_gu0, dw_down0 = _ffn_bwd(dx2, h1, ffn0, w_gu0, w_down0, gt2, "ffn0")
    rs_ffn0, tok = reduce_scatter_start([dw_gu0, dw_down0], ci1, "ffn0")
    dx1, dgn01, dsc2, dsh2 = norm_mod_bwd(dh1, x1, norm_g[0, 1:2], sc2, dx2, "l0_norm2_bwd", after=tok)

    dv, dgt1 = glu_res_bwd(dx1, v, gt1, "s5_res_bwd")
    dyg = mm_nt(dv, s5_w_glu, name="s5_dyg", bmode="cols")
    dw_glu = mm_tn(yg, dv, name="s5_dwglu", out_dtype=BF16, omode="cols", groups=N_CHIPS)
    du, dbsb, dcsb, da, dd = s5_scan_bwd(u, dyg, ypre, p['s5_d'], bsb, csb, cf, crv, ss, "s5_scan_bwd")
    dh0 = mm_nt(du, s5_w_in, name="s5_dh")
    dw_s5_in = mm_tn(h0, du, name="s5_dwin", out_dtype=BF16)
    rs_s5, tok = reduce_scatter_start([rows4(dw_s5_in), dw_glu], ci1, "s5")
    grad_x, dgn00, dsc1, dsh1 = norm_mod_bwd(dh0, x0, norm_g[0, 0:1], sc1, dx1, "l0_norm1_bwd", after=tok)

    dmod = jnp.concatenate([jnp.concatenate([dsh1, dsc1, dgt1, dsh2, dsc2, dgt2], axis=1),
                            jnp.concatenate([dsh1b, dsc1b, dgt1b, dsh2b, dsc2b, dgt2b], axis=1)], axis=0)
    dnorm_g = jnp.stack([jnp.concatenate([dgn00, dgn01]), jnp.concatenate([dgn10, dgn11])])
    dbb_re, dbb_im, dc_re, dc_im = _s5_blockdiag_grads(dbsb, dcsb, P, Cg)
    H = S5_SB_GROUPS * P
    da_re, da_im = da[:, 0, :H].reshape(G, P), da[:, 0, H:].reshape(G, P)
    dw_rg, dw_ig = _lru_blockdiag_grad(dwsb_rg, nb, bs), _lru_blockdiag_grad(dwsb_ig, nb, bs)
    small = [dmod, dnorm_g, da_re, da_im, dd, dconv_w, dconv_b, db_rg, db_ig, dlam, dfinal_g]
    small_shapes = [s.shape for s in small]
    payload = _pack(small, 1024)
    mid = [dbb_re, dbb_im, dc_re, dc_im, dw_rg, dw_ig]
    mid_shapes = [s.shape for s in mid]
    gathered = all_gather_devices([payload] + [s.reshape(s.shape[0], -1) for s in mid], "gather_small_grads")
    gathered_small = gathered[0]
    total = sum_devices(gathered_small, "sum_small_grads").reshape(-1)
    (s_dmod, s_norm_g, s_da_re, s_da_im, s_dd, s_conv_w, s_conv_b, s_b_rg, s_b_ig, s_lam,
     s_final_g) = _unpack(total, small_shapes)
    s_dbb_re, s_dbb_im, s_dc_re, s_dc_im, s_dw_rg, s_dw_ig = [
        sum_devices(g, f"sum_mid_grads_{i}").reshape(s) for i, (g, s) in enumerate(zip(gathered[1:], mid_shapes))]
    g_lam_re, g_lam_im, g_log_dt, g_b_re, g_b_im = s5_disc_vjp((s_da_re, s_da_im, s_dbb_re, s_dbb_im))

    npay = payload.shape[0] * payload.shape[1]
    dmod_all = gathered_small.reshape(N_DEV, npay)[:, :depth * N_MOD * D].reshape(N_DEV, depth, N_CHIPS, Nq)
    dmod_mine = lax.dynamic_index_in_dim(dmod_all, chip, axis=2, keepdims=False).reshape(N_DEV, depth * Nq)
    dmod_mine = jnp.pad(dmod_mine, ((0, 16 - N_DEV), (0, 0)))
    g_w_ada = mm_tn(cond_rep, dmod_mine, name="w_ada_grad", omode="batch", groups=depth)

    def cols(full, width):
        return lax.dynamic_slice_in_dim(full, chip * width, width, axis=full.ndim - 1)

    grads = {
        'norm_g': cols(s_norm_g, Dq), 'w_ada': g_w_ada, 'b_ada': s_dmod,
        's5_lam_re': g_lam_re, 's5_lam_im': g_lam_im, 's5_log_dt': g_log_dt, 's5_b_re': g_b_re, 's5_b_im': g_b_im,
        's5_c_re': s_dc_re, 's5_c_im': s_dc_im, 's5_d': s_dd, 'lru_conv_w': cols(s_conv_w, Eq),
        'lru_conv_b': cols(s_conv_b, Eq), 'lru_w_rg': s_dw_rg, 'lru_b_rg': cols(s_b_rg, Eq),
        'lru_w_ig': s_dw_ig, 'lru_b_ig': cols(s_b_ig, Eq), 'lru_lam': cols(s_lam, Eq), 'final_g': s_final_g,
    }
    grads = {k: g.reshape(p[k].shape) for k, g in grads.items()}

    delta, new_m, new_v = {}, {}, {}

    def adamw_2d(k, rows):
        w2 = p[k].reshape(rows, -1)
        outs = adamw(w2, grads[k].reshape(w2.shape), p['m_' + k].reshape(w2.shape), p['v_' + k].reshape(w2.shape),
                     f"adamw_{k}")
        delta[k], new_m[k], new_v[k] = [o.reshape(p[k].shape) for o in outs]

    adamw_2d('w_ada', depth * D)
    for k in _MID:
        adamw_2d(k, p[k].shape[1])
    rest = [k for k in _W_NAMES if k not in _BIG + _MID]
    shapes = [p[k].shape for k in rest]
    packed = [_pack([src[pre_ + k] if pre_ else src[k] for k in rest], 1024)
              for src, pre_ in ((p, ''), (grads, ''), (p, 'm_'), (p, 'v_'))]
    outs = adamw(*packed, "adamw_small")
    for dst, o in zip((delta, new_m, new_v), outs):
        for k, val in zip(rest, _unpack(o.reshape(-1), shapes)):
            dst[k] = val

    done = delta['w_ada']
    halves = []
    for state, tag in ((rs_ffn1, "ffn1"), (rs_lru, "lru"), (rs_ffn0, "ffn0"), (rs_s5, "s5")):
        halves += reduce_scatter_finish(state, place2, done, tag)
    g_gu1, g_down1, g_lru_in, g_lru_out, g_gu0, g_down0, g_s5_in, g_s5_glu = join_halves(halves, "rs_join_halves")
    grads.update({'s5_w_in': g_s5_in[None], 's5_w_glu': g_s5_glu[None], 'lru_w_in': g_lru_in[None],
                  'lru_w_out': g_lru_out[None], 'ffn_w_gu': jnp.stack([g_gu0, g_gu1]),
                  'ffn_w_down': jnp.stack([g_down0, g_down1])})
    for k in _BIG[1:]:
        adamw_2d(k, math.prod(p[k].shape[:-1]))

    return (loss, grad_x[None], *[grads[k] for k in _W_NAMES], *[delta[k] for k in _W_NAMES],
            *[new_m[k] for k in _W_NAMES], *[new_v[k] for k in _W_NAMES])


_IN_NAMES = (['x', 'c'] + _W_NAMES + ['loss_target'] + ['m_' + k for k in _W_NAMES] + ['v_' + k for k in _W_NAMES])


def kernel(x, c, norm_g, w_ada, b_ada, s5_w_in, s5_lam_re, s5_lam_im, s5_log_dt, s5_b_re, s5_b_im, s5_c_re, s5_c_im, s5_d, s5_w_glu, lru_w_in, lru_conv_w, lru_conv_b, lru_w_rg, lru_b_rg, lru_w_ig, lru_b_ig, lru_lam, lru_w_out, ffn_w_gu, ffn_w_down, final_g, loss_target, m_norm_g, m_w_ada, m_b_ada, m_s5_w_in, m_s5_lam_re, m_s5_lam_im, m_s5_log_dt, m_s5_b_re, m_s5_b_im, m_s5_c_re, m_s5_c_im, m_s5_d, m_s5_w_glu, m_lru_w_in, m_lru_conv_w, m_lru_conv_b, m_lru_w_rg, m_lru_b_rg, m_lru_w_ig, m_lru_b_ig, m_lru_lam, m_lru_w_out, m_ffn_w_gu, m_ffn_w_down, m_final_g, v_norm_g, v_w_ada, v_b_ada, v_s5_w_in, v_s5_lam_re, v_s5_lam_im, v_s5_log_dt, v_s5_b_re, v_s5_b_im, v_s5_c_re, v_s5_c_im, v_s5_d, v_s5_w_glu, v_lru_w_in, v_lru_conv_w, v_lru_conv_b, v_lru_w_rg, v_lru_b_rg, v_lru_w_ig, v_lru_b_ig, v_lru_lam, v_lru_w_out, v_ffn_w_gu, v_ffn_w_down, v_final_g):
    args = (x, c, norm_g, w_ada, b_ada, s5_w_in, s5_lam_re, s5_lam_im, s5_log_dt, s5_b_re, s5_b_im, s5_c_re, s5_c_im, s5_d, s5_w_glu, lru_w_in, lru_conv_w, lru_conv_b, lru_w_rg, lru_b_rg, lru_w_ig, lru_b_ig, lru_lam, lru_w_out, ffn_w_gu, ffn_w_down, final_g, loss_target, m_norm_g, m_w_ada, m_b_ada, m_s5_w_in, m_s5_lam_re, m_s5_lam_im, m_s5_log_dt, m_s5_b_re, m_s5_b_im, m_s5_c_re, m_s5_c_im, m_s5_d, m_s5_w_glu, m_lru_w_in, m_lru_conv_w, m_lru_conv_b, m_lru_w_rg, m_lru_b_rg, m_lru_w_ig, m_lru_b_ig, m_lru_lam, m_lru_w_out, m_ffn_w_gu, m_ffn_w_down, m_final_g, v_norm_g, v_w_ada, v_b_ada, v_s5_w_in, v_s5_lam_re, v_s5_lam_im, v_s5_log_dt, v_s5_b_re, v_s5_b_im, v_s5_c_re, v_s5_c_im, v_s5_d, v_s5_w_glu, v_lru_w_in, v_lru_conv_w, v_lru_conv_b, v_lru_w_rg, v_lru_b_rg, v_lru_w_ig, v_lru_b_ig, v_lru_lam, v_lru_w_out, v_ffn_w_gu, v_ffn_w_down, v_final_g)
    return _step(dict(zip(_IN_NAMES, args)))
```

```python
import functools
import math

import jax
import jax.numpy as jnp
from jax import lax
from jax.experimental import pallas as pl
from jax.experimental.pallas import tpu as pltpu

F32 = jnp.float32
BF16 = jnp.bfloat16
MESH = pl.DeviceIdType.MESH

EPS = 1e-6
LRU_C = 8.0
N_MOD = 6
ADAM_LR = 0.001
ADAM_B1 = 0.9
ADAM_B2 = 0.999
ADAM_EPS = 1e-08
ADAM_WD = 0.01
ADAM_STEP = 10

N_CHIPS = 4
N_DEV = 8
SUBLANES = 8
LANES = 128
S5_SB_GROUPS = 8
V7X_VMEM_LIMIT = 48 * 1024 * 1024
ROW_TILE_ELEMS = 512 * 1024
ROW_CALL_ELEMS = 4 * 1024 * 1024
SCAN_UNROLL = 2
LRU_TILE = 256

_TM = (1024, 1408, 512, 256, 128, 64, 32, 16, 8)
_TN = (1024, 1408, 512, 384, 256, 128)
_TK = (1024, 1408, 512, 256, 128)
_TR = (256, 128, 64, 32, 16, 8)

_GELU_K0 = math.sqrt(2.0 / math.pi)
_GELU_K1 = 0.044715


def _tile(n, cands):
    for t in cands:
        if n % t == 0:
            return t
    return n


def _cp(sem=None):
    return pltpu.CompilerParams(dimension_semantics=sem, vmem_limit_bytes=V7X_VMEM_LIMIT)


def _sds(shape, dtype):
    return jax.ShapeDtypeStruct(shape, dtype)


def _sig(x):
    return 1.0 / (1.0 + jnp.exp(-x))


def _gelu(x):
    t = jnp.tanh(_GELU_K0 * (x + _GELU_K1 * x * x * x))
    return 0.5 * x * (1.0 + t)


def _gelu_grad(x):
    x2 = x * x
    t = jnp.tanh(_GELU_K0 * (x + _GELU_K1 * x * x2))
    return 0.5 * (1.0 + t) + 0.5 * x * (1.0 - t * t) * _GELU_K0 * (1.0 + 3.0 * _GELU_K1 * x2)


def _softplus(z):
    return jnp.maximum(z, 0.0) + jnp.log(1.0 + jnp.exp(-jnp.abs(z)))


def _neg_expm1(x):
    series = -x * (1.0 + x * (0.5 + x * (1.0 / 6.0 + x * (1.0 / 24.0))))
    return jnp.where(x > -0.05, series, 1.0 - jnp.exp(x))


def _row(x, r):
    return x[r:r + 1, :]


def _colsum(x):
    return jnp.sum(x, axis=0, keepdims=True)


_NN = (((1,), (0,)), ((), ()))
_NT = (((1,), (1,)), ((), ()))
_TN_DIMS = (((0,), (0,)), ((), ()))


def _mm_call(name, a, b, a_spec, b_spec, o_spec, grid, out_shape, acc_shape, dims):
    nk = grid[-1]
    kaxis = len(grid) - 1

    def body(a_ref, b_ref, o_ref, acc_ref):
        k = pl.program_id(kaxis)

        def prod():
            return lax.dot_general(a_ref[...].astype(BF16), b_ref[...].astype(BF16), dims,
                                   preferred_element_type=F32)

        if nk == 1:
            o_ref[...] = prod().astype(o_ref.dtype)
            return

        @pl.when(k == 0)
        def _():
            acc_ref[...] = prod()

        if nk > 2:
            @pl.when(jnp.logical_and(k > 0, k < nk - 1))
            def _():
                acc_ref[...] += prod()

        @pl.when(k == nk - 1)
        def _():
            o_ref[...] = (acc_ref[...] + prod()).astype(o_ref.dtype)

    return pl.pallas_call(
        body, name=name, grid=grid, in_specs=[a_spec, b_spec], out_specs=o_spec, out_shape=out_shape,
        scratch_shapes=[pltpu.VMEM(acc_shape, F32)],
        compiler_params=_cp(("parallel", "parallel", "parallel", "arbitrary")),
    )(a, b)


def mm_nn(a, b, *, name, out_dtype=F32, bmode="plain"):
    M = a.shape[0]
    if bmode == "plain":
        G, S = 1, 1
        K, Nc = b.shape
    elif bmode == "cols":
        G = 1
        S, K, Nc = b.shape
    else:
        S = 1
        G, K, Nc = b.shape
    tm, tn, tk = _tile(M, _TM), _tile(Nc, _TN), _tile(K, _TK)
    nkb, nnb = K // tk, Nc // tn
    ncol = S * nnb
    grid = (G, M // tm, ncol, nkb)
    a_spec = pl.BlockSpec((tm, tk), lambda g, i, j, k: (i, g * nkb + k))
    if bmode == "plain":
        b_spec = pl.BlockSpec((tk, tn), lambda g, i, j, k: (k, j))
    elif bmode == "cols":
        b_spec = pl.BlockSpec((None, tk, tn), lambda g, i, j, k: (j // nnb, k, j % nnb))
    else:
        b_spec = pl.BlockSpec((None, tk, tn), lambda g, i, j, k: (g, k, j))
    o_spec = pl.BlockSpec((tm, tn), lambda g, i, j, k: (i, g * ncol + j))
    return _mm_call(name, a, b, a_spec, b_spec, o_spec, grid, _sds((M, G * S * Nc), out_dtype), (tm, tn), _NN)


def mm_nt(a, b, *, name, out_dtype=F32, bmode="plain"):
    M = a.shape[0]
    if bmode == "plain":
        G, S = 1, 1
        Ko, Nc = b.shape
    elif bmode == "cols":
        G = 1
        S, Ko, Nc = b.shape
    else:
        S = 1
        G, Ko, Nc = b.shape
    tm, to, tc = _tile(M, _TM), _tile(Ko, _TN), _tile(Nc, _TK)
    npc = Nc // tc
    nc = S * npc
    nob = Ko // to
    grid = (G, M // tm, nob, nc)
    a_spec = pl.BlockSpec((tm, tc), lambda g, i, j, n: (i, g * nc + n))
    if bmode == "plain":
        b_spec = pl.BlockSpec((to, tc), lambda g, i, j, n: (j, n))
    elif bmode == "cols":
        b_spec = pl.BlockSpec((None, to, tc), lambda g, i, j, n: (n // npc, j, n % npc))
    else:
        b_spec = pl.BlockSpec((None, to, tc), lambda g, i, j, n: (g, j, n))
    o_spec = pl.BlockSpec((tm, to), lambda g, i, j, n: (i, g * nob + j))
    return _mm_call(name, a, b, a_spec, b_spec, o_spec, grid, _sds((M, G * Ko), out_dtype), (tm, to), _NT)


def mm_tn(a, b, *, name, out_dtype=F32, omode="plain", groups=1):
    L = a.shape[0]
    G = groups if omode == "batch" else 1
    S = groups if omode == "cols" else 1
    Mo, N = a.shape[1] // G, b.shape[1] // G
    Nc = N // S
    tm, tn, tl = _tile(Mo, _TM), _tile(Nc, _TN), _tile(L, _TK)
    nmb, nnb = Mo // tm, N // tn
    npj = Nc // tn
    grid = (G, nmb, nnb, L // tl)
    a_spec = pl.BlockSpec((tl, tm), lambda g, i, j, l: (l, g * nmb + i))
    b_spec = pl.BlockSpec((tl, tn), lambda g, i, j, l: (l, g * nnb + j))
    if omode == "plain":
        o_spec = pl.BlockSpec((tm, tn), lambda g, i, j, l: (i, j))
        oshape = (Mo, N)
    elif omode == "cols":
        o_spec = pl.BlockSpec((None, tm, tn), lambda g, i, j, l: (j // npj, i, j % npj))
        oshape = (S, Mo, Nc)
    else:
        o_spec = pl.BlockSpec((None, tm, tn), lambda g, i, j, l: (g, i, j))
        oshape = (G, Mo, N)
    return _mm_call(name, a, b, a_spec, b_spec, o_spec, grid, _sds(oshape, out_dtype), (tm, tn), _TN_DIMS)


def _row_call(name, body, row_ins, vec_ins, row_outs, acc_outs=(), after=None):
    if after is not None:
        n_in = len(row_ins) + len(vec_ins)
        inner = body

        def body(*refs):
            inner(*refs[:n_in], *refs[n_in + 1:])

        return _row_call_impl(name, body, row_ins, vec_ins, row_outs, acc_outs, [after])
    return _row_call_impl(name, body, row_ins, vec_ins, row_outs, acc_outs, [])


def _row_call_impl(name, body, row_ins, vec_ins, row_outs, acc_outs, extra):
    L = row_ins[0].shape[0]
    wmax = max([a.shape[1] for a in row_ins] + [w for w, _ in row_outs])
    narr = len(row_ins) + len(row_outs)
    tr = _tile(L, tuple(t for t in _TR if t * wmax * narr <= ROW_CALL_ELEMS) or (SUBLANES,))
    in_specs = [pl.BlockSpec((tr, a.shape[1]), lambda i: (i, 0)) for a in row_ins]
    in_specs += [pl.BlockSpec(v.shape, lambda i, nd=v.ndim: (0,) * nd) for v in vec_ins]
    in_specs += [pl.BlockSpec(memory_space=pl.ANY) for _ in extra]
    out_shape = [_sds((L, w), dt) for w, dt in row_outs] + [_sds(s, dt) for s, dt in acc_outs]
    out_specs = [pl.BlockSpec((tr, w), lambda i: (i, 0)) for w, _ in row_outs]
    out_specs += [pl.BlockSpec(s, lambda i, nd=len(s): (0,) * nd) for s, _ in acc_outs]
    sem = ("arbitrary",) if acc_outs else ("parallel",)
    return pl.pallas_call(body, name=name, grid=(L // tr,), in_specs=in_specs, out_specs=out_specs,
                          out_shape=out_shape, compiler_params=_cp(sem))(*row_ins, *vec_ins, *extra)


def silu_rows(x, name, after=None):
    def body(x_ref, o_ref):
        v = x_ref[...]
        o_ref[...] = (v * _sig(v)).astype(o_ref.dtype)
    return _row_call(name, body, [x], [], [(x.shape[1], BF16)], after=after)[0]


def norm_mod_fwd(x, gain, sc, sh, name):
    def body(x_ref, g_ref, sc_ref, sh_ref, h_ref):
        v = x_ref[...]
        r = lax.rsqrt(jnp.mean(v * v, axis=-1, keepdims=True) + EPS)
        h_ref[...] = (v * r * g_ref[...] * (1.0 + sc_ref[...]) + sh_ref[...]).astype(BF16)
    return _row_call(name, body, [x], [gain, sc, sh], [(x.shape[1], BF16)])[0]


def norm_mod_bwd(dh, x, gain, sc, dres, name, after=None):
    D = x.shape[1]

    def body(dh_ref, x_ref, dres_ref, g_ref, sc_ref, dx_ref, dg_ref, dsc_ref, dsh_ref):
        @pl.when(pl.program_id(0) == 0)
        def _():
            dg_ref[...] = jnp.zeros_like(dg_ref)
            dsc_ref[...] = jnp.zeros_like(dsc_ref)
            dsh_ref[...] = jnp.zeros_like(dsh_ref)

        v = x_ref[...]
        dh_v = dh_ref[...]
        g = g_ref[...]
        r = lax.rsqrt(jnp.mean(v * v, axis=-1, keepdims=True) + EPS)
        xhat = v * r
        dn = dh_v * (1.0 + sc_ref[...])
        dsc_ref[...] += _colsum(dh_v * xhat * g)
        dsh_ref[...] += _colsum(dh_v)
        dg_ref[...] += _colsum(dn * xhat)
        t = dn * g
        dx_ref[...] = dres_ref[...] + r * (t - xhat * jnp.mean(t * xhat, axis=-1, keepdims=True))

    acc = [((1, D), F32)] * 3
    return _row_call(name, body, [dh, x, dres], [gain, sc], [(D, F32)], acc, after=after)


def final_loss(x, gain, tgt, name):
    D = x.shape[1]

    def body(x_ref, t_ref, g_ref, dx_ref, loss_ref, dg_ref, acc_ref):
        i = pl.program_id(0)

        @pl.when(i == 0)
        def _():
            dg_ref[...] = jnp.zeros_like(dg_ref)
            acc_ref[...] = jnp.zeros_like(acc_ref)

        v = x_ref[...]
        g = g_ref[...]
        r = lax.rsqrt(jnp.mean(v * v, axis=-1, keepdims=True) + EPS)
        xhat = v * r
        err = xhat * g - t_ref[...]
        acc_ref[...] += _colsum(err * err)
        dout = err * (1.0 / D)
        dg_ref[...] += _colsum(dout * xhat)
        t = dout * g
        dx_ref[...] = r * (t - xhat * jnp.mean(t * xhat, axis=-1, keepdims=True))

        @pl.when(i == pl.num_programs(0) - 1)
        def _():
            loss_ref[...] = jnp.zeros_like(loss_ref) + jnp.sum(acc_ref[...]) * (0.5 / D)

    return _row_call(name, body, [x, tgt], [gain], [(D, F32)],
                     [((SUBLANES, LANES), F32), ((1, D), F32), ((1, D), F32)])[:3]


def res_gate_fwd(x, z, g, name):
    def body(x_ref, z_ref, g_ref, o_ref):
        o_ref[...] = x_ref[...] + g_ref[...] * z_ref[...]
    return _row_call(name, body, [x, z], [g], [(x.shape[1], F32)])[0]


def res_gate_bwd(dx, z, g, name):
    D = dx.shape[1]

    def body(dx_ref, z_ref, g_ref, dz_ref, dg_ref):
        @pl.when(pl.program_id(0) == 0)
        def _():
            dg_ref[...] = jnp.zeros_like(dg_ref)
        d = dx_ref[...]
        dz_ref[...] = (g_ref[...] * d).astype(BF16)
        dg_ref[...] += _colsum(d * z_ref[...])
    return _row_call(name, body, [dx, z], [g], [(D, BF16)], [((1, D), F32)])


def glu_res_fwd(x, v, g, name):
    D = x.shape[1]

    def body(x_ref, v_ref, g_ref, o_ref):
        vv = v_ref[...]
        o_ref[...] = x_ref[...] + g_ref[...] * (vv[:, :D] * _sig(vv[:, D:]))
    return _row_call(name, body, [x, v], [g], [(D, F32)])[0]


def glu_res_bwd(dx, v, g, name):
    D = dx.shape[1]

    def body(dx_ref, v_ref, g_ref, dv_ref, dg_ref):
        @pl.when(pl.program_id(0) == 0)
        def _():
            dg_ref[...] = jnp.zeros_like(dg_ref)
        d = dx_ref[...]
        vv = v_ref[...]
        val = vv[:, :D]
        s = _sig(vv[:, D:])
        dg_ref[...] += _colsum(d * val * s)
        dm = g_ref[...] * d
        dv_ref[:, :D] = (dm * s).astype(BF16)
        dv_ref[:, D:] = (dm * val * s * (1.0 - s)).astype(BF16)
    return _row_call(name, body, [dx, v], [g], [(2 * D, BF16)], [((1, D), F32)])


def swiglu_fwd(gu, name):
    F = gu.shape[1] // 2

    def body(gu_ref, o_ref):
        v = gu_ref[...].astype(F32)
        g = v[:, :F]
        o_ref[...] = (g * _sig(g) * v[:, F:]).astype(BF16)
    return _row_call(name, body, [gu], [], [(F, BF16)])[0]


def swiglu_bwd(dact, gu, name):
    F = gu.shape[1] // 2

    def body(da_ref, gu_ref, o_ref):
        v = gu_ref[...].astype(F32)
        g, u = v[:, :F], v[:, F:]
        da = da_ref[...]
        s = _sig(g)
        o_ref[:, :F] = (da * u * s * (1.0 + g * (1.0 - s))).astype(BF16)
        o_ref[:, F:] = (da * g * s).astype(BF16)
    return _row_call(name, body, [dact, gu], [], [(2 * F, BF16)])[0]


def adamw(w, g, m, v, name):
    C = w.shape[1]
    c1 = 1.0 - ADAM_B1 ** ADAM_STEP
    c2 = 1.0 - ADAM_B2 ** ADAM_STEP

    def body(w_ref, g_ref, m_ref, v_ref, d_ref, m2_ref, v2_ref):
        gv = g_ref[...]
        m2 = ADAM_B1 * m_ref[...] + (1.0 - ADAM_B1) * gv
        v2 = ADAM_B2 * v_ref[...] + (1.0 - ADAM_B2) * (gv * gv)
        m2_ref[...] = m2
        v2_ref[...] = v2
        d_ref[...] = -ADAM_LR * ((m2 / c1) / (jnp.sqrt(v2 / c2) + ADAM_EPS) + ADAM_WD * w_ref[...])
    return _row_call(name, body, [w, g, m, v], [], [(C, F32)] * 3)


def sum_devices(parts, name):
    n, R, C = parts.shape
    min_rows = 16 if parts.dtype == BF16 else SUBLANES
    tr = _tile(R, tuple(t for t in _TR if t * C * n <= 4 * ROW_TILE_ELEMS and t >= min_rows) or (min_rows,))

    def body(p_ref, o_ref):
        acc = p_ref[0].astype(F32)
        for d in range(1, n):
            acc = acc + p_ref[d].astype(F32)
        o_ref[...] = acc
    return pl.pallas_call(body, name=name, grid=(R // tr,),
                          in_specs=[pl.BlockSpec((n, tr, C), lambda i: (0, i, 0))],
                          out_specs=pl.BlockSpec((tr, C), lambda i: (i, 0)), out_shape=_sds((R, C), F32),
                          compiler_params=_cp(("parallel",)))(parts)


def _s5_discretize(lam_re, lam_im, log_dt, b_re, b_im):
    dt = jnp.exp(log_dt)[:, None]
    mag = jnp.exp(lam_re * dt)
    ab_re = mag * jnp.cos(lam_im * dt)
    ab_im = mag * jnp.sin(lam_im * dt)
    nr, ni = ab_re - 1.0, ab_im
    den = lam_re * lam_re + lam_im * lam_im
    f_re = (nr * lam_re + ni * lam_im) / den
    f_im = (ni * lam_re - nr * lam_im) / den
    bb_re = f_re[..., None] * b_re - f_im[..., None] * b_im
    bb_im = f_re[..., None] * b_im + f_im[..., None] * b_re
    return ab_re, ab_im, bb_re, bb_im


def _s5_blockdiag(bb_re, bb_im, c_re, c_im):
    G, P, Cg = bb_re.shape
    nsb = G // S5_SB_GROUPS

    def bmat(bb):
        t = jnp.swapaxes(bb.reshape(nsb, S5_SB_GROUPS, P, Cg), 2, 3)
        return _spread_diag(t).reshape(nsb, S5_SB_GROUPS * Cg, S5_SB_GROUPS * P)

    def cmat(cc):
        t = jnp.swapaxes(cc.reshape(nsb, S5_SB_GROUPS, Cg, P), 2, 3)
        return _spread_diag(t).reshape(nsb, S5_SB_GROUPS * P, S5_SB_GROUPS * Cg)

    bsb = jnp.concatenate([bmat(bb_re), bmat(bb_im)], axis=-1)
    csb = jnp.concatenate([cmat(c_re), -cmat(c_im)], axis=1)
    return bsb, csb


def _spread_diag(t):
    ng = t.shape[1]
    eye = jnp.eye(ng, dtype=t.dtype)
    return t[:, :, :, None, :] * eye[None, :, None, :, None]


def _take_diag(t):
    return jnp.stack([t[:, g, :, g, :] for g in range(t.shape[1])], axis=1)


def _s5_blockdiag_grads(dbsb, dcsb, P, Cg):
    nsb = dbsb.shape[0]
    db6 = dbsb.reshape(nsb, S5_SB_GROUPS, Cg, 2, S5_SB_GROUPS, P)
    dbb_re = jnp.swapaxes(_take_diag(db6[:, :, :, 0]), 2, 3).reshape(-1, P, Cg)
    dbb_im = jnp.swapaxes(_take_diag(db6[:, :, :, 1]), 2, 3).reshape(-1, P, Cg)
    dc6 = dcsb.reshape(nsb, 2, S5_SB_GROUPS, P, S5_SB_GROUPS, Cg)
    dc_re = jnp.swapaxes(_take_diag(dc6[:, 0]), 2, 3).reshape(-1, Cg, P)
    dc_im = -jnp.swapaxes(_take_diag(dc6[:, 1]), 2, 3).reshape(-1, Cg, P)
    return dbb_re, dbb_im, dc_re, dc_im


def _s5_scan_consts(ab_re, ab_im):
    G, P = ab_re.shape
    nsb = G // S5_SB_GROUPS
    H = S5_SB_GROUPS * P
    ar, ai = ab_re.reshape(nsb, 1, H), ab_im.reshape(nsb, 1, H)
    pows = [(ar, ai)]
    for _ in range(SUBLANES - 1):
        pr, pi_ = pows[-1]
        pows.append((pr * ar - pi_ * ai, pr * ai + pi_ * ar))
    rows = jnp.arange(SUBLANES).reshape(1, SUBLANES, 1)

    def masked(k, keep):
        pr, pi_ = pows[k - 1]
        return jnp.where(keep, pr, 0.0), jnp.where(keep, pi_, 0.0)

    def per_row(sel):
        pr = jnp.concatenate([pows[sel(r) - 1][0] for r in range(SUBLANES)], axis=1)
        pi_ = jnp.concatenate([pows[sel(r) - 1][1] for r in range(SUBLANES)], axis=1)
        return pr, pi_

    fwd = [masked(1, rows >= 1), masked(2, rows >= 2), masked(4, rows >= 4), per_row(lambda r: r + 1)]
    rev = [masked(1, rows < 7), masked(2, rows < 6), masked(4, rows < 4), per_row(lambda r: SUBLANES - r)]

    def pack(lst, conj):
        sgn = -1.0 if conj else 1.0
        return jnp.stack([jnp.concatenate([jnp.broadcast_to(pr, (nsb, SUBLANES, H)),
                                           sgn * jnp.broadcast_to(pi_, (nsb, SUBLANES, H))], axis=-1)
                          for pr, pi_ in lst], axis=1)

    return pack(fwd, False), pack(rev, True)


def _cmadd(xr, xi, ar, ai, yr, yi):
    return xr + ar * yr - ai * yi, xi + ar * yi + ai * yr


def _s5_scan_fwd_loop(src_ref, dst_ref, sp_ref, cf_ref, cr, ci, nblk, H):
    rows = lax.broadcasted_iota(jnp.int32, (SUBLANES, H), 0)

    def body(k, carry):
        cr, ci = carry
        r0 = pl.multiple_of(k * SUBLANES, SUBLANES)
        xr = src_ref[pl.ds(r0, SUBLANES), pl.ds(0, H)]
        xi = src_ref[pl.ds(r0, SUBLANES), pl.ds(H, H)]
        for idx, d in enumerate((1, 2, 4)):
            xr, xi = _cmadd(xr, xi, cf_ref[idx, :, pl.ds(0, H)], cf_ref[idx, :, pl.ds(H, H)],
                            pltpu.roll(xr, d, 0), pltpu.roll(xi, d, 0))
        xr, xi = _cmadd(xr, xi, cf_ref[3, :, pl.ds(0, H)], cf_ref[3, :, pl.ds(H, H)], cr, ci)
        dst_ref[pl.ds(r0, SUBLANES), pl.ds(0, H)] = xr
        dst_ref[pl.ds(r0, SUBLANES), pl.ds(H, H)] = xi
        if sp_ref is not None:
            sp_ref[pl.ds(r0, SUBLANES), pl.ds(0, H)] = jnp.where(rows == 0, cr, pltpu.roll(xr, 1, 0))
            sp_ref[pl.ds(r0, SUBLANES), pl.ds(H, H)] = jnp.where(rows == 0, ci, pltpu.roll(xi, 1, 0))
        return _row(xr, SUBLANES - 1), _row(xi, SUBLANES - 1)

    return lax.fori_loop(0, nblk, body, (cr, ci))


def s5_scan_fwd(u, d_skip, bsb, csb, cf, name):
    L, W = u.shape
    nsb, GW, H2 = bsb.shape
    H = H2 // 2
    Tc = _tile(L, (512, 256, 128, 64, 32, 16, 8))
    nch = L // Tc

    def body(u_ref, d_ref, b_ref, c_ref, cf_ref, ypre_ref, yg_ref, ss_ref, bu_scr, car_scr):
        @pl.when(pl.program_id(1) == 0)
        def _():
            car_scr[...] = jnp.zeros_like(car_scr)

        ss_ref[...] = car_scr[...]
        ub = u_ref[...]
        bu_scr[...] = jnp.dot(ub.astype(BF16), b_ref[...], preferred_element_type=F32)
        cr, ci = _s5_scan_fwd_loop(bu_scr, bu_scr, None, cf_ref, car_scr[:, pl.ds(0, H)], car_scr[:, pl.ds(H, H)],
                                   Tc // SUBLANES, H)
        car_scr[:, pl.ds(0, H)] = cr
        car_scr[:, pl.ds(H, H)] = ci
        ypre = jnp.dot(bu_scr[...].astype(BF16), c_ref[...], preferred_element_type=F32) + d_ref[...] * ub
        ypre_ref[...] = ypre
        yg_ref[...] = _gelu(ypre).astype(BF16)

    return pl.pallas_call(
        body, name=name, grid=(nsb, nch),
        in_specs=[pl.BlockSpec((Tc, GW), lambda j, i: (i, j)),
                  pl.BlockSpec((1, GW), lambda j, i: (0, j)),
                  pl.BlockSpec((None, GW, H2), lambda j, i: (j, 0, 0)),
                  pl.BlockSpec((None, H2, GW), lambda j, i: (j, 0, 0)),
                  pl.BlockSpec((None, 4, SUBLANES, H2), lambda j, i: (j, 0, 0, 0))],
        out_specs=[pl.BlockSpec((Tc, GW), lambda j, i: (i, j)),
                   pl.BlockSpec((Tc, GW), lambda j, i: (i, j)),
                   pl.BlockSpec((None, None, 1, H2), lambda j, i: (i, j, 0, 0))],
        out_shape=[_sds((L, W), F32), _sds((L, W), BF16), _sds((nch, nsb, 1, H2), F32)],
        scratch_shapes=[pltpu.VMEM((Tc, H2), F32), pltpu.VMEM((1, H2), F32)],
        compiler_params=_cp(("arbitrary", "arbitrary")),
    )(u, d_skip, bsb.astype(BF16), csb.astype(BF16), cf)


def s5_scan_bwd(u, dyg, ypre, d_skip, bsb, csb, cf, crv, ss, name):
    L, W = u.shape
    nsb, GW, H2 = bsb.shape
    H = H2 // 2
    Tc = _tile(L, (512, 256, 128, 64, 32, 16, 8))
    nch = L // Tc
    nblk = Tc // SUBLANES
    bsb_t = jnp.swapaxes(bsb, 1, 2).astype(BF16)
    csb_t = jnp.swapaxes(csb, 1, 2).astype(BF16)

    def body(u_ref, dyg_ref, yp_ref, d_ref, b_ref, bt_ref, ct_ref, cf_ref, crv_ref, ss_ref,
             du_ref, db_ref, dc_ref, da_ref, dd_ref, s_scr, sp_scr, g_scr, gcar_scr):
        @pl.when(pl.program_id(1) == 0)
        def _():
            gcar_scr[...] = jnp.zeros_like(gcar_scr)
            db_ref[...] = jnp.zeros_like(db_ref)
            dc_ref[...] = jnp.zeros_like(dc_ref)
            da_ref[...] = jnp.zeros_like(da_ref)
            dd_ref[...] = jnp.zeros_like(dd_ref)

        ub = u_ref[...]
        ubf = ub.astype(BF16)
        dyp = dyg_ref[...] * _gelu_grad(yp_ref[...])
        dypb = dyp.astype(BF16)
        dd_ref[...] += _colsum(dyp * ub)
        s_scr[...] = jnp.dot(ubf, b_ref[...], preferred_element_type=F32)
        _s5_scan_fwd_loop(s_scr, s_scr, sp_scr, cf_ref, ss_ref[:, pl.ds(0, H)], ss_ref[:, pl.ds(H, H)], nblk, H)
        g_scr[...] = jnp.dot(dypb, ct_ref[...], preferred_element_type=F32)

        def rev(kk, carry):
            gr, gi, acc_r, acc_i = carry
            r0 = pl.multiple_of((nblk - 1 - kk) * SUBLANES, SUBLANES)
            xr = g_scr[pl.ds(r0, SUBLANES), pl.ds(0, H)]
            xi = g_scr[pl.ds(r0, SUBLANES), pl.ds(H, H)]
            for idx, d in enumerate((1, 2, 4)):
                xr, xi = _cmadd(xr, xi, crv_ref[idx, :, pl.ds(0, H)], crv_ref[idx, :, pl.ds(H, H)],
                                pltpu.roll(xr, SUBLANES - d, 0), pltpu.roll(xi, SUBLANES - d, 0))
            xr, xi = _cmadd(xr, xi, crv_ref[3, :, pl.ds(0, H)], crv_ref[3, :, pl.ds(H, H)], gr, gi)
            g_scr[pl.ds(r0, SUBLANES), pl.ds(0, H)] = xr
            g_scr[pl.ds(r0, SUBLANES), pl.ds(H, H)] = xi
            spr = sp_scr[pl.ds(r0, SUBLANES), pl.ds(0, H)]
            spi = sp_scr[pl.ds(r0, SUBLANES), pl.ds(H, H)]
            return (_row(xr, 0), _row(xi, 0), acc_r + xr * spr + xi * spi, acc_i + xi * spr - xr * spi)

        zero = jnp.zeros((SUBLANES, H), F32)
        gr, gi, acc_r, acc_i = lax.fori_loop(
            0, nblk, rev, (gcar_scr[:, pl.ds(0, H)], gcar_scr[:, pl.ds(H, H)], zero, zero))
        gcar_scr[:, pl.ds(0, H)] = gr
        gcar_scr[:, pl.ds(H, H)] = gi
        da_ref[:, pl.ds(0, H)] += _colsum(acc_r)
        da_ref[:, pl.ds(H, H)] += _colsum(acc_i)
        gb = g_scr[...].astype(BF16)
        db_ref[...] += lax.dot_general(ubf, gb, _TN_DIMS, preferred_element_type=F32)
        dc_ref[...] += lax.dot_general(s_scr[...].astype(BF16), dypb, _TN_DIMS, preferred_element_type=F32)
        du_ref[...] = (jnp.dot(gb, bt_ref[...], preferred_element_type=F32) + d_ref[...] * dyp).astype(BF16)

    rmap = lambda j, i: (nch - 1 - i, j)
    return pl.pallas_call(
        body, name=name, grid=(nsb, nch),
        in_specs=[pl.BlockSpec((Tc, GW), rmap), pl.BlockSpec((Tc, GW), rmap), pl.BlockSpec((Tc, GW), rmap),
                  pl.BlockSpec((1, GW), lambda j, i: (0, j)),
                  pl.BlockSpec((None, GW, H2), lambda j, i: (j, 0, 0)),
                  pl.BlockSpec((None, H2, GW), lambda j, i: (j, 0, 0)),
                  pl.BlockSpec((None, GW, H2), lambda j, i: (j, 0, 0)),
                  pl.BlockSpec((None, 4, SUBLANES, H2), lambda j, i: (j, 0, 0, 0)),
                  pl.BlockSpec((None, 4, SUBLANES, H2), lambda j, i: (j, 0, 0, 0)),
                  pl.BlockSpec((None, None, 1, H2), lambda j, i: (nch - 1 - i, j, 0, 0))],
        out_specs=[pl.BlockSpec((Tc, GW), rmap),
                   pl.BlockSpec((None, GW, H2), lambda j, i: (j, 0, 0)),
                   pl.BlockSpec((None, H2, GW), lambda j, i: (j, 0, 0)),
                   pl.BlockSpec((None, 1, H2), lambda j, i: (j, 0, 0)),
                   pl.BlockSpec((1, GW), lambda j, i: (0, j))],
        out_shape=[_sds((L, W), BF16), _sds((nsb, GW, H2), F32), _sds((nsb, H2, GW), F32),
                   _sds((nsb, 1, H2), F32), _sds((1, W), F32)],
        scratch_shapes=[pltpu.VMEM((Tc, H2), F32), pltpu.VMEM((Tc, H2), F32), pltpu.VMEM((Tc, H2), F32),
                        pltpu.VMEM((1, H2), F32)],
        compiler_params=_cp(("arbitrary", "arbitrary")),
    )(u, dyg, ypre, d_skip, bsb.astype(BF16), bsb_t, csb_t, cf, crv, ss)


def _lru_blockdiag(w_rg, w_ig):
    nb, bs, _ = w_rg.shape
    sbw = bs * LANES // math.gcd(bs, LANES)
    bps = sbw // bs
    nsb = nb // bps

    def bd(w):
        return _spread_diag(w.reshape(nsb, bps, bs, bs)).reshape(nsb, sbw, sbw)

    return bd(w_rg), bd(w_ig)


def _lru_blockdiag_grad(dwsb, nb, bs):
    nsb, sbw, _ = dwsb.shape
    bps = sbw // bs
    return _take_diag(dwsb.reshape(nsb, bps, bs, bps, bs)).reshape(nb, bs, bs)


def lru_conv_fwd(p, conv_w, conv_b, name):
    L = p.shape[0]
    E = conv_w.shape[1]
    tc = _tile(E, (256, 128))
    noff = E // tc
    kw = conv_w.shape[0]

    def body(xb_ref, w_ref, b_ref, xc_ref, xcb_ref):
        xb = xb_ref[...]
        rows = lax.broadcasted_iota(jnp.int32, xb.shape, 0)
        acc = w_ref[pl.ds(kw - 1, 1), :] * xb + b_ref[...]
        for k in range(kw - 1):
            sh = kw - 1 - k
            acc = acc + w_ref[pl.ds(k, 1), :] * jnp.where(rows >= sh, pltpu.roll(xb, sh, 0), 0.0)
        xc_ref[...] = acc
        xcb_ref[...] = acc.astype(BF16)

    return pl.pallas_call(
        body, name=name, grid=(noff,),
        in_specs=[pl.BlockSpec((L, tc), lambda t: (0, noff + t)),
                  pl.BlockSpec((kw, tc), lambda t: (0, t)), pl.BlockSpec((1, tc), lambda t: (0, t))],
        out_specs=[pl.BlockSpec((L, tc), lambda t: (0, t))] * 2,
        out_shape=[_sds((L, E), F32), _sds((L, E), BF16)],
        compiler_params=_cp(("parallel",)),
    )(p, conv_w, conv_b)


def lru_conv_bwd(d1, d2, d3, p, conv_w, name):
    L = p.shape[0]
    E = conv_w.shape[1]
    tc = _tile(E, (256, 128))
    noff = E // tc
    kw = conv_w.shape[0]

    def body(d1_ref, d2_ref, d3_ref, xb_ref, w_ref, dxb_ref, dw_ref, db_ref):
        dxc = d1_ref[...] + d2_ref[...] + d3_ref[...]
        xb = xb_ref[...]
        rows = lax.broadcasted_iota(jnp.int32, xb.shape, 0)
        db_ref[...] = _colsum(dxc)
        acc = w_ref[pl.ds(kw - 1, 1), :] * dxc
        dw_ref[pl.ds(kw - 1, 1), :] = _colsum(dxc * xb)
        for k in range(kw - 1):
            sh = kw - 1 - k
            dw_ref[pl.ds(k, 1), :] = _colsum(dxc * jnp.where(rows >= sh, pltpu.roll(xb, sh, 0), 0.0))
            acc = acc + w_ref[pl.ds(k, 1), :] * jnp.where(rows < L - sh, pltpu.roll(dxc, L - sh, 0), 0.0)
        dxb_ref[...] = acc.astype(BF16)

    return pl.pallas_call(
        body, name=name, grid=(noff,),
        in_specs=[pl.BlockSpec((L, tc), lambda t: (0, t))] * 3 +
                 [pl.BlockSpec((L, tc), lambda t: (0, noff + t)), pl.BlockSpec((kw, tc), lambda t: (0, t))],
        out_specs=[pl.BlockSpec((L, tc), lambda t: (0, t)), pl.BlockSpec((kw, tc), lambda t: (0, t)),
                   pl.BlockSpec((1, tc), lambda t: (0, t))],
        out_shape=[_sds((L, E), BF16), _sds((kw, E), F32), _sds((1, E), F32)],
        compiler_params=_cp(("parallel",)),
    )(d1, d2, d3, p, conv_w)


def _lru_gates(pr, pi_, brg, big, sp):
    r = _sig(pr + brg)
    ig = _sig(pi_ + big)
    la = -LRU_C * r * sp
    a = jnp.exp(la)
    mult = jnp.sqrt(_neg_expm1(2.0 * la))
    return r, ig, a, mult


def _lru_specs(L, E):
    tc = _tile(E, (LRU_TILE, LANES))
    col = pl.BlockSpec((L, tc), lambda t: (0, t))
    vec = pl.BlockSpec((1, tc), lambda t: (0, t))
    return tc, col, vec


def lru_scan_fwd(pre_r, pre_i, xc, p, b_rg, b_ig, lam, name):
    L, E = xc.shape
    tc, col, vec = _lru_specs(L, E)
    nblk = L // SUBLANES

    def body(pr_ref, pi_ref, xc_ref, gb_ref, brg_ref, big_ref, lam_ref, hs_ref, yv_ref):
        sp = _softplus(-lam_ref[...])
        brg, big = brg_ref[...], big_ref[...]
        rows = lax.broadcasted_iota(jnp.int32, (SUBLANES, tc), 0)

        def blk(k, carry):
            r0 = pl.multiple_of(k * SUBLANES, SUBLANES)
            sl = pl.ds(r0, SUBLANES)
            _, ig, a, mult = _lru_gates(pr_ref[sl, :], pi_ref[sl, :], brg, big, sp)
            b = mult * ig * xc_ref[sl, :]
            for d in (1, 2, 4):
                keep = rows >= d
                b = b + a * jnp.where(keep, pltpu.roll(b, d, 0), 0.0)
                a = a * jnp.where(keep, pltpu.roll(a, d, 0), 1.0)
            h = b + a * carry
            hs_ref[sl, :] = h
            return _row(h, SUBLANES - 1)

        def trip(kt, carry):
            for q in range(SCAN_UNROLL):
                carry = blk(kt * SCAN_UNROLL + q, carry)
            return carry

        lax.fori_loop(0, nblk // SCAN_UNROLL, trip, jnp.zeros((1, tc), F32))
        yv_ref[...] = (hs_ref[...] * _gelu(gb_ref[...])).astype(BF16)

    return pl.pallas_call(
        body, name=name, grid=(E // tc,),
        in_specs=[col, col, col, col, vec, vec, vec],
        out_specs=[col, col], out_shape=[_sds((L, E), F32), _sds((L, E), BF16)],
        compiler_params=_cp(("parallel",)),
    )(pre_r, pre_i, xc, p, b_rg, b_ig, lam)


def lru_scan_bwd(dyv, hs, pre_r, pre_i, xc, p, b_rg, b_ig, lam, name):
    L, E = xc.shape
    tc, col, vec = _lru_specs(L, E)
    nblk = L // SUBLANES

    def body(dyv_ref, hs_ref, pr_ref, pi_ref, xc_ref, gb_ref, brg_ref, big_ref, lam_ref,
             dgb_ref, dpr_ref, dpi_ref, dxc_ref, dbrg_ref, dbig_ref, dlam_ref, t_gb, t_pr, t_pi):
        lam_v = lam_ref[...]
        sp = _softplus(-lam_v)
        brg, big = brg_ref[...], big_ref[...]
        rows = lax.broadcasted_iota(jnp.int32, (SUBLANES, tc), 0)

        def blk(kk, carry):
            gcar, a_next, acc_sp, acc_r, acc_i = carry
            k = nblk - 1 - kk
            r0 = pl.multiple_of(k * SUBLANES, SUBLANES)
            sl = pl.ds(r0, SUBLANES)
            r, ig, a, mult = _lru_gates(pr_ref[sl, :], pi_ref[sl, :], brg, big, sp)
            gbv, hsv, dyvv, xcv = gb_ref[sl, :], hs_ref[sl, :], dyv_ref[sl, :], xc_ref[sl, :]
            t_gb[sl, :] = dyvv * hsv * _gelu_grad(gbv)
            x = dyvv * _gelu(gbv)
            al = jnp.where(rows == SUBLANES - 1, a_next, pltpu.roll(a, SUBLANES - 1, 0))
            for d in (1, 2, 4):
                keep = rows < SUBLANES - d
                x = x + al * jnp.where(keep, pltpu.roll(x, SUBLANES - d, 0), 0.0)
                al = al * jnp.where(keep, pltpu.roll(al, SUBLANES - d, 0), 1.0)
            g = x + al * gcar
            rp = pl.multiple_of(jnp.maximum(k - 1, 0) * SUBLANES, SUBLANES)
            hlast = _row(hs_ref[pl.ds(rp, SUBLANES), :], SUBLANES - 1) * (k > 0).astype(F32)
            hprev = jnp.where(rows == 0, hlast, pltpu.roll(hsv, 1, 0))
            da = g * hprev
            dmult = g * ig * xcv
            dig = g * mult * xcv
            dxc_ref[sl, :] = g * mult * ig
            dla = da * a - dmult * (a * a) / mult
            dpr = dla * (-LRU_C * sp) * r * (1.0 - r)
            dpi = dig * ig * (1.0 - ig)
            t_pr[sl, :] = dpr
            t_pi[sl, :] = dpi
            return (_row(g, 0), _row(a, 0), acc_sp + dla * (-LRU_C * r), acc_r + dpr, acc_i + dpi)

        zero = jnp.zeros((SUBLANES, tc), F32)
        z1 = jnp.zeros((1, tc), F32)
        def trip(kt, carry):
            for q in range(SCAN_UNROLL):
                carry = blk(kt * SCAN_UNROLL + q, carry)
            return carry

        _, _, acc_sp, acc_r, acc_i = lax.fori_loop(0, nblk // SCAN_UNROLL, trip, (z1, z1, zero, zero, zero))
        dgb_ref[...] = t_gb[...].astype(BF16)
        dpr_ref[...] = t_pr[...].astype(BF16)
        dpi_ref[...] = t_pi[...].astype(BF16)
        dbrg_ref[...] = _colsum(acc_r)
        dbig_ref[...] = _colsum(acc_i)
        dlam_ref[...] = -_colsum(acc_sp) * _sig(-lam_v)

    return pl.pallas_call(
        body, name=name, grid=(E // tc,),
        in_specs=[col, col, col, col, col, col, vec, vec, vec],
        out_specs=[col, col, col, col, vec, vec, vec],
        out_shape=[_sds((L, E), BF16), _sds((L, E), BF16), _sds((L, E), BF16), _sds((L, E), F32),
                   _sds((1, E), F32), _sds((1, E), F32), _sds((1, E), F32)],
        scratch_shapes=[pltpu.VMEM((L, tc), F32)] * 3,
        compiler_params=_cp(("parallel",)),
    )(dyv, hs, pre_r, pre_i, xc, p, b_rg, b_ig, lam)


def _place():
    xi, yi, ci = lax.axis_index("x"), lax.axis_index("y"), lax.axis_index("c")
    chips = [(1 - xi, yi), (xi, 1 - yi), (1 - xi, 1 - yi)]
    return xi, yi, ci, chips


_ANY = pl.BlockSpec(memory_space=pl.ANY)


def all_gather_devices(blks, name):
    n = len(blks)

    def body(*refs):
        ins, outs = refs[:n], refs[n:2 * n]
        send_sems, recv_sems, local_sems = refs[2 * n:]
        xi, yi, ci, chips = _place()
        me, sibling = (xi, yi, ci), (xi, yi, 1 - ci)

        def slab(a, px, py, pc):
            return outs[a].at[4 * px + 2 * py + pc]

        def copy(a, k, block, to, src=None):
            return pltpu.make_async_remote_copy(
                src_ref=slab(a, *block) if src is None else src, dst_ref=slab(a, *block),
                send_sem=send_sems.at[7 * a + k], recv_sem=recv_sems.at[7 * a + k], device_id=to,
                device_id_type=MESH)

        mine = [pltpu.make_async_copy(ins[a], slab(a, *me), local_sems.at[a]) for a in range(n)]
        first, passed = [], []
        for a in range(n):
            mine[a].start()
            first.append(copy(a, 0, me, sibling, src=ins[a]))
            first += [copy(a, 1 + j, me, (*chip, ci), src=ins[a]) for j, chip in enumerate(chips)]
        for cp in first:
            cp.start()
        for a in range(n):
            for j, chip in enumerate(chips):
                copy(a, 1 + j, (*chip, ci), me).wait_recv()
                passed.append(copy(a, 4 + j, (*chip, ci), sibling))
                passed[-1].start()
        for a in range(n):
            copy(a, 0, sibling, me).wait_recv()
            for j, chip in enumerate(chips):
                copy(a, 4 + j, (*chip, 1 - ci), me).wait_recv()
        for cp in first + passed:
            cp.wait_send()
        for cp in mine:
            cp.wait()

    return pl.pallas_call(
        body, name=name, in_specs=[_ANY] * n, out_specs=[_ANY] * n,
        out_shape=[_sds((N_DEV,) + b.shape, b.dtype) for b in blks],
        scratch_shapes=[pltpu.SemaphoreType.DMA((7 * n,)), pltpu.SemaphoreType.DMA((7 * n,)),
                        pltpu.SemaphoreType.DMA((n,))],
    )(*blks)


_HBM = pl.BlockSpec(memory_space=pltpu.HBM)
_SEM = pl.BlockSpec(memory_space=pltpu.SEMAPHORE)
_EFFECT = pltpu.SideEffectType.DATAFLOW_SIDE_EFFECTING


def split_start(name, groups, counts, copies_fn):
    flat = [b for g in groups for b in g]
    n, ng = len(flat), len(groups)

    def body(*refs):
        ins, sems, token = refs[:n], refs[n:n + 2 * ng], refs[-1]
        off = 0
        for gi, g in enumerate(groups):
            for cp in copies_fn(ins[off:off + len(g)], [b.shape for b in g], sems[2 * gi], sems[2 * gi + 1]):
                cp.start()
            off += len(g)
        token[...] = jnp.zeros_like(token)

    out_shape = tuple(pltpu.SemaphoreType.DMA((c,)) for c in counts for _ in range(2))
    out_shape += tuple(pltpu.HBM(b.shape, b.dtype) for b in flat) + (_sds((SUBLANES, LANES), F32),)
    outs = pl.pallas_call(
        body, name=name, in_specs=[_HBM] * n, out_shape=out_shape,
        out_specs=tuple([_SEM] * (2 * ng) + [_HBM] * n + [pl.BlockSpec(memory_space=pltpu.VMEM)]),
        input_output_aliases={i: 2 * ng + i for i in range(n)},
        compiler_params=pltpu.CompilerParams(has_side_effects=_EFFECT),
    )(*[pltpu.with_memory_space_constraint(b, pltpu.HBM) for b in flat])
    sems = [(outs[2 * gi], outs[2 * gi + 1]) for gi in range(ng)]
    thru, off = [], 2 * ng
    for g in groups:
        thru.append(list(outs[off:off + len(g)]))
        off += len(g)
    return sems, thru, outs[-1]


def split_wait(name, bufs, sems, copies_fn, after):
    n = len(bufs)
    shapes = [b.shape for b in bufs]

    def body(*refs):
        for cp in copies_fn(refs[:n], shapes, refs[n], refs[n + 1]):
            cp.wait_send()
            cp.wait_recv()

    return list(pl.pallas_call(
        body, name=name, in_specs=[_HBM] * n + [_SEM, _SEM, _ANY],
        out_shape=tuple(pltpu.HBM(b.shape, b.dtype) for b in bufs), out_specs=tuple([_HBM] * n),
        input_output_aliases={i: i for i in range(n)},
        compiler_params=pltpu.CompilerParams(has_side_effects=_EFFECT),
    )(*bufs, sems[0], sems[1], after))


def _gather_copies(bufs, shapes, send_sems, recv_sems):
    xi, yi, ci, chips = _place()
    cps = []
    for a, ref in enumerate(bufs):
        hr = shapes[a][1] // 2
        rows = ref.at[2 * xi + yi, pl.ds(pl.multiple_of(ci * hr, 16), hr), :]
        for j in range(3):
            cps.append(pltpu.make_async_remote_copy(
                src_ref=rows, dst_ref=rows, send_sem=send_sems.at[3 * a + j], recv_sem=recv_sems.at[3 * a + j],
                device_id=(*chips[j], ci), device_id_type=MESH))
    return cps


def _scatter_copies(bufs, shapes, send_sems, recv_sems):
    xi, yi, ci, chips = _place()
    n = len(bufs) // 2
    cps = []
    for a in range(n):
        for j in range(3):
            cps.append(pltpu.make_async_remote_copy(
                src_ref=bufs[a].at[2 * chips[j][0] + chips[j][1]], dst_ref=bufs[n + a].at[j],
                send_sem=send_sems.at[3 * a + j], recv_sem=recv_sems.at[3 * a + j],
                device_id=(*chips[j], ci), device_id_type=MESH))
    return cps


def cast_place(w, layer, chip1, name, after=None):
    _, R, C = w.shape
    tr = _tile(R, tuple(t for t in _TR if t * C <= ROW_TILE_ELEMS) or (16,))
    extra = [] if after is None else [after]

    def body(c_ref, w_ref, *rest):
        rest[-1][...] = w_ref[...].astype(BF16)

    return pl.pallas_call(
        body, name=name,
        grid_spec=pltpu.PrefetchScalarGridSpec(
            num_scalar_prefetch=1, grid=(R // tr,),
            in_specs=[pl.BlockSpec((None, tr, C), lambda i, c: (layer, i, 0))] + [_ANY] * len(extra),
            out_specs=pl.BlockSpec((None, tr, C), lambda i, c: (c[0], i, 0))),
        out_shape=_sds((N_CHIPS, R, C), BF16), compiler_params=_cp(("parallel",)),
    )(chip1, w, *extra)


def forward_halves(bufs, name):
    n = len(bufs)

    def body(*refs):
        ins, outs = refs[:n], refs[n:2 * n]
        send_sems, recv_sems = refs[2 * n:]
        xi, yi, ci, chips = _place()

        def copy(ref, a, j, h):
            hr = bufs[a].shape[1] // 2
            rows = pl.ds(pl.multiple_of(h * hr, 16), hr)
            slot = 2 * chips[j][0] + chips[j][1]
            return pltpu.make_async_remote_copy(
                src_ref=ref[a].at[slot, rows, :], dst_ref=outs[a].at[slot, rows, :],
                send_sem=send_sems.at[3 * a + j], recv_sem=recv_sems.at[3 * a + j],
                device_id=(xi, yi, 1 - ci), device_id_type=MESH)

        sends = [copy(ins, a, j, ci) for a in range(n) for j in range(3)]
        for cp in sends:
            cp.start()
        for a in range(n):
            for j in range(3):
                copy(outs, a, j, 1 - ci).wait_recv()
        for cp in sends:
            cp.wait_send()

    return pl.pallas_call(
        body, name=name, in_specs=[_ANY] * n, out_specs=[_ANY] * n,
        out_shape=[_sds(b.shape, b.dtype) for b in bufs], input_output_aliases={i: i for i in range(n)},
        scratch_shapes=[pltpu.SemaphoreType.DMA((3 * n,)), pltpu.SemaphoreType.DMA((3 * n,))],
    )(*bufs)


def exchange_halves(grads, name):
    n = len(grads)

    def body(*refs):
        ins, outs = refs[:n], refs[n:2 * n]
        send_sems, recv_sems = refs[2 * n:]
        xi, yi, ci, _ = _place()
        cps = []
        for a in range(n):
            hr = grads[a].shape[1] // 2
            src = ins[a].at[:, pl.ds(pl.multiple_of((1 - ci) * hr, 16), hr), :]
            cps.append(pltpu.make_async_remote_copy(
                src_ref=src, dst_ref=outs[a], send_sem=send_sems.at[a], recv_sem=recv_sems.at[a],
                device_id=(xi, yi, 1 - ci), device_id_type=MESH))
            cps[-1].start()
        for cp in cps:
            cp.wait()

    return pl.pallas_call(
        body, name=name, in_specs=[_ANY] * n, out_specs=[_ANY] * n,
        out_shape=[_sds((N_CHIPS, g.shape[1] // 2, g.shape[2]), g.dtype) for g in grads],
        scratch_shapes=[pltpu.SemaphoreType.DMA((n,)), pltpu.SemaphoreType.DMA((n,))],
    )(*grads)


def add_half(g, got, ci, name):
    S, hr, C = got.shape
    tr = _tile(hr, tuple(t for t in _TR if t * C <= ROW_TILE_ELEMS) or (16,))
    nb = hr // tr

    def body(c_ref, g_ref, r_ref, o_ref):
        o_ref[...] = (g_ref[...].astype(F32) + r_ref[...].astype(F32)).astype(BF16)

    return pl.pallas_call(
        body, name=name,
        grid_spec=pltpu.PrefetchScalarGridSpec(
            num_scalar_prefetch=1, grid=(S, nb),
            in_specs=[pl.BlockSpec((None, tr, C), lambda s, i, c: (s, c[0] * nb + i, 0)),
                      pl.BlockSpec((None, tr, C), lambda s, i, c: (s, i, 0))],
            out_specs=pl.BlockSpec((None, tr, C), lambda s, i, c: (s, i, 0))),
        out_shape=_sds((S, hr, C), BF16), compiler_params=_cp(("parallel", "parallel")),
    )(ci, g, got)


def add_chips(part, got, place, name):
    S, hr, C = part.shape
    tr = _tile(hr, tuple(t for t in _TR if t * C <= ROW_TILE_ELEMS) or (16,))
    nb = hr // tr

    def body(c_ref, p_ref, r_ref, o_ref):
        acc = p_ref[...].astype(F32)
        for j in range(3):
            acc = acc + r_ref[j].astype(F32)
        o_ref[...] = acc

    return pl.pallas_call(
        body, name=name,
        grid_spec=pltpu.PrefetchScalarGridSpec(
            num_scalar_prefetch=1, grid=(nb,),
            in_specs=[pl.BlockSpec((None, tr, C), lambda i, c: (c[0], i, 0)),
                      pl.BlockSpec((3, tr, C), lambda i, c: (0, i, 0))],
            out_specs=pl.BlockSpec((tr, C), lambda i, c: (c[1] * nb + i, 0))),
        out_shape=_sds((2 * hr, C), F32), compiler_params=_cp(("parallel",)),
    )(place, part, got)


def join_halves(bufs, name):
    n = len(bufs)

    def body(*refs):
        ins, outs = refs[:n], refs[n:2 * n]
        send_sems, recv_sems = refs[2 * n:]
        xi, yi, ci, _ = _place()

        def copy(ref, a, h):
            hr = bufs[a].shape[0] // 2
            rows = pl.ds(pl.multiple_of(h * hr, 8), hr)
            return pltpu.make_async_remote_copy(
                src_ref=ref[a].at[rows, :], dst_ref=outs[a].at[rows, :], send_sem=send_sems.at[a],
                recv_sem=recv_sems.at[a], device_id=(xi, yi, 1 - ci), device_id_type=MESH)

        sends = [copy(ins, a, ci) for a in range(n)]
        for cp in sends:
            cp.start()
        for a in range(n):
            copy(outs, a, 1 - ci).wait_recv()
        for cp in sends:
            cp.wait_send()

    return pl.pallas_call(
        body, name=name, in_specs=[_ANY] * n, out_specs=[_ANY] * n,
        out_shape=[_sds(b.shape, b.dtype) for b in bufs], input_output_aliases={i: i for i in range(n)},
        scratch_shapes=[pltpu.SemaphoreType.DMA((n,)), pltpu.SemaphoreType.DMA((n,))],
    )(*bufs)


def reduce_scatter_start(grads, ci1, tag):
    got = exchange_halves(grads, f"rs_halves_{tag}")
    parts = [add_half(g, r, ci1, f"rs_add_half_{tag}{a}") for a, (g, r) in enumerate(zip(grads, got))]
    lands = [lax.empty((3,) + q.shape[1:], q.dtype) for q in parts]
    sems, thru, token = split_start(f"rs_ici_start_{tag}", [parts + lands], [3 * len(parts)], _scatter_copies)
    return (sems[0], thru[0]), token


def reduce_scatter_finish(state, place, after, tag):
    sems, bufs = state
    bufs = split_wait(f"rs_ici_wait_{tag}", bufs, sems, _scatter_copies, after)
    n = len(bufs) // 2
    return [add_chips(bufs[a], bufs[n + a], place, f"rs_add_chips_{tag}{a}") for a in range(n)]


def _pack(parts, width):
    flat = jnp.concatenate([p.reshape(-1).astype(F32) for p in parts])
    per = SUBLANES * width
    total = -(-flat.shape[0] // per) * per
    flat = jnp.pad(flat, (0, total - flat.shape[0]))
    return flat.reshape(total // width, width)


def _unpack(flat, shapes):
    out, off = [], 0
    for s in shapes:
        n = math.prod(s)
        out.append(flat[off:off + n].reshape(s))
        off += n
    return out


_W_NAMES = ['norm_g', 'w_ada', 'b_ada', 's5_w_in', 's5_lam_re', 's5_lam_im', 's5_log_dt', 's5_b_re', 's5_b_im',
            's5_c_re', 's5_c_im', 's5_d', 's5_w_glu', 'lru_w_in', 'lru_conv_w', 'lru_conv_b', 'lru_w_rg',
            'lru_b_rg', 'lru_w_ig', 'lru_b_ig', 'lru_lam', 'lru_w_out', 'ffn_w_gu', 'ffn_w_down', 'final_g']
_BIG = ('w_ada', 's5_w_in', 's5_w_glu', 'lru_w_in', 'lru_w_out', 'ffn_w_gu', 'ffn_w_down')
_MID = ('s5_b_re', 's5_b_im', 's5_c_re', 's5_c_im', 'lru_w_rg', 'lru_w_ig')


def _ffn_fwd(x, h, w_gu, w_down, gate, tag):
    gu = mm_nn(h, w_gu, name=f"{tag}_gu", out_dtype=BF16, bmode="cols")
    act = swiglu_fwd(gu, f"{tag}_act")
    z = mm_nn(act, w_down, name=f"{tag}_down")
    return res_gate_fwd(x, z, gate, f"{tag}_res"), (gu, act, z)


def _ffn_bwd(dx, h, saved, w_gu, w_down, gate, tag):
    gu, act, z = saved
    dz, dgate = res_gate_bwd(dx, z, gate, f"{tag}_res_bwd")
    dact = mm_nt(dz, w_down, name=f"{tag}_dact")
    dw_down = mm_tn(act, dz, name=f"{tag}_dwdown", out_dtype=BF16)
    dgu = swiglu_bwd(dact, gu, f"{tag}_act_bwd")
    dh = mm_nt(dgu, w_gu, name=f"{tag}_dh", bmode="cols")
    dw_gu = mm_tn(h, dgu, name=f"{tag}_dwgu", out_dtype=BF16, omode="cols", groups=N_CHIPS)
    return dh, dgate, dw_gu, dw_down.reshape((N_CHIPS, -1) + dw_down.shape[1:])


def _step(p):
    xi, yi, ci = lax.axis_index("x"), lax.axis_index("y"), lax.axis_index("c")
    chip = 2 * xi + yi
    me = 2 * chip + ci
    ci1 = jnp.reshape(ci, (1,)).astype(jnp.int32)
    chip1 = jnp.reshape(chip, (1,)).astype(jnp.int32)
    place2 = jnp.stack([chip, ci]).astype(jnp.int32)

    x0 = p['x'][0]
    tgt = p['loss_target'][0]
    L, D = x0.shape
    Dq = D // N_CHIPS
    depth = p['w_ada'].shape[0]
    E = p['lru_lam'].shape[1] * N_CHIPS
    Eq = E // N_CHIPS
    kw = p['lru_conv_w'].shape[1]
    Nq = p['w_ada'].shape[2]
    _, G, P, Cg = p['s5_b_re'].shape
    nb, bs = p['lru_w_rg'].shape[1], p['lru_w_rg'].shape[2]

    pay = _pack([p['c'], p['norm_g'], p['lru_conv_w'], p['lru_conv_b'], p['lru_b_rg'], p['lru_b_ig'],
                 p['lru_lam']], 1024)
    g1 = all_gather_devices([pay], "gather_small_params")[0].reshape(N_DEV, -1)
    c_all = g1[:, :D]
    per_chip = g1[0::2]
    sizes = [(depth, 2, Dq), (kw, Eq), (Eq,), (Eq,), (Eq,), (Eq,)]
    offs = D
    pieces = []
    for s in sizes:
        nel = math.prod(s)
        pieces.append(per_chip[:, offs:offs + nel].reshape((N_CHIPS,) + s))
        offs += nel
    norm_g = jnp.moveaxis(pieces[0], 0, 2).reshape(depth, 2, D)
    conv_w = jnp.moveaxis(pieces[1], 0, 1).reshape(kw, E)
    conv_b, b_rg, b_ig, lam = [q.reshape(1, E) for q in pieces[2:]]

    cond = silu_rows(jnp.pad(c_all, ((0, 16 - N_DEV), (0, 0))), "cond_silu")
    cond_rep = jnp.concatenate([cond] * depth, axis=1)
    mod_part = mm_nn(cond_rep, p['w_ada'], name="mod_proj", bmode="batch")[:N_DEV]
    g2 = all_gather_devices([mod_part], "gather_mod")[0][0::2]
    mine = lax.dynamic_index_in_dim(g2, me, axis=1, keepdims=False).reshape(N_CHIPS, depth, Nq)
    mod = jnp.moveaxis(mine, 0, 1).reshape(depth, N_CHIPS * Nq) + p['b_ada']
    mods = [[mod[i:i + 1, k * D:(k + 1) * D] for k in range(N_MOD)] for i in range(depth)]

    def place(key, layer, tag, after=None):
        return cast_place(p[key], layer, chip1, f"place_{tag}", after=after)

    first = [place('s5_w_in', 0, "s5_in", after=mod), place('s5_w_glu', 0, "s5_glu")]
    sems_a, bufs_a, tok_a = split_start("gather_ici_start_s5", [first], [3 * len(first)], _gather_copies)
    others = [[place('ffn_w_gu', 0, "gu0", after=tok_a), place('ffn_w_down', 0, "down0")],
              [place('lru_w_in', 0, "lru_in"), place('lru_w_out', 0, "lru_out")],
              [place('ffn_w_gu', 1, "gu1"), place('ffn_w_down', 1, "down1")]]
    sems_b, bufs_b, tok_b = split_start("gather_ici_start_rest", others, [3 * len(g) for g in others], _gather_copies)
    wsems, wbufs = sems_a + sems_b, bufs_a + bufs_b

    def weights(gi, after, tag):
        landed = split_wait(f"gather_ici_wait_{tag}", wbufs[gi], wsems[gi], _gather_copies, after)
        return forward_halves(landed, f"gather_forward_{tag}")

    s5_small = (p['s5_lam_re'][0], p['s5_lam_im'][0], p['s5_log_dt'][0], p['s5_b_re'][0], p['s5_b_im'][0])
    (ab_re, ab_im, bb_re, bb_im), s5_disc_vjp = jax.vjp(_s5_discretize, *s5_small)
    bsb, csb = _s5_blockdiag(bb_re, bb_im, p['s5_c_re'][0], p['s5_c_im'][0])
    cf, crv = _s5_scan_consts(ab_re, ab_im)
    wsb_rg, wsb_ig = [w.astype(BF16) for w in _lru_blockdiag(p['lru_w_rg'][0], p['lru_w_ig'][0])]
    nsb_lru = wsb_rg.shape[0]

    sh1, sc1, gt1, sh2, sc2, gt2 = mods[0]
    s5_w_in, s5_w_glu = weights(0, tok_b, "s5")
    s5_w_in = s5_w_in.reshape(-1, D)
    h0 = norm_mod_fwd(x0, norm_g[0, 0:1], sc1, sh1, "l0_norm1")
    u = mm_nn(h0, s5_w_in, name="s5_in")
    ypre, yg, ss = s5_scan_fwd(u, p['s5_d'], bsb, csb, cf, "s5_scan")
    v = mm_nn(yg, s5_w_glu, name="s5_glu", bmode="cols")
    w_gu0, w_down0 = weights(1, v, "ffn0")
    w_down0 = w_down0.reshape(-1, D)
    x1 = glu_res_fwd(x0, v, gt1, "s5_res")
    h1 = norm_mod_fwd(x1, norm_g[0, 1:2], sc2, sh2, "l0_norm2")
    x2, ffn0 = _ffn_fwd(x1, h1, w_gu0, w_down0, gt2, "ffn0")

    sh1b, sc1b, gt1b, sh2b, sc2b, gt2b = mods[1]
    lru_w_in, lru_w_out = weights(2, ffn0[0], "lru")
    lru_w_out = lru_w_out.reshape(-1, D)
    h2 = norm_mod_fwd(x2, norm_g[1, 0:1], sc1b, sh1b, "l1_norm1")
    pq = mm_nn(h2, lru_w_in, name="lru_in", bmode="cols")
    xc, xcb = lru_conv_fwd(pq, conv_w, conv_b, "lru_conv")
    pre_r = mm_nn(xcb, wsb_rg, name="lru_gate_r", bmode="batch")
    pre_i = mm_nn(xcb, wsb_ig, name="lru_gate_i", bmode="batch")
    hs, yv = lru_scan_fwd(pre_r, pre_i, xc, pq, b_rg, b_ig, lam, "lru_scan")
    w_gu1, w_down1 = weights(3, hs, "ffn1")
    w_down1 = w_down1.reshape(-1, D)
    mix = mm_nn(yv, lru_w_out, name="lru_out")
    x3 = res_gate_fwd(x2, mix, gt1b, "lru_res")
    h3 = norm_mod_fwd(x3, norm_g[1, 1:2], sc2b, sh2b, "l1_norm2")
    x4, ffn1 = _ffn_fwd(x3, h3, w_gu1, w_down1, gt2b, "ffn1")

    fg = p['final_g'].reshape(1, D)
    dx4, loss_blk, dfinal_g = final_loss(x4, fg, tgt, "final_loss")
    loss = lax.psum(loss_blk[0, 0], ("x", "y", "c"))

    def rows4(g):
        return g.reshape((N_CHIPS, -1) + g.shape[1:])

    dh3, dgt2b, dw_gu1, dw_down1 = _ffn_bwd(dx4, h3, ffn1, w_gu1, w_down1, gt2b, "ffn1")
    rs_ffn1, tok = reduce_scatter_start([dw_gu1, dw_down1], ci1, "ffn1")
    dx3, dgn11, dsc2b, dsh2b = norm_mod_bwd(dh3, x3, norm_g[1, 1:2], sc2b, dx4, "l1_norm2_bwd", after=tok)

    dmix, dgt1b = res_gate_bwd(dx3, mix, gt1b, "lru_res_bwd")
    dyv = mm_nt(dmix, lru_w_out, name="lru_dyv")
    dw_out = mm_tn(yv, dmix, name="lru_dwout", out_dtype=BF16)
    dgb, dpre_r, dpre_i, dxc1, db_rg, db_ig, dlam = lru_scan_bwd(dyv, hs, pre_r, pre_i, xc, pq, b_rg, b_ig, lam,
                                                                "lru_scan_bwd")
    dxc2 = mm_nt(dpre_r, wsb_rg, name="lru_dxc_r", bmode="batch")
    dxc3 = mm_nt(dpre_i, wsb_ig, name="lru_dxc_i", bmode="batch")
    dwsb_rg = mm_tn(xcb, dpre_r, name="lru_dwgate_r", omode="batch", groups=nsb_lru)
    dwsb_ig = mm_tn(xcb, dpre_i, name="lru_dwgate_i", omode="batch", groups=nsb_lru)
    dxb, dconv_w, dconv_b = lru_conv_bwd(dxc1, dxc2, dxc3, pq, conv_w, "lru_conv_bwd")
    dpq = jnp.concatenate([dgb, dxb], axis=1)
    dh2 = mm_nt(dpq, lru_w_in, name="lru_dh", bmode="cols")
    dw_lru_in = mm_tn(h2, dpq, name="lru_dwin", out_dtype=BF16, omode="cols", groups=N_CHIPS)
    rs_lru, tok = reduce_scatter_start([dw_lru_in, rows4(dw_out)], ci1, "lru")
    dx2, dgn10, dsc1b, dsh1b = norm_mod_bwd(dh2, x2, norm_g[1, 0:1], sc1b, dx3, "l1_norm1_bwd", after=tok)

    dh1, dgt2, dw_gu0, dw_down0 = _ffn_bwd(dx2, h1, ffn0, w_gu0, w_down0, gt2, "ffn0")
    rs_ffn0, tok = reduce_scatter_start([dw_gu0, dw_down0], ci1, "ffn0")
    dx1, dgn01, dsc2, dsh2 = norm_mod_bwd(dh1, x1, norm_g[0, 1:2], sc2, dx2, "l0_norm2_bwd", after=tok)

    dv, dgt1 = glu_res_bwd(dx1, v, gt1, "s5_res_bwd")
    dyg = mm_nt(dv, s5_w_glu, name="s5_dyg", bmode="cols")
    dw_glu = mm_tn(yg, dv, name="s5_dwglu", out_dtype=BF16, omode="cols", groups=N_CHIPS)
    du, dbsb, dcsb, da, dd = s5_scan_bwd(u, dyg, ypre, p['s5_d'], bsb, csb, cf, crv, ss, "s5_scan_bwd")
    dh0 = mm_nt(du, s5_w_in, name="s5_dh")
    dw_s5_in = mm_tn(h0, du, name="s5_dwin", out_dtype=BF16)
    rs_s5, tok = reduce_scatter_start([rows4(dw_s5_in), dw_glu], ci1, "s5")
    grad_x, dgn00, dsc1, dsh1 = norm_mod_bwd(dh0, x0, norm_g[0, 0:1], sc1, dx1, "l0_norm1_bwd", after=tok)

    dmod = jnp.concatenate([jnp.concatenate([dsh1, dsc1, dgt1, dsh2, dsc2, dgt2], axis=1),
                            jnp.concatenate([dsh1b, dsc1b, dgt1b, dsh2b, dsc2b, dgt2b], axis=1)], axis=0)
    dnorm_g = jnp.stack([jnp.concatenate([dgn00, dgn01]), jnp.concatenate([dgn10, dgn11])])
    dbb_re, dbb_im, dc_re, dc_im = _s5_blockdiag_grads(dbsb, dcsb, P, Cg)
    H = S5_SB_GROUPS * P
    da_re, da_im = da[:, 0, :H].reshape(G, P), da[:, 0, H:].reshape(G, P)
    dw_rg, dw_ig = _lru_blockdiag_grad(dwsb_rg, nb, bs), _lru_blockdiag_grad(dwsb_ig, nb, bs)
    small = [dmod, dnorm_g, da_re, da_im, dd, dconv_w, dconv_b, db_rg, db_ig, dlam, dfinal_g]
    small_shapes = [s.shape for s in small]
    payload = _pack(small, 1024)
    mid = [dbb_re, dbb_im, dc_re, dc_im, dw_rg, dw_ig]
    mid_shapes = [s.shape for s in mid]
    gathered = all_gather_devices([payload] + [s.reshape(s.shape[0], -1).astype(BF16) for s in mid],
                                  "gather_small_grads")
    gathered_small = gathered[0]
    total = sum_devices(gathered_small, "sum_small_grads").reshape(-1)
    (s_dmod, s_norm_g, s_da_re, s_da_im, s_dd, s_conv_w, s_conv_b, s_b_rg, s_b_ig, s_lam,
     s_final_g) = _unpack(total, small_shapes)
    s_dbb_re, s_dbb_im, s_dc_re, s_dc_im, s_dw_rg, s_dw_ig = [
        sum_devices(g, f"sum_mid_grads_{i}").reshape(s) for i, (g, s) in enumerate(zip(gathered[1:], mid_shapes))]
    g_lam_re, g_lam_im, g_log_dt, g_b_re, g_b_im = s5_disc_vjp((s_da_re, s_da_im, s_dbb_re, s_dbb_im))

    npay = payload.shape[0] * payload.shape[1]
    dmod_all = gathered_small.reshape(N_DEV, npay)[:, :depth * N_MOD * D].reshape(N_DEV, depth, N_CHIPS, Nq)
    dmod_mine = lax.dynamic_index_in_dim(dmod_all, chip, axis=2, keepdims=False).reshape(N_DEV, depth * Nq)
    dmod_mine = jnp.pad(dmod_mine, ((0, 16 - N_DEV), (0, 0)))
    g_w_ada = mm_tn(cond_rep, dmod_mine, name="w_ada_grad", omode="batch", groups=depth)

    def cols(full, width):
        return lax.dynamic_slice_in_dim(full, chip * width, width, axis=full.ndim - 1)

    grads = {
        'norm_g': cols(s_norm_g, Dq), 'w_ada': g_w_ada, 'b_ada': s_dmod,
        's5_lam_re': g_lam_re, 's5_lam_im': g_lam_im, 's5_log_dt': g_log_dt, 's5_b_re': g_b_re, 's5_b_im': g_b_im,
        's5_c_re': s_dc_re, 's5_c_im': s_dc_im, 's5_d': s_dd, 'lru_conv_w': cols(s_conv_w, Eq),
        'lru_conv_b': cols(s_conv_b, Eq), 'lru_w_rg': s_dw_rg, 'lru_b_rg': cols(s_b_rg, Eq),
        'lru_w_ig': s_dw_ig, 'lru_b_ig': cols(s_b_ig, Eq), 'lru_lam': cols(s_lam, Eq), 'final_g': s_final_g,
    }
    grads = {k: g.reshape(p[k].shape) for k, g in grads.items()}

    delta, new_m, new_v = {}, {}, {}

    def adamw_2d(k, rows):
        w2 = p[k].reshape(rows, -1)
        outs = adamw(w2, grads[k].reshape(w2.shape), p['m_' + k].reshape(w2.shape), p['v_' + k].reshape(w2.shape),
                     f"adamw_{k}")
        delta[k], new_m[k], new_v[k] = [o.reshape(p[k].shape) for o in outs]

    adamw_2d('w_ada', depth * D)
    for k in _MID:
        adamw_2d(k, p[k].shape[1])
    rest = [k for k in _W_NAMES if k not in _BIG + _MID]
    shapes = [p[k].shape for k in rest]
    packed = [_pack([src[pre_ + k] if pre_ else src[k] for k in rest], 1024)
              for src, pre_ in ((p, ''), (grads, ''), (p, 'm_'), (p, 'v_'))]
    outs = adamw(*packed, "adamw_small")
    for dst, o in zip((delta, new_m, new_v), outs):
        for k, val in zip(rest, _unpack(o.reshape(-1), shapes)):
            dst[k] = val

    done = delta['w_ada']
    halves = []
    for state, tag in ((rs_ffn1, "ffn1"), (rs_lru, "lru"), (rs_ffn0, "ffn0"), (rs_s5, "s5")):
        halves += reduce_scatter_finish(state, place2, done, tag)
    g_gu1, g_down1, g_lru_in, g_lru_out, g_gu0, g_down0, g_s5_in, g_s5_glu = join_halves(halves, "rs_join_halves")
    grads.update({'s5_w_in': g_s5_in[None], 's5_w_glu': g_s5_glu[None], 'lru_w_in': g_lru_in[None],
                  'lru_w_out': g_lru_out[None], 'ffn_w_gu': jnp.stack([g_gu0, g_gu1]),
                  'ffn_w_down': jnp.stack([g_down0, g_down1])})
    for k in _BIG[1:]:
        adamw_2d(k, math.prod(p[k].shape[:-1]))

    return (loss, grad_x[None], *[grads[k] for k in _W_NAMES], *[delta[k] for k in _W_NAMES],
            *[new_m[k] for k in _W_NAMES], *[new_v[k] for k in _W_NAMES])


_IN_NAMES = (['x', 'c'] + _W_NAMES + ['loss_target'] + ['m_' + k for k in _W_NAMES] + ['v_' + k for k in _W_NAMES])


def kernel(x, c, norm_g, w_ada, b_ada, s5_w_in, s5_lam_re, s5_lam_im, s5_log_dt, s5_b_re, s5_b_im, s5_c_re, s5_c_im, s5_d, s5_w_glu, lru_w_in, lru_conv_w, lru_conv_b, lru_w_rg, lru_b_rg, lru_w_ig, lru_b_ig, lru_lam, lru_w_out, ffn_w_gu, ffn_w_down, final_g, loss_target, m_norm_g, m_w_ada, m_b_ada, m_s5_w_in, m_s5_lam_re, m_s5_lam_im, m_s5_log_dt, m_s5_b_re, m_s5_b_im, m_s5_c_re, m_s5_c_im, m_s5_d, m_s5_w_glu, m_lru_w_in, m_lru_conv_w, m_lru_conv_b, m_lru_w_rg, m_lru_b_rg, m_lru_w_ig, m_lru_b_ig, m_lru_lam, m_lru_w_out, m_ffn_w_gu, m_ffn_w_down, m_final_g, v_norm_g, v_w_ada, v_b_ada, v_s5_w_in, v_s5_lam_re, v_s5_lam_im, v_s5_log_dt, v_s5_b_re, v_s5_b_im, v_s5_c_re, v_s5_c_im, v_s5_d, v_s5_w_glu, v_lru_w_in, v_lru_conv_w, v_lru_conv_b, v_lru_w_rg, v_lru_b_rg, v_lru_w_ig, v_lru_b_ig, v_lru_lam, v_lru_w_out, v_ffn_w_gu, v_ffn_w_down, v_final_g):
    args = (x, c, norm_g, w_ada, b_ada, s5_w_in, s5_lam_re, s5_lam_im, s5_log_dt, s5_b_re, s5_b_im, s5_c_re, s5_c_im, s5_d, s5_w_glu, lru_w_in, lru_conv_w, lru_conv_b, lru_w_rg, lru_b_rg, lru_w_ig, lru_b_ig, lru_lam, lru_w_out, ffn_w_gu, ffn_w_down, final_g, loss_target, m_norm_g, m_w_ada, m_b_ada, m_s5_w_in, m_s5_lam_re, m_s5_lam_im, m_s5_log_dt, m_s5_b_re, m_s5_b_im, m_s5_c_re, m_s5_c_im, m_s5_d, m_s5_w_glu, m_lru_w_in, m_lru_conv_w, m_lru_conv_b, m_lru_w_rg, m_lru_b_rg, m_lru_w_ig, m_lru_b_ig, m_lru_lam, m_lru_w_out, m_ffn_w_gu, m_ffn_w_down, m_final_g, v_norm_g, v_w_ada, v_b_ada, v_s5_w_in, v_s5_lam_re, v_s5_lam_im, v_s5_log_dt, v_s5_b_re, v_s5_b_im, v_s5_c_re, v_s5_c_im, v_s5_d, v_s5_w_glu, v_lru_w_in, v_lru_conv_w, v_lru_conv_b, v_lru_w_rg, v_lru_b_rg, v_lru_w_ig, v_lru_b_ig, v_lru_lam, v_lru_w_out, v_ffn_w_gu, v_ffn_w_down, v_final_g)
    return _step(dict(zip(_IN_NAMES, args)))
```

```python
import functools
import math

import jax
import jax.numpy as jnp
from jax import lax
from jax.experimental import pallas as pl
from jax.experimental.pallas import tpu as pltpu

F32 = jnp.float32
BF16 = jnp.bfloat16
MESH = pl.DeviceIdType.MESH

EPS = 1e-6
LRU_C = 8.0
N_MOD = 6
ADAM_LR = 0.001
ADAM_B1 = 0.9
ADAM_B2 = 0.999
ADAM_EPS = 1e-08
ADAM_WD = 0.01
ADAM_STEP = 10

N_CHIPS = 4
N_DEV = 8
SUBLANES = 8
LANES = 128
S5_SB_GROUPS = 8
V7X_VMEM_LIMIT = 48 * 1024 * 1024
ROW_TILE_ELEMS = 512 * 1024
ROW_CALL_ELEMS = 4 * 1024 * 1024
SCAN_UNROLL = 2
LRU_TILE = 256

_TM = (1024, 1408, 512, 256, 128, 64, 32, 16, 8)
_TN = (1024, 1408, 512, 384, 256, 128)
_TK = (1024, 1408, 512, 256, 128)
_TR = (256, 128, 64, 32, 16, 8)

_GELU_K0 = math.sqrt(2.0 / math.pi)
_GELU_K1 = 0.044715


def _tile(n, cands):
    for t in cands:
        if n % t == 0:
            return t
    return n


def _cp(sem=None):
    return pltpu.CompilerParams(dimension_semantics=sem, vmem_limit_bytes=V7X_VMEM_LIMIT)


def _sds(shape, dtype):
    return jax.ShapeDtypeStruct(shape, dtype)


def _sig(x):
    return 1.0 / (1.0 + jnp.exp(-x))


def _gelu(x):
    t = jnp.tanh(_GELU_K0 * (x + _GELU_K1 * x * x * x))
    return 0.5 * x * (1.0 + t)


def _gelu_grad(x):
    x2 = x * x
    t = jnp.tanh(_GELU_K0 * (x + _GELU_K1 * x * x2))
    return 0.5 * (1.0 + t) + 0.5 * x * (1.0 - t * t) * _GELU_K0 * (1.0 + 3.0 * _GELU_K1 * x2)


def _softplus(z):
    return jnp.maximum(z, 0.0) + jnp.log(1.0 + jnp.exp(-jnp.abs(z)))


def _neg_expm1(x):
    series = -x * (1.0 + x * (0.5 + x * (1.0 / 6.0 + x * (1.0 / 24.0))))
    return jnp.where(x > -0.05, series, 1.0 - jnp.exp(x))


def _row(x, r):
    return x[r:r + 1, :]


def _colsum(x):
    return jnp.sum(x, axis=0, keepdims=True)


_NN = (((1,), (0,)), ((), ()))
_NT = (((1,), (1,)), ((), ()))
_TN_DIMS = (((0,), (0,)), ((), ()))


def _mm_call(name, a, b, a_spec, b_spec, o_spec, grid, out_shape, acc_shape, dims):
    nk = grid[-1]
    kaxis = len(grid) - 1

    def body(a_ref, b_ref, o_ref, acc_ref):
        k = pl.program_id(kaxis)

        def prod():
            return lax.dot_general(a_ref[...].astype(BF16), b_ref[...].astype(BF16), dims,
                                   preferred_element_type=F32)

        if nk == 1:
            o_ref[...] = prod().astype(o_ref.dtype)
            return

        @pl.when(k == 0)
        def _():
            acc_ref[...] = prod()

        if nk > 2:
            @pl.when(jnp.logical_and(k > 0, k < nk - 1))
            def _():
                acc_ref[...] += prod()

        @pl.when(k == nk - 1)
        def _():
            o_ref[...] = (acc_ref[...] + prod()).astype(o_ref.dtype)

    return pl.pallas_call(
        body, name=name, grid=grid, in_specs=[a_spec, b_spec], out_specs=o_spec, out_shape=out_shape,
        scratch_shapes=[pltpu.VMEM(acc_shape, F32)],
        compiler_params=_cp(("parallel", "parallel", "parallel", "arbitrary")),
    )(a, b)


def mm_nn(a, b, *, name, out_dtype=F32, bmode="plain"):
    M = a.shape[0]
    if bmode == "plain":
        G, S = 1, 1
        K, Nc = b.shape
    elif bmode == "cols":
        G = 1
        S, K, Nc = b.shape
    else:
        S = 1
        G, K, Nc = b.shape
    tm, tn, tk = _tile(M, _TM), _tile(Nc, _TN), _tile(K, _TK)
    nkb, nnb = K // tk, Nc // tn
    ncol = S * nnb
    grid = (G, M // tm, ncol, nkb)
    a_spec = pl.BlockSpec((tm, tk), lambda g, i, j, k: (i, g * nkb + k))
    if bmode == "plain":
        b_spec = pl.BlockSpec((tk, tn), lambda g, i, j, k: (k, j))
    elif bmode == "cols":
        b_spec = pl.BlockSpec((None, tk, tn), lambda g, i, j, k: (j // nnb, k, j % nnb))
    else:
        b_spec = pl.BlockSpec((None, tk, tn), lambda g, i, j, k: (g, k, j))
    o_spec = pl.BlockSpec((tm, tn), lambda g, i, j, k: (i, g * ncol + j))
    return _mm_call(name, a, b, a_spec, b_spec, o_spec, grid, _sds((M, G * S * Nc), out_dtype), (tm, tn), _NN)


def mm_nt(a, b, *, name, out_dtype=F32, bmode="plain"):
    M = a.shape[0]
    if bmode == "plain":
        G, S = 1, 1
        Ko, Nc = b.shape
    elif bmode == "cols":
        G = 1
        S, Ko, Nc = b.shape
    else:
        S = 1
        G, Ko, Nc = b.shape
    tm, to, tc = _tile(M, _TM), _tile(Ko, _TN), _tile(Nc, _TK)
    npc = Nc // tc
    nc = S * npc
    nob = Ko // to
    grid = (G, M // tm, nob, nc)
    a_spec = pl.BlockSpec((tm, tc), lambda g, i, j, n: (i, g * nc + n))
    if bmode == "plain":
        b_spec = pl.BlockSpec((to, tc), lambda g, i, j, n: (j, n))
    elif bmode == "cols":
        b_spec = pl.BlockSpec((None, to, tc), lambda g, i, j, n: (n // npc, j, n % npc))
    else:
        b_spec = pl.BlockSpec((None, to, tc), lambda g, i, j, n: (g, j, n))
    o_spec = pl.BlockSpec((tm, to), lambda g, i, j, n: (i, g * nob + j))
    return _mm_call(name, a, b, a_spec, b_spec, o_spec, grid, _sds((M, G * Ko), out_dtype), (tm, to), _NT)


def mm_tn(a, b, *, name, out_dtype=F32, omode="plain", groups=1):
    L = a.shape[0]
    G = groups if omode == "batch" else 1
    S = groups if omode == "cols" else 1
    Mo, N = a.shape[1] // G, b.shape[1] // G
    Nc = N // S
    tm, tn, tl = _tile(Mo, _TM), _tile(Nc, _TN), _tile(L, _TK)
    nmb, nnb = Mo // tm, N // tn
    npj = Nc // tn
    grid = (G, nmb, nnb, L // tl)
    a_spec = pl.BlockSpec((tl, tm), lambda g, i, j, l: (l, g * nmb + i))
    b_spec = pl.BlockSpec((tl, tn), lambda g, i, j, l: (l, g * nnb + j))
    if omode == "plain":
        o_spec = pl.BlockSpec((tm, tn), lambda g, i, j, l: (i, j))
        oshape = (Mo, N)
    elif omode == "cols":
        o_spec = pl.BlockSpec((None, tm, tn), lambda g, i, j, l: (j // npj, i, j % npj))
        oshape = (S, Mo, Nc)
    else:
        o_spec = pl.BlockSpec((None, tm, tn), lambda g, i, j, l: (g, i, j))
        oshape = (G, Mo, N)
    return _mm_call(name, a, b, a_spec, b_spec, o_spec, grid, _sds(oshape, out_dtype), (tm, tn), _TN_DIMS)


def _row_call(name, body, row_ins, vec_ins, row_outs, acc_outs=(), after=None):
    if after is not None:
        n_in = len(row_ins) + len(vec_ins)
        inner = body

        def body(*refs):
            inner(*refs[:n_in], *refs[n_in + 1:])

        return _row_call_impl(name, body, row_ins, vec_ins, row_outs, acc_outs, [after])
    return _row_call_impl(name, body, row_ins, vec_ins, row_outs, acc_outs, [])


def _row_call_impl(name, body, row_ins, vec_ins, row_outs, acc_outs, extra):
    L = row_ins[0].shape[0]
    wmax = max([a.shape[1] for a in row_ins] + [w for w, _ in row_outs])
    narr = len(row_ins) + len(row_outs)
    tr = _tile(L, tuple(t for t in _TR if t * wmax * narr <= ROW_CALL_ELEMS) or (SUBLANES,))
    in_specs = [pl.BlockSpec((tr, a.shape[1]), lambda i: (i, 0)) for a in row_ins]
    in_specs += [pl.BlockSpec(v.shape, lambda i, nd=v.ndim: (0,) * nd) for v in vec_ins]
    in_specs += [pl.BlockSpec(memory_space=pl.ANY) for _ in extra]
    out_shape = [_sds((L, w), dt) for w, dt in row_outs] + [_sds(s, dt) for s, dt in acc_outs]
    out_specs = [pl.BlockSpec((tr, w), lambda i: (i, 0)) for w, _ in row_outs]
    out_specs += [pl.BlockSpec(s, lambda i, nd=len(s): (0,) * nd) for s, _ in acc_outs]
    sem = ("arbitrary",) if acc_outs else ("parallel",)
    return pl.pallas_call(body, name=name, grid=(L // tr,), in_specs=in_specs, out_specs=out_specs,
                          out_shape=out_shape, compiler_params=_cp(sem))(*row_ins, *vec_ins, *extra)


def silu_rows(x, name, after=None):
    def body(x_ref, o_ref):
        v = x_ref[...]
        o_ref[...] = (v * _sig(v)).astype(o_ref.dtype)
    return _row_call(name, body, [x], [], [(x.shape[1], BF16)], after=after)[0]


def norm_mod_fwd(x, gain, sc, sh, name):
    def body(x_ref, g_ref, sc_ref, sh_ref, h_ref):
        v = x_ref[...]
        r = lax.rsqrt(jnp.mean(v * v, axis=-1, keepdims=True) + EPS)
        h_ref[...] = (v * r * g_ref[...] * (1.0 + sc_ref[...]) + sh_ref[...]).astype(BF16)
    return _row_call(name, body, [x], [gain, sc, sh], [(x.shape[1], BF16)])[0]


def norm_mod_bwd(dh, x, gain, sc, dres, name, after=None):
    D = x.shape[1]

    def body(dh_ref, x_ref, dres_ref, g_ref, sc_ref, dx_ref, dg_ref, dsc_ref, dsh_ref):
        @pl.when(pl.program_id(0) == 0)
        def _():
            dg_ref[...] = jnp.zeros_like(dg_ref)
            dsc_ref[...] = jnp.zeros_like(dsc_ref)
            dsh_ref[...] = jnp.zeros_like(dsh_ref)

        v = x_ref[...]
        dh_v = dh_ref[...]
        g = g_ref[...]
        r = lax.rsqrt(jnp.mean(v * v, axis=-1, keepdims=True) + EPS)
        xhat = v * r
        dn = dh_v * (1.0 + sc_ref[...])
        dsc_ref[...] += _colsum(dh_v * xhat * g)
        dsh_ref[...] += _colsum(dh_v)
        dg_ref[...] += _colsum(dn * xhat)
        t = dn * g
        dx_ref[...] = dres_ref[...] + r * (t - xhat * jnp.mean(t * xhat, axis=-1, keepdims=True))

    acc = [((1, D), F32)] * 3
    return _row_call(name, body, [dh, x, dres], [gain, sc], [(D, F32)], acc, after=after)


def final_loss(x, gain, tgt, name):
    D = x.shape[1]

    def body(x_ref, t_ref, g_ref, dx_ref, loss_ref, dg_ref, acc_ref):
        i = pl.program_id(0)

        @pl.when(i == 0)
        def _():
            dg_ref[...] = jnp.zeros_like(dg_ref)
            acc_ref[...] = jnp.zeros_like(acc_ref)

        v = x_ref[...]
        g = g_ref[...]
        r = lax.rsqrt(jnp.mean(v * v, axis=-1, keepdims=True) + EPS)
        xhat = v * r
        err = xhat * g - t_ref[...]
        acc_ref[...] += _colsum(err * err)
        dout = err * (1.0 / D)
        dg_ref[...] += _colsum(dout * xhat)
        t = dout * g
        dx_ref[...] = r * (t - xhat * jnp.mean(t * xhat, axis=-1, keepdims=True))

        @pl.when(i == pl.num_programs(0) - 1)
        def _():
            loss_ref[...] = jnp.zeros_like(loss_ref) + jnp.sum(acc_ref[...]) * (0.5 / D)

    return _row_call(name, body, [x, tgt], [gain], [(D, F32)],
                     [((SUBLANES, LANES), F32), ((1, D), F32), ((1, D), F32)])[:3]


def res_gate_fwd(x, z, g, name):
    def body(x_ref, z_ref, g_ref, o_ref):
        o_ref[...] = x_ref[...] + g_ref[...] * z_ref[...]
    return _row_call(name, body, [x, z], [g], [(x.shape[1], F32)])[0]


def res_gate_bwd(dx, z, g, name):
    D = dx.shape[1]

    def body(dx_ref, z_ref, g_ref, dz_ref, dg_ref):
        @pl.when(pl.program_id(0) == 0)
        def _():
            dg_ref[...] = jnp.zeros_like(dg_ref)
        d = dx_ref[...]
        dz_ref[...] = (g_ref[...] * d).astype(BF16)
        dg_ref[...] += _colsum(d * z_ref[...])
    return _row_call(name, body, [dx, z], [g], [(D, BF16)], [((1, D), F32)])


def glu_res_fwd(x, v, g, name):
    D = x.shape[1]

    def body(x_ref, v_ref, g_ref, o_ref):
        vv = v_ref[...]
        o_ref[...] = x_ref[...] + g_ref[...] * (vv[:, :D] * _sig(vv[:, D:]))
    return _row_call(name, body, [x, v], [g], [(D, F32)])[0]


def glu_res_bwd(dx, v, g, name):
    D = dx.shape[1]

    def body(dx_ref, v_ref, g_ref, dv_ref, dg_ref):
        @pl.when(pl.program_id(0) == 0)
        def _():
            dg_ref[...] = jnp.zeros_like(dg_ref)
        d = dx_ref[...]
        vv = v_ref[...]
        val = vv[:, :D]
        s = _sig(vv[:, D:])
        dg_ref[...] += _colsum(d * val * s)
        dm = g_ref[...] * d
        dv_ref[:, :D] = (dm * s).astype(BF16)
        dv_ref[:, D:] = (dm * val * s * (1.0 - s)).astype(BF16)
    return _row_call(name, body, [dx, v], [g], [(2 * D, BF16)], [((1, D), F32)])


def swiglu_fwd(gu, name):
    F = gu.shape[1] // 2

    def body(gu_ref, o_ref):
        v = gu_ref[...].astype(F32)
        g = v[:, :F]
        o_ref[...] = (g * _sig(g) * v[:, F:]).astype(BF16)
    return _row_call(name, body, [gu], [], [(F, BF16)])[0]


def swiglu_bwd(dact, gu, name):
    F = gu.shape[1] // 2

    def body(da_ref, gu_ref, o_ref):
        v = gu_ref[...].astype(F32)
        g, u = v[:, :F], v[:, F:]
        da = da_ref[...]
        s = _sig(g)
        o_ref[:, :F] = (da * u * s * (1.0 + g * (1.0 - s))).astype(BF16)
        o_ref[:, F:] = (da * g * s).astype(BF16)
    return _row_call(name, body, [dact, gu], [], [(2 * F, BF16)])[0]


def adamw(w, g, m, v, name, after=None):
    C = w.shape[1]
    c1 = 1.0 - ADAM_B1 ** ADAM_STEP
    c2 = 1.0 - ADAM_B2 ** ADAM_STEP

    def body(w_ref, g_ref, m_ref, v_ref, d_ref, m2_ref, v2_ref):
        gv = g_ref[...]
        m2 = ADAM_B1 * m_ref[...] + (1.0 - ADAM_B1) * gv
        v2 = ADAM_B2 * v_ref[...] + (1.0 - ADAM_B2) * (gv * gv)
        m2_ref[...] = m2
        v2_ref[...] = v2
        d_ref[...] = -ADAM_LR * ((m2 / c1) / (jnp.sqrt(v2 / c2) + ADAM_EPS) + ADAM_WD * w_ref[...])
    return _row_call(name, body, [w, g, m, v], [], [(C, F32)] * 3, after=after)


def sum_devices(parts, name):
    n, R, C = parts.shape
    min_rows = 16 if parts.dtype == BF16 else SUBLANES
    tr = _tile(R, tuple(t for t in _TR if t * C * n <= 4 * ROW_TILE_ELEMS and t >= min_rows) or (min_rows,))

    def body(p_ref, o_ref):
        acc = p_ref[0].astype(F32)
        for d in range(1, n):
            acc = acc + p_ref[d].astype(F32)
        o_ref[...] = acc
    return pl.pallas_call(body, name=name, grid=(R // tr,),
                          in_specs=[pl.BlockSpec((n, tr, C), lambda i: (0, i, 0))],
                          out_specs=pl.BlockSpec((tr, C), lambda i: (i, 0)), out_shape=_sds((R, C), F32),
                          compiler_params=_cp(("parallel",)))(parts)


def _s5_discretize(lam_re, lam_im, log_dt, b_re, b_im):
    dt = jnp.exp(log_dt)[:, None]
    mag = jnp.exp(lam_re * dt)
    ab_re = mag * jnp.cos(lam_im * dt)
    ab_im = mag * jnp.sin(lam_im * dt)
    nr, ni = ab_re - 1.0, ab_im
    den = lam_re * lam_re + lam_im * lam_im
    f_re = (nr * lam_re + ni * lam_im) / den
    f_im = (ni * lam_re - nr * lam_im) / den
    bb_re = f_re[..., None] * b_re - f_im[..., None] * b_im
    bb_im = f_re[..., None] * b_im + f_im[..., None] * b_re
    return ab_re, ab_im, bb_re, bb_im


def _s5_blockdiag(bb_re, bb_im, c_re, c_im):
    G, P, Cg = bb_re.shape
    nsb = G // S5_SB_GROUPS

    def bmat(bb):
        return _block_diag(jnp.swapaxes(bb.reshape(nsb, S5_SB_GROUPS, P, Cg), 2, 3))

    def cmat(cc):
        return _block_diag(jnp.swapaxes(cc.reshape(nsb, S5_SB_GROUPS, Cg, P), 2, 3))

    bsb = jnp.concatenate([bmat(bb_re), bmat(bb_im)], axis=-1)
    csb = jnp.concatenate([cmat(c_re), -cmat(c_im)], axis=1)
    return bsb, csb


def _block_diag(t):
    ng, b = t.shape[1], t.shape[3]
    rows = [jnp.pad(t[:, g], ((0, 0), (0, 0), (g * b, (ng - 1 - g) * b))) for g in range(ng)]
    return jnp.concatenate(rows, axis=1)


def _diag_blocks(m, ng):
    a, b = m.shape[1] // ng, m.shape[2] // ng
    return jnp.stack([m[:, g * a:(g + 1) * a, g * b:(g + 1) * b] for g in range(ng)], axis=1)


def _s5_blockdiag_grads(dbsb, dcsb, P, Cg):
    H = S5_SB_GROUPS * P

    def blocks(m):
        return jnp.swapaxes(_diag_blocks(m, S5_SB_GROUPS), 2, 3)

    dbb_re = blocks(dbsb[:, :, :H]).reshape(-1, P, Cg)
    dbb_im = blocks(dbsb[:, :, H:]).reshape(-1, P, Cg)
    dc_re = blocks(dcsb[:, :H, :]).reshape(-1, Cg, P)
    dc_im = -blocks(dcsb[:, H:, :]).reshape(-1, Cg, P)
    return dbb_re, dbb_im, dc_re, dc_im


def _s5_scan_consts(ab_re, ab_im):
    G, P = ab_re.shape
    nsb = G // S5_SB_GROUPS
    H = S5_SB_GROUPS * P
    ar, ai = ab_re.reshape(nsb, 1, H), ab_im.reshape(nsb, 1, H)
    pows = [(ar, ai)]
    for _ in range(SUBLANES - 1):
        pr, pi_ = pows[-1]
        pows.append((pr * ar - pi_ * ai, pr * ai + pi_ * ar))
    rows = jnp.arange(SUBLANES).reshape(1, SUBLANES, 1)

    def masked(k, keep):
        pr, pi_ = pows[k - 1]
        return jnp.where(keep, pr, 0.0), jnp.where(keep, pi_, 0.0)

    def per_row(sel):
        pr = jnp.concatenate([pows[sel(r) - 1][0] for r in range(SUBLANES)], axis=1)
        pi_ = jnp.concatenate([pows[sel(r) - 1][1] for r in range(SUBLANES)], axis=1)
        return pr, pi_

    fwd = [masked(1, rows >= 1), masked(2, rows >= 2), masked(4, rows >= 4), per_row(lambda r: r + 1)]
    rev = [masked(1, rows < 7), masked(2, rows < 6), masked(4, rows < 4), per_row(lambda r: SUBLANES - r)]

    def pack(lst, conj):
        sgn = -1.0 if conj else 1.0
        return jnp.stack([jnp.concatenate([jnp.broadcast_to(pr, (nsb, SUBLANES, H)),
                                           sgn * jnp.broadcast_to(pi_, (nsb, SUBLANES, H))], axis=-1)
                          for pr, pi_ in lst], axis=1)

    return pack(fwd, False), pack(rev, True)


def _cmadd(xr, xi, ar, ai, yr, yi):
    return xr + ar * yr - ai * yi, xi + ar * yi + ai * yr


def _s5_scan_fwd_loop(src_ref, dst_ref, cf_ref, cr, ci, nblk, H):
    def body(k, carry):
        cr, ci = carry
        r0 = pl.multiple_of(k * SUBLANES, SUBLANES)
        xr = src_ref[pl.ds(r0, SUBLANES), pl.ds(0, H)]
        xi = src_ref[pl.ds(r0, SUBLANES), pl.ds(H, H)]
        for idx, d in enumerate((1, 2, 4)):
            xr, xi = _cmadd(xr, xi, cf_ref[idx, :, pl.ds(0, H)], cf_ref[idx, :, pl.ds(H, H)],
                            pltpu.roll(xr, d, 0), pltpu.roll(xi, d, 0))
        xr, xi = _cmadd(xr, xi, cf_ref[3, :, pl.ds(0, H)], cf_ref[3, :, pl.ds(H, H)], cr, ci)
        dst_ref[pl.ds(r0, SUBLANES), pl.ds(0, H)] = xr
        dst_ref[pl.ds(r0, SUBLANES), pl.ds(H, H)] = xi
        return _row(xr, SUBLANES - 1), _row(xi, SUBLANES - 1)

    return lax.fori_loop(0, nblk, body, (cr, ci))


def s5_scan_fwd(u, d_skip, bsb, csb, cf, name):
    L, W = u.shape
    nsb, GW, H2 = bsb.shape
    H = H2 // 2
    Tc = _tile(L, (512, 256, 128, 64, 32, 16, 8))
    nch = L // Tc

    def body(u_ref, d_ref, b_ref, c_ref, cf_ref, ypre_ref, yg_ref, s_ref, ss_ref, bu_scr, car_scr):
        @pl.when(pl.program_id(1) == 0)
        def _():
            car_scr[...] = jnp.zeros_like(car_scr)

        ss_ref[...] = car_scr[...]
        ub = u_ref[...]
        bu_scr[...] = jnp.dot(ub.astype(BF16), b_ref[...], preferred_element_type=F32)
        cr, ci = _s5_scan_fwd_loop(bu_scr, s_ref, cf_ref, car_scr[:, pl.ds(0, H)], car_scr[:, pl.ds(H, H)],
                                   Tc // SUBLANES, H)
        car_scr[:, pl.ds(0, H)] = cr
        car_scr[:, pl.ds(H, H)] = ci
        ypre = jnp.dot(s_ref[...].astype(BF16), c_ref[...], preferred_element_type=F32) + d_ref[...] * ub
        ypre_ref[...] = ypre
        yg_ref[...] = _gelu(ypre).astype(BF16)

    return pl.pallas_call(
        body, name=name, grid=(nsb, nch),
        in_specs=[pl.BlockSpec((Tc, GW), lambda j, i: (i, j)),
                  pl.BlockSpec((1, GW), lambda j, i: (0, j)),
                  pl.BlockSpec((None, GW, H2), lambda j, i: (j, 0, 0)),
                  pl.BlockSpec((None, H2, GW), lambda j, i: (j, 0, 0)),
                  pl.BlockSpec((None, 4, SUBLANES, H2), lambda j, i: (j, 0, 0, 0))],
        out_specs=[pl.BlockSpec((Tc, GW), lambda j, i: (i, j)),
                   pl.BlockSpec((Tc, GW), lambda j, i: (i, j)),
                   pl.BlockSpec((Tc, H2), lambda j, i: (i, j)),
                   pl.BlockSpec((None, None, 1, H2), lambda j, i: (i, j, 0, 0))],
        out_shape=[_sds((L, W), F32), _sds((L, W), BF16), _sds((L, nsb * H2), F32),
                   _sds((nch, nsb, 1, H2), F32)],
        scratch_shapes=[pltpu.VMEM((Tc, H2), F32), pltpu.VMEM((1, H2), F32)],
        compiler_params=_cp(("arbitrary", "arbitrary")),
    )(u, d_skip, bsb.astype(BF16), csb.astype(BF16), cf)


def s5_scan_bwd(u, dyg, ypre, s_all, d_skip, bsb, csb, crv, ss, name):
    L, W = u.shape
    nsb, GW, H2 = bsb.shape
    H = H2 // 2
    Tc = _tile(L, (512, 256, 128, 64, 32, 16, 8))
    nch = L // Tc
    nblk = Tc // SUBLANES
    bsb_t = jnp.swapaxes(bsb, 1, 2).astype(BF16)
    csb_t = jnp.swapaxes(csb, 1, 2).astype(BF16)

    def body(u_ref, dyg_ref, yp_ref, s_ref, d_ref, bt_ref, ct_ref, crv_ref, ss_ref,
             du_ref, db_ref, dc_ref, da_ref, dd_ref, g_scr, gcar_scr):
        @pl.when(pl.program_id(1) == 0)
        def _():
            gcar_scr[...] = jnp.zeros_like(gcar_scr)
            db_ref[...] = jnp.zeros_like(db_ref)
            dc_ref[...] = jnp.zeros_like(dc_ref)
            da_ref[...] = jnp.zeros_like(da_ref)
            dd_ref[...] = jnp.zeros_like(dd_ref)

        ub = u_ref[...]
        ubf = ub.astype(BF16)
        dyp = dyg_ref[...] * _gelu_grad(yp_ref[...])
        dypb = dyp.astype(BF16)
        dd_ref[...] += _colsum(dyp * ub)
        g_scr[...] = jnp.dot(dypb, ct_ref[...], preferred_element_type=F32)
        rows = lax.broadcasted_iota(jnp.int32, (SUBLANES, H), 0)

        def rev(kk, carry):
            gr, gi, acc_r, acc_i = carry
            k = nblk - 1 - kk
            r0 = pl.multiple_of(k * SUBLANES, SUBLANES)
            xr = g_scr[pl.ds(r0, SUBLANES), pl.ds(0, H)]
            xi = g_scr[pl.ds(r0, SUBLANES), pl.ds(H, H)]
            for idx, d in enumerate((1, 2, 4)):
                xr, xi = _cmadd(xr, xi, crv_ref[idx, :, pl.ds(0, H)], crv_ref[idx, :, pl.ds(H, H)],
                                pltpu.roll(xr, SUBLANES - d, 0), pltpu.roll(xi, SUBLANES - d, 0))
            xr, xi = _cmadd(xr, xi, crv_ref[3, :, pl.ds(0, H)], crv_ref[3, :, pl.ds(H, H)], gr, gi)
            g_scr[pl.ds(r0, SUBLANES), pl.ds(0, H)] = xr
            g_scr[pl.ds(r0, SUBLANES), pl.ds(H, H)] = xi
            rp = pl.multiple_of(jnp.maximum(k - 1, 0) * SUBLANES, SUBLANES)
            first = k == 0
            pr = jnp.where(first, ss_ref[:, pl.ds(0, H)], _row(s_ref[pl.ds(rp, SUBLANES), pl.ds(0, H)], SUBLANES - 1))
            pi_ = jnp.where(first, ss_ref[:, pl.ds(H, H)], _row(s_ref[pl.ds(rp, SUBLANES), pl.ds(H, H)], SUBLANES - 1))
            spr = jnp.where(rows == 0, pr, pltpu.roll(s_ref[pl.ds(r0, SUBLANES), pl.ds(0, H)], 1, 0))
            spi = jnp.where(rows == 0, pi_, pltpu.roll(s_ref[pl.ds(r0, SUBLANES), pl.ds(H, H)], 1, 0))
            return (_row(xr, 0), _row(xi, 0), acc_r + xr * spr + xi * spi, acc_i + xi * spr - xr * spi)

        zero = jnp.zeros((SUBLANES, H), F32)
        gr, gi, acc_r, acc_i = lax.fori_loop(
            0, nblk, rev, (gcar_scr[:, pl.ds(0, H)], gcar_scr[:, pl.ds(H, H)], zero, zero))
        gcar_scr[:, pl.ds(0, H)] = gr
        gcar_scr[:, pl.ds(H, H)] = gi
        da_ref[:, pl.ds(0, H)] += _colsum(acc_r)
        da_ref[:, pl.ds(H, H)] += _colsum(acc_i)
        gb = g_scr[...].astype(BF16)
        db_ref[...] += lax.dot_general(ubf, gb, _TN_DIMS, preferred_element_type=F32)
        dc_ref[...] += lax.dot_general(s_ref[...].astype(BF16), dypb, _TN_DIMS, preferred_element_type=F32)
        du_ref[...] = (jnp.dot(gb, bt_ref[...], preferred_element_type=F32) + d_ref[...] * dyp).astype(BF16)

    rmap = lambda j, i: (nch - 1 - i, j)
    return pl.pallas_call(
        body, name=name, grid=(nsb, nch),
        in_specs=[pl.BlockSpec((Tc, GW), rmap), pl.BlockSpec((Tc, GW), rmap), pl.BlockSpec((Tc, GW), rmap),
                  pl.BlockSpec((Tc, H2), rmap),
                  pl.BlockSpec((1, GW), lambda j, i: (0, j)),
                  pl.BlockSpec((None, H2, GW), lambda j, i: (j, 0, 0)),
                  pl.BlockSpec((None, GW, H2), lambda j, i: (j, 0, 0)),
                  pl.BlockSpec((None, 4, SUBLANES, H2), lambda j, i: (j, 0, 0, 0)),
                  pl.BlockSpec((None, None, 1, H2), lambda j, i: (nch - 1 - i, j, 0, 0))],
        out_specs=[pl.BlockSpec((Tc, GW), rmap),
                   pl.BlockSpec((None, GW, H2), lambda j, i: (j, 0, 0)),
                   pl.BlockSpec((None, H2, GW), lambda j, i: (j, 0, 0)),
                   pl.BlockSpec((None, 1, H2), lambda j, i: (j, 0, 0)),
                   pl.BlockSpec((1, GW), lambda j, i: (0, j))],
        out_shape=[_sds((L, W), BF16), _sds((nsb, GW, H2), F32), _sds((nsb, H2, GW), F32),
                   _sds((nsb, 1, H2), F32), _sds((1, W), F32)],
        scratch_shapes=[pltpu.VMEM((Tc, H2), F32), pltpu.VMEM((1, H2), F32)],
        compiler_params=_cp(("arbitrary", "arbitrary")),
    )(u, dyg, ypre, s_all, d_skip, bsb_t, csb_t, crv, ss)


def _lru_blockdiag(w_rg, w_ig):
    nb, bs, _ = w_rg.shape
    sbw = bs * LANES // math.gcd(bs, LANES)
    bps = sbw // bs
    nsb = nb // bps

    def bd(w):
        return _block_diag(w.astype(BF16).reshape(nsb, bps, bs, bs))

    return bd(w_rg), bd(w_ig)


def _lru_blockdiag_grad(dwsb, nb, bs):
    return _diag_blocks(dwsb, dwsb.shape[1] // bs).reshape(nb, bs, bs)


def lru_conv_fwd(p, conv_w, conv_b, name):
    L = p.shape[0]
    E = conv_w.shape[1]
    tc = _tile(E, (256, 128))
    noff = E // tc
    kw = conv_w.shape[0]

    def body(xb_ref, w_ref, b_ref, xc_ref, xcb_ref):
        xb = xb_ref[...]
        rows = lax.broadcasted_iota(jnp.int32, xb.shape, 0)
        acc = w_ref[pl.ds(kw - 1, 1), :] * xb + b_ref[...]
        for k in range(kw - 1):
            sh = kw - 1 - k
            acc = acc + w_ref[pl.ds(k, 1), :] * jnp.where(rows >= sh, pltpu.roll(xb, sh, 0), 0.0)
        xc_ref[...] = acc
        xcb_ref[...] = acc.astype(BF16)

    return pl.pallas_call(
        body, name=name, grid=(noff,),
        in_specs=[pl.BlockSpec((L, tc), lambda t: (0, noff + t)),
                  pl.BlockSpec((kw, tc), lambda t: (0, t)), pl.BlockSpec((1, tc), lambda t: (0, t))],
        out_specs=[pl.BlockSpec((L, tc), lambda t: (0, t))] * 2,
        out_shape=[_sds((L, E), F32), _sds((L, E), BF16)],
        compiler_params=_cp(("parallel",)),
    )(p, conv_w, conv_b)


def lru_conv_bwd(d1, d2, d3, p, conv_w, name):
    L = p.shape[0]
    E = conv_w.shape[1]
    tc = _tile(E, (256, 128))
    noff = E // tc
    kw = conv_w.shape[0]

    def body(d1_ref, d2_ref, d3_ref, xb_ref, w_ref, dxb_ref, dw_ref, db_ref):
        dxc = d1_ref[...] + d2_ref[...] + d3_ref[...]
        xb = xb_ref[...]
        rows = lax.broadcasted_iota(jnp.int32, xb.shape, 0)
        db_ref[...] = _colsum(dxc)
        acc = w_ref[pl.ds(kw - 1, 1), :] * dxc
        dw_ref[pl.ds(kw - 1, 1), :] = _colsum(dxc * xb)
        for k in range(kw - 1):
            sh = kw - 1 - k
            dw_ref[pl.ds(k, 1), :] = _colsum(dxc * jnp.where(rows >= sh, pltpu.roll(xb, sh, 0), 0.0))
            acc = acc + w_ref[pl.ds(k, 1), :] * jnp.where(rows < L - sh, pltpu.roll(dxc, L - sh, 0), 0.0)
        dxb_ref[...] = acc.astype(BF16)

    return pl.pallas_call(
        body, name=name, grid=(noff,),
        in_specs=[pl.BlockSpec((L, tc), lambda t: (0, t))] * 3 +
                 [pl.BlockSpec((L, tc), lambda t: (0, noff + t)), pl.BlockSpec((kw, tc), lambda t: (0, t))],
        out_specs=[pl.BlockSpec((L, tc), lambda t: (0, t)), pl.BlockSpec((kw, tc), lambda t: (0, t)),
                   pl.BlockSpec((1, tc), lambda t: (0, t))],
        out_shape=[_sds((L, E), BF16), _sds((kw, E), F32), _sds((1, E), F32)],
        compiler_params=_cp(("parallel",)),
    )(d1, d2, d3, p, conv_w)


def _lru_gates(pr, pi_, brg, big, sp):
    r = _sig(pr + brg)
    ig = _sig(pi_ + big)
    la = -LRU_C * r * sp
    a = jnp.exp(la)
    mult = jnp.sqrt(_neg_expm1(2.0 * la))
    return r, ig, a, mult


def _lru_specs(L, E):
    tc = _tile(E, (LRU_TILE, LANES))
    col = pl.BlockSpec((L, tc), lambda t: (0, t))
    vec = pl.BlockSpec((1, tc), lambda t: (0, t))
    return tc, col, vec


def lru_scan_fwd(pre_r, pre_i, xc, p, b_rg, b_ig, lam, name):
    L, E = xc.shape
    tc, col, vec = _lru_specs(L, E)
    nblk = L // SUBLANES

    def body(pr_ref, pi_ref, xc_ref, gb_ref, brg_ref, big_ref, lam_ref, hs_ref, yv_ref):
        sp = _softplus(-lam_ref[...])
        brg, big = brg_ref[...], big_ref[...]
        rows = lax.broadcasted_iota(jnp.int32, (SUBLANES, tc), 0)

        def blk(k, carry):
            r0 = pl.multiple_of(k * SUBLANES, SUBLANES)
            sl = pl.ds(r0, SUBLANES)
            _, ig, a, mult = _lru_gates(pr_ref[sl, :], pi_ref[sl, :], brg, big, sp)
            b = mult * ig * xc_ref[sl, :]
            for d in (1, 2, 4):
                keep = rows >= d
                b = b + a * jnp.where(keep, pltpu.roll(b, d, 0), 0.0)
                a = a * jnp.where(keep, pltpu.roll(a, d, 0), 1.0)
            h = b + a * carry
            hs_ref[sl, :] = h
            return _row(h, SUBLANES - 1)

        def trip(kt, carry):
            for q in range(SCAN_UNROLL):
                carry = blk(kt * SCAN_UNROLL + q, carry)
            return carry

        lax.fori_loop(0, nblk // SCAN_UNROLL, trip, jnp.zeros((1, tc), F32))
        yv_ref[...] = (hs_ref[...] * _gelu(gb_ref[...])).astype(BF16)

    return pl.pallas_call(
        body, name=name, grid=(E // tc,),
        in_specs=[col, col, col, col, vec, vec, vec],
        out_specs=[col, col], out_shape=[_sds((L, E), F32), _sds((L, E), BF16)],
        compiler_params=_cp(("parallel",)),
    )(pre_r, pre_i, xc, p, b_rg, b_ig, lam)


def lru_scan_bwd(dyv, hs, pre_r, pre_i, xc, p, b_rg, b_ig, lam, name):
    L, E = xc.shape
    tc, col, vec = _lru_specs(L, E)
    nblk = L // SUBLANES

    def body(dyv_ref, hs_ref, pr_ref, pi_ref, xc_ref, gb_ref, brg_ref, big_ref, lam_ref,
             dgb_ref, dpr_ref, dpi_ref, dxc_ref, dbrg_ref, dbig_ref, dlam_ref, t_gb, t_pr, t_pi):
        lam_v = lam_ref[...]
        sp = _softplus(-lam_v)
        brg, big = brg_ref[...], big_ref[...]
        rows = lax.broadcasted_iota(jnp.int32, (SUBLANES, tc), 0)

        def blk(kk, carry):
            gcar, a_next, acc_sp, acc_r, acc_i = carry
            k = nblk - 1 - kk
            r0 = pl.multiple_of(k * SUBLANES, SUBLANES)
            sl = pl.ds(r0, SUBLANES)
            r, ig, a, mult = _lru_gates(pr_ref[sl, :], pi_ref[sl, :], brg, big, sp)
            gbv, hsv, dyvv, xcv = gb_ref[sl, :], hs_ref[sl, :], dyv_ref[sl, :], xc_ref[sl, :]
            t_gb[sl, :] = dyvv * hsv * _gelu_grad(gbv)
            x = dyvv * _gelu(gbv)
            al = jnp.where(rows == SUBLANES - 1, a_next, pltpu.roll(a, SUBLANES - 1, 0))
            for d in (1, 2, 4):
                keep = rows < SUBLANES - d
                x = x + al * jnp.where(keep, pltpu.roll(x, SUBLANES - d, 0), 0.0)
                al = al * jnp.where(keep, pltpu.roll(al, SUBLANES - d, 0), 1.0)
            g = x + al * gcar
            rp = pl.multiple_of(jnp.maximum(k - 1, 0) * SUBLANES, SUBLANES)
            hlast = _row(hs_ref[pl.ds(rp, SUBLANES), :], SUBLANES - 1) * (k > 0).astype(F32)
            hprev = jnp.where(rows == 0, hlast, pltpu.roll(hsv, 1, 0))
            da = g * hprev
            dmult = g * ig * xcv
            dig = g * mult * xcv
            dxc_ref[sl, :] = g * mult * ig
            dla = da * a - dmult * (a * a) / mult
            dpr = dla * (-LRU_C * sp) * r * (1.0 - r)
            dpi = dig * ig * (1.0 - ig)
            t_pr[sl, :] = dpr
            t_pi[sl, :] = dpi
            return (_row(g, 0), _row(a, 0), acc_sp + dla * (-LRU_C * r), acc_r + dpr, acc_i + dpi)

        zero = jnp.zeros((SUBLANES, tc), F32)
        z1 = jnp.zeros((1, tc), F32)
        def trip(kt, carry):
            for q in range(SCAN_UNROLL):
                carry = blk(kt * SCAN_UNROLL + q, carry)
            return carry

        _, _, acc_sp, acc_r, acc_i = lax.fori_loop(0, nblk // SCAN_UNROLL, trip, (z1, z1, zero, zero, zero))
        dgb_ref[...] = t_gb[...].astype(BF16)
        dpr_ref[...] = t_pr[...].astype(BF16)
        dpi_ref[...] = t_pi[...].astype(BF16)
        dbrg_ref[...] = _colsum(acc_r)
        dbig_ref[...] = _colsum(acc_i)
        dlam_ref[...] = -_colsum(acc_sp) * _sig(-lam_v)

    return pl.pallas_call(
        body, name=name, grid=(E // tc,),
        in_specs=[col, col, col, col, col, col, vec, vec, vec],
        out_specs=[col, col, col, col, vec, vec, vec],
        out_shape=[_sds((L, E), BF16), _sds((L, E), BF16), _sds((L, E), BF16), _sds((L, E), F32),
                   _sds((1, E), F32), _sds((1, E), F32), _sds((1, E), F32)],
        scratch_shapes=[pltpu.VMEM((L, tc), F32)] * 3,
        compiler_params=_cp(("parallel",)),
    )(dyv, hs, pre_r, pre_i, xc, p, b_rg, b_ig, lam)


def _place():
    xi, yi, ci = lax.axis_index("x"), lax.axis_index("y"), lax.axis_index("c")
    chips = [(1 - xi, yi), (xi, 1 - yi), (1 - xi, 1 - yi)]
    return xi, yi, ci, chips


_ANY = pl.BlockSpec(memory_space=pl.ANY)


def all_gather_devices(blks, name):
    n = len(blks)

    def body(*refs):
        ins, outs = refs[:n], refs[n:2 * n]
        send_sems, recv_sems, local_sems = refs[2 * n:]
        xi, yi, ci, chips = _place()
        me, sibling = (xi, yi, ci), (xi, yi, 1 - ci)

        def slab(a, px, py, pc):
            return outs[a].at[4 * px + 2 * py + pc]

        def copy(a, k, block, to, src=None):
            return pltpu.make_async_remote_copy(
                src_ref=slab(a, *block) if src is None else src, dst_ref=slab(a, *block),
                send_sem=send_sems.at[7 * a + k], recv_sem=recv_sems.at[7 * a + k], device_id=to,
                device_id_type=MESH)

        mine = [pltpu.make_async_copy(ins[a], slab(a, *me), local_sems.at[a]) for a in range(n)]
        first, passed = [], []
        for a in range(n):
            mine[a].start()
            first.append(copy(a, 0, me, sibling, src=ins[a]))
            first += [copy(a, 1 + j, me, (*chip, ci), src=ins[a]) for j, chip in enumerate(chips)]
        for cp in first:
            cp.start()
        for a in range(n):
            for j, chip in enumerate(chips):
                copy(a, 1 + j, (*chip, ci), me).wait_recv()
                passed.append(copy(a, 4 + j, (*chip, ci), sibling))
                passed[-1].start()
        for a in range(n):
            copy(a, 0, sibling, me).wait_recv()
            for j, chip in enumerate(chips):
                copy(a, 4 + j, (*chip, 1 - ci), me).wait_recv()
        for cp in first + passed:
            cp.wait_send()
        for cp in mine:
            cp.wait()

    return pl.pallas_call(
        body, name=name, in_specs=[_ANY] * n, out_specs=[_ANY] * n,
        out_shape=[_sds((N_DEV,) + b.shape, b.dtype) for b in blks],
        scratch_shapes=[pltpu.SemaphoreType.DMA((7 * n,)), pltpu.SemaphoreType.DMA((7 * n,)),
                        pltpu.SemaphoreType.DMA((n,))],
    )(*blks)


_HBM = pl.BlockSpec(memory_space=pltpu.HBM)
_SEM = pl.BlockSpec(memory_space=pltpu.SEMAPHORE)
_EFFECT = pltpu.SideEffectType.DATAFLOW_SIDE_EFFECTING


def split_start(name, groups, counts, copies_fn):
    flat = [b for g in groups for b in g]
    n, ng = len(flat), len(groups)

    def body(*refs):
        ins, sems, token = refs[:n], refs[n:n + 2 * ng], refs[-1]
        off = 0
        for gi, g in enumerate(groups):
            for cp in copies_fn(ins[off:off + len(g)], [b.shape for b in g], sems[2 * gi], sems[2 * gi + 1]):
                cp.start()
            off += len(g)
        token[...] = jnp.zeros_like(token)

    out_shape = tuple(pltpu.SemaphoreType.DMA((c,)) for c in counts for _ in range(2))
    out_shape += tuple(pltpu.HBM(b.shape, b.dtype) for b in flat) + (_sds((SUBLANES, LANES), F32),)
    outs = pl.pallas_call(
        body, name=name, in_specs=[_HBM] * n, out_shape=out_shape,
        out_specs=tuple([_SEM] * (2 * ng) + [_HBM] * n + [pl.BlockSpec(memory_space=pltpu.VMEM)]),
        input_output_aliases={i: 2 * ng + i for i in range(n)},
        compiler_params=pltpu.CompilerParams(has_side_effects=_EFFECT),
    )(*[pltpu.with_memory_space_constraint(b, pltpu.HBM) for b in flat])
    sems = [(outs[2 * gi], outs[2 * gi + 1]) for gi in range(ng)]
    thru, off = [], 2 * ng
    for g in groups:
        thru.append(list(outs[off:off + len(g)]))
        off += len(g)
    return sems, thru, outs[-1]


def split_wait(name, bufs, sems, copies_fn, after):
    n = len(bufs)
    shapes = [b.shape for b in bufs]

    def body(*refs):
        for cp in copies_fn(refs[:n], shapes, refs[n], refs[n + 1]):
            cp.wait_send()
            cp.wait_recv()

    return list(pl.pallas_call(
        body, name=name, in_specs=[_HBM] * n + [_SEM, _SEM, _ANY],
        out_shape=tuple(pltpu.HBM(b.shape, b.dtype) for b in bufs), out_specs=tuple([_HBM] * n),
        input_output_aliases={i: i for i in range(n)},
        compiler_params=pltpu.CompilerParams(has_side_effects=_EFFECT),
    )(*bufs, sems[0], sems[1], after))


def _gather_copies(bufs, shapes, send_sems, recv_sems):
    xi, yi, ci, chips = _place()
    cps = []
    for a, ref in enumerate(bufs):
        hr = shapes[a][1] // 2
        rows = ref.at[2 * xi + yi, pl.ds(pl.multiple_of(ci * hr, 16), hr), :]
        for j in range(3):
            cps.append(pltpu.make_async_remote_copy(
                src_ref=rows, dst_ref=rows, send_sem=send_sems.at[3 * a + j], recv_sem=recv_sems.at[3 * a + j],
                device_id=(*chips[j], ci), device_id_type=MESH))
    return cps


def _scatter_copies(bufs, shapes, send_sems, recv_sems):
    xi, yi, ci, chips = _place()
    n = len(bufs) // 2
    cps = []
    for a in range(n):
        for j in range(3):
            cps.append(pltpu.make_async_remote_copy(
                src_ref=bufs[a].at[2 * chips[j][0] + chips[j][1]], dst_ref=bufs[n + a].at[j],
                send_sem=send_sems.at[3 * a + j], recv_sem=recv_sems.at[3 * a + j],
                device_id=(*chips[j], ci), device_id_type=MESH))
    return cps


def cast_place(w, layer, chip1, name, after=None):
    _, R, C = w.shape
    tr = _tile(R, tuple(t for t in _TR if t * C <= ROW_TILE_ELEMS) or (16,))
    extra = [] if after is None else [after]

    def body(c_ref, w_ref, *rest):
        rest[-1][...] = w_ref[...].astype(BF16)

    return pl.pallas_call(
        body, name=name,
        grid_spec=pltpu.PrefetchScalarGridSpec(
            num_scalar_prefetch=1, grid=(R // tr,),
            in_specs=[pl.BlockSpec((None, tr, C), lambda i, c: (layer, i, 0))] + [_ANY] * len(extra),
            out_specs=pl.BlockSpec((None, tr, C), lambda i, c: (c[0], i, 0))),
        out_shape=_sds((N_CHIPS, R, C), BF16), compiler_params=_cp(("parallel",)),
    )(chip1, w, *extra)


def forward_halves(bufs, name):
    n = len(bufs)

    def body(*refs):
        ins, outs = refs[:n], refs[n:2 * n]
        send_sems, recv_sems = refs[2 * n:]
        xi, yi, ci, chips = _place()

        def copy(ref, a, j, h):
            hr = bufs[a].shape[1] // 2
            rows = pl.ds(pl.multiple_of(h * hr, 16), hr)
            slot = 2 * chips[j][0] + chips[j][1]
            return pltpu.make_async_remote_copy(
                src_ref=ref[a].at[slot, rows, :], dst_ref=outs[a].at[slot, rows, :],
                send_sem=send_sems.at[3 * a + j], recv_sem=recv_sems.at[3 * a + j],
                device_id=(xi, yi, 1 - ci), device_id_type=MESH)

        sends = [copy(ins, a, j, ci) for a in range(n) for j in range(3)]
        for cp in sends:
            cp.start()
        for a in range(n):
            for j in range(3):
                copy(outs, a, j, 1 - ci).wait_recv()
        for cp in sends:
            cp.wait_send()

    return pl.pallas_call(
        body, name=name, in_specs=[_ANY] * n, out_specs=[_ANY] * n,
        out_shape=[_sds(b.shape, b.dtype) for b in bufs], input_output_aliases={i: i for i in range(n)},
        scratch_shapes=[pltpu.SemaphoreType.DMA((3 * n,)), pltpu.SemaphoreType.DMA((3 * n,))],
    )(*bufs)


def exchange_halves(grads, name, after=None):
    n = len(grads)
    extra = [] if after is None else [after]

    def body(*refs):
        ins, outs = refs[:n], refs[n + len(extra):2 * n + len(extra)]
        send_sems, recv_sems = refs[2 * n + len(extra):]
        xi, yi, ci, _ = _place()
        cps = []
        for a in range(n):
            hr = grads[a].shape[1] // 2
            src = ins[a].at[:, pl.ds(pl.multiple_of((1 - ci) * hr, 16), hr), :]
            cps.append(pltpu.make_async_remote_copy(
                src_ref=src, dst_ref=outs[a], send_sem=send_sems.at[a], recv_sem=recv_sems.at[a],
                device_id=(xi, yi, 1 - ci), device_id_type=MESH))
            cps[-1].start()
        for cp in cps:
            cp.wait()

    return pl.pallas_call(
        body, name=name, in_specs=[_ANY] * (n + len(extra)), out_specs=[_ANY] * n,
        out_shape=[_sds((N_CHIPS, g.shape[1] // 2, g.shape[2]), g.dtype) for g in grads],
        scratch_shapes=[pltpu.SemaphoreType.DMA((n,)), pltpu.SemaphoreType.DMA((n,))],
    )(*grads, *extra)


def add_half(g, got, ci, name):
    S, hr, C = got.shape
    tr = _tile(hr, tuple(t for t in _TR if t * C <= ROW_TILE_ELEMS) or (16,))
    nb = hr // tr

    def body(c_ref, g_ref, r_ref, o_ref):
        o_ref[...] = (g_ref[...].astype(F32) + r_ref[...].astype(F32)).astype(BF16)

    return pl.pallas_call(
        body, name=name,
        grid_spec=pltpu.PrefetchScalarGridSpec(
            num_scalar_prefetch=1, grid=(S, nb),
            in_specs=[pl.BlockSpec((None, tr, C), lambda s, i, c: (s, c[0] * nb + i, 0)),
                      pl.BlockSpec((None, tr, C), lambda s, i, c: (s, i, 0))],
            out_specs=pl.BlockSpec((None, tr, C), lambda s, i, c: (s, i, 0))),
        out_shape=_sds((S, hr, C), BF16), compiler_params=_cp(("parallel", "parallel")),
    )(ci, g, got)


def add_chips(part, got, place, name):
    S, hr, C = part.shape
    tr = _tile(hr, tuple(t for t in _TR if t * C <= ROW_TILE_ELEMS) or (16,))
    nb = hr // tr

    def body(c_ref, p_ref, r_ref, o_ref):
        acc = p_ref[...].astype(F32)
        for j in range(3):
            acc = acc + r_ref[j].astype(F32)
        o_ref[...] = acc

    return pl.pallas_call(
        body, name=name,
        grid_spec=pltpu.PrefetchScalarGridSpec(
            num_scalar_prefetch=1, grid=(nb,),
            in_specs=[pl.BlockSpec((None, tr, C), lambda i, c: (c[0], i, 0)),
                      pl.BlockSpec((3, tr, C), lambda i, c: (0, i, 0))],
            out_specs=pl.BlockSpec((tr, C), lambda i, c: (c[1] * nb + i, 0))),
        out_shape=_sds((2 * hr, C), F32), compiler_params=_cp(("parallel",)),
    )(place, part, got)


def join_halves(bufs, name):
    n = len(bufs)

    def body(*refs):
        ins, outs = refs[:n], refs[n:2 * n]
        send_sems, recv_sems = refs[2 * n:]
        xi, yi, ci, _ = _place()

        def copy(ref, a, h):
            hr = bufs[a].shape[0] // 2
            rows = pl.ds(pl.multiple_of(h * hr, 8), hr)
            return pltpu.make_async_remote_copy(
                src_ref=ref[a].at[rows, :], dst_ref=outs[a].at[rows, :], send_sem=send_sems.at[a],
                recv_sem=recv_sems.at[a], device_id=(xi, yi, 1 - ci), device_id_type=MESH)

        sends = [copy(ins, a, ci) for a in range(n)]
        for cp in sends:
            cp.start()
        for a in range(n):
            copy(outs, a, 1 - ci).wait_recv()
        for cp in sends:
            cp.wait_send()

    return pl.pallas_call(
        body, name=name, in_specs=[_ANY] * n, out_specs=[_ANY] * n,
        out_shape=[_sds(b.shape, b.dtype) for b in bufs], input_output_aliases={i: i for i in range(n)},
        scratch_shapes=[pltpu.SemaphoreType.DMA((n,)), pltpu.SemaphoreType.DMA((n,))],
    )(*bufs)


def reduce_scatter_start(grads, ci1, tag, after=None):
    got = exchange_halves(grads, f"rs_halves_{tag}", after=after)
    parts = [add_half(g, r, ci1, f"rs_add_half_{tag}{a}") for a, (g, r) in enumerate(zip(grads, got))]
    lands = [lax.empty((3,) + q.shape[1:], q.dtype) for q in parts]
    sems, thru, token = split_start(f"rs_ici_start_{tag}", [parts + lands], [3 * len(parts)], _scatter_copies)
    return (sems[0], thru[0]), token


def reduce_scatter_finish(state, place, after, tag):
    sems, bufs = state
    bufs = split_wait(f"rs_ici_wait_{tag}", bufs, sems, _scatter_copies, after)
    n = len(bufs) // 2
    return [add_chips(bufs[a], bufs[n + a], place, f"rs_add_chips_{tag}{a}") for a in range(n)]


def _pack(parts, width):
    flat = jnp.concatenate([p.reshape(-1).astype(F32) for p in parts])
    per = SUBLANES * width
    total = -(-flat.shape[0] // per) * per
    flat = jnp.pad(flat, (0, total - flat.shape[0]))
    return flat.reshape(total // width, width)


def _unpack(flat, shapes):
    out, off = [], 0
    for s in shapes:
        n = math.prod(s)
        out.append(flat[off:off + n].reshape(s))
        off += n
    return out


_W_NAMES = ['norm_g', 'w_ada', 'b_ada', 's5_w_in', 's5_lam_re', 's5_lam_im', 's5_log_dt', 's5_b_re', 's5_b_im',
            's5_c_re', 's5_c_im', 's5_d', 's5_w_glu', 'lru_w_in', 'lru_conv_w', 'lru_conv_b', 'lru_w_rg',
            'lru_b_rg', 'lru_w_ig', 'lru_b_ig', 'lru_lam', 'lru_w_out', 'ffn_w_gu', 'ffn_w_down', 'final_g']
_BIG = ('w_ada', 's5_w_in', 's5_w_glu', 'lru_w_in', 'lru_w_out', 'ffn_w_gu', 'ffn_w_down')
_MID = ('s5_b_re', 's5_b_im', 's5_c_re', 's5_c_im', 'lru_w_rg', 'lru_w_ig')


def _ffn_fwd(x, h, w_gu, w_down, gate, tag):
    gu = mm_nn(h, w_gu, name=f"{tag}_gu", out_dtype=BF16, bmode="cols")
    act = swiglu_fwd(gu, f"{tag}_act")
    z = mm_nn(act, w_down, name=f"{tag}_down")
    return res_gate_fwd(x, z, gate, f"{tag}_res"), (gu, act, z)


def _ffn_bwd(dx, h, saved, w_gu, w_down, gate, tag):
    gu, act, z = saved
    dz, dgate = res_gate_bwd(dx, z, gate, f"{tag}_res_bwd")
    dact = mm_nt(dz, w_down, name=f"{tag}_dact")
    dw_down = mm_tn(act, dz, name=f"{tag}_dwdown", out_dtype=BF16)
    dgu = swiglu_bwd(dact, gu, f"{tag}_act_bwd")
    dh = mm_nt(dgu, w_gu, name=f"{tag}_dh", bmode="cols")
    dw_gu = mm_tn(h, dgu, name=f"{tag}_dwgu", out_dtype=BF16, omode="cols", groups=N_CHIPS)
    return dh, dgate, dw_gu, dw_down.reshape((N_CHIPS, -1) + dw_down.shape[1:])


def _step(p):
    xi, yi, ci = lax.axis_index("x"), lax.axis_index("y"), lax.axis_index("c")
    chip = 2 * xi + yi
    me = 2 * chip + ci
    ci1 = jnp.reshape(ci, (1,)).astype(jnp.int32)
    chip1 = jnp.reshape(chip, (1,)).astype(jnp.int32)
    place2 = jnp.stack([chip, ci]).astype(jnp.int32)

    x0 = p['x'][0]
    tgt = p['loss_target'][0]
    L, D = x0.shape
    Dq = D // N_CHIPS
    depth = p['w_ada'].shape[0]
    E = p['lru_lam'].shape[1] * N_CHIPS
    Eq = E // N_CHIPS
    kw = p['lru_conv_w'].shape[1]
    Nq = p['w_ada'].shape[2]
    _, G, P, Cg = p['s5_b_re'].shape
    nb, bs = p['lru_w_rg'].shape[1], p['lru_w_rg'].shape[2]

    pay = _pack([p['c'], p['norm_g'], p['lru_conv_w'], p['lru_conv_b'], p['lru_b_rg'], p['lru_b_ig'],
                 p['lru_lam']], 1024)
    g1 = all_gather_devices([pay], "gather_small_params")[0].reshape(N_DEV, -1)
    c_all = g1[:, :D]
    per_chip = g1[0::2]
    sizes = [(depth, 2, Dq), (kw, Eq), (Eq,), (Eq,), (Eq,), (Eq,)]
    offs = D
    pieces = []
    for s in sizes:
        nel = math.prod(s)
        pieces.append(per_chip[:, offs:offs + nel].reshape((N_CHIPS,) + s))
        offs += nel
    norm_g = jnp.moveaxis(pieces[0], 0, 2).reshape(depth, 2, D)
    conv_w = jnp.moveaxis(pieces[1], 0, 1).reshape(kw, E)
    conv_b, b_rg, b_ig, lam = [q.reshape(1, E) for q in pieces[2:]]

    cond = silu_rows(jnp.pad(c_all, ((0, 16 - N_DEV), (0, 0))), "cond_silu")
    cond_rep = jnp.concatenate([cond] * depth, axis=1)
    mod_part = mm_nn(cond_rep, p['w_ada'], name="mod_proj", bmode="batch")[:N_DEV]
    g2 = all_gather_devices([mod_part], "gather_mod")[0][0::2]
    mine = lax.dynamic_index_in_dim(g2, me, axis=1, keepdims=False).reshape(N_CHIPS, depth, Nq)
    mod = jnp.moveaxis(mine, 0, 1).reshape(depth, N_CHIPS * Nq) + p['b_ada']
    mods = [[mod[i:i + 1, k * D:(k + 1) * D] for k in range(N_MOD)] for i in range(depth)]

    def place(key, layer, tag, after=None):
        return cast_place(p[key], layer, chip1, f"place_{tag}", after=after)

    first = [place('s5_w_in', 0, "s5_in", after=mod), place('s5_w_glu', 0, "s5_glu")]
    sems_a, bufs_a, tok_a = split_start("gather_ici_start_s5", [first], [3 * len(first)], _gather_copies)
    others = [[place('ffn_w_gu', 0, "gu0", after=tok_a), place('ffn_w_down', 0, "down0")],
              [place('lru_w_in', 0, "lru_in"), place('lru_w_out', 0, "lru_out")],
              [place('ffn_w_gu', 1, "gu1"), place('ffn_w_down', 1, "down1")]]
    sems_b, bufs_b, tok_b = split_start("gather_ici_start_rest", others, [3 * len(g) for g in others], _gather_copies)
    wsems, wbufs = sems_a + sems_b, bufs_a + bufs_b

    def weights(gi, after, tag):
        landed = split_wait(f"gather_ici_wait_{tag}", wbufs[gi], wsems[gi], _gather_copies, after)
        return forward_halves(landed, f"gather_forward_{tag}")

    s5_small = (p['s5_lam_re'][0], p['s5_lam_im'][0], p['s5_log_dt'][0], p['s5_b_re'][0], p['s5_b_im'][0])
    (ab_re, ab_im, bb_re, bb_im), s5_disc_vjp = jax.vjp(_s5_discretize, *s5_small)
    bsb, csb = _s5_blockdiag(bb_re, bb_im, p['s5_c_re'][0], p['s5_c_im'][0])
    cf, crv = _s5_scan_consts(ab_re, ab_im)
    wsb_rg, wsb_ig = [w.astype(BF16) for w in _lru_blockdiag(p['lru_w_rg'][0], p['lru_w_ig'][0])]
    nsb_lru = wsb_rg.shape[0]

    sh1, sc1, gt1, sh2, sc2, gt2 = mods[0]
    s5_w_in, s5_w_glu = weights(0, tok_b, "s5")
    s5_w_in = s5_w_in.reshape(-1, D)
    h0 = norm_mod_fwd(x0, norm_g[0, 0:1], sc1, sh1, "l0_norm1")
    u = mm_nn(h0, s5_w_in, name="s5_in")
    ypre, yg, s_all, ss = s5_scan_fwd(u, p['s5_d'], bsb, csb, cf, "s5_scan")
    v = mm_nn(yg, s5_w_glu, name="s5_glu", bmode="cols")
    w_gu0, w_down0 = weights(1, v, "ffn0")
    w_down0 = w_down0.reshape(-1, D)
    x1 = glu_res_fwd(x0, v, gt1, "s5_res")
    h1 = norm_mod_fwd(x1, norm_g[0, 1:2], sc2, sh2, "l0_norm2")
    x2, ffn0 = _ffn_fwd(x1, h1, w_gu0, w_down0, gt2, "ffn0")

    sh1b, sc1b, gt1b, sh2b, sc2b, gt2b = mods[1]
    lru_w_in, lru_w_out = weights(2, ffn0[0], "lru")
    lru_w_out = lru_w_out.reshape(-1, D)
    h2 = norm_mod_fwd(x2, norm_g[1, 0:1], sc1b, sh1b, "l1_norm1")
    pq = mm_nn(h2, lru_w_in, name="lru_in", bmode="cols")
    xc, xcb = lru_conv_fwd(pq, conv_w, conv_b, "lru_conv")
    pre_r = mm_nn(xcb, wsb_rg, name="lru_gate_r", bmode="batch")
    pre_i = mm_nn(xcb, wsb_ig, name="lru_gate_i", bmode="batch")
    hs, yv = lru_scan_fwd(pre_r, pre_i, xc, pq, b_rg, b_ig, lam, "lru_scan")
    w_gu1, w_down1 = weights(3, hs, "ffn1")
    w_down1 = w_down1.reshape(-1, D)
    mix = mm_nn(yv, lru_w_out, name="lru_out")
    x3 = res_gate_fwd(x2, mix, gt1b, "lru_res")
    h3 = norm_mod_fwd(x3, norm_g[1, 1:2], sc2b, sh2b, "l1_norm2")
    x4, ffn1 = _ffn_fwd(x3, h3, w_gu1, w_down1, gt2b, "ffn1")

    fg = p['final_g'].reshape(1, D)
    dx4, loss_blk, dfinal_g = final_loss(x4, fg, tgt, "final_loss")
    loss = lax.psum(loss_blk[0, 0], ("x", "y", "c"))

    def rows4(g):
        return g.reshape((N_CHIPS, -1) + g.shape[1:])

    dh3, dgt2b, dw_gu1, dw_down1 = _ffn_bwd(dx4, h3, ffn1, w_gu1, w_down1, gt2b, "ffn1")
    rs_ffn1, tok = reduce_scatter_start([dw_gu1, dw_down1], ci1, "ffn1")
    dx3, dgn11, dsc2b, dsh2b = norm_mod_bwd(dh3, x3, norm_g[1, 1:2], sc2b, dx4, "l1_norm2_bwd", after=tok)

    dmix, dgt1b = res_gate_bwd(dx3, mix, gt1b, "lru_res_bwd")
    dyv = mm_nt(dmix, lru_w_out, name="lru_dyv")
    dw_out = mm_tn(yv, dmix, name="lru_dwout", out_dtype=BF16)
    dgb, dpre_r, dpre_i, dxc1, db_rg, db_ig, dlam = lru_scan_bwd(dyv, hs, pre_r, pre_i, xc, pq, b_rg, b_ig, lam,
                                                                "lru_scan_bwd")
    dxc2 = mm_nt(dpre_r, wsb_rg, name="lru_dxc_r", bmode="batch")
    dxc3 = mm_nt(dpre_i, wsb_ig, name="lru_dxc_i", bmode="batch")
    dwsb_rg = mm_tn(xcb, dpre_r, name="lru_dwgate_r", omode="batch", groups=nsb_lru)
    dwsb_ig = mm_tn(xcb, dpre_i, name="lru_dwgate_i", omode="batch", groups=nsb_lru)
    dxb, dconv_w, dconv_b = lru_conv_bwd(dxc1, dxc2, dxc3, pq, conv_w, "lru_conv_bwd")
    dpq = jnp.concatenate([dgb, dxb], axis=1)
    dh2 = mm_nt(dpq, lru_w_in, name="lru_dh", bmode="cols")
    dw_lru_in = mm_tn(h2, dpq, name="lru_dwin", out_dtype=BF16, omode="cols", groups=N_CHIPS)
    rs_lru, tok = reduce_scatter_start([dw_lru_in, rows4(dw_out)], ci1, "lru")
    dx2, dgn10, dsc1b, dsh1b = norm_mod_bwd(dh2, x2, norm_g[1, 0:1], sc1b, dx3, "l1_norm1_bwd", after=tok)

    dh1, dgt2, dw_gu0, dw_down0 = _ffn_bwd(dx2, h1, ffn0, w_gu0, w_down0, gt2, "ffn0")
    rs_ffn0, tok = reduce_scatter_start([dw_gu0, dw_down0], ci1, "ffn0")
    dx1, dgn01, dsc2, dsh2 = norm_mod_bwd(dh1, x1, norm_g[0, 1:2], sc2, dx2, "l0_norm2_bwd", after=tok)

    dv, dgt1 = glu_res_bwd(dx1, v, gt1, "s5_res_bwd")
    dyg = mm_nt(dv, s5_w_glu, name="s5_dyg", bmode="cols")
    dw_glu = mm_tn(yg, dv, name="s5_dwglu", out_dtype=BF16, omode="cols", groups=N_CHIPS)
    du, dbsb, dcsb, da, dd = s5_scan_bwd(u, dyg, ypre, s_all, p['s5_d'], bsb, csb, crv, ss, "s5_scan_bwd")
    dh0 = mm_nt(du, s5_w_in, name="s5_dh")
    dw_s5_in = mm_tn(h0, du, name="s5_dwin", out_dtype=BF16)
    grad_x, dgn00, dsc1, dsh1 = norm_mod_bwd(dh0, x0, norm_g[0, 0:1], sc1, dx1, "l0_norm1_bwd")

    dmod = jnp.concatenate([jnp.concatenate([dsh1, dsc1, dgt1, dsh2, dsc2, dgt2], axis=1),
                            jnp.concatenate([dsh1b, dsc1b, dgt1b, dsh2b, dsc2b, dgt2b], axis=1)], axis=0)
    dnorm_g = jnp.stack([jnp.concatenate([dgn00, dgn01]), jnp.concatenate([dgn10, dgn11])])
    dbb_re, dbb_im, dc_re, dc_im = _s5_blockdiag_grads(dbsb, dcsb, P, Cg)
    H = S5_SB_GROUPS * P
    da_re, da_im = da[:, 0, :H].reshape(G, P), da[:, 0, H:].reshape(G, P)
    dw_rg, dw_ig = _lru_blockdiag_grad(dwsb_rg, nb, bs), _lru_blockdiag_grad(dwsb_ig, nb, bs)
    small = [dmod, dnorm_g, da_re, da_im, dd, dconv_w, dconv_b, db_rg, db_ig, dlam, dfinal_g]
    small_shapes = [s.shape for s in small]
    payload = _pack(small, 1024)
    mid = [dbb_re, dbb_im, dc_re, dc_im, dw_rg, dw_ig]
    mid_shapes = [s.shape for s in mid]
    gathered = all_gather_devices([payload] + [s.reshape(s.shape[0], -1).astype(BF16) for s in mid],
                                  "gather_small_grads")
    gathered_small = gathered[0]
    rs_s5, tok_s5 = reduce_scatter_start([rows4(dw_s5_in), dw_glu], ci1, "s5", after=gathered_small)
    total = sum_devices(gathered_small, "sum_small_grads").reshape(-1)
    (s_dmod, s_norm_g, s_da_re, s_da_im, s_dd, s_conv_w, s_conv_b, s_b_rg, s_b_ig, s_lam,
     s_final_g) = _unpack(total, small_shapes)
    s_dbb_re, s_dbb_im, s_dc_re, s_dc_im, s_dw_rg, s_dw_ig = [
        sum_devices(g, f"sum_mid_grads_{i}").reshape(s) for i, (g, s) in enumerate(zip(gathered[1:], mid_shapes))]
    g_lam_re, g_lam_im, g_log_dt, g_b_re, g_b_im = s5_disc_vjp((s_da_re, s_da_im, s_dbb_re, s_dbb_im))

    npay = payload.shape[0] * payload.shape[1]
    dmod_all = gathered_small.reshape(N_DEV, npay)[:, :depth * N_MOD * D].reshape(N_DEV, depth, N_CHIPS, Nq)
    dmod_mine = lax.dynamic_index_in_dim(dmod_all, chip, axis=2, keepdims=False).reshape(N_DEV, depth * Nq)
    dmod_mine = jnp.pad(dmod_mine, ((0, 16 - N_DEV), (0, 0)))
    g_w_ada = mm_tn(cond_rep, dmod_mine, name="w_ada_grad", omode="batch", groups=depth)

    def cols(full, width):
        return lax.dynamic_slice_in_dim(full, chip * width, width, axis=full.ndim - 1)

    grads = {
        'norm_g': cols(s_norm_g, Dq), 'w_ada': g_w_ada, 'b_ada': s_dmod,
        's5_lam_re': g_lam_re, 's5_lam_im': g_lam_im, 's5_log_dt': g_log_dt, 's5_b_re': g_b_re, 's5_b_im': g_b_im,
        's5_c_re': s_dc_re, 's5_c_im': s_dc_im, 's5_d': s_dd, 'lru_conv_w': cols(s_conv_w, Eq),
        'lru_conv_b': cols(s_conv_b, Eq), 'lru_w_rg': s_dw_rg, 'lru_b_rg': cols(s_b_rg, Eq),
        'lru_w_ig': s_dw_ig, 'lru_b_ig': cols(s_b_ig, Eq), 'lru_lam': cols(s_lam, Eq), 'final_g': s_final_g,
    }
    grads = {k: g.reshape(p[k].shape) for k, g in grads.items()}

    delta, new_m, new_v = {}, {}, {}

    def adamw_2d(k, rows, after=None):
        w2 = p[k].reshape(rows, -1)
        outs = adamw(w2, grads[k].reshape(w2.shape), p['m_' + k].reshape(w2.shape), p['v_' + k].reshape(w2.shape),
                     f"adamw_{k}", after=after)
        delta[k], new_m[k], new_v[k] = [o.reshape(p[k].shape) for o in outs]

    adamw_2d('w_ada', depth * D, after=tok_s5)
    for k in _MID:
        adamw_2d(k, p[k].shape[1])
    rest = [k for k in _W_NAMES if k not in _BIG + _MID]
    shapes = [p[k].shape for k in rest]
    packed = [_pack([src[pre_ + k] if pre_ else src[k] for k in rest], 1024)
              for src, pre_ in ((p, ''), (grads, ''), (p, 'm_'), (p, 'v_'))]
    outs = adamw(*packed, "adamw_small")
    for dst, o in zip((delta, new_m, new_v), outs):
        for k, val in zip(rest, _unpack(o.reshape(-1), shapes)):
            dst[k] = val

    done = delta['w_ada']
    halves = []
    for state, tag in ((rs_ffn1, "ffn1"), (rs_lru, "lru"), (rs_ffn0, "ffn0"), (rs_s5, "s5")):
        halves += reduce_scatter_finish(state, place2, done, tag)
    g_gu1, g_down1, g_lru_in, g_lru_out, g_gu0, g_down0, g_s5_in, g_s5_glu = join_halves(halves, "rs_join_halves")
    grads.update({'s5_w_in': g_s5_in[None], 's5_w_glu': g_s5_glu[None], 'lru_w_in': g_lru_in[None],
                  'lru_w_out': g_lru_out[None], 'ffn_w_gu': jnp.stack([g_gu0, g_gu1]),
                  'ffn_w_down': jnp.stack([g_down0, g_down1])})
    for k in _BIG[1:]:
        adamw_2d(k, math.prod(p[k].shape[:-1]))

    return (loss, grad_x[None], *[grads[k] for k in _W_NAMES], *[delta[k] for k in _W_NAMES],
            *[new_m[k] for k in _W_NAMES], *[new_v[k] for k in _W_NAMES])


_IN_NAMES = (['x', 'c'] + _W_NAMES + ['loss_target'] + ['m_' + k for k in _W_NAMES] + ['v_' + k for k in _W_NAMES])


def kernel(x, c, norm_g, w_ada, b_ada, s5_w_in, s5_lam_re, s5_lam_im, s5_log_dt, s5_b_re, s5_b_im, s5_c_re, s5_c_im, s5_d, s5_w_glu, lru_w_in, lru_conv_w, lru_conv_b, lru_w_rg, lru_b_rg, lru_w_ig, lru_b_ig, lru_lam, lru_w_out, ffn_w_gu, ffn_w_down, final_g, loss_target, m_norm_g, m_w_ada, m_b_ada, m_s5_w_in, m_s5_lam_re, m_s5_lam_im, m_s5_log_dt, m_s5_b_re, m_s5_b_im, m_s5_c_re, m_s5_c_im, m_s5_d, m_s5_w_glu, m_lru_w_in, m_lru_conv_w, m_lru_conv_b, m_lru_w_rg, m_lru_b_rg, m_lru_w_ig, m_lru_b_ig, m_lru_lam, m_lru_w_out, m_ffn_w_gu, m_ffn_w_down, m_final_g, v_norm_g, v_w_ada, v_b_ada, v_s5_w_in, v_s5_lam_re, v_s5_lam_im, v_s5_log_dt, v_s5_b_re, v_s5_b_im, v_s5_c_re, v_s5_c_im, v_s5_d, v_s5_w_glu, v_lru_w_in, v_lru_conv_w, v_lru_conv_b, v_lru_w_rg, v_lru_b_rg, v_lru_w_ig, v_lru_b_ig, v_lru_lam, v_lru_w_out, v_ffn_w_gu, v_ffn_w_down, v_final_g):
    args = (x, c, norm_g, w_ada, b_ada, s5_w_in, s5_lam_re, s5_lam_im, s5_log_dt, s5_b_re, s5_b_im, s5_c_re, s5_c_im, s5_d, s5_w_glu, lru_w_in, lru_conv_w, lru_conv_b, lru_w_rg, lru_b_rg, lru_w_ig, lru_b_ig, lru_lam, lru_w_out, ffn_w_gu, ffn_w_down, final_g, loss_target, m_norm_g, m_w_ada, m_b_ada, m_s5_w_in, m_s5_lam_re, m_s5_lam_im, m_s5_log_dt, m_s5_b_re, m_s5_b_im, m_s5_c_re, m_s5_c_im, m_s5_d, m_s5_w_glu, m_lru_w_in, m_lru_conv_w, m_lru_conv_b, m_lru_w_rg, m_lru_b_rg, m_lru_w_ig, m_lru_b_ig, m_lru_lam, m_lru_w_out, m_ffn_w_gu, m_ffn_w_down, m_final_g, v_norm_g, v_w_ada, v_b_ada, v_s5_w_in, v_s5_lam_re, v_s5_lam_im, v_s5_log_dt, v_s5_b_re, v_s5_b_im, v_s5_c_re, v_s5_c_im, v_s5_d, v_s5_w_glu, v_lru_w_in, v_lru_conv_w, v_lru_conv_b, v_lru_w_rg, v_lru_b_rg, v_lru_w_ig, v_lru_b_ig, v_lru_lam, v_lru_w_out, v_ffn_w_gu, v_ffn_w_down, v_final_g)
    return _step(dict(zip(_IN_NAMES, args)))
```

```python
import functools
import math

import jax
import jax.numpy as jnp
from jax import lax
from jax.experimental import pallas as pl
from jax.experimental.pallas import tpu as pltpu

F32 = jnp.float32
BF16 = jnp.bfloat16
MESH = pl.DeviceIdType.MESH

EPS = 1e-6
LRU_C = 8.0
N_MOD = 6
ADAM_LR = 0.001
ADAM_B1 = 0.9
ADAM_B2 = 0.999
ADAM_EPS = 1e-08
ADAM_WD = 0.01
ADAM_STEP = 10

N_CHIPS = 4
N_DEV = 8
SUBLANES = 8
LANES = 128
S5_SB_GROUPS = 8
V7X_VMEM_LIMIT = 48 * 1024 * 1024
ROW_TILE_ELEMS = 512 * 1024
ROW_CALL_ELEMS = 4 * 1024 * 1024
SCAN_UNROLL = 2
LRU_TILE = 256

_TM = (1024, 1408, 512, 256, 128, 64, 32, 16, 8)
_TN = (1024, 1408, 512, 384, 256, 128)
_TK = (1024, 1408, 512, 256, 128)
_TR = (256, 128, 64, 32, 16, 8)

_GELU_K0 = math.sqrt(2.0 / math.pi)
_GELU_K1 = 0.044715


def _tile(n, cands):
    for t in cands:
        if n % t == 0:
            return t
    return n


def _cp(sem=None):
    return pltpu.CompilerParams(dimension_semantics=sem, vmem_limit_bytes=V7X_VMEM_LIMIT)


def _sds(shape, dtype):
    return jax.ShapeDtypeStruct(shape, dtype)


def _sig(x):
    return 1.0 / (1.0 + jnp.exp(-x))


def _gelu(x):
    t = jnp.tanh(_GELU_K0 * (x + _GELU_K1 * x * x * x))
    return 0.5 * x * (1.0 + t)


def _gelu_grad(x):
    x2 = x * x
    t = jnp.tanh(_GELU_K0 * (x + _GELU_K1 * x * x2))
    return 0.5 * (1.0 + t) + 0.5 * x * (1.0 - t * t) * _GELU_K0 * (1.0 + 3.0 * _GELU_K1 * x2)


def _softplus(z):
    return jnp.maximum(z, 0.0) + jnp.log(1.0 + jnp.exp(-jnp.abs(z)))


def _neg_expm1(x):
    series = -x * (1.0 + x * (0.5 + x * (1.0 / 6.0 + x * (1.0 / 24.0))))
    return jnp.where(x > -0.05, series, 1.0 - jnp.exp(x))


def _row(x, r):
    return x[r:r + 1, :]


def _colsum(x):
    return jnp.sum(x, axis=0, keepdims=True)


_NN = (((1,), (0,)), ((), ()))
_NT = (((1,), (1,)), ((), ()))
_TN_DIMS = (((0,), (0,)), ((), ()))


def _mm_call(name, a, b, a_spec, b_spec, o_spec, grid, out_shape, acc_shape, dims, after=None):
    nk = grid[-1]
    kaxis = len(grid) - 1
    extra = [] if after is None else [after]

    def body(a_ref, b_ref, *rest):
        o_ref, acc_ref = rest[-2], rest[-1]
        k = pl.program_id(kaxis)

        def prod():
            return lax.dot_general(a_ref[...].astype(BF16), b_ref[...].astype(BF16), dims,
                                   preferred_element_type=F32)

        if nk == 1:
            o_ref[...] = prod().astype(o_ref.dtype)
            return

        @pl.when(k == 0)
        def _():
            acc_ref[...] = prod()

        if nk > 2:
            @pl.when(jnp.logical_and(k > 0, k < nk - 1))
            def _():
                acc_ref[...] += prod()

        @pl.when(k == nk - 1)
        def _():
            o_ref[...] = (acc_ref[...] + prod()).astype(o_ref.dtype)

    return pl.pallas_call(
        body, name=name, grid=grid, in_specs=[a_spec, b_spec] + [pl.BlockSpec(memory_space=pl.ANY)] * len(extra),
        out_specs=o_spec, out_shape=out_shape, scratch_shapes=[pltpu.VMEM(acc_shape, F32)],
        compiler_params=_cp(("parallel", "parallel", "parallel", "arbitrary")),
    )(a, b, *extra)


def mm_nn(a, b, *, name, out_dtype=F32, bmode="plain"):
    M = a.shape[0]
    if bmode == "plain":
        G, S = 1, 1
        K, Nc = b.shape
    elif bmode == "cols":
        G = 1
        S, K, Nc = b.shape
    else:
        S = 1
        G, K, Nc = b.shape
    tm, tn, tk = _tile(M, _TM), _tile(Nc, _TN), _tile(K, _TK)
    nkb, nnb = K // tk, Nc // tn
    ncol = S * nnb
    grid = (G, M // tm, ncol, nkb)
    a_spec = pl.BlockSpec((tm, tk), lambda g, i, j, k: (i, g * nkb + k))
    if bmode == "plain":
        b_spec = pl.BlockSpec((tk, tn), lambda g, i, j, k: (k, j))
    elif bmode == "cols":
        b_spec = pl.BlockSpec((None, tk, tn), lambda g, i, j, k: (j // nnb, k, j % nnb))
    else:
        b_spec = pl.BlockSpec((None, tk, tn), lambda g, i, j, k: (g, k, j))
    o_spec = pl.BlockSpec((tm, tn), lambda g, i, j, k: (i, g * ncol + j))
    return _mm_call(name, a, b, a_spec, b_spec, o_spec, grid, _sds((M, G * S * Nc), out_dtype), (tm, tn), _NN)


def mm_nt(a, b, *, name, out_dtype=F32, bmode="plain", after=None):
    M = a.shape[0]
    if bmode == "plain":
        G, S = 1, 1
        Ko, Nc = b.shape
    elif bmode == "cols":
        G = 1
        S, Ko, Nc = b.shape
    else:
        S = 1
        G, Ko, Nc = b.shape
    tm, to, tc = _tile(M, _TM), _tile(Ko, _TN), _tile(Nc, _TK)
    npc = Nc // tc
    nc = S * npc
    nob = Ko // to
    grid = (G, M // tm, nob, nc)
    a_spec = pl.BlockSpec((tm, tc), lambda g, i, j, n: (i, g * nc + n))
    if bmode == "plain":
        b_spec = pl.BlockSpec((to, tc), lambda g, i, j, n: (j, n))
    elif bmode == "cols":
        b_spec = pl.BlockSpec((None, to, tc), lambda g, i, j, n: (n // npc, j, n % npc))
    else:
        b_spec = pl.BlockSpec((None, to, tc), lambda g, i, j, n: (g, j, n))
    o_spec = pl.BlockSpec((tm, to), lambda g, i, j, n: (i, g * nob + j))
    return _mm_call(name, a, b, a_spec, b_spec, o_spec, grid, _sds((M, G * Ko), out_dtype), (tm, to), _NT,
                    after=after)


def mm_tn(a, b, *, name, out_dtype=F32, omode="plain", groups=1):
    L = a.shape[0]
    G = groups if omode == "batch" else 1
    S = groups if omode == "cols" else 1
    Mo, N = a.shape[1] // G, b.shape[1] // G
    Nc = N // S
    tm, tn, tl = _tile(Mo, _TM), _tile(Nc, _TN), _tile(L, _TK)
    nmb, nnb = Mo // tm, N // tn
    npj = Nc // tn
    grid = (G, nmb, nnb, L // tl)
    a_spec = pl.BlockSpec((tl, tm), lambda g, i, j, l: (l, g * nmb + i))
    b_spec = pl.BlockSpec((tl, tn), lambda g, i, j, l: (l, g * nnb + j))
    if omode == "plain":
        o_spec = pl.BlockSpec((tm, tn), lambda g, i, j, l: (i, j))
        oshape = (Mo, N)
    elif omode == "cols":
        o_spec = pl.BlockSpec((None, tm, tn), lambda g, i, j, l: (j // npj, i, j % npj))
        oshape = (S, Mo, Nc)
    else:
        o_spec = pl.BlockSpec((None, tm, tn), lambda g, i, j, l: (g, i, j))
        oshape = (G, Mo, N)
    return _mm_call(name, a, b, a_spec, b_spec, o_spec, grid, _sds(oshape, out_dtype), (tm, tn), _TN_DIMS)


def _row_call(name, body, row_ins, vec_ins, row_outs, acc_outs=(), after=None):
    if after is not None:
        n_in = len(row_ins) + len(vec_ins)
        inner = body

        def body(*refs):
            inner(*refs[:n_in], *refs[n_in + 1:])

        return _row_call_impl(name, body, row_ins, vec_ins, row_outs, acc_outs, [after])
    return _row_call_impl(name, body, row_ins, vec_ins, row_outs, acc_outs, [])


def _row_call_impl(name, body, row_ins, vec_ins, row_outs, acc_outs, extra):
    L = row_ins[0].shape[0]
    wmax = max([a.shape[1] for a in row_ins] + [w for w, _ in row_outs])
    narr = len(row_ins) + len(row_outs)
    tr = _tile(L, tuple(t for t in _TR if t * wmax * narr <= ROW_CALL_ELEMS) or (SUBLANES,))
    in_specs = [pl.BlockSpec((tr, a.shape[1]), lambda i: (i, 0)) for a in row_ins]
    in_specs += [pl.BlockSpec(v.shape, lambda i, nd=v.ndim: (0,) * nd) for v in vec_ins]
    in_specs += [pl.BlockSpec(memory_space=pl.ANY) for _ in extra]
    out_shape = [_sds((L, w), dt) for w, dt in row_outs] + [_sds(s, dt) for s, dt in acc_outs]
    out_specs = [pl.BlockSpec((tr, w), lambda i: (i, 0)) for w, _ in row_outs]
    out_specs += [pl.BlockSpec(s, lambda i, nd=len(s): (0,) * nd) for s, _ in acc_outs]
    sem = ("arbitrary",) if acc_outs else ("parallel",)
    return pl.pallas_call(body, name=name, grid=(L // tr,), in_specs=in_specs, out_specs=out_specs,
                          out_shape=out_shape, compiler_params=_cp(sem))(*row_ins, *vec_ins, *extra)


def silu_rows(x, name, after=None):
    def body(x_ref, o_ref):
        v = x_ref[...]
        o_ref[...] = (v * _sig(v)).astype(o_ref.dtype)
    return _row_call(name, body, [x], [], [(x.shape[1], BF16)], after=after)[0]


def norm_mod_fwd(x, gain, sc, sh, name, after=None):
    def body(x_ref, g_ref, sc_ref, sh_ref, h_ref):
        v = x_ref[...]
        r = lax.rsqrt(jnp.mean(v * v, axis=-1, keepdims=True) + EPS)
        h_ref[...] = (v * r * g_ref[...] * (1.0 + sc_ref[...]) + sh_ref[...]).astype(BF16)
    return _row_call(name, body, [x], [gain, sc, sh], [(x.shape[1], BF16)], after=after)[0]


def norm_mod_bwd(dh, x, gain, sc, dres, name, after=None):
    D = x.shape[1]

    def body(dh_ref, x_ref, dres_ref, g_ref, sc_ref, dx_ref, dg_ref, dsc_ref, dsh_ref):
        @pl.when(pl.program_id(0) == 0)
        def _():
            dg_ref[...] = jnp.zeros_like(dg_ref)
            dsc_ref[...] = jnp.zeros_like(dsc_ref)
            dsh_ref[...] = jnp.zeros_like(dsh_ref)

        v = x_ref[...]
        dh_v = dh_ref[...]
        g = g_ref[...]
        r = lax.rsqrt(jnp.mean(v * v, axis=-1, keepdims=True) + EPS)
        xhat = v * r
        dn = dh_v * (1.0 + sc_ref[...])
        dsc_ref[...] += _colsum(dh_v * xhat * g)
        dsh_ref[...] += _colsum(dh_v)
        dg_ref[...] += _colsum(dn * xhat)
        t = dn * g
        dx_ref[...] = dres_ref[...] + r * (t - xhat * jnp.mean(t * xhat, axis=-1, keepdims=True))

    acc = [((1, D), F32)] * 3
    return _row_call(name, body, [dh, x, dres], [gain, sc], [(D, F32)], acc, after=after)


def final_loss(x, gain, tgt, name):
    D = x.shape[1]

    def body(x_ref, t_ref, g_ref, dx_ref, loss_ref, dg_ref, acc_ref):
        i = pl.program_id(0)

        @pl.when(i == 0)
        def _():
            dg_ref[...] = jnp.zeros_like(dg_ref)
            acc_ref[...] = jnp.zeros_like(acc_ref)

        v = x_ref[...]
        g = g_ref[...]
        r = lax.rsqrt(jnp.mean(v * v, axis=-1, keepdims=True) + EPS)
        xhat = v * r
        err = xhat * g - t_ref[...]
        acc_ref[...] += _colsum(err * err)
        dout = err * (1.0 / D)
        dg_ref[...] += _colsum(dout * xhat)
        t = dout * g
        dx_ref[...] = r * (t - xhat * jnp.mean(t * xhat, axis=-1, keepdims=True))

        @pl.when(i == pl.num_programs(0) - 1)
        def _():
            loss_ref[...] = jnp.zeros_like(loss_ref) + jnp.sum(acc_ref[...]) * (0.5 / D)

    return _row_call(name, body, [x, tgt], [gain], [(D, F32)],
                     [((SUBLANES, LANES), F32), ((1, D), F32), ((1, D), F32)])[:3]


def res_gate_fwd(x, z, g, name):
    def body(x_ref, z_ref, g_ref, o_ref):
        o_ref[...] = x_ref[...] + g_ref[...] * z_ref[...]
    return _row_call(name, body, [x, z], [g], [(x.shape[1], F32)])[0]


def res_gate_bwd(dx, z, g, name):
    D = dx.shape[1]

    def body(dx_ref, z_ref, g_ref, dz_ref, dg_ref):
        @pl.when(pl.program_id(0) == 0)
        def _():
            dg_ref[...] = jnp.zeros_like(dg_ref)
        d = dx_ref[...]
        dz_ref[...] = (g_ref[...] * d).astype(BF16)
        dg_ref[...] += _colsum(d * z_ref[...])
    return _row_call(name, body, [dx, z], [g], [(D, BF16)], [((1, D), F32)])


def glu_res_fwd(x, v, g, name):
    D = x.shape[1]

    def body(x_ref, v_ref, g_ref, o_ref):
        vv = v_ref[...]
        o_ref[...] = x_ref[...] + g_ref[...] * (vv[:, :D] * _sig(vv[:, D:]))
    return _row_call(name, body, [x, v], [g], [(D, F32)])[0]


def glu_res_bwd(dx, v, g, name, after=None):
    D = dx.shape[1]

    def body(dx_ref, v_ref, g_ref, dv_ref, dg_ref):
        @pl.when(pl.program_id(0) == 0)
        def _():
            dg_ref[...] = jnp.zeros_like(dg_ref)
        d = dx_ref[...]
        vv = v_ref[...]
        val = vv[:, :D]
        s = _sig(vv[:, D:])
        dg_ref[...] += _colsum(d * val * s)
        dm = g_ref[...] * d
        dv_ref[:, :D] = (dm * s).astype(BF16)
        dv_ref[:, D:] = (dm * val * s * (1.0 - s)).astype(BF16)
    return _row_call(name, body, [dx, v], [g], [(2 * D, BF16)], [((1, D), F32)], after=after)


def swiglu_fwd(gu, name):
    F = gu.shape[1] // 2

    def body(gu_ref, o_ref):
        v = gu_ref[...].astype(F32)
        g = v[:, :F]
        o_ref[...] = (g * _sig(g) * v[:, F:]).astype(BF16)
    return _row_call(name, body, [gu], [], [(F, BF16)])[0]


def swiglu_bwd(dact, gu, name):
    F = gu.shape[1] // 2

    def body(da_ref, gu_ref, o_ref):
        v = gu_ref[...].astype(F32)
        g, u = v[:, :F], v[:, F:]
        da = da_ref[...]
        s = _sig(g)
        o_ref[:, :F] = (da * u * s * (1.0 + g * (1.0 - s))).astype(BF16)
        o_ref[:, F:] = (da * g * s).astype(BF16)
    return _row_call(name, body, [dact, gu], [], [(2 * F, BF16)])[0]


def adamw(w, g, m, v, name, after=None):
    C = w.shape[1]
    c1 = 1.0 - ADAM_B1 ** ADAM_STEP
    c2 = 1.0 - ADAM_B2 ** ADAM_STEP

    def body(w_ref, g_ref, m_ref, v_ref, d_ref, m2_ref, v2_ref):
        gv = g_ref[...]
        m2 = ADAM_B1 * m_ref[...] + (1.0 - ADAM_B1) * gv
        v2 = ADAM_B2 * v_ref[...] + (1.0 - ADAM_B2) * (gv * gv)
        m2_ref[...] = m2
        v2_ref[...] = v2
        d_ref[...] = -ADAM_LR * ((m2 / c1) / (jnp.sqrt(v2 / c2) + ADAM_EPS) + ADAM_WD * w_ref[...])
    return _row_call(name, body, [w, g, m, v], [], [(C, F32)] * 3, after=after)


def sum_devices(parts, name):
    n, R, C = parts.shape
    min_rows = 16 if parts.dtype == BF16 else SUBLANES
    tr = _tile(R, tuple(t for t in _TR if t * C * n <= 4 * ROW_TILE_ELEMS and t >= min_rows) or (min_rows,))

    def body(p_ref, o_ref):
        acc = p_ref[0].astype(F32)
        for d in range(1, n):
            acc = acc + p_ref[d].astype(F32)
        o_ref[...] = acc
    return pl.pallas_call(body, name=name, grid=(R // tr,),
                          in_specs=[pl.BlockSpec((n, tr, C), lambda i: (0, i, 0))],
                          out_specs=pl.BlockSpec((tr, C), lambda i: (i, 0)), out_shape=_sds((R, C), F32),
                          compiler_params=_cp(("parallel",)))(parts)


def _s5_discretize(lam_re, lam_im, log_dt, b_re, b_im):
    dt = jnp.exp(log_dt)[:, None]
    mag = jnp.exp(lam_re * dt)
    ab_re = mag * jnp.cos(lam_im * dt)
    ab_im = mag * jnp.sin(lam_im * dt)
    nr, ni = ab_re - 1.0, ab_im
    den = lam_re * lam_re + lam_im * lam_im
    f_re = (nr * lam_re + ni * lam_im) / den
    f_im = (ni * lam_re - nr * lam_im) / den
    bb_re = f_re[..., None] * b_re - f_im[..., None] * b_im
    bb_im = f_re[..., None] * b_im + f_im[..., None] * b_re
    return ab_re, ab_im, bb_re, bb_im


def _s5_blockdiag(bb_re, bb_im, c_re, c_im):
    G, P, Cg = bb_re.shape
    nsb = G // S5_SB_GROUPS

    def bmat(bb):
        return _block_diag(jnp.swapaxes(bb.reshape(nsb, S5_SB_GROUPS, P, Cg), 2, 3))

    def cmat(cc):
        return _block_diag(jnp.swapaxes(cc.reshape(nsb, S5_SB_GROUPS, Cg, P), 2, 3))

    bsb = jnp.concatenate([bmat(bb_re), bmat(bb_im)], axis=-1)
    csb = jnp.concatenate([cmat(c_re), -cmat(c_im)], axis=1)
    return bsb, csb


def _block_diag(t):
    ng, b = t.shape[1], t.shape[3]
    rows = [jnp.pad(t[:, g], ((0, 0), (0, 0), (g * b, (ng - 1 - g) * b))) for g in range(ng)]
    return jnp.concatenate(rows, axis=1)


def _diag_blocks(m, ng):
    a, b = m.shape[1] // ng, m.shape[2] // ng
    return jnp.stack([m[:, g * a:(g + 1) * a, g * b:(g + 1) * b] for g in range(ng)], axis=1)


def _s5_blockdiag_grads(dbsb, dcsb, P, Cg):
    H = S5_SB_GROUPS * P

    def blocks(m):
        return jnp.swapaxes(_diag_blocks(m, S5_SB_GROUPS), 2, 3)

    dbb_re = blocks(dbsb[:, :, :H]).reshape(-1, P, Cg)
    dbb_im = blocks(dbsb[:, :, H:]).reshape(-1, P, Cg)
    dc_re = blocks(dcsb[:, :H, :]).reshape(-1, Cg, P)
    dc_im = -blocks(dcsb[:, H:, :]).reshape(-1, Cg, P)
    return dbb_re, dbb_im, dc_re, dc_im


def _s5_scan_consts(ab_re, ab_im):
    G, P = ab_re.shape
    nsb = G // S5_SB_GROUPS
    H = S5_SB_GROUPS * P
    ar, ai = ab_re.reshape(nsb, 1, H), ab_im.reshape(nsb, 1, H)
    pows = [(ar, ai)]
    for _ in range(SUBLANES - 1):
        pr, pi_ = pows[-1]
        pows.append((pr * ar - pi_ * ai, pr * ai + pi_ * ar))
    rows = jnp.arange(SUBLANES).reshape(1, SUBLANES, 1)

    def masked(k, keep):
        pr, pi_ = pows[k - 1]
        return jnp.where(keep, pr, 0.0), jnp.where(keep, pi_, 0.0)

    def per_row(sel):
        pr = jnp.concatenate([pows[sel(r) - 1][0] for r in range(SUBLANES)], axis=1)
        pi_ = jnp.concatenate([pows[sel(r) - 1][1] for r in range(SUBLANES)], axis=1)
        return pr, pi_

    fwd = [masked(1, rows >= 1), masked(2, rows >= 2), masked(4, rows >= 4), per_row(lambda r: r + 1)]
    rev = [masked(1, rows < 7), masked(2, rows < 6), masked(4, rows < 4), per_row(lambda r: SUBLANES - r)]

    def pack(lst, conj):
        sgn = -1.0 if conj else 1.0
        return jnp.stack([jnp.concatenate([jnp.broadcast_to(pr, (nsb, SUBLANES, H)),
                                           sgn * jnp.broadcast_to(pi_, (nsb, SUBLANES, H))], axis=-1)
                          for pr, pi_ in lst], axis=1)

    return pack(fwd, False), pack(rev, True)


def _cmadd(xr, xi, ar, ai, yr, yi):
    return xr + ar * yr - ai * yi, xi + ar * yi + ai * yr


def _s5_scan_fwd_loop(src_ref, dst_ref, cf_ref, cr, ci, nblk, H):
    def body(k, carry):
        cr, ci = carry
        r0 = pl.multiple_of(k * SUBLANES, SUBLANES)
        xr = src_ref[pl.ds(r0, SUBLANES), pl.ds(0, H)]
        xi = src_ref[pl.ds(r0, SUBLANES), pl.ds(H, H)]
        for idx, d in enumerate((1, 2, 4)):
            xr, xi = _cmadd(xr, xi, cf_ref[idx, :, pl.ds(0, H)], cf_ref[idx, :, pl.ds(H, H)],
                            pltpu.roll(xr, d, 0), pltpu.roll(xi, d, 0))
        xr, xi = _cmadd(xr, xi, cf_ref[3, :, pl.ds(0, H)], cf_ref[3, :, pl.ds(H, H)], cr, ci)
        dst_ref[pl.ds(r0, SUBLANES), pl.ds(0, H)] = xr
        dst_ref[pl.ds(r0, SUBLANES), pl.ds(H, H)] = xi
        return _row(xr, SUBLANES - 1), _row(xi, SUBLANES - 1)

    return lax.fori_loop(0, nblk, body, (cr, ci))


def s5_scan_fwd(u, d_skip, bsb, csb, cf, name):
    L, W = u.shape
    nsb, GW, H2 = bsb.shape
    H = H2 // 2
    Tc = _tile(L, (512, 256, 128, 64, 32, 16, 8))
    nch = L // Tc

    def body(u_ref, d_ref, b_ref, c_ref, cf_ref, ypre_ref, yg_ref, s_ref, ss_ref, bu_scr, car_scr):
        @pl.when(pl.program_id(1) == 0)
        def _():
            car_scr[...] = jnp.zeros_like(car_scr)

        ss_ref[...] = car_scr[...]
        ub = u_ref[...]
        bu_scr[...] = jnp.dot(ub.astype(BF16), b_ref[...], preferred_element_type=F32)
        cr, ci = _s5_scan_fwd_loop(bu_scr, s_ref, cf_ref, car_scr[:, pl.ds(0, H)], car_scr[:, pl.ds(H, H)],
                                   Tc // SUBLANES, H)
        car_scr[:, pl.ds(0, H)] = cr
        car_scr[:, pl.ds(H, H)] = ci
        ypre = jnp.dot(s_ref[...].astype(BF16), c_ref[...], preferred_element_type=F32) + d_ref[...] * ub
        ypre_ref[...] = ypre
        yg_ref[...] = _gelu(ypre).astype(BF16)

    return pl.pallas_call(
        body, name=name, grid=(nsb, nch),
        in_specs=[pl.BlockSpec((Tc, GW), lambda j, i: (i, j)),
                  pl.BlockSpec((1, GW), lambda j, i: (0, j)),
                  pl.BlockSpec((None, GW, H2), lambda j, i: (j, 0, 0)),
                  pl.BlockSpec((None, H2, GW), lambda j, i: (j, 0, 0)),
                  pl.BlockSpec((None, 4, SUBLANES, H2), lambda j, i: (j, 0, 0, 0))],
        out_specs=[pl.BlockSpec((Tc, GW), lambda j, i: (i, j)),
                   pl.BlockSpec((Tc, GW), lambda j, i: (i, j)),
                   pl.BlockSpec((Tc, H2), lambda j, i: (i, j)),
                   pl.BlockSpec((None, None, 1, H2), lambda j, i: (i, j, 0, 0))],
        out_shape=[_sds((L, W), F32), _sds((L, W), BF16), _sds((L, nsb * H2), F32),
                   _sds((nch, nsb, 1, H2), F32)],
        scratch_shapes=[pltpu.VMEM((Tc, H2), F32), pltpu.VMEM((1, H2), F32)],
        compiler_params=_cp(("arbitrary", "arbitrary")),
    )(u, d_skip, bsb.astype(BF16), csb.astype(BF16), cf)


def s5_scan_bwd(u, dyg, ypre, s_all, d_skip, bsb, csb, crv, ss, name):
    L, W = u.shape
    nsb, GW, H2 = bsb.shape
    H = H2 // 2
    Tc = _tile(L, (512, 256, 128, 64, 32, 16, 8))
    nch = L // Tc
    nblk = Tc // SUBLANES
    bsb_t = jnp.swapaxes(bsb, 1, 2).astype(BF16)
    csb_t = jnp.swapaxes(csb, 1, 2).astype(BF16)

    def body(u_ref, dyg_ref, yp_ref, s_ref, d_ref, bt_ref, ct_ref, crv_ref, ss_ref,
             du_ref, db_ref, dc_ref, da_ref, dd_ref, g_scr, gcar_scr):
        @pl.when(pl.program_id(1) == 0)
        def _():
            gcar_scr[...] = jnp.zeros_like(gcar_scr)
            db_ref[...] = jnp.zeros_like(db_ref)
            dc_ref[...] = jnp.zeros_like(dc_ref)
            da_ref[...] = jnp.zeros_like(da_ref)
            dd_ref[...] = jnp.zeros_like(dd_ref)

        ub = u_ref[...]
        ubf = ub.astype(BF16)
        dyp = dyg_ref[...] * _gelu_grad(yp_ref[...])
        dypb = dyp.astype(BF16)
        dd_ref[...] += _colsum(dyp * ub)
        g_scr[...] = jnp.dot(dypb, ct_ref[...], preferred_element_type=F32)
        rows = lax.broadcasted_iota(jnp.int32, (SUBLANES, H), 0)

        def rev(kk, carry):
            gr, gi, acc_r, acc_i = carry
            k = nblk - 1 - kk
            r0 = pl.multiple_of(k * SUBLANES, SUBLANES)
            xr = g_scr[pl.ds(r0, SUBLANES), pl.ds(0, H)]
            xi = g_scr[pl.ds(r0, SUBLANES), pl.ds(H, H)]
            for idx, d in enumerate((1, 2, 4)):
                xr, xi = _cmadd(xr, xi, crv_ref[idx, :, pl.ds(0, H)], crv_ref[idx, :, pl.ds(H, H)],
                                pltpu.roll(xr, SUBLANES - d, 0), pltpu.roll(xi, SUBLANES - d, 0))
            xr, xi = _cmadd(xr, xi, crv_ref[3, :, pl.ds(0, H)], crv_ref[3, :, pl.ds(H, H)], gr, gi)
            g_scr[pl.ds(r0, SUBLANES), pl.ds(0, H)] = xr
            g_scr[pl.ds(r0, SUBLANES), pl.ds(H, H)] = xi
            rp = pl.multiple_of(jnp.maximum(k - 1, 0) * SUBLANES, SUBLANES)
            first = k == 0
            pr = jnp.where(first, ss_ref[:, pl.ds(0, H)], _row(s_ref[pl.ds(rp, SUBLANES), pl.ds(0, H)], SUBLANES - 1))
            pi_ = jnp.where(first, ss_ref[:, pl.ds(H, H)], _row(s_ref[pl.ds(rp, SUBLANES), pl.ds(H, H)], SUBLANES - 1))
            spr = jnp.where(rows == 0, pr, pltpu.roll(s_ref[pl.ds(r0, SUBLANES), pl.ds(0, H)], 1, 0))
            spi = jnp.where(rows == 0, pi_, pltpu.roll(s_ref[pl.ds(r0, SUBLANES), pl.ds(H, H)], 1, 0))
            return (_row(xr, 0), _row(xi, 0), acc_r + xr * spr + xi * spi, acc_i + xi * spr - xr * spi)

        zero = jnp.zeros((SUBLANES, H), F32)
        gr, gi, acc_r, acc_i = lax.fori_loop(
            0, nblk, rev, (gcar_scr[:, pl.ds(0, H)], gcar_scr[:, pl.ds(H, H)], zero, zero))
        gcar_scr[:, pl.ds(0, H)] = gr
        gcar_scr[:, pl.ds(H, H)] = gi
        da_ref[:, pl.ds(0, H)] += _colsum(acc_r)
        da_ref[:, pl.ds(H, H)] += _colsum(acc_i)
        gb = g_scr[...].astype(BF16)
        db_ref[...] += lax.dot_general(ubf, gb, _TN_DIMS, preferred_element_type=F32)
        dc_ref[...] += lax.dot_general(s_ref[...].astype(BF16), dypb, _TN_DIMS, preferred_element_type=F32)
        du_ref[...] = (jnp.dot(gb, bt_ref[...], preferred_element_type=F32) + d_ref[...] * dyp).astype(BF16)

    rmap = lambda j, i: (nch - 1 - i, j)
    return pl.pallas_call(
        body, name=name, grid=(nsb, nch),
        in_specs=[pl.BlockSpec((Tc, GW), rmap), pl.BlockSpec((Tc, GW), rmap), pl.BlockSpec((Tc, GW), rmap),
                  pl.BlockSpec((Tc, H2), rmap),
                  pl.BlockSpec((1, GW), lambda j, i: (0, j)),
                  pl.BlockSpec((None, H2, GW), lambda j, i: (j, 0, 0)),
                  pl.BlockSpec((None, GW, H2), lambda j, i: (j, 0, 0)),
                  pl.BlockSpec((None, 4, SUBLANES, H2), lambda j, i: (j, 0, 0, 0)),
                  pl.BlockSpec((None, None, 1, H2), lambda j, i: (nch - 1 - i, j, 0, 0))],
        out_specs=[pl.BlockSpec((Tc, GW), rmap),
                   pl.BlockSpec((None, GW, H2), lambda j, i: (j, 0, 0)),
                   pl.BlockSpec((None, H2, GW), lambda j, i: (j, 0, 0)),
                   pl.BlockSpec((None, 1, H2), lambda j, i: (j, 0, 0)),
                   pl.BlockSpec((1, GW), lambda j, i: (0, j))],
        out_shape=[_sds((L, W), BF16), _sds((nsb, GW, H2), F32), _sds((nsb, H2, GW), F32),
                   _sds((nsb, 1, H2), F32), _sds((1, W), F32)],
        scratch_shapes=[pltpu.VMEM((Tc, H2), F32), pltpu.VMEM((1, H2), F32)],
        compiler_params=_cp(("arbitrary", "arbitrary")),
    )(u, dyg, ypre, s_all, d_skip, bsb_t, csb_t, crv, ss)


def _lru_blockdiag(w_rg, w_ig):
    nb, bs, _ = w_rg.shape
    sbw = bs * LANES // math.gcd(bs, LANES)
    bps = sbw // bs
    nsb = nb // bps

    def bd(w):
        return _block_diag(w.astype(BF16).reshape(nsb, bps, bs, bs))

    return bd(w_rg), bd(w_ig)


def _lru_blockdiag_grad(dwsb, nb, bs):
    return _diag_blocks(dwsb, dwsb.shape[1] // bs).reshape(nb, bs, bs)


def lru_conv_fwd(p, conv_w, conv_b, name):
    L = p.shape[0]
    E = conv_w.shape[1]
    tc = _tile(E, (256, 128))
    noff = E // tc
    kw = conv_w.shape[0]

    def body(xb_ref, w_ref, b_ref, xc_ref, xcb_ref):
        xb = xb_ref[...]
        rows = lax.broadcasted_iota(jnp.int32, xb.shape, 0)
        acc = w_ref[pl.ds(kw - 1, 1), :] * xb + b_ref[...]
        for k in range(kw - 1):
            sh = kw - 1 - k
            acc = acc + w_ref[pl.ds(k, 1), :] * jnp.where(rows >= sh, pltpu.roll(xb, sh, 0), 0.0)
        xc_ref[...] = acc
        xcb_ref[...] = acc.astype(BF16)

    return pl.pallas_call(
        body, name=name, grid=(noff,),
        in_specs=[pl.BlockSpec((L, tc), lambda t: (0, noff + t)),
                  pl.BlockSpec((kw, tc), lambda t: (0, t)), pl.BlockSpec((1, tc), lambda t: (0, t))],
        out_specs=[pl.BlockSpec((L, tc), lambda t: (0, t))] * 2,
        out_shape=[_sds((L, E), F32), _sds((L, E), BF16)],
        compiler_params=_cp(("parallel",)),
    )(p, conv_w, conv_b)


def lru_conv_bwd(d1, d2, d3, p, conv_w, name):
    L = p.shape[0]
    E = conv_w.shape[1]
    tc = _tile(E, (256, 128))
    noff = E // tc
    kw = conv_w.shape[0]

    def body(d1_ref, d2_ref, d3_ref, xb_ref, w_ref, dxb_ref, dw_ref, db_ref):
        dxc = d1_ref[...] + d2_ref[...] + d3_ref[...]
        xb = xb_ref[...]
        rows = lax.broadcasted_iota(jnp.int32, xb.shape, 0)
        db_ref[...] = _colsum(dxc)
        acc = w_ref[pl.ds(kw - 1, 1), :] * dxc
        dw_ref[pl.ds(kw - 1, 1), :] = _colsum(dxc * xb)
        for k in range(kw - 1):
            sh = kw - 1 - k
            dw_ref[pl.ds(k, 1), :] = _colsum(dxc * jnp.where(rows >= sh, pltpu.roll(xb, sh, 0), 0.0))
            acc = acc + w_ref[pl.ds(k, 1), :] * jnp.where(rows < L - sh, pltpu.roll(dxc, L - sh, 0), 0.0)
        dxb_ref[...] = acc.astype(BF16)

    return pl.pallas_call(
        body, name=name, grid=(noff,),
        in_specs=[pl.BlockSpec((L, tc), lambda t: (0, t))] * 3 +
                 [pl.BlockSpec((L, tc), lambda t: (0, noff + t)), pl.BlockSpec((kw, tc), lambda t: (0, t))],
        out_specs=[pl.BlockSpec((L, tc), lambda t: (0, t)), pl.BlockSpec((kw, tc), lambda t: (0, t)),
                   pl.BlockSpec((1, tc), lambda t: (0, t))],
        out_shape=[_sds((L, E), BF16), _sds((kw, E), F32), _sds((1, E), F32)],
        compiler_params=_cp(("parallel",)),
    )(d1, d2, d3, p, conv_w)


def _lru_gates(pr, pi_, brg, big, sp):
    r = _sig(pr + brg)
    ig = _sig(pi_ + big)
    la = -LRU_C * r * sp
    a = jnp.exp(la)
    mult = jnp.sqrt(_neg_expm1(2.0 * la))
    return r, ig, a, mult


def _lru_specs(L, E):
    tc = _tile(E, (LRU_TILE, LANES))
    col = pl.BlockSpec((L, tc), lambda t: (0, t))
    vec = pl.BlockSpec((1, tc), lambda t: (0, t))
    return tc, col, vec


def lru_scan_fwd(pre_r, pre_i, xc, p, b_rg, b_ig, lam, name):
    L, E = xc.shape
    tc, col, vec = _lru_specs(L, E)
    nblk = L // SUBLANES

    def body(pr_ref, pi_ref, xc_ref, gb_ref, brg_ref, big_ref, lam_ref, hs_ref, yv_ref):
        sp = _softplus(-lam_ref[...])
        brg, big = brg_ref[...], big_ref[...]
        rows = lax.broadcasted_iota(jnp.int32, (SUBLANES, tc), 0)

        def blk(k, carry):
            r0 = pl.multiple_of(k * SUBLANES, SUBLANES)
            sl = pl.ds(r0, SUBLANES)
            _, ig, a, mult = _lru_gates(pr_ref[sl, :], pi_ref[sl, :], brg, big, sp)
            b = mult * ig * xc_ref[sl, :]
            for d in (1, 2, 4):
                keep = rows >= d
                b = b + a * jnp.where(keep, pltpu.roll(b, d, 0), 0.0)
                a = a * jnp.where(keep, pltpu.roll(a, d, 0), 1.0)
            h = b + a * carry
            hs_ref[sl, :] = h
            return _row(h, SUBLANES - 1)

        def trip(kt, carry):
            for q in range(SCAN_UNROLL):
                carry = blk(kt * SCAN_UNROLL + q, carry)
            return carry

        lax.fori_loop(0, nblk // SCAN_UNROLL, trip, jnp.zeros((1, tc), F32))
        yv_ref[...] = (hs_ref[...] * _gelu(gb_ref[...])).astype(BF16)

    return pl.pallas_call(
        body, name=name, grid=(E // tc,),
        in_specs=[col, col, col, col, vec, vec, vec],
        out_specs=[col, col], out_shape=[_sds((L, E), F32), _sds((L, E), BF16)],
        compiler_params=_cp(("parallel",)),
    )(pre_r, pre_i, xc, p, b_rg, b_ig, lam)


def lru_scan_bwd(dyv, hs, pre_r, pre_i, xc, p, b_rg, b_ig, lam, name):
    L, E = xc.shape
    tc, col, vec = _lru_specs(L, E)
    nblk = L // SUBLANES

    def body(dyv_ref, hs_ref, pr_ref, pi_ref, xc_ref, gb_ref, brg_ref, big_ref, lam_ref,
             dgb_ref, dpr_ref, dpi_ref, dxc_ref, dbrg_ref, dbig_ref, dlam_ref, t_gb, t_pr, t_pi):
        lam_v = lam_ref[...]
        sp = _softplus(-lam_v)
        brg, big = brg_ref[...], big_ref[...]
        rows = lax.broadcasted_iota(jnp.int32, (SUBLANES, tc), 0)

        def blk(kk, carry):
            gcar, a_next, acc_sp, acc_r, acc_i = carry
            k = nblk - 1 - kk
            r0 = pl.multiple_of(k * SUBLANES, SUBLANES)
            sl = pl.ds(r0, SUBLANES)
            r, ig, a, mult = _lru_gates(pr_ref[sl, :], pi_ref[sl, :], brg, big, sp)
            gbv, hsv, dyvv, xcv = gb_ref[sl, :], hs_ref[sl, :], dyv_ref[sl, :], xc_ref[sl, :]
            t_gb[sl, :] = dyvv * hsv * _gelu_grad(gbv)
            x = dyvv * _gelu(gbv)
            al = jnp.where(rows == SUBLANES - 1, a_next, pltpu.roll(a, SUBLANES - 1, 0))
            for d in (1, 2, 4):
                keep = rows < SUBLANES - d
                x = x + al * jnp.where(keep, pltpu.roll(x, SUBLANES - d, 0), 0.0)
                al = al * jnp.where(keep, pltpu.roll(al, SUBLANES - d, 0), 1.0)
            g = x + al * gcar
            rp = pl.multiple_of(jnp.maximum(k - 1, 0) * SUBLANES, SUBLANES)
            hlast = _row(hs_ref[pl.ds(rp, SUBLANES), :], SUBLANES - 1) * (k > 0).astype(F32)
            hprev = jnp.where(rows == 0, hlast, pltpu.roll(hsv, 1, 0))
            da = g * hprev
            dmult = g * ig * xcv
            dig = g * mult * xcv
            dxc_ref[sl, :] = g * mult * ig
            dla = da * a - dmult * (a * a) / mult
            dpr = dla * (-LRU_C * sp) * r * (1.0 - r)
            dpi = dig * ig * (1.0 - ig)
            t_pr[sl, :] = dpr
            t_pi[sl, :] = dpi
            return (_row(g, 0), _row(a, 0), acc_sp + dla * (-LRU_C * r), acc_r + dpr, acc_i + dpi)

        zero = jnp.zeros((SUBLANES, tc), F32)
        z1 = jnp.zeros((1, tc), F32)
        def trip(kt, carry):
            for q in range(SCAN_UNROLL):
                carry = blk(kt * SCAN_UNROLL + q, carry)
            return carry

        _, _, acc_sp, acc_r, acc_i = lax.fori_loop(0, nblk // SCAN_UNROLL, trip, (z1, z1, zero, zero, zero))
        dgb_ref[...] = t_gb[...].astype(BF16)
        dpr_ref[...] = t_pr[...].astype(BF16)
        dpi_ref[...] = t_pi[...].astype(BF16)
        dbrg_ref[...] = _colsum(acc_r)
        dbig_ref[...] = _colsum(acc_i)
        dlam_ref[...] = -_colsum(acc_sp) * _sig(-lam_v)

    return pl.pallas_call(
        body, name=name, grid=(E // tc,),
        in_specs=[col, col, col, col, col, col, vec, vec, vec],
        out_specs=[col, col, col, col, vec, vec, vec],
        out_shape=[_sds((L, E), BF16), _sds((L, E), BF16), _sds((L, E), BF16), _sds((L, E), F32),
                   _sds((1, E), F32), _sds((1, E), F32), _sds((1, E), F32)],
        scratch_shapes=[pltpu.VMEM((L, tc), F32)] * 3,
        compiler_params=_cp(("parallel",)),
    )(dyv, hs, pre_r, pre_i, xc, p, b_rg, b_ig, lam)


def _place():
    xi, yi, ci = lax.axis_index("x"), lax.axis_index("y"), lax.axis_index("c")
    chips = [(1 - xi, yi), (xi, 1 - yi), (1 - xi, 1 - yi)]
    return xi, yi, ci, chips


_ANY = pl.BlockSpec(memory_space=pl.ANY)


def all_gather_devices(blks, name):
    n = len(blks)

    def body(*refs):
        ins, outs = refs[:n], refs[n:2 * n]
        send_sems, recv_sems, local_sems = refs[2 * n:]
        xi, yi, ci, chips = _place()
        me, sibling = (xi, yi, ci), (xi, yi, 1 - ci)

        def slab(a, px, py, pc):
            return outs[a].at[4 * px + 2 * py + pc]

        def copy(a, k, block, to, src=None):
            return pltpu.make_async_remote_copy(
                src_ref=slab(a, *block) if src is None else src, dst_ref=slab(a, *block),
                send_sem=send_sems.at[7 * a + k], recv_sem=recv_sems.at[7 * a + k], device_id=to,
                device_id_type=MESH)

        mine = [pltpu.make_async_copy(ins[a], slab(a, *me), local_sems.at[a]) for a in range(n)]
        first, passed = [], []
        for a in range(n):
            mine[a].start()
            first.append(copy(a, 0, me, sibling, src=ins[a]))
            first += [copy(a, 1 + j, me, (*chip, ci), src=ins[a]) for j, chip in enumerate(chips)]
        for cp in first:
            cp.start()
        for a in range(n):
            for j, chip in enumerate(chips):
                copy(a, 1 + j, (*chip, ci), me).wait_recv()
                passed.append(copy(a, 4 + j, (*chip, ci), sibling))
                passed[-1].start()
        for a in range(n):
            copy(a, 0, sibling, me).wait_recv()
            for j, chip in enumerate(chips):
                copy(a, 4 + j, (*chip, 1 - ci), me).wait_recv()
        for cp in first + passed:
            cp.wait_send()
        for cp in mine:
            cp.wait()

    return pl.pallas_call(
        body, name=name, in_specs=[_ANY] * n, out_specs=[_ANY] * n,
        out_shape=[_sds((N_DEV,) + b.shape, b.dtype) for b in blks],
        scratch_shapes=[pltpu.SemaphoreType.DMA((7 * n,)), pltpu.SemaphoreType.DMA((7 * n,)),
                        pltpu.SemaphoreType.DMA((n,))],
    )(*blks)


_HBM = pl.BlockSpec(memory_space=pltpu.HBM)
_SEM = pl.BlockSpec(memory_space=pltpu.SEMAPHORE)
_EFFECT = pltpu.SideEffectType.DATAFLOW_SIDE_EFFECTING


def split_start(name, groups, counts, copies_fn):
    flat = [b for g in groups for b in g]
    n, ng = len(flat), len(groups)

    def body(*refs):
        ins, sems, token = refs[:n], refs[n:n + 2 * ng], refs[-1]
        off = 0
        for gi, g in enumerate(groups):
            for cp in copies_fn(ins[off:off + len(g)], [b.shape for b in g], sems[2 * gi], sems[2 * gi + 1]):
                cp.start()
            off += len(g)
        token[...] = jnp.zeros_like(token)

    out_shape = tuple(pltpu.SemaphoreType.DMA((c,)) for c in counts for _ in range(2))
    out_shape += tuple(pltpu.HBM(b.shape, b.dtype) for b in flat) + (_sds((SUBLANES, LANES), F32),)
    outs = pl.pallas_call(
        body, name=name, in_specs=[_HBM] * n, out_shape=out_shape,
        out_specs=tuple([_SEM] * (2 * ng) + [_HBM] * n + [pl.BlockSpec(memory_space=pltpu.VMEM)]),
        input_output_aliases={i: 2 * ng + i for i in range(n)},
        compiler_params=pltpu.CompilerParams(has_side_effects=_EFFECT),
    )(*[pltpu.with_memory_space_constraint(b, pltpu.HBM) for b in flat])
    sems = [(outs[2 * gi], outs[2 * gi + 1]) for gi in range(ng)]
    thru, off = [], 2 * ng
    for g in groups:
        thru.append(list(outs[off:off + len(g)]))
        off += len(g)
    return sems, thru, outs[-1]


def split_wait(name, bufs, sems, copies_fn, after):
    n = len(bufs)
    shapes = [b.shape for b in bufs]

    def body(*refs):
        for cp in copies_fn(refs[:n], shapes, refs[n], refs[n + 1]):
            cp.wait_send()
            cp.wait_recv()

    return list(pl.pallas_call(
        body, name=name, in_specs=[_HBM] * n + [_SEM, _SEM, _ANY],
        out_shape=tuple(pltpu.HBM(b.shape, b.dtype) for b in bufs), out_specs=tuple([_HBM] * n),
        input_output_aliases={i: i for i in range(n)},
        compiler_params=pltpu.CompilerParams(has_side_effects=_EFFECT),
    )(*bufs, sems[0], sems[1], after))


def _gather_copies(bufs, shapes, send_sems, recv_sems):
    xi, yi, ci, chips = _place()
    cps = []
    for a, ref in enumerate(bufs):
        hr = shapes[a][1] // 2
        rows = ref.at[2 * xi + yi, pl.ds(pl.multiple_of(ci * hr, 16), hr), :]
        for j in range(3):
            cps.append(pltpu.make_async_remote_copy(
                src_ref=rows, dst_ref=rows, send_sem=send_sems.at[3 * a + j], recv_sem=recv_sems.at[3 * a + j],
                device_id=(*chips[j], ci), device_id_type=MESH))
    return cps


def _scatter_copies(bufs, shapes, send_sems, recv_sems):
    xi, yi, ci, chips = _place()
    n = len(bufs) // 2
    cps = []
    for a in range(n):
        for j in range(3):
            cps.append(pltpu.make_async_remote_copy(
                src_ref=bufs[a].at[2 * chips[j][0] + chips[j][1]], dst_ref=bufs[n + a].at[j],
                send_sem=send_sems.at[3 * a + j], recv_sem=recv_sems.at[3 * a + j],
                device_id=(*chips[j], ci), device_id_type=MESH))
    return cps


def cast_place(w, layer, chip1, name, after=None):
    _, R, C = w.shape
    tr = _tile(R, tuple(t for t in _TR if t * C <= ROW_TILE_ELEMS) or (16,))
    extra = [] if after is None else [after]

    def body(c_ref, w_ref, *rest):
        rest[-1][...] = w_ref[...].astype(BF16)

    return pl.pallas_call(
        body, name=name,
        grid_spec=pltpu.PrefetchScalarGridSpec(
            num_scalar_prefetch=1, grid=(R // tr,),
            in_specs=[pl.BlockSpec((None, tr, C), lambda i, c: (layer, i, 0))] + [_ANY] * len(extra),
            out_specs=pl.BlockSpec((None, tr, C), lambda i, c: (c[0], i, 0))),
        out_shape=_sds((N_CHIPS, R, C), BF16), compiler_params=_cp(("parallel",)),
    )(chip1, w, *extra)


def _forward_copies(bufs, shapes, send_sems, recv_sems):
    xi, yi, ci, chips = _place()
    cps = []
    for a, ref in enumerate(bufs):
        hr = shapes[a][1] // 2
        for j in range(3):
            rows = ref.at[2 * chips[j][0] + chips[j][1], pl.ds(pl.multiple_of(ci * hr, 16), hr), :]
            cps.append(pltpu.make_async_remote_copy(
                src_ref=rows, dst_ref=rows, send_sem=send_sems.at[3 * a + j], recv_sem=recv_sems.at[3 * a + j],
                device_id=(xi, yi, 1 - ci), device_id_type=MESH))
    return cps


def _halves_copies(bufs, shapes, send_sems, recv_sems):
    xi, yi, ci, _ = _place()
    n = len(bufs) // 2
    cps = []
    for a in range(n):
        hr = shapes[a][1] // 2
        cps.append(pltpu.make_async_remote_copy(
            src_ref=bufs[a].at[:, pl.ds(pl.multiple_of((1 - ci) * hr, 16), hr), :], dst_ref=bufs[n + a],
            send_sem=send_sems.at[a], recv_sem=recv_sems.at[a], device_id=(xi, yi, 1 - ci), device_id_type=MESH))
    return cps


def add_half(g, got, ci, name):
    S, hr, C = got.shape
    tr = _tile(hr, tuple(t for t in _TR if t * C <= ROW_TILE_ELEMS) or (16,))
    nb = hr // tr

    def body(c_ref, g_ref, r_ref, o_ref):
        o_ref[...] = (g_ref[...].astype(F32) + r_ref[...].astype(F32)).astype(BF16)

    return pl.pallas_call(
        body, name=name,
        grid_spec=pltpu.PrefetchScalarGridSpec(
            num_scalar_prefetch=1, grid=(S, nb),
            in_specs=[pl.BlockSpec((None, tr, C), lambda s, i, c: (s, c[0] * nb + i, 0)),
                      pl.BlockSpec((None, tr, C), lambda s, i, c: (s, i, 0))],
            out_specs=pl.BlockSpec((None, tr, C), lambda s, i, c: (s, i, 0))),
        out_shape=_sds((S, hr, C), BF16), compiler_params=_cp(("parallel", "parallel")),
    )(ci, g, got)


def add_chips(part, got, place, name):
    S, hr, C = part.shape
    tr = _tile(hr, tuple(t for t in _TR if t * C <= ROW_TILE_ELEMS) or (16,))
    nb = hr // tr

    def body(c_ref, p_ref, r_ref, o_ref):
        acc = p_ref[...].astype(F32)
        for j in range(3):
            acc = acc + r_ref[j].astype(F32)
        o_ref[...] = acc

    return pl.pallas_call(
        body, name=name,
        grid_spec=pltpu.PrefetchScalarGridSpec(
            num_scalar_prefetch=1, grid=(nb,),
            in_specs=[pl.BlockSpec((None, tr, C), lambda i, c: (c[0], i, 0)),
                      pl.BlockSpec((3, tr, C), lambda i, c: (0, i, 0))],
            out_specs=pl.BlockSpec((tr, C), lambda i, c: (c[1] * nb + i, 0))),
        out_shape=_sds((2 * hr, C), F32), compiler_params=_cp(("parallel",)),
    )(place, part, got)


def join_halves(bufs, name):
    n = len(bufs)

    def body(*refs):
        ins, outs = refs[:n], refs[n:2 * n]
        send_sems, recv_sems = refs[2 * n:]
        xi, yi, ci, _ = _place()

        def copy(ref, a, h):
            hr = bufs[a].shape[0] // 2
            rows = pl.ds(pl.multiple_of(h * hr, 8), hr)
            return pltpu.make_async_remote_copy(
                src_ref=ref[a].at[rows, :], dst_ref=outs[a].at[rows, :], send_sem=send_sems.at[a],
                recv_sem=recv_sems.at[a], device_id=(xi, yi, 1 - ci), device_id_type=MESH)

        sends = [copy(ins, a, ci) for a in range(n)]
        for cp in sends:
            cp.start()
        for a in range(n):
            copy(outs, a, 1 - ci).wait_recv()
        for cp in sends:
            cp.wait_send()

    return pl.pallas_call(
        body, name=name, in_specs=[_ANY] * n, out_specs=[_ANY] * n,
        out_shape=[_sds(b.shape, b.dtype) for b in bufs], input_output_aliases={i: i for i in range(n)},
        scratch_shapes=[pltpu.SemaphoreType.DMA((n,)), pltpu.SemaphoreType.DMA((n,))],
    )(*bufs)


def reduce_scatter_start(grads, tag):
    lands = [lax.empty((N_CHIPS, g.shape[1] // 2, g.shape[2]), g.dtype) for g in grads]
    sems, thru, token = split_start(f"rs_d2d_start_{tag}", [list(grads) + lands], [len(grads)], _halves_copies)
    return (sems[0], thru[0]), token


def reduce_scatter_relay(state, ci1, after, tag):
    sems, bufs = state
    bufs = split_wait(f"rs_d2d_wait_{tag}", bufs, sems, _halves_copies, after)
    n = len(bufs) // 2
    parts = [add_half(bufs[a], bufs[n + a], ci1, f"rs_add_half_{tag}{a}") for a in range(n)]
    lands = [lax.empty((3,) + q.shape[1:], q.dtype) for q in parts]
    sems, thru, token = split_start(f"rs_ici_start_{tag}", [parts + lands], [3 * n], _scatter_copies)
    return (sems[0], thru[0]), token


def reduce_scatter_finish(state, place, after, tag):
    sems, bufs = state
    bufs = split_wait(f"rs_ici_wait_{tag}", bufs, sems, _scatter_copies, after)
    n = len(bufs) // 2
    return [add_chips(bufs[a], bufs[n + a], place, f"rs_add_chips_{tag}{a}") for a in range(n)]


def _pack(parts, width):
    flat = jnp.concatenate([p.reshape(-1).astype(F32) for p in parts])
    per = SUBLANES * width
    total = -(-flat.shape[0] // per) * per
    flat = jnp.pad(flat, (0, total - flat.shape[0]))
    return flat.reshape(total // width, width)


def _unpack(flat, shapes):
    out, off = [], 0
    for s in shapes:
        n = math.prod(s)
        out.append(flat[off:off + n].reshape(s))
        off += n
    return out


_W_NAMES = ['norm_g', 'w_ada', 'b_ada', 's5_w_in', 's5_lam_re', 's5_lam_im', 's5_log_dt', 's5_b_re', 's5_b_im',
            's5_c_re', 's5_c_im', 's5_d', 's5_w_glu', 'lru_w_in', 'lru_conv_w', 'lru_conv_b', 'lru_w_rg',
            'lru_b_rg', 'lru_w_ig', 'lru_b_ig', 'lru_lam', 'lru_w_out', 'ffn_w_gu', 'ffn_w_down', 'final_g']
_BIG = ('w_ada', 's5_w_in', 's5_w_glu', 'lru_w_in', 'lru_w_out', 'ffn_w_gu', 'ffn_w_down')
_MID = ('s5_b_re', 's5_b_im', 's5_c_re', 's5_c_im', 'lru_w_rg', 'lru_w_ig')


def _ffn_fwd(x, gu, w_down, gate, tag):
    act = swiglu_fwd(gu, f"{tag}_act")
    z = mm_nn(act, w_down, name=f"{tag}_down")
    return res_gate_fwd(x, z, gate, f"{tag}_res"), (gu, act, z)


def _ffn_bwd(dx, h, saved, w_gu, w_down, gate, tag):
    gu, act, z = saved
    dz, dgate = res_gate_bwd(dx, z, gate, f"{tag}_res_bwd")
    dact = mm_nt(dz, w_down, name=f"{tag}_dact")
    dw_down = mm_tn(act, dz, name=f"{tag}_dwdown", out_dtype=BF16)
    dgu = swiglu_bwd(dact, gu, f"{tag}_act_bwd")
    dh = mm_nt(dgu, w_gu, name=f"{tag}_dh", bmode="cols")
    dw_gu = mm_tn(h, dgu, name=f"{tag}_dwgu", out_dtype=BF16, omode="cols", groups=N_CHIPS)
    return dh, dgate, dw_gu, dw_down.reshape((N_CHIPS, -1) + dw_down.shape[1:])


def _step(p):
    xi, yi, ci = lax.axis_index("x"), lax.axis_index("y"), lax.axis_index("c")
    chip = 2 * xi + yi
    me = 2 * chip + ci
    ci1 = jnp.reshape(ci, (1,)).astype(jnp.int32)
    chip1 = jnp.reshape(chip, (1,)).astype(jnp.int32)
    place2 = jnp.stack([chip, ci]).astype(jnp.int32)

    x0 = p['x'][0]
    tgt = p['loss_target'][0]
    L, D = x0.shape
    Dq = D // N_CHIPS
    depth = p['w_ada'].shape[0]
    E = p['lru_lam'].shape[1] * N_CHIPS
    Eq = E // N_CHIPS
    kw = p['lru_conv_w'].shape[1]
    Nq = p['w_ada'].shape[2]
    _, G, P, Cg = p['s5_b_re'].shape
    nb, bs = p['lru_w_rg'].shape[1], p['lru_w_rg'].shape[2]

    pay = _pack([p['c'], p['norm_g'], p['lru_conv_w'], p['lru_conv_b'], p['lru_b_rg'], p['lru_b_ig'],
                 p['lru_lam']], 1024)
    g1 = all_gather_devices([pay], "gather_small_params")[0].reshape(N_DEV, -1)
    c_all = g1[:, :D]
    per_chip = g1[0::2]
    sizes = [(depth, 2, Dq), (kw, Eq), (Eq,), (Eq,), (Eq,), (Eq,)]
    offs = D
    pieces = []
    for s in sizes:
        nel = math.prod(s)
        pieces.append(per_chip[:, offs:offs + nel].reshape((N_CHIPS,) + s))
        offs += nel
    norm_g = jnp.moveaxis(pieces[0], 0, 2).reshape(depth, 2, D)
    conv_w = jnp.moveaxis(pieces[1], 0, 1).reshape(kw, E)
    conv_b, b_rg, b_ig, lam = [q.reshape(1, E) for q in pieces[2:]]

    cond = silu_rows(jnp.pad(c_all, ((0, 16 - N_DEV), (0, 0))), "cond_silu")
    cond_rep = jnp.concatenate([cond] * depth, axis=1)
    mod_part = mm_nn(cond_rep, p['w_ada'], name="mod_proj", bmode="batch")[:N_DEV]
    g2 = all_gather_devices([mod_part], "gather_mod")[0][0::2]
    mine = lax.dynamic_index_in_dim(g2, me, axis=1, keepdims=False).reshape(N_CHIPS, depth, Nq)
    mod = jnp.moveaxis(mine, 0, 1).reshape(depth, N_CHIPS * Nq) + p['b_ada']
    mods = [[mod[i:i + 1, k * D:(k + 1) * D] for k in range(N_MOD)] for i in range(depth)]

    def place(key, layer, tag, after=None):
        return cast_place(p[key], layer, chip1, f"place_{tag}", after=after)

    first = [place('s5_w_in', 0, "s5_in", after=mod), place('s5_w_glu', 0, "s5_glu")]
    sems_a, bufs_a, tok_a = split_start("gather_ici_start_s5", [first], [3 * len(first)], _gather_copies)
    others = [[place('ffn_w_gu', 0, "gu0", after=tok_a), place('ffn_w_down', 0, "down0")],
              [place('lru_w_in', 0, "lru_in"), place('lru_w_out', 0, "lru_out")],
              [place('ffn_w_gu', 1, "gu1"), place('ffn_w_down', 1, "down1")]]
    sems_b, bufs_b, tok_b = split_start("gather_ici_start_rest", others, [3 * len(g) for g in others], _gather_copies)
    wsems, wbufs = sems_a + sems_b, bufs_a + bufs_b

    def landed(gi, after, tag):
        bufs = split_wait(f"gather_ici_wait_{tag}", wbufs[gi], wsems[gi], _gather_copies, after)
        sems, thru, tok = split_start(f"gather_d2d_start_{tag}", [bufs], [3 * len(bufs)], _forward_copies)
        return (sems[0], thru[0]), tok

    def weights(state, after, tag):
        return split_wait(f"gather_d2d_wait_{tag}", state[1], state[0], _forward_copies, after)

    s5_small = (p['s5_lam_re'][0], p['s5_lam_im'][0], p['s5_log_dt'][0], p['s5_b_re'][0], p['s5_b_im'][0])
    (ab_re, ab_im, bb_re, bb_im), s5_disc_vjp = jax.vjp(_s5_discretize, *s5_small)
    bsb, csb = _s5_blockdiag(bb_re, bb_im, p['s5_c_re'][0], p['s5_c_im'][0])
    cf, crv = _s5_scan_consts(ab_re, ab_im)
    wsb_rg, wsb_ig = [w.astype(BF16) for w in _lru_blockdiag(p['lru_w_rg'][0], p['lru_w_ig'][0])]
    nsb_lru = wsb_rg.shape[0]

    sh1, sc1, gt1, sh2, sc2, gt2 = mods[0]
    st, tok = landed(0, tok_b, "s5")
    s5_w_in, s5_w_glu = weights(st, tok, "s5")
    s5_w_in = s5_w_in.reshape(-1, D)
    h0 = norm_mod_fwd(x0, norm_g[0, 0:1], sc1, sh1, "l0_norm1")
    u = mm_nn(h0, s5_w_in, name="s5_in")
    ypre, yg, s_all, ss = s5_scan_fwd(u, p['s5_d'], bsb, csb, cf, "s5_scan")
    v = mm_nn(yg, s5_w_glu, name="s5_glu", bmode="cols")
    st, tok = landed(1, v, "ffn0")
    x1 = glu_res_fwd(x0, v, gt1, "s5_res")
    h1 = norm_mod_fwd(x1, norm_g[0, 1:2], sc2, sh2, "l0_norm2", after=tok)
    w_gu0, w_down0 = weights(st, h1, "ffn0")
    w_down0 = w_down0.reshape(-1, D)
    gu0 = mm_nn(h1, w_gu0, name="ffn0_gu", out_dtype=BF16, bmode="cols")
    st, tok = landed(2, gu0, "lru")
    x2, ffn0 = _ffn_fwd(x1, gu0, w_down0, gt2, "ffn0")

    sh1b, sc1b, gt1b, sh2b, sc2b, gt2b = mods[1]
    h2 = norm_mod_fwd(x2, norm_g[1, 0:1], sc1b, sh1b, "l1_norm1", after=tok)
    lru_w_in, lru_w_out = weights(st, h2, "lru")
    lru_w_out = lru_w_out.reshape(-1, D)
    pq = mm_nn(h2, lru_w_in, name="lru_in", bmode="cols")
    xc, xcb = lru_conv_fwd(pq, conv_w, conv_b, "lru_conv")
    pre_r = mm_nn(xcb, wsb_rg, name="lru_gate_r", bmode="batch")
    pre_i = mm_nn(xcb, wsb_ig, name="lru_gate_i", bmode="batch")
    hs, yv = lru_scan_fwd(pre_r, pre_i, xc, pq, b_rg, b_ig, lam, "lru_scan")
    st, tok = landed(3, hs, "ffn1")
    mix = mm_nn(yv, lru_w_out, name="lru_out")
    x3 = res_gate_fwd(x2, mix, gt1b, "lru_res")
    h3 = norm_mod_fwd(x3, norm_g[1, 1:2], sc2b, sh2b, "l1_norm2", after=tok)
    w_gu1, w_down1 = weights(st, h3, "ffn1")
    w_down1 = w_down1.reshape(-1, D)
    gu1 = mm_nn(h3, w_gu1, name="ffn1_gu", out_dtype=BF16, bmode="cols")
    x4, ffn1 = _ffn_fwd(x3, gu1, w_down1, gt2b, "ffn1")

    fg = p['final_g'].reshape(1, D)
    dx4, loss_blk, dfinal_g = final_loss(x4, fg, tgt, "final_loss")
    loss = lax.psum(loss_blk[0, 0], ("x", "y", "c"))

    def rows4(g):
        return g.reshape((N_CHIPS, -1) + g.shape[1:])

    dh3, dgt2b, dw_gu1, dw_down1 = _ffn_bwd(dx4, h3, ffn1, w_gu1, w_down1, gt2b, "ffn1")
    d2d_ffn1, tok = reduce_scatter_start([dw_gu1, dw_down1], "ffn1")
    dx3, dgn11, dsc2b, dsh2b = norm_mod_bwd(dh3, x3, norm_g[1, 1:2], sc2b, dx4, "l1_norm2_bwd", after=tok)

    dmix, dgt1b = res_gate_bwd(dx3, mix, gt1b, "lru_res_bwd")
    dyv = mm_nt(dmix, lru_w_out, name="lru_dyv")
    dw_out = mm_tn(yv, dmix, name="lru_dwout", out_dtype=BF16)
    rs_ffn1, tok = reduce_scatter_relay(d2d_ffn1, ci1, dyv, "ffn1")
    dgb, dpre_r, dpre_i, dxc1, db_rg, db_ig, dlam = lru_scan_bwd(dyv, hs, pre_r, pre_i, xc, pq, b_rg, b_ig, lam,
                                                                "lru_scan_bwd")
    dxc2 = mm_nt(dpre_r, wsb_rg, name="lru_dxc_r", bmode="batch", after=tok)
    dxc3 = mm_nt(dpre_i, wsb_ig, name="lru_dxc_i", bmode="batch")
    dwsb_rg = mm_tn(xcb, dpre_r, name="lru_dwgate_r", omode="batch", groups=nsb_lru)
    dwsb_ig = mm_tn(xcb, dpre_i, name="lru_dwgate_i", omode="batch", groups=nsb_lru)
    dxb, dconv_w, dconv_b = lru_conv_bwd(dxc1, dxc2, dxc3, pq, conv_w, "lru_conv_bwd")
    dpq = jnp.concatenate([dgb, dxb], axis=1)
    dh2 = mm_nt(dpq, lru_w_in, name="lru_dh", bmode="cols")
    dw_lru_in = mm_tn(h2, dpq, name="lru_dwin", out_dtype=BF16, omode="cols", groups=N_CHIPS)
    d2d_lru, tok = reduce_scatter_start([dw_lru_in, rows4(dw_out)], "lru")
    dx2, dgn10, dsc1b, dsh1b = norm_mod_bwd(dh2, x2, norm_g[1, 0:1], sc1b, dx3, "l1_norm1_bwd", after=tok)

    dh1, dgt2, dw_gu0, dw_down0 = _ffn_bwd(dx2, h1, ffn0, w_gu0, w_down0, gt2, "ffn0")
    rs_lru, tok_lru = reduce_scatter_relay(d2d_lru, ci1, dh1, "lru")
    d2d_ffn0, tok = reduce_scatter_start([dw_gu0, dw_down0], "ffn0")
    dx1, dgn01, dsc2, dsh2 = norm_mod_bwd(dh1, x1, norm_g[0, 1:2], sc2, dx2, "l0_norm2_bwd", after=tok)

    dv, dgt1 = glu_res_bwd(dx1, v, gt1, "s5_res_bwd", after=tok_lru)
    dyg = mm_nt(dv, s5_w_glu, name="s5_dyg", bmode="cols")
    dw_glu = mm_tn(yg, dv, name="s5_dwglu", out_dtype=BF16, omode="cols", groups=N_CHIPS)
    rs_ffn0, tok = reduce_scatter_relay(d2d_ffn0, ci1, dyg, "ffn0")
    du, dbsb, dcsb, da, dd = s5_scan_bwd(u, dyg, ypre, s_all, p['s5_d'], bsb, csb, crv, ss, "s5_scan_bwd")
    dh0 = mm_nt(du, s5_w_in, name="s5_dh", after=tok)
    dw_s5_in = mm_tn(h0, du, name="s5_dwin", out_dtype=BF16)
    d2d_s5, tok = reduce_scatter_start([rows4(dw_s5_in), dw_glu], "s5")
    grad_x, dgn00, dsc1, dsh1 = norm_mod_bwd(dh0, x0, norm_g[0, 0:1], sc1, dx1, "l0_norm1_bwd", after=tok)

    dmod = jnp.concatenate([jnp.concatenate([dsh1, dsc1, dgt1, dsh2, dsc2, dgt2], axis=1),
                            jnp.concatenate([dsh1b, dsc1b, dgt1b, dsh2b, dsc2b, dgt2b], axis=1)], axis=0)
    dnorm_g = jnp.stack([jnp.concatenate([dgn00, dgn01]), jnp.concatenate([dgn10, dgn11])])
    dbb_re, dbb_im, dc_re, dc_im = _s5_blockdiag_grads(dbsb, dcsb, P, Cg)
    H = S5_SB_GROUPS * P
    da_re, da_im = da[:, 0, :H].reshape(G, P), da[:, 0, H:].reshape(G, P)
    dw_rg, dw_ig = _lru_blockdiag_grad(dwsb_rg, nb, bs), _lru_blockdiag_grad(dwsb_ig, nb, bs)
    small = [dmod, dnorm_g, da_re, da_im, dd, dconv_w, dconv_b, db_rg, db_ig, dlam, dfinal_g]
    small_shapes = [s.shape for s in small]
    payload = _pack(small, 1024)
    mid = [dbb_re, dbb_im, dc_re, dc_im, dw_rg, dw_ig]
    mid_shapes = [s.shape for s in mid]
    gathered = all_gather_devices([payload] + [s.reshape(s.shape[0], -1).astype(BF16) for s in mid],
                                  "gather_small_grads")
    gathered_small = gathered[0]
    rs_s5, tok_s5 = reduce_scatter_relay(d2d_s5, ci1, gathered_small, "s5")
    total = sum_devices(gathered_small, "sum_small_grads").reshape(-1)
    (s_dmod, s_norm_g, s_da_re, s_da_im, s_dd, s_conv_w, s_conv_b, s_b_rg, s_b_ig, s_lam,
     s_final_g) = _unpack(total, small_shapes)
    s_dbb_re, s_dbb_im, s_dc_re, s_dc_im, s_dw_rg, s_dw_ig = [
        sum_devices(g, f"sum_mid_grads_{i}").reshape(s) for i, (g, s) in enumerate(zip(gathered[1:], mid_shapes))]
    g_lam_re, g_lam_im, g_log_dt, g_b_re, g_b_im = s5_disc_vjp((s_da_re, s_da_im, s_dbb_re, s_dbb_im))

    npay = payload.shape[0] * payload.shape[1]
    dmod_all = gathered_small.reshape(N_DEV, npay)[:, :depth * N_MOD * D].reshape(N_DEV, depth, N_CHIPS, Nq)
    dmod_mine = lax.dynamic_index_in_dim(dmod_all, chip, axis=2, keepdims=False).reshape(N_DEV, depth * Nq)
    dmod_mine = jnp.pad(dmod_mine, ((0, 16 - N_DEV), (0, 0)))
    g_w_ada = mm_tn(cond_rep, dmod_mine, name="w_ada_grad", omode="batch", groups=depth)

    def cols(full, width):
        return lax.dynamic_slice_in_dim(full, chip * width, width, axis=full.ndim - 1)

    grads = {
        'norm_g': cols(s_norm_g, Dq), 'w_ada': g_w_ada, 'b_ada': s_dmod,
        's5_lam_re': g_lam_re, 's5_lam_im': g_lam_im, 's5_log_dt': g_log_dt, 's5_b_re': g_b_re, 's5_b_im': g_b_im,
        's5_c_re': s_dc_re, 's5_c_im': s_dc_im, 's5_d': s_dd, 'lru_conv_w': cols(s_conv_w, Eq),
        'lru_conv_b': cols(s_conv_b, Eq), 'lru_w_rg': s_dw_rg, 'lru_b_rg': cols(s_b_rg, Eq),
        'lru_w_ig': s_dw_ig, 'lru_b_ig': cols(s_b_ig, Eq), 'lru_lam': cols(s_lam, Eq), 'final_g': s_final_g,
    }
    grads = {k: g.reshape(p[k].shape) for k, g in grads.items()}

    delta, new_m, new_v = {}, {}, {}

    def adamw_2d(k, rows, after=None):
        w2 = p[k].reshape(rows, -1)
        outs = adamw(w2, grads[k].reshape(w2.shape), p['m_' + k].reshape(w2.shape), p['v_' + k].reshape(w2.shape),
                     f"adamw_{k}", after=after)
        delta[k], new_m[k], new_v[k] = [o.reshape(p[k].shape) for o in outs]

    adamw_2d('w_ada', depth * D, after=tok_s5)
    for k in _MID:
        adamw_2d(k, p[k].shape[1])
    rest = [k for k in _W_NAMES if k not in _BIG + _MID]
    shapes = [p[k].shape for k in rest]
    packed = [_pack([src[pre_ + k] if pre_ else src[k] for k in rest], 1024)
              for src, pre_ in ((p, ''), (grads, ''), (p, 'm_'), (p, 'v_'))]
    outs = adamw(*packed, "adamw_small")
    for dst, o in zip((delta, new_m, new_v), outs):
        for k, val in zip(rest, _unpack(o.reshape(-1), shapes)):
            dst[k] = val

    done = delta['w_ada']
    halves = []
    for state, tag in ((rs_ffn1, "ffn1"), (rs_lru, "lru"), (rs_ffn0, "ffn0"), (rs_s5, "s5")):
        halves += reduce_scatter_finish(state, place2, done, tag)
    g_gu1, g_down1, g_lru_in, g_lru_out, g_gu0, g_down0, g_s5_in, g_s5_glu = join_halves(halves, "rs_join_halves")
    grads.update({'s5_w_in': g_s5_in[None], 's5_w_glu': g_s5_glu[None], 'lru_w_in': g_lru_in[None],
                  'lru_w_out': g_lru_out[None], 'ffn_w_gu': jnp.stack([g_gu0, g_gu1]),
                  'ffn_w_down': jnp.stack([g_down0, g_down1])})
    for k in _BIG[1:]:
        adamw_2d(k, math.prod(p[k].shape[:-1]))

    return (loss, grad_x[None], *[grads[k] for k in _W_NAMES], *[delta[k] for k in _W_NAMES],
            *[new_m[k] for k in _W_NAMES], *[new_v[k] for k in _W_NAMES])


_IN_NAMES = (['x', 'c'] + _W_NAMES + ['loss_target'] + ['m_' + k for k in _W_NAMES] + ['v_' + k for k in _W_NAMES])


def kernel(x, c, norm_g, w_ada, b_ada, s5_w_in, s5_lam_re, s5_lam_im, s5_log_dt, s5_b_re, s5_b_im, s5_c_re, s5_c_im, s5_d, s5_w_glu, lru_w_in, lru_conv_w, lru_conv_b, lru_w_rg, lru_b_rg, lru_w_ig, lru_b_ig, lru_lam, lru_w_out, ffn_w_gu, ffn_w_down, final_g, loss_target, m_norm_g, m_w_ada, m_b_ada, m_s5_w_in, m_s5_lam_re, m_s5_lam_im, m_s5_log_dt, m_s5_b_re, m_s5_b_im, m_s5_c_re, m_s5_c_im, m_s5_d, m_s5_w_glu, m_lru_w_in, m_lru_conv_w, m_lru_conv_b, m_lru_w_rg, m_lru_b_rg, m_lru_w_ig, m_lru_b_ig, m_lru_lam, m_lru_w_out, m_ffn_w_gu, m_ffn_w_down, m_final_g, v_norm_g, v_w_ada, v_b_ada, v_s5_w_in, v_s5_lam_re, v_s5_lam_im, v_s5_log_dt, v_s5_b_re, v_s5_b_im, v_s5_c_re, v_s5_c_im, v_s5_d, v_s5_w_glu, v_lru_w_in, v_lru_conv_w, v_lru_conv_b, v_lru_w_rg, v_lru_b_rg, v_lru_w_ig, v_lru_b_ig, v_lru_lam, v_lru_w_out, v_ffn_w_gu, v_ffn_w_down, v_final_g):
    args = (x, c, norm_g, w_ada, b_ada, s5_w_in, s5_lam_re, s5_lam_im, s5_log_dt, s5_b_re, s5_b_im, s5_c_re, s5_c_im, s5_d, s5_w_glu, lru_w_in, lru_conv_w, lru_conv_b, lru_w_rg, lru_b_rg, lru_w_ig, lru_b_ig, lru_lam, lru_w_out, ffn_w_gu, ffn_w_down, final_g, loss_target, m_norm_g, m_w_ada, m_b_ada, m_s5_w_in, m_s5_lam_re, m_s5_lam_im, m_s5_log_dt, m_s5_b_re, m_s5_b_im, m_s5_c_re, m_s5_c_im, m_s5_d, m_s5_w_glu, m_lru_w_in, m_lru_conv_w, m_lru_conv_b, m_lru_w_rg, m_lru_b_rg, m_lru_w_ig, m_lru_b_ig, m_lru_lam, m_lru_w_out, m_ffn_w_gu, m_ffn_w_down, m_final_g, v_norm_g, v_w_ada, v_b_ada, v_s5_w_in, v_s5_lam_re, v_s5_lam_im, v_s5_log_dt, v_s5_b_re, v_s5_b_im, v_s5_c_re, v_s5_c_im, v_s5_d, v_s5_w_glu, v_lru_w_in, v_lru_conv_w, v_lru_conv_b, v_lru_w_rg, v_lru_b_rg, v_lru_w_ig, v_lru_b_ig, v_lru_lam, v_lru_w_out, v_ffn_w_gu, v_ffn_w_down, v_final_g)
    return _step(dict(zip(_IN_NAMES, args)))
```

```python
import functools
import math

import jax
import jax.numpy as jnp
from jax import lax
from jax.experimental import pallas as pl
from jax.experimental.pallas import tpu as pltpu

F32 = jnp.float32
BF16 = jnp.bfloat16
MESH = pl.DeviceIdType.MESH

EPS = 1e-6
LRU_C = 8.0
N_MOD = 6
ADAM_LR = 0.001
ADAM_B1 = 0.9
ADAM_B2 = 0.999
ADAM_EPS = 1e-08
ADAM_WD = 0.01
ADAM_STEP = 10

N_CHIPS = 4
N_DEV = 8
SUBLANES = 8
LANES = 128
S5_SB_GROUPS = 8
V7X_VMEM_LIMIT = 48 * 1024 * 1024
ROW_TILE_ELEMS = 512 * 1024
ROW_CALL_ELEMS = 4 * 1024 * 1024
SCAN_UNROLL = 2
LRU_TILE = 256

_TM = (1024, 1408, 512, 256, 128, 64, 32, 16, 8)
_TN = (1024, 1408, 512, 384, 256, 128)
_TK = (1024, 1408, 512, 256, 128)
_TR = (256, 128, 64, 32, 16, 8)

_GELU_K0 = math.sqrt(2.0 / math.pi)
_GELU_K1 = 0.044715


def _tile(n, cands):
    for t in cands:
        if n % t == 0:
            return t
    return n


def _cp(sem=None):
    return pltpu.CompilerParams(dimension_semantics=sem, vmem_limit_bytes=V7X_VMEM_LIMIT)


def _sds(shape, dtype):
    return jax.ShapeDtypeStruct(shape, dtype)


def _sig(x):
    return 1.0 / (1.0 + jnp.exp(-x))


def _gelu(x):
    t = jnp.tanh(_GELU_K0 * (x + _GELU_K1 * x * x * x))
    return 0.5 * x * (1.0 + t)


def _gelu_grad(x):
    x2 = x * x
    t = jnp.tanh(_GELU_K0 * (x + _GELU_K1 * x * x2))
    return 0.5 * (1.0 + t) + 0.5 * x * (1.0 - t * t) * _GELU_K0 * (1.0 + 3.0 * _GELU_K1 * x2)


def _softplus(z):
    return jnp.maximum(z, 0.0) + jnp.log(1.0 + jnp.exp(-jnp.abs(z)))


def _neg_expm1(x):
    series = -x * (1.0 + x * (0.5 + x * (1.0 / 6.0 + x * (1.0 / 24.0))))
    return jnp.where(x > -0.05, series, 1.0 - jnp.exp(x))


def _row(x, r):
    return x[r:r + 1, :]


def _colsum(x):
    return jnp.sum(x, axis=0, keepdims=True)


_NN = (((1,), (0,)), ((), ()))
_NT = (((1,), (1,)), ((), ()))
_TN_DIMS = (((0,), (0,)), ((), ()))


def _mm_call(name, a, b, a_spec, b_spec, o_spec, grid, out_shape, acc_shape, dims, after=None):
    nk = grid[-1]
    kaxis = len(grid) - 1
    extra = [] if after is None else [after]

    def body(a_ref, b_ref, *rest):
        o_ref, acc_ref = rest[-2], rest[-1]
        k = pl.program_id(kaxis)

        def prod():
            return lax.dot_general(a_ref[...].astype(BF16), b_ref[...].astype(BF16), dims,
                                   preferred_element_type=F32)

        if nk == 1:
            o_ref[...] = prod().astype(o_ref.dtype)
            return

        @pl.when(k == 0)
        def _():
            acc_ref[...] = prod()

        if nk > 2:
            @pl.when(jnp.logical_and(k > 0, k < nk - 1))
            def _():
                acc_ref[...] += prod()

        @pl.when(k == nk - 1)
        def _():
            o_ref[...] = (acc_ref[...] + prod()).astype(o_ref.dtype)

    return pl.pallas_call(
        body, name=name, grid=grid, in_specs=[a_spec, b_spec] + [pl.BlockSpec(memory_space=pl.ANY)] * len(extra),
        out_specs=o_spec, out_shape=out_shape, scratch_shapes=[pltpu.VMEM(acc_shape, F32)],
        compiler_params=_cp(("parallel", "parallel", "parallel", "arbitrary")),
    )(a, b, *extra)


def mm_nn(a, b, *, name, out_dtype=F32, bmode="plain"):
    M = a.shape[0]
    if bmode == "plain":
        G, S = 1, 1
        K, Nc = b.shape
    elif bmode == "cols":
        G = 1
        S, K, Nc = b.shape
    else:
        S = 1
        G, K, Nc = b.shape
    tm, tn, tk = _tile(M, _TM), _tile(Nc, _TN), _tile(K, _TK)
    nkb, nnb = K // tk, Nc // tn
    ncol = S * nnb
    grid = (G, M // tm, ncol, nkb)
    a_spec = pl.BlockSpec((tm, tk), lambda g, i, j, k: (i, g * nkb + k))
    if bmode == "plain":
        b_spec = pl.BlockSpec((tk, tn), lambda g, i, j, k: (k, j))
    elif bmode == "cols":
        b_spec = pl.BlockSpec((None, tk, tn), lambda g, i, j, k: (j // nnb, k, j % nnb))
    else:
        b_spec = pl.BlockSpec((None, tk, tn), lambda g, i, j, k: (g, k, j))
    o_spec = pl.BlockSpec((tm, tn), lambda g, i, j, k: (i, g * ncol + j))
    return _mm_call(name, a, b, a_spec, b_spec, o_spec, grid, _sds((M, G * S * Nc), out_dtype), (tm, tn), _NN)


def mm_nt(a, b, *, name, out_dtype=F32, bmode="plain", after=None):
    M = a.shape[0]
    if bmode == "plain":
        G, S = 1, 1
        Ko, Nc = b.shape
    elif bmode == "cols":
        G = 1
        S, Ko, Nc = b.shape
    else:
        S = 1
        G, Ko, Nc = b.shape
    tm, to, tc = _tile(M, _TM), _tile(Ko, _TN), _tile(Nc, _TK)
    npc = Nc // tc
    nc = S * npc
    nob = Ko // to
    grid = (G, M // tm, nob, nc)
    a_spec = pl.BlockSpec((tm, tc), lambda g, i, j, n: (i, g * nc + n))
    if bmode == "plain":
        b_spec = pl.BlockSpec((to, tc), lambda g, i, j, n: (j, n))
    elif bmode == "cols":
        b_spec = pl.BlockSpec((None, to, tc), lambda g, i, j, n: (n // npc, j, n % npc))
    else:
        b_spec = pl.BlockSpec((None, to, tc), lambda g, i, j, n: (g, j, n))
    o_spec = pl.BlockSpec((tm, to), lambda g, i, j, n: (i, g * nob + j))
    return _mm_call(name, a, b, a_spec, b_spec, o_spec, grid, _sds((M, G * Ko), out_dtype), (tm, to), _NT,
                    after=after)


def mm_tn(a, b, *, name, out_dtype=F32, omode="plain", groups=1):
    L = a.shape[0]
    G = groups if omode == "batch" else 1
    S = groups if omode == "cols" else 1
    Mo, N = a.shape[1] // G, b.shape[1] // G
    Nc = N // S
    tm, tn, tl = _tile(Mo, _TM), _tile(Nc, _TN), _tile(L, _TK)
    nmb, nnb = Mo // tm, N // tn
    npj = Nc // tn
    grid = (G, nmb, nnb, L // tl)
    a_spec = pl.BlockSpec((tl, tm), lambda g, i, j, l: (l, g * nmb + i))
    b_spec = pl.BlockSpec((tl, tn), lambda g, i, j, l: (l, g * nnb + j))
    if omode == "plain":
        o_spec = pl.BlockSpec((tm, tn), lambda g, i, j, l: (i, j))
        oshape = (Mo, N)
    elif omode == "cols":
        o_spec = pl.BlockSpec((None, tm, tn), lambda g, i, j, l: (j // npj, i, j % npj))
        oshape = (S, Mo, Nc)
    else:
        o_spec = pl.BlockSpec((None, tm, tn), lambda g, i, j, l: (g, i, j))
        oshape = (G, Mo, N)
    return _mm_call(name, a, b, a_spec, b_spec, o_spec, grid, _sds(oshape, out_dtype), (tm, tn), _TN_DIMS)


def _row_call(name, body, row_ins, vec_ins, row_outs, acc_outs=(), after=None):
    if after is not None:
        n_in = len(row_ins) + len(vec_ins)
        inner = body

        def body(*refs):
            inner(*refs[:n_in], *refs[n_in + 1:])

        return _row_call_impl(name, body, row_ins, vec_ins, row_outs, acc_outs, [after])
    return _row_call_impl(name, body, row_ins, vec_ins, row_outs, acc_outs, [])


def _row_call_impl(name, body, row_ins, vec_ins, row_outs, acc_outs, extra):
    L = row_ins[0].shape[0]
    wmax = max([a.shape[1] for a in row_ins] + [w for w, _ in row_outs])
    narr = len(row_ins) + len(row_outs)
    tr = _tile(L, tuple(t for t in _TR if t * wmax * narr <= ROW_CALL_ELEMS) or (SUBLANES,))
    in_specs = [pl.BlockSpec((tr, a.shape[1]), lambda i: (i, 0)) for a in row_ins]
    in_specs += [pl.BlockSpec(v.shape, lambda i, nd=v.ndim: (0,) * nd) for v in vec_ins]
    in_specs += [pl.BlockSpec(memory_space=pl.ANY) for _ in extra]
    out_shape = [_sds((L, w), dt) for w, dt in row_outs] + [_sds(s, dt) for s, dt in acc_outs]
    out_specs = [pl.BlockSpec((tr, w), lambda i: (i, 0)) for w, _ in row_outs]
    out_specs += [pl.BlockSpec(s, lambda i, nd=len(s): (0,) * nd) for s, _ in acc_outs]
    sem = ("arbitrary",) if acc_outs else ("parallel",)
    return pl.pallas_call(body, name=name, grid=(L // tr,), in_specs=in_specs, out_specs=out_specs,
                          out_shape=out_shape, compiler_params=_cp(sem))(*row_ins, *vec_ins, *extra)


def silu_rows(x, name, after=None):
    def body(x_ref, o_ref):
        v = x_ref[...]
        o_ref[...] = (v * _sig(v)).astype(o_ref.dtype)
    return _row_call(name, body, [x], [], [(x.shape[1], BF16)], after=after)[0]


def norm_mod_fwd(x, gain, sc, sh, name, after=None):
    def body(x_ref, g_ref, sc_ref, sh_ref, h_ref):
        v = x_ref[...]
        r = lax.rsqrt(jnp.mean(v * v, axis=-1, keepdims=True) + EPS)
        h_ref[...] = (v * r * g_ref[...] * (1.0 + sc_ref[...]) + sh_ref[...]).astype(BF16)
    return _row_call(name, body, [x], [gain, sc, sh], [(x.shape[1], BF16)], after=after)[0]


def norm_mod_bwd(dh, x, gain, sc, dres, name, after=None):
    D = x.shape[1]

    def body(dh_ref, x_ref, dres_ref, g_ref, sc_ref, dx_ref, dg_ref, dsc_ref, dsh_ref):
        @pl.when(pl.program_id(0) == 0)
        def _():
            dg_ref[...] = jnp.zeros_like(dg_ref)
            dsc_ref[...] = jnp.zeros_like(dsc_ref)
            dsh_ref[...] = jnp.zeros_like(dsh_ref)

        v = x_ref[...]
        dh_v = dh_ref[...]
        g = g_ref[...]
        r = lax.rsqrt(jnp.mean(v * v, axis=-1, keepdims=True) + EPS)
        xhat = v * r
        dn = dh_v * (1.0 + sc_ref[...])
        dsc_ref[...] += _colsum(dh_v * xhat * g)
        dsh_ref[...] += _colsum(dh_v)
        dg_ref[...] += _colsum(dn * xhat)
        t = dn * g
        dx_ref[...] = dres_ref[...] + r * (t - xhat * jnp.mean(t * xhat, axis=-1, keepdims=True))

    acc = [((1, D), F32)] * 3
    return _row_call(name, body, [dh, x, dres], [gain, sc], [(D, F32)], acc, after=after)


def final_loss(x, gain, tgt, name):
    D = x.shape[1]

    def body(x_ref, t_ref, g_ref, dx_ref, loss_ref, dg_ref, acc_ref):
        i = pl.program_id(0)

        @pl.when(i == 0)
        def _():
            dg_ref[...] = jnp.zeros_like(dg_ref)
            acc_ref[...] = jnp.zeros_like(acc_ref)

        v = x_ref[...]
        g = g_ref[...]
        r = lax.rsqrt(jnp.mean(v * v, axis=-1, keepdims=True) + EPS)
        xhat = v * r
        err = xhat * g - t_ref[...]
        acc_ref[...] += _colsum(err * err)
        dout = err * (1.0 / D)
        dg_ref[...] += _colsum(dout * xhat)
        t = dout * g
        dx_ref[...] = r * (t - xhat * jnp.mean(t * xhat, axis=-1, keepdims=True))

        @pl.when(i == pl.num_programs(0) - 1)
        def _():
            loss_ref[...] = jnp.zeros_like(loss_ref) + jnp.sum(acc_ref[...]) * (0.5 / D)

    return _row_call(name, body, [x, tgt], [gain], [(D, F32)],
                     [((SUBLANES, LANES), F32), ((1, D), F32), ((1, D), F32)])[:3]


def res_gate_fwd(x, z, g, name):
    def body(x_ref, z_ref, g_ref, o_ref):
        o_ref[...] = x_ref[...] + g_ref[...] * z_ref[...]
    return _row_call(name, body, [x, z], [g], [(x.shape[1], F32)])[0]


def res_gate_bwd(dx, z, g, name):
    D = dx.shape[1]

    def body(dx_ref, z_ref, g_ref, dz_ref, dg_ref):
        @pl.when(pl.program_id(0) == 0)
        def _():
            dg_ref[...] = jnp.zeros_like(dg_ref)
        d = dx_ref[...]
        dz_ref[...] = (g_ref[...] * d).astype(BF16)
        dg_ref[...] += _colsum(d * z_ref[...])
    return _row_call(name, body, [dx, z], [g], [(D, BF16)], [((1, D), F32)])


def glu_res_fwd(x, v, g, name):
    D = x.shape[1]

    def body(x_ref, v_ref, g_ref, o_ref):
        vv = v_ref[...]
        o_ref[...] = x_ref[...] + g_ref[...] * (vv[:, :D] * _sig(vv[:, D:]))
    return _row_call(name, body, [x, v], [g], [(D, F32)])[0]


def glu_res_bwd(dx, v, g, name, after=None):
    D = dx.shape[1]

    def body(dx_ref, v_ref, g_ref, dv_ref, dg_ref):
        @pl.when(pl.program_id(0) == 0)
        def _():
            dg_ref[...] = jnp.zeros_like(dg_ref)
        d = dx_ref[...]
        vv = v_ref[...]
        val = vv[:, :D]
        s = _sig(vv[:, D:])
        dg_ref[...] += _colsum(d * val * s)
        dm = g_ref[...] * d
        dv_ref[:, :D] = (dm * s).astype(BF16)
        dv_ref[:, D:] = (dm * val * s * (1.0 - s)).astype(BF16)
    return _row_call(name, body, [dx, v], [g], [(2 * D, BF16)], [((1, D), F32)], after=after)


def swiglu_fwd(gu, name):
    F = gu.shape[1] // 2

    def body(gu_ref, o_ref):
        v = gu_ref[...].astype(F32)
        g = v[:, :F]
        o_ref[...] = (g * _sig(g) * v[:, F:]).astype(BF16)
    return _row_call(name, body, [gu], [], [(F, BF16)])[0]


def swiglu_bwd(dact, gu, name):
    F = gu.shape[1] // 2

    def body(da_ref, gu_ref, o_ref):
        v = gu_ref[...].astype(F32)
        g, u = v[:, :F], v[:, F:]
        da = da_ref[...]
        s = _sig(g)
        o_ref[:, :F] = (da * u * s * (1.0 + g * (1.0 - s))).astype(BF16)
        o_ref[:, F:] = (da * g * s).astype(BF16)
    return _row_call(name, body, [dact, gu], [], [(2 * F, BF16)])[0]


def adamw(w, g, m, v, name, after=None):
    C = w.shape[1]
    c1 = 1.0 - ADAM_B1 ** ADAM_STEP
    c2 = 1.0 - ADAM_B2 ** ADAM_STEP

    def body(w_ref, g_ref, m_ref, v_ref, d_ref, m2_ref, v2_ref):
        gv = g_ref[...]
        m2 = ADAM_B1 * m_ref[...] + (1.0 - ADAM_B1) * gv
        v2 = ADAM_B2 * v_ref[...] + (1.0 - ADAM_B2) * (gv * gv)
        m2_ref[...] = m2
        v2_ref[...] = v2
        d_ref[...] = -ADAM_LR * ((m2 / c1) / (jnp.sqrt(v2 / c2) + ADAM_EPS) + ADAM_WD * w_ref[...])
    return _row_call(name, body, [w, g, m, v], [], [(C, F32)] * 3, after=after)


def sum_devices(parts, name):
    n, R, C = parts.shape
    min_rows = 16 if parts.dtype == BF16 else SUBLANES
    tr = _tile(R, tuple(t for t in _TR if t * C * n <= 4 * ROW_TILE_ELEMS and t >= min_rows) or (min_rows,))

    def body(p_ref, o_ref):
        acc = p_ref[0].astype(F32)
        for d in range(1, n):
            acc = acc + p_ref[d].astype(F32)
        o_ref[...] = acc
    return pl.pallas_call(body, name=name, grid=(R // tr,),
                          in_specs=[pl.BlockSpec((n, tr, C), lambda i: (0, i, 0))],
                          out_specs=pl.BlockSpec((tr, C), lambda i: (i, 0)), out_shape=_sds((R, C), F32),
                          compiler_params=_cp(("parallel",)))(parts)


def _s5_discretize(lam_re, lam_im, log_dt, b_re, b_im):
    dt = jnp.exp(log_dt)[:, None]
    mag = jnp.exp(lam_re * dt)
    ab_re = mag * jnp.cos(lam_im * dt)
    ab_im = mag * jnp.sin(lam_im * dt)
    nr, ni = ab_re - 1.0, ab_im
    den = lam_re * lam_re + lam_im * lam_im
    f_re = (nr * lam_re + ni * lam_im) / den
    f_im = (ni * lam_re - nr * lam_im) / den
    bb_re = f_re[..., None] * b_re - f_im[..., None] * b_im
    bb_im = f_re[..., None] * b_im + f_im[..., None] * b_re
    return ab_re, ab_im, bb_re, bb_im


def _s5_blockdiag(bb_re, bb_im, c_re, c_im):
    G, P, Cg = bb_re.shape
    nsb = G // S5_SB_GROUPS

    def bmat(bb):
        return _block_diag(jnp.swapaxes(bb.reshape(nsb, S5_SB_GROUPS, P, Cg), 2, 3))

    def cmat(cc):
        return _block_diag(jnp.swapaxes(cc.reshape(nsb, S5_SB_GROUPS, Cg, P), 2, 3))

    bsb = jnp.concatenate([bmat(bb_re), bmat(bb_im)], axis=-1)
    csb = jnp.concatenate([cmat(c_re), -cmat(c_im)], axis=1)
    return bsb, csb


def _block_diag(t):
    ng, b = t.shape[1], t.shape[3]
    rows = [jnp.pad(t[:, g], ((0, 0), (0, 0), (g * b, (ng - 1 - g) * b))) for g in range(ng)]
    return jnp.concatenate(rows, axis=1)


def _diag_blocks(m, ng):
    a, b = m.shape[1] // ng, m.shape[2] // ng
    return jnp.stack([m[:, g * a:(g + 1) * a, g * b:(g + 1) * b] for g in range(ng)], axis=1)


def _s5_blockdiag_grads(dbsb, dcsb, P, Cg):
    H = S5_SB_GROUPS * P

    def blocks(m):
        return jnp.swapaxes(_diag_blocks(m, S5_SB_GROUPS), 2, 3)

    dbb_re = blocks(dbsb[:, :, :H]).reshape(-1, P, Cg)
    dbb_im = blocks(dbsb[:, :, H:]).reshape(-1, P, Cg)
    dc_re = blocks(dcsb[:, :H, :]).reshape(-1, Cg, P)
    dc_im = -blocks(dcsb[:, H:, :]).reshape(-1, Cg, P)
    return dbb_re, dbb_im, dc_re, dc_im


def _s5_scan_consts(ab_re, ab_im):
    G, P = ab_re.shape
    nsb = G // S5_SB_GROUPS
    H = S5_SB_GROUPS * P
    ar, ai = ab_re.reshape(nsb, 1, H), ab_im.reshape(nsb, 1, H)
    pows = [(ar, ai)]
    for _ in range(SUBLANES - 1):
        pr, pi_ = pows[-1]
        pows.append((pr * ar - pi_ * ai, pr * ai + pi_ * ar))
    rows = jnp.arange(SUBLANES).reshape(1, SUBLANES, 1)

    def masked(k, keep):
        pr, pi_ = pows[k - 1]
        return jnp.where(keep, pr, 0.0), jnp.where(keep, pi_, 0.0)

    def per_row(sel):
        pr = jnp.concatenate([pows[sel(r) - 1][0] for r in range(SUBLANES)], axis=1)
        pi_ = jnp.concatenate([pows[sel(r) - 1][1] for r in range(SUBLANES)], axis=1)
        return pr, pi_

    fwd = [masked(1, rows >= 1), masked(2, rows >= 2), masked(4, rows >= 4), per_row(lambda r: r + 1)]
    rev = [masked(1, rows < 7), masked(2, rows < 6), masked(4, rows < 4), per_row(lambda r: SUBLANES - r)]

    def pack(lst, conj):
        sgn = -1.0 if conj else 1.0
        return jnp.stack([jnp.concatenate([jnp.broadcast_to(pr, (nsb, SUBLANES, H)),
                                           sgn * jnp.broadcast_to(pi_, (nsb, SUBLANES, H))], axis=-1)
                          for pr, pi_ in lst], axis=1)

    return pack(fwd, False), pack(rev, True)


def _cmadd(xr, xi, ar, ai, yr, yi):
    return xr + ar * yr - ai * yi, xi + ar * yi + ai * yr


def _s5_scan_fwd_loop(src_ref, dst_ref, cf_ref, cr, ci, nblk, H):
    def body(k, carry):
        cr, ci = carry
        r0 = pl.multiple_of(k * SUBLANES, SUBLANES)
        xr = src_ref[pl.ds(r0, SUBLANES), pl.ds(0, H)]
        xi = src_ref[pl.ds(r0, SUBLANES), pl.ds(H, H)]
        for idx, d in enumerate((1, 2, 4)):
            xr, xi = _cmadd(xr, xi, cf_ref[idx, :, pl.ds(0, H)], cf_ref[idx, :, pl.ds(H, H)],
                            pltpu.roll(xr, d, 0), pltpu.roll(xi, d, 0))
        xr, xi = _cmadd(xr, xi, cf_ref[3, :, pl.ds(0, H)], cf_ref[3, :, pl.ds(H, H)], cr, ci)
        dst_ref[pl.ds(r0, SUBLANES), pl.ds(0, H)] = xr
        dst_ref[pl.ds(r0, SUBLANES), pl.ds(H, H)] = xi
        return _row(xr, SUBLANES - 1), _row(xi, SUBLANES - 1)

    return lax.fori_loop(0, nblk, body, (cr, ci))


def s5_scan_fwd(u, d_skip, bsb, csb, cf, name):
    L, W = u.shape
    nsb, GW, H2 = bsb.shape
    H = H2 // 2
    Tc = _tile(L, (512, 256, 128, 64, 32, 16, 8))
    nch = L // Tc

    def body(u_ref, d_ref, b_ref, c_ref, cf_ref, ypre_ref, yg_ref, s_ref, ss_ref, bu_scr, car_scr):
        @pl.when(pl.program_id(1) == 0)
        def _():
            car_scr[...] = jnp.zeros_like(car_scr)

        ss_ref[...] = car_scr[...]
        ub = u_ref[...]
        bu_scr[...] = jnp.dot(ub.astype(BF16), b_ref[...], preferred_element_type=F32)
        cr, ci = _s5_scan_fwd_loop(bu_scr, s_ref, cf_ref, car_scr[:, pl.ds(0, H)], car_scr[:, pl.ds(H, H)],
                                   Tc // SUBLANES, H)
        car_scr[:, pl.ds(0, H)] = cr
        car_scr[:, pl.ds(H, H)] = ci
        ypre = jnp.dot(s_ref[...].astype(BF16), c_ref[...], preferred_element_type=F32) + d_ref[...] * ub
        ypre_ref[...] = ypre
        yg_ref[...] = _gelu(ypre).astype(BF16)

    return pl.pallas_call(
        body, name=name, grid=(nsb, nch),
        in_specs=[pl.BlockSpec((Tc, GW), lambda j, i: (i, j)),
                  pl.BlockSpec((1, GW), lambda j, i: (0, j)),
                  pl.BlockSpec((None, GW, H2), lambda j, i: (j, 0, 0)),
                  pl.BlockSpec((None, H2, GW), lambda j, i: (j, 0, 0)),
                  pl.BlockSpec((None, 4, SUBLANES, H2), lambda j, i: (j, 0, 0, 0))],
        out_specs=[pl.BlockSpec((Tc, GW), lambda j, i: (i, j)),
                   pl.BlockSpec((Tc, GW), lambda j, i: (i, j)),
                   pl.BlockSpec((Tc, H2), lambda j, i: (i, j)),
                   pl.BlockSpec((None, None, 1, H2), lambda j, i: (i, j, 0, 0))],
        out_shape=[_sds((L, W), F32), _sds((L, W), BF16), _sds((L, nsb * H2), F32),
                   _sds((nch, nsb, 1, H2), F32)],
        scratch_shapes=[pltpu.VMEM((Tc, H2), F32), pltpu.VMEM((1, H2), F32)],
        compiler_params=_cp(("arbitrary", "arbitrary")),
    )(u, d_skip, bsb.astype(BF16), csb.astype(BF16), cf)


def s5_scan_bwd(u, dyg, ypre, s_all, d_skip, bsb, csb, crv, ss, name):
    L, W = u.shape
    nsb, GW, H2 = bsb.shape
    H = H2 // 2
    Tc = _tile(L, (512, 256, 128, 64, 32, 16, 8))
    nch = L // Tc
    nblk = Tc // SUBLANES
    bsb_t = jnp.swapaxes(bsb, 1, 2).astype(BF16)
    csb_t = jnp.swapaxes(csb, 1, 2).astype(BF16)

    def body(u_ref, dyg_ref, yp_ref, s_ref, d_ref, bt_ref, ct_ref, crv_ref, ss_ref,
             du_ref, db_ref, dc_ref, da_ref, dd_ref, g_scr, gcar_scr):
        @pl.when(pl.program_id(1) == 0)
        def _():
            gcar_scr[...] = jnp.zeros_like(gcar_scr)
            db_ref[...] = jnp.zeros_like(db_ref)
            dc_ref[...] = jnp.zeros_like(dc_ref)
            da_ref[...] = jnp.zeros_like(da_ref)
            dd_ref[...] = jnp.zeros_like(dd_ref)

        ub = u_ref[...]
        ubf = ub.astype(BF16)
        dyp = dyg_ref[...] * _gelu_grad(yp_ref[...])
        dypb = dyp.astype(BF16)
        dd_ref[...] += _colsum(dyp * ub)
        g_scr[...] = jnp.dot(dypb, ct_ref[...], preferred_element_type=F32)
        rows = lax.broadcasted_iota(jnp.int32, (SUBLANES, H), 0)

        def rev(kk, carry):
            gr, gi, acc_r, acc_i = carry
            k = nblk - 1 - kk
            r0 = pl.multiple_of(k * SUBLANES, SUBLANES)
            xr = g_scr[pl.ds(r0, SUBLANES), pl.ds(0, H)]
            xi = g_scr[pl.ds(r0, SUBLANES), pl.ds(H, H)]
            for idx, d in enumerate((1, 2, 4)):
                xr, xi = _cmadd(xr, xi, crv_ref[idx, :, pl.ds(0, H)], crv_ref[idx, :, pl.ds(H, H)],
                                pltpu.roll(xr, SUBLANES - d, 0), pltpu.roll(xi, SUBLANES - d, 0))
            xr, xi = _cmadd(xr, xi, crv_ref[3, :, pl.ds(0, H)], crv_ref[3, :, pl.ds(H, H)], gr, gi)
            g_scr[pl.ds(r0, SUBLANES), pl.ds(0, H)] = xr
            g_scr[pl.ds(r0, SUBLANES), pl.ds(H, H)] = xi
            rp = pl.multiple_of(jnp.maximum(k - 1, 0) * SUBLANES, SUBLANES)
            first = k == 0
            pr = jnp.where(first, ss_ref[:, pl.ds(0, H)], _row(s_ref[pl.ds(rp, SUBLANES), pl.ds(0, H)], SUBLANES - 1))
            pi_ = jnp.where(first, ss_ref[:, pl.ds(H, H)], _row(s_ref[pl.ds(rp, SUBLANES), pl.ds(H, H)], SUBLANES - 1))
            spr = jnp.where(rows == 0, pr, pltpu.roll(s_ref[pl.ds(r0, SUBLANES), pl.ds(0, H)], 1, 0))
            spi = jnp.where(rows == 0, pi_, pltpu.roll(s_ref[pl.ds(r0, SUBLANES), pl.ds(H, H)], 1, 0))
            return (_row(xr, 0), _row(xi, 0), acc_r + xr * spr + xi * spi, acc_i + xi * spr - xr * spi)

        zero = jnp.zeros((SUBLANES, H), F32)
        gr, gi, acc_r, acc_i = lax.fori_loop(
            0, nblk, rev, (gcar_scr[:, pl.ds(0, H)], gcar_scr[:, pl.ds(H, H)], zero, zero))
        gcar_scr[:, pl.ds(0, H)] = gr
        gcar_scr[:, pl.ds(H, H)] = gi
        da_ref[:, pl.ds(0, H)] += _colsum(acc_r)
        da_ref[:, pl.ds(H, H)] += _colsum(acc_i)
        gb = g_scr[...].astype(BF16)
        db_ref[...] += lax.dot_general(ubf, gb, _TN_DIMS, preferred_element_type=F32)
        dc_ref[...] += lax.dot_general(s_ref[...].astype(BF16), dypb, _TN_DIMS, preferred_element_type=F32)
        du_ref[...] = (jnp.dot(gb, bt_ref[...], preferred_element_type=F32) + d_ref[...] * dyp).astype(BF16)

    rmap = lambda j, i: (nch - 1 - i, j)
    return pl.pallas_call(
        body, name=name, grid=(nsb, nch),
        in_specs=[pl.BlockSpec((Tc, GW), rmap), pl.BlockSpec((Tc, GW), rmap), pl.BlockSpec((Tc, GW), rmap),
                  pl.BlockSpec((Tc, H2), rmap),
                  pl.BlockSpec((1, GW), lambda j, i: (0, j)),
                  pl.BlockSpec((None, H2, GW), lambda j, i: (j, 0, 0)),
                  pl.BlockSpec((None, GW, H2), lambda j, i: (j, 0, 0)),
                  pl.BlockSpec((None, 4, SUBLANES, H2), lambda j, i: (j, 0, 0, 0)),
                  pl.BlockSpec((None, None, 1, H2), lambda j, i: (nch - 1 - i, j, 0, 0))],
        out_specs=[pl.BlockSpec((Tc, GW), rmap),
                   pl.BlockSpec((None, GW, H2), lambda j, i: (j, 0, 0)),
                   pl.BlockSpec((None, H2, GW), lambda j, i: (j, 0, 0)),
                   pl.BlockSpec((None, 1, H2), lambda j, i: (j, 0, 0)),
                   pl.BlockSpec((1, GW), lambda j, i: (0, j))],
        out_shape=[_sds((L, W), BF16), _sds((nsb, GW, H2), F32), _sds((nsb, H2, GW), F32),
                   _sds((nsb, 1, H2), F32), _sds((1, W), F32)],
        scratch_shapes=[pltpu.VMEM((Tc, H2), F32), pltpu.VMEM((1, H2), F32)],
        compiler_params=_cp(("arbitrary", "arbitrary")),
    )(u, dyg, ypre, s_all, d_skip, bsb_t, csb_t, crv, ss)


def _lru_blockdiag(w_rg, w_ig):
    nb, bs, _ = w_rg.shape
    sbw = bs * LANES // math.gcd(bs, LANES)
    bps = sbw // bs
    nsb = nb // bps

    def bd(w):
        return _block_diag(w.astype(BF16).reshape(nsb, bps, bs, bs))

    return bd(w_rg), bd(w_ig)


def _lru_blockdiag_grad(dwsb, nb, bs):
    return _diag_blocks(dwsb, dwsb.shape[1] // bs).reshape(nb, bs, bs)


def lru_conv_fwd(p, conv_w, conv_b, name):
    L = p.shape[0]
    E = conv_w.shape[1]
    tc = _tile(E, (256, 128))
    noff = E // tc
    kw = conv_w.shape[0]

    def body(xb_ref, w_ref, b_ref, xc_ref, xcb_ref):
        xb = xb_ref[...]
        rows = lax.broadcasted_iota(jnp.int32, xb.shape, 0)
        acc = w_ref[pl.ds(kw - 1, 1), :] * xb + b_ref[...]
        for k in range(kw - 1):
            sh = kw - 1 - k
            acc = acc + w_ref[pl.ds(k, 1), :] * jnp.where(rows >= sh, pltpu.roll(xb, sh, 0), 0.0)
        xc_ref[...] = acc
        xcb_ref[...] = acc.astype(BF16)

    return pl.pallas_call(
        body, name=name, grid=(noff,),
        in_specs=[pl.BlockSpec((L, tc), lambda t: (0, noff + t)),
                  pl.BlockSpec((kw, tc), lambda t: (0, t)), pl.BlockSpec((1, tc), lambda t: (0, t))],
        out_specs=[pl.BlockSpec((L, tc), lambda t: (0, t))] * 2,
        out_shape=[_sds((L, E), F32), _sds((L, E), BF16)],
        compiler_params=_cp(("parallel",)),
    )(p, conv_w, conv_b)


def lru_conv_bwd(d1, d2, d3, p, conv_w, name):
    L = p.shape[0]
    E = conv_w.shape[1]
    tc = _tile(E, (256, 128))
    noff = E // tc
    kw = conv_w.shape[0]

    def body(d1_ref, d2_ref, d3_ref, xb_ref, w_ref, dxb_ref, dw_ref, db_ref):
        dxc = d1_ref[...] + d2_ref[...] + d3_ref[...]
        xb = xb_ref[...]
        rows = lax.broadcasted_iota(jnp.int32, xb.shape, 0)
        db_ref[...] = _colsum(dxc)
        acc = w_ref[pl.ds(kw - 1, 1), :] * dxc
        dw_ref[pl.ds(kw - 1, 1), :] = _colsum(dxc * xb)
        for k in range(kw - 1):
            sh = kw - 1 - k
            dw_ref[pl.ds(k, 1), :] = _colsum(dxc * jnp.where(rows >= sh, pltpu.roll(xb, sh, 0), 0.0))
            acc = acc + w_ref[pl.ds(k, 1), :] * jnp.where(rows < L - sh, pltpu.roll(dxc, L - sh, 0), 0.0)
        dxb_ref[...] = acc.astype(BF16)

    return pl.pallas_call(
        body, name=name, grid=(noff,),
        in_specs=[pl.BlockSpec((L, tc), lambda t: (0, t))] * 3 +
                 [pl.BlockSpec((L, tc), lambda t: (0, noff + t)), pl.BlockSpec((kw, tc), lambda t: (0, t))],
        out_specs=[pl.BlockSpec((L, tc), lambda t: (0, t)), pl.BlockSpec((kw, tc), lambda t: (0, t)),
                   pl.BlockSpec((1, tc), lambda t: (0, t))],
        out_shape=[_sds((L, E), BF16), _sds((kw, E), F32), _sds((1, E), F32)],
        compiler_params=_cp(("parallel",)),
    )(d1, d2, d3, p, conv_w)


def _lru_gates(pr, pi_, brg, big, sp):
    r = _sig(pr + brg)
    ig = _sig(pi_ + big)
    la = -LRU_C * r * sp
    a = jnp.exp(la)
    mult = jnp.sqrt(_neg_expm1(2.0 * la))
    return r, ig, a, mult


def _lru_specs(L, E):
    tc = _tile(E, (LRU_TILE, LANES))
    col = pl.BlockSpec((L, tc), lambda t: (0, t))
    vec = pl.BlockSpec((1, tc), lambda t: (0, t))
    return tc, col, vec


def lru_scan_fwd(pre_r, pre_i, xc, p, b_rg, b_ig, lam, name):
    L, E = xc.shape
    tc, col, vec = _lru_specs(L, E)
    nblk = L // SUBLANES

    def body(pr_ref, pi_ref, xc_ref, gb_ref, brg_ref, big_ref, lam_ref, hs_ref, yv_ref):
        sp = _softplus(-lam_ref[...])
        brg, big = brg_ref[...], big_ref[...]
        rows = lax.broadcasted_iota(jnp.int32, (SUBLANES, tc), 0)

        def blk(k, carry):
            r0 = pl.multiple_of(k * SUBLANES, SUBLANES)
            sl = pl.ds(r0, SUBLANES)
            _, ig, a, mult = _lru_gates(pr_ref[sl, :], pi_ref[sl, :], brg, big, sp)
            b = mult * ig * xc_ref[sl, :]
            for d in (1, 2, 4):
                keep = rows >= d
                b = b + a * jnp.where(keep, pltpu.roll(b, d, 0), 0.0)
                a = a * jnp.where(keep, pltpu.roll(a, d, 0), 1.0)
            h = b + a * carry
            hs_ref[sl, :] = h
            return _row(h, SUBLANES - 1)

        def trip(kt, carry):
            for q in range(SCAN_UNROLL):
                carry = blk(kt * SCAN_UNROLL + q, carry)
            return carry

        lax.fori_loop(0, nblk // SCAN_UNROLL, trip, jnp.zeros((1, tc), F32))
        yv_ref[...] = (hs_ref[...] * _gelu(gb_ref[...])).astype(BF16)

    return pl.pallas_call(
        body, name=name, grid=(E // tc,),
        in_specs=[col, col, col, col, vec, vec, vec],
        out_specs=[col, col], out_shape=[_sds((L, E), F32), _sds((L, E), BF16)],
        compiler_params=_cp(("parallel",)),
    )(pre_r, pre_i, xc, p, b_rg, b_ig, lam)


def lru_scan_bwd(dyv, hs, pre_r, pre_i, xc, p, b_rg, b_ig, lam, name):
    L, E = xc.shape
    tc, col, vec = _lru_specs(L, E)
    nblk = L // SUBLANES

    def body(dyv_ref, hs_ref, pr_ref, pi_ref, xc_ref, gb_ref, brg_ref, big_ref, lam_ref,
             dgb_ref, dpr_ref, dpi_ref, dxc_ref, dbrg_ref, dbig_ref, dlam_ref, t_gb, t_pr, t_pi):
        lam_v = lam_ref[...]
        sp = _softplus(-lam_v)
        brg, big = brg_ref[...], big_ref[...]
        rows = lax.broadcasted_iota(jnp.int32, (SUBLANES, tc), 0)

        def blk(kk, carry):
            gcar, a_next, acc_sp, acc_r, acc_i = carry
            k = nblk - 1 - kk
            r0 = pl.multiple_of(k * SUBLANES, SUBLANES)
            sl = pl.ds(r0, SUBLANES)
            r, ig, a, mult = _lru_gates(pr_ref[sl, :], pi_ref[sl, :], brg, big, sp)
            gbv, hsv, dyvv, xcv = gb_ref[sl, :], hs_ref[sl, :], dyv_ref[sl, :], xc_ref[sl, :]
            t_gb[sl, :] = dyvv * hsv * _gelu_grad(gbv)
            x = dyvv * _gelu(gbv)
            al = jnp.where(rows == SUBLANES - 1, a_next, pltpu.roll(a, SUBLANES - 1, 0))
            for d in (1, 2, 4):
                keep = rows < SUBLANES - d
                x = x + al * jnp.where(keep, pltpu.roll(x, SUBLANES - d, 0), 0.0)
                al = al * jnp.where(keep, pltpu.roll(al, SUBLANES - d, 0), 1.0)
            g = x + al * gcar
            rp = pl.multiple_of(jnp.maximum(k - 1, 0) * SUBLANES, SUBLANES)
            hlast = _row(hs_ref[pl.ds(rp, SUBLANES), :], SUBLANES - 1) * (k > 0).astype(F32)
            hprev = jnp.where(rows == 0, hlast, pltpu.roll(hsv, 1, 0))
            da = g * hprev
            dmult = g * ig * xcv
            dig = g * mult * xcv
            dxc_ref[sl, :] = g * mult * ig
            dla = da * a - dmult * (a * a) / mult
            dpr = dla * (-LRU_C * sp) * r * (1.0 - r)
            dpi = dig * ig * (1.0 - ig)
            t_pr[sl, :] = dpr
            t_pi[sl, :] = dpi
            return (_row(g, 0), _row(a, 0), acc_sp + dla * (-LRU_C * r), acc_r + dpr, acc_i + dpi)

        zero = jnp.zeros((SUBLANES, tc), F32)
        z1 = jnp.zeros((1, tc), F32)
        def trip(kt, carry):
            for q in range(SCAN_UNROLL):
                carry = blk(kt * SCAN_UNROLL + q, carry)
            return carry

        _, _, acc_sp, acc_r, acc_i = lax.fori_loop(0, nblk // SCAN_UNROLL, trip, (z1, z1, zero, zero, zero))
        dgb_ref[...] = t_gb[...].astype(BF16)
        dpr_ref[...] = t_pr[...].astype(BF16)
        dpi_ref[...] = t_pi[...].astype(BF16)
        dbrg_ref[...] = _colsum(acc_r)
        dbig_ref[...] = _colsum(acc_i)
        dlam_ref[...] = -_colsum(acc_sp) * _sig(-lam_v)

    return pl.pallas_call(
        body, name=name, grid=(E // tc,),
        in_specs=[col, col, col, col, col, col, vec, vec, vec],
        out_specs=[col, col, col, col, vec, vec, vec],
        out_shape=[_sds((L, E), BF16), _sds((L, E), BF16), _sds((L, E), BF16), _sds((L, E), F32),
                   _sds((1, E), F32), _sds((1, E), F32), _sds((1, E), F32)],
        scratch_shapes=[pltpu.VMEM((L, tc), F32)] * 3,
        compiler_params=_cp(("parallel",)),
    )(dyv, hs, pre_r, pre_i, xc, p, b_rg, b_ig, lam)


def _place():
    xi, yi, ci = lax.axis_index("x"), lax.axis_index("y"), lax.axis_index("c")
    chips = [(1 - xi, yi), (xi, 1 - yi), (1 - xi, 1 - yi)]
    return xi, yi, ci, chips


_ANY = pl.BlockSpec(memory_space=pl.ANY)


def all_gather_devices(blks, name):
    n = len(blks)

    def body(*refs):
        ins, outs = refs[:n], refs[n:2 * n]
        send_sems, recv_sems, local_sems = refs[2 * n:]
        xi, yi, ci, chips = _place()
        me, sibling = (xi, yi, ci), (xi, yi, 1 - ci)

        def slab(a, px, py, pc):
            return outs[a].at[4 * px + 2 * py + pc]

        def copy(a, k, block, to, src=None):
            return pltpu.make_async_remote_copy(
                src_ref=slab(a, *block) if src is None else src, dst_ref=slab(a, *block),
                send_sem=send_sems.at[7 * a + k], recv_sem=recv_sems.at[7 * a + k], device_id=to,
                device_id_type=MESH)

        mine = [pltpu.make_async_copy(ins[a], slab(a, *me), local_sems.at[a]) for a in range(n)]
        first, passed = [], []
        for a in range(n):
            mine[a].start()
            first.append(copy(a, 0, me, sibling, src=ins[a]))
            first += [copy(a, 1 + j, me, (*chip, ci), src=ins[a]) for j, chip in enumerate(chips)]
        for cp in first:
            cp.start()
        for a in range(n):
            for j, chip in enumerate(chips):
                copy(a, 1 + j, (*chip, ci), me).wait_recv()
                passed.append(copy(a, 4 + j, (*chip, ci), sibling))
                passed[-1].start()
        for a in range(n):
            copy(a, 0, sibling, me).wait_recv()
            for j, chip in enumerate(chips):
                copy(a, 4 + j, (*chip, 1 - ci), me).wait_recv()
        for cp in first + passed:
            cp.wait_send()
        for cp in mine:
            cp.wait()

    return pl.pallas_call(
        body, name=name, in_specs=[_ANY] * n, out_specs=[_ANY] * n,
        out_shape=[_sds((N_DEV,) + b.shape, b.dtype) for b in blks],
        scratch_shapes=[pltpu.SemaphoreType.DMA((7 * n,)), pltpu.SemaphoreType.DMA((7 * n,)),
                        pltpu.SemaphoreType.DMA((n,))],
    )(*blks)


_HBM = pl.BlockSpec(memory_space=pltpu.HBM)
_SEM = pl.BlockSpec(memory_space=pltpu.SEMAPHORE)
_EFFECT = pltpu.SideEffectType.DATAFLOW_SIDE_EFFECTING


def split_start(name, groups, counts, copies_fn):
    flat = [b for g in groups for b in g]
    n, ng = len(flat), len(groups)

    def body(*refs):
        ins, sems, token = refs[:n], refs[n:n + 2 * ng], refs[-1]
        off = 0
        for gi, g in enumerate(groups):
            for cp in copies_fn(ins[off:off + len(g)], [b.shape for b in g], sems[2 * gi], sems[2 * gi + 1]):
                cp.start()
            off += len(g)
        token[...] = jnp.zeros_like(token)

    out_shape = tuple(pltpu.SemaphoreType.DMA((c,)) for c in counts for _ in range(2))
    out_shape += tuple(pltpu.HBM(b.shape, b.dtype) for b in flat) + (_sds((SUBLANES, LANES), F32),)
    outs = pl.pallas_call(
        body, name=name, in_specs=[_HBM] * n, out_shape=out_shape,
        out_specs=tuple([_SEM] * (2 * ng) + [_HBM] * n + [pl.BlockSpec(memory_space=pltpu.VMEM)]),
        input_output_aliases={i: 2 * ng + i for i in range(n)},
        compiler_params=pltpu.CompilerParams(has_side_effects=_EFFECT),
    )(*[pltpu.with_memory_space_constraint(b, pltpu.HBM) for b in flat])
    sems = [(outs[2 * gi], outs[2 * gi + 1]) for gi in range(ng)]
    thru, off = [], 2 * ng
    for g in groups:
        thru.append(list(outs[off:off + len(g)]))
        off += len(g)
    return sems, thru, outs[-1]


def split_wait(name, bufs, sems, copies_fn, after):
    n = len(bufs)
    shapes = [b.shape for b in bufs]

    def body(*refs):
        for cp in copies_fn(refs[:n], shapes, refs[n], refs[n + 1]):
            cp.wait_send()
            cp.wait_recv()

    return list(pl.pallas_call(
        body, name=name, in_specs=[_HBM] * n + [_SEM, _SEM, _ANY],
        out_shape=tuple(pltpu.HBM(b.shape, b.dtype) for b in bufs), out_specs=tuple([_HBM] * n),
        input_output_aliases={i: i for i in range(n)},
        compiler_params=pltpu.CompilerParams(has_side_effects=_EFFECT),
    )(*bufs, sems[0], sems[1], after))


def _gather_copies(bufs, shapes, send_sems, recv_sems):
    xi, yi, ci, chips = _place()
    cps = []
    for a, ref in enumerate(bufs):
        hr = shapes[a][1] // 2
        rows = ref.at[2 * xi + yi, pl.ds(pl.multiple_of(ci * hr, 16), hr), :]
        for j in range(3):
            cps.append(pltpu.make_async_remote_copy(
                src_ref=rows, dst_ref=rows, send_sem=send_sems.at[3 * a + j], recv_sem=recv_sems.at[3 * a + j],
                device_id=(*chips[j], ci), device_id_type=MESH))
    return cps


def _scatter_copies(bufs, shapes, send_sems, recv_sems):
    xi, yi, ci, chips = _place()
    n = len(bufs) // 2
    cps = []
    for a in range(n):
        for j in range(3):
            cps.append(pltpu.make_async_remote_copy(
                src_ref=bufs[a].at[2 * chips[j][0] + chips[j][1]], dst_ref=bufs[n + a].at[j],
                send_sem=send_sems.at[3 * a + j], recv_sem=recv_sems.at[3 * a + j],
                device_id=(*chips[j], ci), device_id_type=MESH))
    return cps


def cast_place(w, layer, chip1, name, after=None):
    _, R, C = w.shape
    tr = _tile(R, tuple(t for t in _TR if t * C <= ROW_TILE_ELEMS) or (16,))
    extra = [] if after is None else [after]

    def body(c_ref, w_ref, *rest):
        rest[-1][...] = w_ref[...].astype(BF16)

    return pl.pallas_call(
        body, name=name,
        grid_spec=pltpu.PrefetchScalarGridSpec(
            num_scalar_prefetch=1, grid=(R // tr,),
            in_specs=[pl.BlockSpec((None, tr, C), lambda i, c: (layer, i, 0))] + [_ANY] * len(extra),
            out_specs=pl.BlockSpec((None, tr, C), lambda i, c: (c[0], i, 0))),
        out_shape=_sds((N_CHIPS, R, C), BF16), compiler_params=_cp(("parallel",)),
    )(chip1, w, *extra)


def _forward_copies(bufs, shapes, send_sems, recv_sems):
    xi, yi, ci, chips = _place()
    cps = []
    for a, ref in enumerate(bufs):
        hr = shapes[a][1] // 2
        for j in range(3):
            rows = ref.at[2 * chips[j][0] + chips[j][1], pl.ds(pl.multiple_of(ci * hr, 16), hr), :]
            cps.append(pltpu.make_async_remote_copy(
                src_ref=rows, dst_ref=rows, send_sem=send_sems.at[3 * a + j], recv_sem=recv_sems.at[3 * a + j],
                device_id=(xi, yi, 1 - ci), device_id_type=MESH))
    return cps


def _halves_copies(bufs, shapes, send_sems, recv_sems):
    xi, yi, ci, _ = _place()
    n = len(bufs) // 2
    cps = []
    for a in range(n):
        hr = shapes[a][1] // 2
        cps.append(pltpu.make_async_remote_copy(
            src_ref=bufs[a].at[:, pl.ds(pl.multiple_of((1 - ci) * hr, 16), hr), :], dst_ref=bufs[n + a],
            send_sem=send_sems.at[a], recv_sem=recv_sems.at[a], device_id=(xi, yi, 1 - ci), device_id_type=MESH))
    return cps


def add_half(g, got, ci, name):
    S, hr, C = got.shape
    tr = _tile(hr, tuple(t for t in _TR if t * C <= ROW_TILE_ELEMS) or (16,))
    nb = hr // tr

    def body(c_ref, g_ref, r_ref, o_ref):
        o_ref[...] = (g_ref[...].astype(F32) + r_ref[...].astype(F32)).astype(BF16)

    return pl.pallas_call(
        body, name=name,
        grid_spec=pltpu.PrefetchScalarGridSpec(
            num_scalar_prefetch=1, grid=(S, nb),
            in_specs=[pl.BlockSpec((None, tr, C), lambda s, i, c: (s, c[0] * nb + i, 0)),
                      pl.BlockSpec((None, tr, C), lambda s, i, c: (s, i, 0))],
            out_specs=pl.BlockSpec((None, tr, C), lambda s, i, c: (s, i, 0))),
        out_shape=_sds((S, hr, C), BF16), compiler_params=_cp(("parallel", "parallel")),
    )(ci, g, got)


def add_chips(part, got, place, name):
    S, hr, C = part.shape
    tr = _tile(hr, tuple(t for t in _TR if t * C <= ROW_TILE_ELEMS) or (16,))
    nb = hr // tr

    def body(c_ref, p_ref, r_ref, o_ref):
        acc = p_ref[...].astype(F32)
        for j in range(3):
            acc = acc + r_ref[j].astype(F32)
        o_ref[...] = acc

    return pl.pallas_call(
        body, name=name,
        grid_spec=pltpu.PrefetchScalarGridSpec(
            num_scalar_prefetch=1, grid=(nb,),
            in_specs=[pl.BlockSpec((None, tr, C), lambda i, c: (c[0], i, 0)),
                      pl.BlockSpec((3, tr, C), lambda i, c: (0, i, 0))],
            out_specs=pl.BlockSpec((tr, C), lambda i, c: (c[1] * nb + i, 0))),
        out_shape=_sds((2 * hr, C), F32), compiler_params=_cp(("parallel",)),
    )(place, part, got)


def join_halves(bufs, name):
    n = len(bufs)

    def body(*refs):
        ins, outs = refs[:n], refs[n:2 * n]
        send_sems, recv_sems = refs[2 * n:]
        xi, yi, ci, _ = _place()

        def copy(ref, a, h):
            hr = bufs[a].shape[0] // 2
            rows = pl.ds(pl.multiple_of(h * hr, 8), hr)
            return pltpu.make_async_remote_copy(
                src_ref=ref[a].at[rows, :], dst_ref=outs[a].at[rows, :], send_sem=send_sems.at[a],
                recv_sem=recv_sems.at[a], device_id=(xi, yi, 1 - ci), device_id_type=MESH)

        sends = [copy(ins, a, ci) for a in range(n)]
        for cp in sends:
            cp.start()
        for a in range(n):
            copy(outs, a, 1 - ci).wait_recv()
        for cp in sends:
            cp.wait_send()

    return pl.pallas_call(
        body, name=name, in_specs=[_ANY] * n, out_specs=[_ANY] * n,
        out_shape=[_sds(b.shape, b.dtype) for b in bufs], input_output_aliases={i: i for i in range(n)},
        scratch_shapes=[pltpu.SemaphoreType.DMA((n,)), pltpu.SemaphoreType.DMA((n,))],
    )(*bufs)


def reduce_scatter_start(grads, tag):
    lands = [lax.empty((N_CHIPS, g.shape[1] // 2, g.shape[2]), g.dtype) for g in grads]
    sems, thru, token = split_start(f"rs_d2d_start_{tag}", [list(grads) + lands], [len(grads)], _halves_copies)
    return (sems[0], thru[0]), token


def reduce_scatter_relay(state, ci1, after, tag):
    sems, bufs = state
    bufs = split_wait(f"rs_d2d_wait_{tag}", bufs, sems, _halves_copies, after)
    n = len(bufs) // 2
    parts = [add_half(bufs[a], bufs[n + a], ci1, f"rs_add_half_{tag}{a}") for a in range(n)]
    lands = [lax.empty((3,) + q.shape[1:], q.dtype) for q in parts]
    sems, thru, token = split_start(f"rs_ici_start_{tag}", [parts + lands], [3 * n], _scatter_copies)
    return (sems[0], thru[0]), token


def reduce_scatter_finish(state, place, after, tag):
    sems, bufs = state
    bufs = split_wait(f"rs_ici_wait_{tag}", bufs, sems, _scatter_copies, after)
    n = len(bufs) // 2
    return [add_chips(bufs[a], bufs[n + a], place, f"rs_add_chips_{tag}{a}") for a in range(n)]


def _pack(parts, width):
    flat = jnp.concatenate([p.reshape(-1).astype(F32) for p in parts])
    per = SUBLANES * width
    total = -(-flat.shape[0] // per) * per
    flat = jnp.pad(flat, (0, total - flat.shape[0]))
    return flat.reshape(total // width, width)


def _unpack(flat, shapes):
    out, off = [], 0
    for s in shapes:
        n = math.prod(s)
        out.append(flat[off:off + n].reshape(s))
        off += n
    return out


_W_NAMES = ['norm_g', 'w_ada', 'b_ada', 's5_w_in', 's5_lam_re', 's5_lam_im', 's5_log_dt', 's5_b_re', 's5_b_im',
            's5_c_re', 's5_c_im', 's5_d', 's5_w_glu', 'lru_w_in', 'lru_conv_w', 'lru_conv_b', 'lru_w_rg',
            'lru_b_rg', 'lru_w_ig', 'lru_b_ig', 'lru_lam', 'lru_w_out', 'ffn_w_gu', 'ffn_w_down', 'final_g']
_BIG = ('w_ada', 's5_w_in', 's5_w_glu', 'lru_w_in', 'lru_w_out', 'ffn_w_gu', 'ffn_w_down')
_MID = ('s5_b_re', 's5_b_im', 's5_c_re', 's5_c_im', 'lru_w_rg', 'lru_w_ig')


def _ffn_fwd(x, gu, w_down, gate, tag):
    act = swiglu_fwd(gu, f"{tag}_act")
    z = mm_nn(act, w_down, name=f"{tag}_down")
    return res_gate_fwd(x, z, gate, f"{tag}_res"), (gu, act, z)


def _ffn_bwd(dx, h, saved, w_gu, w_down, gate, tag, relay=None):
    gu, act, z = saved
    dz, dgate = res_gate_bwd(dx, z, gate, f"{tag}_res_bwd")
    dact = mm_nt(dz, w_down, name=f"{tag}_dact")
    tok = None if relay is None else relay(dact)
    dw_down = mm_tn(act, dz, name=f"{tag}_dwdown", out_dtype=BF16)
    dgu = swiglu_bwd(dact, gu, f"{tag}_act_bwd")
    dh = mm_nt(dgu, w_gu, name=f"{tag}_dh", bmode="cols", after=tok)
    dw_gu = mm_tn(h, dgu, name=f"{tag}_dwgu", out_dtype=BF16, omode="cols", groups=N_CHIPS)
    return dh, dgate, dw_gu, dw_down.reshape((N_CHIPS, -1) + dw_down.shape[1:])


def _step(p):
    xi, yi, ci = lax.axis_index("x"), lax.axis_index("y"), lax.axis_index("c")
    chip = 2 * xi + yi
    me = 2 * chip + ci
    ci1 = jnp.reshape(ci, (1,)).astype(jnp.int32)
    chip1 = jnp.reshape(chip, (1,)).astype(jnp.int32)
    place2 = jnp.stack([chip, ci]).astype(jnp.int32)

    x0 = p['x'][0]
    tgt = p['loss_target'][0]
    L, D = x0.shape
    Dq = D // N_CHIPS
    depth = p['w_ada'].shape[0]
    E = p['lru_lam'].shape[1] * N_CHIPS
    Eq = E // N_CHIPS
    kw = p['lru_conv_w'].shape[1]
    Nq = p['w_ada'].shape[2]
    _, G, P, Cg = p['s5_b_re'].shape
    nb, bs = p['lru_w_rg'].shape[1], p['lru_w_rg'].shape[2]

    pay = _pack([p['c'], p['norm_g'], p['lru_conv_w'], p['lru_conv_b'], p['lru_b_rg'], p['lru_b_ig'],
                 p['lru_lam']], 1024)
    g1 = all_gather_devices([pay], "gather_small_params")[0].reshape(N_DEV, -1)
    c_all = g1[:, :D]
    per_chip = g1[0::2]
    sizes = [(depth, 2, Dq), (kw, Eq), (Eq,), (Eq,), (Eq,), (Eq,)]
    offs = D
    pieces = []
    for s in sizes:
        nel = math.prod(s)
        pieces.append(per_chip[:, offs:offs + nel].reshape((N_CHIPS,) + s))
        offs += nel
    norm_g = jnp.moveaxis(pieces[0], 0, 2).reshape(depth, 2, D)
    conv_w = jnp.moveaxis(pieces[1], 0, 1).reshape(kw, E)
    conv_b, b_rg, b_ig, lam = [q.reshape(1, E) for q in pieces[2:]]

    cond = silu_rows(jnp.pad(c_all, ((0, 16 - N_DEV), (0, 0))), "cond_silu")
    cond_rep = jnp.concatenate([cond] * depth, axis=1)
    mod_part = mm_nn(cond_rep, p['w_ada'], name="mod_proj", bmode="batch")[:N_DEV]
    g2 = all_gather_devices([mod_part], "gather_mod")[0][0::2]
    mine = lax.dynamic_index_in_dim(g2, me, axis=1, keepdims=False).reshape(N_CHIPS, depth, Nq)
    mod = jnp.moveaxis(mine, 0, 1).reshape(depth, N_CHIPS * Nq) + p['b_ada']
    mods = [[mod[i:i + 1, k * D:(k + 1) * D] for k in range(N_MOD)] for i in range(depth)]

    def place(key, layer, tag, after=None):
        return cast_place(p[key], layer, chip1, f"place_{tag}", after=after)

    first = [place('s5_w_in', 0, "s5_in", after=mod), place('s5_w_glu', 0, "s5_glu")]
    sems_a, bufs_a, tok_a = split_start("gather_ici_start_s5", [first], [3 * len(first)], _gather_copies)
    others = [[place('ffn_w_gu', 0, "gu0", after=tok_a), place('ffn_w_down', 0, "down0")],
              [place('lru_w_in', 0, "lru_in"), place('lru_w_out', 0, "lru_out")],
              [place('ffn_w_gu', 1, "gu1"), place('ffn_w_down', 1, "down1")]]
    sems_b, bufs_b, tok_b = split_start("gather_ici_start_rest", others, [3 * len(g) for g in others], _gather_copies)
    wsems, wbufs = sems_a + sems_b, bufs_a + bufs_b

    def landed(gi, after, tag):
        bufs = split_wait(f"gather_ici_wait_{tag}", wbufs[gi], wsems[gi], _gather_copies, after)
        sems, thru, tok = split_start(f"gather_d2d_start_{tag}", [bufs], [3 * len(bufs)], _forward_copies)
        return (sems[0], thru[0]), tok

    def weights(state, after, tag):
        return split_wait(f"gather_d2d_wait_{tag}", state[1], state[0], _forward_copies, after)

    s5_small = (p['s5_lam_re'][0], p['s5_lam_im'][0], p['s5_log_dt'][0], p['s5_b_re'][0], p['s5_b_im'][0])
    (ab_re, ab_im, bb_re, bb_im), s5_disc_vjp = jax.vjp(_s5_discretize, *s5_small)
    bsb, csb = _s5_blockdiag(bb_re, bb_im, p['s5_c_re'][0], p['s5_c_im'][0])
    cf, crv = _s5_scan_consts(ab_re, ab_im)
    wsb_rg, wsb_ig = [w.astype(BF16) for w in _lru_blockdiag(p['lru_w_rg'][0], p['lru_w_ig'][0])]
    nsb_lru = wsb_rg.shape[0]

    sh1, sc1, gt1, sh2, sc2, gt2 = mods[0]
    st, tok = landed(0, tok_b, "s5")
    s5_w_in, s5_w_glu = weights(st, tok, "s5")
    s5_w_in = s5_w_in.reshape(-1, D)
    h0 = norm_mod_fwd(x0, norm_g[0, 0:1], sc1, sh1, "l0_norm1")
    u = mm_nn(h0, s5_w_in, name="s5_in")
    ypre, yg, s_all, ss = s5_scan_fwd(u, p['s5_d'], bsb, csb, cf, "s5_scan")
    v = mm_nn(yg, s5_w_glu, name="s5_glu", bmode="cols")
    st, tok = landed(1, v, "ffn0")
    x1 = glu_res_fwd(x0, v, gt1, "s5_res")
    h1 = norm_mod_fwd(x1, norm_g[0, 1:2], sc2, sh2, "l0_norm2", after=tok)
    w_gu0, w_down0 = weights(st, h1, "ffn0")
    w_down0 = w_down0.reshape(-1, D)
    gu0 = mm_nn(h1, w_gu0, name="ffn0_gu", out_dtype=BF16, bmode="cols")
    st, tok = landed(2, gu0, "lru")
    x2, ffn0 = _ffn_fwd(x1, gu0, w_down0, gt2, "ffn0")

    sh1b, sc1b, gt1b, sh2b, sc2b, gt2b = mods[1]
    h2 = norm_mod_fwd(x2, norm_g[1, 0:1], sc1b, sh1b, "l1_norm1", after=tok)
    lru_w_in, lru_w_out = weights(st, h2, "lru")
    lru_w_out = lru_w_out.reshape(-1, D)
    pq = mm_nn(h2, lru_w_in, name="lru_in", bmode="cols")
    xc, xcb = lru_conv_fwd(pq, conv_w, conv_b, "lru_conv")
    pre_r = mm_nn(xcb, wsb_rg, name="lru_gate_r", bmode="batch")
    pre_i = mm_nn(xcb, wsb_ig, name="lru_gate_i", bmode="batch")
    hs, yv = lru_scan_fwd(pre_r, pre_i, xc, pq, b_rg, b_ig, lam, "lru_scan")
    st, tok = landed(3, hs, "ffn1")
    mix = mm_nn(yv, lru_w_out, name="lru_out")
    x3 = res_gate_fwd(x2, mix, gt1b, "lru_res")
    h3 = norm_mod_fwd(x3, norm_g[1, 1:2], sc2b, sh2b, "l1_norm2", after=tok)
    w_gu1, w_down1 = weights(st, h3, "ffn1")
    w_down1 = w_down1.reshape(-1, D)
    gu1 = mm_nn(h3, w_gu1, name="ffn1_gu", out_dtype=BF16, bmode="cols")
    x4, ffn1 = _ffn_fwd(x3, gu1, w_down1, gt2b, "ffn1")

    fg = p['final_g'].reshape(1, D)
    dx4, loss_blk, dfinal_g = final_loss(x4, fg, tgt, "final_loss")
    loss = lax.psum(loss_blk[0, 0], ("x", "y", "c"))

    def rows4(g):
        return g.reshape((N_CHIPS, -1) + g.shape[1:])

    dh3, dgt2b, dw_gu1, dw_down1 = _ffn_bwd(dx4, h3, ffn1, w_gu1, w_down1, gt2b, "ffn1")
    d2d_ffn1, tok = reduce_scatter_start([dw_gu1, dw_down1], "ffn1")
    dx3, dgn11, dsc2b, dsh2b = norm_mod_bwd(dh3, x3, norm_g[1, 1:2], sc2b, dx4, "l1_norm2_bwd", after=tok)

    dmix, dgt1b = res_gate_bwd(dx3, mix, gt1b, "lru_res_bwd")
    dyv = mm_nt(dmix, lru_w_out, name="lru_dyv")
    dw_out = mm_tn(yv, dmix, name="lru_dwout", out_dtype=BF16)
    rs_ffn1, tok = reduce_scatter_relay(d2d_ffn1, ci1, dyv, "ffn1")
    dgb, dpre_r, dpre_i, dxc1, db_rg, db_ig, dlam = lru_scan_bwd(dyv, hs, pre_r, pre_i, xc, pq, b_rg, b_ig, lam,
                                                                "lru_scan_bwd")
    dxc2 = mm_nt(dpre_r, wsb_rg, name="lru_dxc_r", bmode="batch", after=tok)
    dxc3 = mm_nt(dpre_i, wsb_ig, name="lru_dxc_i", bmode="batch")
    dwsb_rg = mm_tn(xcb, dpre_r, name="lru_dwgate_r", omode="batch", groups=nsb_lru)
    dwsb_ig = mm_tn(xcb, dpre_i, name="lru_dwgate_i", omode="batch", groups=nsb_lru)
    dxb, dconv_w, dconv_b = lru_conv_bwd(dxc1, dxc2, dxc3, pq, conv_w, "lru_conv_bwd")
    dpq = jnp.concatenate([dgb, dxb], axis=1)
    dh2 = mm_nt(dpq, lru_w_in, name="lru_dh", bmode="cols")
    dw_lru_in = mm_tn(h2, dpq, name="lru_dwin", out_dtype=BF16, omode="cols", groups=N_CHIPS)
    d2d_lru, tok = reduce_scatter_start([dw_lru_in, rows4(dw_out)], "lru")
    dx2, dgn10, dsc1b, dsh1b = norm_mod_bwd(dh2, x2, norm_g[1, 0:1], sc1b, dx3, "l1_norm1_bwd", after=tok)

    lru_relay = {}

    def relay_lru(after):
        lru_relay['state'], tok = reduce_scatter_relay(d2d_lru, ci1, after, "lru")
        return tok

    dh1, dgt2, dw_gu0, dw_down0 = _ffn_bwd(dx2, h1, ffn0, w_gu0, w_down0, gt2, "ffn0", relay=relay_lru)
    rs_lru = lru_relay['state']
    d2d_ffn0, tok = reduce_scatter_start([dw_gu0, dw_down0], "ffn0")
    rs_ffn0, tok = reduce_scatter_relay(d2d_ffn0, ci1, tok, "ffn0")
    dx1, dgn01, dsc2, dsh2 = norm_mod_bwd(dh1, x1, norm_g[0, 1:2], sc2, dx2, "l0_norm2_bwd", after=tok)

    dv, dgt1 = glu_res_bwd(dx1, v, gt1, "s5_res_bwd")
    dyg = mm_nt(dv, s5_w_glu, name="s5_dyg", bmode="cols")
    dw_glu = mm_tn(yg, dv, name="s5_dwglu", out_dtype=BF16, omode="cols", groups=N_CHIPS)
    du, dbsb, dcsb, da, dd = s5_scan_bwd(u, dyg, ypre, s_all, p['s5_d'], bsb, csb, crv, ss, "s5_scan_bwd")
    dh0 = mm_nt(du, s5_w_in, name="s5_dh")
    dw_s5_in = mm_tn(h0, du, name="s5_dwin", out_dtype=BF16)
    d2d_s5, tok = reduce_scatter_start([rows4(dw_s5_in), dw_glu], "s5")
    grad_x, dgn00, dsc1, dsh1 = norm_mod_bwd(dh0, x0, norm_g[0, 0:1], sc1, dx1, "l0_norm1_bwd", after=tok)

    dmod = jnp.concatenate([jnp.concatenate([dsh1, dsc1, dgt1, dsh2, dsc2, dgt2], axis=1),
                            jnp.concatenate([dsh1b, dsc1b, dgt1b, dsh2b, dsc2b, dgt2b], axis=1)], axis=0)
    dnorm_g = jnp.stack([jnp.concatenate([dgn00, dgn01]), jnp.concatenate([dgn10, dgn11])])
    dbb_re, dbb_im, dc_re, dc_im = _s5_blockdiag_grads(dbsb, dcsb, P, Cg)
    H = S5_SB_GROUPS * P
    da_re, da_im = da[:, 0, :H].reshape(G, P), da[:, 0, H:].reshape(G, P)
    dw_rg, dw_ig = _lru_blockdiag_grad(dwsb_rg, nb, bs), _lru_blockdiag_grad(dwsb_ig, nb, bs)
    small = [dmod, dnorm_g, da_re, da_im, dd, dconv_w, dconv_b, db_rg, db_ig, dlam, dfinal_g]
    small_shapes = [s.shape for s in small]
    payload = _pack(small, 1024)
    mid = [dbb_re, dbb_im, dc_re, dc_im, dw_rg, dw_ig]
    mid_shapes = [s.shape for s in mid]
    gathered = all_gather_devices([payload] + [s.reshape(s.shape[0], -1).astype(BF16) for s in mid],
                                  "gather_small_grads")
    gathered_small = gathered[0]
    rs_s5, tok_s5 = reduce_scatter_relay(d2d_s5, ci1, gathered_small, "s5")
    total = sum_devices(gathered_small, "sum_small_grads").reshape(-1)
    (s_dmod, s_norm_g, s_da_re, s_da_im, s_dd, s_conv_w, s_conv_b, s_b_rg, s_b_ig, s_lam,
     s_final_g) = _unpack(total, small_shapes)
    s_dbb_re, s_dbb_im, s_dc_re, s_dc_im, s_dw_rg, s_dw_ig = [
        sum_devices(g, f"sum_mid_grads_{i}").reshape(s) for i, (g, s) in enumerate(zip(gathered[1:], mid_shapes))]
    g_lam_re, g_lam_im, g_log_dt, g_b_re, g_b_im = s5_disc_vjp((s_da_re, s_da_im, s_dbb_re, s_dbb_im))

    npay = payload.shape[0] * payload.shape[1]
    dmod_all = gathered_small.reshape(N_DEV, npay)[:, :depth * N_MOD * D].reshape(N_DEV, depth, N_CHIPS, Nq)
    dmod_mine = lax.dynamic_index_in_dim(dmod_all, chip, axis=2, keepdims=False).reshape(N_DEV, depth * Nq)
    dmod_mine = jnp.pad(dmod_mine, ((0, 16 - N_DEV), (0, 0)))
    g_w_ada = mm_tn(cond_rep, dmod_mine, name="w_ada_grad", omode="batch", groups=depth)

    def cols(full, width):
        return lax.dynamic_slice_in_dim(full, chip * width, width, axis=full.ndim - 1)

    grads = {
        'norm_g': cols(s_norm_g, Dq), 'w_ada': g_w_ada, 'b_ada': s_dmod,
        's5_lam_re': g_lam_re, 's5_lam_im': g_lam_im, 's5_log_dt': g_log_dt, 's5_b_re': g_b_re, 's5_b_im': g_b_im,
        's5_c_re': s_dc_re, 's5_c_im': s_dc_im, 's5_d': s_dd, 'lru_conv_w': cols(s_conv_w, Eq),
        'lru_conv_b': cols(s_conv_b, Eq), 'lru_w_rg': s_dw_rg, 'lru_b_rg': cols(s_b_rg, Eq),
        'lru_w_ig': s_dw_ig, 'lru_b_ig': cols(s_b_ig, Eq), 'lru_lam': cols(s_lam, Eq), 'final_g': s_final_g,
    }
    grads = {k: g.reshape(p[k].shape) for k, g in grads.items()}

    delta, new_m, new_v = {}, {}, {}

    def adamw_2d(k, rows, after=None):
        w2 = p[k].reshape(rows, -1)
        outs = adamw(w2, grads[k].reshape(w2.shape), p['m_' + k].reshape(w2.shape), p['v_' + k].reshape(w2.shape),
                     f"adamw_{k}", after=after)
        delta[k], new_m[k], new_v[k] = [o.reshape(p[k].shape) for o in outs]

    adamw_2d('w_ada', depth * D, after=tok_s5)
    for k in _MID:
        adamw_2d(k, p[k].shape[1])
    rest = [k for k in _W_NAMES if k not in _BIG + _MID]
    shapes = [p[k].shape for k in rest]
    packed = [_pack([src[pre_ + k] if pre_ else src[k] for k in rest], 1024)
              for src, pre_ in ((p, ''), (grads, ''), (p, 'm_'), (p, 'v_'))]
    outs = adamw(*packed, "adamw_small")
    for dst, o in zip((delta, new_m, new_v), outs):
        for k, val in zip(rest, _unpack(o.reshape(-1), shapes)):
            dst[k] = val

    done = delta['w_ada']
    halves = []
    for state, tag in ((rs_ffn1, "ffn1"), (rs_lru, "lru"), (rs_ffn0, "ffn0"), (rs_s5, "s5")):
        halves += reduce_scatter_finish(state, place2, done, tag)
    g_gu1, g_down1, g_lru_in, g_lru_out, g_gu0, g_down0, g_s5_in, g_s5_glu = join_halves(halves, "rs_join_halves")
    grads.update({'s5_w_in': g_s5_in[None], 's5_w_glu': g_s5_glu[None], 'lru_w_in': g_lru_in[None],
                  'lru_w_out': g_lru_out[None], 'ffn_w_gu': jnp.stack([g_gu0, g_gu1]),
                  'ffn_w_down': jnp.stack([g_down0, g_down1])})
    for k in _BIG[1:]:
        adamw_2d(k, math.prod(p[k].shape[:-1]))

    return (loss, grad_x[None], *[grads[k] for k in _W_NAMES], *[delta[k] for k in _W_NAMES],
            *[new_m[k] for k in _W_NAMES], *[new_v[k] for k in _W_NAMES])


_IN_NAMES = (['x', 'c'] + _W_NAMES + ['loss_target'] + ['m_' + k for k in _W_NAMES] + ['v_' + k for k in _W_NAMES])


def kernel(x, c, norm_g, w_ada, b_ada, s5_w_in, s5_lam_re, s5_lam_im, s5_log_dt, s5_b_re, s5_b_im, s5_c_re, s5_c_im, s5_d, s5_w_glu, lru_w_in, lru_conv_w, lru_conv_b, lru_w_rg, lru_b_rg, lru_w_ig, lru_b_ig, lru_lam, lru_w_out, ffn_w_gu, ffn_w_down, final_g, loss_target, m_norm_g, m_w_ada, m_b_ada, m_s5_w_in, m_s5_lam_re, m_s5_lam_im, m_s5_log_dt, m_s5_b_re, m_s5_b_im, m_s5_c_re, m_s5_c_im, m_s5_d, m_s5_w_glu, m_lru_w_in, m_lru_conv_w, m_lru_conv_b, m_lru_w_rg, m_lru_b_rg, m_lru_w_ig, m_lru_b_ig, m_lru_lam, m_lru_w_out, m_ffn_w_gu, m_ffn_w_down, m_final_g, v_norm_g, v_w_ada, v_b_ada, v_s5_w_in, v_s5_lam_re, v_s5_lam_im, v_s5_log_dt, v_s5_b_re, v_s5_b_im, v_s5_c_re, v_s5_c_im, v_s5_d, v_s5_w_glu, v_lru_w_in, v_lru_conv_w, v_lru_conv_b, v_lru_w_rg, v_lru_b_rg, v_lru_w_ig, v_lru_b_ig, v_lru_lam, v_lru_w_out, v_ffn_w_gu, v_ffn_w_down, v_final_g):
    args = (x, c, norm_g, w_ada, b_ada, s5_w_in, s5_lam_re, s5_lam_im, s5_log_dt, s5_b_re, s5_b_im, s5_c_re, s5_c_im, s5_d, s5_w_glu, lru_w_in, lru_conv_w, lru_conv_b, lru_w_rg, lru_b_rg, lru_w_ig, lru_b_ig, lru_lam, lru_w_out, ffn_w_gu, ffn_w_down, final_g, loss_target, m_norm_g, m_w_ada, m_b_ada, m_s5_w_in, m_s5_lam_re, m_s5_lam_im, m_s5_log_dt, m_s5_b_re, m_s5_b_im, m_s5_c_re, m_s5_c_im, m_s5_d, m_s5_w_glu, m_lru_w_in, m_lru_conv_w, m_lru_conv_b, m_lru_w_rg, m_lru_b_rg, m_lru_w_ig, m_lru_b_ig, m_lru_lam, m_lru_w_out, m_ffn_w_gu, m_ffn_w_down, m_final_g, v_norm_g, v_w_ada, v_b_ada, v_s5_w_in, v_s5_lam_re, v_s5_lam_im, v_s5_log_dt, v_s5_b_re, v_s5_b_im, v_s5_c_re, v_s5_c_im, v_s5_d, v_s5_w_glu, v_lru_w_in, v_lru_conv_w, v_lru_conv_b, v_lru_w_rg, v_lru_b_rg, v_lru_w_ig, v_lru_b_ig, v_lru_lam, v_lru_w_out, v_ffn_w_gu, v_ffn_w_down, v_final_g)
    return _step(dict(zip(_IN_NAMES, args)))
```

```python
import functools
import math

import jax
import jax.numpy as jnp
from jax import lax
from jax.experimental import pallas as pl
from jax.experimental.pallas import tpu as pltpu

F32 = jnp.float32
BF16 = jnp.bfloat16
MESH = pl.DeviceIdType.MESH

EPS = 1e-6
LRU_C = 8.0
N_MOD = 6
ADAM_LR = 0.001
ADAM_B1 = 0.9
ADAM_B2 = 0.999
ADAM_EPS = 1e-08
ADAM_WD = 0.01
ADAM_STEP = 10

N_CHIPS = 4
N_DEV = 8
SUBLANES = 8
LANES = 128
S5_SB_GROUPS = 8
V7X_VMEM_LIMIT = 48 * 1024 * 1024
ROW_TILE_ELEMS = 512 * 1024
ROW_CALL_ELEMS = 4 * 1024 * 1024
SCAN_UNROLL = 2
LRU_TILE = 256

_TM = (1024, 1408, 512, 256, 128, 64, 32, 16, 8)
_TN = (1024, 1408, 512, 384, 256, 128)
_TK = (1024, 1408, 512, 256, 128)
_TR = (256, 128, 64, 32, 16, 8)

_GELU_K0 = math.sqrt(2.0 / math.pi)
_GELU_K1 = 0.044715


def _tile(n, cands):
    for t in cands:
        if n % t == 0:
            return t
    return n


def _cp(sem=None):
    return pltpu.CompilerParams(dimension_semantics=sem, vmem_limit_bytes=V7X_VMEM_LIMIT)


def _sds(shape, dtype):
    return jax.ShapeDtypeStruct(shape, dtype)


def _sig(x):
    return 1.0 / (1.0 + jnp.exp(-x))


def _gelu(x):
    t = jnp.tanh(_GELU_K0 * (x + _GELU_K1 * x * x * x))
    return 0.5 * x * (1.0 + t)


def _gelu_grad(x):
    x2 = x * x
    t = jnp.tanh(_GELU_K0 * (x + _GELU_K1 * x * x2))
    return 0.5 * (1.0 + t) + 0.5 * x * (1.0 - t * t) * _GELU_K0 * (1.0 + 3.0 * _GELU_K1 * x2)


def _softplus(z):
    return jnp.maximum(z, 0.0) + jnp.log(1.0 + jnp.exp(-jnp.abs(z)))


def _neg_expm1(x):
    series = -x * (1.0 + x * (0.5 + x * (1.0 / 6.0 + x * (1.0 / 24.0))))
    return jnp.where(x > -0.05, series, 1.0 - jnp.exp(x))


def _row(x, r):
    return x[r:r + 1, :]


def _colsum(x):
    return jnp.sum(x, axis=0, keepdims=True)


_NN = (((1,), (0,)), ((), ()))
_NT = (((1,), (1,)), ((), ()))
_TN_DIMS = (((0,), (0,)), ((), ()))


def _mm_call(name, a, b, a_spec, b_spec, o_spec, grid, out_shape, acc_shape, dims, after=None):
    nk = grid[-1]
    kaxis = len(grid) - 1
    extra = [] if after is None else [after]

    def body(a_ref, b_ref, *rest):
        o_ref, acc_ref = rest[-2], rest[-1]
        k = pl.program_id(kaxis)

        def prod():
            return lax.dot_general(a_ref[...].astype(BF16), b_ref[...].astype(BF16), dims,
                                   preferred_element_type=F32)

        if nk == 1:
            o_ref[...] = prod().astype(o_ref.dtype)
            return

        @pl.when(k == 0)
        def _():
            acc_ref[...] = prod()

        if nk > 2:
            @pl.when(jnp.logical_and(k > 0, k < nk - 1))
            def _():
                acc_ref[...] += prod()

        @pl.when(k == nk - 1)
        def _():
            o_ref[...] = (acc_ref[...] + prod()).astype(o_ref.dtype)

    return pl.pallas_call(
        body, name=name, grid=grid, in_specs=[a_spec, b_spec] + [pl.BlockSpec(memory_space=pl.ANY)] * len(extra),
        out_specs=o_spec, out_shape=out_shape, scratch_shapes=[pltpu.VMEM(acc_shape, F32)],
        compiler_params=_cp(("parallel", "parallel", "parallel", "arbitrary")),
    )(a, b, *extra)


def mm_nn(a, b, *, name, out_dtype=F32, bmode="plain"):
    M = a.shape[0]
    if bmode == "plain":
        G, S = 1, 1
        K, Nc = b.shape
    elif bmode == "cols":
        G = 1
        S, K, Nc = b.shape
    else:
        S = 1
        G, K, Nc = b.shape
    tm, tn, tk = _tile(M, _TM), _tile(Nc, _TN), _tile(K, _TK)
    nkb, nnb = K // tk, Nc // tn
    ncol = S * nnb
    grid = (G, M // tm, ncol, nkb)
    a_spec = pl.BlockSpec((tm, tk), lambda g, i, j, k: (i, g * nkb + k))
    if bmode == "plain":
        b_spec = pl.BlockSpec((tk, tn), lambda g, i, j, k: (k, j))
    elif bmode == "cols":
        b_spec = pl.BlockSpec((None, tk, tn), lambda g, i, j, k: (j // nnb, k, j % nnb))
    else:
        b_spec = pl.BlockSpec((None, tk, tn), lambda g, i, j, k: (g, k, j))
    o_spec = pl.BlockSpec((tm, tn), lambda g, i, j, k: (i, g * ncol + j))
    return _mm_call(name, a, b, a_spec, b_spec, o_spec, grid, _sds((M, G * S * Nc), out_dtype), (tm, tn), _NN)


def mm_nt(a, b, *, name, out_dtype=F32, bmode="plain", after=None):
    M = a.shape[0]
    if bmode == "plain":
        G, S = 1, 1
        Ko, Nc = b.shape
    elif bmode == "cols":
        G = 1
        S, Ko, Nc = b.shape
    else:
        S = 1
        G, Ko, Nc = b.shape
    tm, to, tc = _tile(M, _TM), _tile(Ko, _TN), _tile(Nc, _TK)
    npc = Nc // tc
    nc = S * npc
    nob = Ko // to
    grid = (G, M // tm, nob, nc)
    a_spec = pl.BlockSpec((tm, tc), lambda g, i, j, n: (i, g * nc + n))
    if bmode == "plain":
        b_spec = pl.BlockSpec((to, tc), lambda g, i, j, n: (j, n))
    elif bmode == "cols":
        b_spec = pl.BlockSpec((None, to, tc), lambda g, i, j, n: (n // npc, j, n % npc))
    else:
        b_spec = pl.BlockSpec((None, to, tc), lambda g, i, j, n: (g, j, n))
    o_spec = pl.BlockSpec((tm, to), lambda g, i, j, n: (i, g * nob + j))
    return _mm_call(name, a, b, a_spec, b_spec, o_spec, grid, _sds((M, G * Ko), out_dtype), (tm, to), _NT,
                    after=after)


def mm_tn(a, b, *, name, out_dtype=F32, omode="plain", groups=1):
    L = a.shape[0]
    G = groups if omode == "batch" else 1
    S = groups if omode == "cols" else 1
    Mo, N = a.shape[1] // G, b.shape[1] // G
    Nc = N // S
    tm, tn, tl = _tile(Mo, _TM), _tile(Nc, _TN), _tile(L, _TK)
    nmb, nnb = Mo // tm, N // tn
    npj = Nc // tn
    grid = (G, nmb, nnb, L // tl)
    a_spec = pl.BlockSpec((tl, tm), lambda g, i, j, l: (l, g * nmb + i))
    b_spec = pl.BlockSpec((tl, tn), lambda g, i, j, l: (l, g * nnb + j))
    if omode == "plain":
        o_spec = pl.BlockSpec((tm, tn), lambda g, i, j, l: (i, j))
        oshape = (Mo, N)
    elif omode == "cols":
        o_spec = pl.BlockSpec((None, tm, tn), lambda g, i, j, l: (j // npj, i, j % npj))
        oshape = (S, Mo, Nc)
    else:
        o_spec = pl.BlockSpec((None, tm, tn), lambda g, i, j, l: (g, i, j))
        oshape = (G, Mo, N)
    return _mm_call(name, a, b, a_spec, b_spec, o_spec, grid, _sds(oshape, out_dtype), (tm, tn), _TN_DIMS)


def _row_call(name, body, row_ins, vec_ins, row_outs, acc_outs=(), after=None):
    if after is not None:
        n_in = len(row_ins) + len(vec_ins)
        inner = body

        def body(*refs):
            inner(*refs[:n_in], *refs[n_in + 1:])

        return _row_call_impl(name, body, row_ins, vec_ins, row_outs, acc_outs, [after])
    return _row_call_impl(name, body, row_ins, vec_ins, row_outs, acc_outs, [])


def _row_call_impl(name, body, row_ins, vec_ins, row_outs, acc_outs, extra):
    L = row_ins[0].shape[0]
    wmax = max([a.shape[1] for a in row_ins] + [w for w, _ in row_outs])
    narr = len(row_ins) + len(row_outs)
    tr = _tile(L, tuple(t for t in _TR if t * wmax * narr <= ROW_CALL_ELEMS) or (SUBLANES,))
    in_specs = [pl.BlockSpec((tr, a.shape[1]), lambda i: (i, 0)) for a in row_ins]
    in_specs += [pl.BlockSpec(v.shape, lambda i, nd=v.ndim: (0,) * nd) for v in vec_ins]
    in_specs += [pl.BlockSpec(memory_space=pl.ANY) for _ in extra]
    out_shape = [_sds((L, w), dt) for w, dt in row_outs] + [_sds(s, dt) for s, dt in acc_outs]
    out_specs = [pl.BlockSpec((tr, w), lambda i: (i, 0)) for w, _ in row_outs]
    out_specs += [pl.BlockSpec(s, lambda i, nd=len(s): (0,) * nd) for s, _ in acc_outs]
    sem = ("arbitrary",) if acc_outs else ("parallel",)
    return pl.pallas_call(body, name=name, grid=(L // tr,), in_specs=in_specs, out_specs=out_specs,
                          out_shape=out_shape, compiler_params=_cp(sem))(*row_ins, *vec_ins, *extra)


def silu_rows(x, name, after=None):
    def body(x_ref, o_ref):
        v = x_ref[...]
        o_ref[...] = (v * _sig(v)).astype(o_ref.dtype)
    return _row_call(name, body, [x], [], [(x.shape[1], BF16)], after=after)[0]


def norm_mod_fwd(x, gain, sc, sh, name, after=None):
    def body(x_ref, g_ref, sc_ref, sh_ref, h_ref):
        v = x_ref[...]
        r = lax.rsqrt(jnp.mean(v * v, axis=-1, keepdims=True) + EPS)
        h_ref[...] = (v * r * g_ref[...] * (1.0 + sc_ref[...]) + sh_ref[...]).astype(BF16)
    return _row_call(name, body, [x], [gain, sc, sh], [(x.shape[1], BF16)], after=after)[0]


def norm_mod_bwd(dh, x, gain, sc, dres, name, after=None):
    D = x.shape[1]

    def body(dh_ref, x_ref, dres_ref, g_ref, sc_ref, dx_ref, dg_ref, dsc_ref, dsh_ref):
        @pl.when(pl.program_id(0) == 0)
        def _():
            dg_ref[...] = jnp.zeros_like(dg_ref)
            dsc_ref[...] = jnp.zeros_like(dsc_ref)
            dsh_ref[...] = jnp.zeros_like(dsh_ref)

        v = x_ref[...]
        dh_v = dh_ref[...]
        g = g_ref[...]
        r = lax.rsqrt(jnp.mean(v * v, axis=-1, keepdims=True) + EPS)
        xhat = v * r
        dn = dh_v * (1.0 + sc_ref[...])
        dsc_ref[...] += _colsum(dh_v * xhat * g)
        dsh_ref[...] += _colsum(dh_v)
        dg_ref[...] += _colsum(dn * xhat)
        t = dn * g
        dx_ref[...] = dres_ref[...] + r * (t - xhat * jnp.mean(t * xhat, axis=-1, keepdims=True))

    acc = [((1, D), F32)] * 3
    return _row_call(name, body, [dh, x, dres], [gain, sc], [(D, F32)], acc, after=after)


def final_loss(x, gain, tgt, name):
    D = x.shape[1]

    def body(x_ref, t_ref, g_ref, dx_ref, loss_ref, dg_ref, acc_ref):
        i = pl.program_id(0)

        @pl.when(i == 0)
        def _():
            dg_ref[...] = jnp.zeros_like(dg_ref)
            acc_ref[...] = jnp.zeros_like(acc_ref)

        v = x_ref[...]
        g = g_ref[...]
        r = lax.rsqrt(jnp.mean(v * v, axis=-1, keepdims=True) + EPS)
        xhat = v * r
        err = xhat * g - t_ref[...]
        acc_ref[...] += _colsum(err * err)
        dout = err * (1.0 / D)
        dg_ref[...] += _colsum(dout * xhat)
        t = dout * g
        dx_ref[...] = r * (t - xhat * jnp.mean(t * xhat, axis=-1, keepdims=True))

        @pl.when(i == pl.num_programs(0) - 1)
        def _():
            loss_ref[...] = jnp.zeros_like(loss_ref) + jnp.sum(acc_ref[...]) * (0.5 / D)

    return _row_call(name, body, [x, tgt], [gain], [(D, F32)],
                     [((SUBLANES, LANES), F32), ((1, D), F32), ((1, D), F32)])[:3]


def res_gate_fwd(x, z, g, name):
    def body(x_ref, z_ref, g_ref, o_ref):
        o_ref[...] = x_ref[...] + g_ref[...] * z_ref[...]
    return _row_call(name, body, [x, z], [g], [(x.shape[1], F32)])[0]


def res_gate_bwd(dx, z, g, name):
    D = dx.shape[1]

    def body(dx_ref, z_ref, g_ref, dz_ref, dg_ref):
        @pl.when(pl.program_id(0) == 0)
        def _():
            dg_ref[...] = jnp.zeros_like(dg_ref)
        d = dx_ref[...]
        dz_ref[...] = (g_ref[...] * d).astype(BF16)
        dg_ref[...] += _colsum(d * z_ref[...])
    return _row_call(name, body, [dx, z], [g], [(D, BF16)], [((1, D), F32)])


def glu_res_fwd(x, v, g, name):
    D = x.shape[1]

    def body(x_ref, v_ref, g_ref, o_ref):
        vv = v_ref[...]
        o_ref[...] = x_ref[...] + g_ref[...] * (vv[:, :D] * _sig(vv[:, D:]))
    return _row_call(name, body, [x, v], [g], [(D, F32)])[0]


def glu_res_bwd(dx, v, g, name, after=None):
    D = dx.shape[1]

    def body(dx_ref, v_ref, g_ref, dv_ref, dg_ref):
        @pl.when(pl.program_id(0) == 0)
        def _():
            dg_ref[...] = jnp.zeros_like(dg_ref)
        d = dx_ref[...]
        vv = v_ref[...]
        val = vv[:, :D]
        s = _sig(vv[:, D:])
        dg_ref[...] += _colsum(d * val * s)
        dm = g_ref[...] * d
        dv_ref[:, :D] = (dm * s).astype(BF16)
        dv_ref[:, D:] = (dm * val * s * (1.0 - s)).astype(BF16)
    return _row_call(name, body, [dx, v], [g], [(2 * D, BF16)], [((1, D), F32)], after=after)


def swiglu_fwd(gu, name):
    F = gu.shape[1] // 2

    def body(gu_ref, o_ref):
        v = gu_ref[...].astype(F32)
        g = v[:, :F]
        o_ref[...] = (g * _sig(g) * v[:, F:]).astype(BF16)
    return _row_call(name, body, [gu], [], [(F, BF16)])[0]


def swiglu_bwd(dact, gu, name, after=None):
    F = gu.shape[1] // 2

    def body(da_ref, gu_ref, o_ref):
        v = gu_ref[...].astype(F32)
        g, u = v[:, :F], v[:, F:]
        da = da_ref[...]
        s = _sig(g)
        o_ref[:, :F] = (da * u * s * (1.0 + g * (1.0 - s))).astype(BF16)
        o_ref[:, F:] = (da * g * s).astype(BF16)
    return _row_call(name, body, [dact, gu], [], [(2 * F, BF16)], after=after)[0]


def adamw(w, g, m, v, name, after=None):
    C = w.shape[1]
    c1 = 1.0 - ADAM_B1 ** ADAM_STEP
    c2 = 1.0 - ADAM_B2 ** ADAM_STEP

    def body(w_ref, g_ref, m_ref, v_ref, d_ref, m2_ref, v2_ref):
        gv = g_ref[...]
        m2 = ADAM_B1 * m_ref[...] + (1.0 - ADAM_B1) * gv
        v2 = ADAM_B2 * v_ref[...] + (1.0 - ADAM_B2) * (gv * gv)
        m2_ref[...] = m2
        v2_ref[...] = v2
        d_ref[...] = -ADAM_LR * ((m2 / c1) / (jnp.sqrt(v2 / c2) + ADAM_EPS) + ADAM_WD * w_ref[...])
    return _row_call(name, body, [w, g, m, v], [], [(C, F32)] * 3, after=after)


def sum_devices(parts, name):
    n, R, C = parts.shape
    min_rows = 16 if parts.dtype == BF16 else SUBLANES
    tr = _tile(R, tuple(t for t in _TR if t * C * n <= 4 * ROW_TILE_ELEMS and t >= min_rows) or (min_rows,))

    def body(p_ref, o_ref):
        acc = p_ref[0].astype(F32)
        for d in range(1, n):
            acc = acc + p_ref[d].astype(F32)
        o_ref[...] = acc
    return pl.pallas_call(body, name=name, grid=(R // tr,),
                          in_specs=[pl.BlockSpec((n, tr, C), lambda i: (0, i, 0))],
                          out_specs=pl.BlockSpec((tr, C), lambda i: (i, 0)), out_shape=_sds((R, C), F32),
                          compiler_params=_cp(("parallel",)))(parts)


def _s5_discretize(lam_re, lam_im, log_dt, b_re, b_im):
    dt = jnp.exp(log_dt)[:, None]
    mag = jnp.exp(lam_re * dt)
    ab_re = mag * jnp.cos(lam_im * dt)
    ab_im = mag * jnp.sin(lam_im * dt)
    nr, ni = ab_re - 1.0, ab_im
    den = lam_re * lam_re + lam_im * lam_im
    f_re = (nr * lam_re + ni * lam_im) / den
    f_im = (ni * lam_re - nr * lam_im) / den
    bb_re = f_re[..., None] * b_re - f_im[..., None] * b_im
    bb_im = f_re[..., None] * b_im + f_im[..., None] * b_re
    return ab_re, ab_im, bb_re, bb_im


def _s5_blockdiag(bb_re, bb_im, c_re, c_im):
    G, P, Cg = bb_re.shape
    nsb = G // S5_SB_GROUPS

    def bmat(bb):
        return _block_diag(jnp.swapaxes(bb.reshape(nsb, S5_SB_GROUPS, P, Cg), 2, 3))

    def cmat(cc):
        return _block_diag(jnp.swapaxes(cc.reshape(nsb, S5_SB_GROUPS, Cg, P), 2, 3))

    bsb = jnp.concatenate([bmat(bb_re), bmat(bb_im)], axis=-1)
    csb = jnp.concatenate([cmat(c_re), -cmat(c_im)], axis=1)
    return bsb, csb


def _block_diag(t):
    ng, b = t.shape[1], t.shape[3]
    rows = [jnp.pad(t[:, g], ((0, 0), (0, 0), (g * b, (ng - 1 - g) * b))) for g in range(ng)]
    return jnp.concatenate(rows, axis=1)


def _diag_blocks(m, ng):
    a, b = m.shape[1] // ng, m.shape[2] // ng
    return jnp.stack([m[:, g * a:(g + 1) * a, g * b:(g + 1) * b] for g in range(ng)], axis=1)


def _s5_blockdiag_grads(dbsb, dcsb, P, Cg):
    H = S5_SB_GROUPS * P

    def blocks(m):
        return jnp.swapaxes(_diag_blocks(m, S5_SB_GROUPS), 2, 3)

    dbb_re = blocks(dbsb[:, :, :H]).reshape(-1, P, Cg)
    dbb_im = blocks(dbsb[:, :, H:]).reshape(-1, P, Cg)
    dc_re = blocks(dcsb[:, :H, :]).reshape(-1, Cg, P)
    dc_im = -blocks(dcsb[:, H:, :]).reshape(-1, Cg, P)
    return dbb_re, dbb_im, dc_re, dc_im


def _s5_scan_consts(ab_re, ab_im):
    G, P = ab_re.shape
    nsb = G // S5_SB_GROUPS
    H = S5_SB_GROUPS * P
    ar, ai = ab_re.reshape(nsb, 1, H), ab_im.reshape(nsb, 1, H)
    pows = [(ar, ai)]
    for _ in range(SUBLANES - 1):
        pr, pi_ = pows[-1]
        pows.append((pr * ar - pi_ * ai, pr * ai + pi_ * ar))
    rows = jnp.arange(SUBLANES).reshape(1, SUBLANES, 1)

    def masked(k, keep):
        pr, pi_ = pows[k - 1]
        return jnp.where(keep, pr, 0.0), jnp.where(keep, pi_, 0.0)

    def per_row(sel):
        pr = jnp.concatenate([pows[sel(r) - 1][0] for r in range(SUBLANES)], axis=1)
        pi_ = jnp.concatenate([pows[sel(r) - 1][1] for r in range(SUBLANES)], axis=1)
        return pr, pi_

    fwd = [masked(1, rows >= 1), masked(2, rows >= 2), masked(4, rows >= 4), per_row(lambda r: r + 1)]
    rev = [masked(1, rows < 7), masked(2, rows < 6), masked(4, rows < 4), per_row(lambda r: SUBLANES - r)]

    def pack(lst, conj):
        sgn = -1.0 if conj else 1.0
        return jnp.stack([jnp.concatenate([jnp.broadcast_to(pr, (nsb, SUBLANES, H)),
                                           sgn * jnp.broadcast_to(pi_, (nsb, SUBLANES, H))], axis=-1)
                          for pr, pi_ in lst], axis=1)

    return pack(fwd, False), pack(rev, True)


def _cmadd(xr, xi, ar, ai, yr, yi):
    return xr + ar * yr - ai * yi, xi + ar * yi + ai * yr


def _s5_scan_fwd_loop(src_ref, dst_ref, cf_ref, cr, ci, nblk, H):
    def body(k, carry):
        cr, ci = carry
        r0 = pl.multiple_of(k * SUBLANES, SUBLANES)
        xr = src_ref[pl.ds(r0, SUBLANES), pl.ds(0, H)]
        xi = src_ref[pl.ds(r0, SUBLANES), pl.ds(H, H)]
        for idx, d in enumerate((1, 2, 4)):
            xr, xi = _cmadd(xr, xi, cf_ref[idx, :, pl.ds(0, H)], cf_ref[idx, :, pl.ds(H, H)],
                            pltpu.roll(xr, d, 0), pltpu.roll(xi, d, 0))
        xr, xi = _cmadd(xr, xi, cf_ref[3, :, pl.ds(0, H)], cf_ref[3, :, pl.ds(H, H)], cr, ci)
        dst_ref[pl.ds(r0, SUBLANES), pl.ds(0, H)] = xr
        dst_ref[pl.ds(r0, SUBLANES), pl.ds(H, H)] = xi
        return _row(xr, SUBLANES - 1), _row(xi, SUBLANES - 1)

    return lax.fori_loop(0, nblk, body, (cr, ci))


def s5_scan_fwd(u, d_skip, bsb, csb, cf, name):
    L, W = u.shape
    nsb, GW, H2 = bsb.shape
    H = H2 // 2
    Tc = _tile(L, (512, 256, 128, 64, 32, 16, 8))
    nch = L // Tc

    def body(u_ref, d_ref, b_ref, c_ref, cf_ref, ypre_ref, yg_ref, s_ref, ss_ref, bu_scr, car_scr):
        @pl.when(pl.program_id(1) == 0)
        def _():
            car_scr[...] = jnp.zeros_like(car_scr)

        ss_ref[...] = car_scr[...]
        ub = u_ref[...]
        bu_scr[...] = jnp.dot(ub.astype(BF16), b_ref[...], preferred_element_type=F32)
        cr, ci = _s5_scan_fwd_loop(bu_scr, s_ref, cf_ref, car_scr[:, pl.ds(0, H)], car_scr[:, pl.ds(H, H)],
                                   Tc // SUBLANES, H)
        car_scr[:, pl.ds(0, H)] = cr
        car_scr[:, pl.ds(H, H)] = ci
        ypre = jnp.dot(s_ref[...].astype(BF16), c_ref[...], preferred_element_type=F32) + d_ref[...] * ub
        ypre_ref[...] = ypre
        yg_ref[...] = _gelu(ypre).astype(BF16)

    return pl.pallas_call(
        body, name=name, grid=(nsb, nch),
        in_specs=[pl.BlockSpec((Tc, GW), lambda j, i: (i, j)),
                  pl.BlockSpec((1, GW), lambda j, i: (0, j)),
                  pl.BlockSpec((None, GW, H2), lambda j, i: (j, 0, 0)),
                  pl.BlockSpec((None, H2, GW), lambda j, i: (j, 0, 0)),
                  pl.BlockSpec((None, 4, SUBLANES, H2), lambda j, i: (j, 0, 0, 0))],
        out_specs=[pl.BlockSpec((Tc, GW), lambda j, i: (i, j)),
                   pl.BlockSpec((Tc, GW), lambda j, i: (i, j)),
                   pl.BlockSpec((Tc, H2), lambda j, i: (i, j)),
                   pl.BlockSpec((None, None, 1, H2), lambda j, i: (i, j, 0, 0))],
        out_shape=[_sds((L, W), F32), _sds((L, W), BF16), _sds((L, nsb * H2), F32),
                   _sds((nch, nsb, 1, H2), F32)],
        scratch_shapes=[pltpu.VMEM((Tc, H2), F32), pltpu.VMEM((1, H2), F32)],
        compiler_params=_cp(("arbitrary", "arbitrary")),
    )(u, d_skip, bsb.astype(BF16), csb.astype(BF16), cf)


def s5_scan_bwd(u, dyg, ypre, s_all, d_skip, bsb, csb, crv, ss, name):
    L, W = u.shape
    nsb, GW, H2 = bsb.shape
    H = H2 // 2
    Tc = _tile(L, (512, 256, 128, 64, 32, 16, 8))
    nch = L // Tc
    nblk = Tc // SUBLANES
    bsb_t = jnp.swapaxes(bsb, 1, 2).astype(BF16)
    csb_t = jnp.swapaxes(csb, 1, 2).astype(BF16)

    def body(u_ref, dyg_ref, yp_ref, s_ref, d_ref, bt_ref, ct_ref, crv_ref, ss_ref,
             du_ref, db_ref, dc_ref, da_ref, dd_ref, g_scr, gcar_scr):
        @pl.when(pl.program_id(1) == 0)
        def _():
            gcar_scr[...] = jnp.zeros_like(gcar_scr)
            db_ref[...] = jnp.zeros_like(db_ref)
            dc_ref[...] = jnp.zeros_like(dc_ref)
            da_ref[...] = jnp.zeros_like(da_ref)
            dd_ref[...] = jnp.zeros_like(dd_ref)

        ub = u_ref[...]
        ubf = ub.astype(BF16)
        dyp = dyg_ref[...] * _gelu_grad(yp_ref[...])
        dypb = dyp.astype(BF16)
        dd_ref[...] += _colsum(dyp * ub)
        g_scr[...] = jnp.dot(dypb, ct_ref[...], preferred_element_type=F32)
        rows = lax.broadcasted_iota(jnp.int32, (SUBLANES, H), 0)

        def rev(kk, carry):
            gr, gi, acc_r, acc_i = carry
            k = nblk - 1 - kk
            r0 = pl.multiple_of(k * SUBLANES, SUBLANES)
            xr = g_scr[pl.ds(r0, SUBLANES), pl.ds(0, H)]
            xi = g_scr[pl.ds(r0, SUBLANES), pl.ds(H, H)]
            for idx, d in enumerate((1, 2, 4)):
                xr, xi = _cmadd(xr, xi, crv_ref[idx, :, pl.ds(0, H)], crv_ref[idx, :, pl.ds(H, H)],
                                pltpu.roll(xr, SUBLANES - d, 0), pltpu.roll(xi, SUBLANES - d, 0))
            xr, xi = _cmadd(xr, xi, crv_ref[3, :, pl.ds(0, H)], crv_ref[3, :, pl.ds(H, H)], gr, gi)
            g_scr[pl.ds(r0, SUBLANES), pl.ds(0, H)] = xr
            g_scr[pl.ds(r0, SUBLANES), pl.ds(H, H)] = xi
            rp = pl.multiple_of(jnp.maximum(k - 1, 0) * SUBLANES, SUBLANES)
            first = k == 0
            pr = jnp.where(first, ss_ref[:, pl.ds(0, H)], _row(s_ref[pl.ds(rp, SUBLANES), pl.ds(0, H)], SUBLANES - 1))
            pi_ = jnp.where(first, ss_ref[:, pl.ds(H, H)], _row(s_ref[pl.ds(rp, SUBLANES), pl.ds(H, H)], SUBLANES - 1))
            spr = jnp.where(rows == 0, pr, pltpu.roll(s_ref[pl.ds(r0, SUBLANES), pl.ds(0, H)], 1, 0))
            spi = jnp.where(rows == 0, pi_, pltpu.roll(s_ref[pl.ds(r0, SUBLANES), pl.ds(H, H)], 1, 0))
            return (_row(xr, 0), _row(xi, 0), acc_r + xr * spr + xi * spi, acc_i + xi * spr - xr * spi)

        zero = jnp.zeros((SUBLANES, H), F32)
        gr, gi, acc_r, acc_i = lax.fori_loop(
            0, nblk, rev, (gcar_scr[:, pl.ds(0, H)], gcar_scr[:, pl.ds(H, H)], zero, zero))
        gcar_scr[:, pl.ds(0, H)] = gr
        gcar_scr[:, pl.ds(H, H)] = gi
        da_ref[:, pl.ds(0, H)] += _colsum(acc_r)
        da_ref[:, pl.ds(H, H)] += _colsum(acc_i)
        gb = g_scr[...].astype(BF16)
        db_ref[...] += lax.dot_general(ubf, gb, _TN_DIMS, preferred_element_type=F32)
        dc_ref[...] += lax.dot_general(s_ref[...].astype(BF16), dypb, _TN_DIMS, preferred_element_type=F32)
        du_ref[...] = (jnp.dot(gb, bt_ref[...], preferred_element_type=F32) + d_ref[...] * dyp).astype(BF16)

    rmap = lambda j, i: (nch - 1 - i, j)
    return pl.pallas_call(
        body, name=name, grid=(nsb, nch),
        in_specs=[pl.BlockSpec((Tc, GW), rmap), pl.BlockSpec((Tc, GW), rmap), pl.BlockSpec((Tc, GW), rmap),
                  pl.BlockSpec((Tc, H2), rmap),
                  pl.BlockSpec((1, GW), lambda j, i: (0, j)),
                  pl.BlockSpec((None, H2, GW), lambda j, i: (j, 0, 0)),
                  pl.BlockSpec((None, GW, H2), lambda j, i: (j, 0, 0)),
                  pl.BlockSpec((None, 4, SUBLANES, H2), lambda j, i: (j, 0, 0, 0)),
                  pl.BlockSpec((None, None, 1, H2), lambda j, i: (nch - 1 - i, j, 0, 0))],
        out_specs=[pl.BlockSpec((Tc, GW), rmap),
                   pl.BlockSpec((None, GW, H2), lambda j, i: (j, 0, 0)),
                   pl.BlockSpec((None, H2, GW), lambda j, i: (j, 0, 0)),
                   pl.BlockSpec((None, 1, H2), lambda j, i: (j, 0, 0)),
                   pl.BlockSpec((1, GW), lambda j, i: (0, j))],
        out_shape=[_sds((L, W), BF16), _sds((nsb, GW, H2), F32), _sds((nsb, H2, GW), F32),
                   _sds((nsb, 1, H2), F32), _sds((1, W), F32)],
        scratch_shapes=[pltpu.VMEM((Tc, H2), F32), pltpu.VMEM((1, H2), F32)],
        compiler_params=_cp(("arbitrary", "arbitrary")),
    )(u, dyg, ypre, s_all, d_skip, bsb_t, csb_t, crv, ss)


def _lru_blockdiag(w_rg, w_ig):
    nb, bs, _ = w_rg.shape
    sbw = bs * LANES // math.gcd(bs, LANES)
    bps = sbw // bs
    nsb = nb // bps

    def bd(w):
        return _block_diag(w.astype(BF16).reshape(nsb, bps, bs, bs))

    return bd(w_rg), bd(w_ig)


def _lru_blockdiag_grad(dwsb, nb, bs):
    return _diag_blocks(dwsb, dwsb.shape[1] // bs).reshape(nb, bs, bs)


def lru_conv_fwd(p, conv_w, conv_b, name):
    L = p.shape[0]
    E = conv_w.shape[1]
    tc = _tile(E, (256, 128))
    noff = E // tc
    kw = conv_w.shape[0]

    def body(xb_ref, w_ref, b_ref, xc_ref, xcb_ref):
        xb = xb_ref[...]
        rows = lax.broadcasted_iota(jnp.int32, xb.shape, 0)
        acc = w_ref[pl.ds(kw - 1, 1), :] * xb + b_ref[...]
        for k in range(kw - 1):
            sh = kw - 1 - k
            acc = acc + w_ref[pl.ds(k, 1), :] * jnp.where(rows >= sh, pltpu.roll(xb, sh, 0), 0.0)
        xc_ref[...] = acc
        xcb_ref[...] = acc.astype(BF16)

    return pl.pallas_call(
        body, name=name, grid=(noff,),
        in_specs=[pl.BlockSpec((L, tc), lambda t: (0, noff + t)),
                  pl.BlockSpec((kw, tc), lambda t: (0, t)), pl.BlockSpec((1, tc), lambda t: (0, t))],
        out_specs=[pl.BlockSpec((L, tc), lambda t: (0, t))] * 2,
        out_shape=[_sds((L, E), F32), _sds((L, E), BF16)],
        compiler_params=_cp(("parallel",)),
    )(p, conv_w, conv_b)


def lru_conv_bwd(d1, d2, d3, p, conv_w, name):
    L = p.shape[0]
    E = conv_w.shape[1]
    tc = _tile(E, (256, 128))
    noff = E // tc
    kw = conv_w.shape[0]

    def body(d1_ref, d2_ref, d3_ref, xb_ref, w_ref, dxb_ref, dw_ref, db_ref):
        dxc = d1_ref[...] + d2_ref[...] + d3_ref[...]
        xb = xb_ref[...]
        rows = lax.broadcasted_iota(jnp.int32, xb.shape, 0)
        db_ref[...] = _colsum(dxc)
        acc = w_ref[pl.ds(kw - 1, 1), :] * dxc
        dw_ref[pl.ds(kw - 1, 1), :] = _colsum(dxc * xb)
        for k in range(kw - 1):
            sh = kw - 1 - k
            dw_ref[pl.ds(k, 1), :] = _colsum(dxc * jnp.where(rows >= sh, pltpu.roll(xb, sh, 0), 0.0))
            acc = acc + w_ref[pl.ds(k, 1), :] * jnp.where(rows < L - sh, pltpu.roll(dxc, L - sh, 0), 0.0)
        dxb_ref[...] = acc.astype(BF16)

    return pl.pallas_call(
        body, name=name, grid=(noff,),
        in_specs=[pl.BlockSpec((L, tc), lambda t: (0, t))] * 3 +
                 [pl.BlockSpec((L, tc), lambda t: (0, noff + t)), pl.BlockSpec((kw, tc), lambda t: (0, t))],
        out_specs=[pl.BlockSpec((L, tc), lambda t: (0, t)), pl.BlockSpec((kw, tc), lambda t: (0, t)),
                   pl.BlockSpec((1, tc), lambda t: (0, t))],
        out_shape=[_sds((L, E), BF16), _sds((kw, E), F32), _sds((1, E), F32)],
        compiler_params=_cp(("parallel",)),
    )(d1, d2, d3, p, conv_w)


def _lru_gates(pr, pi_, brg, big, sp):
    r = _sig(pr + brg)
    ig = _sig(pi_ + big)
    la = -LRU_C * r * sp
    a = jnp.exp(la)
    mult = jnp.sqrt(_neg_expm1(2.0 * la))
    return r, ig, a, mult


def _lru_specs(L, E):
    tc = _tile(E, (LRU_TILE, LANES))
    col = pl.BlockSpec((L, tc), lambda t: (0, t))
    vec = pl.BlockSpec((1, tc), lambda t: (0, t))
    return tc, col, vec


def lru_scan_fwd(pre_r, pre_i, xc, p, b_rg, b_ig, lam, name):
    L, E = xc.shape
    tc, col, vec = _lru_specs(L, E)
    nblk = L // SUBLANES

    def body(pr_ref, pi_ref, xc_ref, gb_ref, brg_ref, big_ref, lam_ref, hs_ref, yv_ref):
        sp = _softplus(-lam_ref[...])
        brg, big = brg_ref[...], big_ref[...]
        rows = lax.broadcasted_iota(jnp.int32, (SUBLANES, tc), 0)

        def blk(k, carry):
            r0 = pl.multiple_of(k * SUBLANES, SUBLANES)
            sl = pl.ds(r0, SUBLANES)
            _, ig, a, mult = _lru_gates(pr_ref[sl, :], pi_ref[sl, :], brg, big, sp)
            b = mult * ig * xc_ref[sl, :]
            for d in (1, 2, 4):
                keep = rows >= d
                b = b + a * jnp.where(keep, pltpu.roll(b, d, 0), 0.0)
                a = a * jnp.where(keep, pltpu.roll(a, d, 0), 1.0)
            h = b + a * carry
            hs_ref[sl, :] = h
            return _row(h, SUBLANES - 1)

        def trip(kt, carry):
            for q in range(SCAN_UNROLL):
                carry = blk(kt * SCAN_UNROLL + q, carry)
            return carry

        lax.fori_loop(0, nblk // SCAN_UNROLL, trip, jnp.zeros((1, tc), F32))
        yv_ref[...] = (hs_ref[...] * _gelu(gb_ref[...])).astype(BF16)

    return pl.pallas_call(
        body, name=name, grid=(E // tc,),
        in_specs=[col, col, col, col, vec, vec, vec],
        out_specs=[col, col], out_shape=[_sds((L, E), F32), _sds((L, E), BF16)],
        compiler_params=_cp(("parallel",)),
    )(pre_r, pre_i, xc, p, b_rg, b_ig, lam)


def lru_scan_bwd(dyv, hs, pre_r, pre_i, xc, p, b_rg, b_ig, lam, name):
    L, E = xc.shape
    tc, col, vec = _lru_specs(L, E)
    nblk = L // SUBLANES

    def body(dyv_ref, hs_ref, pr_ref, pi_ref, xc_ref, gb_ref, brg_ref, big_ref, lam_ref,
             dgb_ref, dpr_ref, dpi_ref, dxc_ref, dbrg_ref, dbig_ref, dlam_ref, t_gb, t_pr, t_pi):
        lam_v = lam_ref[...]
        sp = _softplus(-lam_v)
        brg, big = brg_ref[...], big_ref[...]
        rows = lax.broadcasted_iota(jnp.int32, (SUBLANES, tc), 0)

        def blk(kk, carry):
            gcar, a_next, acc_sp, acc_r, acc_i = carry
            k = nblk - 1 - kk
            r0 = pl.multiple_of(k * SUBLANES, SUBLANES)
            sl = pl.ds(r0, SUBLANES)
            r, ig, a, mult = _lru_gates(pr_ref[sl, :], pi_ref[sl, :], brg, big, sp)
            gbv, hsv, dyvv, xcv = gb_ref[sl, :], hs_ref[sl, :], dyv_ref[sl, :], xc_ref[sl, :]
            t_gb[sl, :] = dyvv * hsv * _gelu_grad(gbv)
            x = dyvv * _gelu(gbv)
            al = jnp.where(rows == SUBLANES - 1, a_next, pltpu.roll(a, SUBLANES - 1, 0))
            for d in (1, 2, 4):
                keep = rows < SUBLANES - d
                x = x + al * jnp.where(keep, pltpu.roll(x, SUBLANES - d, 0), 0.0)
                al = al * jnp.where(keep, pltpu.roll(al, SUBLANES - d, 0), 1.0)
            g = x + al * gcar
            rp = pl.multiple_of(jnp.maximum(k - 1, 0) * SUBLANES, SUBLANES)
            hlast = _row(hs_ref[pl.ds(rp, SUBLANES), :], SUBLANES - 1) * (k > 0).astype(F32)
            hprev = jnp.where(rows == 0, hlast, pltpu.roll(hsv, 1, 0))
            da = g * hprev
            dmult = g * ig * xcv
            dig = g * mult * xcv
            dxc_ref[sl, :] = g * mult * ig
            dla = da * a - dmult * (a * a) / mult
            dpr = dla * (-LRU_C * sp) * r * (1.0 - r)
            dpi = dig * ig * (1.0 - ig)
            t_pr[sl, :] = dpr
            t_pi[sl, :] = dpi
            return (_row(g, 0), _row(a, 0), acc_sp + dla * (-LRU_C * r), acc_r + dpr, acc_i + dpi)

        zero = jnp.zeros((SUBLANES, tc), F32)
        z1 = jnp.zeros((1, tc), F32)
        def trip(kt, carry):
            for q in range(SCAN_UNROLL):
                carry = blk(kt * SCAN_UNROLL + q, carry)
            return carry

        _, _, acc_sp, acc_r, acc_i = lax.fori_loop(0, nblk // SCAN_UNROLL, trip, (z1, z1, zero, zero, zero))
        dgb_ref[...] = t_gb[...].astype(BF16)
        dpr_ref[...] = t_pr[...].astype(BF16)
        dpi_ref[...] = t_pi[...].astype(BF16)
        dbrg_ref[...] = _colsum(acc_r)
        dbig_ref[...] = _colsum(acc_i)
        dlam_ref[...] = -_colsum(acc_sp) * _sig(-lam_v)

    return pl.pallas_call(
        body, name=name, grid=(E // tc,),
        in_specs=[col, col, col, col, col, col, vec, vec, vec],
        out_specs=[col, col, col, col, vec, vec, vec],
        out_shape=[_sds((L, E), BF16), _sds((L, E), BF16), _sds((L, E), BF16), _sds((L, E), F32),
                   _sds((1, E), F32), _sds((1, E), F32), _sds((1, E), F32)],
        scratch_shapes=[pltpu.VMEM((L, tc), F32)] * 3,
        compiler_params=_cp(("parallel",)),
    )(dyv, hs, pre_r, pre_i, xc, p, b_rg, b_ig, lam)


def _place():
    xi, yi, ci = lax.axis_index("x"), lax.axis_index("y"), lax.axis_index("c")
    chips = [(1 - xi, yi), (xi, 1 - yi), (1 - xi, 1 - yi)]
    return xi, yi, ci, chips


_ANY = pl.BlockSpec(memory_space=pl.ANY)


def all_gather_devices(blks, name):
    n = len(blks)

    def body(*refs):
        ins, outs = refs[:n], refs[n:2 * n]
        send_sems, recv_sems, local_sems = refs[2 * n:]
        xi, yi, ci, chips = _place()
        me, sibling = (xi, yi, ci), (xi, yi, 1 - ci)

        def slab(a, px, py, pc):
            return outs[a].at[4 * px + 2 * py + pc]

        def copy(a, k, block, to, src=None):
            return pltpu.make_async_remote_copy(
                src_ref=slab(a, *block) if src is None else src, dst_ref=slab(a, *block),
                send_sem=send_sems.at[7 * a + k], recv_sem=recv_sems.at[7 * a + k], device_id=to,
                device_id_type=MESH)

        mine = [pltpu.make_async_copy(ins[a], slab(a, *me), local_sems.at[a]) for a in range(n)]
        first, passed = [], []
        for a in range(n):
            mine[a].start()
            first.append(copy(a, 0, me, sibling, src=ins[a]))
            first += [copy(a, 1 + j, me, (*chip, ci), src=ins[a]) for j, chip in enumerate(chips)]
        for cp in first:
            cp.start()
        for a in range(n):
            for j, chip in enumerate(chips):
                copy(a, 1 + j, (*chip, ci), me).wait_recv()
                passed.append(copy(a, 4 + j, (*chip, ci), sibling))
                passed[-1].start()
        for a in range(n):
            copy(a, 0, sibling, me).wait_recv()
            for j, chip in enumerate(chips):
                copy(a, 4 + j, (*chip, 1 - ci), me).wait_recv()
        for cp in first + passed:
            cp.wait_send()
        for cp in mine:
            cp.wait()

    return pl.pallas_call(
        body, name=name, in_specs=[_ANY] * n, out_specs=[_ANY] * n,
        out_shape=[_sds((N_DEV,) + b.shape, b.dtype) for b in blks],
        scratch_shapes=[pltpu.SemaphoreType.DMA((7 * n,)), pltpu.SemaphoreType.DMA((7 * n,)),
                        pltpu.SemaphoreType.DMA((n,))],
    )(*blks)


_HBM = pl.BlockSpec(memory_space=pltpu.HBM)
_SEM = pl.BlockSpec(memory_space=pltpu.SEMAPHORE)
_EFFECT = pltpu.SideEffectType.DATAFLOW_SIDE_EFFECTING


def split_start(name, groups, counts, copies_fn):
    flat = [b for g in groups for b in g]
    n, ng = len(flat), len(groups)

    def body(*refs):
        ins, sems, token = refs[:n], refs[n:n + 2 * ng], refs[-1]
        off = 0
        for gi, g in enumerate(groups):
            for cp in copies_fn(ins[off:off + len(g)], [b.shape for b in g], sems[2 * gi], sems[2 * gi + 1]):
                cp.start()
            off += len(g)
        token[...] = jnp.zeros_like(token)

    out_shape = tuple(pltpu.SemaphoreType.DMA((c,)) for c in counts for _ in range(2))
    out_shape += tuple(pltpu.HBM(b.shape, b.dtype) for b in flat) + (_sds((SUBLANES, LANES), F32),)
    outs = pl.pallas_call(
        body, name=name, in_specs=[_HBM] * n, out_shape=out_shape,
        out_specs=tuple([_SEM] * (2 * ng) + [_HBM] * n + [pl.BlockSpec(memory_space=pltpu.VMEM)]),
        input_output_aliases={i: 2 * ng + i for i in range(n)},
        compiler_params=pltpu.CompilerParams(has_side_effects=_EFFECT),
    )(*[pltpu.with_memory_space_constraint(b, pltpu.HBM) for b in flat])
    sems = [(outs[2 * gi], outs[2 * gi + 1]) for gi in range(ng)]
    thru, off = [], 2 * ng
    for g in groups:
        thru.append(list(outs[off:off + len(g)]))
        off += len(g)
    return sems, thru, outs[-1]


def split_wait(name, bufs, sems, copies_fn, after):
    n = len(bufs)
    shapes = [b.shape for b in bufs]

    def body(*refs):
        for cp in copies_fn(refs[:n], shapes, refs[n], refs[n + 1]):
            cp.wait_send()
            cp.wait_recv()

    return list(pl.pallas_call(
        body, name=name, in_specs=[_HBM] * n + [_SEM, _SEM, _ANY],
        out_shape=tuple(pltpu.HBM(b.shape, b.dtype) for b in bufs), out_specs=tuple([_HBM] * n),
        input_output_aliases={i: i for i in range(n)},
        compiler_params=pltpu.CompilerParams(has_side_effects=_EFFECT),
    )(*bufs, sems[0], sems[1], after))


def _gather_copies(bufs, shapes, send_sems, recv_sems):
    xi, yi, ci, chips = _place()
    cps = []
    for a, ref in enumerate(bufs):
        hr = shapes[a][1] // 2
        rows = ref.at[2 * xi + yi, pl.ds(pl.multiple_of(ci * hr, 16), hr), :]
        for j in range(3):
            cps.append(pltpu.make_async_remote_copy(
                src_ref=rows, dst_ref=rows, send_sem=send_sems.at[3 * a + j], recv_sem=recv_sems.at[3 * a + j],
                device_id=(*chips[j], ci), device_id_type=MESH))
    return cps


def _scatter_copies(bufs, shapes, send_sems, recv_sems):
    xi, yi, ci, chips = _place()
    n = len(bufs) // 2
    cps = []
    for a in range(n):
        for j in range(3):
            cps.append(pltpu.make_async_remote_copy(
                src_ref=bufs[a].at[2 * chips[j][0] + chips[j][1]], dst_ref=bufs[n + a].at[j],
                send_sem=send_sems.at[3 * a + j], recv_sem=recv_sems.at[3 * a + j],
                device_id=(*chips[j], ci), device_id_type=MESH))
    return cps


def cast_place(w, layer, chip1, name, after=None):
    _, R, C = w.shape
    tr = _tile(R, tuple(t for t in _TR if t * C <= ROW_TILE_ELEMS) or (16,))
    extra = [] if after is None else [after]

    def body(c_ref, w_ref, *rest):
        rest[-1][...] = w_ref[...].astype(BF16)

    return pl.pallas_call(
        body, name=name,
        grid_spec=pltpu.PrefetchScalarGridSpec(
            num_scalar_prefetch=1, grid=(R // tr,),
            in_specs=[pl.BlockSpec((None, tr, C), lambda i, c: (layer, i, 0))] + [_ANY] * len(extra),
            out_specs=pl.BlockSpec((None, tr, C), lambda i, c: (c[0], i, 0))),
        out_shape=_sds((N_CHIPS, R, C), BF16), compiler_params=_cp(("parallel",)),
    )(chip1, w, *extra)


def _forward_copies(bufs, shapes, send_sems, recv_sems):
    xi, yi, ci, chips = _place()
    cps = []
    for a, ref in enumerate(bufs):
        hr = shapes[a][1] // 2
        for j in range(3):
            rows = ref.at[2 * chips[j][0] + chips[j][1], pl.ds(pl.multiple_of(ci * hr, 16), hr), :]
            cps.append(pltpu.make_async_remote_copy(
                src_ref=rows, dst_ref=rows, send_sem=send_sems.at[3 * a + j], recv_sem=recv_sems.at[3 * a + j],
                device_id=(xi, yi, 1 - ci), device_id_type=MESH))
    return cps


def _halves_copies(bufs, shapes, send_sems, recv_sems):
    xi, yi, ci, _ = _place()
    n = len(bufs) // 2
    cps = []
    for a in range(n):
        hr = shapes[a][1] // 2
        cps.append(pltpu.make_async_remote_copy(
            src_ref=bufs[a].at[:, pl.ds(pl.multiple_of((1 - ci) * hr, 16), hr), :], dst_ref=bufs[n + a],
            send_sem=send_sems.at[a], recv_sem=recv_sems.at[a], device_id=(xi, yi, 1 - ci), device_id_type=MESH))
    return cps


def add_half(g, got, ci, name):
    S, hr, C = got.shape
    tr = _tile(hr, tuple(t for t in _TR if t * C <= ROW_TILE_ELEMS) or (16,))
    nb = hr // tr

    def body(c_ref, g_ref, r_ref, o_ref):
        o_ref[...] = (g_ref[...].astype(F32) + r_ref[...].astype(F32)).astype(BF16)

    return pl.pallas_call(
        body, name=name,
        grid_spec=pltpu.PrefetchScalarGridSpec(
            num_scalar_prefetch=1, grid=(S, nb),
            in_specs=[pl.BlockSpec((None, tr, C), lambda s, i, c: (s, c[0] * nb + i, 0)),
                      pl.BlockSpec((None, tr, C), lambda s, i, c: (s, i, 0))],
            out_specs=pl.BlockSpec((None, tr, C), lambda s, i, c: (s, i, 0))),
        out_shape=_sds((S, hr, C), BF16), compiler_params=_cp(("parallel", "parallel")),
    )(ci, g, got)


def add_chips(part, got, place, name):
    S, hr, C = part.shape
    tr = _tile(hr, tuple(t for t in _TR if t * C <= ROW_TILE_ELEMS) or (16,))
    nb = hr // tr

    def body(c_ref, p_ref, r_ref, o_ref):
        acc = p_ref[...].astype(F32)
        for j in range(3):
            acc = acc + r_ref[j].astype(F32)
        o_ref[...] = acc

    return pl.pallas_call(
        body, name=name,
        grid_spec=pltpu.PrefetchScalarGridSpec(
            num_scalar_prefetch=1, grid=(nb,),
            in_specs=[pl.BlockSpec((None, tr, C), lambda i, c: (c[0], i, 0)),
                      pl.BlockSpec((3, tr, C), lambda i, c: (0, i, 0))],
            out_specs=pl.BlockSpec((tr, C), lambda i, c: (c[1] * nb + i, 0))),
        out_shape=_sds((2 * hr, C), F32), compiler_params=_cp(("parallel",)),
    )(place, part, got)


def join_halves(bufs, name):
    n = len(bufs)

    def body(*refs):
        ins, outs = refs[:n], refs[n:2 * n]
        send_sems, recv_sems = refs[2 * n:]
        xi, yi, ci, _ = _place()

        def copy(ref, a, h):
            hr = bufs[a].shape[0] // 2
            rows = pl.ds(pl.multiple_of(h * hr, 8), hr)
            return pltpu.make_async_remote_copy(
                src_ref=ref[a].at[rows, :], dst_ref=outs[a].at[rows, :], send_sem=send_sems.at[a],
                recv_sem=recv_sems.at[a], device_id=(xi, yi, 1 - ci), device_id_type=MESH)

        sends = [copy(ins, a, ci) for a in range(n)]
        for cp in sends:
            cp.start()
        for a in range(n):
            copy(outs, a, 1 - ci).wait_recv()
        for cp in sends:
            cp.wait_send()

    return pl.pallas_call(
        body, name=name, in_specs=[_ANY] * n, out_specs=[_ANY] * n,
        out_shape=[_sds(b.shape, b.dtype) for b in bufs], input_output_aliases={i: i for i in range(n)},
        scratch_shapes=[pltpu.SemaphoreType.DMA((n,)), pltpu.SemaphoreType.DMA((n,))],
    )(*bufs)


def reduce_scatter_start(grads, tag):
    lands = [lax.empty((N_CHIPS, g.shape[1] // 2, g.shape[2]), g.dtype) for g in grads]
    sems, thru, token = split_start(f"rs_d2d_start_{tag}", [list(grads) + lands], [len(grads)], _halves_copies)
    return (sems[0], thru[0]), token


def reduce_scatter_relay(state, ci1, after, tag):
    sems, bufs = state
    bufs = split_wait(f"rs_d2d_wait_{tag}", bufs, sems, _halves_copies, after)
    n = len(bufs) // 2
    parts = [add_half(bufs[a], bufs[n + a], ci1, f"rs_add_half_{tag}{a}") for a in range(n)]
    lands = [lax.empty((3,) + q.shape[1:], q.dtype) for q in parts]
    sems, thru, token = split_start(f"rs_ici_start_{tag}", [parts + lands], [3 * n], _scatter_copies)
    return (sems[0], thru[0]), token


def reduce_scatter_finish(state, place, after, tag):
    sems, bufs = state
    bufs = split_wait(f"rs_ici_wait_{tag}", bufs, sems, _scatter_copies, after)
    n = len(bufs) // 2
    return [add_chips(bufs[a], bufs[n + a], place, f"rs_add_chips_{tag}{a}") for a in range(n)]


def _pack(parts, width):
    flat = jnp.concatenate([p.reshape(-1).astype(F32) for p in parts])
    per = SUBLANES * width
    total = -(-flat.shape[0] // per) * per
    flat = jnp.pad(flat, (0, total - flat.shape[0]))
    return flat.reshape(total // width, width)


def _unpack(flat, shapes):
    out, off = [], 0
    for s in shapes:
        n = math.prod(s)
        out.append(flat[off:off + n].reshape(s))
        off += n
    return out


_W_NAMES = ['norm_g', 'w_ada', 'b_ada', 's5_w_in', 's5_lam_re', 's5_lam_im', 's5_log_dt', 's5_b_re', 's5_b_im',
            's5_c_re', 's5_c_im', 's5_d', 's5_w_glu', 'lru_w_in', 'lru_conv_w', 'lru_conv_b', 'lru_w_rg',
            'lru_b_rg', 'lru_w_ig', 'lru_b_ig', 'lru_lam', 'lru_w_out', 'ffn_w_gu', 'ffn_w_down', 'final_g']
_BIG = ('w_ada', 's5_w_in', 's5_w_glu', 'lru_w_in', 'lru_w_out', 'ffn_w_gu', 'ffn_w_down')
_MID = ('s5_b_re', 's5_b_im', 's5_c_re', 's5_c_im', 'lru_w_rg', 'lru_w_ig')


def _ffn_fwd(x, gu, w_down, gate, tag):
    act = swiglu_fwd(gu, f"{tag}_act")
    z = mm_nn(act, w_down, name=f"{tag}_down")
    return res_gate_fwd(x, z, gate, f"{tag}_res"), (gu, act, z)


def _ffn_bwd(dx, h, saved, w_gu, w_down, gate, tag, relay=None):
    gu, act, z = saved
    dz, dgate = res_gate_bwd(dx, z, gate, f"{tag}_res_bwd")
    dact = mm_nt(dz, w_down, name=f"{tag}_dact")
    tok = None if relay is None else relay(dact)
    dw_down = mm_tn(act, dz, name=f"{tag}_dwdown", out_dtype=BF16)
    dgu = swiglu_bwd(dact, gu, f"{tag}_act_bwd", after=tok)
    dw_gu = mm_tn(h, dgu, name=f"{tag}_dwgu", out_dtype=BF16, omode="cols", groups=N_CHIPS)
    started, tok = reduce_scatter_start([dw_gu, dw_down.reshape((N_CHIPS, -1) + dw_down.shape[1:])], tag)
    dh = mm_nt(dgu, w_gu, name=f"{tag}_dh", bmode="cols", after=tok)
    return dh, dgate, started


def _step(p):
    xi, yi, ci = lax.axis_index("x"), lax.axis_index("y"), lax.axis_index("c")
    chip = 2 * xi + yi
    me = 2 * chip + ci
    ci1 = jnp.reshape(ci, (1,)).astype(jnp.int32)
    chip1 = jnp.reshape(chip, (1,)).astype(jnp.int32)
    place2 = jnp.stack([chip, ci]).astype(jnp.int32)

    x0 = p['x'][0]
    tgt = p['loss_target'][0]
    L, D = x0.shape
    Dq = D // N_CHIPS
    depth = p['w_ada'].shape[0]
    E = p['lru_lam'].shape[1] * N_CHIPS
    Eq = E // N_CHIPS
    kw = p['lru_conv_w'].shape[1]
    Nq = p['w_ada'].shape[2]
    _, G, P, Cg = p['s5_b_re'].shape
    nb, bs = p['lru_w_rg'].shape[1], p['lru_w_rg'].shape[2]

    pay = _pack([p['c'], p['norm_g'], p['lru_conv_w'], p['lru_conv_b'], p['lru_b_rg'], p['lru_b_ig'],
                 p['lru_lam']], 1024)
    g1 = all_gather_devices([pay], "gather_small_params")[0].reshape(N_DEV, -1)
    c_all = g1[:, :D]
    per_chip = g1[0::2]
    sizes = [(depth, 2, Dq), (kw, Eq), (Eq,), (Eq,), (Eq,), (Eq,)]
    offs = D
    pieces = []
    for s in sizes:
        nel = math.prod(s)
        pieces.append(per_chip[:, offs:offs + nel].reshape((N_CHIPS,) + s))
        offs += nel
    norm_g = jnp.moveaxis(pieces[0], 0, 2).reshape(depth, 2, D)
    conv_w = jnp.moveaxis(pieces[1], 0, 1).reshape(kw, E)
    conv_b, b_rg, b_ig, lam = [q.reshape(1, E) for q in pieces[2:]]

    cond = silu_rows(jnp.pad(c_all, ((0, 16 - N_DEV), (0, 0))), "cond_silu")
    cond_rep = jnp.concatenate([cond] * depth, axis=1)
    mod_part = mm_nn(cond_rep, p['w_ada'], name="mod_proj", bmode="batch")[:N_DEV]
    g2 = all_gather_devices([mod_part], "gather_mod")[0][0::2]
    mine = lax.dynamic_index_in_dim(g2, me, axis=1, keepdims=False).reshape(N_CHIPS, depth, Nq)
    mod = jnp.moveaxis(mine, 0, 1).reshape(depth, N_CHIPS * Nq) + p['b_ada']
    mods = [[mod[i:i + 1, k * D:(k + 1) * D] for k in range(N_MOD)] for i in range(depth)]

    def place(key, layer, tag, after=None):
        return cast_place(p[key], layer, chip1, f"place_{tag}", after=after)

    first = [place('s5_w_in', 0, "s5_in", after=mod), place('s5_w_glu', 0, "s5_glu")]
    sems_a, bufs_a, tok_a = split_start("gather_ici_start_s5", [[w] for w in first], [3] * len(first), _gather_copies)
    others = [[place('ffn_w_gu', 0, "gu0", after=tok_a), place('ffn_w_down', 0, "down0")],
              [place('lru_w_in', 0, "lru_in"), place('lru_w_out', 0, "lru_out")],
              [place('ffn_w_gu', 1, "gu1"), place('ffn_w_down', 1, "down1")]]
    sems_b, bufs_b, tok_b = split_start("gather_ici_start_rest", others, [3 * len(g) for g in others], _gather_copies)
    wsems, wbufs = sems_a + sems_b, bufs_a + bufs_b

    def landed(gi, after, tag):
        bufs = split_wait(f"gather_ici_wait_{tag}", wbufs[gi], wsems[gi], _gather_copies, after)
        sems, thru, tok = split_start(f"gather_d2d_start_{tag}", [bufs], [3 * len(bufs)], _forward_copies)
        return (sems[0], thru[0]), tok

    def weights(state, after, tag):
        return split_wait(f"gather_d2d_wait_{tag}", state[1], state[0], _forward_copies, after)

    s5_small = (p['s5_lam_re'][0], p['s5_lam_im'][0], p['s5_log_dt'][0], p['s5_b_re'][0], p['s5_b_im'][0])
    (ab_re, ab_im, bb_re, bb_im), s5_disc_vjp = jax.vjp(_s5_discretize, *s5_small)
    bsb, csb = _s5_blockdiag(bb_re, bb_im, p['s5_c_re'][0], p['s5_c_im'][0])
    cf, crv = _s5_scan_consts(ab_re, ab_im)
    wsb_rg, wsb_ig = [w.astype(BF16) for w in _lru_blockdiag(p['lru_w_rg'][0], p['lru_w_ig'][0])]
    nsb_lru = wsb_rg.shape[0]

    sh1, sc1, gt1, sh2, sc2, gt2 = mods[0]
    st, tok = landed(0, tok_b, "s5_in")
    s5_w_in = weights(st, tok, "s5_in")[0].reshape(-1, D)
    h0 = norm_mod_fwd(x0, norm_g[0, 0:1], sc1, sh1, "l0_norm1")
    u = mm_nn(h0, s5_w_in, name="s5_in")
    st, tok = landed(1, u, "s5_glu")
    ypre, yg, s_all, ss = s5_scan_fwd(u, p['s5_d'], bsb, csb, cf, "s5_scan")
    s5_w_glu = weights(st, yg, "s5_glu")[0]
    v = mm_nn(yg, s5_w_glu, name="s5_glu", bmode="cols")
    st, tok = landed(2, v, "ffn0")
    x1 = glu_res_fwd(x0, v, gt1, "s5_res")
    h1 = norm_mod_fwd(x1, norm_g[0, 1:2], sc2, sh2, "l0_norm2", after=tok)
    w_gu0, w_down0 = weights(st, h1, "ffn0")
    w_down0 = w_down0.reshape(-1, D)
    gu0 = mm_nn(h1, w_gu0, name="ffn0_gu", out_dtype=BF16, bmode="cols")
    st, tok = landed(3, gu0, "lru")
    x2, ffn0 = _ffn_fwd(x1, gu0, w_down0, gt2, "ffn0")

    sh1b, sc1b, gt1b, sh2b, sc2b, gt2b = mods[1]
    h2 = norm_mod_fwd(x2, norm_g[1, 0:1], sc1b, sh1b, "l1_norm1", after=tok)
    lru_w_in, lru_w_out = weights(st, h2, "lru")
    lru_w_out = lru_w_out.reshape(-1, D)
    pq = mm_nn(h2, lru_w_in, name="lru_in", bmode="cols")
    xc, xcb = lru_conv_fwd(pq, conv_w, conv_b, "lru_conv")
    pre_r = mm_nn(xcb, wsb_rg, name="lru_gate_r", bmode="batch")
    pre_i = mm_nn(xcb, wsb_ig, name="lru_gate_i", bmode="batch")
    hs, yv = lru_scan_fwd(pre_r, pre_i, xc, pq, b_rg, b_ig, lam, "lru_scan")
    st, tok = landed(4, hs, "ffn1")
    mix = mm_nn(yv, lru_w_out, name="lru_out")
    x3 = res_gate_fwd(x2, mix, gt1b, "lru_res")
    h3 = norm_mod_fwd(x3, norm_g[1, 1:2], sc2b, sh2b, "l1_norm2", after=tok)
    w_gu1, w_down1 = weights(st, h3, "ffn1")
    w_down1 = w_down1.reshape(-1, D)
    gu1 = mm_nn(h3, w_gu1, name="ffn1_gu", out_dtype=BF16, bmode="cols")
    x4, ffn1 = _ffn_fwd(x3, gu1, w_down1, gt2b, "ffn1")

    fg = p['final_g'].reshape(1, D)
    dx4, loss_blk, dfinal_g = final_loss(x4, fg, tgt, "final_loss")
    loss = lax.psum(loss_blk[0, 0], ("x", "y", "c"))

    def rows4(g):
        return g.reshape((N_CHIPS, -1) + g.shape[1:])

    dh3, dgt2b, d2d_ffn1 = _ffn_bwd(dx4, h3, ffn1, w_gu1, w_down1, gt2b, "ffn1")
    rs_ffn1, tok = reduce_scatter_relay(d2d_ffn1, ci1, dh3, "ffn1")
    dx3, dgn11, dsc2b, dsh2b = norm_mod_bwd(dh3, x3, norm_g[1, 1:2], sc2b, dx4, "l1_norm2_bwd", after=tok)

    dmix, dgt1b = res_gate_bwd(dx3, mix, gt1b, "lru_res_bwd")
    dyv = mm_nt(dmix, lru_w_out, name="lru_dyv")
    dw_out = mm_tn(yv, dmix, name="lru_dwout", out_dtype=BF16)
    dgb, dpre_r, dpre_i, dxc1, db_rg, db_ig, dlam = lru_scan_bwd(dyv, hs, pre_r, pre_i, xc, pq, b_rg, b_ig, lam,
                                                                "lru_scan_bwd")
    dxc2 = mm_nt(dpre_r, wsb_rg, name="lru_dxc_r", bmode="batch")
    dxc3 = mm_nt(dpre_i, wsb_ig, name="lru_dxc_i", bmode="batch")
    dwsb_rg = mm_tn(xcb, dpre_r, name="lru_dwgate_r", omode="batch", groups=nsb_lru)
    dwsb_ig = mm_tn(xcb, dpre_i, name="lru_dwgate_i", omode="batch", groups=nsb_lru)
    dxb, dconv_w, dconv_b = lru_conv_bwd(dxc1, dxc2, dxc3, pq, conv_w, "lru_conv_bwd")
    dpq = jnp.concatenate([dgb, dxb], axis=1)
    dh2 = mm_nt(dpq, lru_w_in, name="lru_dh", bmode="cols")
    dw_lru_in = mm_tn(h2, dpq, name="lru_dwin", out_dtype=BF16, omode="cols", groups=N_CHIPS)
    d2d_lru, tok = reduce_scatter_start([dw_lru_in, rows4(dw_out)], "lru")
    dx2, dgn10, dsc1b, dsh1b = norm_mod_bwd(dh2, x2, norm_g[1, 0:1], sc1b, dx3, "l1_norm1_bwd", after=tok)

    lru_relay = {}

    def relay_lru(after):
        lru_relay['state'], tok = reduce_scatter_relay(d2d_lru, ci1, after, "lru")
        return tok

    dh1, dgt2, d2d_ffn0 = _ffn_bwd(dx2, h1, ffn0, w_gu0, w_down0, gt2, "ffn0", relay=relay_lru)
    rs_lru = lru_relay['state']
    rs_ffn0, tok = reduce_scatter_relay(d2d_ffn0, ci1, dh1, "ffn0")
    dx1, dgn01, dsc2, dsh2 = norm_mod_bwd(dh1, x1, norm_g[0, 1:2], sc2, dx2, "l0_norm2_bwd", after=tok)

    dv, dgt1 = glu_res_bwd(dx1, v, gt1, "s5_res_bwd")
    dyg = mm_nt(dv, s5_w_glu, name="s5_dyg", bmode="cols")
    dw_glu = mm_tn(yg, dv, name="s5_dwglu", out_dtype=BF16, omode="cols", groups=N_CHIPS)
    du, dbsb, dcsb, da, dd = s5_scan_bwd(u, dyg, ypre, s_all, p['s5_d'], bsb, csb, crv, ss, "s5_scan_bwd")
    dh0 = mm_nt(du, s5_w_in, name="s5_dh")
    dw_s5_in = mm_tn(h0, du, name="s5_dwin", out_dtype=BF16)
    d2d_s5, tok = reduce_scatter_start([rows4(dw_s5_in), dw_glu], "s5")
    grad_x, dgn00, dsc1, dsh1 = norm_mod_bwd(dh0, x0, norm_g[0, 0:1], sc1, dx1, "l0_norm1_bwd", after=tok)

    dmod = jnp.concatenate([jnp.concatenate([dsh1, dsc1, dgt1, dsh2, dsc2, dgt2], axis=1),
                            jnp.concatenate([dsh1b, dsc1b, dgt1b, dsh2b, dsc2b, dgt2b], axis=1)], axis=0)
    dnorm_g = jnp.stack([jnp.concatenate([dgn00, dgn01]), jnp.concatenate([dgn10, dgn11])])
    dbb_re, dbb_im, dc_re, dc_im = _s5_blockdiag_grads(dbsb, dcsb, P, Cg)
    H = S5_SB_GROUPS * P
    da_re, da_im = da[:, 0, :H].reshape(G, P), da[:, 0, H:].reshape(G, P)
    dw_rg, dw_ig = _lru_blockdiag_grad(dwsb_rg, nb, bs), _lru_blockdiag_grad(dwsb_ig, nb, bs)
    small = [dmod, dnorm_g, da_re, da_im, dd, dconv_w, dconv_b, db_rg, db_ig, dlam, dfinal_g]
    small_shapes = [s.shape for s in small]
    payload = _pack(small, 1024)
    mid = [dbb_re, dbb_im, dc_re, dc_im, dw_rg, dw_ig]
    mid_shapes = [s.shape for s in mid]
    gathered = all_gather_devices([payload] + [s.reshape(s.shape[0], -1).astype(BF16) for s in mid],
                                  "gather_small_grads")
    gathered_small = gathered[0]
    rs_s5, tok_s5 = reduce_scatter_relay(d2d_s5, ci1, gathered_small, "s5")
    total = sum_devices(gathered_small, "sum_small_grads").reshape(-1)
    (s_dmod, s_norm_g, s_da_re, s_da_im, s_dd, s_conv_w, s_conv_b, s_b_rg, s_b_ig, s_lam,
     s_final_g) = _unpack(total, small_shapes)
    s_dbb_re, s_dbb_im, s_dc_re, s_dc_im, s_dw_rg, s_dw_ig = [
        sum_devices(g, f"sum_mid_grads_{i}").reshape(s) for i, (g, s) in enumerate(zip(gathered[1:], mid_shapes))]
    g_lam_re, g_lam_im, g_log_dt, g_b_re, g_b_im = s5_disc_vjp((s_da_re, s_da_im, s_dbb_re, s_dbb_im))

    npay = payload.shape[0] * payload.shape[1]
    dmod_all = gathered_small.reshape(N_DEV, npay)[:, :depth * N_MOD * D].reshape(N_DEV, depth, N_CHIPS, Nq)
    dmod_mine = lax.dynamic_index_in_dim(dmod_all, chip, axis=2, keepdims=False).reshape(N_DEV, depth * Nq)
    dmod_mine = jnp.pad(dmod_mine, ((0, 16 - N_DEV), (0, 0)))
    g_w_ada = mm_tn(cond_rep, dmod_mine, name="w_ada_grad", omode="batch", groups=depth)

    def cols(full, width):
        return lax.dynamic_slice_in_dim(full, chip * width, width, axis=full.ndim - 1)

    grads = {
        'norm_g': cols(s_norm_g, Dq), 'w_ada': g_w_ada, 'b_ada': s_dmod,
        's5_lam_re': g_lam_re, 's5_lam_im': g_lam_im, 's5_log_dt': g_log_dt, 's5_b_re': g_b_re, 's5_b_im': g_b_im,
        's5_c_re': s_dc_re, 's5_c_im': s_dc_im, 's5_d': s_dd, 'lru_conv_w': cols(s_conv_w, Eq),
        'lru_conv_b': cols(s_conv_b, Eq), 'lru_w_rg': s_dw_rg, 'lru_b_rg': cols(s_b_rg, Eq),
        'lru_w_ig': s_dw_ig, 'lru_b_ig': cols(s_b_ig, Eq), 'lru_lam': cols(s_lam, Eq), 'final_g': s_final_g,
    }
    grads = {k: g.reshape(p[k].shape) for k, g in grads.items()}

    delta, new_m, new_v = {}, {}, {}

    def adamw_2d(k, rows, after=None):
        w2 = p[k].reshape(rows, -1)
        outs = adamw(w2, grads[k].reshape(w2.shape), p['m_' + k].reshape(w2.shape), p['v_' + k].reshape(w2.shape),
                     f"adamw_{k}", after=after)
        delta[k], new_m[k], new_v[k] = [o.reshape(p[k].shape) for o in outs]

    adamw_2d('w_ada', depth * D, after=tok_s5)
    for k in _MID:
        adamw_2d(k, p[k].shape[1])
    rest = [k for k in _W_NAMES if k not in _BIG + _MID]
    shapes = [p[k].shape for k in rest]
    packed = [_pack([src[pre_ + k] if pre_ else src[k] for k in rest], 1024)
              for src, pre_ in ((p, ''), (grads, ''), (p, 'm_'), (p, 'v_'))]
    outs = adamw(*packed, "adamw_small")
    for dst, o in zip((delta, new_m, new_v), outs):
        for k, val in zip(rest, _unpack(o.reshape(-1), shapes)):
            dst[k] = val

    done = delta['w_ada']
    halves = []
    for state, tag in ((rs_ffn1, "ffn1"), (rs_lru, "lru"), (rs_ffn0, "ffn0"), (rs_s5, "s5")):
        halves += reduce_scatter_finish(state, place2, done, tag)
    g_gu1, g_down1, g_lru_in, g_lru_out, g_gu0, g_down0, g_s5_in, g_s5_glu = join_halves(halves, "rs_join_halves")
    grads.update({'s5_w_in': g_s5_in[None], 's5_w_glu': g_s5_glu[None], 'lru_w_in': g_lru_in[None],
                  'lru_w_out': g_lru_out[None], 'ffn_w_gu': jnp.stack([g_gu0, g_gu1]),
                  'ffn_w_down': jnp.stack([g_down0, g_down1])})
    for k in _BIG[1:]:
        adamw_2d(k, math.prod(p[k].shape[:-1]))

    return (loss, grad_x[None], *[grads[k] for k in _W_NAMES], *[delta[k] for k in _W_NAMES],
            *[new_m[k] for k in _W_NAMES], *[new_v[k] for k in _W_NAMES])


_IN_NAMES = (['x', 'c'] + _W_NAMES + ['loss_target'] + ['m_' + k for k in _W_NAMES] + ['v_' + k for k in _W_NAMES])


def kernel(x, c, norm_g, w_ada, b_ada, s5_w_in, s5_lam_re, s5_lam_im, s5_log_dt, s5_b_re, s5_b_im, s5_c_re, s5_c_im, s5_d, s5_w_glu, lru_w_in, lru_conv_w, lru_conv_b, lru_w_rg, lru_b_rg, lru_w_ig, lru_b_ig, lru_lam, lru_w_out, ffn_w_gu, ffn_w_down, final_g, loss_target, m_norm_g, m_w_ada, m_b_ada, m_s5_w_in, m_s5_lam_re, m_s5_lam_im, m_s5_log_dt, m_s5_b_re, m_s5_b_im, m_s5_c_re, m_s5_c_im, m_s5_d, m_s5_w_glu, m_lru_w_in, m_lru_conv_w, m_lru_conv_b, m_lru_w_rg, m_lru_b_rg, m_lru_w_ig, m_lru_b_ig, m_lru_lam, m_lru_w_out, m_ffn_w_gu, m_ffn_w_down, m_final_g, v_norm_g, v_w_ada, v_b_ada, v_s5_w_in, v_s5_lam_re, v_s5_lam_im, v_s5_log_dt, v_s5_b_re, v_s5_b_im, v_s5_c_re, v_s5_c_im, v_s5_d, v_s5_w_glu, v_lru_w_in, v_lru_conv_w, v_lru_conv_b, v_lru_w_rg, v_lru_b_rg, v_lru_w_ig, v_lru_b_ig, v_lru_lam, v_lru_w_out, v_ffn_w_gu, v_ffn_w_down, v_final_g):
    args = (x, c, norm_g, w_ada, b_ada, s5_w_in, s5_lam_re, s5_lam_im, s5_log_dt, s5_b_re, s5_b_im, s5_c_re, s5_c_im, s5_d, s5_w_glu, lru_w_in, lru_conv_w, lru_conv_b, lru_w_rg, lru_b_rg, lru_w_ig, lru_b_ig, lru_lam, lru_w_out, ffn_w_gu, ffn_w_down, final_g, loss_target, m_norm_g, m_w_ada, m_b_ada, m_s5_w_in, m_s5_lam_re, m_s5_lam_im, m_s5_log_dt, m_s5_b_re, m_s5_b_im, m_s5_c_re, m_s5_c_im, m_s5_d, m_s5_w_glu, m_lru_w_in, m_lru_conv_w, m_lru_conv_b, m_lru_w_rg, m_lru_b_rg, m_lru_w_ig, m_lru_b_ig, m_lru_lam, m_lru_w_out, m_ffn_w_gu, m_ffn_w_down, m_final_g, v_norm_g, v_w_ada, v_b_ada, v_s5_w_in, v_s5_lam_re, v_s5_lam_im, v_s5_log_dt, v_s5_b_re, v_s5_b_im, v_s5_c_re, v_s5_c_im, v_s5_d, v_s5_w_glu, v_lru_w_in, v_lru_conv_w, v_lru_conv_b, v_lru_w_rg, v_lru_b_rg, v_lru_w_ig, v_lru_b_ig, v_lru_lam, v_lru_w_out, v_ffn_w_gu, v_ffn_w_down, v_final_g)
    return _step(dict(zip(_IN_NAMES, args)))
```

```python
import functools
import math

import jax
import jax.numpy as jnp
from jax import lax
from jax.experimental import pallas as pl
from jax.experimental.pallas import tpu as pltpu

F32 = jnp.float32
BF16 = jnp.bfloat16
MESH = pl.DeviceIdType.MESH

EPS = 1e-6
LRU_C = 8.0
N_MOD = 6
ADAM_LR = 0.001
ADAM_B1 = 0.9
ADAM_B2 = 0.999
ADAM_EPS = 1e-08
ADAM_WD = 0.01
ADAM_STEP = 10

N_CHIPS = 4
N_DEV = 8
SUBLANES = 8
LANES = 128
S5_SB_GROUPS = 8
V7X_VMEM_LIMIT = 48 * 1024 * 1024
ROW_TILE_ELEMS = 512 * 1024
ROW_CALL_ELEMS = 4 * 1024 * 1024
SCAN_UNROLL = 4
S5_CHUNKS = (1024, 512, 256, 128, 64, 32, 16, 8)
LRU_TILE = 256

_TM = (1024, 1408, 512, 256, 128, 64, 32, 16, 8)
_TN = (1024, 1408, 512, 384, 256, 128)
_TK = (1024, 1408, 512, 256, 128)
_TR = (256, 128, 64, 32, 16, 8)

_GELU_K0 = math.sqrt(2.0 / math.pi)
_GELU_K1 = 0.044715


def _tile(n, cands):
    for t in cands:
        if n % t == 0:
            return t
    return n


def _cp(sem=None):
    return pltpu.CompilerParams(dimension_semantics=sem, vmem_limit_bytes=V7X_VMEM_LIMIT)


def _sds(shape, dtype):
    return jax.ShapeDtypeStruct(shape, dtype)


def _sig(x):
    return 1.0 / (1.0 + jnp.exp(-x))


def _gelu(x):
    t = jnp.tanh(_GELU_K0 * (x + _GELU_K1 * x * x * x))
    return 0.5 * x * (1.0 + t)


def _gelu_grad(x):
    x2 = x * x
    t = jnp.tanh(_GELU_K0 * (x + _GELU_K1 * x * x2))
    return 0.5 * (1.0 + t) + 0.5 * x * (1.0 - t * t) * _GELU_K0 * (1.0 + 3.0 * _GELU_K1 * x2)


def _softplus(z):
    return jnp.maximum(z, 0.0) + jnp.log(1.0 + jnp.exp(-jnp.abs(z)))


def _neg_expm1(x):
    series = -x * (1.0 + x * (0.5 + x * (1.0 / 6.0 + x * (1.0 / 24.0))))
    return jnp.where(x > -0.05, series, 1.0 - jnp.exp(x))


def _row(x, r):
    return x[r:r + 1, :]


def _colsum(x):
    return jnp.sum(x, axis=0, keepdims=True)


_NN = (((1,), (0,)), ((), ()))
_NT = (((1,), (1,)), ((), ()))
_TN_DIMS = (((0,), (0,)), ((), ()))


def _mm_call(name, a, b, a_spec, b_spec, o_spec, grid, out_shape, acc_shape, dims, after=None):
    nk = grid[-1]
    kaxis = len(grid) - 1
    extra = [] if after is None else [after]

    def body(a_ref, b_ref, *rest):
        o_ref, acc_ref = rest[-2], rest[-1]
        k = pl.program_id(kaxis)

        def prod():
            return lax.dot_general(a_ref[...].astype(BF16), b_ref[...].astype(BF16), dims,
                                   preferred_element_type=F32)

        if nk == 1:
            o_ref[...] = prod().astype(o_ref.dtype)
            return

        @pl.when(k == 0)
        def _():
            acc_ref[...] = prod()

        if nk > 2:
            @pl.when(jnp.logical_and(k > 0, k < nk - 1))
            def _():
                acc_ref[...] += prod()

        @pl.when(k == nk - 1)
        def _():
            o_ref[...] = (acc_ref[...] + prod()).astype(o_ref.dtype)

    return pl.pallas_call(
        body, name=name, grid=grid, in_specs=[a_spec, b_spec] + [pl.BlockSpec(memory_space=pl.ANY)] * len(extra),
        out_specs=o_spec, out_shape=out_shape, scratch_shapes=[pltpu.VMEM(acc_shape, F32)],
        compiler_params=_cp(("parallel", "parallel", "parallel", "arbitrary")),
    )(a, b, *extra)


def mm_nn(a, b, *, name, out_dtype=F32, bmode="plain"):
    M = a.shape[0]
    if bmode == "plain":
        G, S = 1, 1
        K, Nc = b.shape
    elif bmode == "cols":
        G = 1
        S, K, Nc = b.shape
    else:
        S = 1
        G, K, Nc = b.shape
    tm, tn, tk = _tile(M, _TM), _tile(Nc, _TN), _tile(K, _TK)
    nkb, nnb = K // tk, Nc // tn
    ncol = S * nnb
    grid = (G, M // tm, ncol, nkb)
    a_spec = pl.BlockSpec((tm, tk), lambda g, i, j, k: (i, g * nkb + k))
    if bmode == "plain":
        b_spec = pl.BlockSpec((tk, tn), lambda g, i, j, k: (k, j))
    elif bmode == "cols":
        b_spec = pl.BlockSpec((None, tk, tn), lambda g, i, j, k: (j // nnb, k, j % nnb))
    else:
        b_spec = pl.BlockSpec((None, tk, tn), lambda g, i, j, k: (g, k, j))
    o_spec = pl.BlockSpec((tm, tn), lambda g, i, j, k: (i, g * ncol + j))
    return _mm_call(name, a, b, a_spec, b_spec, o_spec, grid, _sds((M, G * S * Nc), out_dtype), (tm, tn), _NN)


def mm_nt(a, b, *, name, out_dtype=F32, bmode="plain", after=None):
    M = a.shape[0]
    if bmode == "plain":
        G, S = 1, 1
        Ko, Nc = b.shape
    elif bmode == "cols":
        G = 1
        S, Ko, Nc = b.shape
    else:
        S = 1
        G, Ko, Nc = b.shape
    tm, to, tc = _tile(M, _TM), _tile(Ko, _TN), _tile(Nc, _TK)
    npc = Nc // tc
    nc = S * npc
    nob = Ko // to
    grid = (G, M // tm, nob, nc)
    a_spec = pl.BlockSpec((tm, tc), lambda g, i, j, n: (i, g * nc + n))
    if bmode == "plain":
        b_spec = pl.BlockSpec((to, tc), lambda g, i, j, n: (j, n))
    elif bmode == "cols":
        b_spec = pl.BlockSpec((None, to, tc), lambda g, i, j, n: (n // npc, j, n % npc))
    else:
        b_spec = pl.BlockSpec((None, to, tc), lambda g, i, j, n: (g, j, n))
    o_spec = pl.BlockSpec((tm, to), lambda g, i, j, n: (i, g * nob + j))
    return _mm_call(name, a, b, a_spec, b_spec, o_spec, grid, _sds((M, G * Ko), out_dtype), (tm, to), _NT,
                    after=after)


def mm_tn(a, b, *, name, out_dtype=F32, omode="plain", groups=1):
    L = a.shape[0]
    G = groups if omode == "batch" else 1
    S = groups if omode == "cols" else 1
    Mo, N = a.shape[1] // G, b.shape[1] // G
    Nc = N // S
    tm, tn, tl = _tile(Mo, _TM), _tile(Nc, _TN), _tile(L, _TK)
    nmb, nnb = Mo // tm, N // tn
    npj = Nc // tn
    grid = (G, nmb, nnb, L // tl)
    a_spec = pl.BlockSpec((tl, tm), lambda g, i, j, l: (l, g * nmb + i))
    b_spec = pl.BlockSpec((tl, tn), lambda g, i, j, l: (l, g * nnb + j))
    if omode == "plain":
        o_spec = pl.BlockSpec((tm, tn), lambda g, i, j, l: (i, j))
        oshape = (Mo, N)
    elif omode == "cols":
        o_spec = pl.BlockSpec((None, tm, tn), lambda g, i, j, l: (j // npj, i, j % npj))
        oshape = (S, Mo, Nc)
    else:
        o_spec = pl.BlockSpec((None, tm, tn), lambda g, i, j, l: (g, i, j))
        oshape = (G, Mo, N)
    return _mm_call(name, a, b, a_spec, b_spec, o_spec, grid, _sds(oshape, out_dtype), (tm, tn), _TN_DIMS)


def _row_call(name, body, row_ins, vec_ins, row_outs, acc_outs=(), after=None):
    if after is not None:
        n_in = len(row_ins) + len(vec_ins)
        inner = body

        def body(*refs):
            inner(*refs[:n_in], *refs[n_in + 1:])

        return _row_call_impl(name, body, row_ins, vec_ins, row_outs, acc_outs, [after])
    return _row_call_impl(name, body, row_ins, vec_ins, row_outs, acc_outs, [])


def _row_call_impl(name, body, row_ins, vec_ins, row_outs, acc_outs, extra):
    L = row_ins[0].shape[0]
    wmax = max([a.shape[1] for a in row_ins] + [w for w, _ in row_outs])
    narr = len(row_ins) + len(row_outs)
    tr = _tile(L, tuple(t for t in _TR if t * wmax * narr <= ROW_CALL_ELEMS) or (SUBLANES,))
    in_specs = [pl.BlockSpec((tr, a.shape[1]), lambda i: (i, 0)) for a in row_ins]
    in_specs += [pl.BlockSpec(v.shape, lambda i, nd=v.ndim: (0,) * nd) for v in vec_ins]
    in_specs += [pl.BlockSpec(memory_space=pl.ANY) for _ in extra]
    out_shape = [_sds((L, w), dt) for w, dt in row_outs] + [_sds(s, dt) for s, dt in acc_outs]
    out_specs = [pl.BlockSpec((tr, w), lambda i: (i, 0)) for w, _ in row_outs]
    out_specs += [pl.BlockSpec(s, lambda i, nd=len(s): (0,) * nd) for s, _ in acc_outs]
    sem = ("arbitrary",) if acc_outs else ("parallel",)
    return pl.pallas_call(body, name=name, grid=(L // tr,), in_specs=in_specs, out_specs=out_specs,
                          out_shape=out_shape, compiler_params=_cp(sem))(*row_ins, *vec_ins, *extra)


def silu_rows(x, name, after=None):
    def body(x_ref, o_ref):
        v = x_ref[...]
        o_ref[...] = (v * _sig(v)).astype(o_ref.dtype)
    return _row_call(name, body, [x], [], [(x.shape[1], BF16)], after=after)[0]


def norm_mod_fwd(x, gain, sc, sh, name, after=None):
    def body(x_ref, g_ref, sc_ref, sh_ref, h_ref):
        v = x_ref[...]
        r = lax.rsqrt(jnp.mean(v * v, axis=-1, keepdims=True) + EPS)
        h_ref[...] = (v * r * g_ref[...] * (1.0 + sc_ref[...]) + sh_ref[...]).astype(BF16)
    return _row_call(name, body, [x], [gain, sc, sh], [(x.shape[1], BF16)], after=after)[0]


def norm_mod_bwd(dh, x, gain, sc, dres, name, after=None):
    D = x.shape[1]

    def body(dh_ref, x_ref, dres_ref, g_ref, sc_ref, dx_ref, dg_ref, dsc_ref, dsh_ref):
        @pl.when(pl.program_id(0) == 0)
        def _():
            dg_ref[...] = jnp.zeros_like(dg_ref)
            dsc_ref[...] = jnp.zeros_like(dsc_ref)
            dsh_ref[...] = jnp.zeros_like(dsh_ref)

        v = x_ref[...]
        dh_v = dh_ref[...]
        g = g_ref[...]
        r = lax.rsqrt(jnp.mean(v * v, axis=-1, keepdims=True) + EPS)
        xhat = v * r
        dn = dh_v * (1.0 + sc_ref[...])
        dsc_ref[...] += _colsum(dh_v * xhat * g)
        dsh_ref[...] += _colsum(dh_v)
        dg_ref[...] += _colsum(dn * xhat)
        t = dn * g
        dx_ref[...] = dres_ref[...] + r * (t - xhat * jnp.mean(t * xhat, axis=-1, keepdims=True))

    acc = [((1, D), F32)] * 3
    return _row_call(name, body, [dh, x, dres], [gain, sc], [(D, F32)], acc, after=after)


def final_loss(x, gain, tgt, name):
    D = x.shape[1]

    def body(x_ref, t_ref, g_ref, dx_ref, loss_ref, dg_ref, acc_ref):
        i = pl.program_id(0)

        @pl.when(i == 0)
        def _():
            dg_ref[...] = jnp.zeros_like(dg_ref)
            acc_ref[...] = jnp.zeros_like(acc_ref)

        v = x_ref[...]
        g = g_ref[...]
        r = lax.rsqrt(jnp.mean(v * v, axis=-1, keepdims=True) + EPS)
        xhat = v * r
        err = xhat * g - t_ref[...]
        acc_ref[...] += _colsum(err * err)
        dout = err * (1.0 / D)
        dg_ref[...] += _colsum(dout * xhat)
        t = dout * g
        dx_ref[...] = r * (t - xhat * jnp.mean(t * xhat, axis=-1, keepdims=True))

        @pl.when(i == pl.num_programs(0) - 1)
        def _():
            loss_ref[...] = jnp.zeros_like(loss_ref) + jnp.sum(acc_ref[...]) * (0.5 / D)

    return _row_call(name, body, [x, tgt], [gain], [(D, F32)],
                     [((SUBLANES, LANES), F32), ((1, D), F32), ((1, D), F32)])[:3]


def res_gate_fwd(x, z, g, name):
    def body(x_ref, z_ref, g_ref, o_ref):
        o_ref[...] = x_ref[...] + g_ref[...] * z_ref[...]
    return _row_call(name, body, [x, z], [g], [(x.shape[1], F32)])[0]


def res_gate_bwd(dx, z, g, name):
    D = dx.shape[1]

    def body(dx_ref, z_ref, g_ref, dz_ref, dg_ref):
        @pl.when(pl.program_id(0) == 0)
        def _():
            dg_ref[...] = jnp.zeros_like(dg_ref)
        d = dx_ref[...]
        dz_ref[...] = (g_ref[...] * d).astype(BF16)
        dg_ref[...] += _colsum(d * z_ref[...])
    return _row_call(name, body, [dx, z], [g], [(D, BF16)], [((1, D), F32)])


def glu_res_fwd(x, v, g, name):
    D = x.shape[1]

    def body(x_ref, v_ref, g_ref, o_ref):
        vv = v_ref[...]
        o_ref[...] = x_ref[...] + g_ref[...] * (vv[:, :D] * _sig(vv[:, D:]))
    return _row_call(name, body, [x, v], [g], [(D, F32)])[0]


def glu_res_bwd(dx, v, g, name, after=None):
    D = dx.shape[1]

    def body(dx_ref, v_ref, g_ref, dv_ref, dg_ref):
        @pl.when(pl.program_id(0) == 0)
        def _():
            dg_ref[...] = jnp.zeros_like(dg_ref)
        d = dx_ref[...]
        vv = v_ref[...]
        val = vv[:, :D]
        s = _sig(vv[:, D:])
        dg_ref[...] += _colsum(d * val * s)
        dm = g_ref[...] * d
        dv_ref[:, :D] = (dm * s).astype(BF16)
        dv_ref[:, D:] = (dm * val * s * (1.0 - s)).astype(BF16)
    return _row_call(name, body, [dx, v], [g], [(2 * D, BF16)], [((1, D), F32)], after=after)


def swiglu_fwd(gu, name):
    F = gu.shape[1] // 2

    def body(gu_ref, o_ref):
        v = gu_ref[...].astype(F32)
        g = v[:, :F]
        o_ref[...] = (g * _sig(g) * v[:, F:]).astype(BF16)
    return _row_call(name, body, [gu], [], [(F, BF16)])[0]


def swiglu_bwd(dact, gu, name, after=None):
    F = gu.shape[1] // 2

    def body(da_ref, gu_ref, o_ref):
        v = gu_ref[...].astype(F32)
        g, u = v[:, :F], v[:, F:]
        da = da_ref[...]
        s = _sig(g)
        o_ref[:, :F] = (da * u * s * (1.0 + g * (1.0 - s))).astype(BF16)
        o_ref[:, F:] = (da * g * s).astype(BF16)
    return _row_call(name, body, [dact, gu], [], [(2 * F, BF16)], after=after)[0]


def adamw(w, g, m, v, name, after=None):
    C = w.shape[1]
    c1 = 1.0 - ADAM_B1 ** ADAM_STEP
    c2 = 1.0 - ADAM_B2 ** ADAM_STEP

    def body(w_ref, g_ref, m_ref, v_ref, d_ref, m2_ref, v2_ref):
        gv = g_ref[...]
        m2 = ADAM_B1 * m_ref[...] + (1.0 - ADAM_B1) * gv
        v2 = ADAM_B2 * v_ref[...] + (1.0 - ADAM_B2) * (gv * gv)
        m2_ref[...] = m2
        v2_ref[...] = v2
        d_ref[...] = -ADAM_LR * ((m2 / c1) / (jnp.sqrt(v2 / c2) + ADAM_EPS) + ADAM_WD * w_ref[...])
    return _row_call(name, body, [w, g, m, v], [], [(C, F32)] * 3, after=after)


def sum_devices(parts, name):
    n, R, C = parts.shape
    min_rows = 16 if parts.dtype == BF16 else SUBLANES
    tr = _tile(R, tuple(t for t in _TR if t * C * n <= 4 * ROW_TILE_ELEMS and t >= min_rows) or (min_rows,))

    def body(p_ref, o_ref):
        acc = p_ref[0].astype(F32)
        for d in range(1, n):
            acc = acc + p_ref[d].astype(F32)
        o_ref[...] = acc
    return pl.pallas_call(body, name=name, grid=(R // tr,),
                          in_specs=[pl.BlockSpec((n, tr, C), lambda i: (0, i, 0))],
                          out_specs=pl.BlockSpec((tr, C), lambda i: (i, 0)), out_shape=_sds((R, C), F32),
                          compiler_params=_cp(("parallel",)))(parts)


def _s5_discretize(lam_re, lam_im, log_dt, b_re, b_im):
    dt = jnp.exp(log_dt)[:, None]
    mag = jnp.exp(lam_re * dt)
    ab_re = mag * jnp.cos(lam_im * dt)
    ab_im = mag * jnp.sin(lam_im * dt)
    nr, ni = ab_re - 1.0, ab_im
    den = lam_re * lam_re + lam_im * lam_im
    f_re = (nr * lam_re + ni * lam_im) / den
    f_im = (ni * lam_re - nr * lam_im) / den
    bb_re = f_re[..., None] * b_re - f_im[..., None] * b_im
    bb_im = f_re[..., None] * b_im + f_im[..., None] * b_re
    return ab_re, ab_im, bb_re, bb_im


def _s5_blockdiag(bb_re, bb_im, c_re, c_im):
    G, P, Cg = bb_re.shape
    nsb = G // S5_SB_GROUPS

    def bmat(bb):
        return _block_diag(jnp.swapaxes(bb.reshape(nsb, S5_SB_GROUPS, P, Cg), 2, 3))

    def cmat(cc):
        return _block_diag(jnp.swapaxes(cc.reshape(nsb, S5_SB_GROUPS, Cg, P), 2, 3))

    bsb = jnp.concatenate([bmat(bb_re), bmat(bb_im)], axis=-1)
    csb = jnp.concatenate([cmat(c_re), -cmat(c_im)], axis=1)
    return bsb, csb


def _block_diag(t):
    ng, b = t.shape[1], t.shape[3]
    rows = [jnp.pad(t[:, g], ((0, 0), (0, 0), (g * b, (ng - 1 - g) * b))) for g in range(ng)]
    return jnp.concatenate(rows, axis=1)


def _diag_blocks(m, ng):
    a, b = m.shape[1] // ng, m.shape[2] // ng
    return jnp.stack([m[:, g * a:(g + 1) * a, g * b:(g + 1) * b] for g in range(ng)], axis=1)


def _s5_blockdiag_grads(dbsb, dcsb, P, Cg):
    H = S5_SB_GROUPS * P

    def blocks(m):
        return jnp.swapaxes(_diag_blocks(m, S5_SB_GROUPS), 2, 3)

    dbb_re = blocks(dbsb[:, :, :H]).reshape(-1, P, Cg)
    dbb_im = blocks(dbsb[:, :, H:]).reshape(-1, P, Cg)
    dc_re = blocks(dcsb[:, :H, :]).reshape(-1, Cg, P)
    dc_im = -blocks(dcsb[:, H:, :]).reshape(-1, Cg, P)
    return dbb_re, dbb_im, dc_re, dc_im


def _s5_scan_consts(ab_re, ab_im):
    G, P = ab_re.shape
    nsb = G // S5_SB_GROUPS
    H = S5_SB_GROUPS * P
    ar, ai = ab_re.reshape(nsb, 1, H), ab_im.reshape(nsb, 1, H)
    pows = [(ar, ai)]
    for _ in range(SUBLANES - 1):
        pr, pi_ = pows[-1]
        pows.append((pr * ar - pi_ * ai, pr * ai + pi_ * ar))
    rows = jnp.arange(SUBLANES).reshape(1, SUBLANES, 1)

    def masked(k, keep):
        pr, pi_ = pows[k - 1]
        return jnp.where(keep, pr, 0.0), jnp.where(keep, pi_, 0.0)

    def per_row(sel):
        pr = jnp.concatenate([pows[sel(r) - 1][0] for r in range(SUBLANES)], axis=1)
        pi_ = jnp.concatenate([pows[sel(r) - 1][1] for r in range(SUBLANES)], axis=1)
        return pr, pi_

    fwd = [masked(1, rows >= 1), masked(2, rows >= 2), masked(4, rows >= 4), per_row(lambda r: r + 1)]
    rev = [masked(1, rows < 7), masked(2, rows < 6), masked(4, rows < 4), per_row(lambda r: SUBLANES - r)]

    def pack(lst, conj):
        sgn = -1.0 if conj else 1.0
        return jnp.stack([jnp.concatenate([jnp.broadcast_to(pr, (nsb, SUBLANES, H)),
                                           sgn * jnp.broadcast_to(pi_, (nsb, SUBLANES, H))], axis=-1)
                          for pr, pi_ in lst], axis=1)

    return pack(fwd, False), pack(rev, True)


def _cmadd(xr, xi, ar, ai, yr, yi):
    return xr + ar * yr - ai * yi, xi + ar * yi + ai * yr


def _s5_scan_fwd_loop(src_ref, dst_ref, cf_ref, cr, ci, nblk, H):
    def body(k, carry):
        cr, ci = carry
        r0 = pl.multiple_of(k * SUBLANES, SUBLANES)
        xr = src_ref[pl.ds(r0, SUBLANES), pl.ds(0, H)]
        xi = src_ref[pl.ds(r0, SUBLANES), pl.ds(H, H)]
        for idx, d in enumerate((1, 2, 4)):
            xr, xi = _cmadd(xr, xi, cf_ref[idx, :, pl.ds(0, H)], cf_ref[idx, :, pl.ds(H, H)],
                            pltpu.roll(xr, d, 0), pltpu.roll(xi, d, 0))
        xr, xi = _cmadd(xr, xi, cf_ref[3, :, pl.ds(0, H)], cf_ref[3, :, pl.ds(H, H)], cr, ci)
        dst_ref[pl.ds(r0, SUBLANES), pl.ds(0, H)] = xr
        dst_ref[pl.ds(r0, SUBLANES), pl.ds(H, H)] = xi
        return _row(xr, SUBLANES - 1), _row(xi, SUBLANES - 1)

    return lax.fori_loop(0, nblk, body, (cr, ci))


def s5_scan_fwd(u, d_skip, bsb, csb, cf, name):
    L, W = u.shape
    nsb, GW, H2 = bsb.shape
    H = H2 // 2
    Tc = _tile(L, S5_CHUNKS)
    nch = L // Tc

    def body(u_ref, d_ref, b_ref, c_ref, cf_ref, ypre_ref, yg_ref, s_ref, ss_ref, bu_scr, car_scr):
        @pl.when(pl.program_id(1) == 0)
        def _():
            car_scr[...] = jnp.zeros_like(car_scr)

        ss_ref[...] = car_scr[...]
        ub = u_ref[...]
        bu_scr[...] = jnp.dot(ub.astype(BF16), b_ref[...], preferred_element_type=F32)
        cr, ci = _s5_scan_fwd_loop(bu_scr, s_ref, cf_ref, car_scr[:, pl.ds(0, H)], car_scr[:, pl.ds(H, H)],
                                   Tc // SUBLANES, H)
        car_scr[:, pl.ds(0, H)] = cr
        car_scr[:, pl.ds(H, H)] = ci
        ypre = jnp.dot(s_ref[...].astype(BF16), c_ref[...], preferred_element_type=F32) + d_ref[...] * ub
        ypre_ref[...] = ypre
        yg_ref[...] = _gelu(ypre).astype(BF16)

    return pl.pallas_call(
        body, name=name, grid=(nsb, nch),
        in_specs=[pl.BlockSpec((Tc, GW), lambda j, i: (i, j)),
                  pl.BlockSpec((1, GW), lambda j, i: (0, j)),
                  pl.BlockSpec((None, GW, H2), lambda j, i: (j, 0, 0)),
                  pl.BlockSpec((None, H2, GW), lambda j, i: (j, 0, 0)),
                  pl.BlockSpec((None, 4, SUBLANES, H2), lambda j, i: (j, 0, 0, 0))],
        out_specs=[pl.BlockSpec((Tc, GW), lambda j, i: (i, j)),
                   pl.BlockSpec((Tc, GW), lambda j, i: (i, j)),
                   pl.BlockSpec((Tc, H2), lambda j, i: (i, j)),
                   pl.BlockSpec((None, None, 1, H2), lambda j, i: (i, j, 0, 0))],
        out_shape=[_sds((L, W), F32), _sds((L, W), BF16), _sds((L, nsb * H2), F32),
                   _sds((nch, nsb, 1, H2), F32)],
        scratch_shapes=[pltpu.VMEM((Tc, H2), F32), pltpu.VMEM((1, H2), F32)],
        compiler_params=_cp(("arbitrary", "arbitrary")),
    )(u, d_skip, bsb.astype(BF16), csb.astype(BF16), cf)


def s5_scan_bwd(u, dyg, ypre, s_all, d_skip, bsb, csb, crv, ss, name):
    L, W = u.shape
    nsb, GW, H2 = bsb.shape
    H = H2 // 2
    Tc = _tile(L, S5_CHUNKS)
    nch = L // Tc
    nblk = Tc // SUBLANES
    bsb_t = jnp.swapaxes(bsb, 1, 2).astype(BF16)
    csb_t = jnp.swapaxes(csb, 1, 2).astype(BF16)

    def body(u_ref, dyg_ref, yp_ref, s_ref, d_ref, bt_ref, ct_ref, crv_ref, ss_ref,
             du_ref, db_ref, dc_ref, da_ref, dd_ref, g_scr, gcar_scr):
        @pl.when(pl.program_id(1) == 0)
        def _():
            gcar_scr[...] = jnp.zeros_like(gcar_scr)
            db_ref[...] = jnp.zeros_like(db_ref)
            dc_ref[...] = jnp.zeros_like(dc_ref)
            da_ref[...] = jnp.zeros_like(da_ref)
            dd_ref[...] = jnp.zeros_like(dd_ref)

        ub = u_ref[...]
        ubf = ub.astype(BF16)
        dyp = dyg_ref[...] * _gelu_grad(yp_ref[...])
        dypb = dyp.astype(BF16)
        dd_ref[...] += _colsum(dyp * ub)
        g_scr[...] = jnp.dot(dypb, ct_ref[...], preferred_element_type=F32)
        rows = lax.broadcasted_iota(jnp.int32, (SUBLANES, H), 0)

        def rev(kk, carry):
            gr, gi, acc_r, acc_i = carry
            k = nblk - 1 - kk
            r0 = pl.multiple_of(k * SUBLANES, SUBLANES)
            xr = g_scr[pl.ds(r0, SUBLANES), pl.ds(0, H)]
            xi = g_scr[pl.ds(r0, SUBLANES), pl.ds(H, H)]
            for idx, d in enumerate((1, 2, 4)):
                xr, xi = _cmadd(xr, xi, crv_ref[idx, :, pl.ds(0, H)], crv_ref[idx, :, pl.ds(H, H)],
                                pltpu.roll(xr, SUBLANES - d, 0), pltpu.roll(xi, SUBLANES - d, 0))
            xr, xi = _cmadd(xr, xi, crv_ref[3, :, pl.ds(0, H)], crv_ref[3, :, pl.ds(H, H)], gr, gi)
            g_scr[pl.ds(r0, SUBLANES), pl.ds(0, H)] = xr
            g_scr[pl.ds(r0, SUBLANES), pl.ds(H, H)] = xi
            rp = pl.multiple_of(jnp.maximum(k - 1, 0) * SUBLANES, SUBLANES)
            first = k == 0
            pr = jnp.where(first, ss_ref[:, pl.ds(0, H)], _row(s_ref[pl.ds(rp, SUBLANES), pl.ds(0, H)], SUBLANES - 1))
            pi_ = jnp.where(first, ss_ref[:, pl.ds(H, H)], _row(s_ref[pl.ds(rp, SUBLANES), pl.ds(H, H)], SUBLANES - 1))
            spr = jnp.where(rows == 0, pr, pltpu.roll(s_ref[pl.ds(r0, SUBLANES), pl.ds(0, H)], 1, 0))
            spi = jnp.where(rows == 0, pi_, pltpu.roll(s_ref[pl.ds(r0, SUBLANES), pl.ds(H, H)], 1, 0))
            return (_row(xr, 0), _row(xi, 0), acc_r + xr * spr + xi * spi, acc_i + xi * spr - xr * spi)

        zero = jnp.zeros((SUBLANES, H), F32)
        gr, gi, acc_r, acc_i = lax.fori_loop(
            0, nblk, rev, (gcar_scr[:, pl.ds(0, H)], gcar_scr[:, pl.ds(H, H)], zero, zero))
        gcar_scr[:, pl.ds(0, H)] = gr
        gcar_scr[:, pl.ds(H, H)] = gi
        da_ref[:, pl.ds(0, H)] += _colsum(acc_r)
        da_ref[:, pl.ds(H, H)] += _colsum(acc_i)
        gb = g_scr[...].astype(BF16)
        db_ref[...] += lax.dot_general(ubf, gb, _TN_DIMS, preferred_element_type=F32)
        dc_ref[...] += lax.dot_general(s_ref[...].astype(BF16), dypb, _TN_DIMS, preferred_element_type=F32)
        du_ref[...] = (jnp.dot(gb, bt_ref[...], preferred_element_type=F32) + d_ref[...] * dyp).astype(BF16)

    rmap = lambda j, i: (nch - 1 - i, j)
    return pl.pallas_call(
        body, name=name, grid=(nsb, nch),
        in_specs=[pl.BlockSpec((Tc, GW), rmap), pl.BlockSpec((Tc, GW), rmap), pl.BlockSpec((Tc, GW), rmap),
                  pl.BlockSpec((Tc, H2), rmap),
                  pl.BlockSpec((1, GW), lambda j, i: (0, j)),
                  pl.BlockSpec((None, H2, GW), lambda j, i: (j, 0, 0)),
                  pl.BlockSpec((None, GW, H2), lambda j, i: (j, 0, 0)),
                  pl.BlockSpec((None, 4, SUBLANES, H2), lambda j, i: (j, 0, 0, 0)),
                  pl.BlockSpec((None, None, 1, H2), lambda j, i: (nch - 1 - i, j, 0, 0))],
        out_specs=[pl.BlockSpec((Tc, GW), rmap),
                   pl.BlockSpec((None, GW, H2), lambda j, i: (j, 0, 0)),
                   pl.BlockSpec((None, H2, GW), lambda j, i: (j, 0, 0)),
                   pl.BlockSpec((None, 1, H2), lambda j, i: (j, 0, 0)),
                   pl.BlockSpec((1, GW), lambda j, i: (0, j))],
        out_shape=[_sds((L, W), BF16), _sds((nsb, GW, H2), F32), _sds((nsb, H2, GW), F32),
                   _sds((nsb, 1, H2), F32), _sds((1, W), F32)],
        scratch_shapes=[pltpu.VMEM((Tc, H2), F32), pltpu.VMEM((1, H2), F32)],
        compiler_params=_cp(("arbitrary", "arbitrary")),
    )(u, dyg, ypre, s_all, d_skip, bsb_t, csb_t, crv, ss)


def _lru_blockdiag(w_rg, w_ig):
    nb, bs, _ = w_rg.shape
    sbw = bs * LANES // math.gcd(bs, LANES)
    bps = sbw // bs
    nsb = nb // bps

    def bd(w):
        return _block_diag(w.astype(BF16).reshape(nsb, bps, bs, bs))

    return bd(w_rg), bd(w_ig)


def _lru_blockdiag_grad(dwsb, nb, bs):
    return _diag_blocks(dwsb, dwsb.shape[1] // bs).reshape(nb, bs, bs)


def lru_conv_fwd(p, conv_w, conv_b, name):
    L = p.shape[0]
    E = conv_w.shape[1]
    tc = _tile(E, (256, 128))
    noff = E // tc
    kw = conv_w.shape[0]

    def body(xb_ref, w_ref, b_ref, xc_ref, xcb_ref):
        xb = xb_ref[...]
        rows = lax.broadcasted_iota(jnp.int32, xb.shape, 0)
        acc = w_ref[pl.ds(kw - 1, 1), :] * xb + b_ref[...]
        for k in range(kw - 1):
            sh = kw - 1 - k
            acc = acc + w_ref[pl.ds(k, 1), :] * jnp.where(rows >= sh, pltpu.roll(xb, sh, 0), 0.0)
        xc_ref[...] = acc
        xcb_ref[...] = acc.astype(BF16)

    return pl.pallas_call(
        body, name=name, grid=(noff,),
        in_specs=[pl.BlockSpec((L, tc), lambda t: (0, noff + t)),
                  pl.BlockSpec((kw, tc), lambda t: (0, t)), pl.BlockSpec((1, tc), lambda t: (0, t))],
        out_specs=[pl.BlockSpec((L, tc), lambda t: (0, t))] * 2,
        out_shape=[_sds((L, E), F32), _sds((L, E), BF16)],
        compiler_params=_cp(("parallel",)),
    )(p, conv_w, conv_b)


def lru_conv_bwd(d1, d2, d3, p, conv_w, name):
    L = p.shape[0]
    E = conv_w.shape[1]
    tc = _tile(E, (256, 128))
    noff = E // tc
    kw = conv_w.shape[0]

    def body(d1_ref, d2_ref, d3_ref, xb_ref, w_ref, dxb_ref, dw_ref, db_ref):
        dxc = d1_ref[...] + d2_ref[...] + d3_ref[...]
        xb = xb_ref[...]
        rows = lax.broadcasted_iota(jnp.int32, xb.shape, 0)
        db_ref[...] = _colsum(dxc)
        acc = w_ref[pl.ds(kw - 1, 1), :] * dxc
        dw_ref[pl.ds(kw - 1, 1), :] = _colsum(dxc * xb)
        for k in range(kw - 1):
            sh = kw - 1 - k
            dw_ref[pl.ds(k, 1), :] = _colsum(dxc * jnp.where(rows >= sh, pltpu.roll(xb, sh, 0), 0.0))
            acc = acc + w_ref[pl.ds(k, 1), :] * jnp.where(rows < L - sh, pltpu.roll(dxc, L - sh, 0), 0.0)
        dxb_ref[...] = acc.astype(BF16)

    return pl.pallas_call(
        body, name=name, grid=(noff,),
        in_specs=[pl.BlockSpec((L, tc), lambda t: (0, t))] * 3 +
                 [pl.BlockSpec((L, tc), lambda t: (0, noff + t)), pl.BlockSpec((kw, tc), lambda t: (0, t))],
        out_specs=[pl.BlockSpec((L, tc), lambda t: (0, t)), pl.BlockSpec((kw, tc), lambda t: (0, t)),
                   pl.BlockSpec((1, tc), lambda t: (0, t))],
        out_shape=[_sds((L, E), BF16), _sds((kw, E), F32), _sds((1, E), F32)],
        compiler_params=_cp(("parallel",)),
    )(d1, d2, d3, p, conv_w)


def _lru_gates(pr, pi_, brg, big, sp):
    r = _sig(pr + brg)
    ig = _sig(pi_ + big)
    la = -LRU_C * r * sp
    a = jnp.exp(la)
    mult = jnp.sqrt(_neg_expm1(2.0 * la))
    return r, ig, a, mult


def _lru_specs(L, E):
    tc = _tile(E, (LRU_TILE, LANES))
    col = pl.BlockSpec((L, tc), lambda t: (0, t))
    vec = pl.BlockSpec((1, tc), lambda t: (0, t))
    return tc, col, vec


def lru_scan_fwd(pre_r, pre_i, xc, p, b_rg, b_ig, lam, name):
    L, E = xc.shape
    tc, col, vec = _lru_specs(L, E)
    nblk = L // SUBLANES

    def body(pr_ref, pi_ref, xc_ref, gb_ref, brg_ref, big_ref, lam_ref, hs_ref, yv_ref):
        sp = _softplus(-lam_ref[...])
        brg, big = brg_ref[...], big_ref[...]
        rows = lax.broadcasted_iota(jnp.int32, (SUBLANES, tc), 0)

        def blk(k, carry):
            r0 = pl.multiple_of(k * SUBLANES, SUBLANES)
            sl = pl.ds(r0, SUBLANES)
            _, ig, a, mult = _lru_gates(pr_ref[sl, :], pi_ref[sl, :], brg, big, sp)
            b = mult * ig * xc_ref[sl, :]
            for d in (1, 2, 4):
                keep = rows >= d
                b = b + a * jnp.where(keep, pltpu.roll(b, d, 0), 0.0)
                a = a * jnp.where(keep, pltpu.roll(a, d, 0), 1.0)
            h = b + a * carry
            hs_ref[sl, :] = h
            return _row(h, SUBLANES - 1)

        def trip(kt, carry):
            for q in range(SCAN_UNROLL):
                carry = blk(kt * SCAN_UNROLL + q, carry)
            return carry

        lax.fori_loop(0, nblk // SCAN_UNROLL, trip, jnp.zeros((1, tc), F32))
        yv_ref[...] = (hs_ref[...] * _gelu(gb_ref[...])).astype(BF16)

    return pl.pallas_call(
        body, name=name, grid=(E // tc,),
        in_specs=[col, col, col, col, vec, vec, vec],
        out_specs=[col, col], out_shape=[_sds((L, E), F32), _sds((L, E), BF16)],
        compiler_params=_cp(("parallel",)),
    )(pre_r, pre_i, xc, p, b_rg, b_ig, lam)


def lru_scan_bwd(dyv, hs, pre_r, pre_i, xc, p, b_rg, b_ig, lam, name):
    L, E = xc.shape
    tc, col, vec = _lru_specs(L, E)
    nblk = L // SUBLANES

    def body(dyv_ref, hs_ref, pr_ref, pi_ref, xc_ref, gb_ref, brg_ref, big_ref, lam_ref,
             dgb_ref, dpr_ref, dpi_ref, dxc_ref, dbrg_ref, dbig_ref, dlam_ref, t_gb, t_pr, t_pi):
        lam_v = lam_ref[...]
        sp = _softplus(-lam_v)
        brg, big = brg_ref[...], big_ref[...]
        rows = lax.broadcasted_iota(jnp.int32, (SUBLANES, tc), 0)

        def blk(kk, carry):
            gcar, a_next, acc_sp, acc_r, acc_i = carry
            k = nblk - 1 - kk
            r0 = pl.multiple_of(k * SUBLANES, SUBLANES)
            sl = pl.ds(r0, SUBLANES)
            r, ig, a, mult = _lru_gates(pr_ref[sl, :], pi_ref[sl, :], brg, big, sp)
            gbv, hsv, dyvv, xcv = gb_ref[sl, :], hs_ref[sl, :], dyv_ref[sl, :], xc_ref[sl, :]
            t_gb[sl, :] = dyvv * hsv * _gelu_grad(gbv)
            x = dyvv * _gelu(gbv)
            al = jnp.where(rows == SUBLANES - 1, a_next, pltpu.roll(a, SUBLANES - 1, 0))
            for d in (1, 2, 4):
                keep = rows < SUBLANES - d
                x = x + al * jnp.where(keep, pltpu.roll(x, SUBLANES - d, 0), 0.0)
                al = al * jnp.where(keep, pltpu.roll(al, SUBLANES - d, 0), 1.0)
            g = x + al * gcar
            rp = pl.multiple_of(jnp.maximum(k - 1, 0) * SUBLANES, SUBLANES)
            hlast = _row(hs_ref[pl.ds(rp, SUBLANES), :], SUBLANES - 1) * (k > 0).astype(F32)
            hprev = jnp.where(rows == 0, hlast, pltpu.roll(hsv, 1, 0))
            da = g * hprev
            dmult = g * ig * xcv
            dig = g * mult * xcv
            dxc_ref[sl, :] = g * mult * ig
            dla = da * a - dmult * (a * a) / mult
            dpr = dla * (-LRU_C * sp) * r * (1.0 - r)
            dpi = dig * ig * (1.0 - ig)
            t_pr[sl, :] = dpr
            t_pi[sl, :] = dpi
            return (_row(g, 0), _row(a, 0), acc_sp + dla * (-LRU_C * r), acc_r + dpr, acc_i + dpi)

        zero = jnp.zeros((SUBLANES, tc), F32)
        z1 = jnp.zeros((1, tc), F32)
        def trip(kt, carry):
            for q in range(SCAN_UNROLL):
                carry = blk(kt * SCAN_UNROLL + q, carry)
            return carry

        _, _, acc_sp, acc_r, acc_i = lax.fori_loop(0, nblk // SCAN_UNROLL, trip, (z1, z1, zero, zero, zero))
        dgb_ref[...] = t_gb[...].astype(BF16)
        dpr_ref[...] = t_pr[...].astype(BF16)
        dpi_ref[...] = t_pi[...].astype(BF16)
        dbrg_ref[...] = _colsum(acc_r)
        dbig_ref[...] = _colsum(acc_i)
        dlam_ref[...] = -_colsum(acc_sp) * _sig(-lam_v)

    return pl.pallas_call(
        body, name=name, grid=(E // tc,),
        in_specs=[col, col, col, col, col, col, vec, vec, vec],
        out_specs=[col, col, col, col, vec, vec, vec],
        out_shape=[_sds((L, E), BF16), _sds((L, E), BF16), _sds((L, E), BF16), _sds((L, E), F32),
                   _sds((1, E), F32), _sds((1, E), F32), _sds((1, E), F32)],
        scratch_shapes=[pltpu.VMEM((L, tc), F32)] * 3,
        compiler_params=_cp(("parallel",)),
    )(dyv, hs, pre_r, pre_i, xc, p, b_rg, b_ig, lam)


def _place():
    xi, yi, ci = lax.axis_index("x"), lax.axis_index("y"), lax.axis_index("c")
    chips = [(1 - xi, yi), (xi, 1 - yi), (1 - xi, 1 - yi)]
    return xi, yi, ci, chips


_ANY = pl.BlockSpec(memory_space=pl.ANY)


def all_gather_devices(blks, name):
    n = len(blks)

    def body(*refs):
        ins, outs = refs[:n], refs[n:2 * n]
        send_sems, recv_sems, local_sems = refs[2 * n:]
        xi, yi, ci, chips = _place()
        me, sibling = (xi, yi, ci), (xi, yi, 1 - ci)

        def slab(a, px, py, pc):
            return outs[a].at[4 * px + 2 * py + pc]

        def copy(a, k, block, to, src=None):
            return pltpu.make_async_remote_copy(
                src_ref=slab(a, *block) if src is None else src, dst_ref=slab(a, *block),
                send_sem=send_sems.at[7 * a + k], recv_sem=recv_sems.at[7 * a + k], device_id=to,
                device_id_type=MESH)

        mine = [pltpu.make_async_copy(ins[a], slab(a, *me), local_sems.at[a]) for a in range(n)]
        first, passed = [], []
        for a in range(n):
            mine[a].start()
            first.append(copy(a, 0, me, sibling, src=ins[a]))
            first += [copy(a, 1 + j, me, (*chip, ci), src=ins[a]) for j, chip in enumerate(chips)]
        for cp in first:
            cp.start()
        for a in range(n):
            for j, chip in enumerate(chips):
                copy(a, 1 + j, (*chip, ci), me).wait_recv()
                passed.append(copy(a, 4 + j, (*chip, ci), sibling))
                passed[-1].start()
        for a in range(n):
            copy(a, 0, sibling, me).wait_recv()
            for j, chip in enumerate(chips):
                copy(a, 4 + j, (*chip, 1 - ci), me).wait_recv()
        for cp in first + passed:
            cp.wait_send()
        for cp in mine:
            cp.wait()

    return pl.pallas_call(
        body, name=name, in_specs=[_ANY] * n, out_specs=[_ANY] * n,
        out_shape=[_sds((N_DEV,) + b.shape, b.dtype) for b in blks],
        scratch_shapes=[pltpu.SemaphoreType.DMA((7 * n,)), pltpu.SemaphoreType.DMA((7 * n,)),
                        pltpu.SemaphoreType.DMA((n,))],
    )(*blks)


_HBM = pl.BlockSpec(memory_space=pltpu.HBM)
_SEM = pl.BlockSpec(memory_space=pltpu.SEMAPHORE)
_EFFECT = pltpu.SideEffectType.DATAFLOW_SIDE_EFFECTING


def split_start(name, groups, counts, copies_fn):
    flat = [b for g in groups for b in g]
    n, ng = len(flat), len(groups)

    def body(*refs):
        ins, sems, token = refs[:n], refs[n:n + 2 * ng], refs[-1]
        off = 0
        for gi, g in enumerate(groups):
            for cp in copies_fn(ins[off:off + len(g)], [b.shape for b in g], sems[2 * gi], sems[2 * gi + 1]):
                cp.start()
            off += len(g)
        token[...] = jnp.zeros_like(token)

    out_shape = tuple(pltpu.SemaphoreType.DMA((c,)) for c in counts for _ in range(2))
    out_shape += tuple(pltpu.HBM(b.shape, b.dtype) for b in flat) + (_sds((SUBLANES, LANES), F32),)
    outs = pl.pallas_call(
        body, name=name, in_specs=[_HBM] * n, out_shape=out_shape,
        out_specs=tuple([_SEM] * (2 * ng) + [_HBM] * n + [pl.BlockSpec(memory_space=pltpu.VMEM)]),
        input_output_aliases={i: 2 * ng + i for i in range(n)},
        compiler_params=pltpu.CompilerParams(has_side_effects=_EFFECT),
    )(*[pltpu.with_memory_space_constraint(b, pltpu.HBM) for b in flat])
    sems = [(outs[2 * gi], outs[2 * gi + 1]) for gi in range(ng)]
    thru, off = [], 2 * ng
    for g in groups:
        thru.append(list(outs[off:off + len(g)]))
        off += len(g)
    return sems, thru, outs[-1]


def split_wait(name, bufs, sems, copies_fn, after):
    n = len(bufs)
    shapes = [b.shape for b in bufs]

    def body(*refs):
        for cp in copies_fn(refs[:n], shapes, refs[n], refs[n + 1]):
            cp.wait_send()
            cp.wait_recv()

    return list(pl.pallas_call(
        body, name=name, in_specs=[_HBM] * n + [_SEM, _SEM, _ANY],
        out_shape=tuple(pltpu.HBM(b.shape, b.dtype) for b in bufs), out_specs=tuple([_HBM] * n),
        input_output_aliases={i: i for i in range(n)},
        compiler_params=pltpu.CompilerParams(has_side_effects=_EFFECT),
    )(*bufs, sems[0], sems[1], after))


def _gather_copies(bufs, shapes, send_sems, recv_sems):
    xi, yi, ci, chips = _place()
    cps = []
    for a, ref in enumerate(bufs):
        hr = shapes[a][1] // 2
        rows = ref.at[2 * xi + yi, pl.ds(pl.multiple_of(ci * hr, 16), hr), :]
        for j in range(3):
            cps.append(pltpu.make_async_remote_copy(
                src_ref=rows, dst_ref=rows, send_sem=send_sems.at[3 * a + j], recv_sem=recv_sems.at[3 * a + j],
                device_id=(*chips[j], ci), device_id_type=MESH))
    return cps


def _scatter_copies(bufs, shapes, send_sems, recv_sems):
    xi, yi, ci, chips = _place()
    n = len(bufs) // 2
    cps = []
    for a in range(n):
        for j in range(3):
            cps.append(pltpu.make_async_remote_copy(
                src_ref=bufs[a].at[2 * chips[j][0] + chips[j][1]], dst_ref=bufs[n + a].at[j],
                send_sem=send_sems.at[3 * a + j], recv_sem=recv_sems.at[3 * a + j],
                device_id=(*chips[j], ci), device_id_type=MESH))
    return cps


def cast_place(w, layer, chip1, name, after=None):
    _, R, C = w.shape
    tr = _tile(R, tuple(t for t in _TR if t * C <= ROW_TILE_ELEMS) or (16,))
    extra = [] if after is None else [after]

    def body(c_ref, w_ref, *rest):
        rest[-1][...] = w_ref[...].astype(BF16)

    return pl.pallas_call(
        body, name=name,
        grid_spec=pltpu.PrefetchScalarGridSpec(
            num_scalar_prefetch=1, grid=(R // tr,),
            in_specs=[pl.BlockSpec((None, tr, C), lambda i, c: (layer, i, 0))] + [_ANY] * len(extra),
            out_specs=pl.BlockSpec((None, tr, C), lambda i, c: (c[0], i, 0))),
        out_shape=_sds((N_CHIPS, R, C), BF16), compiler_params=_cp(("parallel",)),
    )(chip1, w, *extra)


def _forward_copies(bufs, shapes, send_sems, recv_sems):
    xi, yi, ci, chips = _place()
    cps = []
    for a, ref in enumerate(bufs):
        hr = shapes[a][1] // 2
        for j in range(3):
            rows = ref.at[2 * chips[j][0] + chips[j][1], pl.ds(pl.multiple_of(ci * hr, 16), hr), :]
            cps.append(pltpu.make_async_remote_copy(
                src_ref=rows, dst_ref=rows, send_sem=send_sems.at[3 * a + j], recv_sem=recv_sems.at[3 * a + j],
                device_id=(xi, yi, 1 - ci), device_id_type=MESH))
    return cps


def _halves_copies(bufs, shapes, send_sems, recv_sems):
    xi, yi, ci, _ = _place()
    n = len(bufs) // 2
    cps = []
    for a in range(n):
        hr = shapes[a][1] // 2
        cps.append(pltpu.make_async_remote_copy(
            src_ref=bufs[a].at[:, pl.ds(pl.multiple_of((1 - ci) * hr, 16), hr), :], dst_ref=bufs[n + a],
            send_sem=send_sems.at[a], recv_sem=recv_sems.at[a], device_id=(xi, yi, 1 - ci), device_id_type=MESH))
    return cps


def add_half(g, got, ci, name):
    S, hr, C = got.shape
    tr = _tile(hr, tuple(t for t in _TR if t * C <= ROW_TILE_ELEMS) or (16,))
    nb = hr // tr

    def body(c_ref, g_ref, r_ref, o_ref):
        o_ref[...] = (g_ref[...].astype(F32) + r_ref[...].astype(F32)).astype(BF16)

    return pl.pallas_call(
        body, name=name,
        grid_spec=pltpu.PrefetchScalarGridSpec(
            num_scalar_prefetch=1, grid=(S, nb),
            in_specs=[pl.BlockSpec((None, tr, C), lambda s, i, c: (s, c[0] * nb + i, 0)),
                      pl.BlockSpec((None, tr, C), lambda s, i, c: (s, i, 0))],
            out_specs=pl.BlockSpec((None, tr, C), lambda s, i, c: (s, i, 0))),
        out_shape=_sds((S, hr, C), BF16), compiler_params=_cp(("parallel", "parallel")),
    )(ci, g, got)


def add_chips(part, got, place, name):
    S, hr, C = part.shape
    tr = _tile(hr, tuple(t for t in _TR if t * C <= ROW_TILE_ELEMS) or (16,))
    nb = hr // tr

    def body(c_ref, p_ref, r_ref, o_ref):
        acc = p_ref[...].astype(F32)
        for j in range(3):
            acc = acc + r_ref[j].astype(F32)
        o_ref[...] = acc

    return pl.pallas_call(
        body, name=name,
        grid_spec=pltpu.PrefetchScalarGridSpec(
            num_scalar_prefetch=1, grid=(nb,),
            in_specs=[pl.BlockSpec((None, tr, C), lambda i, c: (c[0], i, 0)),
                      pl.BlockSpec((3, tr, C), lambda i, c: (0, i, 0))],
            out_specs=pl.BlockSpec((tr, C), lambda i, c: (c[1] * nb + i, 0))),
        out_shape=_sds((2 * hr, C), F32), compiler_params=_cp(("parallel",)),
    )(place, part, got)


def join_halves(bufs, name):
    n = len(bufs)

    def body(*refs):
        ins, outs = refs[:n], refs[n:2 * n]
        send_sems, recv_sems = refs[2 * n:]
        xi, yi, ci, _ = _place()

        def copy(ref, a, h):
            hr = bufs[a].shape[0] // 2
            rows = pl.ds(pl.multiple_of(h * hr, 8), hr)
            return pltpu.make_async_remote_copy(
                src_ref=ref[a].at[rows, :], dst_ref=outs[a].at[rows, :], send_sem=send_sems.at[a],
                recv_sem=recv_sems.at[a], device_id=(xi, yi, 1 - ci), device_id_type=MESH)

        sends = [copy(ins, a, ci) for a in range(n)]
        for cp in sends:
            cp.start()
        for a in range(n):
            copy(outs, a, 1 - ci).wait_recv()
        for cp in sends:
            cp.wait_send()

    return pl.pallas_call(
        body, name=name, in_specs=[_ANY] * n, out_specs=[_ANY] * n,
        out_shape=[_sds(b.shape, b.dtype) for b in bufs], input_output_aliases={i: i for i in range(n)},
        scratch_shapes=[pltpu.SemaphoreType.DMA((n,)), pltpu.SemaphoreType.DMA((n,))],
    )(*bufs)


def reduce_scatter_start(grads, tag):
    lands = [lax.empty((N_CHIPS, g.shape[1] // 2, g.shape[2]), g.dtype) for g in grads]
    sems, thru, token = split_start(f"rs_d2d_start_{tag}", [list(grads) + lands], [len(grads)], _halves_copies)
    return (sems[0], thru[0]), token


def reduce_scatter_relay(state, ci1, after, tag):
    sems, bufs = state
    bufs = split_wait(f"rs_d2d_wait_{tag}", bufs, sems, _halves_copies, after)
    n = len(bufs) // 2
    parts = [add_half(bufs[a], bufs[n + a], ci1, f"rs_add_half_{tag}{a}") for a in range(n)]
    lands = [lax.empty((3,) + q.shape[1:], q.dtype) for q in parts]
    sems, thru, token = split_start(f"rs_ici_start_{tag}", [parts + lands], [3 * n], _scatter_copies)
    return (sems[0], thru[0]), token


def reduce_scatter_finish(state, place, after, tag):
    sems, bufs = state
    bufs = split_wait(f"rs_ici_wait_{tag}", bufs, sems, _scatter_copies, after)
    n = len(bufs) // 2
    return [add_chips(bufs[a], bufs[n + a], place, f"rs_add_chips_{tag}{a}") for a in range(n)]


def _pack(parts, width):
    flat = jnp.concatenate([p.reshape(-1).astype(F32) for p in parts])
    per = SUBLANES * width
    total = -(-flat.shape[0] // per) * per
    flat = jnp.pad(flat, (0, total - flat.shape[0]))
    return flat.reshape(total // width, width)


def _unpack(flat, shapes):
    out, off = [], 0
    for s in shapes:
        n = math.prod(s)
        out.append(flat[off:off + n].reshape(s))
        off += n
    return out


_W_NAMES = ['norm_g', 'w_ada', 'b_ada', 's5_w_in', 's5_lam_re', 's5_lam_im', 's5_log_dt', 's5_b_re', 's5_b_im',
            's5_c_re', 's5_c_im', 's5_d', 's5_w_glu', 'lru_w_in', 'lru_conv_w', 'lru_conv_b', 'lru_w_rg',
            'lru_b_rg', 'lru_w_ig', 'lru_b_ig', 'lru_lam', 'lru_w_out', 'ffn_w_gu', 'ffn_w_down', 'final_g']
_BIG = ('w_ada', 's5_w_in', 's5_w_glu', 'lru_w_in', 'lru_w_out', 'ffn_w_gu', 'ffn_w_down')
_MID = ('s5_b_re', 's5_b_im', 's5_c_re', 's5_c_im', 'lru_w_rg', 'lru_w_ig')


def _ffn_fwd(x, gu, w_down, gate, tag):
    act = swiglu_fwd(gu, f"{tag}_act")
    z = mm_nn(act, w_down, name=f"{tag}_down")
    return res_gate_fwd(x, z, gate, f"{tag}_res"), (gu, act, z)


def _ffn_bwd(dx, h, saved, w_gu, w_down, gate, tag, relay=None):
    gu, act, z = saved
    dz, dgate = res_gate_bwd(dx, z, gate, f"{tag}_res_bwd")
    dact = mm_nt(dz, w_down, name=f"{tag}_dact")
    tok = None if relay is None else relay(dact)
    dw_down = mm_tn(act, dz, name=f"{tag}_dwdown", out_dtype=BF16)
    dgu = swiglu_bwd(dact, gu, f"{tag}_act_bwd", after=tok)
    dw_gu = mm_tn(h, dgu, name=f"{tag}_dwgu", out_dtype=BF16, omode="cols", groups=N_CHIPS)
    started, tok = reduce_scatter_start([dw_gu, dw_down.reshape((N_CHIPS, -1) + dw_down.shape[1:])], tag)
    dh = mm_nt(dgu, w_gu, name=f"{tag}_dh", bmode="cols", after=tok)
    return dh, dgate, started


def _step(p):
    xi, yi, ci = lax.axis_index("x"), lax.axis_index("y"), lax.axis_index("c")
    chip = 2 * xi + yi
    me = 2 * chip + ci
    ci1 = jnp.reshape(ci, (1,)).astype(jnp.int32)
    chip1 = jnp.reshape(chip, (1,)).astype(jnp.int32)
    place2 = jnp.stack([chip, ci]).astype(jnp.int32)

    x0 = p['x'][0]
    tgt = p['loss_target'][0]
    L, D = x0.shape
    Dq = D // N_CHIPS
    depth = p['w_ada'].shape[0]
    E = p['lru_lam'].shape[1] * N_CHIPS
    Eq = E // N_CHIPS
    kw = p['lru_conv_w'].shape[1]
    Nq = p['w_ada'].shape[2]
    _, G, P, Cg = p['s5_b_re'].shape
    nb, bs = p['lru_w_rg'].shape[1], p['lru_w_rg'].shape[2]

    pay = _pack([p['c'], p['norm_g'], p['lru_conv_w'], p['lru_conv_b'], p['lru_b_rg'], p['lru_b_ig'],
                 p['lru_lam']], 1024)
    g1 = all_gather_devices([pay], "gather_small_params")[0].reshape(N_DEV, -1)
    c_all = g1[:, :D]
    per_chip = g1[0::2]
    sizes = [(depth, 2, Dq), (kw, Eq), (Eq,), (Eq,), (Eq,), (Eq,)]
    offs = D
    pieces = []
    for s in sizes:
        nel = math.prod(s)
        pieces.append(per_chip[:, offs:offs + nel].reshape((N_CHIPS,) + s))
        offs += nel
    norm_g = jnp.moveaxis(pieces[0], 0, 2).reshape(depth, 2, D)
    conv_w = jnp.moveaxis(pieces[1], 0, 1).reshape(kw, E)
    conv_b, b_rg, b_ig, lam = [q.reshape(1, E) for q in pieces[2:]]

    cond = silu_rows(jnp.pad(c_all, ((0, 16 - N_DEV), (0, 0))), "cond_silu")
    cond_rep = jnp.concatenate([cond] * depth, axis=1)
    mod_part = mm_nn(cond_rep, p['w_ada'], name="mod_proj", bmode="batch")[:N_DEV]
    g2 = all_gather_devices([mod_part], "gather_mod")[0][0::2]
    mine = lax.dynamic_index_in_dim(g2, me, axis=1, keepdims=False).reshape(N_CHIPS, depth, Nq)
    mod = jnp.moveaxis(mine, 0, 1).reshape(depth, N_CHIPS * Nq) + p['b_ada']
    mods = [[mod[i:i + 1, k * D:(k + 1) * D] for k in range(N_MOD)] for i in range(depth)]

    def place(key, layer, tag, after=None):
        return cast_place(p[key], layer, chip1, f"place_{tag}", after=after)

    first = [place('s5_w_in', 0, "s5_in", after=mod), place('s5_w_glu', 0, "s5_glu")]
    sems_a, bufs_a, tok_a = split_start("gather_ici_start_s5", [[w] for w in first], [3] * len(first), _gather_copies)
    others = [[place('ffn_w_gu', 0, "gu0", after=tok_a), place('ffn_w_down', 0, "down0")],
              [place('lru_w_in', 0, "lru_in"), place('lru_w_out', 0, "lru_out")],
              [place('ffn_w_gu', 1, "gu1"), place('ffn_w_down', 1, "down1")]]
    sems_b, bufs_b, tok_b = split_start("gather_ici_start_rest", others, [3 * len(g) for g in others], _gather_copies)
    wsems, wbufs = sems_a + sems_b, bufs_a + bufs_b

    def landed(gi, after, tag):
        bufs = split_wait(f"gather_ici_wait_{tag}", wbufs[gi], wsems[gi], _gather_copies, after)
        sems, thru, tok = split_start(f"gather_d2d_start_{tag}", [bufs], [3 * len(bufs)], _forward_copies)
        return (sems[0], thru[0]), tok

    def weights(state, after, tag):
        return split_wait(f"gather_d2d_wait_{tag}", state[1], state[0], _forward_copies, after)

    s5_small = (p['s5_lam_re'][0], p['s5_lam_im'][0], p['s5_log_dt'][0], p['s5_b_re'][0], p['s5_b_im'][0])
    (ab_re, ab_im, bb_re, bb_im), s5_disc_vjp = jax.vjp(_s5_discretize, *s5_small)
    bsb, csb = _s5_blockdiag(bb_re, bb_im, p['s5_c_re'][0], p['s5_c_im'][0])
    cf, crv = _s5_scan_consts(ab_re, ab_im)
    wsb_rg, wsb_ig = [w.astype(BF16) for w in _lru_blockdiag(p['lru_w_rg'][0], p['lru_w_ig'][0])]
    nsb_lru = wsb_rg.shape[0]

    sh1, sc1, gt1, sh2, sc2, gt2 = mods[0]
    st, tok = landed(0, tok_b, "s5_in")
    s5_w_in = weights(st, tok, "s5_in")[0].reshape(-1, D)
    h0 = norm_mod_fwd(x0, norm_g[0, 0:1], sc1, sh1, "l0_norm1")
    u = mm_nn(h0, s5_w_in, name="s5_in")
    st, tok = landed(1, u, "s5_glu")
    ypre, yg, s_all, ss = s5_scan_fwd(u, p['s5_d'], bsb, csb, cf, "s5_scan")
    s5_w_glu = weights(st, yg, "s5_glu")[0]
    v = mm_nn(yg, s5_w_glu, name="s5_glu", bmode="cols")
    st, tok = landed(2, v, "ffn0")
    x1 = glu_res_fwd(x0, v, gt1, "s5_res")
    h1 = norm_mod_fwd(x1, norm_g[0, 1:2], sc2, sh2, "l0_norm2", after=tok)
    w_gu0, w_down0 = weights(st, h1, "ffn0")
    w_down0 = w_down0.reshape(-1, D)
    gu0 = mm_nn(h1, w_gu0, name="ffn0_gu", out_dtype=BF16, bmode="cols")
    st, tok = landed(3, gu0, "lru")
    x2, ffn0 = _ffn_fwd(x1, gu0, w_down0, gt2, "ffn0")

    sh1b, sc1b, gt1b, sh2b, sc2b, gt2b = mods[1]
    h2 = norm_mod_fwd(x2, norm_g[1, 0:1], sc1b, sh1b, "l1_norm1", after=tok)
    lru_w_in, lru_w_out = weights(st, h2, "lru")
    lru_w_out = lru_w_out.reshape(-1, D)
    pq = mm_nn(h2, lru_w_in, name="lru_in", bmode="cols")
    xc, xcb = lru_conv_fwd(pq, conv_w, conv_b, "lru_conv")
    pre_r = mm_nn(xcb, wsb_rg, name="lru_gate_r", bmode="batch")
    pre_i = mm_nn(xcb, wsb_ig, name="lru_gate_i", bmode="batch")
    hs, yv = lru_scan_fwd(pre_r, pre_i, xc, pq, b_rg, b_ig, lam, "lru_scan")
    st, tok = landed(4, hs, "ffn1")
    mix = mm_nn(yv, lru_w_out, name="lru_out")
    x3 = res_gate_fwd(x2, mix, gt1b, "lru_res")
    h3 = norm_mod_fwd(x3, norm_g[1, 1:2], sc2b, sh2b, "l1_norm2", after=tok)
    w_gu1, w_down1 = weights(st, h3, "ffn1")
    w_down1 = w_down1.reshape(-1, D)
    gu1 = mm_nn(h3, w_gu1, name="ffn1_gu", out_dtype=BF16, bmode="cols")
    x4, ffn1 = _ffn_fwd(x3, gu1, w_down1, gt2b, "ffn1")

    fg = p['final_g'].reshape(1, D)
    dx4, loss_blk, dfinal_g = final_loss(x4, fg, tgt, "final_loss")
    loss = lax.psum(loss_blk[0, 0], ("x", "y", "c"))

    def rows4(g):
        return g.reshape((N_CHIPS, -1) + g.shape[1:])

    dh3, dgt2b, d2d_ffn1 = _ffn_bwd(dx4, h3, ffn1, w_gu1, w_down1, gt2b, "ffn1")
    rs_ffn1, tok = reduce_scatter_relay(d2d_ffn1, ci1, dh3, "ffn1")
    dx3, dgn11, dsc2b, dsh2b = norm_mod_bwd(dh3, x3, norm_g[1, 1:2], sc2b, dx4, "l1_norm2_bwd", after=tok)

    dmix, dgt1b = res_gate_bwd(dx3, mix, gt1b, "lru_res_bwd")
    dyv = mm_nt(dmix, lru_w_out, name="lru_dyv")
    dw_out = mm_tn(yv, dmix, name="lru_dwout", out_dtype=BF16)
    dgb, dpre_r, dpre_i, dxc1, db_rg, db_ig, dlam = lru_scan_bwd(dyv, hs, pre_r, pre_i, xc, pq, b_rg, b_ig, lam,
                                                                "lru_scan_bwd")
    dxc2 = mm_nt(dpre_r, wsb_rg, name="lru_dxc_r", bmode="batch")
    dxc3 = mm_nt(dpre_i, wsb_ig, name="lru_dxc_i", bmode="batch")
    dwsb_rg = mm_tn(xcb, dpre_r, name="lru_dwgate_r", omode="batch", groups=nsb_lru)
    dwsb_ig = mm_tn(xcb, dpre_i, name="lru_dwgate_i", omode="batch", groups=nsb_lru)
    dxb, dconv_w, dconv_b = lru_conv_bwd(dxc1, dxc2, dxc3, pq, conv_w, "lru_conv_bwd")
    dpq = jnp.concatenate([dgb, dxb], axis=1)
    dh2 = mm_nt(dpq, lru_w_in, name="lru_dh", bmode="cols")
    dw_lru_in = mm_tn(h2, dpq, name="lru_dwin", out_dtype=BF16, omode="cols", groups=N_CHIPS)
    d2d_lru, tok = reduce_scatter_start([dw_lru_in, rows4(dw_out)], "lru")
    dx2, dgn10, dsc1b, dsh1b = norm_mod_bwd(dh2, x2, norm_g[1, 0:1], sc1b, dx3, "l1_norm1_bwd", after=tok)

    lru_relay = {}

    def relay_lru(after):
        lru_relay['state'], tok = reduce_scatter_relay(d2d_lru, ci1, after, "lru")
        return tok

    dh1, dgt2, d2d_ffn0 = _ffn_bwd(dx2, h1, ffn0, w_gu0, w_down0, gt2, "ffn0", relay=relay_lru)
    rs_lru = lru_relay['state']
    rs_ffn0, tok = reduce_scatter_relay(d2d_ffn0, ci1, dh1, "ffn0")
    dx1, dgn01, dsc2, dsh2 = norm_mod_bwd(dh1, x1, norm_g[0, 1:2], sc2, dx2, "l0_norm2_bwd", after=tok)

    dv, dgt1 = glu_res_bwd(dx1, v, gt1, "s5_res_bwd")
    dyg = mm_nt(dv, s5_w_glu, name="s5_dyg", bmode="cols")
    dw_glu = mm_tn(yg, dv, name="s5_dwglu", out_dtype=BF16, omode="cols", groups=N_CHIPS)
    du, dbsb, dcsb, da, dd = s5_scan_bwd(u, dyg, ypre, s_all, p['s5_d'], bsb, csb, crv, ss, "s5_scan_bwd")
    dh0 = mm_nt(du, s5_w_in, name="s5_dh")
    dw_s5_in = mm_tn(h0, du, name="s5_dwin", out_dtype=BF16)
    d2d_s5, tok = reduce_scatter_start([rows4(dw_s5_in), dw_glu], "s5")
    grad_x, dgn00, dsc1, dsh1 = norm_mod_bwd(dh0, x0, norm_g[0, 0:1], sc1, dx1, "l0_norm1_bwd", after=tok)

    dmod = jnp.concatenate([jnp.concatenate([dsh1, dsc1, dgt1, dsh2, dsc2, dgt2], axis=1),
                            jnp.concatenate([dsh1b, dsc1b, dgt1b, dsh2b, dsc2b, dgt2b], axis=1)], axis=0)
    dnorm_g = jnp.stack([jnp.concatenate([dgn00, dgn01]), jnp.concatenate([dgn10, dgn11])])
    dbb_re, dbb_im, dc_re, dc_im = _s5_blockdiag_grads(dbsb, dcsb, P, Cg)
    H = S5_SB_GROUPS * P
    da_re, da_im = da[:, 0, :H].reshape(G, P), da[:, 0, H:].reshape(G, P)
    dw_rg, dw_ig = _lru_blockdiag_grad(dwsb_rg, nb, bs), _lru_blockdiag_grad(dwsb_ig, nb, bs)
    small = [dmod, dnorm_g, da_re, da_im, dd, dconv_w, dconv_b, db_rg, db_ig, dlam, dfinal_g]
    small_shapes = [s.shape for s in small]
    payload = _pack(small, 1024)
    mid = [dbb_re, dbb_im, dc_re, dc_im, dw_rg, dw_ig]
    mid_shapes = [s.shape for s in mid]
    gathered = all_gather_devices([payload] + [s.reshape(s.shape[0], -1).astype(BF16) for s in mid],
                                  "gather_small_grads")
    gathered_small = gathered[0]
    rs_s5, tok_s5 = reduce_scatter_relay(d2d_s5, ci1, gathered_small, "s5")
    total = sum_devices(gathered_small, "sum_small_grads").reshape(-1)
    (s_dmod, s_norm_g, s_da_re, s_da_im, s_dd, s_conv_w, s_conv_b, s_b_rg, s_b_ig, s_lam,
     s_final_g) = _unpack(total, small_shapes)
    s_dbb_re, s_dbb_im, s_dc_re, s_dc_im, s_dw_rg, s_dw_ig = [
        sum_devices(g, f"sum_mid_grads_{i}").reshape(s) for i, (g, s) in enumerate(zip(gathered[1:], mid_shapes))]
    g_lam_re, g_lam_im, g_log_dt, g_b_re, g_b_im = s5_disc_vjp((s_da_re, s_da_im, s_dbb_re, s_dbb_im))

    npay = payload.shape[0] * payload.shape[1]
    dmod_all = gathered_small.reshape(N_DEV, npay)[:, :depth * N_MOD * D].reshape(N_DEV, depth, N_CHIPS, Nq)
    dmod_mine = lax.dynamic_index_in_dim(dmod_all, chip, axis=2, keepdims=False).reshape(N_DEV, depth * Nq)
    dmod_mine = jnp.pad(dmod_mine, ((0, 16 - N_DEV), (0, 0)))
    g_w_ada = mm_tn(cond_rep, dmod_mine, name="w_ada_grad", omode="batch", groups=depth)

    def cols(full, width):
        return lax.dynamic_slice_in_dim(full, chip * width, width, axis=full.ndim - 1)

    grads = {
        'norm_g': cols(s_norm_g, Dq), 'w_ada': g_w_ada, 'b_ada': s_dmod,
        's5_lam_re': g_lam_re, 's5_lam_im': g_lam_im, 's5_log_dt': g_log_dt, 's5_b_re': g_b_re, 's5_b_im': g_b_im,
        's5_c_re': s_dc_re, 's5_c_im': s_dc_im, 's5_d': s_dd, 'lru_conv_w': cols(s_conv_w, Eq),
        'lru_conv_b': cols(s_conv_b, Eq), 'lru_w_rg': s_dw_rg, 'lru_b_rg': cols(s_b_rg, Eq),
        'lru_w_ig': s_dw_ig, 'lru_b_ig': cols(s_b_ig, Eq), 'lru_lam': cols(s_lam, Eq), 'final_g': s_final_g,
    }
    grads = {k: g.reshape(p[k].shape) for k, g in grads.items()}

    delta, new_m, new_v = {}, {}, {}

    def adamw_2d(k, rows, after=None):
        w2 = p[k].reshape(rows, -1)
        outs = adamw(w2, grads[k].reshape(w2.shape), p['m_' + k].reshape(w2.shape), p['v_' + k].reshape(w2.shape),
                     f"adamw_{k}", after=after)
        delta[k], new_m[k], new_v[k] = [o.reshape(p[k].shape) for o in outs]

    adamw_2d('w_ada', depth * D, after=tok_s5)
    for k in _MID:
        adamw_2d(k, p[k].shape[1])
    rest = [k for k in _W_NAMES if k not in _BIG + _MID]
    shapes = [p[k].shape for k in rest]
    packed = [_pack([src[pre_ + k] if pre_ else src[k] for k in rest], 1024)
              for src, pre_ in ((p, ''), (grads, ''), (p, 'm_'), (p, 'v_'))]
    outs = adamw(*packed, "adamw_small")
    for dst, o in zip((delta, new_m, new_v), outs):
        for k, val in zip(rest, _unpack(o.reshape(-1), shapes)):
            dst[k] = val

    done = delta['w_ada']
    halves = []
    for state, tag in ((rs_ffn1, "ffn1"), (rs_lru, "lru"), (rs_ffn0, "ffn0"), (rs_s5, "s5")):
        halves += reduce_scatter_finish(state, place2, done, tag)
    g_gu1, g_down1, g_lru_in, g_lru_out, g_gu0, g_down0, g_s5_in, g_s5_glu = join_halves(halves, "rs_join_halves")
    grads.update({'s5_w_in': g_s5_in[None], 's5_w_glu': g_s5_glu[None], 'lru_w_in': g_lru_in[None],
                  'lru_w_out': g_lru_out[None], 'ffn_w_gu': jnp.stack([g_gu0, g_gu1]),
                  'ffn_w_down': jnp.stack([g_down0, g_down1])})
    for k in _BIG[1:]:
        adamw_2d(k, math.prod(p[k].shape[:-1]))

    return (loss, grad_x[None], *[grads[k] for k in _W_NAMES], *[delta[k] for k in _W_NAMES],
            *[new_m[k] for k in _W_NAMES], *[new_v[k] for k in _W_NAMES])


_IN_NAMES = (['x', 'c'] + _W_NAMES + ['loss_target'] + ['m_' + k for k in _W_NAMES] + ['v_' + k for k in _W_NAMES])


def kernel(x, c, norm_g, w_ada, b_ada, s5_w_in, s5_lam_re, s5_lam_im, s5_log_dt, s5_b_re, s5_b_im, s5_c_re, s5_c_im, s5_d, s5_w_glu, lru_w_in, lru_conv_w, lru_conv_b, lru_w_rg, lru_b_rg, lru_w_ig, lru_b_ig, lru_lam, lru_w_out, ffn_w_gu, ffn_w_down, final_g, loss_target, m_norm_g, m_w_ada, m_b_ada, m_s5_w_in, m_s5_lam_re, m_s5_lam_im, m_s5_log_dt, m_s5_b_re, m_s5_b_im, m_s5_c_re, m_s5_c_im, m_s5_d, m_s5_w_glu, m_lru_w_in, m_lru_conv_w, m_lru_conv_b, m_lru_w_rg, m_lru_b_rg, m_lru_w_ig, m_lru_b_ig, m_lru_lam, m_lru_w_out, m_ffn_w_gu, m_ffn_w_down, m_final_g, v_norm_g, v_w_ada, v_b_ada, v_s5_w_in, v_s5_lam_re, v_s5_lam_im, v_s5_log_dt, v_s5_b_re, v_s5_b_im, v_s5_c_re, v_s5_c_im, v_s5_d, v_s5_w_glu, v_lru_w_in, v_lru_conv_w, v_lru_conv_b, v_lru_w_rg, v_lru_b_rg, v_lru_w_ig, v_lru_b_ig, v_lru_lam, v_lru_w_out, v_ffn_w_gu, v_ffn_w_down, v_final_g):
    args = (x, c, norm_g, w_ada, b_ada, s5_w_in, s5_lam_re, s5_lam_im, s5_log_dt, s5_b_re, s5_b_im, s5_c_re, s5_c_im, s5_d, s5_w_glu, lru_w_in, lru_conv_w, lru_conv_b, lru_w_rg, lru_b_rg, lru_w_ig, lru_b_ig, lru_lam, lru_w_out, ffn_w_gu, ffn_w_down, final_g, loss_target, m_norm_g, m_w_ada, m_b_ada, m_s5_w_in, m_s5_lam_re, m_s5_lam_im, m_s5_log_dt, m_s5_b_re, m_s5_b_im, m_s5_c_re, m_s5_c_im, m_s5_d, m_s5_w_glu, m_lru_w_in, m_lru_conv_w, m_lru_conv_b, m_lru_w_rg, m_lru_b_rg, m_lru_w_ig, m_lru_b_ig, m_lru_lam, m_lru_w_out, m_ffn_w_gu, m_ffn_w_down, m_final_g, v_norm_g, v_w_ada, v_b_ada, v_s5_w_in, v_s5_lam_re, v_s5_lam_im, v_s5_log_dt, v_s5_b_re, v_s5_b_im, v_s5_c_re, v_s5_c_im, v_s5_d, v_s5_w_glu, v_lru_w_in, v_lru_conv_w, v_lru_conv_b, v_lru_w_rg, v_lru_b_rg, v_lru_w_ig, v_lru_b_ig, v_lru_lam, v_lru_w_out, v_ffn_w_gu, v_ffn_w_down, v_final_g)
    return _step(dict(zip(_IN_NAMES, args)))
```

```python
import functools
import math

import jax
import jax.numpy as jnp
from jax import lax
from jax.experimental import pallas as pl
from jax.experimental.pallas import tpu as pltpu

F32 = jnp.float32
BF16 = jnp.bfloat16
MESH = pl.DeviceIdType.MESH

EPS = 1e-6
LRU_C = 8.0
N_MOD = 6
ADAM_LR = 0.001
ADAM_B1 = 0.9
ADAM_B2 = 0.999
ADAM_EPS = 1e-08
ADAM_WD = 0.01
ADAM_STEP = 10

N_CHIPS = 4
N_DEV = 8
SUBLANES = 8
LANES = 128
S5_SB_GROUPS = 8
V7X_VMEM_LIMIT = 48 * 1024 * 1024
ROW_TILE_ELEMS = 512 * 1024
ROW_CALL_ELEMS = 4 * 1024 * 1024
SCAN_UNROLL = 4
S5_CHUNKS = (1024, 512, 256, 128, 64, 32, 16, 8)
LRU_TILE = 256

_TM = (1024, 1408, 512, 256, 128, 64, 32, 16, 8)
_TN = (1024, 1408, 512, 384, 256, 128)
_TK = (1024, 1408, 512, 256, 128)
_TR = (256, 128, 64, 32, 16, 8)

_GELU_K0 = math.sqrt(2.0 / math.pi)
_GELU_K1 = 0.044715


def _tile(n, cands):
    for t in cands:
        if n % t == 0:
            return t
    return n


def _cp(sem=None):
    return pltpu.CompilerParams(dimension_semantics=sem, vmem_limit_bytes=V7X_VMEM_LIMIT)


def _sds(shape, dtype):
    return jax.ShapeDtypeStruct(shape, dtype)


def _sig(x):
    return 1.0 / (1.0 + jnp.exp(-x))


def _gelu(x):
    t = jnp.tanh(_GELU_K0 * (x + _GELU_K1 * x * x * x))
    return 0.5 * x * (1.0 + t)


def _gelu_grad(x):
    x2 = x * x
    t = jnp.tanh(_GELU_K0 * (x + _GELU_K1 * x * x2))
    return 0.5 * (1.0 + t) + 0.5 * x * (1.0 - t * t) * _GELU_K0 * (1.0 + 3.0 * _GELU_K1 * x2)


def _softplus(z):
    return jnp.maximum(z, 0.0) + jnp.log(1.0 + jnp.exp(-jnp.abs(z)))


def _neg_expm1(x):
    series = -x * (1.0 + x * (0.5 + x * (1.0 / 6.0 + x * (1.0 / 24.0))))
    return jnp.where(x > -0.05, series, 1.0 - jnp.exp(x))


def _row(x, r):
    return x[r:r + 1, :]


def _colsum(x):
    return jnp.sum(x, axis=0, keepdims=True)


_NN = (((1,), (0,)), ((), ()))
_NT = (((1,), (1,)), ((), ()))
_TN_DIMS = (((0,), (0,)), ((), ()))


def _mm_call(name, a, b, a_spec, b_spec, o_spec, grid, out_shape, acc_shape, dims, after=None):
    nk = grid[-1]
    kaxis = len(grid) - 1
    extra = [] if after is None else [after]

    def body(a_ref, b_ref, *rest):
        o_ref, acc_ref = rest[-2], rest[-1]
        k = pl.program_id(kaxis)

        def prod():
            return lax.dot_general(a_ref[...].astype(BF16), b_ref[...].astype(BF16), dims,
                                   preferred_element_type=F32)

        if nk == 1:
            o_ref[...] = prod().astype(o_ref.dtype)
            return

        @pl.when(k == 0)
        def _():
            acc_ref[...] = prod()

        if nk > 2:
            @pl.when(jnp.logical_and(k > 0, k < nk - 1))
            def _():
                acc_ref[...] += prod()

        @pl.when(k == nk - 1)
        def _():
            o_ref[...] = (acc_ref[...] + prod()).astype(o_ref.dtype)

    return pl.pallas_call(
        body, name=name, grid=grid, in_specs=[a_spec, b_spec] + [pl.BlockSpec(memory_space=pl.ANY)] * len(extra),
        out_specs=o_spec, out_shape=out_shape, scratch_shapes=[pltpu.VMEM(acc_shape, F32)],
        compiler_params=_cp(("parallel", "parallel", "parallel", "arbitrary")),
    )(a, b, *extra)


def mm_nn(a, b, *, name, out_dtype=F32, bmode="plain"):
    M = a.shape[0]
    if bmode == "plain":
        G, S = 1, 1
        K, Nc = b.shape
    elif bmode == "cols":
        G = 1
        S, K, Nc = b.shape
    else:
        S = 1
        G, K, Nc = b.shape
    tm, tn, tk = _tile(M, _TM), _tile(Nc, _TN), _tile(K, _TK)
    nkb, nnb = K // tk, Nc // tn
    ncol = S * nnb
    grid = (G, M // tm, ncol, nkb)
    a_spec = pl.BlockSpec((tm, tk), lambda g, i, j, k: (i, g * nkb + k))
    if bmode == "plain":
        b_spec = pl.BlockSpec((tk, tn), lambda g, i, j, k: (k, j))
    elif bmode == "cols":
        b_spec = pl.BlockSpec((None, tk, tn), lambda g, i, j, k: (j // nnb, k, j % nnb))
    else:
        b_spec = pl.BlockSpec((None, tk, tn), lambda g, i, j, k: (g, k, j))
    o_spec = pl.BlockSpec((tm, tn), lambda g, i, j, k: (i, g * ncol + j))
    return _mm_call(name, a, b, a_spec, b_spec, o_spec, grid, _sds((M, G * S * Nc), out_dtype), (tm, tn), _NN)


def mm_nt(a, b, *, name, out_dtype=F32, bmode="plain", after=None):
    M = a.shape[0]
    if bmode == "plain":
        G, S = 1, 1
        Ko, Nc = b.shape
    elif bmode == "cols":
        G = 1
        S, Ko, Nc = b.shape
    else:
        S = 1
        G, Ko, Nc = b.shape
    tm, to, tc = _tile(M, _TM), _tile(Ko, _TN), _tile(Nc, _TK)
    npc = Nc // tc
    nc = S * npc
    nob = Ko // to
    grid = (G, M // tm, nob, nc)
    a_spec = pl.BlockSpec((tm, tc), lambda g, i, j, n: (i, g * nc + n))
    if bmode == "plain":
        b_spec = pl.BlockSpec((to, tc), lambda g, i, j, n: (j, n))
    elif bmode == "cols":
        b_spec = pl.BlockSpec((None, to, tc), lambda g, i, j, n: (n // npc, j, n % npc))
    else:
        b_spec = pl.BlockSpec((None, to, tc), lambda g, i, j, n: (g, j, n))
    o_spec = pl.BlockSpec((tm, to), lambda g, i, j, n: (i, g * nob + j))
    return _mm_call(name, a, b, a_spec, b_spec, o_spec, grid, _sds((M, G * Ko), out_dtype), (tm, to), _NT,
                    after=after)


def mm_tn(a, b, *, name, out_dtype=F32, omode="plain", groups=1):
    L = a.shape[0]
    G = groups if omode == "batch" else 1
    S = groups if omode == "cols" else 1
    Mo, N = a.shape[1] // G, b.shape[1] // G
    Nc = N // S
    tm, tn, tl = _tile(Mo, _TM), _tile(Nc, _TN), _tile(L, _TK)
    nmb, nnb = Mo // tm, N // tn
    npj = Nc // tn
    grid = (G, nmb, nnb, L // tl)
    a_spec = pl.BlockSpec((tl, tm), lambda g, i, j, l: (l, g * nmb + i))
    b_spec = pl.BlockSpec((tl, tn), lambda g, i, j, l: (l, g * nnb + j))
    if omode == "plain":
        o_spec = pl.BlockSpec((tm, tn), lambda g, i, j, l: (i, j))
        oshape = (Mo, N)
    elif omode == "cols":
        o_spec = pl.BlockSpec((None, tm, tn), lambda g, i, j, l: (j // npj, i, j % npj))
        oshape = (S, Mo, Nc)
    else:
        o_spec = pl.BlockSpec((None, tm, tn), lambda g, i, j, l: (g, i, j))
        oshape = (G, Mo, N)
    return _mm_call(name, a, b, a_spec, b_spec, o_spec, grid, _sds(oshape, out_dtype), (tm, tn), _TN_DIMS)


def _row_call(name, body, row_ins, vec_ins, row_outs, acc_outs=(), after=None):
    if after is not None:
        n_in = len(row_ins) + len(vec_ins)
        inner = body

        def body(*refs):
            inner(*refs[:n_in], *refs[n_in + 1:])

        return _row_call_impl(name, body, row_ins, vec_ins, row_outs, acc_outs, [after])
    return _row_call_impl(name, body, row_ins, vec_ins, row_outs, acc_outs, [])


def _row_call_impl(name, body, row_ins, vec_ins, row_outs, acc_outs, extra):
    L = row_ins[0].shape[0]
    wmax = max([a.shape[1] for a in row_ins] + [w for w, _ in row_outs])
    narr = len(row_ins) + len(row_outs)
    tr = _tile(L, tuple(t for t in _TR if t * wmax * narr <= ROW_CALL_ELEMS) or (SUBLANES,))
    in_specs = [pl.BlockSpec((tr, a.shape[1]), lambda i: (i, 0)) for a in row_ins]
    in_specs += [pl.BlockSpec(v.shape, lambda i, nd=v.ndim: (0,) * nd) for v in vec_ins]
    in_specs += [pl.BlockSpec(memory_space=pl.ANY) for _ in extra]
    out_shape = [_sds((L, w), dt) for w, dt in row_outs] + [_sds(s, dt) for s, dt in acc_outs]
    out_specs = [pl.BlockSpec((tr, w), lambda i: (i, 0)) for w, _ in row_outs]
    out_specs += [pl.BlockSpec(s, lambda i, nd=len(s): (0,) * nd) for s, _ in acc_outs]
    sem = ("arbitrary",) if acc_outs else ("parallel",)
    return pl.pallas_call(body, name=name, grid=(L // tr,), in_specs=in_specs, out_specs=out_specs,
                          out_shape=out_shape, compiler_params=_cp(sem))(*row_ins, *vec_ins, *extra)


def silu_rows(x, name, after=None):
    def body(x_ref, o_ref):
        v = x_ref[...]
        o_ref[...] = (v * _sig(v)).astype(o_ref.dtype)
    return _row_call(name, body, [x], [], [(x.shape[1], BF16)], after=after)[0]


def norm_mod_fwd(x, gain, sc, sh, name, after=None):
    def body(x_ref, g_ref, sc_ref, sh_ref, h_ref):
        v = x_ref[...]
        r = lax.rsqrt(jnp.mean(v * v, axis=-1, keepdims=True) + EPS)
        h_ref[...] = (v * r * g_ref[...] * (1.0 + sc_ref[...]) + sh_ref[...]).astype(BF16)
    return _row_call(name, body, [x], [gain, sc, sh], [(x.shape[1], BF16)], after=after)[0]


def norm_mod_bwd(dh, x, gain, sc, dres, name, after=None):
    D = x.shape[1]

    def body(dh_ref, x_ref, dres_ref, g_ref, sc_ref, dx_ref, dg_ref, dsc_ref, dsh_ref):
        @pl.when(pl.program_id(0) == 0)
        def _():
            dg_ref[...] = jnp.zeros_like(dg_ref)
            dsc_ref[...] = jnp.zeros_like(dsc_ref)
            dsh_ref[...] = jnp.zeros_like(dsh_ref)

        v = x_ref[...]
        dh_v = dh_ref[...]
        g = g_ref[...]
        r = lax.rsqrt(jnp.mean(v * v, axis=-1, keepdims=True) + EPS)
        xhat = v * r
        dn = dh_v * (1.0 + sc_ref[...])
        dsc_ref[...] += _colsum(dh_v * xhat * g)
        dsh_ref[...] += _colsum(dh_v)
        dg_ref[...] += _colsum(dn * xhat)
        t = dn * g
        dx_ref[...] = dres_ref[...] + r * (t - xhat * jnp.mean(t * xhat, axis=-1, keepdims=True))

    acc = [((1, D), F32)] * 3
    return _row_call(name, body, [dh, x, dres], [gain, sc], [(D, F32)], acc, after=after)


def final_loss(x, gain, tgt, name):
    D = x.shape[1]

    def body(x_ref, t_ref, g_ref, dx_ref, loss_ref, dg_ref, acc_ref):
        i = pl.program_id(0)

        @pl.when(i == 0)
        def _():
            dg_ref[...] = jnp.zeros_like(dg_ref)
            acc_ref[...] = jnp.zeros_like(acc_ref)

        v = x_ref[...]
        g = g_ref[...]
        r = lax.rsqrt(jnp.mean(v * v, axis=-1, keepdims=True) + EPS)
        xhat = v * r
        err = xhat * g - t_ref[...]
        acc_ref[...] += _colsum(err * err)
        dout = err * (1.0 / D)
        dg_ref[...] += _colsum(dout * xhat)
        t = dout * g
        dx_ref[...] = r * (t - xhat * jnp.mean(t * xhat, axis=-1, keepdims=True))

        @pl.when(i == pl.num_programs(0) - 1)
        def _():
            loss_ref[...] = jnp.zeros_like(loss_ref) + jnp.sum(acc_ref[...]) * (0.5 / D)

    return _row_call(name, body, [x, tgt], [gain], [(D, F32)],
                     [((SUBLANES, LANES), F32), ((1, D), F32), ((1, D), F32)])[:3]


def res_gate_fwd(x, z, g, name):
    def body(x_ref, z_ref, g_ref, o_ref):
        o_ref[...] = x_ref[...] + g_ref[...] * z_ref[...]
    return _row_call(name, body, [x, z], [g], [(x.shape[1], F32)])[0]


def res_gate_bwd(dx, z, g, name):
    D = dx.shape[1]

    def body(dx_ref, z_ref, g_ref, dz_ref, dg_ref):
        @pl.when(pl.program_id(0) == 0)
        def _():
            dg_ref[...] = jnp.zeros_like(dg_ref)
        d = dx_ref[...]
        dz_ref[...] = (g_ref[...] * d).astype(BF16)
        dg_ref[...] += _colsum(d * z_ref[...])
    return _row_call(name, body, [dx, z], [g], [(D, BF16)], [((1, D), F32)])


def glu_res_fwd(x, v, g, name):
    D = x.shape[1]

    def body(x_ref, v_ref, g_ref, o_ref):
        vv = v_ref[...]
        o_ref[...] = x_ref[...] + g_ref[...] * (vv[:, :D] * _sig(vv[:, D:]))
    return _row_call(name, body, [x, v], [g], [(D, F32)])[0]


def glu_res_bwd(dx, v, g, name, after=None):
    D = dx.shape[1]

    def body(dx_ref, v_ref, g_ref, dv_ref, dg_ref):
        @pl.when(pl.program_id(0) == 0)
        def _():
            dg_ref[...] = jnp.zeros_like(dg_ref)
        d = dx_ref[...]
        vv = v_ref[...]
        val = vv[:, :D]
        s = _sig(vv[:, D:])
        dg_ref[...] += _colsum(d * val * s)
        dm = g_ref[...] * d
        dv_ref[:, :D] = (dm * s).astype(BF16)
        dv_ref[:, D:] = (dm * val * s * (1.0 - s)).astype(BF16)
    return _row_call(name, body, [dx, v], [g], [(2 * D, BF16)], [((1, D), F32)], after=after)


def swiglu_fwd(gu, name):
    F = gu.shape[1] // 2

    def body(gu_ref, o_ref):
        v = gu_ref[...].astype(F32)
        g = v[:, :F]
        o_ref[...] = (g * _sig(g) * v[:, F:]).astype(BF16)
    return _row_call(name, body, [gu], [], [(F, BF16)])[0]


def swiglu_bwd(dact, gu, name, after=None):
    F = gu.shape[1] // 2

    def body(da_ref, gu_ref, o_ref):
        v = gu_ref[...].astype(F32)
        g, u = v[:, :F], v[:, F:]
        da = da_ref[...]
        s = _sig(g)
        o_ref[:, :F] = (da * u * s * (1.0 + g * (1.0 - s))).astype(BF16)
        o_ref[:, F:] = (da * g * s).astype(BF16)
    return _row_call(name, body, [dact, gu], [], [(2 * F, BF16)], after=after)[0]


def adamw(w, g, m, v, name, after=None):
    C = w.shape[1]
    c1 = 1.0 - ADAM_B1 ** ADAM_STEP
    c2 = 1.0 - ADAM_B2 ** ADAM_STEP

    def body(w_ref, g_ref, m_ref, v_ref, d_ref, m2_ref, v2_ref):
        gv = g_ref[...]
        m2 = ADAM_B1 * m_ref[...] + (1.0 - ADAM_B1) * gv
        v2 = ADAM_B2 * v_ref[...] + (1.0 - ADAM_B2) * (gv * gv)
        m2_ref[...] = m2
        v2_ref[...] = v2
        d_ref[...] = -ADAM_LR * ((m2 / c1) / (jnp.sqrt(v2 / c2) + ADAM_EPS) + ADAM_WD * w_ref[...])
    return _row_call(name, body, [w, g, m, v], [], [(C, F32)] * 3, after=after)


def sum_devices(parts, name):
    n, R, C = parts.shape
    min_rows = 16 if parts.dtype == BF16 else SUBLANES
    tr = _tile(R, tuple(t for t in _TR if t * C * n <= 4 * ROW_TILE_ELEMS and t >= min_rows) or (min_rows,))

    def body(p_ref, o_ref):
        acc = p_ref[0].astype(F32)
        for d in range(1, n):
            acc = acc + p_ref[d].astype(F32)
        o_ref[...] = acc
    return pl.pallas_call(body, name=name, grid=(R // tr,),
                          in_specs=[pl.BlockSpec((n, tr, C), lambda i: (0, i, 0))],
                          out_specs=pl.BlockSpec((tr, C), lambda i: (i, 0)), out_shape=_sds((R, C), F32),
                          compiler_params=_cp(("parallel",)))(parts)


def _s5_discretize(lam_re, lam_im, log_dt, b_re, b_im):
    dt = jnp.exp(log_dt)[:, None]
    mag = jnp.exp(lam_re * dt)
    ab_re = mag * jnp.cos(lam_im * dt)
    ab_im = mag * jnp.sin(lam_im * dt)
    nr, ni = ab_re - 1.0, ab_im
    den = lam_re * lam_re + lam_im * lam_im
    f_re = (nr * lam_re + ni * lam_im) / den
    f_im = (ni * lam_re - nr * lam_im) / den
    bb_re = f_re[..., None] * b_re - f_im[..., None] * b_im
    bb_im = f_re[..., None] * b_im + f_im[..., None] * b_re
    return ab_re, ab_im, bb_re, bb_im


def _s5_blockdiag(bb_re, bb_im, c_re, c_im):
    G, P, Cg = bb_re.shape
    nsb = G // S5_SB_GROUPS

    def bmat(bb):
        return _block_diag(jnp.swapaxes(bb.reshape(nsb, S5_SB_GROUPS, P, Cg), 2, 3))

    def cmat(cc):
        return _block_diag(jnp.swapaxes(cc.reshape(nsb, S5_SB_GROUPS, Cg, P), 2, 3))

    bsb = jnp.concatenate([bmat(bb_re), bmat(bb_im)], axis=-1)
    csb = jnp.concatenate([cmat(c_re), -cmat(c_im)], axis=1)
    return bsb, csb


def _block_diag(t):
    ng, b = t.shape[1], t.shape[3]
    rows = [jnp.pad(t[:, g], ((0, 0), (0, 0), (g * b, (ng - 1 - g) * b))) for g in range(ng)]
    return jnp.concatenate(rows, axis=1)


def _diag_blocks(m, ng):
    a, b = m.shape[1] // ng, m.shape[2] // ng
    return jnp.stack([m[:, g * a:(g + 1) * a, g * b:(g + 1) * b] for g in range(ng)], axis=1)


def _s5_blockdiag_grads(dbsb, dcsb, P, Cg):
    H = S5_SB_GROUPS * P

    def blocks(m):
        return jnp.swapaxes(_diag_blocks(m, S5_SB_GROUPS), 2, 3)

    dbb_re = blocks(dbsb[:, :, :H]).reshape(-1, P, Cg)
    dbb_im = blocks(dbsb[:, :, H:]).reshape(-1, P, Cg)
    dc_re = blocks(dcsb[:, :H, :]).reshape(-1, Cg, P)
    dc_im = -blocks(dcsb[:, H:, :]).reshape(-1, Cg, P)
    return dbb_re, dbb_im, dc_re, dc_im


def _s5_scan_consts(ab_re, ab_im):
    G, P = ab_re.shape
    nsb = G // S5_SB_GROUPS
    H = S5_SB_GROUPS * P
    ar, ai = ab_re.reshape(nsb, 1, H), ab_im.reshape(nsb, 1, H)
    pows = [(ar, ai)]
    for _ in range(SUBLANES - 1):
        pr, pi_ = pows[-1]
        pows.append((pr * ar - pi_ * ai, pr * ai + pi_ * ar))
    rows = jnp.arange(SUBLANES).reshape(1, SUBLANES, 1)

    def masked(k, keep):
        pr, pi_ = pows[k - 1]
        return jnp.where(keep, pr, 0.0), jnp.where(keep, pi_, 0.0)

    def per_row(sel):
        pr = jnp.concatenate([pows[sel(r) - 1][0] for r in range(SUBLANES)], axis=1)
        pi_ = jnp.concatenate([pows[sel(r) - 1][1] for r in range(SUBLANES)], axis=1)
        return pr, pi_

    fwd = [masked(1, rows >= 1), masked(2, rows >= 2), masked(4, rows >= 4), per_row(lambda r: r + 1)]
    rev = [masked(1, rows < 7), masked(2, rows < 6), masked(4, rows < 4), per_row(lambda r: SUBLANES - r)]

    def pack(lst, conj):
        sgn = -1.0 if conj else 1.0
        return jnp.stack([jnp.concatenate([jnp.broadcast_to(pr, (nsb, SUBLANES, H)),
                                           sgn * jnp.broadcast_to(pi_, (nsb, SUBLANES, H))], axis=-1)
                          for pr, pi_ in lst], axis=1)

    return pack(fwd, False), pack(rev, True)


def _cmadd(xr, xi, ar, ai, yr, yi):
    return xr + ar * yr - ai * yi, xi + ar * yi + ai * yr


def _s5_scan_fwd_loop(src_ref, dst_ref, cf_ref, cr, ci, nblk, H):
    def body(k, carry):
        cr, ci = carry
        r0 = pl.multiple_of(k * SUBLANES, SUBLANES)
        xr = src_ref[pl.ds(r0, SUBLANES), pl.ds(0, H)]
        xi = src_ref[pl.ds(r0, SUBLANES), pl.ds(H, H)]
        for idx, d in enumerate((1, 2, 4)):
            xr, xi = _cmadd(xr, xi, cf_ref[idx, :, pl.ds(0, H)], cf_ref[idx, :, pl.ds(H, H)],
                            pltpu.roll(xr, d, 0), pltpu.roll(xi, d, 0))
        xr, xi = _cmadd(xr, xi, cf_ref[3, :, pl.ds(0, H)], cf_ref[3, :, pl.ds(H, H)], cr, ci)
        dst_ref[pl.ds(r0, SUBLANES), pl.ds(0, H)] = xr
        dst_ref[pl.ds(r0, SUBLANES), pl.ds(H, H)] = xi
        return _row(xr, SUBLANES - 1), _row(xi, SUBLANES - 1)

    return lax.fori_loop(0, nblk, body, (cr, ci))


def s5_scan_fwd(u, d_skip, bsb, csb, cf, name):
    L, W = u.shape
    nsb, GW, H2 = bsb.shape
    H = H2 // 2
    Tc = _tile(L, S5_CHUNKS)
    nch = L // Tc

    def body(u_ref, d_ref, b_ref, c_ref, cf_ref, ypre_ref, yg_ref, s_ref, ss_ref, bu_scr, car_scr):
        @pl.when(pl.program_id(1) == 0)
        def _():
            car_scr[...] = jnp.zeros_like(car_scr)

        ss_ref[...] = car_scr[...]
        ub = u_ref[...]
        bu_scr[...] = jnp.dot(ub.astype(BF16), b_ref[...], preferred_element_type=F32)
        cr, ci = _s5_scan_fwd_loop(bu_scr, s_ref, cf_ref, car_scr[:, pl.ds(0, H)], car_scr[:, pl.ds(H, H)],
                                   Tc // SUBLANES, H)
        car_scr[:, pl.ds(0, H)] = cr
        car_scr[:, pl.ds(H, H)] = ci
        ypre = jnp.dot(s_ref[...].astype(BF16), c_ref[...], preferred_element_type=F32) + d_ref[...] * ub
        ypre_ref[...] = ypre
        yg_ref[...] = _gelu(ypre).astype(BF16)

    return pl.pallas_call(
        body, name=name, grid=(nsb, nch),
        in_specs=[pl.BlockSpec((Tc, GW), lambda j, i: (i, j)),
                  pl.BlockSpec((1, GW), lambda j, i: (0, j)),
                  pl.BlockSpec((None, GW, H2), lambda j, i: (j, 0, 0)),
                  pl.BlockSpec((None, H2, GW), lambda j, i: (j, 0, 0)),
                  pl.BlockSpec((None, 4, SUBLANES, H2), lambda j, i: (j, 0, 0, 0))],
        out_specs=[pl.BlockSpec((Tc, GW), lambda j, i: (i, j)),
                   pl.BlockSpec((Tc, GW), lambda j, i: (i, j)),
                   pl.BlockSpec((Tc, H2), lambda j, i: (i, j)),
                   pl.BlockSpec((None, None, 1, H2), lambda j, i: (i, j, 0, 0))],
        out_shape=[_sds((L, W), F32), _sds((L, W), BF16), _sds((L, nsb * H2), F32),
                   _sds((nch, nsb, 1, H2), F32)],
        scratch_shapes=[pltpu.VMEM((Tc, H2), F32), pltpu.VMEM((1, H2), F32)],
        compiler_params=_cp(("arbitrary", "arbitrary")),
    )(u, d_skip, bsb.astype(BF16), csb.astype(BF16), cf)


def s5_scan_bwd(u, dyg, ypre, s_all, d_skip, bsb, csb, crv, ss, name):
    L, W = u.shape
    nsb, GW, H2 = bsb.shape
    H = H2 // 2
    Tc = _tile(L, S5_CHUNKS)
    nch = L // Tc
    nblk = Tc // SUBLANES
    bsb_t = jnp.swapaxes(bsb, 1, 2).astype(BF16)
    csb_t = jnp.swapaxes(csb, 1, 2).astype(BF16)

    def body(u_ref, dyg_ref, yp_ref, s_ref, d_ref, bt_ref, ct_ref, crv_ref, ss_ref,
             du_ref, db_ref, dc_ref, da_ref, dd_ref, g_scr, gcar_scr):
        @pl.when(pl.program_id(1) == 0)
        def _():
            gcar_scr[...] = jnp.zeros_like(gcar_scr)
            db_ref[...] = jnp.zeros_like(db_ref)
            dc_ref[...] = jnp.zeros_like(dc_ref)
            da_ref[...] = jnp.zeros_like(da_ref)
            dd_ref[...] = jnp.zeros_like(dd_ref)

        ub = u_ref[...]
        ubf = ub.astype(BF16)
        dyp = dyg_ref[...] * _gelu_grad(yp_ref[...])
        dypb = dyp.astype(BF16)
        dd_ref[...] += _colsum(dyp * ub)
        g_scr[...] = jnp.dot(dypb, ct_ref[...], preferred_element_type=F32)
        rows = lax.broadcasted_iota(jnp.int32, (SUBLANES, H), 0)

        def rev(kk, carry):
            gr, gi, acc_r, acc_i = carry
            k = nblk - 1 - kk
            r0 = pl.multiple_of(k * SUBLANES, SUBLANES)
            xr = g_scr[pl.ds(r0, SUBLANES), pl.ds(0, H)]
            xi = g_scr[pl.ds(r0, SUBLANES), pl.ds(H, H)]
            for idx, d in enumerate((1, 2, 4)):
                xr, xi = _cmadd(xr, xi, crv_ref[idx, :, pl.ds(0, H)], crv_ref[idx, :, pl.ds(H, H)],
                                pltpu.roll(xr, SUBLANES - d, 0), pltpu.roll(xi, SUBLANES - d, 0))
            xr, xi = _cmadd(xr, xi, crv_ref[3, :, pl.ds(0, H)], crv_ref[3, :, pl.ds(H, H)], gr, gi)
            g_scr[pl.ds(r0, SUBLANES), pl.ds(0, H)] = xr
            g_scr[pl.ds(r0, SUBLANES), pl.ds(H, H)] = xi
            rp = pl.multiple_of(jnp.maximum(k - 1, 0) * SUBLANES, SUBLANES)
            first = k == 0
            pr = jnp.where(first, ss_ref[:, pl.ds(0, H)], _row(s_ref[pl.ds(rp, SUBLANES), pl.ds(0, H)], SUBLANES - 1))
            pi_ = jnp.where(first, ss_ref[:, pl.ds(H, H)], _row(s_ref[pl.ds(rp, SUBLANES), pl.ds(H, H)], SUBLANES - 1))
            spr = jnp.where(rows == 0, pr, pltpu.roll(s_ref[pl.ds(r0, SUBLANES), pl.ds(0, H)], 1, 0))
            spi = jnp.where(rows == 0, pi_, pltpu.roll(s_ref[pl.ds(r0, SUBLANES), pl.ds(H, H)], 1, 0))
            return (_row(xr, 0), _row(xi, 0), acc_r + xr * spr + xi * spi, acc_i + xi * spr - xr * spi)

        zero = jnp.zeros((SUBLANES, H), F32)
        gr, gi, acc_r, acc_i = lax.fori_loop(
            0, nblk, rev, (gcar_scr[:, pl.ds(0, H)], gcar_scr[:, pl.ds(H, H)], zero, zero))
        gcar_scr[:, pl.ds(0, H)] = gr
        gcar_scr[:, pl.ds(H, H)] = gi
        da_ref[:, pl.ds(0, H)] += _colsum(acc_r)
        da_ref[:, pl.ds(H, H)] += _colsum(acc_i)
        gb = g_scr[...].astype(BF16)
        db_ref[...] += lax.dot_general(ubf, gb, _TN_DIMS, preferred_element_type=F32)
        dc_ref[...] += lax.dot_general(s_ref[...].astype(BF16), dypb, _TN_DIMS, preferred_element_type=F32)
        du_ref[...] = (jnp.dot(gb, bt_ref[...], preferred_element_type=F32) + d_ref[...] * dyp).astype(BF16)

    rmap = lambda j, i: (nch - 1 - i, j)
    return pl.pallas_call(
        body, name=name, grid=(nsb, nch),
        in_specs=[pl.BlockSpec((Tc, GW), rmap), pl.BlockSpec((Tc, GW), rmap), pl.BlockSpec((Tc, GW), rmap),
                  pl.BlockSpec((Tc, H2), rmap),
                  pl.BlockSpec((1, GW), lambda j, i: (0, j)),
                  pl.BlockSpec((None, H2, GW), lambda j, i: (j, 0, 0)),
                  pl.BlockSpec((None, GW, H2), lambda j, i: (j, 0, 0)),
                  pl.BlockSpec((None, 4, SUBLANES, H2), lambda j, i: (j, 0, 0, 0)),
                  pl.BlockSpec((None, None, 1, H2), lambda j, i: (nch - 1 - i, j, 0, 0))],
        out_specs=[pl.BlockSpec((Tc, GW), rmap),
                   pl.BlockSpec((None, GW, H2), lambda j, i: (j, 0, 0)),
                   pl.BlockSpec((None, H2, GW), lambda j, i: (j, 0, 0)),
                   pl.BlockSpec((None, 1, H2), lambda j, i: (j, 0, 0)),
                   pl.BlockSpec((1, GW), lambda j, i: (0, j))],
        out_shape=[_sds((L, W), BF16), _sds((nsb, GW, H2), F32), _sds((nsb, H2, GW), F32),
                   _sds((nsb, 1, H2), F32), _sds((1, W), F32)],
        scratch_shapes=[pltpu.VMEM((Tc, H2), F32), pltpu.VMEM((1, H2), F32)],
        compiler_params=_cp(("arbitrary", "arbitrary")),
    )(u, dyg, ypre, s_all, d_skip, bsb_t, csb_t, crv, ss)


def _lru_blockdiag(w_rg, w_ig):
    nb, bs, _ = w_rg.shape
    sbw = bs * LANES // math.gcd(bs, LANES)
    bps = sbw // bs
    nsb = nb // bps

    def bd(w):
        return _block_diag(w.astype(BF16).reshape(nsb, bps, bs, bs))

    return bd(w_rg), bd(w_ig)


def _lru_blockdiag_grad(dwsb, nb, bs):
    return _diag_blocks(dwsb, dwsb.shape[1] // bs).reshape(nb, bs, bs)


def lru_conv_fwd(p, conv_w, conv_b, name):
    L = p.shape[0]
    E = conv_w.shape[1]
    tc = _tile(E, (256, 128))
    noff = E // tc
    kw = conv_w.shape[0]

    def body(xb_ref, w_ref, b_ref, xc_ref, xcb_ref):
        xb = xb_ref[...]
        rows = lax.broadcasted_iota(jnp.int32, xb.shape, 0)
        acc = w_ref[pl.ds(kw - 1, 1), :] * xb + b_ref[...]
        for k in range(kw - 1):
            sh = kw - 1 - k
            acc = acc + w_ref[pl.ds(k, 1), :] * jnp.where(rows >= sh, pltpu.roll(xb, sh, 0), 0.0)
        xc_ref[...] = acc
        xcb_ref[...] = acc.astype(BF16)

    return pl.pallas_call(
        body, name=name, grid=(noff,),
        in_specs=[pl.BlockSpec((L, tc), lambda t: (0, noff + t)),
                  pl.BlockSpec((kw, tc), lambda t: (0, t)), pl.BlockSpec((1, tc), lambda t: (0, t))],
        out_specs=[pl.BlockSpec((L, tc), lambda t: (0, t))] * 2,
        out_shape=[_sds((L, E), F32), _sds((L, E), BF16)],
        compiler_params=_cp(("parallel",)),
    )(p, conv_w, conv_b)


def lru_conv_bwd(d1, d2, d3, p, conv_w, name):
    L = p.shape[0]
    E = conv_w.shape[1]
    tc = _tile(E, (256, 128))
    noff = E // tc
    kw = conv_w.shape[0]

    def body(d1_ref, d2_ref, d3_ref, xb_ref, w_ref, dxb_ref, dw_ref, db_ref):
        dxc = d1_ref[...] + d2_ref[...] + d3_ref[...]
        xb = xb_ref[...]
        rows = lax.broadcasted_iota(jnp.int32, xb.shape, 0)
        db_ref[...] = _colsum(dxc)
        acc = w_ref[pl.ds(kw - 1, 1), :] * dxc
        dw_ref[pl.ds(kw - 1, 1), :] = _colsum(dxc * xb)
        for k in range(kw - 1):
            sh = kw - 1 - k
            dw_ref[pl.ds(k, 1), :] = _colsum(dxc * jnp.where(rows >= sh, pltpu.roll(xb, sh, 0), 0.0))
            acc = acc + w_ref[pl.ds(k, 1), :] * jnp.where(rows < L - sh, pltpu.roll(dxc, L - sh, 0), 0.0)
        dxb_ref[...] = acc.astype(BF16)

    return pl.pallas_call(
        body, name=name, grid=(noff,),
        in_specs=[pl.BlockSpec((L, tc), lambda t: (0, t))] * 3 +
                 [pl.BlockSpec((L, tc), lambda t: (0, noff + t)), pl.BlockSpec((kw, tc), lambda t: (0, t))],
        out_specs=[pl.BlockSpec((L, tc), lambda t: (0, t)), pl.BlockSpec((kw, tc), lambda t: (0, t)),
                   pl.BlockSpec((1, tc), lambda t: (0, t))],
        out_shape=[_sds((L, E), BF16), _sds((kw, E), F32), _sds((1, E), F32)],
        compiler_params=_cp(("parallel",)),
    )(d1, d2, d3, p, conv_w)


def _lru_gates(pr, pi_, brg, big, sp):
    r = _sig(pr + brg)
    ig = _sig(pi_ + big)
    la = -LRU_C * r * sp
    a = jnp.exp(la)
    mult = jnp.sqrt(_neg_expm1(2.0 * la))
    return r, ig, a, mult


def _lru_specs(L, E):
    tc = _tile(E, (LRU_TILE, LANES))
    col = pl.BlockSpec((L, tc), lambda t: (0, t))
    vec = pl.BlockSpec((1, tc), lambda t: (0, t))
    return tc, col, vec


def lru_scan_fwd(pre_r, pre_i, xc, p, b_rg, b_ig, lam, name):
    L, E = xc.shape
    tc, col, vec = _lru_specs(L, E)
    nblk = L // SUBLANES

    def body(pr_ref, pi_ref, xc_ref, gb_ref, brg_ref, big_ref, lam_ref, hs_ref, yv_ref):
        sp = _softplus(-lam_ref[...])
        brg, big = brg_ref[...], big_ref[...]
        rows = lax.broadcasted_iota(jnp.int32, (SUBLANES, tc), 0)

        def blk(k, carry):
            r0 = pl.multiple_of(k * SUBLANES, SUBLANES)
            sl = pl.ds(r0, SUBLANES)
            _, ig, a, mult = _lru_gates(pr_ref[sl, :], pi_ref[sl, :], brg, big, sp)
            b = mult * ig * xc_ref[sl, :]
            for d in (1, 2, 4):
                keep = rows >= d
                b = b + a * jnp.where(keep, pltpu.roll(b, d, 0), 0.0)
                a = a * jnp.where(keep, pltpu.roll(a, d, 0), 1.0)
            h = b + a * carry
            hs_ref[sl, :] = h
            return _row(h, SUBLANES - 1)

        def trip(kt, carry):
            for q in range(SCAN_UNROLL):
                carry = blk(kt * SCAN_UNROLL + q, carry)
            return carry

        lax.fori_loop(0, nblk // SCAN_UNROLL, trip, jnp.zeros((1, tc), F32))
        yv_ref[...] = (hs_ref[...] * _gelu(gb_ref[...])).astype(BF16)

    return pl.pallas_call(
        body, name=name, grid=(E // tc,),
        in_specs=[col, col, col, col, vec, vec, vec],
        out_specs=[col, col], out_shape=[_sds((L, E), F32), _sds((L, E), BF16)],
        compiler_params=_cp(("parallel",)),
    )(pre_r, pre_i, xc, p, b_rg, b_ig, lam)


def lru_scan_bwd(dyv, hs, pre_r, pre_i, xc, p, b_rg, b_ig, lam, name):
    L, E = xc.shape
    tc, col, vec = _lru_specs(L, E)
    nblk = L // SUBLANES

    def body(dyv_ref, hs_ref, pr_ref, pi_ref, xc_ref, gb_ref, brg_ref, big_ref, lam_ref,
             dgb_ref, dpr_ref, dpi_ref, dxc_ref, dbrg_ref, dbig_ref, dlam_ref, t_gb, t_pr, t_pi):
        lam_v = lam_ref[...]
        sp = _softplus(-lam_v)
        brg, big = brg_ref[...], big_ref[...]
        rows = lax.broadcasted_iota(jnp.int32, (SUBLANES, tc), 0)

        def blk(kk, carry):
            gcar, a_next, acc_sp, acc_r, acc_i = carry
            k = nblk - 1 - kk
            r0 = pl.multiple_of(k * SUBLANES, SUBLANES)
            sl = pl.ds(r0, SUBLANES)
            r, ig, a, mult = _lru_gates(pr_ref[sl, :], pi_ref[sl, :], brg, big, sp)
            gbv, hsv, dyvv, xcv = gb_ref[sl, :], hs_ref[sl, :], dyv_ref[sl, :], xc_ref[sl, :]
            t_gb[sl, :] = dyvv * hsv * _gelu_grad(gbv)
            x = dyvv * _gelu(gbv)
            al = jnp.where(rows == SUBLANES - 1, a_next, pltpu.roll(a, SUBLANES - 1, 0))
            for d in (1, 2, 4):
                keep = rows < SUBLANES - d
                x = x + al * jnp.where(keep, pltpu.roll(x, SUBLANES - d, 0), 0.0)
                al = al * jnp.where(keep, pltpu.roll(al, SUBLANES - d, 0), 1.0)
            g = x + al * gcar
            rp = pl.multiple_of(jnp.maximum(k - 1, 0) * SUBLANES, SUBLANES)
            hlast = _row(hs_ref[pl.ds(rp, SUBLANES), :], SUBLANES - 1) * (k > 0).astype(F32)
            hprev = jnp.where(rows == 0, hlast, pltpu.roll(hsv, 1, 0))
            da = g * hprev
            dmult = g * ig * xcv
            dig = g * mult * xcv
            dxc_ref[sl, :] = g * mult * ig
            dla = da * a - dmult * (a * a) / mult
            dpr = dla * (-LRU_C * sp) * r * (1.0 - r)
            dpi = dig * ig * (1.0 - ig)
            t_pr[sl, :] = dpr
            t_pi[sl, :] = dpi
            return (_row(g, 0), _row(a, 0), acc_sp + dla * (-LRU_C * r), acc_r + dpr, acc_i + dpi)

        zero = jnp.zeros((SUBLANES, tc), F32)
        z1 = jnp.zeros((1, tc), F32)
        def trip(kt, carry):
            for q in range(SCAN_UNROLL):
                carry = blk(kt * SCAN_UNROLL + q, carry)
            return carry

        _, _, acc_sp, acc_r, acc_i = lax.fori_loop(0, nblk // SCAN_UNROLL, trip, (z1, z1, zero, zero, zero))
        dgb_ref[...] = t_gb[...].astype(BF16)
        dpr_ref[...] = t_pr[...].astype(BF16)
        dpi_ref[...] = t_pi[...].astype(BF16)
        dbrg_ref[...] = _colsum(acc_r)
        dbig_ref[...] = _colsum(acc_i)
        dlam_ref[...] = -_colsum(acc_sp) * _sig(-lam_v)

    return pl.pallas_call(
        body, name=name, grid=(E // tc,),
        in_specs=[col, col, col, col, col, col, vec, vec, vec],
        out_specs=[col, col, col, col, vec, vec, vec],
        out_shape=[_sds((L, E), BF16), _sds((L, E), BF16), _sds((L, E), BF16), _sds((L, E), F32),
                   _sds((1, E), F32), _sds((1, E), F32), _sds((1, E), F32)],
        scratch_shapes=[pltpu.VMEM((L, tc), F32)] * 3,
        compiler_params=_cp(("parallel",)),
    )(dyv, hs, pre_r, pre_i, xc, p, b_rg, b_ig, lam)


def _place():
    xi, yi, ci = lax.axis_index("x"), lax.axis_index("y"), lax.axis_index("c")
    chips = [(1 - xi, yi), (xi, 1 - yi), (1 - xi, 1 - yi)]
    return xi, yi, ci, chips


_ANY = pl.BlockSpec(memory_space=pl.ANY)


def all_gather_devices(blks, name):
    n = len(blks)

    def body(*refs):
        ins, outs = refs[:n], refs[n:2 * n]
        send_sems, recv_sems, local_sems = refs[2 * n:]
        xi, yi, ci, chips = _place()
        me, sibling = (xi, yi, ci), (xi, yi, 1 - ci)

        def slab(a, px, py, pc):
            return outs[a].at[4 * px + 2 * py + pc]

        def copy(a, k, block, to, src=None):
            return pltpu.make_async_remote_copy(
                src_ref=slab(a, *block) if src is None else src, dst_ref=slab(a, *block),
                send_sem=send_sems.at[7 * a + k], recv_sem=recv_sems.at[7 * a + k], device_id=to,
                device_id_type=MESH)

        mine = [pltpu.make_async_copy(ins[a], slab(a, *me), local_sems.at[a]) for a in range(n)]
        first, passed = [], []
        for a in range(n):
            mine[a].start()
            first.append(copy(a, 0, me, sibling, src=ins[a]))
            first += [copy(a, 1 + j, me, (*chip, ci), src=ins[a]) for j, chip in enumerate(chips)]
        for cp in first:
            cp.start()
        for a in range(n):
            for j, chip in enumerate(chips):
                copy(a, 1 + j, (*chip, ci), me).wait_recv()
                passed.append(copy(a, 4 + j, (*chip, ci), sibling))
                passed[-1].start()
        for a in range(n):
            copy(a, 0, sibling, me).wait_recv()
            for j, chip in enumerate(chips):
                copy(a, 4 + j, (*chip, 1 - ci), me).wait_recv()
        for cp in first + passed:
            cp.wait_send()
        for cp in mine:
            cp.wait()

    return pl.pallas_call(
        body, name=name, in_specs=[_ANY] * n, out_specs=[_ANY] * n,
        out_shape=[_sds((N_DEV,) + b.shape, b.dtype) for b in blks],
        scratch_shapes=[pltpu.SemaphoreType.DMA((7 * n,)), pltpu.SemaphoreType.DMA((7 * n,)),
                        pltpu.SemaphoreType.DMA((n,))],
    )(*blks)


_HBM = pl.BlockSpec(memory_space=pltpu.HBM)
_SEM = pl.BlockSpec(memory_space=pltpu.SEMAPHORE)
_EFFECT = pltpu.SideEffectType.DATAFLOW_SIDE_EFFECTING


def split_start(name, groups, counts, copies_fn):
    flat = [b for g in groups for b in g]
    n, ng = len(flat), len(groups)

    def body(*refs):
        ins, sems, token = refs[:n], refs[n:n + 2 * ng], refs[-1]
        off = 0
        for gi, g in enumerate(groups):
            for cp in copies_fn(ins[off:off + len(g)], [b.shape for b in g], sems[2 * gi], sems[2 * gi + 1]):
                cp.start()
            off += len(g)
        token[...] = jnp.zeros_like(token)

    out_shape = tuple(pltpu.SemaphoreType.DMA((c,)) for c in counts for _ in range(2))
    out_shape += tuple(pltpu.HBM(b.shape, b.dtype) for b in flat) + (_sds((SUBLANES, LANES), F32),)
    outs = pl.pallas_call(
        body, name=name, in_specs=[_HBM] * n, out_shape=out_shape,
        out_specs=tuple([_SEM] * (2 * ng) + [_HBM] * n + [pl.BlockSpec(memory_space=pltpu.VMEM)]),
        input_output_aliases={i: 2 * ng + i for i in range(n)},
        compiler_params=pltpu.CompilerParams(has_side_effects=_EFFECT),
    )(*[pltpu.with_memory_space_constraint(b, pltpu.HBM) for b in flat])
    sems = [(outs[2 * gi], outs[2 * gi + 1]) for gi in range(ng)]
    thru, off = [], 2 * ng
    for g in groups:
        thru.append(list(outs[off:off + len(g)]))
        off += len(g)
    return sems, thru, outs[-1]


def split_wait(name, bufs, sems, copies_fn, after):
    n = len(bufs)
    shapes = [b.shape for b in bufs]

    def body(*refs):
        for cp in copies_fn(refs[:n], shapes, refs[n], refs[n + 1]):
            cp.wait_send()
            cp.wait_recv()

    return list(pl.pallas_call(
        body, name=name, in_specs=[_HBM] * n + [_SEM, _SEM, _ANY],
        out_shape=tuple(pltpu.HBM(b.shape, b.dtype) for b in bufs), out_specs=tuple([_HBM] * n),
        input_output_aliases={i: i for i in range(n)},
        compiler_params=pltpu.CompilerParams(has_side_effects=_EFFECT),
    )(*bufs, sems[0], sems[1], after))


def _gather_copies(bufs, shapes, send_sems, recv_sems):
    xi, yi, ci, chips = _place()
    cps = []
    for a, ref in enumerate(bufs):
        hr = shapes[a][1] // 2
        rows = ref.at[2 * xi + yi, pl.ds(pl.multiple_of(ci * hr, 16), hr), :]
        for j in range(3):
            cps.append(pltpu.make_async_remote_copy(
                src_ref=rows, dst_ref=rows, send_sem=send_sems.at[3 * a + j], recv_sem=recv_sems.at[3 * a + j],
                device_id=(*chips[j], ci), device_id_type=MESH))
    return cps


def _scatter_copies(bufs, shapes, send_sems, recv_sems):
    xi, yi, ci, chips = _place()
    n = len(bufs) // 2
    cps = []
    for a in range(n):
        for j in range(3):
            cps.append(pltpu.make_async_remote_copy(
                src_ref=bufs[a].at[2 * chips[j][0] + chips[j][1]], dst_ref=bufs[n + a].at[j],
                send_sem=send_sems.at[3 * a + j], recv_sem=recv_sems.at[3 * a + j],
                device_id=(*chips[j], ci), device_id_type=MESH))
    return cps


def cast_place(w, layer, chip1, name, after=None):
    _, R, C = w.shape
    tr = _tile(R, tuple(t for t in _TR if t * C <= ROW_TILE_ELEMS) or (16,))
    extra = [] if after is None else [after]

    def body(c_ref, w_ref, *rest):
        rest[-1][...] = w_ref[...].astype(BF16)

    return pl.pallas_call(
        body, name=name,
        grid_spec=pltpu.PrefetchScalarGridSpec(
            num_scalar_prefetch=1, grid=(R // tr,),
            in_specs=[pl.BlockSpec((None, tr, C), lambda i, c: (layer, i, 0))] + [_ANY] * len(extra),
            out_specs=pl.BlockSpec((None, tr, C), lambda i, c: (c[0], i, 0))),
        out_shape=_sds((N_CHIPS, R, C), BF16), compiler_params=_cp(("parallel",)),
    )(chip1, w, *extra)


def _forward_copies(bufs, shapes, send_sems, recv_sems):
    xi, yi, ci, chips = _place()
    cps = []
    for a, ref in enumerate(bufs):
        hr = shapes[a][1] // 2
        for j in range(3):
            rows = ref.at[2 * chips[j][0] + chips[j][1], pl.ds(pl.multiple_of(ci * hr, 16), hr), :]
            cps.append(pltpu.make_async_remote_copy(
                src_ref=rows, dst_ref=rows, send_sem=send_sems.at[3 * a + j], recv_sem=recv_sems.at[3 * a + j],
                device_id=(xi, yi, 1 - ci), device_id_type=MESH))
    return cps


def _join_copies(bufs, shapes, send_sems, recv_sems):
    xi, yi, ci, _ = _place()
    cps = []
    for a, ref in enumerate(bufs):
        hr = shapes[a][0] // 2
        rows = ref.at[pl.ds(pl.multiple_of(ci * hr, 8), hr), :]
        cps.append(pltpu.make_async_remote_copy(
            src_ref=rows, dst_ref=rows, send_sem=send_sems.at[a], recv_sem=recv_sems.at[a],
            device_id=(xi, yi, 1 - ci), device_id_type=MESH))
    return cps


def _halves_copies(bufs, shapes, send_sems, recv_sems):
    xi, yi, ci, _ = _place()
    n = len(bufs) // 2
    cps = []
    for a in range(n):
        hr = shapes[a][1] // 2
        cps.append(pltpu.make_async_remote_copy(
            src_ref=bufs[a].at[:, pl.ds(pl.multiple_of((1 - ci) * hr, 16), hr), :], dst_ref=bufs[n + a],
            send_sem=send_sems.at[a], recv_sem=recv_sems.at[a], device_id=(xi, yi, 1 - ci), device_id_type=MESH))
    return cps


def add_half(g, got, ci, name):
    S, hr, C = got.shape
    tr = _tile(hr, tuple(t for t in _TR if t * C <= ROW_TILE_ELEMS) or (16,))
    nb = hr // tr

    def body(c_ref, g_ref, r_ref, o_ref):
        o_ref[...] = (g_ref[...].astype(F32) + r_ref[...].astype(F32)).astype(BF16)

    return pl.pallas_call(
        body, name=name,
        grid_spec=pltpu.PrefetchScalarGridSpec(
            num_scalar_prefetch=1, grid=(S, nb),
            in_specs=[pl.BlockSpec((None, tr, C), lambda s, i, c: (s, c[0] * nb + i, 0)),
                      pl.BlockSpec((None, tr, C), lambda s, i, c: (s, i, 0))],
            out_specs=pl.BlockSpec((None, tr, C), lambda s, i, c: (s, i, 0))),
        out_shape=_sds((S, hr, C), BF16), compiler_params=_cp(("parallel", "parallel")),
    )(ci, g, got)


def add_chips(part, got, place, name):
    S, hr, C = part.shape
    tr = _tile(hr, tuple(t for t in _TR if t * C <= ROW_TILE_ELEMS) or (16,))
    nb = hr // tr

    def body(c_ref, p_ref, r_ref, o_ref):
        acc = p_ref[...].astype(F32)
        for j in range(3):
            acc = acc + r_ref[j].astype(F32)
        o_ref[...] = acc

    return pl.pallas_call(
        body, name=name,
        grid_spec=pltpu.PrefetchScalarGridSpec(
            num_scalar_prefetch=1, grid=(nb,),
            in_specs=[pl.BlockSpec((None, tr, C), lambda i, c: (c[0], i, 0)),
                      pl.BlockSpec((3, tr, C), lambda i, c: (0, i, 0))],
            out_specs=pl.BlockSpec((tr, C), lambda i, c: (c[1] * nb + i, 0))),
        out_shape=_sds((2 * hr, C), F32), compiler_params=_cp(("parallel",)),
    )(place, part, got)


def join_halves(bufs, name):
    n = len(bufs)

    def body(*refs):
        ins, outs = refs[:n], refs[n:2 * n]
        send_sems, recv_sems = refs[2 * n:]
        xi, yi, ci, _ = _place()

        def copy(ref, a, h):
            hr = bufs[a].shape[0] // 2
            rows = pl.ds(pl.multiple_of(h * hr, 8), hr)
            return pltpu.make_async_remote_copy(
                src_ref=ref[a].at[rows, :], dst_ref=outs[a].at[rows, :], send_sem=send_sems.at[a],
                recv_sem=recv_sems.at[a], device_id=(xi, yi, 1 - ci), device_id_type=MESH)

        sends = [copy(ins, a, ci) for a in range(n)]
        for cp in sends:
            cp.start()
        for a in range(n):
            copy(outs, a, 1 - ci).wait_recv()
        for cp in sends:
            cp.wait_send()

    return pl.pallas_call(
        body, name=name, in_specs=[_ANY] * n, out_specs=[_ANY] * n,
        out_shape=[_sds(b.shape, b.dtype) for b in bufs], input_output_aliases={i: i for i in range(n)},
        scratch_shapes=[pltpu.SemaphoreType.DMA((n,)), pltpu.SemaphoreType.DMA((n,))],
    )(*bufs)


def reduce_scatter_start(grads, tag):
    lands = [lax.empty((N_CHIPS, g.shape[1] // 2, g.shape[2]), g.dtype) for g in grads]
    sems, thru, token = split_start(f"rs_d2d_start_{tag}", [list(grads) + lands], [len(grads)], _halves_copies)
    return (sems[0], thru[0]), token


def reduce_scatter_relay(state, ci1, after, tag):
    sems, bufs = state
    bufs = split_wait(f"rs_d2d_wait_{tag}", bufs, sems, _halves_copies, after)
    n = len(bufs) // 2
    parts = [add_half(bufs[a], bufs[n + a], ci1, f"rs_add_half_{tag}{a}") for a in range(n)]
    lands = [lax.empty((3,) + q.shape[1:], q.dtype) for q in parts]
    sems, thru, token = split_start(f"rs_ici_start_{tag}", [parts + lands], [3 * n], _scatter_copies)
    return (sems[0], thru[0]), token


def reduce_scatter_finish(state, place, after, tag):
    sems, bufs = state
    bufs = split_wait(f"rs_ici_wait_{tag}", bufs, sems, _scatter_copies, after)
    n = len(bufs) // 2
    return [add_chips(bufs[a], bufs[n + a], place, f"rs_add_chips_{tag}{a}") for a in range(n)]


def _pack(parts, width):
    flat = jnp.concatenate([p.reshape(-1).astype(F32) for p in parts])
    per = SUBLANES * width
    total = -(-flat.shape[0] // per) * per
    flat = jnp.pad(flat, (0, total - flat.shape[0]))
    return flat.reshape(total // width, width)


def _unpack(flat, shapes):
    out, off = [], 0
    for s in shapes:
        n = math.prod(s)
        out.append(flat[off:off + n].reshape(s))
        off += n
    return out


_W_NAMES = ['norm_g', 'w_ada', 'b_ada', 's5_w_in', 's5_lam_re', 's5_lam_im', 's5_log_dt', 's5_b_re', 's5_b_im',
            's5_c_re', 's5_c_im', 's5_d', 's5_w_glu', 'lru_w_in', 'lru_conv_w', 'lru_conv_b', 'lru_w_rg',
            'lru_b_rg', 'lru_w_ig', 'lru_b_ig', 'lru_lam', 'lru_w_out', 'ffn_w_gu', 'ffn_w_down', 'final_g']
_BIG = ('w_ada', 's5_w_in', 's5_w_glu', 'lru_w_in', 'lru_w_out', 'ffn_w_gu', 'ffn_w_down')
_MID = ('s5_b_re', 's5_b_im', 's5_c_re', 's5_c_im', 'lru_w_rg', 'lru_w_ig')


def _ffn_fwd(x, gu, w_down, gate, tag):
    act = swiglu_fwd(gu, f"{tag}_act")
    z = mm_nn(act, w_down, name=f"{tag}_down")
    return res_gate_fwd(x, z, gate, f"{tag}_res"), (gu, act, z)


def _ffn_bwd(dx, h, saved, w_gu, w_down, gate, tag, relay=None):
    gu, act, z = saved
    dz, dgate = res_gate_bwd(dx, z, gate, f"{tag}_res_bwd")
    dact = mm_nt(dz, w_down, name=f"{tag}_dact")
    tok = None if relay is None else relay(dact)
    dw_down = mm_tn(act, dz, name=f"{tag}_dwdown", out_dtype=BF16)
    dgu = swiglu_bwd(dact, gu, f"{tag}_act_bwd", after=tok)
    dw_gu = mm_tn(h, dgu, name=f"{tag}_dwgu", out_dtype=BF16, omode="cols", groups=N_CHIPS)
    started, tok = reduce_scatter_start([dw_gu, dw_down.reshape((N_CHIPS, -1) + dw_down.shape[1:])], tag)
    dh = mm_nt(dgu, w_gu, name=f"{tag}_dh", bmode="cols", after=tok)
    return dh, dgate, started


def _step(p):
    xi, yi, ci = lax.axis_index("x"), lax.axis_index("y"), lax.axis_index("c")
    chip = 2 * xi + yi
    me = 2 * chip + ci
    ci1 = jnp.reshape(ci, (1,)).astype(jnp.int32)
    chip1 = jnp.reshape(chip, (1,)).astype(jnp.int32)
    place2 = jnp.stack([chip, ci]).astype(jnp.int32)

    x0 = p['x'][0]
    tgt = p['loss_target'][0]
    L, D = x0.shape
    Dq = D // N_CHIPS
    depth = p['w_ada'].shape[0]
    E = p['lru_lam'].shape[1] * N_CHIPS
    Eq = E // N_CHIPS
    kw = p['lru_conv_w'].shape[1]
    Nq = p['w_ada'].shape[2]
    _, G, P, Cg = p['s5_b_re'].shape
    nb, bs = p['lru_w_rg'].shape[1], p['lru_w_rg'].shape[2]

    pay = _pack([p['c'], p['norm_g'], p['lru_conv_w'], p['lru_conv_b'], p['lru_b_rg'], p['lru_b_ig'],
                 p['lru_lam']], 1024)
    g1 = all_gather_devices([pay], "gather_small_params")[0].reshape(N_DEV, -1)
    c_all = g1[:, :D]
    per_chip = g1[0::2]
    sizes = [(depth, 2, Dq), (kw, Eq), (Eq,), (Eq,), (Eq,), (Eq,)]
    offs = D
    pieces = []
    for s in sizes:
        nel = math.prod(s)
        pieces.append(per_chip[:, offs:offs + nel].reshape((N_CHIPS,) + s))
        offs += nel
    norm_g = jnp.moveaxis(pieces[0], 0, 2).reshape(depth, 2, D)
    conv_w = jnp.moveaxis(pieces[1], 0, 1).reshape(kw, E)
    conv_b, b_rg, b_ig, lam = [q.reshape(1, E) for q in pieces[2:]]

    cond = silu_rows(jnp.pad(c_all, ((0, 16 - N_DEV), (0, 0))), "cond_silu")
    cond_rep = jnp.concatenate([cond] * depth, axis=1)
    mod_part = mm_nn(cond_rep, p['w_ada'], name="mod_proj", bmode="batch")[:N_DEV]
    g2 = all_gather_devices([mod_part], "gather_mod")[0][0::2]
    mine = lax.dynamic_index_in_dim(g2, me, axis=1, keepdims=False).reshape(N_CHIPS, depth, Nq)
    mod = jnp.moveaxis(mine, 0, 1).reshape(depth, N_CHIPS * Nq) + p['b_ada']
    mods = [[mod[i:i + 1, k * D:(k + 1) * D] for k in range(N_MOD)] for i in range(depth)]

    def place(key, layer, tag, after=None):
        return cast_place(p[key], layer, chip1, f"place_{tag}", after=after)

    first = [place('s5_w_in', 0, "s5_in", after=mod), place('s5_w_glu', 0, "s5_glu")]
    sems_a, bufs_a, tok_a = split_start("gather_ici_start_s5", [[w] for w in first], [3] * len(first), _gather_copies)
    others = [[place('ffn_w_gu', 0, "gu0", after=tok_a), place('ffn_w_down', 0, "down0")],
              [place('lru_w_in', 0, "lru_in"), place('lru_w_out', 0, "lru_out")],
              [place('ffn_w_gu', 1, "gu1"), place('ffn_w_down', 1, "down1")]]
    sems_b, bufs_b, tok_b = split_start("gather_ici_start_rest", others, [3 * len(g) for g in others], _gather_copies)
    wsems, wbufs = sems_a + sems_b, bufs_a + bufs_b

    def landed(gi, after, tag):
        bufs = split_wait(f"gather_ici_wait_{tag}", wbufs[gi], wsems[gi], _gather_copies, after)
        sems, thru, tok = split_start(f"gather_d2d_start_{tag}", [bufs], [3 * len(bufs)], _forward_copies)
        return (sems[0], thru[0]), tok

    def weights(state, after, tag):
        return split_wait(f"gather_d2d_wait_{tag}", state[1], state[0], _forward_copies, after)

    s5_small = (p['s5_lam_re'][0], p['s5_lam_im'][0], p['s5_log_dt'][0], p['s5_b_re'][0], p['s5_b_im'][0])
    (ab_re, ab_im, bb_re, bb_im), s5_disc_vjp = jax.vjp(_s5_discretize, *s5_small)
    bsb, csb = _s5_blockdiag(bb_re, bb_im, p['s5_c_re'][0], p['s5_c_im'][0])
    cf, crv = _s5_scan_consts(ab_re, ab_im)
    wsb_rg, wsb_ig = [w.astype(BF16) for w in _lru_blockdiag(p['lru_w_rg'][0], p['lru_w_ig'][0])]
    nsb_lru = wsb_rg.shape[0]

    sh1, sc1, gt1, sh2, sc2, gt2 = mods[0]
    st, tok = landed(0, tok_b, "s5_in")
    s5_w_in = weights(st, tok, "s5_in")[0].reshape(-1, D)
    h0 = norm_mod_fwd(x0, norm_g[0, 0:1], sc1, sh1, "l0_norm1")
    u = mm_nn(h0, s5_w_in, name="s5_in")
    st, tok = landed(1, u, "s5_glu")
    ypre, yg, s_all, ss = s5_scan_fwd(u, p['s5_d'], bsb, csb, cf, "s5_scan")
    s5_w_glu = weights(st, yg, "s5_glu")[0]
    v = mm_nn(yg, s5_w_glu, name="s5_glu", bmode="cols")
    st, tok = landed(2, v, "ffn0")
    x1 = glu_res_fwd(x0, v, gt1, "s5_res")
    h1 = norm_mod_fwd(x1, norm_g[0, 1:2], sc2, sh2, "l0_norm2", after=tok)
    w_gu0, w_down0 = weights(st, h1, "ffn0")
    w_down0 = w_down0.reshape(-1, D)
    gu0 = mm_nn(h1, w_gu0, name="ffn0_gu", out_dtype=BF16, bmode="cols")
    st, tok = landed(3, gu0, "lru")
    x2, ffn0 = _ffn_fwd(x1, gu0, w_down0, gt2, "ffn0")

    sh1b, sc1b, gt1b, sh2b, sc2b, gt2b = mods[1]
    h2 = norm_mod_fwd(x2, norm_g[1, 0:1], sc1b, sh1b, "l1_norm1", after=tok)
    lru_w_in, lru_w_out = weights(st, h2, "lru")
    lru_w_out = lru_w_out.reshape(-1, D)
    pq = mm_nn(h2, lru_w_in, name="lru_in", bmode="cols")
    xc, xcb = lru_conv_fwd(pq, conv_w, conv_b, "lru_conv")
    pre_r = mm_nn(xcb, wsb_rg, name="lru_gate_r", bmode="batch")
    pre_i = mm_nn(xcb, wsb_ig, name="lru_gate_i", bmode="batch")
    hs, yv = lru_scan_fwd(pre_r, pre_i, xc, pq, b_rg, b_ig, lam, "lru_scan")
    st, tok = landed(4, hs, "ffn1")
    mix = mm_nn(yv, lru_w_out, name="lru_out")
    x3 = res_gate_fwd(x2, mix, gt1b, "lru_res")
    h3 = norm_mod_fwd(x3, norm_g[1, 1:2], sc2b, sh2b, "l1_norm2", after=tok)
    w_gu1, w_down1 = weights(st, h3, "ffn1")
    w_down1 = w_down1.reshape(-1, D)
    gu1 = mm_nn(h3, w_gu1, name="ffn1_gu", out_dtype=BF16, bmode="cols")
    x4, ffn1 = _ffn_fwd(x3, gu1, w_down1, gt2b, "ffn1")

    fg = p['final_g'].reshape(1, D)
    dx4, loss_blk, dfinal_g = final_loss(x4, fg, tgt, "final_loss")
    loss = lax.psum(loss_blk[0, 0], ("x", "y", "c"))

    def rows4(g):
        return g.reshape((N_CHIPS, -1) + g.shape[1:])

    dh3, dgt2b, d2d_ffn1 = _ffn_bwd(dx4, h3, ffn1, w_gu1, w_down1, gt2b, "ffn1")
    rs_ffn1, tok = reduce_scatter_relay(d2d_ffn1, ci1, dh3, "ffn1")
    dx3, dgn11, dsc2b, dsh2b = norm_mod_bwd(dh3, x3, norm_g[1, 1:2], sc2b, dx4, "l1_norm2_bwd", after=tok)

    dmix, dgt1b = res_gate_bwd(dx3, mix, gt1b, "lru_res_bwd")
    dyv = mm_nt(dmix, lru_w_out, name="lru_dyv")
    dw_out = mm_tn(yv, dmix, name="lru_dwout", out_dtype=BF16)
    dgb, dpre_r, dpre_i, dxc1, db_rg, db_ig, dlam = lru_scan_bwd(dyv, hs, pre_r, pre_i, xc, pq, b_rg, b_ig, lam,
                                                                "lru_scan_bwd")
    dxc2 = mm_nt(dpre_r, wsb_rg, name="lru_dxc_r", bmode="batch")
    dxc3 = mm_nt(dpre_i, wsb_ig, name="lru_dxc_i", bmode="batch")
    dwsb_rg = mm_tn(xcb, dpre_r, name="lru_dwgate_r", omode="batch", groups=nsb_lru)
    dwsb_ig = mm_tn(xcb, dpre_i, name="lru_dwgate_i", omode="batch", groups=nsb_lru)
    dxb, dconv_w, dconv_b = lru_conv_bwd(dxc1, dxc2, dxc3, pq, conv_w, "lru_conv_bwd")
    dpq = jnp.concatenate([dgb, dxb], axis=1)
    dh2 = mm_nt(dpq, lru_w_in, name="lru_dh", bmode="cols")
    dw_lru_in = mm_tn(h2, dpq, name="lru_dwin", out_dtype=BF16, omode="cols", groups=N_CHIPS)
    d2d_lru, tok = reduce_scatter_start([dw_lru_in, rows4(dw_out)], "lru")
    dx2, dgn10, dsc1b, dsh1b = norm_mod_bwd(dh2, x2, norm_g[1, 0:1], sc1b, dx3, "l1_norm1_bwd", after=tok)

    lru_relay = {}

    def relay_lru(after):
        lru_relay['state'], tok = reduce_scatter_relay(d2d_lru, ci1, after, "lru")
        return tok

    dh1, dgt2, d2d_ffn0 = _ffn_bwd(dx2, h1, ffn0, w_gu0, w_down0, gt2, "ffn0", relay=relay_lru)
    rs_lru = lru_relay['state']
    rs_ffn0, tok = reduce_scatter_relay(d2d_ffn0, ci1, dh1, "ffn0")
    dx1, dgn01, dsc2, dsh2 = norm_mod_bwd(dh1, x1, norm_g[0, 1:2], sc2, dx2, "l0_norm2_bwd", after=tok)

    dv, dgt1 = glu_res_bwd(dx1, v, gt1, "s5_res_bwd")
    dyg = mm_nt(dv, s5_w_glu, name="s5_dyg", bmode="cols")
    dw_glu = mm_tn(yg, dv, name="s5_dwglu", out_dtype=BF16, omode="cols", groups=N_CHIPS)
    du, dbsb, dcsb, da, dd = s5_scan_bwd(u, dyg, ypre, s_all, p['s5_d'], bsb, csb, crv, ss, "s5_scan_bwd")
    dh0 = mm_nt(du, s5_w_in, name="s5_dh")
    dw_s5_in = mm_tn(h0, du, name="s5_dwin", out_dtype=BF16)
    d2d_s5, tok = reduce_scatter_start([rows4(dw_s5_in), dw_glu], "s5")
    grad_x, dgn00, dsc1, dsh1 = norm_mod_bwd(dh0, x0, norm_g[0, 0:1], sc1, dx1, "l0_norm1_bwd", after=tok)

    dmod = jnp.concatenate([jnp.concatenate([dsh1, dsc1, dgt1, dsh2, dsc2, dgt2], axis=1),
                            jnp.concatenate([dsh1b, dsc1b, dgt1b, dsh2b, dsc2b, dgt2b], axis=1)], axis=0)
    dnorm_g = jnp.stack([jnp.concatenate([dgn00, dgn01]), jnp.concatenate([dgn10, dgn11])])
    dbb_re, dbb_im, dc_re, dc_im = _s5_blockdiag_grads(dbsb, dcsb, P, Cg)
    H = S5_SB_GROUPS * P
    da_re, da_im = da[:, 0, :H].reshape(G, P), da[:, 0, H:].reshape(G, P)
    dw_rg, dw_ig = _lru_blockdiag_grad(dwsb_rg, nb, bs), _lru_blockdiag_grad(dwsb_ig, nb, bs)
    small = [dmod, dnorm_g, da_re, da_im, dd, dconv_w, dconv_b, db_rg, db_ig, dlam, dfinal_g]
    small_shapes = [s.shape for s in small]
    payload = _pack(small, 1024)
    mid = [dbb_re, dbb_im, dc_re, dc_im, dw_rg, dw_ig]
    mid_shapes = [s.shape for s in mid]
    gathered = all_gather_devices([payload] + [s.reshape(s.shape[0], -1).astype(BF16) for s in mid],
                                  "gather_small_grads")
    gathered_small = gathered[0]
    rs_s5, tok_s5 = reduce_scatter_relay(d2d_s5, ci1, gathered_small, "s5")
    total = sum_devices(gathered_small, "sum_small_grads").reshape(-1)
    (s_dmod, s_norm_g, s_da_re, s_da_im, s_dd, s_conv_w, s_conv_b, s_b_rg, s_b_ig, s_lam,
     s_final_g) = _unpack(total, small_shapes)
    s_dbb_re, s_dbb_im, s_dc_re, s_dc_im, s_dw_rg, s_dw_ig = [
        sum_devices(g, f"sum_mid_grads_{i}").reshape(s) for i, (g, s) in enumerate(zip(gathered[1:], mid_shapes))]
    g_lam_re, g_lam_im, g_log_dt, g_b_re, g_b_im = s5_disc_vjp((s_da_re, s_da_im, s_dbb_re, s_dbb_im))

    npay = payload.shape[0] * payload.shape[1]
    dmod_all = gathered_small.reshape(N_DEV, npay)[:, :depth * N_MOD * D].reshape(N_DEV, depth, N_CHIPS, Nq)
    dmod_mine = lax.dynamic_index_in_dim(dmod_all, chip, axis=2, keepdims=False).reshape(N_DEV, depth * Nq)
    dmod_mine = jnp.pad(dmod_mine, ((0, 16 - N_DEV), (0, 0)))
    g_w_ada = mm_tn(cond_rep, dmod_mine, name="w_ada_grad", omode="batch", groups=depth)

    def cols(full, width):
        return lax.dynamic_slice_in_dim(full, chip * width, width, axis=full.ndim - 1)

    grads = {
        'norm_g': cols(s_norm_g, Dq), 'w_ada': g_w_ada, 'b_ada': s_dmod,
        's5_lam_re': g_lam_re, 's5_lam_im': g_lam_im, 's5_log_dt': g_log_dt, 's5_b_re': g_b_re, 's5_b_im': g_b_im,
        's5_c_re': s_dc_re, 's5_c_im': s_dc_im, 's5_d': s_dd, 'lru_conv_w': cols(s_conv_w, Eq),
        'lru_conv_b': cols(s_conv_b, Eq), 'lru_w_rg': s_dw_rg, 'lru_b_rg': cols(s_b_rg, Eq),
        'lru_w_ig': s_dw_ig, 'lru_b_ig': cols(s_b_ig, Eq), 'lru_lam': cols(s_lam, Eq), 'final_g': s_final_g,
    }
    grads = {k: g.reshape(p[k].shape) for k, g in grads.items()}

    delta, new_m, new_v = {}, {}, {}

    def adamw_2d(k, rows, after=None):
        w2 = p[k].reshape(rows, -1)
        outs = adamw(w2, grads[k].reshape(w2.shape), p['m_' + k].reshape(w2.shape), p['v_' + k].reshape(w2.shape),
                     f"adamw_{k}", after=after)
        delta[k], new_m[k], new_v[k] = [o.reshape(p[k].shape) for o in outs]

    adamw_2d('w_ada', depth * D, after=tok_s5)
    for k in _MID:
        adamw_2d(k, p[k].shape[1])
    rest = [k for k in _W_NAMES if k not in _BIG + _MID]
    shapes = [p[k].shape for k in rest]
    packed = [_pack([src[pre_ + k] if pre_ else src[k] for k in rest], 1024)
              for src, pre_ in ((p, ''), (grads, ''), (p, 'm_'), (p, 'v_'))]
    outs = adamw(*packed, "adamw_small")
    for dst, o in zip((delta, new_m, new_v), outs):
        for k, val in zip(rest, _unpack(o.reshape(-1), shapes)):
            dst[k] = val

    done = delta['w_ada']
    tags = ("lru", "s5", "ffn1", "ffn0")
    states = {"ffn1": rs_ffn1, "lru": rs_lru, "ffn0": rs_ffn0, "s5": rs_s5}
    halves = [reduce_scatter_finish(states[t], place2, done, t) for t in tags]
    jsems, jbufs, _ = split_start("rs_join_start", halves, [len(h) for h in halves], _join_copies)

    def joined(gi, after):
        return split_wait(f"rs_join_wait_{tags[gi]}", jbufs[gi], jsems[gi], _join_copies, after)

    def adamw_big(ks):
        for k in ks:
            adamw_2d(k, math.prod(p[k].shape[:-1]))

    g_lru_in, g_lru_out = joined(0, done)
    grads.update({'lru_w_in': g_lru_in[None], 'lru_w_out': g_lru_out[None]})
    adamw_big(('lru_w_in', 'lru_w_out'))
    g_s5_in, g_s5_glu = joined(1, delta['lru_w_out'])
    grads.update({'s5_w_in': g_s5_in[None], 's5_w_glu': g_s5_glu[None]})
    adamw_big(('s5_w_in', 's5_w_glu'))
    g_gu1, g_down1 = joined(2, delta['s5_w_glu'])
    g_gu0, g_down0 = joined(3, delta['s5_w_glu'])
    grads.update({'ffn_w_gu': jnp.stack([g_gu0, g_gu1]), 'ffn_w_down': jnp.stack([g_down0, g_down1])})
    adamw_big(('ffn_w_gu', 'ffn_w_down'))

    return (loss, grad_x[None], *[grads[k] for k in _W_NAMES], *[delta[k] for k in _W_NAMES],
            *[new_m[k] for k in _W_NAMES], *[new_v[k] for k in _W_NAMES])


_IN_NAMES = (['x', 'c'] + _W_NAMES + ['loss_target'] + ['m_' + k for k in _W_NAMES] + ['v_' + k for k in _W_NAMES])


def kernel(x, c, norm_g, w_ada, b_ada, s5_w_in, s5_lam_re, s5_lam_im, s5_log_dt, s5_b_re, s5_b_im, s5_c_re, s5_c_im, s5_d, s5_w_glu, lru_w_in, lru_conv_w, lru_conv_b, lru_w_rg, lru_b_rg, lru_w_ig, lru_b_ig, lru_lam, lru_w_out, ffn_w_gu, ffn_w_down, final_g, loss_target, m_norm_g, m_w_ada, m_b_ada, m_s5_w_in, m_s5_lam_re, m_s5_lam_im, m_s5_log_dt, m_s5_b_re, m_s5_b_im, m_s5_c_re, m_s5_c_im, m_s5_d, m_s5_w_glu, m_lru_w_in, m_lru_conv_w, m_lru_conv_b, m_lru_w_rg, m_lru_b_rg, m_lru_w_ig, m_lru_b_ig, m_lru_lam, m_lru_w_out, m_ffn_w_gu, m_ffn_w_down, m_final_g, v_norm_g, v_w_ada, v_b_ada, v_s5_w_in, v_s5_lam_re, v_s5_lam_im, v_s5_log_dt, v_s5_b_re, v_s5_b_im, v_s5_c_re, v_s5_c_im, v_s5_d, v_s5_w_glu, v_lru_w_in, v_lru_conv_w, v_lru_conv_b, v_lru_w_rg, v_lru_b_rg, v_lru_w_ig, v_lru_b_ig, v_lru_lam, v_lru_w_out, v_ffn_w_gu, v_ffn_w_down, v_final_g):
    args = (x, c, norm_g, w_ada, b_ada, s5_w_in, s5_lam_re, s5_lam_im, s5_log_dt, s5_b_re, s5_b_im, s5_c_re, s5_c_im, s5_d, s5_w_glu, lru_w_in, lru_conv_w, lru_conv_b, lru_w_rg, lru_b_rg, lru_w_ig, lru_b_ig, lru_lam, lru_w_out, ffn_w_gu, ffn_w_down, final_g, loss_target, m_norm_g, m_w_ada, m_b_ada, m_s5_w_in, m_s5_lam_re, m_s5_lam_im, m_s5_log_dt, m_s5_b_re, m_s5_b_im, m_s5_c_re, m_s5_c_im, m_s5_d, m_s5_w_glu, m_lru_w_in, m_lru_conv_w, m_lru_conv_b, m_lru_w_rg, m_lru_b_rg, m_lru_w_ig, m_lru_b_ig, m_lru_lam, m_lru_w_out, m_ffn_w_gu, m_ffn_w_down, m_final_g, v_norm_g, v_w_ada, v_b_ada, v_s5_w_in, v_s5_lam_re, v_s5_lam_im, v_s5_log_dt, v_s5_b_re, v_s5_b_im, v_s5_c_re, v_s5_c_im, v_s5_d, v_s5_w_glu, v_lru_w_in, v_lru_conv_w, v_lru_conv_b, v_lru_w_rg, v_lru_b_rg, v_lru_w_ig, v_lru_b_ig, v_lru_lam, v_lru_w_out, v_ffn_w_gu, v_ffn_w_down, v_final_g)
    return _step(dict(zip(_IN_NAMES, args)))
```
